```python
import jax, jax.numpy as jnp
from jax import lax
import numpy as np

D_MODEL = 1024
BATCH = 8
SEQ = 4096
DEPTH = 4

N_MIXERS = 2
N_MLA = (DEPTH + 1) // 2
N_HGRN = DEPTH // 2

MLA_HEADS = 8
MLA_Q_LORA = 512
MLA_KV_LORA = 256
MLA_NOPE = 128
MLA_ROPE = 64
MLA_V = 128
ROPE_BASE = 10000.0
ATTN_BLOCK = 128

HGRN_HEADS = 8
HGRN_DK = D_MODEL // HGRN_HEADS
HGRN_DV = D_MODEL // HGRN_HEADS
HGRN_CHUNK = 32

D_FF = 4 * D_MODEL

EPS = 1e-6

kernel_name = "hybrid_mla_hgrn2_sqrelu_sandwich"


def rms_norm(x, g):
    xf = x.astype(jnp.float32)
    y = xf * lax.rsqrt(jnp.mean(xf * xf, axis=-1, keepdims=True) + EPS)
    return (y * g.astype(jnp.float32)).astype(x.dtype)


def rope_cos_sin(positions):
    inv_freq = jnp.power(ROPE_BASE, -jnp.arange(0, MLA_ROPE, 2, dtype=jnp.float32) / MLA_ROPE)
    ang = positions.astype(jnp.float32)[..., None] * inv_freq
    return jnp.cos(ang), jnp.sin(ang)


def apply_rope(t, cos, sin):
    tf = t.astype(jnp.float32)
    t1, t2 = jnp.split(tf, 2, axis=-1)
    return jnp.concatenate([t1 * cos - t2 * sin, t1 * sin + t2 * cos], axis=-1).astype(t.dtype)


def mla_mixer(h, cos, sin, w_in, q_norm, kv_norm, w_uq, w_ukv, w_o):
    B, S, _ = h.shape
    H = MLA_HEADS
    proj = h @ w_in
    c_q, c_kv, k_r = jnp.split(proj, [MLA_Q_LORA, MLA_Q_LORA + MLA_KV_LORA], axis=-1)
    q = (rms_norm(c_q, q_norm) @ w_uq).reshape(B, S, H, MLA_NOPE + MLA_ROPE)
    q_nope = q[..., :MLA_NOPE]
    q_rope = apply_rope(q[..., MLA_NOPE:], cos[:, :, None, :], sin[:, :, None, :])
    kv = (rms_norm(c_kv, kv_norm) @ w_ukv).reshape(B, S, H, MLA_NOPE + MLA_V)
    k_nope, v = kv[..., :MLA_NOPE], kv[..., MLA_NOPE:]
    k_rope = apply_rope(k_r, cos, sin)
    scale = (MLA_NOPE + MLA_ROPE) ** -0.5
    nb = S // ATTN_BLOCK
    qn_b = q_nope.reshape(B, nb, ATTN_BLOCK, H, MLA_NOPE).transpose(1, 0, 2, 3, 4)
    qr_b = q_rope.reshape(B, nb, ATTN_BLOCK, H, MLA_ROPE).transpose(1, 0, 2, 3, 4)
    k_pos = jnp.arange(S)

    def block(args):
        qn, qr, blk = args
        s = (jnp.einsum('bqhd,bkhd->bhqk', qn, k_nope)
             + jnp.einsum('bqhr,bkr->bhqk', qr, k_rope)).astype(jnp.float32) * scale
        q_pos = blk * ATTN_BLOCK + jnp.arange(ATTN_BLOCK)
        s = jnp.where(q_pos[:, None] >= k_pos[None, :], s, -jnp.inf)
        p = jax.nn.softmax(s, axis=-1).astype(v.dtype)
        return jnp.einsum('bhqk,bkhd->bqhd', p, v)

    o = lax.map(block, (qn_b, qr_b, jnp.arange(nb)))
    o = o.transpose(1, 0, 2, 3, 4).reshape(B, S, H * MLA_V)
    return o @ w_o


def hgrn2_mixer(h, lb, w_in, o_norm, w_o):
    B, S, _ = h.shape
    H, DK, DV, C = HGRN_HEADS, HGRN_DK, HGRN_DV, HGRN_CHUNK
    nc = S // C
    HK, HV = H * DK, H * DV
    proj = h @ w_in
    q_x, f_x, i_x, g_x = jnp.split(proj, [HK, 2 * HK, 2 * HK + HV], axis=-1)

    def heads(t, d):
        return t.astype(jnp.float32).reshape(B, nc, C, H, d).transpose(0, 3, 1, 2, 4)

    f = lb + (1.0 - lb) * jax.nn.sigmoid(f_x.astype(jnp.float32))
    q = heads(jax.nn.silu(q_x.astype(jnp.float32)), DK)
    k = heads(1.0 - f, DK)
    log_f = heads(jnp.log(f), DK)
    v = heads(i_x, DV)

    b = jnp.cumsum(log_f, axis=3)
    b_ref = b[:, :, :, C // 2:C // 2 + 1, :]
    b_last = b[:, :, :, -1:, :]
    q_rel = q * jnp.exp(b - b_ref)
    k_rel = k * jnp.exp(b_ref - b)
    causal = jnp.tril(jnp.ones((C, C), dtype=bool))
    a = jnp.where(causal, jnp.einsum('bhncd,bhnsd->bhncs', q_rel, k_rel), 0.0)
    o_intra = jnp.einsum('bhncs,bhnse->bhnce', a, v)

    q_dec = q * jnp.exp(b)
    k_dec = k * jnp.exp(b_last - b)
    chunk_decay = jnp.exp(b_last[:, :, :, 0, :])

    def step(state, xs):
        qd, kd, vc, dec = xs
        o_inter = jnp.einsum('bhcd,bhde->bhce', qd, state)
        state = dec[..., None] * state + jnp.einsum('bhcd,bhce->bhde', kd, vc)
        return state, o_inter

    mv = lambda t: jnp.moveaxis(t, 2, 0)
    s0 = jnp.zeros((B, H, DK, DV), jnp.float32)
    _, o_inter = lax.scan(step, s0, (mv(q_dec), mv(k_dec), mv(v), mv(chunk_decay)))
    o = o_intra + jnp.moveaxis(o_inter, 0, 2)
    o = o.transpose(0, 2, 3, 1, 4).reshape(B, S, H, DV)
    gate = jax.nn.silu(g_x.astype(jnp.float32)).reshape(B, S, H, DV)
    o = rms_norm(o, o_norm) * gate
    return o.reshape(B, S, HV).astype(h.dtype) @ w_o


def sq_relu_mlp(h, w1, w2):
    a = jax.nn.relu(h @ w1)
    return (a * a) @ w2


def _fwd_setup_inputs(seed: int = 0) -> dict:
    key = jax.random.key(seed)
    ks = jax.random.split(key, 16)
    f32 = jnp.float32
    nrm = lambda k, shape, fan_in: jax.random.normal(k, shape, f32) * (fan_in ** -0.5)
    mla_in_w = MLA_Q_LORA + MLA_KV_LORA + MLA_ROPE
    hgrn_in_w = 3 * HGRN_HEADS * HGRN_DK + HGRN_HEADS * HGRN_DV
    return {
        "x": jax.random.normal(ks[0], (BATCH, SEQ, D_MODEL), f32),
        "positions": jnp.broadcast_to(jnp.arange(SEQ, dtype=jnp.int32), (BATCH, SEQ)),
        "norm_gains": 1.0 + 0.1 * jax.random.normal(ks[1], (DEPTH, 4, D_MODEL), f32),
        "mla_w_in": nrm(ks[2], (N_MLA, D_MODEL, mla_in_w), D_MODEL),
        "mla_q_norm": 1.0 + 0.1 * jax.random.normal(ks[3], (N_MLA, MLA_Q_LORA), f32),
        "mla_kv_norm": 1.0 + 0.1 * jax.random.normal(ks[4], (N_MLA, MLA_KV_LORA), f32),
        "mla_w_uq": nrm(ks[5], (N_MLA, MLA_Q_LORA, MLA_HEADS * (MLA_NOPE + MLA_ROPE)), MLA_Q_LORA),
        "mla_w_ukv": nrm(ks[6], (N_MLA, MLA_KV_LORA, MLA_HEADS * (MLA_NOPE + MLA_V)), MLA_KV_LORA),
        "mla_w_o": nrm(ks[7], (N_MLA, MLA_HEADS * MLA_V, D_MODEL), MLA_HEADS * MLA_V),
        "hgrn_w_in": nrm(ks[8], (N_HGRN, D_MODEL, hgrn_in_w), D_MODEL),
        "hgrn_lb_logits": 0.1 * jax.random.normal(ks[9], (DEPTH, HGRN_HEADS * HGRN_DK), f32),
        "hgrn_o_norm": 1.0 + 0.1 * jax.random.normal(ks[10], (N_HGRN, HGRN_DV), f32),
        "hgrn_w_o": nrm(ks[11], (N_HGRN, HGRN_HEADS * HGRN_DV, D_MODEL), HGRN_HEADS * HGRN_DV),
        "mlp_w1": nrm(ks[12], (DEPTH, D_MODEL, D_FF), D_MODEL),
        "mlp_w2": nrm(ks[13], (DEPTH, D_FF, D_MODEL), D_FF),
    }


def _fwd_reference(x, positions, norm_gains, mla_w_in, mla_q_norm, mla_kv_norm, mla_w_uq, mla_w_ukv,
              mla_w_o, hgrn_w_in, hgrn_lb_logits, hgrn_o_norm, hgrn_w_o, mlp_w1, mlp_w2):
    cos, sin = rope_cos_sin(positions)
    p = jax.nn.softmax(hgrn_lb_logits.astype(jnp.float32), axis=0)
    lower_bounds = jnp.cumsum(p, axis=0) - p[0]
    h = x
    for layer in range(DEPTH):
        slot = layer // N_MIXERS
        a = rms_norm(h, norm_gains[layer, 0])
        if layer % N_MIXERS == 0:
            m = mla_mixer(a, cos, sin, mla_w_in[slot], mla_q_norm[slot], mla_kv_norm[slot],
                          mla_w_uq[slot], mla_w_ukv[slot], mla_w_o[slot])
        else:
            m = hgrn2_mixer(a, lower_bounds[layer], hgrn_w_in[slot], hgrn_o_norm[slot], hgrn_w_o[slot])
        h = h + rms_norm(m, norm_gains[layer, 1])
        a = rms_norm(h, norm_gains[layer, 2])
        h = h + rms_norm(sq_relu_mlp(a, mlp_w1[layer], mlp_w2[layer]), norm_gains[layer, 3])
    return h


import jax as _jax
import jax.numpy as _jnp

TWIN_FORMAT = 'train_step'
FWD_PARAMS = ['x', 'positions', 'norm_gains', 'mla_w_in', 'mla_q_norm', 'mla_kv_norm', 'mla_w_uq', 'mla_w_ukv', 'mla_w_o', 'hgrn_w_in', 'hgrn_lb_logits', 'hgrn_o_norm', 'hgrn_w_o', 'mlp_w1', 'mlp_w2']
TWIN_WEIGHTS = ['norm_gains', 'mla_w_in', 'mla_q_norm', 'mla_kv_norm', 'mla_w_uq', 'mla_w_ukv', 'mla_w_o', 'hgrn_w_in', 'hgrn_lb_logits', 'hgrn_o_norm', 'hgrn_w_o', 'mlp_w1', 'mlp_w2']
TWIN_DIFF_INPUT = 'x'
TWIN_INPUTS = ['x', 'positions', 'norm_gains', 'mla_w_in', 'mla_q_norm', 'mla_kv_norm', 'mla_w_uq', 'mla_w_ukv', 'mla_w_o', 'hgrn_w_in', 'hgrn_lb_logits', 'hgrn_o_norm', 'hgrn_w_o', 'mlp_w1', 'mlp_w2', 'loss_target', 'm_norm_gains', 'm_mla_w_in', 'm_mla_q_norm', 'm_mla_kv_norm', 'm_mla_w_uq', 'm_mla_w_ukv', 'm_mla_w_o', 'm_hgrn_w_in', 'm_hgrn_lb_logits', 'm_hgrn_o_norm', 'm_hgrn_w_o', 'm_mlp_w1', 'm_mlp_w2', 'v_norm_gains', 'v_mla_w_in', 'v_mla_q_norm', 'v_mla_kv_norm', 'v_mla_w_uq', 'v_mla_w_ukv', 'v_mla_w_o', 'v_hgrn_w_in', 'v_hgrn_lb_logits', 'v_hgrn_o_norm', 'v_hgrn_w_o', 'v_mlp_w1', 'v_mlp_w2']
TWIN_OUTPUTS = ['loss', 'grad_x', 'grad_norm_gains', 'grad_mla_w_in', 'grad_mla_q_norm', 'grad_mla_kv_norm', 'grad_mla_w_uq', 'grad_mla_w_ukv', 'grad_mla_w_o', 'grad_hgrn_w_in', 'grad_hgrn_lb_logits', 'grad_hgrn_o_norm', 'grad_hgrn_w_o', 'grad_mlp_w1', 'grad_mlp_w2', 'delta_norm_gains', 'delta_mla_w_in', 'delta_mla_q_norm', 'delta_mla_kv_norm', 'delta_mla_w_uq', 'delta_mla_w_ukv', 'delta_mla_w_o', 'delta_hgrn_w_in', 'delta_hgrn_lb_logits', 'delta_hgrn_o_norm', 'delta_hgrn_w_o', 'delta_mlp_w1', 'delta_mlp_w2', 'new_m_norm_gains', 'new_m_mla_w_in', 'new_m_mla_q_norm', 'new_m_mla_kv_norm', 'new_m_mla_w_uq', 'new_m_mla_w_ukv', 'new_m_mla_w_o', 'new_m_hgrn_w_in', 'new_m_hgrn_lb_logits', 'new_m_hgrn_o_norm', 'new_m_hgrn_w_o', 'new_m_mlp_w1', 'new_m_mlp_w2', 'new_v_norm_gains', 'new_v_mla_w_in', 'new_v_mla_q_norm', 'new_v_mla_kv_norm', 'new_v_mla_w_uq', 'new_v_mla_w_ukv', 'new_v_mla_w_o', 'new_v_hgrn_w_in', 'new_v_hgrn_lb_logits', 'new_v_hgrn_o_norm', 'new_v_hgrn_w_o', 'new_v_mlp_w1', 'new_v_mlp_w2']
TWIN_LEAF_KINDS = {'loss': 'loss', 'grad_x': 'grad_x', 'grad_norm_gains': 'grad_w', 'grad_mla_w_in': 'grad_w', 'grad_mla_q_norm': 'grad_w', 'grad_mla_kv_norm': 'grad_w', 'grad_mla_w_uq': 'grad_w', 'grad_mla_w_ukv': 'grad_w', 'grad_mla_w_o': 'grad_w', 'grad_hgrn_w_in': 'grad_w', 'grad_hgrn_lb_logits': 'grad_w', 'grad_hgrn_o_norm': 'grad_w', 'grad_hgrn_w_o': 'grad_w', 'grad_mlp_w1': 'grad_w', 'grad_mlp_w2': 'grad_w', 'delta_norm_gains': 'delta_w', 'delta_mla_w_in': 'delta_w', 'delta_mla_q_norm': 'delta_w', 'delta_mla_kv_norm': 'delta_w', 'delta_mla_w_uq': 'delta_w', 'delta_mla_w_ukv': 'delta_w', 'delta_mla_w_o': 'delta_w', 'delta_hgrn_w_in': 'delta_w', 'delta_hgrn_lb_logits': 'delta_w', 'delta_hgrn_o_norm': 'delta_w', 'delta_hgrn_w_o': 'delta_w', 'delta_mlp_w1': 'delta_w', 'delta_mlp_w2': 'delta_w', 'new_m_norm_gains': 'new_m', 'new_m_mla_w_in': 'new_m', 'new_m_mla_q_norm': 'new_m', 'new_m_mla_kv_norm': 'new_m', 'new_m_mla_w_uq': 'new_m', 'new_m_mla_w_ukv': 'new_m', 'new_m_mla_w_o': 'new_m', 'new_m_hgrn_w_in': 'new_m', 'new_m_hgrn_lb_logits': 'new_m', 'new_m_hgrn_o_norm': 'new_m', 'new_m_hgrn_w_o': 'new_m', 'new_m_mlp_w1': 'new_m', 'new_m_mlp_w2': 'new_m', 'new_v_norm_gains': 'new_v', 'new_v_mla_w_in': 'new_v', 'new_v_mla_q_norm': 'new_v', 'new_v_mla_kv_norm': 'new_v', 'new_v_mla_w_uq': 'new_v', 'new_v_mla_w_ukv': 'new_v', 'new_v_mla_w_o': 'new_v', 'new_v_hgrn_w_in': 'new_v', 'new_v_hgrn_lb_logits': 'new_v', 'new_v_hgrn_o_norm': 'new_v', 'new_v_hgrn_w_o': 'new_v', 'new_v_mlp_w1': 'new_v', 'new_v_mlp_w2': 'new_v'}


def _forward(args):
    return _fwd_reference(*[args[k] for k in FWD_PARAMS])


def _output_shape():
    def fwd():
        inp = _fwd_setup_inputs(0)
        return _fwd_reference(*[inp[k] for k in FWD_PARAMS])
    out = _jax.eval_shape(fwd)
    return out.shape, out.dtype

N_MICROBATCH = 1
ADAM_LR = 0.001
ADAM_B1 = 0.9
ADAM_B2 = 0.999
ADAM_EPS = 1e-08
ADAM_WD = 0.01
ADAM_STEP = 10
PER_EXAMPLE_BATCH_AXIS = {'x': 0, 'positions': 0, 'loss_target': 0}
SHARED_INPUTS = []
_WEIGHT_DTYPES = {'norm_gains': _jnp.float32, 'mla_w_in': _jnp.float32, 'mla_q_norm': _jnp.float32, 'mla_kv_norm': _jnp.float32, 'mla_w_uq': _jnp.float32, 'mla_w_ukv': _jnp.float32, 'mla_w_o': _jnp.float32, 'hgrn_w_in': _jnp.float32, 'hgrn_lb_logits': _jnp.float32, 'hgrn_o_norm': _jnp.float32, 'hgrn_w_o': _jnp.float32, 'mlp_w1': _jnp.float32, 'mlp_w2': _jnp.float32}
MOMENT_SCALE = {'norm_gains': 3.066115e+01, 'mla_w_in': 2.754907e+01, 'mla_q_norm': 2.857228e+00, 'mla_kv_norm': 4.600268e+01, 'mla_w_uq': 1.532696e+00, 'mla_w_ukv': 1.701096e+01, 'mla_w_o': 2.377219e+01, 'hgrn_w_in': 8.571503e+00, 'hgrn_lb_logits': 8.620588e-02, 'hgrn_o_norm': 3.918819e+01, 'hgrn_w_o': 1.384514e+01, 'mlp_w1': 5.648174e+00, 'mlp_w2': 2.329309e+01}


def _to_microbatches(a, axis):
    t = _jnp.moveaxis(a, axis, 0)
    t = t.reshape((N_MICROBATCH, t.shape[0] // N_MICROBATCH) + t.shape[1:])
    return _jnp.moveaxis(t, 1, axis + 1)


def setup_inputs(seed: int = 0) -> dict:
    inp = _fwd_setup_inputs(seed)
    key = _jax.random.fold_in(_jax.random.key(seed), 7919)
    shape, _ = _output_shape()
    out = dict(inp)
    out["loss_target"] = _jax.random.normal(_jax.random.fold_in(key, 0), shape, _jnp.float32)
    for i, name in enumerate(TWIN_WEIGHTS):
        w = inp[name].astype(_jnp.float32)
        if MOMENT_SCALE is None:
            s = _jnp.sqrt(_jnp.mean(_jnp.square(w)) + 1e-30)
        else:
            s = MOMENT_SCALE[name]
        km, kv = _jax.random.split(_jax.random.fold_in(key, i + 1))
        out[name] = w
        out["m_" + name] = s * _jax.random.normal(km, w.shape, _jnp.float32)
        out["v_" + name] = (s * s) * _jax.random.uniform(kv, w.shape, _jnp.float32, 0.5, 1.5)
    if N_MICROBATCH > 1:
        for name, axis in PER_EXAMPLE_BATCH_AXIS.items():
            out[name] = _to_microbatches(out[name], axis)
    return {'x': out['x'], 'positions': out['positions'], 'norm_gains': out['norm_gains'], 'mla_w_in': out['mla_w_in'], 'mla_q_norm': out['mla_q_norm'], 'mla_kv_norm': out['mla_kv_norm'], 'mla_w_uq': out['mla_w_uq'], 'mla_w_ukv': out['mla_w_ukv'], 'mla_w_o': out['mla_w_o'], 'hgrn_w_in': out['hgrn_w_in'], 'hgrn_lb_logits': out['hgrn_lb_logits'], 'hgrn_o_norm': out['hgrn_o_norm'], 'hgrn_w_o': out['hgrn_w_o'], 'mlp_w1': out['mlp_w1'], 'mlp_w2': out['mlp_w2'], 'loss_target': out['loss_target'], 'm_norm_gains': out['m_norm_gains'], 'm_mla_w_in': out['m_mla_w_in'], 'm_mla_q_norm': out['m_mla_q_norm'], 'm_mla_kv_norm': out['m_mla_kv_norm'], 'm_mla_w_uq': out['m_mla_w_uq'], 'm_mla_w_ukv': out['m_mla_w_ukv'], 'm_mla_w_o': out['m_mla_w_o'], 'm_hgrn_w_in': out['m_hgrn_w_in'], 'm_hgrn_lb_logits': out['m_hgrn_lb_logits'], 'm_hgrn_o_norm': out['m_hgrn_o_norm'], 'm_hgrn_w_o': out['m_hgrn_w_o'], 'm_mlp_w1': out['m_mlp_w1'], 'm_mlp_w2': out['m_mlp_w2'], 'v_norm_gains': out['v_norm_gains'], 'v_mla_w_in': out['v_mla_w_in'], 'v_mla_q_norm': out['v_mla_q_norm'], 'v_mla_kv_norm': out['v_mla_kv_norm'], 'v_mla_w_uq': out['v_mla_w_uq'], 'v_mla_w_ukv': out['v_mla_w_ukv'], 'v_mla_w_o': out['v_mla_w_o'], 'v_hgrn_w_in': out['v_hgrn_w_in'], 'v_hgrn_lb_logits': out['v_hgrn_lb_logits'], 'v_hgrn_o_norm': out['v_hgrn_o_norm'], 'v_hgrn_w_o': out['v_hgrn_w_o'], 'v_mlp_w1': out['v_mlp_w1'], 'v_mlp_w2': out['v_mlp_w2']}


def _loss(weights, diff, rest, loss_target):
    with _jax.named_scope("forward"):
        args = {**rest, TWIN_DIFF_INPUT: diff, **{k: w.astype(_WEIGHT_DTYPES[k]) for k, w in weights.items()}}
        y = _forward(args)
    with _jax.named_scope("loss_head"):
        err = _jnp.square(y.astype(_jnp.float32) - loss_target)
        return 0.5 * _jnp.sum(_jnp.mean(err, axis=-1)) if err.ndim else 0.5 * err


def _adamw(w, g, m, v):
    m = ADAM_B1 * m + (1.0 - ADAM_B1) * g
    v = ADAM_B2 * v + (1.0 - ADAM_B2) * _jnp.square(g)
    m_hat = m / (1.0 - ADAM_B1 ** ADAM_STEP)
    v_hat = v / (1.0 - ADAM_B2 ** ADAM_STEP)
    delta = -ADAM_LR * (m_hat / (_jnp.sqrt(v_hat) + ADAM_EPS) + ADAM_WD * w)
    return delta, m, v


def reference(x, positions, norm_gains, mla_w_in, mla_q_norm, mla_kv_norm, mla_w_uq, mla_w_ukv, mla_w_o, hgrn_w_in, hgrn_lb_logits, hgrn_o_norm, hgrn_w_o, mlp_w1, mlp_w2, loss_target, m_norm_gains, m_mla_w_in, m_mla_q_norm, m_mla_kv_norm, m_mla_w_uq, m_mla_w_ukv, m_mla_w_o, m_hgrn_w_in, m_hgrn_lb_logits, m_hgrn_o_norm, m_hgrn_w_o, m_mlp_w1, m_mlp_w2, v_norm_gains, v_mla_w_in, v_mla_q_norm, v_mla_kv_norm, v_mla_w_uq, v_mla_w_ukv, v_mla_w_o, v_hgrn_w_in, v_hgrn_lb_logits, v_hgrn_o_norm, v_hgrn_w_o, v_mlp_w1, v_mlp_w2):
    given = dict(x=x, positions=positions, norm_gains=norm_gains, mla_w_in=mla_w_in, mla_q_norm=mla_q_norm, mla_kv_norm=mla_kv_norm, mla_w_uq=mla_w_uq, mla_w_ukv=mla_w_ukv, mla_w_o=mla_w_o, hgrn_w_in=hgrn_w_in, hgrn_lb_logits=hgrn_lb_logits, hgrn_o_norm=hgrn_o_norm, hgrn_w_o=hgrn_w_o, mlp_w1=mlp_w1, mlp_w2=mlp_w2, loss_target=loss_target, m_norm_gains=m_norm_gains, m_mla_w_in=m_mla_w_in, m_mla_q_norm=m_mla_q_norm, m_mla_kv_norm=m_mla_kv_norm, m_mla_w_uq=m_mla_w_uq, m_mla_w_ukv=m_mla_w_ukv, m_mla_w_o=m_mla_w_o, m_hgrn_w_in=m_hgrn_w_in, m_hgrn_lb_logits=m_hgrn_lb_logits, m_hgrn_o_norm=m_hgrn_o_norm, m_hgrn_w_o=m_hgrn_w_o, m_mlp_w1=m_mlp_w1, m_mlp_w2=m_mlp_w2, v_norm_gains=v_norm_gains, v_mla_w_in=v_mla_w_in, v_mla_q_norm=v_mla_q_norm, v_mla_kv_norm=v_mla_kv_norm, v_mla_w_uq=v_mla_w_uq, v_mla_w_ukv=v_mla_w_ukv, v_mla_w_o=v_mla_w_o, v_hgrn_w_in=v_hgrn_w_in, v_hgrn_lb_logits=v_hgrn_lb_logits, v_hgrn_o_norm=v_hgrn_o_norm, v_hgrn_w_o=v_hgrn_w_o, v_mlp_w1=v_mlp_w1, v_mlp_w2=v_mlp_w2)
    weights = {n: given[n] for n in TWIN_WEIGHTS}
    shared = {n: given[n] for n in SHARED_INPUTS}
    per_example = {n: given[n] for n in ['x', 'positions']}
    grad_fn = _jax.value_and_grad(_loss, argnums=(0, 1))

    def one_microbatch(ex, loss_target):
        ex = dict(ex)
        diff = ex.pop(TWIN_DIFF_INPUT)
        return grad_fn(weights, diff, {**shared, **ex}, loss_target)

    if N_MICROBATCH == 1:
        loss, (grad_w, grad_x) = one_microbatch(per_example, given["loss_target"])
    else:
        def body(carry, xs):
            loss_sum, grad_sum = carry
            l_k, (gw_k, gx_k) = one_microbatch(xs[0], xs[1])
            with _jax.named_scope("update"):
                return (loss_sum + l_k, _jax.tree.map(_jnp.add, grad_sum, gw_k)), gx_k

        init = (_jnp.zeros((), _jnp.float32), _jax.tree.map(_jnp.zeros_like, weights))
        (loss, grad_w), grad_x = _jax.lax.scan(body, init, (per_example, given["loss_target"]))
    with _jax.named_scope("update"):
        delta_w, new_m, new_v = {}, {}, {}
        for n in TWIN_WEIGHTS:
            delta_w[n], new_m[n], new_v[n] = _adamw(weights[n], grad_w[n], given["m_" + n], given["v_" + n])
    return (loss, grad_x, *[grad_w[n] for n in TWIN_WEIGHTS], *[delta_w[n] for n in TWIN_WEIGHTS],
            *[new_m[n] for n in TWIN_WEIGHTS], *[new_v[n] for n in TWIN_WEIGHTS])
```

```python
import numpy as np
import jax
import jax.numpy as jnp
from jax import lax
from jax.experimental import pallas as pl
from jax.experimental.pallas import tpu as pltpu

F32, BF16 = jnp.float32, jnp.bfloat16

N_DEV = 8
AXES = ("x", "y", "c")
D_MODEL = 1024
DEPTH = 4
MLA_HEADS = 8
MLA_Q_LORA = 512
MLA_KV_LORA = 256
MLA_NOPE = 128
MLA_ROPE = 64
MLA_V = 128
MLA_QK = MLA_NOPE + MLA_ROPE
MLA_IN = MLA_Q_LORA + MLA_KV_LORA + MLA_ROPE
ROPE_BASE = 10000.0
HGRN_HEADS = 8
HGRN_D = 128
HGRN_CHUNK = 32
D_FF = 4 * D_MODEL
EPS = 1e-6
ADAM_LR, ADAM_B1, ADAM_B2, ADAM_EPS, ADAM_WD, ADAM_STEP = 0.001, 0.9, 0.999, 1e-08, 0.01, 10

V7X_VMEM_LIMIT_BYTES = 56 * 1024 * 1024

NN = (((1,), (0,)), ((), ()))
NT = (((1,), (1,)), ((), ()))
TN = (((0,), (0,)), ((), ()))
_DIMS = {"nn": NN, "nt": NT, "tn": TN}


def _params(*sem):
    return pltpu.CompilerParams(dimension_semantics=sem, vmem_limit_bytes=V7X_VMEM_LIMIT_BYTES)


def _dot(a, b, dims=NN):
    return lax.dot_general(a, b, dims, preferred_element_type=F32)


def _dot_f32(a, b):
    return lax.dot_general(a, b, NN, precision=lax.Precision.HIGHEST, preferred_element_type=F32)


def _rstd(x):
    return lax.rsqrt(jnp.mean(x * x, axis=-1, keepdims=True) + EPS)


def _rms_bwd_rows(x, g, dy):
    r = _rstd(x)
    xh = x * r
    dyg = dy * g
    dx = r * (dyg - xh * jnp.mean(dyg * xh, axis=-1, keepdims=True))
    return dx, dy * xh


def _row_tile(n, want):
    t = min(n, want)
    assert n % t == 0, (n, t)
    return t


def _rms_fwd(x, g, res, out_dtype, name):
    s, d = x.shape
    ts = _row_tile(s, 512)

    def body(x_ref, g_ref, *rest):
        xf = x_ref[...]
        y = xf * _rstd(xf) * g_ref[...]
        if res is not None:
            y = rest[0][...] + y
        rest[-1][...] = y.astype(out_dtype)

    row = pl.BlockSpec((ts, d), lambda i: (i, 0))
    vec = pl.BlockSpec((1, d), lambda i: (0, 0))
    ins = [x, g] + ([res] if res is not None else [])
    return pl.pallas_call(
        body, grid=(s // ts,), in_specs=[row, vec] + ([row] if res is not None else []), out_specs=row,
        out_shape=jax.ShapeDtypeStruct((s, d), out_dtype), compiler_params=_params("parallel"), name=name)(*ins)


def _rms_bwd(x, g, dy, res, out_dtype, name):
    s, d = x.shape
    ts = _row_tile(s, 512)

    def body(x_ref, g_ref, dy_ref, *rest):
        dx_ref, dg_ref = rest[-2:]
        dx, dg = _rms_bwd_rows(x_ref[...], g_ref[...], dy_ref[...].astype(F32))
        if res is not None:
            dx = rest[0][...] + dx
        dx_ref[...] = dx.astype(out_dtype)

        @pl.when(pl.program_id(0) == 0)
        def _():
            dg_ref[...] = jnp.zeros_like(dg_ref)

        dg_ref[...] += jnp.sum(dg, axis=0, keepdims=True)

    row = pl.BlockSpec((ts, d), lambda i: (i, 0))
    vec = pl.BlockSpec((1, d), lambda i: (0, 0))
    ins = [x, g, dy] + ([res] if res is not None else [])
    return pl.pallas_call(
        body, grid=(s // ts,), in_specs=[row, vec, row] + ([row] if res is not None else []), out_specs=(row, vec),
        out_shape=(jax.ShapeDtypeStruct((s, d), out_dtype), jax.ShapeDtypeStruct((1, d), F32)),
        compiler_params=_params("arbitrary"), name=name)(*ins)


def _mm(a, b, mode, tm, tn, name, out_dtypes=(F32,), shard=None, epi=None, extras=()):
    if mode == "tn":
        k, m = a.shape
        a_spec = pl.BlockSpec((k, tm), lambda i, j: (0, i))
    else:
        m, k = a.shape
        a_spec = pl.BlockSpec((tm, k), lambda i, j: (i, 0))
    if mode == "nt":
        n = b.shape[0]
        b_spec = pl.BlockSpec((tn, k), lambda i, j: (j, 0))
    else:
        n = b.shape[1]
        b_spec = pl.BlockSpec((k, tn), lambda i, j: (0, j))
    assert m % tm == 0 and n % tn == 0, (name, m, tm, n, tn)
    tile = pl.BlockSpec((tm, tn), lambda i, j: (i, j))
    if shard == "rows":
        per = m // N_DEV // tm
        out_specs = [pl.BlockSpec((None, tm, tn), lambda i, j: (i // per, i % per, j))]
        out_shape = [jax.ShapeDtypeStruct((N_DEV, m // N_DEV, n), out_dtypes[0])]
    elif shard == "cols":
        per = n // N_DEV // tn
        out_specs = [pl.BlockSpec((None, tm, tn), lambda i, j: (j // per, i, j % per))]
        out_shape = [jax.ShapeDtypeStruct((N_DEV, m, n // N_DEV), out_dtypes[0])]
    else:
        out_specs = [tile for _ in out_dtypes]
        out_shape = [jax.ShapeDtypeStruct((m, n), dt) for dt in out_dtypes]
    n_ex = len(extras)

    def body(a_ref, b_ref, *refs):
        acc = _dot(a_ref[...].astype(BF16), b_ref[...].astype(BF16), _DIMS[mode])
        vals = (acc,) if epi is None else epi(acc, *[r[...] for r in refs[:n_ex]])
        for o_ref, val in zip(refs[n_ex:], vals):
            o_ref[...] = val.astype(o_ref.dtype)

    out = pl.pallas_call(
        body, grid=(m // tm, n // tn), in_specs=[a_spec, b_spec] + [tile] * n_ex, out_specs=out_specs,
        out_shape=out_shape, compiler_params=_params("parallel", "parallel"), name=name)(a, b, *extras)
    return out[0] if len(out) == 1 else out


def _rope_tables(pos, name):
    s = pos.shape[0]
    half = MLA_ROPE // 2
    inv_freq = jnp.asarray(np.power(np.float32(ROPE_BASE), -np.arange(0, MLA_ROPE, 2, dtype=np.float32) / MLA_ROPE)
                           .astype(np.float32).reshape(1, half))

    def body(p_ref, f_ref, c_ref, s_ref):
        ang = p_ref[...].astype(F32) * f_ref[...]
        c_ref[...] = jnp.cos(ang)
        s_ref[...] = jnp.sin(ang)

    return pl.pallas_call(
        body, out_shape=(jax.ShapeDtypeStruct((s, half), F32), jax.ShapeDtypeStruct((s, half), F32)), name=name)(pos, inv_freq)


def _lb_softmax(logits):
    m = jnp.max(logits, axis=0, keepdims=True)
    e = jnp.exp(logits - m)
    return e / jnp.sum(e, axis=0, keepdims=True)


def _lb_fwd(logits, name):
    def body(l_ref, o_ref):
        p = _lb_softmax(l_ref[...])
        acc = jnp.zeros_like(p[0:1])
        o_ref[0:1, :] = acc
        for layer in range(1, DEPTH):
            acc = acc + p[layer:layer + 1]
            o_ref[layer:layer + 1, :] = acc

    return pl.pallas_call(body, out_shape=jax.ShapeDtypeStruct(logits.shape, F32), name=name)(logits)


def _lb_bwd(logits, dlb, name):
    def body(l_ref, d_ref, o_ref):
        p = _lb_softmax(l_ref[...])
        d = d_ref[...]
        dp = [jnp.zeros_like(d[0:1])] * DEPTH
        run = jnp.zeros_like(d[0:1])
        for layer in range(DEPTH - 1, 0, -1):
            run = run + d[layer:layer + 1]
            dp[layer] = run
        inner = sum(p[layer:layer + 1] * dp[layer] for layer in range(DEPTH))
        for layer in range(DEPTH):
            o_ref[layer:layer + 1, :] = p[layer:layer + 1] * (dp[layer] - inner)

    return pl.pallas_call(body, out_shape=jax.ShapeDtypeStruct(logits.shape, F32), name=name)(logits, dlb)


def _loss(y, target, name):
    s, d = y.shape
    ts = _row_tile(s, 512)

    def body(y_ref, t_ref, l_ref, dy_ref):
        e = y_ref[...] - t_ref[...]
        dy_ref[...] = e / d

        @pl.when(pl.program_id(0) == 0)
        def _():
            l_ref[...] = jnp.zeros_like(l_ref)

        l_ref[...] += 0.5 * jnp.sum(jnp.mean(e * e, axis=-1, keepdims=True), axis=0, keepdims=True)

    row = pl.BlockSpec((ts, d), lambda i: (i, 0))
    return pl.pallas_call(
        body, grid=(s // ts,), in_specs=[row, row], out_specs=(pl.BlockSpec((1, 1), lambda i: (0, 0)), row),
        out_shape=(jax.ShapeDtypeStruct((1, 1), F32), jax.ShapeDtypeStruct((s, d), F32)),
        compiler_params=_params("arbitrary"), name=name)(y, target)


def _rope(t1, t2, cos, sin):
    return t1 * cos - t2 * sin, t1 * sin + t2 * cos


def _rope_bwd(d1, d2, cos, sin):
    return d1 * cos + d2 * sin, d2 * cos - d1 * sin


def _mla_qkv(proj, qn, kvn, w_uq, w_ukv, cos, sin, name):
    s = proj.shape[0]
    ts = _row_tile(s, 256)
    hh, half = MLA_HEADS, MLA_ROPE // 2

    def body(p_ref, qn_ref, kvn_ref, wq_ref, wkv_ref, c_ref, s_ref, cq_ref, ckv_ref, q_ref, k_ref, v_ref):
        p = p_ref[...]
        cq, ckv, kr = p[:, :MLA_Q_LORA], p[:, MLA_Q_LORA:MLA_Q_LORA + MLA_KV_LORA], p[:, MLA_Q_LORA + MLA_KV_LORA:]
        cqn = (cq * _rstd(cq) * qn_ref[...]).astype(BF16)
        ckvn = (ckv * _rstd(ckv) * kvn_ref[...]).astype(BF16)
        cq_ref[...] = cqn
        ckv_ref[...] = ckvn
        qe = _dot(cqn, wq_ref[...])
        kve = _dot(ckvn, wkv_ref[...])
        cos_, sin_ = c_ref[...], s_ref[...]
        k1, k2 = _rope(kr[:, :half], kr[:, half:], cos_, sin_)
        k1, k2 = k1.astype(BF16), k2.astype(BF16)
        for h in range(hh):
            b = h * MLA_QK
            q_ref[h, :, 0:MLA_NOPE] = qe[:, b:b + MLA_NOPE].astype(BF16)
            q1, q2 = _rope(qe[:, b + MLA_NOPE:b + MLA_NOPE + half], qe[:, b + MLA_NOPE + half:b + MLA_QK], cos_, sin_)
            q_ref[h, :, MLA_NOPE:MLA_NOPE + half] = q1.astype(BF16)
            q_ref[h, :, MLA_NOPE + half:MLA_QK] = q2.astype(BF16)
            b = h * (MLA_NOPE + MLA_V)
            k_ref[h, :, 0:MLA_NOPE] = kve[:, b:b + MLA_NOPE].astype(BF16)
            k_ref[h, :, MLA_NOPE:MLA_NOPE + half] = k1
            k_ref[h, :, MLA_NOPE + half:MLA_QK] = k2
            v_ref[h] = kve[:, b + MLA_NOPE:b + MLA_NOPE + MLA_V].astype(BF16)

    def row(w):
        return pl.BlockSpec((ts, w), lambda i: (i, 0))

    def full(shape):
        return pl.BlockSpec(shape, lambda i: (0,) * len(shape))

    def heads(w):
        return pl.BlockSpec((hh, ts, w), lambda i: (0, i, 0))

    return pl.pallas_call(
        body, grid=(s // ts,),
        in_specs=[row(MLA_IN), full(qn.shape), full(kvn.shape), full(w_uq.shape), full(w_ukv.shape), row(half), row(half)],
        out_specs=(row(MLA_Q_LORA), row(MLA_KV_LORA), heads(MLA_QK), heads(MLA_QK), heads(MLA_V)),
        out_shape=(jax.ShapeDtypeStruct((s, MLA_Q_LORA), BF16), jax.ShapeDtypeStruct((s, MLA_KV_LORA), BF16),
                   jax.ShapeDtypeStruct((hh, s, MLA_QK), BF16), jax.ShapeDtypeStruct((hh, s, MLA_QK), BF16),
                   jax.ShapeDtypeStruct((hh, s, MLA_V), BF16)),
        compiler_params=_params("parallel"), name=name)(proj, qn, kvn, w_uq, w_ukv, cos, sin)


def _attn_block(s):
    return _row_tile(s, 512)


def _attn_fwd(q, k, v, name):
    hh, s, _ = q.shape
    blk = _attn_block(s)
    nb = s // blk
    scale = MLA_QK ** -0.5

    def body(q_ref, k_ref, v_ref, o_ref, lse_ref, m_scr, l_scr, acc_scr):
        i, j = pl.program_id(1), pl.program_id(2)

        @pl.when(j == 0)
        def _():
            m_scr[...] = jnp.full_like(m_scr, -jnp.inf)
            l_scr[...] = jnp.zeros_like(l_scr)
            acc_scr[...] = jnp.zeros_like(acc_scr)

        @pl.when(j <= i)
        def _():
            sc = _dot(q_ref[...], k_ref[...], NT) * scale
            qpos = i * blk + lax.broadcasted_iota(jnp.int32, (blk, blk), 0)
            kpos = j * blk + lax.broadcasted_iota(jnp.int32, (blk, blk), 1)
            sc = jnp.where(qpos >= kpos, sc, -jnp.inf)
            m_prev = m_scr[...]
            m_new = jnp.maximum(m_prev, jnp.max(sc, axis=-1, keepdims=True))
            alpha = jnp.exp(m_prev - m_new)
            p = jnp.exp(sc - m_new)
            l_scr[...] = alpha * l_scr[...] + jnp.sum(p, axis=-1, keepdims=True)
            acc_scr[...] = alpha * acc_scr[...] + _dot(p.astype(BF16), v_ref[...])
            m_scr[...] = m_new

        @pl.when(j == nb - 1)
        def _():
            o_ref[...] = acc_scr[...] / l_scr[...]
            lse_ref[...] = m_scr[...] + jnp.log(l_scr[...])

    return pl.pallas_call(
        body, grid=(hh, nb, nb),
        in_specs=[pl.BlockSpec((None, blk, MLA_QK), lambda h, i, j: (h, i, 0)),
                  pl.BlockSpec((None, blk, MLA_QK), lambda h, i, j: (h, jnp.minimum(i, j), 0)),
                  pl.BlockSpec((None, blk, MLA_V), lambda h, i, j: (h, jnp.minimum(i, j), 0))],
        out_specs=(pl.BlockSpec((blk, MLA_V), lambda h, i, j: (i, h)),
                   pl.BlockSpec((None, blk, 1), lambda h, i, j: (h, i, 0))),
        out_shape=(jax.ShapeDtypeStruct((s, hh * MLA_V), F32), jax.ShapeDtypeStruct((hh, s, 1), F32)),
        scratch_shapes=[pltpu.VMEM((blk, 1), F32), pltpu.VMEM((blk, 1), F32), pltpu.VMEM((blk, MLA_V), F32)],
        compiler_params=_params("parallel", "parallel", "arbitrary"), name=name)(q, k, v)


def _attn_delta(do, o, name):
    s = do.shape[0]
    ts = _row_tile(s, 512)
    hh = MLA_HEADS

    def body(do_ref, o_ref, d_ref):
        prod = do_ref[...] * o_ref[...]
        for h in range(hh):
            d_ref[h] = jnp.sum(prod[:, h * MLA_V:(h + 1) * MLA_V], axis=-1, keepdims=True)

    row = pl.BlockSpec((ts, hh * MLA_V), lambda i: (i, 0))
    return pl.pallas_call(
        body, grid=(s // ts,), in_specs=[row, row], out_specs=pl.BlockSpec((hh, ts, 1), lambda i: (0, i, 0)),
        out_shape=jax.ShapeDtypeStruct((hh, s, 1), F32), compiler_params=_params("parallel"), name=name)(do, o)


def _attn_bwd(q, k, v, do, lse_row, delta_row, name):
    hh, s, _ = q.shape
    blk = _attn_block(s)
    nb = s // blk
    scale = MLA_QK ** -0.5

    def body(q_ref, k_ref, v_ref, do_ref, lse_ref, dl_ref, dq_ref, dk_ref, dv_ref):
        j, i = pl.program_id(1), pl.program_id(2)

        @pl.when((j == 0) & (i == 0))
        def _():
            dq_ref[...] = jnp.zeros_like(dq_ref)

        @pl.when(i == 0)
        def _():
            dk_ref[...] = jnp.zeros_like(dk_ref)
            dv_ref[...] = jnp.zeros_like(dv_ref)

        @pl.when(i >= j)
        def _():
            qb, kb, vb = q_ref[...], k_ref[...], v_ref[...]
            dob = do_ref[...].astype(BF16)
            st = _dot(kb, qb, NT) * scale
            kpos = j * blk + lax.broadcasted_iota(jnp.int32, (blk, blk), 0)
            qpos = i * blk + lax.broadcasted_iota(jnp.int32, (blk, blk), 1)
            pt = jnp.where(qpos >= kpos, jnp.exp(st - lse_ref[...]), 0.0)
            dv_ref[...] += _dot(pt.astype(BF16), dob)
            dpt = _dot(vb, dob, NT)
            dst = (pt * (dpt - dl_ref[...]) * scale).astype(BF16)
            dk_ref[...] += _dot(dst, qb)
            rows = pl.ds(pl.multiple_of(i * blk, blk), blk)
            dq_ref[rows, :] += _dot(dst, kb, TN)

    def qmap(h, j, i):
        return (h, jnp.maximum(i, j), 0)

    return pl.pallas_call(
        body, grid=(hh, nb, nb),
        in_specs=[pl.BlockSpec((None, blk, MLA_QK), qmap),
                  pl.BlockSpec((None, blk, MLA_QK), lambda h, j, i: (h, j, 0)),
                  pl.BlockSpec((None, blk, MLA_V), lambda h, j, i: (h, j, 0)),
                  pl.BlockSpec((blk, MLA_V), lambda h, j, i: (jnp.maximum(i, j), h)),
                  pl.BlockSpec((None, 1, blk), lambda h, j, i: (h, 0, jnp.maximum(i, j))),
                  pl.BlockSpec((None, 1, blk), lambda h, j, i: (h, 0, jnp.maximum(i, j)))],
        out_specs=(pl.BlockSpec((None, s, MLA_QK), lambda h, j, i: (h, 0, 0)),
                   pl.BlockSpec((None, blk, MLA_QK), lambda h, j, i: (h, j, 0)),
                   pl.BlockSpec((None, blk, MLA_V), lambda h, j, i: (h, j, 0))),
        out_shape=(jax.ShapeDtypeStruct((hh, s, MLA_QK), F32), jax.ShapeDtypeStruct((hh, s, MLA_QK), F32),
                   jax.ShapeDtypeStruct((hh, s, MLA_V), F32)),
        compiler_params=_params("parallel", "arbitrary", "arbitrary"), name=name)(q, k, v, do, lse_row, delta_row)


def _mla_bwd_mid(dq, dk, dv, cos, sin, proj, qn, kvn, w_uq, w_ukv, name):
    s = proj.shape[0]
    ts = _row_tile(s, 256)
    hh, half = MLA_HEADS, MLA_ROPE // 2
    nq, nkv = hh * MLA_QK, hh * (MLA_NOPE + MLA_V)

    def body(dq_ref, dk_ref, dv_ref, c_ref, s_ref, p_ref, qn_ref, kvn_ref, wq_ref, wkv_ref,
             dqe_ref, dkve_ref, dp_ref, dqn_ref, dkvn_ref):
        cos_, sin_ = c_ref[...], s_ref[...]
        dkr1 = jnp.zeros((ts, half), F32)
        dkr2 = jnp.zeros((ts, half), F32)
        for h in range(hh):
            dqh, dkh = dq_ref[h], dk_ref[h]
            b = h * MLA_QK
            dqe_ref[:, b:b + MLA_NOPE] = dqh[:, :MLA_NOPE].astype(BF16)
            d1, d2 = _rope_bwd(dqh[:, MLA_NOPE:MLA_NOPE + half], dqh[:, MLA_NOPE + half:], cos_, sin_)
            dqe_ref[:, b + MLA_NOPE:b + MLA_NOPE + half] = d1.astype(BF16)
            dqe_ref[:, b + MLA_NOPE + half:b + MLA_QK] = d2.astype(BF16)
            b = h * (MLA_NOPE + MLA_V)
            dkve_ref[:, b:b + MLA_NOPE] = dkh[:, :MLA_NOPE].astype(BF16)
            dkve_ref[:, b + MLA_NOPE:b + MLA_NOPE + MLA_V] = dv_ref[h].astype(BF16)
            dkr1 = dkr1 + dkh[:, MLA_NOPE:MLA_NOPE + half]
            dkr2 = dkr2 + dkh[:, MLA_NOPE + half:]
        dkr1, dkr2 = _rope_bwd(dkr1, dkr2, cos_, sin_)
        dcqn = _dot(dqe_ref[...], wq_ref[...], NT)
        dckvn = _dot(dkve_ref[...], wkv_ref[...], NT)
        p = p_ref[...]
        dcq, dqn = _rms_bwd_rows(p[:, :MLA_Q_LORA], qn_ref[...], dcqn)
        dckv, dkvn = _rms_bwd_rows(p[:, MLA_Q_LORA:MLA_Q_LORA + MLA_KV_LORA], kvn_ref[...], dckvn)
        dp_ref[:, :MLA_Q_LORA] = dcq.astype(BF16)
        dp_ref[:, MLA_Q_LORA:MLA_Q_LORA + MLA_KV_LORA] = dckv.astype(BF16)
        dp_ref[:, MLA_Q_LORA + MLA_KV_LORA:MLA_Q_LORA + MLA_KV_LORA + half] = dkr1.astype(BF16)
        dp_ref[:, MLA_Q_LORA + MLA_KV_LORA + half:] = dkr2.astype(BF16)

        @pl.when(pl.program_id(0) == 0)
        def _():
            dqn_ref[...] = jnp.zeros_like(dqn_ref)
            dkvn_ref[...] = jnp.zeros_like(dkvn_ref)

        dqn_ref[...] += jnp.sum(dqn, axis=0, keepdims=True)
        dkvn_ref[...] += jnp.sum(dkvn, axis=0, keepdims=True)

    def row(w):
        return pl.BlockSpec((ts, w), lambda i: (i, 0))

    def full(shape):
        return pl.BlockSpec(shape, lambda i: (0,) * len(shape))

    def heads(w):
        return pl.BlockSpec((hh, ts, w), lambda i: (0, i, 0))

    return pl.pallas_call(
        body, grid=(s // ts,),
        in_specs=[heads(MLA_QK), heads(MLA_QK), heads(MLA_V), row(half), row(half), row(MLA_IN),
                  full(qn.shape), full(kvn.shape), full(w_uq.shape), full(w_ukv.shape)],
        out_specs=(row(nq), row(nkv), row(MLA_IN), full(qn.shape), full(kvn.shape)),
        out_shape=(jax.ShapeDtypeStruct((s, nq), BF16), jax.ShapeDtypeStruct((s, nkv), BF16),
                   jax.ShapeDtypeStruct((s, MLA_IN), BF16), jax.ShapeDtypeStruct(qn.shape, F32),
                   jax.ShapeDtypeStruct(kvn.shape, F32)),
        compiler_params=_params("arbitrary"), name=name)(dq, dk, dv, cos, sin, proj, qn, kvn, w_uq, w_ukv)


HGRN_TILE = 128


def _chunk_masks(t):
    r = lax.broadcasted_iota(jnp.int32, (t, t), 0)
    c = lax.broadcasted_iota(jnp.int32, (t, t), 1)
    same = (r // HGRN_CHUNK) == (c // HGRN_CHUNK)
    return r, c, same


def _hgrn_gates(p, lb):
    hk = HGRN_HEADS * HGRN_D
    qx, fx, ix, gx = p[:, :hk], p[:, hk:2 * hk], p[:, 2 * hk:3 * hk], p[:, 3 * hk:]
    sig_f = jax.nn.sigmoid(fx)
    f = lb + (1.0 - lb) * sig_f
    sig_q = jax.nn.sigmoid(qx)
    t = p.shape[0]
    r, c, same = _chunk_masks(t)
    lower = jnp.where(same & (c <= r), 1.0, 0.0).astype(F32)
    b = _dot_f32(lower, jnp.log(f))
    b3 = b.reshape(t // HGRN_CHUNK, HGRN_CHUNK, hk)
    bref = jnp.broadcast_to(b3[:, HGRN_CHUNK // 2:HGRN_CHUNK // 2 + 1, :], b3.shape).reshape(t, hk)
    blast = jnp.broadcast_to(b3[:, HGRN_CHUNK - 1:, :], b3.shape).reshape(t, hk)
    return qx, ix, gx, sig_f, f, sig_q, b, bref, blast


def _hgrn_fwd(proj, lb, onorm, name):
    s = proj.shape[0]
    t = _row_tile(s, HGRN_TILE)
    nc = t // HGRN_CHUNK
    hh, dd, hk = HGRN_HEADS, HGRN_D, HGRN_HEADS * HGRN_D

    def body(p_ref, lb_ref, on_ref, y_ref, o_ref, st_ref, st_scr):
        @pl.when(pl.program_id(0) == 0)
        def _():
            st_scr[...] = jnp.zeros_like(st_scr)

        qx, ix, gx, _, f, sig_q, b, bref, blast = _hgrn_gates(p_ref[...], lb_ref[...])
        q = qx * sig_q
        k = 1.0 - f
        r, c, same = _chunk_masks(t)
        causal = same & (c <= r)
        for h in range(hh):
            sl = slice(h * dd, (h + 1) * dd)
            bh, brefh, blasth, qh, kh = b[:, sl], bref[:, sl], blast[:, sl], q[:, sl], k[:, sl]
            vh = ix[:, sl].astype(BF16)
            q_rel = (qh * jnp.exp(bh - brefh)).astype(BF16)
            k_rel = (kh * jnp.exp(brefh - bh)).astype(BF16)
            a = jnp.where(causal, _dot(q_rel, k_rel, NT), 0.0)
            o_intra = _dot(a.astype(BF16), vh)
            q_dec = (qh * jnp.exp(bh)).astype(BF16)
            k_dec = (kh * jnp.exp(blasth - bh)).astype(BF16)
            dec = jnp.exp(blasth)
            pieces = []
            for ci in range(nc):
                rows = slice(ci * HGRN_CHUNK, (ci + 1) * HGRN_CHUNK)
                st = st_scr[h]
                st_ref[ci, h] = st
                pieces.append(_dot(q_dec[rows], st.astype(BF16), NT))
                st_scr[h] = st * dec[ci * HGRN_CHUNK:ci * HGRN_CHUNK + 1, :] + _dot(vh[rows], k_dec[rows], TN)
            oh = o_intra + jnp.concatenate(pieces, axis=0)
            o_ref[:, sl] = oh
            gate = gx[:, sl] * jax.nn.sigmoid(gx[:, sl])
            y_ref[:, sl] = (oh * _rstd(oh) * on_ref[...] * gate).astype(BF16)

    return pl.pallas_call(
        body, grid=(s // t,),
        in_specs=[pl.BlockSpec((t, 4 * hk), lambda i: (i, 0)), pl.BlockSpec((1, hk), lambda i: (0, 0)),
                  pl.BlockSpec((1, dd), lambda i: (0, 0))],
        out_specs=(pl.BlockSpec((t, hk), lambda i: (i, 0)), pl.BlockSpec((t, hk), lambda i: (i, 0)),
                   pl.BlockSpec((nc, hh, dd, dd), lambda i: (i, 0, 0, 0))),
        out_shape=(jax.ShapeDtypeStruct((s, hk), BF16), jax.ShapeDtypeStruct((s, hk), F32),
                   jax.ShapeDtypeStruct((s // HGRN_CHUNK, hh, dd, dd), F32)),
        scratch_shapes=[pltpu.VMEM((hh, dd, dd), F32)],
        compiler_params=_params("arbitrary"), name=name)(proj, lb, onorm)


def _hgrn_bwd(proj, lb, onorm, o, states, dy, name):
    s = proj.shape[0]
    t = _row_tile(s, HGRN_TILE)
    nt = s // t
    nc = t // HGRN_CHUNK
    hh, dd, hk = HGRN_HEADS, HGRN_D, HGRN_HEADS * HGRN_D

    def body(p_ref, lb_ref, on_ref, o_ref, st_ref, dy_ref, dp_ref, dlb_ref, don_ref, dst_scr, cat_scr, ext_scr, dk_scr, dq_scr):
        @pl.when(pl.program_id(0) == 0)
        def _():
            dst_scr[...] = jnp.zeros_like(dst_scr)
            dlb_ref[...] = jnp.zeros_like(dlb_ref)
            don_ref[...] = jnp.zeros_like(don_ref)

        lbv = lb_ref[...]
        qx, ix, gx, sig_f, f, sig_q, b, bref, blast = _hgrn_gates(p_ref[...], lbv)
        q = qx * sig_q
        k = 1.0 - f
        r, c, same = _chunk_masks(t)
        causal = same & (c <= r)
        on = on_ref[...]
        don = jnp.zeros((1, dd), F32)
        for h in range(hh):
            sl = slice(h * dd, (h + 1) * dd)
            oh = o_ref[:, sl]
            dyh = dy_ref[:, sl]
            gxh = gx[:, sl]
            sig_g = jax.nn.sigmoid(gxh)
            rs = _rstd(oh)
            dgate = dyh * (oh * rs * on)
            dp_ref[:, 3 * hk + h * dd:3 * hk + (h + 1) * dd] = (dgate * (sig_g * (1.0 + gxh * (1.0 - sig_g)))).astype(BF16)
            do, donh = _rms_bwd_rows(oh, on, dyh * (gxh * sig_g))
            don = don + jnp.sum(donh, axis=0, keepdims=True)
            dob = do.astype(BF16)
            bh, brefh, blasth, qh, kh = b[:, sl], bref[:, sl], blast[:, sl], q[:, sl], k[:, sl]
            vh = ix[:, sl].astype(BF16)
            e_qr, e_kr, e_qd, e_kd = jnp.exp(bh - brefh), jnp.exp(brefh - bh), jnp.exp(bh), jnp.exp(blasth - bh)
            dec = jnp.exp(blasth)
            q_rel, k_rel, q_dec, k_dec = qh * e_qr, kh * e_kr, qh * e_qd, kh * e_kd
            q_relb, k_relb, q_decb, k_decb = q_rel.astype(BF16), k_rel.astype(BF16), q_dec.astype(BF16), k_dec.astype(BF16)
            a = jnp.where(causal, _dot(q_relb, k_relb, NT), 0.0).astype(BF16)
            dv = _dot(a, dob, TN)
            da = jnp.where(causal, _dot(dob, vh, NT), 0.0).astype(BF16)
            dq_rel = _dot(da, k_relb)
            dk_rel = _dot(da, q_relb, TN)
            dq_dec, dk_dec, dv_inter, ddec = [None] * nc, [None] * nc, [None] * nc, [None] * nc
            for ci in range(nc - 1, -1, -1):
                rows = slice(ci * HGRN_CHUNK, (ci + 1) * HGRN_CHUNK)
                st = st_ref[ci, h]
                dst = dst_scr[h]
                dstb = dst.astype(BF16)
                dq_dec[ci] = _dot(dob[rows], st.astype(BF16))
                dk_dec[ci] = _dot(vh[rows], dstb)
                dv_inter[ci] = _dot(k_decb[rows], dstb, NT)
                ddec[ci] = jnp.broadcast_to(jnp.sum(dst * st, axis=0, keepdims=True), (HGRN_CHUNK, dd))
                dst_scr[h] = dst * dec[ci * HGRN_CHUNK:ci * HGRN_CHUNK + 1, :] + _dot(dob[rows], q_decb[rows], TN)
            dq_dec = jnp.concatenate(dq_dec, axis=0)
            dk_dec = jnp.concatenate(dk_dec, axis=0)
            dv = dv + jnp.concatenate(dv_inter, axis=0)
            ddec = jnp.concatenate(ddec, axis=0)
            dp_ref[:, 2 * hk + h * dd:2 * hk + (h + 1) * dd] = dv.astype(BF16)
            dq_scr[:, sl] = dq_rel * e_qr + dq_dec * e_qd
            dk_scr[:, sl] = dk_rel * e_kr + dk_dec * e_kd
            g_qr, g_kr, g_qd, g_kd = dq_rel * q_rel, dk_rel * k_rel, dq_dec * q_dec, dk_dec * k_dec
            cat_scr[0:t, sl] = g_qr - g_kr + g_qd - g_kd
            cat_scr[t:2 * t, sl] = g_kr - g_qr
            cat_scr[2 * t:3 * t, sl] = g_kd
            ext_scr[:, sl] = ddec * dec
        upper = jnp.where(same & (c >= r), 1.0, 0.0).astype(F32)
        to_ref = jnp.where(same & (r % HGRN_CHUNK <= HGRN_CHUNK // 2), 1.0, 0.0).astype(F32)
        to_all = jnp.where(same, 1.0, 0.0).astype(F32)
        dlogf = _dot_f32(jnp.concatenate([upper, to_ref, to_all], axis=1), cat_scr[...]) + ext_scr[...]
        df = dlogf / f - dk_scr[...]
        dp_ref[:, hk:2 * hk] = (df * (1.0 - lbv) * sig_f * (1.0 - sig_f)).astype(BF16)
        dp_ref[:, 0:hk] = (dq_scr[...] * (sig_q * (1.0 + qx * (1.0 - sig_q)))).astype(BF16)
        dlb_ref[...] += jnp.sum(df * (1.0 - sig_f), axis=0, keepdims=True)
        don_ref[...] += don

    def rev(i):
        return nt - 1 - i

    return pl.pallas_call(
        body, grid=(nt,),
        in_specs=[pl.BlockSpec((t, 4 * hk), lambda i: (rev(i), 0)), pl.BlockSpec((1, hk), lambda i: (0, 0)),
                  pl.BlockSpec((1, dd), lambda i: (0, 0)), pl.BlockSpec((t, hk), lambda i: (rev(i), 0)),
                  pl.BlockSpec((nc, hh, dd, dd), lambda i: (rev(i), 0, 0, 0)), pl.BlockSpec((t, hk), lambda i: (rev(i), 0))],
        out_specs=(pl.BlockSpec((t, 4 * hk), lambda i: (rev(i), 0)), pl.BlockSpec((1, hk), lambda i: (0, 0)),
                   pl.BlockSpec((1, dd), lambda i: (0, 0))),
        out_shape=(jax.ShapeDtypeStruct((s, 4 * hk), BF16), jax.ShapeDtypeStruct((1, hk), F32),
                   jax.ShapeDtypeStruct((1, dd), F32)),
        scratch_shapes=[pltpu.VMEM((hh, dd, dd), F32), pltpu.VMEM((3 * t, hk), F32), pltpu.VMEM((t, hk), F32),
                        pltpu.VMEM((t, hk), F32), pltpu.VMEM((t, hk), F32)],
        compiler_params=_params("arbitrary"), name=name)(proj, lb, onorm, o, states, dy)


def _adamw(w, g, m, v, name):
    rows, cols = w.shape
    tr = rows
    for cand in (256, 128, 64, 32, 16, 8):
        if rows % cand == 0 and rows > cand:
            tr = cand
            break

    def body(w_ref, g_ref, m_ref, v_ref, d_ref, nm_ref, nv_ref):
        gg = g_ref[...]
        nm = ADAM_B1 * m_ref[...] + (1.0 - ADAM_B1) * gg
        nv = ADAM_B2 * v_ref[...] + (1.0 - ADAM_B2) * (gg * gg)
        m_hat = nm / (1.0 - ADAM_B1 ** ADAM_STEP)
        v_hat = nv / (1.0 - ADAM_B2 ** ADAM_STEP)
        d_ref[...] = -ADAM_LR * (m_hat / (jnp.sqrt(v_hat) + ADAM_EPS) + ADAM_WD * w_ref[...])
        nm_ref[...] = nm
        nv_ref[...] = nv

    blk = pl.BlockSpec((tr, cols), lambda i: (i, 0))
    shp = jax.ShapeDtypeStruct((rows, cols), F32)
    return pl.pallas_call(
        body, grid=(rows // tr,), in_specs=[blk] * 4, out_specs=(blk,) * 3, out_shape=(shp,) * 3,
        compiler_params=_params("parallel"), name=name)(w, g, m, v)


_HBM = pl.BlockSpec(memory_space=pltpu.HBM)


def _all_gather(xs, name):
    def body(x_ref, out_ref, send_sems, recv_sems, local_sem):
        x, y, c = lax.axis_index("x"), lax.axis_index("y"), lax.axis_index("c")
        me, sibling = (x, y, c), (x, y, 1 - c)
        chips = [(1 - x, y), (x, 1 - y), (1 - x, 1 - y)]

        def slot(px, py, pc):
            return out_ref.at[4 * px + 2 * py + pc]

        def copy(k, block, to, src=None):
            return pltpu.make_async_remote_copy(
                src_ref=slot(*block) if src is None else src, dst_ref=slot(*block), send_sem=send_sems.at[k],
                recv_sem=recv_sems.at[k], device_id=to, device_id_type=pl.DeviceIdType.MESH)

        mine = pltpu.make_async_copy(x_ref, slot(*me), local_sem)
        mine.start()
        first = [copy(0, me, sibling, src=x_ref)]
        first += [copy(1 + j, me, (*chip, c), src=x_ref) for j, chip in enumerate(chips)]
        for cp in first:
            cp.start()
        passed = [copy(4 + j, (*chip, c), sibling) for j, chip in enumerate(chips)]
        for j, chip in enumerate(chips):
            copy(1 + j, (*chip, c), me).wait_recv()
            passed[j].start()
        copy(0, sibling, me).wait_recv()
        for j, chip in enumerate(chips):
            copy(4 + j, (*chip, 1 - c), me).wait_recv()
        for cp in first + passed:
            cp.wait_send()
        mine.wait()

    return pl.pallas_call(
        body, out_shape=jax.ShapeDtypeStruct((N_DEV,) + xs.shape, xs.dtype), in_specs=[_HBM], out_specs=_HBM,
        scratch_shapes=[pltpu.SemaphoreType.DMA((7,)), pltpu.SemaphoreType.DMA((7,)), pltpu.SemaphoreType.DMA],
        name=name)(xs)


def _exchange(send, name):
    def body(s_ref, land_ref, send_sems, recv_sems, local_sem):
        x, y, c = lax.axis_index("x"), lax.axis_index("y"), lax.axis_index("c")
        me = 4 * x + 2 * y + c
        own = pltpu.make_async_copy(s_ref.at[me], land_ref.at[me], local_sem)
        own.start()
        sends, recvs = [], []
        for rel in range(1, N_DEV):
            px = 1 - x if rel & 4 else x
            py = 1 - y if rel & 2 else y
            pc = 1 - c if rel & 1 else c
            peer = 4 * px + 2 * py + pc
            sends.append(pltpu.make_async_remote_copy(
                src_ref=s_ref.at[peer], dst_ref=land_ref.at[me], send_sem=send_sems.at[rel - 1],
                recv_sem=recv_sems.at[rel - 1], device_id=(px, py, pc), device_id_type=pl.DeviceIdType.MESH))
            recvs.append(pltpu.make_async_remote_copy(
                src_ref=s_ref.at[me], dst_ref=land_ref.at[peer], send_sem=send_sems.at[rel - 1],
                recv_sem=recv_sems.at[rel - 1], device_id=(px, py, pc), device_id_type=pl.DeviceIdType.MESH))
        for cp in sends:
            cp.start()
        for cp in recvs:
            cp.wait_recv()
        for cp in sends:
            cp.wait_send()
        own.wait()

    return pl.pallas_call(
        body, out_shape=jax.ShapeDtypeStruct(send.shape, send.dtype), in_specs=[_HBM], out_specs=_HBM,
        scratch_shapes=[pltpu.SemaphoreType.DMA((7,)), pltpu.SemaphoreType.DMA((7,)), pltpu.SemaphoreType.DMA],
        name=name)(send)


def _sum_slots(parts, name):
    _, rows, cols = parts.shape
    tr = rows
    for cand in (256, 128, 64, 32, 16, 8):
        if rows % cand == 0 and rows > cand:
            tr = cand
            break

    def body(p_ref, o_ref):
        acc = p_ref[0].astype(F32)
        for slot in range(1, N_DEV):
            acc = acc + p_ref[slot].astype(F32)
        o_ref[...] = acc

    return pl.pallas_call(
        body, grid=(rows // tr,), in_specs=[pl.BlockSpec((N_DEV, tr, cols), lambda i: (0, i, 0))],
        out_specs=pl.BlockSpec((tr, cols), lambda i: (i, 0)), out_shape=jax.ShapeDtypeStruct((rows, cols), F32),
        compiler_params=_params("parallel"), name=name)(parts)


def _mlp_fwd(h, g_pre, g_post, w1, w2, tag):
    a = _rms_fwd(h, g_pre, None, BF16, f"{tag}_norm")
    u, r2 = _mm(a, w1, "nn", 512, 1024, f"{tag}_up", out_dtypes=(F32, BF16),
                epi=lambda acc: (acc, jnp.square(jnp.maximum(acc, 0.0))))
    z = _mm(r2, w2, "nn", 512, 512, f"{tag}_down")
    out = _rms_fwd(z, g_post, h, F32, f"{tag}_out")
    return out, (h, a, u, r2, z)


def _mlp_bwd(dh, saved, g_pre, g_post, w1, w2, tag):
    h, a, u, r2, z = saved
    dz, dg_post = _rms_bwd(z, g_post, dh, None, BF16, f"{tag}_dout")
    du = _mm(dz, w2, "nt", 512, 1024, f"{tag}_ddown", out_dtypes=(BF16,), extras=(u,),
             epi=lambda acc, uu: (acc * (2.0 * jnp.maximum(uu, 0.0)),))
    dw2 = _mm(r2, dz, "tn", 512, 512, f"{tag}_dw2", out_dtypes=(BF16,), shard="rows")
    dw1 = _mm(a, du, "tn", 512, 512, f"{tag}_dw1", out_dtypes=(BF16,), shard="cols")
    da = _mm(du, w1, "nt", 512, 512, f"{tag}_dup")
    dh_in, dg_pre = _rms_bwd(h, g_pre, da, dh, F32, f"{tag}_dnorm")
    return dh_in, dg_pre, dg_post, dw1, dw2


def _hgrn_layer_fwd(h, g_pre, g_post, lb, onorm, w_in, w_o, tag):
    a = _rms_fwd(h, g_pre, None, BF16, f"{tag}_norm")
    proj = _mm(a, w_in, "nn", 512, 1024, f"{tag}_in")
    y, o, states = _hgrn_fwd(proj, lb, onorm, f"{tag}_scan")
    m = _mm(y, w_o, "nn", 512, 1024, f"{tag}_o")
    out = _rms_fwd(m, g_post, h, F32, f"{tag}_out")
    return out, (h, a, proj, y, o, states, m)


def _hgrn_layer_bwd(dh, saved, g_pre, g_post, lb, onorm, w_in, w_o, tag):
    h, a, proj, y, o, states, m = saved
    dm, dg_post = _rms_bwd(m, g_post, dh, None, BF16, f"{tag}_dout")
    dy = _mm(dm, w_o, "nt", 512, 1024, f"{tag}_do")
    dw_o = _mm(y, dm, "tn", 128, 1024, f"{tag}_dwo", out_dtypes=(BF16,), shard="rows")
    dproj, dlb, donorm = _hgrn_bwd(proj, lb, onorm, o, states, dy, f"{tag}_dscan")
    dw_in = _mm(a, dproj, "tn", 512, 512, f"{tag}_dwin", out_dtypes=(BF16,), shard="cols")
    da = _mm(dproj, w_in, "nt", 512, 512, f"{tag}_din")
    dh_in, dg_pre = _rms_bwd(h, g_pre, da, dh, F32, f"{tag}_dnorm")
    return dh_in, dg_pre, dg_post, dlb, donorm, dw_in, dw_o


def _mla_layer_fwd(h, g_pre, g_post, cos, sin, w_in, qn, kvn, w_uq, w_ukv, w_o, tag):
    a = _rms_fwd(h, g_pre, None, BF16, f"{tag}_norm")
    proj = _mm(a, w_in, "nn", 512, MLA_IN, f"{tag}_in")
    cqn, ckvn, q, k, v = _mla_qkv(proj, qn, kvn, w_uq, w_ukv, cos, sin, f"{tag}_qkv")
    o, lse = _attn_fwd(q, k, v, f"{tag}_attn")
    m = _mm(o, w_o, "nn", 512, 1024, f"{tag}_o")
    out = _rms_fwd(m, g_post, h, F32, f"{tag}_out")
    return out, (h, a, proj, cqn, ckvn, q, k, v, o, lse, m)


def _mla_layer_bwd(dh, saved, g_pre, g_post, cos, sin, w_in, qn, kvn, w_uq, w_ukv, w_o, tag):
    h, a, proj, cqn, ckvn, q, k, v, o, lse, m = saved
    hh, s = q.shape[0], q.shape[1]
    dm, dg_post = _rms_bwd(m, g_post, dh, None, BF16, f"{tag}_dout")
    do = _mm(dm, w_o, "nt", 512, 1024, f"{tag}_do")
    dw_o = _mm(o, dm, "tn", 128, 1024, f"{tag}_dwo", out_dtypes=(BF16,), shard="rows")
    delta = _attn_delta(do, o, f"{tag}_delta")
    dq, dk, dv = _attn_bwd(q, k, v, do, lse.reshape(hh, 1, s), delta.reshape(hh, 1, s), f"{tag}_dattn")
    dqe, dkve, dproj, dqn, dkvn = _mla_bwd_mid(dq, dk, dv, cos, sin, proj, qn, kvn, w_uq, w_ukv, f"{tag}_dqkv")
    dw_uq = _mm(cqn, dqe, "tn", MLA_Q_LORA, 768, f"{tag}_dwuq", out_dtypes=(BF16,))
    dw_ukv = _mm(ckvn, dkve, "tn", MLA_KV_LORA, 256, f"{tag}_dwukv", out_dtypes=(BF16,), shard="cols")
    dw_in = _mm(a, dproj, "tn", 128, MLA_IN, f"{tag}_dwin", out_dtypes=(BF16,), shard="rows")
    da = _mm(dproj, w_in, "nt", 512, 1024, f"{tag}_din")
    dh_in, dg_pre = _rms_bwd(h, g_pre, da, dh, F32, f"{tag}_dnorm")
    return dh_in, dg_pre, dg_post, dqn, dkvn, dw_in, dw_uq, dw_ukv, dw_o


_BIG = (("mla_w_in", "rows"), ("mla_w_uq", "cols"), ("mla_w_ukv", "cols"), ("mla_w_o", "rows"),
        ("hgrn_w_in", "cols"), ("hgrn_w_o", "rows"), ("mlp_w1", "cols"), ("mlp_w2", "rows"))
_FLAT_COLS = 1024


def _pack(parts):
    return jnp.concatenate([p.reshape(-1) for p in parts]).reshape(-1, _FLAT_COLS)


def _unpack_gathered(gathered, shards):
    flat = gathered.reshape(N_DEV, -1)
    out, off = {}, 0
    for (name, cut), shard in zip(_BIG, shards):
        ll, r, c = shard.shape
        blk = flat[:, off:off + ll * r * c].reshape(N_DEV, ll, r, c)
        off += ll * r * c
        if cut == "rows":
            out[name] = blk.transpose(1, 0, 2, 3).reshape(ll, N_DEV * r, c)
        else:
            out[name] = blk.transpose(1, 2, 0, 3).reshape(ll, r, N_DEV * c)
    return out


def _split_flat(flat, shards):
    flat = flat.reshape(-1)
    out, off = {}, 0
    for (name, _), shard in zip(_BIG, shards):
        n = int(np.prod(shard.shape))
        out[name] = flat[off:off + n].reshape(shard.shape)
        off += n
    return out


def _adamw_nd(w, g, m, v, name):
    shape = w.shape
    c = shape[-1]
    d, nm, nv = _adamw(w.reshape(-1, c), g.reshape(-1, c), m.reshape(-1, c), v.reshape(-1, c), name)
    return d.reshape(shape), nm.reshape(shape), nv.reshape(shape)


def kernel(x, positions, norm_gains, mla_w_in, mla_q_norm, mla_kv_norm, mla_w_uq, mla_w_ukv, mla_w_o, hgrn_w_in, hgrn_lb_logits, hgrn_o_norm, hgrn_w_o, mlp_w1, mlp_w2, loss_target, m_norm_gains, m_mla_w_in, m_mla_q_norm, m_mla_kv_norm, m_mla_w_uq, m_mla_w_ukv, m_mla_w_o, m_hgrn_w_in, m_hgrn_lb_logits, m_hgrn_o_norm, m_hgrn_w_o, m_mlp_w1, m_mlp_w2, v_norm_gains, v_mla_w_in, v_mla_q_norm, v_mla_kv_norm, v_mla_w_uq, v_mla_w_ukv, v_mla_w_o, v_hgrn_w_in, v_hgrn_lb_logits, v_hgrn_o_norm, v_hgrn_w_o, v_mlp_w1, v_mlp_w2):
    weights = dict(norm_gains=norm_gains, mla_w_in=mla_w_in, mla_q_norm=mla_q_norm, mla_kv_norm=mla_kv_norm,
                   mla_w_uq=mla_w_uq, mla_w_ukv=mla_w_ukv, mla_w_o=mla_w_o, hgrn_w_in=hgrn_w_in,
                   hgrn_lb_logits=hgrn_lb_logits, hgrn_o_norm=hgrn_o_norm, hgrn_w_o=hgrn_w_o, mlp_w1=mlp_w1, mlp_w2=mlp_w2)
    mom_m = dict(norm_gains=m_norm_gains, mla_w_in=m_mla_w_in, mla_q_norm=m_mla_q_norm, mla_kv_norm=m_mla_kv_norm,
                 mla_w_uq=m_mla_w_uq, mla_w_ukv=m_mla_w_ukv, mla_w_o=m_mla_w_o, hgrn_w_in=m_hgrn_w_in,
                 hgrn_lb_logits=m_hgrn_lb_logits, hgrn_o_norm=m_hgrn_o_norm, hgrn_w_o=m_hgrn_w_o, mlp_w1=m_mlp_w1, mlp_w2=m_mlp_w2)
    mom_v = dict(norm_gains=v_norm_gains, mla_w_in=v_mla_w_in, mla_q_norm=v_mla_q_norm, mla_kv_norm=v_mla_kv_norm,
                 mla_w_uq=v_mla_w_uq, mla_w_ukv=v_mla_w_ukv, mla_w_o=v_mla_w_o, hgrn_w_in=v_hgrn_w_in,
                 hgrn_lb_logits=v_hgrn_lb_logits, hgrn_o_norm=v_hgrn_o_norm, hgrn_w_o=v_hgrn_w_o, mlp_w1=v_mlp_w1, mlp_w2=v_mlp_w2)
    order = list(weights)
    seq = x.shape[1]
    h = x.reshape(seq, D_MODEL)
    target = loss_target.reshape(seq, D_MODEL)

    shards = [weights[name] for name, _ in _BIG]
    full = _unpack_gathered(_all_gather(_pack([sh.astype(BF16) for sh in shards]), "gather_weights"), shards)
    gains = _all_gather(norm_gains.reshape(DEPTH * 4, D_MODEL // N_DEV), "gather_gains")
    gains = gains.transpose(1, 0, 2).reshape(DEPTH, 4, 1, D_MODEL)

    cos, sin = _rope_tables(positions.reshape(seq, 1), "rope_tables")
    lower = _lb_fwd(hgrn_lb_logits, "lower_bounds")

    saved = []
    for layer in range(DEPTH):
        slot = layer // 2
        g = gains[layer]
        if layer % 2 == 0:
            mix = (cos, sin, full["mla_w_in"][slot], mla_q_norm[slot:slot + 1], mla_kv_norm[slot:slot + 1],
                   full["mla_w_uq"][slot], full["mla_w_ukv"][slot], full["mla_w_o"][slot])
            h, sv_mix = _mla_layer_fwd(h, g[0], g[1], *mix, f"l{layer}_mla")
        else:
            mix = (lower[layer:layer + 1], hgrn_o_norm[slot:slot + 1], full["hgrn_w_in"][slot], full["hgrn_w_o"][slot])
            h, sv_mix = _hgrn_layer_fwd(h, g[0], g[1], *mix, f"l{layer}_hgrn")
        h, sv_mlp = _mlp_fwd(h, g[2], g[3], full["mlp_w1"][layer], full["mlp_w2"][layer], f"l{layer}_mlp")
        saved.append((mix, sv_mix, sv_mlp))

    loss_part, dh = _loss(h, target, "loss")
    loss = lax.psum(loss_part[0, 0], AXES)

    zero_row = jnp.zeros((1, D_MODEL), F32)
    dgains = [[None] * 4 for _ in range(DEPTH)]
    dlower = [zero_row] * DEPTH
    big = {name: [None] * (DEPTH if name.startswith("mlp") else DEPTH // 2) for name, _ in _BIG}
    dqn, dkvn, donorm = [None] * 2, [None] * 2, [None] * 2
    for layer in range(DEPTH - 1, -1, -1):
        slot = layer // 2
        g = gains[layer]
        mix, sv_mix, sv_mlp = saved[layer]
        dh, dgains[layer][2], dgains[layer][3], big["mlp_w1"][layer], big["mlp_w2"][layer] = _mlp_bwd(
            dh, sv_mlp, g[2], g[3], full["mlp_w1"][layer], full["mlp_w2"][layer], f"l{layer}_mlp")
        if layer % 2 == 0:
            (dh, dgains[layer][0], dgains[layer][1], dqn[slot], dkvn[slot], big["mla_w_in"][slot], dw_uq,
             big["mla_w_ukv"][slot], big["mla_w_o"][slot]) = _mla_layer_bwd(dh, sv_mix, g[0], g[1], *mix, f"l{layer}_mla")
            big["mla_w_uq"][slot] = dw_uq.reshape(MLA_Q_LORA, N_DEV, -1).transpose(1, 0, 2)
        else:
            (dh, dgains[layer][0], dgains[layer][1], dlower[layer], donorm[slot], big["hgrn_w_in"][slot],
             big["hgrn_w_o"][slot]) = _hgrn_layer_bwd(dh, sv_mix, g[0], g[1], *mix, f"l{layer}_hgrn")
    grad_x = dh.reshape(x.shape)
    dlogits = _lb_bwd(hgrn_lb_logits, jnp.concatenate(dlower, axis=0), "lower_bounds_bwd")

    send = jnp.concatenate([jnp.stack(big[name], axis=1).reshape(N_DEV, -1) for name, _ in _BIG], axis=1)
    send = send.reshape(N_DEV, -1, _FLAT_COLS)
    gbig = _split_flat(_sum_slots(_exchange(send, "exchange_grads"), "sum_grads"), shards)

    pad = jnp.zeros((1, D_MODEL - 2 * MLA_KV_LORA), F32)
    pad2 = jnp.zeros((1, D_MODEL - 2 * HGRN_D), F32)
    small = jnp.concatenate(
        [jnp.concatenate([gg for row in dgains for gg in row], axis=0), jnp.concatenate(dqn, axis=1),
         jnp.concatenate(dkvn + [pad], axis=1), dlogits, jnp.concatenate(donorm + [pad2], axis=1)], axis=0)
    small = _sum_slots(_all_gather(small, "gather_small_grads"), "sum_small_grads")
    me = 4 * lax.axis_index("x") + 2 * lax.axis_index("y") + lax.axis_index("c")
    n_g = DEPTH * 4
    width = D_MODEL // N_DEV
    grads = dict(gbig)
    grads["norm_gains"] = lax.dynamic_slice(small[:n_g], (0, me * width), (n_g, width)).reshape(DEPTH, 4, width)
    grads["mla_q_norm"] = small[n_g].reshape(2, MLA_Q_LORA)
    grads["mla_kv_norm"] = small[n_g + 1, :2 * MLA_KV_LORA].reshape(2, MLA_KV_LORA)
    grads["hgrn_lb_logits"] = small[n_g + 2:n_g + 2 + DEPTH]
    grads["hgrn_o_norm"] = small[n_g + 2 + DEPTH, :2 * HGRN_D].reshape(2, HGRN_D)

    deltas, new_m, new_v = {}, {}, {}
    for name in order:
        deltas[name], new_m[name], new_v[name] = _adamw_nd(weights[name], grads[name], mom_m[name], mom_v[name], f"adamw_{name}")
    return (loss, grad_x, *[grads[n] for n in order], *[deltas[n] for n in order], *[new_m[n] for n in order],
            *[new_v[n] for n in order])
```

```python
import numpy as np
import jax
import jax.numpy as jnp
from jax import lax
from jax.experimental import pallas as pl
from jax.experimental.pallas import tpu as pltpu

F32, BF16 = jnp.float32, jnp.bfloat16

N_DEV = 8
AXES = ("x", "y", "c")
D_MODEL = 1024
DEPTH = 4
MLA_HEADS = 8
MLA_Q_LORA = 512
MLA_KV_LORA = 256
MLA_NOPE = 128
MLA_ROPE = 64
MLA_V = 128
MLA_QK = MLA_NOPE + MLA_ROPE
MLA_IN = MLA_Q_LORA + MLA_KV_LORA + MLA_ROPE
ROPE_BASE = 10000.0
HGRN_HEADS = 8
HGRN_D = 128
HGRN_CHUNK = 32
D_FF = 4 * D_MODEL
EPS = 1e-6
LOG2_E = 1.4426950408889634
ADAM_LR, ADAM_B1, ADAM_B2, ADAM_EPS, ADAM_WD, ADAM_STEP = 0.001, 0.9, 0.999, 1e-08, 0.01, 10

V7X_VMEM_LIMIT_BYTES = 56 * 1024 * 1024

NN = (((1,), (0,)), ((), ()))
NT = (((1,), (1,)), ((), ()))
TN = (((0,), (0,)), ((), ()))
_DIMS = {"nn": NN, "nt": NT, "tn": TN}


def _params(*sem):
    return pltpu.CompilerParams(dimension_semantics=sem, vmem_limit_bytes=V7X_VMEM_LIMIT_BYTES)


def _dot(a, b, dims=NN):
    return lax.dot_general(a, b, dims, preferred_element_type=F32)


def _dot_f32(a, b):
    return lax.dot_general(a, b, NN, precision=lax.Precision.HIGHEST, preferred_element_type=F32)


def _rstd(x):
    return lax.rsqrt(jnp.mean(x * x, axis=-1, keepdims=True) + EPS)


def _rms_bwd_rows(x, g, dy):
    r = _rstd(x)
    xh = x * r
    dyg = dy * g
    dx = r * (dyg - xh * jnp.mean(dyg * xh, axis=-1, keepdims=True))
    return dx, dy * xh


def _row_tile(n, want):
    t = min(n, want)
    assert n % t == 0, (n, t)
    return t


def _divisor_tile(n, cap, mult):
    for t in range(min(cap, n) - min(cap, n) % mult, 0, -mult):
        if n % t == 0:
            return t
    return n


def _rms_fwd(x, g, res, out_dtype, name):
    s, d = x.shape
    ts = _row_tile(s, 512)

    def body(x_ref, g_ref, *rest):
        xf = x_ref[...]
        y = xf * _rstd(xf) * g_ref[...]
        if res is not None:
            y = rest[0][...] + y
        rest[-1][...] = y.astype(out_dtype)

    row = pl.BlockSpec((ts, d), lambda i: (i, 0))
    vec = pl.BlockSpec((1, d), lambda i: (0, 0))
    ins = [x, g] + ([res] if res is not None else [])
    return pl.pallas_call(
        body, grid=(s // ts,), in_specs=[row, vec] + ([row] if res is not None else []), out_specs=row,
        out_shape=jax.ShapeDtypeStruct((s, d), out_dtype), compiler_params=_params("parallel"), name=name)(*ins)


def _rms_bwd(x, g, dy, res, out_dtype, name):
    s, d = x.shape
    ts = _row_tile(s, 512)

    def body(x_ref, g_ref, dy_ref, *rest):
        dx_ref, dg_ref = rest[-2:]
        dx, dg = _rms_bwd_rows(x_ref[...], g_ref[...], dy_ref[...].astype(F32))
        if res is not None:
            dx = rest[0][...] + dx
        dx_ref[...] = dx.astype(out_dtype)

        @pl.when(pl.program_id(0) == 0)
        def _():
            dg_ref[...] = jnp.zeros_like(dg_ref)

        dg_ref[...] += jnp.sum(dg, axis=0, keepdims=True)

    row = pl.BlockSpec((ts, d), lambda i: (i, 0))
    vec = pl.BlockSpec((1, d), lambda i: (0, 0))
    ins = [x, g, dy] + ([res] if res is not None else [])
    return pl.pallas_call(
        body, grid=(s // ts,), in_specs=[row, vec, row] + ([row] if res is not None else []), out_specs=(row, vec),
        out_shape=(jax.ShapeDtypeStruct((s, d), out_dtype), jax.ShapeDtypeStruct((1, d), F32)),
        compiler_params=_params("arbitrary"), name=name)(*ins)


def _mm(a, b, mode, tm, tn, name, out_dtypes=(F32,), shard=None, epi=None, extras=()):
    if mode == "tn":
        k, m = a.shape
        a_spec = pl.BlockSpec((k, tm), lambda i, j: (0, i))
    else:
        m, k = a.shape
        a_spec = pl.BlockSpec((tm, k), lambda i, j: (i, 0))
    if mode == "nt":
        n = b.shape[0]
        b_spec = pl.BlockSpec((tn, k), lambda i, j: (j, 0))
    else:
        n = b.shape[1]
        b_spec = pl.BlockSpec((k, tn), lambda i, j: (0, j))
    assert m % tm == 0 and n % tn == 0, (name, m, tm, n, tn)
    tile = pl.BlockSpec((tm, tn), lambda i, j: (i, j))
    if shard == "rows":
        per = m // N_DEV // tm
        out_specs = [pl.BlockSpec((None, tm, tn), lambda i, j: (i // per, i % per, j))]
        out_shape = [jax.ShapeDtypeStruct((N_DEV, m // N_DEV, n), out_dtypes[0])]
    elif shard == "cols":
        per = n // N_DEV // tn
        out_specs = [pl.BlockSpec((None, tm, tn), lambda i, j: (j // per, i, j % per))]
        out_shape = [jax.ShapeDtypeStruct((N_DEV, m, n // N_DEV), out_dtypes[0])]
    else:
        out_specs = [tile for _ in out_dtypes]
        out_shape = [jax.ShapeDtypeStruct((m, n), dt) for dt in out_dtypes]
    n_ex = len(extras)

    def body(a_ref, b_ref, *refs):
        acc = _dot(a_ref[...].astype(BF16), b_ref[...].astype(BF16), _DIMS[mode])
        vals = (acc,) if epi is None else epi(acc, *[r[...] for r in refs[:n_ex]])
        for o_ref, val in zip(refs[n_ex:], vals):
            o_ref[...] = val.astype(o_ref.dtype)

    out = pl.pallas_call(
        body, grid=(m // tm, n // tn), in_specs=[a_spec, b_spec] + [tile] * n_ex, out_specs=out_specs,
        out_shape=out_shape, compiler_params=_params("parallel", "parallel"), name=name)(a, b, *extras)
    return out[0] if len(out) == 1 else out


def _rope_tables(pos, name):
    s = pos.shape[0]
    half = MLA_ROPE // 2
    inv_freq = jnp.asarray(np.power(np.float32(ROPE_BASE), -np.arange(0, MLA_ROPE, 2, dtype=np.float32) / MLA_ROPE)
                           .astype(np.float32).reshape(1, half))

    def body(p_ref, f_ref, c_ref, s_ref):
        ang = p_ref[...].astype(F32) * f_ref[...]
        c_ref[...] = jnp.cos(ang)
        s_ref[...] = jnp.sin(ang)

    return pl.pallas_call(
        body, out_shape=(jax.ShapeDtypeStruct((s, half), F32), jax.ShapeDtypeStruct((s, half), F32)), name=name)(pos, inv_freq)


def _lb_softmax(logits):
    m = jnp.max(logits, axis=0, keepdims=True)
    e = jnp.exp(logits - m)
    return e / jnp.sum(e, axis=0, keepdims=True)


def _lb_fwd(logits, name):
    def body(l_ref, o_ref):
        p = _lb_softmax(l_ref[...])
        acc = jnp.zeros_like(p[0:1])
        o_ref[0:1, :] = acc
        for layer in range(1, DEPTH):
            acc = acc + p[layer:layer + 1]
            o_ref[layer:layer + 1, :] = acc

    return pl.pallas_call(body, out_shape=jax.ShapeDtypeStruct(logits.shape, F32), name=name)(logits)


def _lb_bwd(logits, dlb, name):
    def body(l_ref, d_ref, o_ref):
        p = _lb_softmax(l_ref[...])
        d = d_ref[...]
        dp = [jnp.zeros_like(d[0:1])] * DEPTH
        run = jnp.zeros_like(d[0:1])
        for layer in range(DEPTH - 1, 0, -1):
            run = run + d[layer:layer + 1]
            dp[layer] = run
        inner = sum(p[layer:layer + 1] * dp[layer] for layer in range(DEPTH))
        for layer in range(DEPTH):
            o_ref[layer:layer + 1, :] = p[layer:layer + 1] * (dp[layer] - inner)

    return pl.pallas_call(body, out_shape=jax.ShapeDtypeStruct(logits.shape, F32), name=name)(logits, dlb)


def _loss(y, target, name):
    s, d = y.shape
    ts = _row_tile(s, 512)

    def body(y_ref, t_ref, l_ref, dy_ref):
        e = y_ref[...] - t_ref[...]
        dy_ref[...] = e / d

        @pl.when(pl.program_id(0) == 0)
        def _():
            l_ref[...] = jnp.zeros_like(l_ref)

        l_ref[...] += 0.5 * jnp.sum(jnp.mean(e * e, axis=-1, keepdims=True), axis=0, keepdims=True)

    row = pl.BlockSpec((ts, d), lambda i: (i, 0))
    return pl.pallas_call(
        body, grid=(s // ts,), in_specs=[row, row], out_specs=(pl.BlockSpec((1, 1), lambda i: (0, 0)), row),
        out_shape=(jax.ShapeDtypeStruct((1, 1), F32), jax.ShapeDtypeStruct((s, d), F32)),
        compiler_params=_params("arbitrary"), name=name)(y, target)


def _rope(t1, t2, cos, sin):
    return t1 * cos - t2 * sin, t1 * sin + t2 * cos


def _rope_bwd(d1, d2, cos, sin):
    return d1 * cos + d2 * sin, d2 * cos - d1 * sin


def _mla_qkv(proj, qn, kvn, w_uq, w_ukv, cos, sin, name):
    s = proj.shape[0]
    ts = _row_tile(s, 256)
    hh, half = MLA_HEADS, MLA_ROPE // 2

    def body(p_ref, qn_ref, kvn_ref, wq_ref, wkv_ref, c_ref, s_ref, cq_ref, ckv_ref, q_ref, k_ref, v_ref):
        p = p_ref[...]
        cq, ckv, kr = p[:, :MLA_Q_LORA], p[:, MLA_Q_LORA:MLA_Q_LORA + MLA_KV_LORA], p[:, MLA_Q_LORA + MLA_KV_LORA:]
        cqn = (cq * _rstd(cq) * qn_ref[...]).astype(BF16)
        ckvn = (ckv * _rstd(ckv) * kvn_ref[...]).astype(BF16)
        cq_ref[...] = cqn
        ckv_ref[...] = ckvn
        qe = _dot(cqn, wq_ref[...])
        kve = _dot(ckvn, wkv_ref[...])
        cos_, sin_ = c_ref[...], s_ref[...]
        k1, k2 = _rope(kr[:, :half], kr[:, half:], cos_, sin_)
        k1, k2 = k1.astype(BF16), k2.astype(BF16)
        for h in range(hh):
            b = h * MLA_QK
            q_ref[h, :, 0:MLA_NOPE] = qe[:, b:b + MLA_NOPE].astype(BF16)
            q1, q2 = _rope(qe[:, b + MLA_NOPE:b + MLA_NOPE + half], qe[:, b + MLA_NOPE + half:b + MLA_QK], cos_, sin_)
            q_ref[h, :, MLA_NOPE:MLA_NOPE + half] = q1.astype(BF16)
            q_ref[h, :, MLA_NOPE + half:MLA_QK] = q2.astype(BF16)
            b = h * (MLA_NOPE + MLA_V)
            k_ref[h, :, 0:MLA_NOPE] = kve[:, b:b + MLA_NOPE].astype(BF16)
            k_ref[h, :, MLA_NOPE:MLA_NOPE + half] = k1
            k_ref[h, :, MLA_NOPE + half:MLA_QK] = k2
            v_ref[h] = kve[:, b + MLA_NOPE:b + MLA_NOPE + MLA_V].astype(BF16)

    def row(w):
        return pl.BlockSpec((ts, w), lambda i: (i, 0))

    def full(shape):
        return pl.BlockSpec(shape, lambda i: (0,) * len(shape))

    def heads(w):
        return pl.BlockSpec((hh, ts, w), lambda i: (0, i, 0))

    return pl.pallas_call(
        body, grid=(s // ts,),
        in_specs=[row(MLA_IN), full(qn.shape), full(kvn.shape), full(w_uq.shape), full(w_ukv.shape), row(half), row(half)],
        out_specs=(row(MLA_Q_LORA), row(MLA_KV_LORA), heads(MLA_QK), heads(MLA_QK), heads(MLA_V)),
        out_shape=(jax.ShapeDtypeStruct((s, MLA_Q_LORA), BF16), jax.ShapeDtypeStruct((s, MLA_KV_LORA), BF16),
                   jax.ShapeDtypeStruct((hh, s, MLA_QK), BF16), jax.ShapeDtypeStruct((hh, s, MLA_QK), BF16),
                   jax.ShapeDtypeStruct((hh, s, MLA_V), BF16)),
        compiler_params=_params("parallel"), name=name)(proj, qn, kvn, w_uq, w_ukv, cos, sin)


def _attn_block(s):
    return _row_tile(s, 512)


def _causal_pairs(nb, kv_major):
    if kv_major:
        pairs = [(i, j) for j in range(nb) for i in range(j, nb)]
    else:
        pairs = [(i, j) for i in range(nb) for j in range(i + 1)]
    return (jnp.asarray(np.array([p[0] for p in pairs], np.int32)), jnp.asarray(np.array([p[1] for p in pairs], np.int32)))


def _ride(rider, step, total, refs):
    if rider is None:
        return

    @pl.when(step == 0)
    def _():
        rider.start(*refs)

    @pl.when(step == (total * 3) // 5)
    def _():
        rider.middle(*refs)

    @pl.when(step == total - 1)
    def _():
        rider.finish(*refs)


def _rider_specs(rider):
    if rider is None:
        return [], [], [], [], []
    return (list(rider.operands), [_HBM] * len(rider.operands), list(rider.out_shapes), [_HBM] * len(rider.out_shapes),
            list(rider.scratch))


def _attn_fwd(q, k, v, name, rider=None):
    hh, s, _ = q.shape
    blk = _attn_block(s)
    tq = min(blk, 256)
    nb = s // blk
    it, jt = _causal_pairs(nb, kv_major=False)
    npair = int(it.shape[0])
    scale = MLA_QK ** -0.5
    c2 = scale * LOG2_E
    r_in, r_in_specs, r_out, r_out_specs, r_scr = _rider_specs(rider)
    n_rin, n_rout, n_rscr = len(r_in), len(r_out), len(r_scr)

    def body(it_ref, jt_ref, q_ref, k_ref, v_ref, *refs):
        r_refs = refs[:n_rin] + refs[n_rin + 2:n_rin + 2 + n_rout] + refs[len(refs) - n_rscr:]
        o_ref, lse_ref = refs[n_rin:n_rin + 2]
        m_scr, acc_scr, v_scr = refs[n_rin + 2 + n_rout:n_rin + 2 + n_rout + 3]
        h, t = pl.program_id(0), pl.program_id(1)
        step = h * npair + t
        _ride(rider, step, hh * npair, r_refs)
        i, j = it_ref[t], jt_ref[t]

        @pl.when(j == 0)
        def _():
            m_scr[...] = jnp.full_like(m_scr, -jnp.inf)
            acc_scr[...] = jnp.zeros_like(acc_scr)
            v_scr[:, MLA_V:] = jnp.ones((blk, MLA_V), BF16)

        def block(masked):
            kb = k_ref[...]
            v_scr[:, :MLA_V] = v_ref[...]
            vb = v_scr[...]
            for sub in range(blk // tq):
                rows = slice(sub * tq, (sub + 1) * tq)
                sc = _dot(q_ref[rows, :], kb, NT)
                if masked:
                    qpos = sub * tq + lax.broadcasted_iota(jnp.int32, (tq, blk), 0)
                    kpos = lax.broadcasted_iota(jnp.int32, (tq, blk), 1)
                    sc = jnp.where(qpos >= kpos, sc, -jnp.inf)
                m_prev = m_scr[rows, :]
                m_new = jnp.maximum(m_prev, jnp.max(sc, axis=-1, keepdims=True))
                alpha = jnp.exp2((m_prev - m_new) * c2)
                p = jnp.exp2((sc - m_new) * c2)
                acc_scr[rows, :] = alpha * acc_scr[rows, :] + _dot(p.astype(BF16), vb)
                m_scr[rows, :] = m_new

        @pl.when(j < i)
        def _():
            block(False)

        @pl.when(j == i)
        def _():
            block(True)
            acc = acc_scr[...]
            l = acc[:, MLA_V:MLA_V + 1]
            o_ref[...] = acc[:, :MLA_V] / l
            lse_ref[...] = m_scr[...] * scale + jnp.log(l)

    grid_spec = pltpu.PrefetchScalarGridSpec(
        num_scalar_prefetch=2, grid=(hh, npair),
        in_specs=[pl.BlockSpec((None, blk, MLA_QK), lambda h, t, it_, jt_: (h, it_[t], 0)),
                  pl.BlockSpec((None, blk, MLA_QK), lambda h, t, it_, jt_: (h, jt_[t], 0)),
                  pl.BlockSpec((None, blk, MLA_V), lambda h, t, it_, jt_: (h, jt_[t], 0))] + r_in_specs,
        out_specs=[pl.BlockSpec((blk, MLA_V), lambda h, t, it_, jt_: (it_[t], h)),
                   pl.BlockSpec((None, blk, 1), lambda h, t, it_, jt_: (h, it_[t], 0))] + r_out_specs,
        scratch_shapes=[pltpu.VMEM((blk, 1), F32), pltpu.VMEM((blk, 2 * MLA_V), F32),
                        pltpu.VMEM((blk, 2 * MLA_V), BF16)] + r_scr)
    out = pl.pallas_call(
        body, grid_spec=grid_spec,
        out_shape=[jax.ShapeDtypeStruct((s, hh * MLA_V), F32), jax.ShapeDtypeStruct((hh, s, 1), F32)] + r_out,
        compiler_params=_params("arbitrary", "arbitrary"), name=name)(it, jt, q, k, v, *r_in)
    return out[0], out[1], out[2:]


def _attn_delta(do, o, name):
    s = do.shape[0]
    ts = _row_tile(s, 512)
    hh = MLA_HEADS

    def body(do_ref, o_ref, d_ref):
        prod = do_ref[...] * o_ref[...]
        for h in range(hh):
            d_ref[h] = jnp.sum(prod[:, h * MLA_V:(h + 1) * MLA_V], axis=-1, keepdims=True)

    row = pl.BlockSpec((ts, hh * MLA_V), lambda i: (i, 0))
    return pl.pallas_call(
        body, grid=(s // ts,), in_specs=[row, row], out_specs=pl.BlockSpec((hh, ts, 1), lambda i: (0, i, 0)),
        out_shape=jax.ShapeDtypeStruct((hh, s, 1), F32), compiler_params=_params("parallel"), name=name)(do, o)


def _attn_bwd(q, k, v, do, lse_row, delta_row, name, rider=None):
    hh, s, _ = q.shape
    blk = _attn_block(s)
    nb = s // blk
    it, jt = _causal_pairs(nb, kv_major=True)
    npair = int(it.shape[0])
    scale = MLA_QK ** -0.5
    c2 = scale * LOG2_E
    r_in, r_in_specs, r_out, r_out_specs, r_scr = _rider_specs(rider)
    n_rin, n_rout, n_rscr = len(r_in), len(r_out), len(r_scr)

    def body(it_ref, jt_ref, q_ref, k_ref, v_ref, do_ref, lse_ref, dl_ref, *refs):
        r_refs = refs[:n_rin] + refs[n_rin + 3:n_rin + 3 + n_rout] + refs[len(refs) - n_rscr:]
        dq_ref, dk_ref, dv_ref = refs[n_rin:n_rin + 3]
        h, t = pl.program_id(0), pl.program_id(1)
        step = h * npair + t
        _ride(rider, step, hh * npair, r_refs)
        i, j = it_ref[t], jt_ref[t]

        @pl.when(t == 0)
        def _():
            dq_ref[...] = jnp.zeros_like(dq_ref)

        def block(masked):
            qb, kb, vb = q_ref[...], k_ref[...], v_ref[...]
            dob = do_ref[...].astype(BF16)
            pt = jnp.exp2(_dot(kb, qb, NT) * c2 - lse_ref[...] * LOG2_E)
            if masked:
                kpos = lax.broadcasted_iota(jnp.int32, (blk, blk), 0)
                qpos = lax.broadcasted_iota(jnp.int32, (blk, blk), 1)
                pt = jnp.where(qpos >= kpos, pt, 0.0)
            dv = _dot(pt.astype(BF16), dob)
            dpt = _dot(vb, dob, NT)
            dst = (pt * (dpt - dl_ref[...]) * scale).astype(BF16)
            dk = _dot(dst, qb)
            if masked:
                dv_ref[...] = dv
                dk_ref[...] = dk
            else:
                dv_ref[...] += dv
                dk_ref[...] += dk
            rows = pl.ds(pl.multiple_of(i * blk, blk), blk)
            dq_ref[rows, :] += _dot(dst, kb, TN)

        @pl.when(i == j)
        def _():
            block(True)

        @pl.when(i > j)
        def _():
            block(False)

    grid_spec = pltpu.PrefetchScalarGridSpec(
        num_scalar_prefetch=2, grid=(hh, npair),
        in_specs=[pl.BlockSpec((None, blk, MLA_QK), lambda h, t, it_, jt_: (h, it_[t], 0)),
                  pl.BlockSpec((None, blk, MLA_QK), lambda h, t, it_, jt_: (h, jt_[t], 0)),
                  pl.BlockSpec((None, blk, MLA_V), lambda h, t, it_, jt_: (h, jt_[t], 0)),
                  pl.BlockSpec((blk, MLA_V), lambda h, t, it_, jt_: (it_[t], h)),
                  pl.BlockSpec((None, 1, blk), lambda h, t, it_, jt_: (h, 0, it_[t])),
                  pl.BlockSpec((None, 1, blk), lambda h, t, it_, jt_: (h, 0, it_[t]))] + r_in_specs,
        out_specs=[pl.BlockSpec((None, s, MLA_QK), lambda h, t, it_, jt_: (h, 0, 0)),
                   pl.BlockSpec((None, blk, MLA_QK), lambda h, t, it_, jt_: (h, jt_[t], 0)),
                   pl.BlockSpec((None, blk, MLA_V), lambda h, t, it_, jt_: (h, jt_[t], 0))] + r_out_specs,
        scratch_shapes=r_scr)
    out = pl.pallas_call(
        body, grid_spec=grid_spec,
        out_shape=[jax.ShapeDtypeStruct((hh, s, MLA_QK), F32), jax.ShapeDtypeStruct((hh, s, MLA_QK), F32),
                   jax.ShapeDtypeStruct((hh, s, MLA_V), F32)] + r_out,
        compiler_params=_params("arbitrary", "arbitrary"), name=name)(it, jt, q, k, v, do, lse_row, delta_row, *r_in)
    return out[0], out[1], out[2], out[3:]


def _mla_bwd_mid(dq, dk, dv, cos, sin, proj, qn, kvn, w_uq, w_ukv, name):
    s = proj.shape[0]
    ts = _row_tile(s, 256)
    hh, half = MLA_HEADS, MLA_ROPE // 2
    nq, nkv = hh * MLA_QK, hh * (MLA_NOPE + MLA_V)

    def body(dq_ref, dk_ref, dv_ref, c_ref, s_ref, p_ref, qn_ref, kvn_ref, wq_ref, wkv_ref,
             dqe_ref, dkve_ref, dp_ref, dqn_ref, dkvn_ref):
        cos_, sin_ = c_ref[...], s_ref[...]
        dkr1 = jnp.zeros((ts, half), F32)
        dkr2 = jnp.zeros((ts, half), F32)
        for h in range(hh):
            dqh, dkh = dq_ref[h], dk_ref[h]
            b = h * MLA_QK
            dqe_ref[:, b:b + MLA_NOPE] = dqh[:, :MLA_NOPE].astype(BF16)
            d1, d2 = _rope_bwd(dqh[:, MLA_NOPE:MLA_NOPE + half], dqh[:, MLA_NOPE + half:], cos_, sin_)
            dqe_ref[:, b + MLA_NOPE:b + MLA_NOPE + half] = d1.astype(BF16)
            dqe_ref[:, b + MLA_NOPE + half:b + MLA_QK] = d2.astype(BF16)
            b = h * (MLA_NOPE + MLA_V)
            dkve_ref[:, b:b + MLA_NOPE] = dkh[:, :MLA_NOPE].astype(BF16)
            dkve_ref[:, b + MLA_NOPE:b + MLA_NOPE + MLA_V] = dv_ref[h].astype(BF16)
            dkr1 = dkr1 + dkh[:, MLA_NOPE:MLA_NOPE + half]
            dkr2 = dkr2 + dkh[:, MLA_NOPE + half:]
        dkr1, dkr2 = _rope_bwd(dkr1, dkr2, cos_, sin_)
        dcqn = _dot(dqe_ref[...], wq_ref[...], NT)
        dckvn = _dot(dkve_ref[...], wkv_ref[...], NT)
        p = p_ref[...]
        dcq, dqn = _rms_bwd_rows(p[:, :MLA_Q_LORA], qn_ref[...], dcqn)
        dckv, dkvn = _rms_bwd_rows(p[:, MLA_Q_LORA:MLA_Q_LORA + MLA_KV_LORA], kvn_ref[...], dckvn)
        dp_ref[:, :MLA_Q_LORA] = dcq.astype(BF16)
        dp_ref[:, MLA_Q_LORA:MLA_Q_LORA + MLA_KV_LORA] = dckv.astype(BF16)
        dp_ref[:, MLA_Q_LORA + MLA_KV_LORA:MLA_Q_LORA + MLA_KV_LORA + half] = dkr1.astype(BF16)
        dp_ref[:, MLA_Q_LORA + MLA_KV_LORA + half:] = dkr2.astype(BF16)

        @pl.when(pl.program_id(0) == 0)
        def _():
            dqn_ref[...] = jnp.zeros_like(dqn_ref)
            dkvn_ref[...] = jnp.zeros_like(dkvn_ref)

        dqn_ref[...] += jnp.sum(dqn, axis=0, keepdims=True)
        dkvn_ref[...] += jnp.sum(dkvn, axis=0, keepdims=True)

    def row(w):
        return pl.BlockSpec((ts, w), lambda i: (i, 0))

    def full(shape):
        return pl.BlockSpec(shape, lambda i: (0,) * len(shape))

    def heads(w):
        return pl.BlockSpec((hh, ts, w), lambda i: (0, i, 0))

    return pl.pallas_call(
        body, grid=(s // ts,),
        in_specs=[heads(MLA_QK), heads(MLA_QK), heads(MLA_V), row(half), row(half), row(MLA_IN),
                  full(qn.shape), full(kvn.shape), full(w_uq.shape), full(w_ukv.shape)],
        out_specs=(row(nq), row(nkv), row(MLA_IN), full(qn.shape), full(kvn.shape)),
        out_shape=(jax.ShapeDtypeStruct((s, nq), BF16), jax.ShapeDtypeStruct((s, nkv), BF16),
                   jax.ShapeDtypeStruct((s, MLA_IN), BF16), jax.ShapeDtypeStruct(qn.shape, F32),
                   jax.ShapeDtypeStruct(kvn.shape, F32)),
        compiler_params=_params("arbitrary"), name=name)(dq, dk, dv, cos, sin, proj, qn, kvn, w_uq, w_ukv)


HGRN_TILE = 128


def _chunk_masks(t):
    r = lax.broadcasted_iota(jnp.int32, (t, t), 0)
    c = lax.broadcasted_iota(jnp.int32, (t, t), 1)
    same = (r // HGRN_CHUNK) == (c // HGRN_CHUNK)
    return r, c, same


def _hgrn_gates(p, lb):
    hk = HGRN_HEADS * HGRN_D
    qx, fx, ix, gx = p[:, :hk], p[:, hk:2 * hk], p[:, 2 * hk:3 * hk], p[:, 3 * hk:]
    sig_f = jax.nn.sigmoid(fx)
    f = lb + (1.0 - lb) * sig_f
    sig_q = jax.nn.sigmoid(qx)
    t = p.shape[0]
    r, c, same = _chunk_masks(t)
    lower = jnp.where(same & (c <= r), 1.0, 0.0).astype(F32)
    b = _dot_f32(lower, jnp.log(f))
    b3 = b.reshape(t // HGRN_CHUNK, HGRN_CHUNK, hk)
    bref = jnp.broadcast_to(b3[:, HGRN_CHUNK // 2:HGRN_CHUNK // 2 + 1, :], b3.shape).reshape(t, hk)
    blast = jnp.broadcast_to(b3[:, HGRN_CHUNK - 1:, :], b3.shape).reshape(t, hk)
    return qx, ix, gx, sig_f, f, sig_q, b, bref, blast


def _hgrn_fwd(proj, lb, onorm, name):
    s = proj.shape[0]
    t = _row_tile(s, HGRN_TILE)
    nc = t // HGRN_CHUNK
    hh, dd, hk = HGRN_HEADS, HGRN_D, HGRN_HEADS * HGRN_D

    def body(p_ref, lb_ref, on_ref, y_ref, o_ref, st_ref, st_scr):
        @pl.when(pl.program_id(0) == 0)
        def _():
            st_scr[...] = jnp.zeros_like(st_scr)

        qx, ix, gx, _, f, sig_q, b, bref, blast = _hgrn_gates(p_ref[...], lb_ref[...])
        q = qx * sig_q
        k = 1.0 - f
        r, c, same = _chunk_masks(t)
        causal = same & (c <= r)
        for h in range(hh):
            sl = slice(h * dd, (h + 1) * dd)
            bh, brefh, blasth, qh, kh = b[:, sl], bref[:, sl], blast[:, sl], q[:, sl], k[:, sl]
            vh = ix[:, sl].astype(BF16)
            q_rel = (qh * jnp.exp(bh - brefh)).astype(BF16)
            k_rel = (kh * jnp.exp(brefh - bh)).astype(BF16)
            a = jnp.where(causal, _dot(q_rel, k_rel, NT), 0.0)
            o_intra = _dot(a.astype(BF16), vh)
            q_dec = (qh * jnp.exp(bh)).astype(BF16)
            k_dec = (kh * jnp.exp(blasth - bh)).astype(BF16)
            dec = jnp.exp(blasth)
            pieces = []
            for ci in range(nc):
                rows = slice(ci * HGRN_CHUNK, (ci + 1) * HGRN_CHUNK)
                st = st_scr[h]
                st_ref[ci, h] = st
                pieces.append(_dot(q_dec[rows], st.astype(BF16), NT))
                st_scr[h] = st * dec[ci * HGRN_CHUNK:ci * HGRN_CHUNK + 1, :] + _dot(vh[rows], k_dec[rows], TN)
            oh = o_intra + jnp.concatenate(pieces, axis=0)
            o_ref[:, sl] = oh
            gate = gx[:, sl] * jax.nn.sigmoid(gx[:, sl])
            y_ref[:, sl] = (oh * _rstd(oh) * on_ref[...] * gate).astype(BF16)

    return pl.pallas_call(
        body, grid=(s // t,),
        in_specs=[pl.BlockSpec((t, 4 * hk), lambda i: (i, 0)), pl.BlockSpec((1, hk), lambda i: (0, 0)),
                  pl.BlockSpec((1, dd), lambda i: (0, 0))],
        out_specs=(pl.BlockSpec((t, hk), lambda i: (i, 0)), pl.BlockSpec((t, hk), lambda i: (i, 0)),
                   pl.BlockSpec((nc, hh, dd, dd), lambda i: (i, 0, 0, 0))),
        out_shape=(jax.ShapeDtypeStruct((s, hk), BF16), jax.ShapeDtypeStruct((s, hk), F32),
                   jax.ShapeDtypeStruct((s // HGRN_CHUNK, hh, dd, dd), F32)),
        scratch_shapes=[pltpu.VMEM((hh, dd, dd), F32)],
        compiler_params=_params("arbitrary"), name=name)(proj, lb, onorm)


def _hgrn_bwd(proj, lb, onorm, o, states, dy, name, rider=None):
    s = proj.shape[0]
    t = _row_tile(s, HGRN_TILE)
    nt = s // t
    nc = t // HGRN_CHUNK
    hh, dd, hk = HGRN_HEADS, HGRN_D, HGRN_HEADS * HGRN_D
    r_in, r_in_specs, r_out, r_out_specs, r_scr = _rider_specs(rider)
    n_rin, n_rout, n_rscr = len(r_in), len(r_out), len(r_scr)

    def body(p_ref, lb_ref, on_ref, o_ref, st_ref, dy_ref, *refs):
        r_refs = refs[:n_rin] + refs[n_rin + 3:n_rin + 3 + n_rout] + refs[len(refs) - n_rscr:]
        dp_ref, dlb_ref, don_ref = refs[n_rin:n_rin + 3]
        dst_scr, cat_scr, ext_scr, dk_scr, dq_scr = refs[n_rin + 3 + n_rout:n_rin + 3 + n_rout + 5]
        _ride(rider, pl.program_id(0), nt, r_refs)

        @pl.when(pl.program_id(0) == 0)
        def _():
            dst_scr[...] = jnp.zeros_like(dst_scr)
            dlb_ref[...] = jnp.zeros_like(dlb_ref)
            don_ref[...] = jnp.zeros_like(don_ref)

        lbv = lb_ref[...]
        qx, ix, gx, sig_f, f, sig_q, b, bref, blast = _hgrn_gates(p_ref[...], lbv)
        q = qx * sig_q
        k = 1.0 - f
        r, c, same = _chunk_masks(t)
        causal = same & (c <= r)
        on = on_ref[...]
        don = jnp.zeros((1, dd), F32)
        for h in range(hh):
            sl = slice(h * dd, (h + 1) * dd)
            oh = o_ref[:, sl]
            dyh = dy_ref[:, sl]
            gxh = gx[:, sl]
            sig_g = jax.nn.sigmoid(gxh)
            rs = _rstd(oh)
            dgate = dyh * (oh * rs * on)
            dp_ref[:, 3 * hk + h * dd:3 * hk + (h + 1) * dd] = (dgate * (sig_g * (1.0 + gxh * (1.0 - sig_g)))).astype(BF16)
            do, donh = _rms_bwd_rows(oh, on, dyh * (gxh * sig_g))
            don = don + jnp.sum(donh, axis=0, keepdims=True)
            dob = do.astype(BF16)
            bh, brefh, blasth, qh, kh = b[:, sl], bref[:, sl], blast[:, sl], q[:, sl], k[:, sl]
            vh = ix[:, sl].astype(BF16)
            e_qr, e_kr, e_qd, e_kd = jnp.exp(bh - brefh), jnp.exp(brefh - bh), jnp.exp(bh), jnp.exp(blasth - bh)
            dec = jnp.exp(blasth)
            q_rel, k_rel, q_dec, k_dec = qh * e_qr, kh * e_kr, qh * e_qd, kh * e_kd
            q_relb, k_relb, q_decb, k_decb = q_rel.astype(BF16), k_rel.astype(BF16), q_dec.astype(BF16), k_dec.astype(BF16)
            a = jnp.where(causal, _dot(q_relb, k_relb, NT), 0.0).astype(BF16)
            dv = _dot(a, dob, TN)
            da = jnp.where(causal, _dot(dob, vh, NT), 0.0).astype(BF16)
            dq_rel = _dot(da, k_relb)
            dk_rel = _dot(da, q_relb, TN)
            dq_dec, dk_dec, dv_inter, ddec = [None] * nc, [None] * nc, [None] * nc, [None] * nc
            for ci in range(nc - 1, -1, -1):
                rows = slice(ci * HGRN_CHUNK, (ci + 1) * HGRN_CHUNK)
                st = st_ref[ci, h]
                dst = dst_scr[h]
                dstb = dst.astype(BF16)
                dq_dec[ci] = _dot(dob[rows], st.astype(BF16))
                dk_dec[ci] = _dot(vh[rows], dstb)
                dv_inter[ci] = _dot(k_decb[rows], dstb, NT)
                ddec[ci] = jnp.broadcast_to(jnp.sum(dst * st, axis=0, keepdims=True), (HGRN_CHUNK, dd))
                dst_scr[h] = dst * dec[ci * HGRN_CHUNK:ci * HGRN_CHUNK + 1, :] + _dot(dob[rows], q_decb[rows], TN)
            dq_dec = jnp.concatenate(dq_dec, axis=0)
            dk_dec = jnp.concatenate(dk_dec, axis=0)
            dv = dv + jnp.concatenate(dv_inter, axis=0)
            ddec = jnp.concatenate(ddec, axis=0)
            dp_ref[:, 2 * hk + h * dd:2 * hk + (h + 1) * dd] = dv.astype(BF16)
            dq_scr[:, sl] = dq_rel * e_qr + dq_dec * e_qd
            dk_scr[:, sl] = dk_rel * e_kr + dk_dec * e_kd
            g_qr, g_kr, g_qd, g_kd = dq_rel * q_rel, dk_rel * k_rel, dq_dec * q_dec, dk_dec * k_dec
            cat_scr[0:t, sl] = g_qr - g_kr + g_qd - g_kd
            cat_scr[t:2 * t, sl] = g_kr - g_qr
            cat_scr[2 * t:3 * t, sl] = g_kd
            ext_scr[:, sl] = ddec * dec
        upper = jnp.where(same & (c >= r), 1.0, 0.0).astype(F32)
        to_ref = jnp.where(same & (r % HGRN_CHUNK <= HGRN_CHUNK // 2), 1.0, 0.0).astype(F32)
        to_all = jnp.where(same, 1.0, 0.0).astype(F32)
        dlogf = _dot_f32(jnp.concatenate([upper, to_ref, to_all], axis=1), cat_scr[...]) + ext_scr[...]
        df = dlogf / f - dk_scr[...]
        dp_ref[:, hk:2 * hk] = (df * (1.0 - lbv) * sig_f * (1.0 - sig_f)).astype(BF16)
        dp_ref[:, 0:hk] = (dq_scr[...] * (sig_q * (1.0 + qx * (1.0 - sig_q)))).astype(BF16)
        dlb_ref[...] += jnp.sum(df * (1.0 - sig_f), axis=0, keepdims=True)
        don_ref[...] += don

    def rev(i):
        return nt - 1 - i

    out = pl.pallas_call(
        body, grid=(nt,),
        in_specs=[pl.BlockSpec((t, 4 * hk), lambda i: (rev(i), 0)), pl.BlockSpec((1, hk), lambda i: (0, 0)),
                  pl.BlockSpec((1, dd), lambda i: (0, 0)), pl.BlockSpec((t, hk), lambda i: (rev(i), 0)),
                  pl.BlockSpec((nc, hh, dd, dd), lambda i: (rev(i), 0, 0, 0)),
                  pl.BlockSpec((t, hk), lambda i: (rev(i), 0))] + r_in_specs,
        out_specs=[pl.BlockSpec((t, 4 * hk), lambda i: (rev(i), 0)), pl.BlockSpec((1, hk), lambda i: (0, 0)),
                   pl.BlockSpec((1, dd), lambda i: (0, 0))] + r_out_specs,
        out_shape=[jax.ShapeDtypeStruct((s, 4 * hk), BF16), jax.ShapeDtypeStruct((1, hk), F32),
                   jax.ShapeDtypeStruct((1, dd), F32)] + r_out,
        scratch_shapes=[pltpu.VMEM((hh, dd, dd), F32), pltpu.VMEM((3 * t, hk), F32), pltpu.VMEM((t, hk), F32),
                        pltpu.VMEM((t, hk), F32), pltpu.VMEM((t, hk), F32)] + r_scr,
        compiler_params=_params("arbitrary"), name=name)(proj, lb, onorm, o, states, dy, *r_in)
    return out[0], out[1], out[2], out[3:]


def _adamw(w, g, m, v, name):
    rows, cols = w.shape
    tr = rows
    for cand in (256, 128, 64, 32, 16, 8):
        if rows % cand == 0 and rows > cand:
            tr = cand
            break

    def body(w_ref, g_ref, m_ref, v_ref, d_ref, nm_ref, nv_ref):
        gg = g_ref[...]
        nm = ADAM_B1 * m_ref[...] + (1.0 - ADAM_B1) * gg
        nv = ADAM_B2 * v_ref[...] + (1.0 - ADAM_B2) * (gg * gg)
        m_hat = nm / (1.0 - ADAM_B1 ** ADAM_STEP)
        v_hat = nv / (1.0 - ADAM_B2 ** ADAM_STEP)
        d_ref[...] = -ADAM_LR * (m_hat / (jnp.sqrt(v_hat) + ADAM_EPS) + ADAM_WD * w_ref[...])
        nm_ref[...] = nm
        nv_ref[...] = nv

    blk = pl.BlockSpec((tr, cols), lambda i: (i, 0))
    shp = jax.ShapeDtypeStruct((rows, cols), F32)
    return pl.pallas_call(
        body, grid=(rows // tr,), in_specs=[blk] * 4, out_specs=(blk,) * 3, out_shape=(shp,) * 3,
        compiler_params=_params("parallel"), name=name)(w, g, m, v)


_HBM = pl.BlockSpec(memory_space=pltpu.HBM)
_MESH = pl.DeviceIdType.MESH


class _GatherRide:
    def __init__(self, xs):
        self.operands = [xs]
        self.out_shapes = [jax.ShapeDtypeStruct((N_DEV,) + xs.shape, xs.dtype)]
        self.scratch = [pltpu.SemaphoreType.DMA((7,)), pltpu.SemaphoreType.DMA((7,)), pltpu.SemaphoreType.DMA]

    @staticmethod
    def _parts(x_ref, out_ref, send_sems, recv_sems, local_sem):
        x, y, c = lax.axis_index("x"), lax.axis_index("y"), lax.axis_index("c")
        me, sibling = (x, y, c), (x, y, 1 - c)
        chips = [(1 - x, y), (x, 1 - y), (1 - x, 1 - y)]

        def slot(px, py, pc):
            return out_ref.at[4 * px + 2 * py + pc]

        def copy(k, block, to, src=None):
            return pltpu.make_async_remote_copy(
                src_ref=slot(*block) if src is None else src, dst_ref=slot(*block), send_sem=send_sems.at[k],
                recv_sem=recv_sems.at[k], device_id=to, device_id_type=_MESH)

        mine = pltpu.make_async_copy(x_ref, slot(*me), local_sem)
        first = [copy(0, me, sibling, src=x_ref)] + [copy(1 + j, me, (*chip, c), src=x_ref) for j, chip in enumerate(chips)]
        passed = [copy(4 + j, (*chip, c), sibling) for j, chip in enumerate(chips)]
        landed = [copy(1 + j, (*chip, c), me) for j, chip in enumerate(chips)]
        from_sibling = [copy(0, sibling, me)] + [copy(4 + j, (*chip, 1 - c), me) for j, chip in enumerate(chips)]
        return mine, first, passed, landed, from_sibling

    def start(self, *refs):
        mine, first, _, _, _ = self._parts(*refs)
        mine.start()
        for cp in first:
            cp.start()

    def middle(self, *refs):
        _, _, passed, landed, _ = self._parts(*refs)
        for got, fwd in zip(landed, passed):
            got.wait_recv()
            fwd.start()

    def finish(self, *refs):
        mine, first, passed, _, from_sibling = self._parts(*refs)
        for cp in from_sibling:
            cp.wait_recv()
        for cp in first + passed:
            cp.wait_send()
        mine.wait()


class _ExchangeRide:
    def __init__(self, send):
        self.operands = [send]
        self.out_shapes = [jax.ShapeDtypeStruct(send.shape, send.dtype)]
        self.scratch = [pltpu.SemaphoreType.DMA((7,)), pltpu.SemaphoreType.DMA((7,)), pltpu.SemaphoreType.DMA]

    @staticmethod
    def _parts(s_ref, land_ref, send_sems, recv_sems, local_sem):
        x, y, c = lax.axis_index("x"), lax.axis_index("y"), lax.axis_index("c")
        me = 4 * x + 2 * y + c
        own = pltpu.make_async_copy(s_ref.at[me], land_ref.at[me], local_sem)
        sends, recvs = [], []
        for rel in range(1, N_DEV):
            px = 1 - x if rel & 4 else x
            py = 1 - y if rel & 2 else y
            pc = 1 - c if rel & 1 else c
            peer = 4 * px + 2 * py + pc
            sends.append(pltpu.make_async_remote_copy(
                src_ref=s_ref.at[peer], dst_ref=land_ref.at[me], send_sem=send_sems.at[rel - 1],
                recv_sem=recv_sems.at[rel - 1], device_id=(px, py, pc), device_id_type=_MESH))
            recvs.append(pltpu.make_async_remote_copy(
                src_ref=s_ref.at[me], dst_ref=land_ref.at[peer], send_sem=send_sems.at[rel - 1],
                recv_sem=recv_sems.at[rel - 1], device_id=(px, py, pc), device_id_type=_MESH))
        return own, sends, recvs

    def start(self, *refs):
        own, sends, _ = self._parts(*refs)
        own.start()
        for cp in sends:
            cp.start()

    def middle(self, *refs):
        pass

    def finish(self, *refs):
        own, sends, recvs = self._parts(*refs)
        for cp in recvs:
            cp.wait_recv()
        for cp in sends:
            cp.wait_send()
        own.wait()


def _run_alone(rider, name):
    def body(*refs):
        rider.start(*refs)
        rider.middle(*refs)
        rider.finish(*refs)

    return pl.pallas_call(
        body, out_shape=rider.out_shapes, in_specs=[_HBM] * len(rider.operands), out_specs=[_HBM] * len(rider.out_shapes),
        scratch_shapes=rider.scratch, name=name)(*rider.operands)[0]


def _all_gather(xs, name):
    return _run_alone(_GatherRide(xs), name)


def _exchange(send, name):
    return _run_alone(_ExchangeRide(send), name)


def _sum_slots(parts, name):
    _, rows, cols = parts.shape
    tr = _divisor_tile(rows, 256, 16)

    def body(p_ref, o_ref):
        acc = p_ref[0].astype(F32)
        for slot in range(1, N_DEV):
            acc = acc + p_ref[slot].astype(F32)
        o_ref[...] = acc

    return pl.pallas_call(
        body, grid=(rows // tr,), in_specs=[pl.BlockSpec((N_DEV, tr, cols), lambda i: (0, i, 0))],
        out_specs=pl.BlockSpec((tr, cols), lambda i: (i, 0)), out_shape=jax.ShapeDtypeStruct((rows, cols), F32),
        compiler_params=_params("parallel"), name=name)(parts)


def _mlp_fwd(h, g_pre, g_post, w1, w2, tag):
    a = _rms_fwd(h, g_pre, None, BF16, f"{tag}_norm")
    u, r2 = _mm(a, w1, "nn", 512, 1024, f"{tag}_up", out_dtypes=(F32, BF16),
                epi=lambda acc: (acc, jnp.square(jnp.maximum(acc, 0.0))))
    z = _mm(r2, w2, "nn", 512, 512, f"{tag}_down")
    out = _rms_fwd(z, g_post, h, F32, f"{tag}_out")
    return out, (h, a, u, r2, z)


def _mlp_bwd(dh, saved, g_pre, g_post, w1, w2, tag):
    h, a, u, r2, z = saved
    dz, dg_post = _rms_bwd(z, g_post, dh, None, BF16, f"{tag}_dout")
    du = _mm(dz, w2, "nt", 512, 1024, f"{tag}_ddown", out_dtypes=(BF16,), extras=(u,),
             epi=lambda acc, uu: (acc * (2.0 * jnp.maximum(uu, 0.0)),))
    dw2 = _mm(r2, dz, "tn", 512, 512, f"{tag}_dw2", out_dtypes=(BF16,), shard="rows")
    dw1 = _mm(a, du, "tn", 512, 512, f"{tag}_dw1", out_dtypes=(BF16,), shard="cols")
    da = _mm(du, w1, "nt", 512, 512, f"{tag}_dup")
    dh_in, dg_pre = _rms_bwd(h, g_pre, da, dh, F32, f"{tag}_dnorm")
    return dh_in, dg_pre, dg_post, dw1, dw2


def _hgrn_layer_fwd(h, g_pre, g_post, lb, onorm, w_in, w_o, tag):
    a = _rms_fwd(h, g_pre, None, BF16, f"{tag}_norm")
    proj = _mm(a, w_in, "nn", 512, 1024, f"{tag}_in")
    y, o, states = _hgrn_fwd(proj, lb, onorm, f"{tag}_scan")
    m = _mm(y, w_o, "nn", 512, 1024, f"{tag}_o")
    out = _rms_fwd(m, g_post, h, F32, f"{tag}_out")
    return out, (h, a, proj, y, o, states, m)


def _hgrn_layer_bwd(dh, saved, g_pre, g_post, lb, onorm, w_in, w_o, tag, rider=None):
    h, a, proj, y, o, states, m = saved
    dm, dg_post = _rms_bwd(m, g_post, dh, None, BF16, f"{tag}_dout")
    dy = _mm(dm, w_o, "nt", 512, 1024, f"{tag}_do")
    dw_o = _mm(y, dm, "tn", 128, 1024, f"{tag}_dwo", out_dtypes=(BF16,), shard="rows")
    dproj, dlb, donorm, rode = _hgrn_bwd(proj, lb, onorm, o, states, dy, f"{tag}_dscan", rider)
    dw_in = _mm(a, dproj, "tn", 512, 512, f"{tag}_dwin", out_dtypes=(BF16,), shard="cols")
    da = _mm(dproj, w_in, "nt", 512, 512, f"{tag}_din")
    dh_in, dg_pre = _rms_bwd(h, g_pre, da, dh, F32, f"{tag}_dnorm")
    return dh_in, dg_pre, dg_post, dlb, donorm, dw_in, dw_o, rode


def _mla_layer_fwd(h, g_pre, g_post, cos, sin, w_in, qn, kvn, w_uq, w_ukv, w_o, tag, rider=None):
    a = _rms_fwd(h, g_pre, None, BF16, f"{tag}_norm")
    proj = _mm(a, w_in, "nn", 512, MLA_IN, f"{tag}_in")
    cqn, ckvn, q, k, v = _mla_qkv(proj, qn, kvn, w_uq, w_ukv, cos, sin, f"{tag}_qkv")
    o, lse, rode = _attn_fwd(q, k, v, f"{tag}_attn", rider)
    m = _mm(o, w_o, "nn", 512, 1024, f"{tag}_o")
    out = _rms_fwd(m, g_post, h, F32, f"{tag}_out")
    return out, (h, a, proj, cqn, ckvn, q, k, v, o, lse, m), rode


def _mla_layer_bwd(dh, saved, g_pre, g_post, cos, sin, w_in, qn, kvn, w_uq, w_ukv, w_o, tag, rider=None):
    h, a, proj, cqn, ckvn, q, k, v, o, lse, m = saved
    hh, s = q.shape[0], q.shape[1]
    dm, dg_post = _rms_bwd(m, g_post, dh, None, BF16, f"{tag}_dout")
    do = _mm(dm, w_o, "nt", 512, 1024, f"{tag}_do")
    dw_o = _mm(o, dm, "tn", 128, 1024, f"{tag}_dwo", out_dtypes=(BF16,), shard="rows")
    delta = _attn_delta(do, o, f"{tag}_delta")
    dq, dk, dv, rode = _attn_bwd(q, k, v, do, lse.reshape(hh, 1, s), delta.reshape(hh, 1, s), f"{tag}_dattn", rider)
    dqe, dkve, dproj, dqn, dkvn = _mla_bwd_mid(dq, dk, dv, cos, sin, proj, qn, kvn, w_uq, w_ukv, f"{tag}_dqkv")
    dw_uq = _mm(cqn, dqe, "tn", MLA_Q_LORA, 768, f"{tag}_dwuq", out_dtypes=(BF16,))
    dw_ukv = _mm(ckvn, dkve, "tn", MLA_KV_LORA, 256, f"{tag}_dwukv", out_dtypes=(BF16,), shard="cols")
    dw_in = _mm(a, dproj, "tn", 128, MLA_IN, f"{tag}_dwin", out_dtypes=(BF16,), shard="rows")
    da = _mm(dproj, w_in, "nt", 512, 1024, f"{tag}_din")
    dh_in, dg_pre = _rms_bwd(h, g_pre, da, dh, F32, f"{tag}_dnorm")
    dw_uq = dw_uq.reshape(MLA_Q_LORA, N_DEV, -1).transpose(1, 0, 2)
    return dh_in, dg_pre, dg_post, dqn, dkvn, dw_in, dw_uq, dw_ukv, dw_o, rode


_CUT = dict(mla_w_in="rows", mla_w_uq="cols", mla_w_ukv="cols", mla_w_o="rows", hgrn_w_in="cols", hgrn_w_o="rows",
            mlp_w1="cols", mlp_w2="rows")
_FLAT_COLS = 1024


def _unit(layer, kind):
    slot = layer // 2
    if kind == "mla":
        return [("mla_w_in", slot), ("mla_w_uq", slot), ("mla_w_ukv", slot), ("mla_w_o", slot)]
    if kind == "hgrn":
        return [("hgrn_w_in", slot), ("hgrn_w_o", slot)]
    return [("mlp_w1", layer), ("mlp_w2", layer)]


_UNITS = [_unit(layer, kind) for layer in range(DEPTH) for kind in (("mla", "hgrn")[layer % 2], "mlp")]
_GATHER_GROUPS = [[0], [1, 2, 3, 4], [5, 6, 7]]
_EXCHANGE_GROUPS = [[7], [6, 5], [4, 3], [2, 1], [0]]


def _entries(group):
    return [e for u in group for e in _UNITS[u]]


def _pack_shards(weights, group):
    flat = jnp.concatenate([weights[name][idx].astype(BF16).reshape(-1) for name, idx in _entries(group)])
    return flat.reshape(-1, _FLAT_COLS)


def _unpack_gathered(gathered, weights, group):
    flat = gathered.reshape(N_DEV, -1)
    out, off = {}, 0
    for name, idx in _entries(group):
        r, c = weights[name].shape[1:]
        blk = flat[:, off:off + r * c].reshape(N_DEV, r, c)
        off += r * c
        out[(name, idx)] = blk.reshape(N_DEV * r, c) if _CUT[name] == "rows" else blk.transpose(1, 0, 2).reshape(r, N_DEV * c)
    return out


def _pack_partials(partials, group):
    flat = jnp.concatenate([partials[e].reshape(N_DEV, -1) for e in _entries(group)], axis=1)
    return flat.reshape(N_DEV, -1, _FLAT_COLS)


def _split_summed(flat, weights, group):
    flat = flat.reshape(-1)
    out, off = {}, 0
    for name, idx in _entries(group):
        r, c = weights[name].shape[1:]
        out[(name, idx)] = flat[off:off + r * c].reshape(r, c)
        off += r * c
    return out


def _adamw_nd(w, g, m, v, name):
    shape = w.shape
    c = shape[-1]
    d, nm, nv = _adamw(w.reshape(-1, c), g.reshape(-1, c), m.reshape(-1, c), v.reshape(-1, c), name)
    return d.reshape(shape), nm.reshape(shape), nv.reshape(shape)


def kernel(x, positions, norm_gains, mla_w_in, mla_q_norm, mla_kv_norm, mla_w_uq, mla_w_ukv, mla_w_o, hgrn_w_in, hgrn_lb_logits, hgrn_o_norm, hgrn_w_o, mlp_w1, mlp_w2, loss_target, m_norm_gains, m_mla_w_in, m_mla_q_norm, m_mla_kv_norm, m_mla_w_uq, m_mla_w_ukv, m_mla_w_o, m_hgrn_w_in, m_hgrn_lb_logits, m_hgrn_o_norm, m_hgrn_w_o, m_mlp_w1, m_mlp_w2, v_norm_gains, v_mla_w_in, v_mla_q_norm, v_mla_kv_norm, v_mla_w_uq, v_mla_w_ukv, v_mla_w_o, v_hgrn_w_in, v_hgrn_lb_logits, v_hgrn_o_norm, v_hgrn_w_o, v_mlp_w1, v_mlp_w2):
    weights = dict(norm_gains=norm_gains, mla_w_in=mla_w_in, mla_q_norm=mla_q_norm, mla_kv_norm=mla_kv_norm,
                   mla_w_uq=mla_w_uq, mla_w_ukv=mla_w_ukv, mla_w_o=mla_w_o, hgrn_w_in=hgrn_w_in,
                   hgrn_lb_logits=hgrn_lb_logits, hgrn_o_norm=hgrn_o_norm, hgrn_w_o=hgrn_w_o, mlp_w1=mlp_w1, mlp_w2=mlp_w2)
    mom_m = dict(norm_gains=m_norm_gains, mla_w_in=m_mla_w_in, mla_q_norm=m_mla_q_norm, mla_kv_norm=m_mla_kv_norm,
                 mla_w_uq=m_mla_w_uq, mla_w_ukv=m_mla_w_ukv, mla_w_o=m_mla_w_o, hgrn_w_in=m_hgrn_w_in,
                 hgrn_lb_logits=m_hgrn_lb_logits, hgrn_o_norm=m_hgrn_o_norm, hgrn_w_o=m_hgrn_w_o, mlp_w1=m_mlp_w1, mlp_w2=m_mlp_w2)
    mom_v = dict(norm_gains=v_norm_gains, mla_w_in=v_mla_w_in, mla_q_norm=v_mla_q_norm, mla_kv_norm=v_mla_kv_norm,
                 mla_w_uq=v_mla_w_uq, mla_w_ukv=v_mla_w_ukv, mla_w_o=v_mla_w_o, hgrn_w_in=v_hgrn_w_in,
                 hgrn_lb_logits=v_hgrn_lb_logits, hgrn_o_norm=v_hgrn_o_norm, hgrn_w_o=v_hgrn_w_o, mlp_w1=v_mlp_w1, mlp_w2=v_mlp_w2)
    order = list(weights)
    seq = x.shape[1]
    h = x.reshape(seq, D_MODEL)
    target = loss_target.reshape(seq, D_MODEL)

    full = _unpack_gathered(_all_gather(_pack_shards(weights, _GATHER_GROUPS[0]), "gather_first"), weights, _GATHER_GROUPS[0])
    gains = _all_gather(norm_gains.reshape(DEPTH * 4, D_MODEL // N_DEV), "gather_gains")
    gains = gains.transpose(1, 0, 2).reshape(DEPTH, 4, 1, D_MODEL)
    gather_behind = {0: _GATHER_GROUPS[1], 2: _GATHER_GROUPS[2]}

    cos, sin = _rope_tables(positions.reshape(seq, 1), "rope_tables")
    lower = _lb_fwd(hgrn_lb_logits, "lower_bounds")

    def mixer_args(layer):
        slot = layer // 2
        if layer % 2 == 0:
            return (cos, sin, full[("mla_w_in", slot)], mla_q_norm[slot:slot + 1], mla_kv_norm[slot:slot + 1],
                    full[("mla_w_uq", slot)], full[("mla_w_ukv", slot)], full[("mla_w_o", slot)])
        return (lower[layer:layer + 1], hgrn_o_norm[slot:slot + 1], full[("hgrn_w_in", slot)], full[("hgrn_w_o", slot)])

    saved = []
    for layer in range(DEPTH):
        g = gains[layer]
        if layer % 2 == 0:
            group = gather_behind[layer]
            h, sv_mix, rode = _mla_layer_fwd(h, g[0], g[1], *mixer_args(layer), f"l{layer}_mla",
                                             _GatherRide(_pack_shards(weights, group)))
            full.update(_unpack_gathered(rode[0], weights, group))
        else:
            h, sv_mix = _hgrn_layer_fwd(h, g[0], g[1], *mixer_args(layer), f"l{layer}_hgrn")
        h, sv_mlp = _mlp_fwd(h, g[2], g[3], full[("mlp_w1", layer)], full[("mlp_w2", layer)], f"l{layer}_mlp")
        saved.append((sv_mix, sv_mlp))

    loss_part, dh = _loss(h, target, "loss")
    loss = lax.psum(loss_part[0, 0], AXES)

    zero_row = jnp.zeros((1, D_MODEL), F32)
    dgains = [[None] * 4 for _ in range(DEPTH)]
    dlower = [zero_row] * DEPTH
    partials, gshard = {}, {}
    dqn, dkvn, donorm = [None] * 2, [None] * 2, [None] * 2
    pending = list(_EXCHANGE_GROUPS)

    def next_exchange():
        group = pending.pop(0)
        return group, _ExchangeRide(_pack_partials(partials, group))

    def landed(group, land, tag):
        gshard.update(_split_summed(_sum_slots(land, f"sum_grads_{tag}"), weights, group))

    for layer in range(DEPTH - 1, -1, -1):
        slot = layer // 2
        g = gains[layer]
        sv_mix, sv_mlp = saved[layer]
        dh, dgains[layer][2], dgains[layer][3], partials[("mlp_w1", layer)], partials[("mlp_w2", layer)] = _mlp_bwd(
            dh, sv_mlp, g[2], g[3], full[("mlp_w1", layer)], full[("mlp_w2", layer)], f"l{layer}_mlp")
        group, rider = next_exchange()
        if layer % 2 == 0:
            (dh, dgains[layer][0], dgains[layer][1], dqn[slot], dkvn[slot], partials[("mla_w_in", slot)],
             partials[("mla_w_uq", slot)], partials[("mla_w_ukv", slot)], partials[("mla_w_o", slot)], rode) = _mla_layer_bwd(
                dh, sv_mix, g[0], g[1], *mixer_args(layer), f"l{layer}_mla", rider)
        else:
            (dh, dgains[layer][0], dgains[layer][1], dlower[layer], donorm[slot], partials[("hgrn_w_in", slot)],
             partials[("hgrn_w_o", slot)], rode) = _hgrn_layer_bwd(dh, sv_mix, g[0], g[1], *mixer_args(layer), f"l{layer}_hgrn", rider)
        landed(group, rode[0], f"l{layer}")
    group, rider = next_exchange()
    landed(group, _run_alone(rider, "exchange_last"), "last")
    grad_x = dh.reshape(x.shape)
    dlogits = _lb_bwd(hgrn_lb_logits, jnp.concatenate(dlower, axis=0), "lower_bounds_bwd")

    pad = jnp.zeros((1, D_MODEL - 2 * MLA_KV_LORA), F32)
    pad2 = jnp.zeros((1, D_MODEL - 2 * HGRN_D), F32)
    small = jnp.concatenate(
        [jnp.concatenate([gg for row in dgains for gg in row], axis=0), jnp.concatenate(dqn, axis=1),
         jnp.concatenate(dkvn + [pad], axis=1), dlogits, jnp.concatenate(donorm + [pad2], axis=1), zero_row], axis=0)
    small = _sum_slots(_all_gather(small, "gather_small_grads"), "sum_small_grads")
    me = 4 * lax.axis_index("x") + 2 * lax.axis_index("y") + lax.axis_index("c")
    n_g = DEPTH * 4
    width = D_MODEL // N_DEV
    grads = {name: jnp.stack([gshard[(name, idx)] for idx in range(weights[name].shape[0])]) for name in _CUT}
    grads["norm_gains"] = lax.dynamic_slice(small[:n_g], (0, me * width), (n_g, width)).reshape(DEPTH, 4, width)
    grads["mla_q_norm"] = small[n_g].reshape(2, MLA_Q_LORA)
    grads["mla_kv_norm"] = small[n_g + 1, :2 * MLA_KV_LORA].reshape(2, MLA_KV_LORA)
    grads["hgrn_lb_logits"] = small[n_g + 2:n_g + 2 + DEPTH]
    grads["hgrn_o_norm"] = small[n_g + 2 + DEPTH, :2 * HGRN_D].reshape(2, HGRN_D)

    deltas, new_m, new_v = {}, {}, {}
    for name in order:
        deltas[name], new_m[name], new_v[name] = _adamw_nd(weights[name], grads[name], mom_m[name], mom_v[name], f"adamw_{name}")
    return (loss, grad_x, *[grads[n] for n in order], *[deltas[n] for n in order], *[new_m[n] for n in order],
            *[new_v[n] for n in order])
```

```python
import numpy as np
import jax
import jax.numpy as jnp
from jax import lax
from jax.experimental import pallas as pl
from jax.experimental.pallas import tpu as pltpu

F32, BF16 = jnp.float32, jnp.bfloat16

N_DEV = 8
AXES = ("x", "y", "c")
D_MODEL = 1024
DEPTH = 4
MLA_HEADS = 8
MLA_Q_LORA = 512
MLA_KV_LORA = 256
MLA_NOPE = 128
MLA_ROPE = 64
MLA_V = 128
MLA_QK = MLA_NOPE + MLA_ROPE
MLA_IN = MLA_Q_LORA + MLA_KV_LORA + MLA_ROPE
ROPE_BASE = 10000.0
HGRN_HEADS = 8
HGRN_D = 128
HGRN_CHUNK = 32
D_FF = 4 * D_MODEL
EPS = 1e-6
LOG2_E = 1.4426950408889634
ADAM_LR, ADAM_B1, ADAM_B2, ADAM_EPS, ADAM_WD, ADAM_STEP = 0.001, 0.9, 0.999, 1e-08, 0.01, 10

V7X_VMEM_LIMIT_BYTES = 56 * 1024 * 1024

NN = (((1,), (0,)), ((), ()))
NT = (((1,), (1,)), ((), ()))
TN = (((0,), (0,)), ((), ()))
_DIMS = {"nn": NN, "nt": NT, "tn": TN}


def _params(*sem):
    return pltpu.CompilerParams(dimension_semantics=sem, vmem_limit_bytes=V7X_VMEM_LIMIT_BYTES)


def _dot(a, b, dims=NN):
    return lax.dot_general(a, b, dims, preferred_element_type=F32)


def _dot_f32(a, b):
    return lax.dot_general(a, b, NN, precision=lax.Precision.HIGHEST, preferred_element_type=F32)


def _rstd(x):
    return lax.rsqrt(jnp.mean(x * x, axis=-1, keepdims=True) + EPS)


def _rms_bwd_rows(x, g, dy):
    r = _rstd(x)
    xh = x * r
    dyg = dy * g
    dx = r * (dyg - xh * jnp.mean(dyg * xh, axis=-1, keepdims=True))
    return dx, dy * xh


def _row_tile(n, want):
    t = min(n, want)
    assert n % t == 0, (n, t)
    return t


def _divisor_tile(n, cap, mult):
    for t in range(min(cap, n) - min(cap, n) % mult, 0, -mult):
        if n % t == 0:
            return t
    return n


def _rms_fwd(x, g, res, out_dtype, name):
    s, d = x.shape
    ts = _row_tile(s, 512)

    def body(x_ref, g_ref, *rest):
        xf = x_ref[...]
        y = xf * _rstd(xf) * g_ref[...]
        if res is not None:
            y = rest[0][...] + y
        rest[-1][...] = y.astype(out_dtype)

    row = pl.BlockSpec((ts, d), lambda i: (i, 0))
    vec = pl.BlockSpec((1, d), lambda i: (0, 0))
    ins = [x, g] + ([res] if res is not None else [])
    return pl.pallas_call(
        body, grid=(s // ts,), in_specs=[row, vec] + ([row] if res is not None else []), out_specs=row,
        out_shape=jax.ShapeDtypeStruct((s, d), out_dtype), compiler_params=_params("parallel"), name=name)(*ins)


def _rms_bwd(x, g, dy, res, out_dtype, name):
    s, d = x.shape
    ts = _row_tile(s, 512)

    def body(x_ref, g_ref, dy_ref, *rest):
        dx_ref, dg_ref = rest[-2:]
        dx, dg = _rms_bwd_rows(x_ref[...], g_ref[...], dy_ref[...].astype(F32))
        if res is not None:
            dx = rest[0][...] + dx
        dx_ref[...] = dx.astype(out_dtype)

        @pl.when(pl.program_id(0) == 0)
        def _():
            dg_ref[...] = jnp.zeros_like(dg_ref)

        dg_ref[...] += jnp.sum(dg, axis=0, keepdims=True)

    row = pl.BlockSpec((ts, d), lambda i: (i, 0))
    vec = pl.BlockSpec((1, d), lambda i: (0, 0))
    ins = [x, g, dy] + ([res] if res is not None else [])
    return pl.pallas_call(
        body, grid=(s // ts,), in_specs=[row, vec, row] + ([row] if res is not None else []), out_specs=(row, vec),
        out_shape=(jax.ShapeDtypeStruct((s, d), out_dtype), jax.ShapeDtypeStruct((1, d), F32)),
        compiler_params=_params("arbitrary"), name=name)(*ins)


def _mm(a, b, mode, tm, tn, name, out_dtypes=(F32,), shard=None, epi=None, extras=()):
    if mode == "tn":
        k, m = a.shape
        a_spec = pl.BlockSpec((k, tm), lambda i, j: (0, i))
    else:
        m, k = a.shape
        a_spec = pl.BlockSpec((tm, k), lambda i, j: (i, 0))
    if mode == "nt":
        n = b.shape[0]
        b_spec = pl.BlockSpec((tn, k), lambda i, j: (j, 0))
    else:
        n = b.shape[1]
        b_spec = pl.BlockSpec((k, tn), lambda i, j: (0, j))
    assert m % tm == 0 and n % tn == 0, (name, m, tm, n, tn)
    tile = pl.BlockSpec((tm, tn), lambda i, j: (i, j))
    if shard == "rows":
        per = m // N_DEV // tm
        out_specs = [pl.BlockSpec((None, tm, tn), lambda i, j: (i // per, i % per, j))]
        out_shape = [jax.ShapeDtypeStruct((N_DEV, m // N_DEV, n), out_dtypes[0])]
    elif shard == "cols":
        per = n // N_DEV // tn
        out_specs = [pl.BlockSpec((None, tm, tn), lambda i, j: (j // per, i, j % per))]
        out_shape = [jax.ShapeDtypeStruct((N_DEV, m, n // N_DEV), out_dtypes[0])]
    else:
        out_specs = [tile for _ in out_dtypes]
        out_shape = [jax.ShapeDtypeStruct((m, n), dt) for dt in out_dtypes]
    n_ex = len(extras)

    def body(a_ref, b_ref, *refs):
        acc = _dot(a_ref[...].astype(BF16), b_ref[...].astype(BF16), _DIMS[mode])
        vals = (acc,) if epi is None else epi(acc, *[r[...] for r in refs[:n_ex]])
        for o_ref, val in zip(refs[n_ex:], vals):
            o_ref[...] = val.astype(o_ref.dtype)

    out = pl.pallas_call(
        body, grid=(m // tm, n // tn), in_specs=[a_spec, b_spec] + [tile] * n_ex, out_specs=out_specs,
        out_shape=out_shape, compiler_params=_params("parallel", "parallel"), name=name)(a, b, *extras)
    return out[0] if len(out) == 1 else out


def _rope_tables(pos, name):
    s = pos.shape[0]
    half = MLA_ROPE // 2
    inv_freq = jnp.asarray(np.power(np.float32(ROPE_BASE), -np.arange(0, MLA_ROPE, 2, dtype=np.float32) / MLA_ROPE)
                           .astype(np.float32).reshape(1, half))

    def body(p_ref, f_ref, c_ref, s_ref):
        ang = p_ref[...].astype(F32) * f_ref[...]
        c_ref[...] = jnp.cos(ang)
        s_ref[...] = jnp.sin(ang)

    return pl.pallas_call(
        body, out_shape=(jax.ShapeDtypeStruct((s, half), F32), jax.ShapeDtypeStruct((s, half), F32)), name=name)(pos, inv_freq)


def _lb_softmax(logits):
    m = jnp.max(logits, axis=0, keepdims=True)
    e = jnp.exp(logits - m)
    return e / jnp.sum(e, axis=0, keepdims=True)


def _lb_fwd(logits, name):
    def body(l_ref, o_ref):
        p = _lb_softmax(l_ref[...])
        acc = jnp.zeros_like(p[0:1])
        o_ref[0:1, :] = acc
        for layer in range(1, DEPTH):
            acc = acc + p[layer:layer + 1]
            o_ref[layer:layer + 1, :] = acc

    return pl.pallas_call(body, out_shape=jax.ShapeDtypeStruct(logits.shape, F32), name=name)(logits)


def _lb_bwd(logits, dlb, name):
    def body(l_ref, d_ref, o_ref):
        p = _lb_softmax(l_ref[...])
        d = d_ref[...]
        dp = [jnp.zeros_like(d[0:1])] * DEPTH
        run = jnp.zeros_like(d[0:1])
        for layer in range(DEPTH - 1, 0, -1):
            run = run + d[layer:layer + 1]
            dp[layer] = run
        inner = sum(p[layer:layer + 1] * dp[layer] for layer in range(DEPTH))
        for layer in range(DEPTH):
            o_ref[layer:layer + 1, :] = p[layer:layer + 1] * (dp[layer] - inner)

    return pl.pallas_call(body, out_shape=jax.ShapeDtypeStruct(logits.shape, F32), name=name)(logits, dlb)


def _loss(y, target, name):
    s, d = y.shape
    ts = _row_tile(s, 512)

    def body(y_ref, t_ref, l_ref, dy_ref):
        e = y_ref[...] - t_ref[...]
        dy_ref[...] = e / d

        @pl.when(pl.program_id(0) == 0)
        def _():
            l_ref[...] = jnp.zeros_like(l_ref)

        l_ref[...] += 0.5 * jnp.sum(jnp.mean(e * e, axis=-1, keepdims=True), axis=0, keepdims=True)

    row = pl.BlockSpec((ts, d), lambda i: (i, 0))
    return pl.pallas_call(
        body, grid=(s // ts,), in_specs=[row, row], out_specs=(pl.BlockSpec((1, 1), lambda i: (0, 0)), row),
        out_shape=(jax.ShapeDtypeStruct((1, 1), F32), jax.ShapeDtypeStruct((s, d), F32)),
        compiler_params=_params("arbitrary"), name=name)(y, target)


def _rope(t1, t2, cos, sin):
    return t1 * cos - t2 * sin, t1 * sin + t2 * cos


def _rope_bwd(d1, d2, cos, sin):
    return d1 * cos + d2 * sin, d2 * cos - d1 * sin


def _mla_qkv(proj, qn, kvn, w_uq, w_ukv, cos, sin, name):
    s = proj.shape[0]
    ts = _row_tile(s, 256)
    hh, half = MLA_HEADS, MLA_ROPE // 2

    def body(p_ref, qn_ref, kvn_ref, wq_ref, wkv_ref, c_ref, s_ref, cq_ref, ckv_ref, q_ref, k_ref, v_ref):
        p = p_ref[...]
        cq, ckv, kr = p[:, :MLA_Q_LORA], p[:, MLA_Q_LORA:MLA_Q_LORA + MLA_KV_LORA], p[:, MLA_Q_LORA + MLA_KV_LORA:]
        cqn = (cq * _rstd(cq) * qn_ref[...]).astype(BF16)
        ckvn = (ckv * _rstd(ckv) * kvn_ref[...]).astype(BF16)
        cq_ref[...] = cqn
        ckv_ref[...] = ckvn
        qe = _dot(cqn, wq_ref[...])
        kve = _dot(ckvn, wkv_ref[...])
        cos_, sin_ = c_ref[...], s_ref[...]
        k1, k2 = _rope(kr[:, :half], kr[:, half:], cos_, sin_)
        k1, k2 = k1.astype(BF16), k2.astype(BF16)
        for h in range(hh):
            b = h * MLA_QK
            q_ref[h, :, 0:MLA_NOPE] = qe[:, b:b + MLA_NOPE].astype(BF16)
            q1, q2 = _rope(qe[:, b + MLA_NOPE:b + MLA_NOPE + half], qe[:, b + MLA_NOPE + half:b + MLA_QK], cos_, sin_)
            q_ref[h, :, MLA_NOPE:MLA_NOPE + half] = q1.astype(BF16)
            q_ref[h, :, MLA_NOPE + half:MLA_QK] = q2.astype(BF16)
            b = h * (MLA_NOPE + MLA_V)
            k_ref[h, :, 0:MLA_NOPE] = kve[:, b:b + MLA_NOPE].astype(BF16)
            k_ref[h, :, MLA_NOPE:MLA_NOPE + half] = k1
            k_ref[h, :, MLA_NOPE + half:MLA_QK] = k2
            v_ref[h] = kve[:, b + MLA_NOPE:b + MLA_NOPE + MLA_V].astype(BF16)

    def row(w):
        return pl.BlockSpec((ts, w), lambda i: (i, 0))

    def full(shape):
        return pl.BlockSpec(shape, lambda i: (0,) * len(shape))

    def heads(w):
        return pl.BlockSpec((hh, ts, w), lambda i: (0, i, 0))

    return pl.pallas_call(
        body, grid=(s // ts,),
        in_specs=[row(MLA_IN), full(qn.shape), full(kvn.shape), full(w_uq.shape), full(w_ukv.shape), row(half), row(half)],
        out_specs=(row(MLA_Q_LORA), row(MLA_KV_LORA), heads(MLA_QK), heads(MLA_QK), heads(MLA_V)),
        out_shape=(jax.ShapeDtypeStruct((s, MLA_Q_LORA), BF16), jax.ShapeDtypeStruct((s, MLA_KV_LORA), BF16),
                   jax.ShapeDtypeStruct((hh, s, MLA_QK), BF16), jax.ShapeDtypeStruct((hh, s, MLA_QK), BF16),
                   jax.ShapeDtypeStruct((hh, s, MLA_V), BF16)),
        compiler_params=_params("parallel"), name=name)(proj, qn, kvn, w_uq, w_ukv, cos, sin)


ATTN_BLOCK = 1024
ATTN_FWD_TILE = (256, 512)
ATTN_BWD_TILE = (512, 512)


def _attn_block(s):
    return _row_tile(s, ATTN_BLOCK)


def _tile_sees(diag, q0, tq, k0, tk):
    if not diag:
        return True, False
    return k0 <= q0 + tq - 1, k0 + tk - 1 > q0


def _causal_pairs(nb, kv_major):
    if kv_major:
        pairs = [(i, j) for j in range(nb) for i in range(j, nb)]
    else:
        pairs = [(i, j) for i in range(nb) for j in range(i + 1)]
    return (jnp.asarray(np.array([p[0] for p in pairs], np.int32)), jnp.asarray(np.array([p[1] for p in pairs], np.int32)))


def _ride(rider, step, total, refs):
    if rider is None:
        return

    @pl.when(step == 0)
    def _():
        rider.start(*refs)

    @pl.when(step == (total * 3) // 5)
    def _():
        rider.middle(*refs)

    @pl.when(step == total - 1)
    def _():
        rider.finish(*refs)


def _rider_specs(rider):
    if rider is None:
        return [], [], [], [], []
    return (list(rider.operands), [_HBM] * len(rider.operands), list(rider.out_shapes), [_HBM] * len(rider.out_shapes),
            list(rider.scratch))


def _attn_fwd(q, k, v, name, rider=None):
    hh, s, _ = q.shape
    blk = _attn_block(s)
    tq, tk = min(blk, ATTN_FWD_TILE[0]), min(blk, ATTN_FWD_TILE[1])
    nb = s // blk
    it, jt = _causal_pairs(nb, kv_major=False)
    npair = int(it.shape[0])
    scale = MLA_QK ** -0.5
    c2 = scale * LOG2_E
    r_in, r_in_specs, r_out, r_out_specs, r_scr = _rider_specs(rider)
    n_rin, n_rout, n_rscr = len(r_in), len(r_out), len(r_scr)

    def body(it_ref, jt_ref, q_ref, k_ref, v_ref, *refs):
        r_refs = refs[:n_rin] + refs[n_rin + 2:n_rin + 2 + n_rout] + refs[len(refs) - n_rscr:]
        o_ref, lse_ref = refs[n_rin:n_rin + 2]
        m_scr, acc_scr, v_scr = refs[n_rin + 2 + n_rout:n_rin + 2 + n_rout + 3]
        h, t = pl.program_id(0), pl.program_id(1)
        step = h * npair + t
        _ride(rider, step, hh * npair, r_refs)
        i, j = it_ref[t], jt_ref[t]

        @pl.when(j == 0)
        def _():
            m_scr[...] = jnp.full_like(m_scr, -jnp.inf)
            acc_scr[...] = jnp.zeros_like(acc_scr)
            v_scr[:, MLA_V:] = jnp.ones((blk, MLA_V), BF16)

        def block(diag):
            v_scr[:, :MLA_V] = v_ref[...]
            for k0 in range(0, blk, tk):
                kb, vb = k_ref[k0:k0 + tk, :], v_scr[k0:k0 + tk, :]
                for q0 in range(0, blk, tq):
                    visible, needs_mask = _tile_sees(diag, q0, tq, k0, tk)
                    if not visible:
                        continue
                    rows = slice(q0, q0 + tq)
                    sc = _dot(q_ref[rows, :], kb, NT)
                    if needs_mask:
                        qpos = q0 + lax.broadcasted_iota(jnp.int32, (tq, tk), 0)
                        kpos = k0 + lax.broadcasted_iota(jnp.int32, (tq, tk), 1)
                        sc = jnp.where(qpos >= kpos, sc, -jnp.inf)
                    m_prev = m_scr[rows, :]
                    m_new = jnp.maximum(m_prev, jnp.max(sc, axis=-1, keepdims=True))
                    alpha = jnp.exp2((m_prev - m_new) * c2)
                    p = jnp.exp2((sc - m_new) * c2)
                    acc_scr[rows, :] = alpha * acc_scr[rows, :] + _dot(p.astype(BF16), vb)
                    m_scr[rows, :] = m_new

        @pl.when(j < i)
        def _():
            block(False)

        @pl.when(j == i)
        def _():
            block(True)
            acc = acc_scr[...]
            l = acc[:, MLA_V:MLA_V + 1]
            o_ref[...] = acc[:, :MLA_V] / l
            lse_ref[...] = m_scr[...] * scale + jnp.log(l)

    grid_spec = pltpu.PrefetchScalarGridSpec(
        num_scalar_prefetch=2, grid=(hh, npair),
        in_specs=[pl.BlockSpec((None, blk, MLA_QK), lambda h, t, it_, jt_: (h, it_[t], 0)),
                  pl.BlockSpec((None, blk, MLA_QK), lambda h, t, it_, jt_: (h, jt_[t], 0)),
                  pl.BlockSpec((None, blk, MLA_V), lambda h, t, it_, jt_: (h, jt_[t], 0))] + r_in_specs,
        out_specs=[pl.BlockSpec((blk, MLA_V), lambda h, t, it_, jt_: (it_[t], h)),
                   pl.BlockSpec((None, blk, 1), lambda h, t, it_, jt_: (h, it_[t], 0))] + r_out_specs,
        scratch_shapes=[pltpu.VMEM((blk, 1), F32), pltpu.VMEM((blk, 2 * MLA_V), F32),
                        pltpu.VMEM((blk, 2 * MLA_V), BF16)] + r_scr)
    out = pl.pallas_call(
        body, grid_spec=grid_spec,
        out_shape=[jax.ShapeDtypeStruct((s, hh * MLA_V), F32), jax.ShapeDtypeStruct((hh, s, 1), F32)] + r_out,
        compiler_params=_params("arbitrary", "arbitrary"), name=name)(it, jt, q, k, v, *r_in)
    return out[0], out[1], out[2:]


def _attn_delta(do, o, name):
    s = do.shape[0]
    ts = _row_tile(s, 512)
    hh = MLA_HEADS

    def body(do_ref, o_ref, d_ref):
        prod = do_ref[...] * o_ref[...]
        for h in range(hh):
            d_ref[h] = jnp.sum(prod[:, h * MLA_V:(h + 1) * MLA_V], axis=-1, keepdims=True)

    row = pl.BlockSpec((ts, hh * MLA_V), lambda i: (i, 0))
    return pl.pallas_call(
        body, grid=(s // ts,), in_specs=[row, row], out_specs=pl.BlockSpec((hh, ts, 1), lambda i: (0, i, 0)),
        out_shape=jax.ShapeDtypeStruct((hh, s, 1), F32), compiler_params=_params("parallel"), name=name)(do, o)


def _attn_bwd(q, k, v, do, lse_row, delta_row, name, rider=None):
    hh, s, _ = q.shape
    blk = _attn_block(s)
    tq, tk = min(blk, ATTN_BWD_TILE[0]), min(blk, ATTN_BWD_TILE[1])
    nb = s // blk
    it, jt = _causal_pairs(nb, kv_major=True)
    npair = int(it.shape[0])
    scale = MLA_QK ** -0.5
    c2 = scale * LOG2_E
    r_in, r_in_specs, r_out, r_out_specs, r_scr = _rider_specs(rider)
    n_rin, n_rout, n_rscr = len(r_in), len(r_out), len(r_scr)

    def body(it_ref, jt_ref, q_ref, k_ref, v_ref, do_ref, lse_ref, dl_ref, *refs):
        r_refs = refs[:n_rin] + refs[n_rin + 3:n_rin + 3 + n_rout] + refs[len(refs) - n_rscr:]
        dq_ref, dk_ref, dv_ref = refs[n_rin:n_rin + 3]
        h, t = pl.program_id(0), pl.program_id(1)
        step = h * npair + t
        _ride(rider, step, hh * npair, r_refs)
        i, j = it_ref[t], jt_ref[t]

        @pl.when(t == 0)
        def _():
            dq_ref[...] = jnp.zeros_like(dq_ref)

        def block(diag):
            if diag:
                dk_ref[...] = jnp.zeros_like(dk_ref)
                dv_ref[...] = jnp.zeros_like(dv_ref)
            for q0 in range(0, blk, tq):
                qb = q_ref[q0:q0 + tq, :]
                dob = do_ref[q0:q0 + tq, :].astype(BF16)
                lse2 = lse_ref[:, q0:q0 + tq] * LOG2_E
                dl = dl_ref[:, q0:q0 + tq]
                dq = None
                for k0 in range(0, blk, tk):
                    visible, needs_mask = _tile_sees(diag, q0, tq, k0, tk)
                    if not visible:
                        continue
                    kb, vb = k_ref[k0:k0 + tk, :], v_ref[k0:k0 + tk, :]
                    pt = jnp.exp2(_dot(kb, qb, NT) * c2 - lse2)
                    if needs_mask:
                        kpos = k0 + lax.broadcasted_iota(jnp.int32, (tk, tq), 0)
                        qpos = q0 + lax.broadcasted_iota(jnp.int32, (tk, tq), 1)
                        pt = jnp.where(qpos >= kpos, pt, 0.0)
                    dv_ref[k0:k0 + tk, :] += _dot(pt.astype(BF16), dob)
                    dpt = _dot(vb, dob, NT)
                    dst = (pt * (dpt - dl) * scale).astype(BF16)
                    dk_ref[k0:k0 + tk, :] += _dot(dst, qb)
                    part = _dot(dst, kb, TN)
                    dq = part if dq is None else dq + part
                rows = pl.ds(pl.multiple_of(i * blk + q0, tq), tq)
                dq_ref[rows, :] += dq

        @pl.when(i == j)
        def _():
            block(True)

        @pl.when(i > j)
        def _():
            block(False)

    grid_spec = pltpu.PrefetchScalarGridSpec(
        num_scalar_prefetch=2, grid=(hh, npair),
        in_specs=[pl.BlockSpec((None, blk, MLA_QK), lambda h, t, it_, jt_: (h, it_[t], 0)),
                  pl.BlockSpec((None, blk, MLA_QK), lambda h, t, it_, jt_: (h, jt_[t], 0)),
                  pl.BlockSpec((None, blk, MLA_V), lambda h, t, it_, jt_: (h, jt_[t], 0)),
                  pl.BlockSpec((blk, MLA_V), lambda h, t, it_, jt_: (it_[t], h)),
                  pl.BlockSpec((None, 1, blk), lambda h, t, it_, jt_: (h, 0, it_[t])),
                  pl.BlockSpec((None, 1, blk), lambda h, t, it_, jt_: (h, 0, it_[t]))] + r_in_specs,
        out_specs=[pl.BlockSpec((None, s, MLA_QK), lambda h, t, it_, jt_: (h, 0, 0)),
                   pl.BlockSpec((None, blk, MLA_QK), lambda h, t, it_, jt_: (h, jt_[t], 0)),
                   pl.BlockSpec((None, blk, MLA_V), lambda h, t, it_, jt_: (h, jt_[t], 0))] + r_out_specs,
        scratch_shapes=r_scr)
    out = pl.pallas_call(
        body, grid_spec=grid_spec,
        out_shape=[jax.ShapeDtypeStruct((hh, s, MLA_QK), F32), jax.ShapeDtypeStruct((hh, s, MLA_QK), F32),
                   jax.ShapeDtypeStruct((hh, s, MLA_V), F32)] + r_out,
        compiler_params=_params("arbitrary", "arbitrary"), name=name)(it, jt, q, k, v, do, lse_row, delta_row, *r_in)
    return out[0], out[1], out[2], out[3:]


def _mla_bwd_mid(dq, dk, dv, cos, sin, proj, qn, kvn, w_uq, w_ukv, name):
    s = proj.shape[0]
    ts = _row_tile(s, 256)
    hh, half = MLA_HEADS, MLA_ROPE // 2
    nq, nkv = hh * MLA_QK, hh * (MLA_NOPE + MLA_V)

    def body(dq_ref, dk_ref, dv_ref, c_ref, s_ref, p_ref, qn_ref, kvn_ref, wq_ref, wkv_ref,
             dqe_ref, dkve_ref, dp_ref, dqn_ref, dkvn_ref):
        cos_, sin_ = c_ref[...], s_ref[...]
        dkr1 = jnp.zeros((ts, half), F32)
        dkr2 = jnp.zeros((ts, half), F32)
        for h in range(hh):
            dqh, dkh = dq_ref[h], dk_ref[h]
            b = h * MLA_QK
            dqe_ref[:, b:b + MLA_NOPE] = dqh[:, :MLA_NOPE].astype(BF16)
            d1, d2 = _rope_bwd(dqh[:, MLA_NOPE:MLA_NOPE + half], dqh[:, MLA_NOPE + half:], cos_, sin_)
            dqe_ref[:, b + MLA_NOPE:b + MLA_NOPE + half] = d1.astype(BF16)
            dqe_ref[:, b + MLA_NOPE + half:b + MLA_QK] = d2.astype(BF16)
            b = h * (MLA_NOPE + MLA_V)
            dkve_ref[:, b:b + MLA_NOPE] = dkh[:, :MLA_NOPE].astype(BF16)
            dkve_ref[:, b + MLA_NOPE:b + MLA_NOPE + MLA_V] = dv_ref[h].astype(BF16)
            dkr1 = dkr1 + dkh[:, MLA_NOPE:MLA_NOPE + half]
            dkr2 = dkr2 + dkh[:, MLA_NOPE + half:]
        dkr1, dkr2 = _rope_bwd(dkr1, dkr2, cos_, sin_)
        dcqn = _dot(dqe_ref[...], wq_ref[...], NT)
        dckvn = _dot(dkve_ref[...], wkv_ref[...], NT)
        p = p_ref[...]
        dcq, dqn = _rms_bwd_rows(p[:, :MLA_Q_LORA], qn_ref[...], dcqn)
        dckv, dkvn = _rms_bwd_rows(p[:, MLA_Q_LORA:MLA_Q_LORA + MLA_KV_LORA], kvn_ref[...], dckvn)
        dp_ref[:, :MLA_Q_LORA] = dcq.astype(BF16)
        dp_ref[:, MLA_Q_LORA:MLA_Q_LORA + MLA_KV_LORA] = dckv.astype(BF16)
        dp_ref[:, MLA_Q_LORA + MLA_KV_LORA:MLA_Q_LORA + MLA_KV_LORA + half] = dkr1.astype(BF16)
        dp_ref[:, MLA_Q_LORA + MLA_KV_LORA + half:] = dkr2.astype(BF16)

        @pl.when(pl.program_id(0) == 0)
        def _():
            dqn_ref[...] = jnp.zeros_like(dqn_ref)
            dkvn_ref[...] = jnp.zeros_like(dkvn_ref)

        dqn_ref[...] += jnp.sum(dqn, axis=0, keepdims=True)
        dkvn_ref[...] += jnp.sum(dkvn, axis=0, keepdims=True)

    def row(w):
        return pl.BlockSpec((ts, w), lambda i: (i, 0))

    def full(shape):
        return pl.BlockSpec(shape, lambda i: (0,) * len(shape))

    def heads(w):
        return pl.BlockSpec((hh, ts, w), lambda i: (0, i, 0))

    return pl.pallas_call(
        body, grid=(s // ts,),
        in_specs=[heads(MLA_QK), heads(MLA_QK), heads(MLA_V), row(half), row(half), row(MLA_IN),
                  full(qn.shape), full(kvn.shape), full(w_uq.shape), full(w_ukv.shape)],
        out_specs=(row(nq), row(nkv), row(MLA_IN), full(qn.shape), full(kvn.shape)),
        out_shape=(jax.ShapeDtypeStruct((s, nq), BF16), jax.ShapeDtypeStruct((s, nkv), BF16),
                   jax.ShapeDtypeStruct((s, MLA_IN), BF16), jax.ShapeDtypeStruct(qn.shape, F32),
                   jax.ShapeDtypeStruct(kvn.shape, F32)),
        compiler_params=_params("arbitrary"), name=name)(dq, dk, dv, cos, sin, proj, qn, kvn, w_uq, w_ukv)


HGRN_TILE = 128


def _chunk_masks(t):
    r = lax.broadcasted_iota(jnp.int32, (t, t), 0)
    c = lax.broadcasted_iota(jnp.int32, (t, t), 1)
    same = (r // HGRN_CHUNK) == (c // HGRN_CHUNK)
    return r, c, same


def _hgrn_gates(p, lb):
    hk = HGRN_HEADS * HGRN_D
    qx, fx, ix, gx = p[:, :hk], p[:, hk:2 * hk], p[:, 2 * hk:3 * hk], p[:, 3 * hk:]
    sig_f = jax.nn.sigmoid(fx)
    f = lb + (1.0 - lb) * sig_f
    sig_q = jax.nn.sigmoid(qx)
    t = p.shape[0]
    r, c, same = _chunk_masks(t)
    lower = jnp.where(same & (c <= r), 1.0, 0.0).astype(F32)
    b = _dot_f32(lower, jnp.log(f))
    b3 = b.reshape(t // HGRN_CHUNK, HGRN_CHUNK, hk)
    bref = jnp.broadcast_to(b3[:, HGRN_CHUNK // 2:HGRN_CHUNK // 2 + 1, :], b3.shape).reshape(t, hk)
    blast = jnp.broadcast_to(b3[:, HGRN_CHUNK - 1:, :], b3.shape).reshape(t, hk)
    return qx, ix, gx, sig_f, f, sig_q, b, bref, blast


def _hgrn_fwd(proj, lb, onorm, name):
    s = proj.shape[0]
    t = _row_tile(s, HGRN_TILE)
    nc = t // HGRN_CHUNK
    hh, dd, hk = HGRN_HEADS, HGRN_D, HGRN_HEADS * HGRN_D

    def body(p_ref, lb_ref, on_ref, y_ref, o_ref, st_ref, st_scr):
        @pl.when(pl.program_id(0) == 0)
        def _():
            st_scr[...] = jnp.zeros_like(st_scr)

        qx, ix, gx, _, f, sig_q, b, bref, blast = _hgrn_gates(p_ref[...], lb_ref[...])
        q = qx * sig_q
        k = 1.0 - f
        r, c, same = _chunk_masks(t)
        causal = same & (c <= r)
        for h in range(hh):
            sl = slice(h * dd, (h + 1) * dd)
            bh, brefh, blasth, qh, kh = b[:, sl], bref[:, sl], blast[:, sl], q[:, sl], k[:, sl]
            vh = ix[:, sl].astype(BF16)
            q_rel = (qh * jnp.exp(bh - brefh)).astype(BF16)
            k_rel = (kh * jnp.exp(brefh - bh)).astype(BF16)
            a = jnp.where(causal, _dot(q_rel, k_rel, NT), 0.0)
            o_intra = _dot(a.astype(BF16), vh)
            q_dec = (qh * jnp.exp(bh)).astype(BF16)
            k_dec = (kh * jnp.exp(blasth - bh)).astype(BF16)
            dec = jnp.exp(blasth)
            pieces = []
            for ci in range(nc):
                rows = slice(ci * HGRN_CHUNK, (ci + 1) * HGRN_CHUNK)
                st = st_scr[h]
                st_ref[ci, h] = st
                pieces.append(_dot(q_dec[rows], st.astype(BF16), NT))
                st_scr[h] = st * dec[ci * HGRN_CHUNK:ci * HGRN_CHUNK + 1, :] + _dot(vh[rows], k_dec[rows], TN)
            oh = o_intra + jnp.concatenate(pieces, axis=0)
            o_ref[:, sl] = oh
            gate = gx[:, sl] * jax.nn.sigmoid(gx[:, sl])
            y_ref[:, sl] = (oh * _rstd(oh) * on_ref[...] * gate).astype(BF16)

    return pl.pallas_call(
        body, grid=(s // t,),
        in_specs=[pl.BlockSpec((t, 4 * hk), lambda i: (i, 0)), pl.BlockSpec((1, hk), lambda i: (0, 0)),
                  pl.BlockSpec((1, dd), lambda i: (0, 0))],
        out_specs=(pl.BlockSpec((t, hk), lambda i: (i, 0)), pl.BlockSpec((t, hk), lambda i: (i, 0)),
                   pl.BlockSpec((nc, hh, dd, dd), lambda i: (i, 0, 0, 0))),
        out_shape=(jax.ShapeDtypeStruct((s, hk), BF16), jax.ShapeDtypeStruct((s, hk), F32),
                   jax.ShapeDtypeStruct((s // HGRN_CHUNK, hh, dd, dd), F32)),
        scratch_shapes=[pltpu.VMEM((hh, dd, dd), F32)],
        compiler_params=_params("arbitrary"), name=name)(proj, lb, onorm)


def _hgrn_bwd(proj, lb, onorm, o, states, dy, name, rider=None):
    s = proj.shape[0]
    t = _row_tile(s, HGRN_TILE)
    nt = s // t
    nc = t // HGRN_CHUNK
    hh, dd, hk = HGRN_HEADS, HGRN_D, HGRN_HEADS * HGRN_D
    r_in, r_in_specs, r_out, r_out_specs, r_scr = _rider_specs(rider)
    n_rin, n_rout, n_rscr = len(r_in), len(r_out), len(r_scr)

    def body(p_ref, lb_ref, on_ref, o_ref, st_ref, dy_ref, *refs):
        r_refs = refs[:n_rin] + refs[n_rin + 3:n_rin + 3 + n_rout] + refs[len(refs) - n_rscr:]
        dp_ref, dlb_ref, don_ref = refs[n_rin:n_rin + 3]
        dst_scr, cat_scr, ext_scr, dk_scr, dq_scr = refs[n_rin + 3 + n_rout:n_rin + 3 + n_rout + 5]
        _ride(rider, pl.program_id(0), nt, r_refs)

        @pl.when(pl.program_id(0) == 0)
        def _():
            dst_scr[...] = jnp.zeros_like(dst_scr)
            dlb_ref[...] = jnp.zeros_like(dlb_ref)
            don_ref[...] = jnp.zeros_like(don_ref)

        lbv = lb_ref[...]
        qx, ix, gx, sig_f, f, sig_q, b, bref, blast = _hgrn_gates(p_ref[...], lbv)
        q = qx * sig_q
        k = 1.0 - f
        r, c, same = _chunk_masks(t)
        causal = same & (c <= r)
        on = on_ref[...]
        don = jnp.zeros((1, dd), F32)
        for h in range(hh):
            sl = slice(h * dd, (h + 1) * dd)
            oh = o_ref[:, sl]
            dyh = dy_ref[:, sl]
            gxh = gx[:, sl]
            sig_g = jax.nn.sigmoid(gxh)
            rs = _rstd(oh)
            dgate = dyh * (oh * rs * on)
            dp_ref[:, 3 * hk + h * dd:3 * hk + (h + 1) * dd] = (dgate * (sig_g * (1.0 + gxh * (1.0 - sig_g)))).astype(BF16)
            do, donh = _rms_bwd_rows(oh, on, dyh * (gxh * sig_g))
            don = don + jnp.sum(donh, axis=0, keepdims=True)
            dob = do.astype(BF16)
            bh, brefh, blasth, qh, kh = b[:, sl], bref[:, sl], blast[:, sl], q[:, sl], k[:, sl]
            vh = ix[:, sl].astype(BF16)
            e_qr, e_kr, e_qd, e_kd = jnp.exp(bh - brefh), jnp.exp(brefh - bh), jnp.exp(bh), jnp.exp(blasth - bh)
            dec = jnp.exp(blasth)
            q_rel, k_rel, q_dec, k_dec = qh * e_qr, kh * e_kr, qh * e_qd, kh * e_kd
            q_relb, k_relb, q_decb, k_decb = q_rel.astype(BF16), k_rel.astype(BF16), q_dec.astype(BF16), k_dec.astype(BF16)
            a = jnp.where(causal, _dot(q_relb, k_relb, NT), 0.0).astype(BF16)
            dv = _dot(a, dob, TN)
            da = jnp.where(causal, _dot(dob, vh, NT), 0.0).astype(BF16)
            dq_rel = _dot(da, k_relb)
            dk_rel = _dot(da, q_relb, TN)
            dq_dec, dk_dec, dv_inter, ddec = [None] * nc, [None] * nc, [None] * nc, [None] * nc
            for ci in range(nc - 1, -1, -1):
                rows = slice(ci * HGRN_CHUNK, (ci + 1) * HGRN_CHUNK)
                st = st_ref[ci, h]
                dst = dst_scr[h]
                dstb = dst.astype(BF16)
                dq_dec[ci] = _dot(dob[rows], st.astype(BF16))
                dk_dec[ci] = _dot(vh[rows], dstb)
                dv_inter[ci] = _dot(k_decb[rows], dstb, NT)
                ddec[ci] = jnp.broadcast_to(jnp.sum(dst * st, axis=0, keepdims=True), (HGRN_CHUNK, dd))
                dst_scr[h] = dst * dec[ci * HGRN_CHUNK:ci * HGRN_CHUNK + 1, :] + _dot(dob[rows], q_decb[rows], TN)
            dq_dec = jnp.concatenate(dq_dec, axis=0)
            dk_dec = jnp.concatenate(dk_dec, axis=0)
            dv = dv + jnp.concatenate(dv_inter, axis=0)
            ddec = jnp.concatenate(ddec, axis=0)
            dp_ref[:, 2 * hk + h * dd:2 * hk + (h + 1) * dd] = dv.astype(BF16)
            dq_scr[:, sl] = dq_rel * e_qr + dq_dec * e_qd
            dk_scr[:, sl] = dk_rel * e_kr + dk_dec * e_kd
            g_qr, g_kr, g_qd, g_kd = dq_rel * q_rel, dk_rel * k_rel, dq_dec * q_dec, dk_dec * k_dec
            cat_scr[0:t, sl] = g_qr - g_kr + g_qd - g_kd
            cat_scr[t:2 * t, sl] = g_kr - g_qr
            cat_scr[2 * t:3 * t, sl] = g_kd
            ext_scr[:, sl] = ddec * dec
        upper = jnp.where(same & (c >= r), 1.0, 0.0).astype(F32)
        to_ref = jnp.where(same & (r % HGRN_CHUNK <= HGRN_CHUNK // 2), 1.0, 0.0).astype(F32)
        to_all = jnp.where(same, 1.0, 0.0).astype(F32)
        dlogf = _dot_f32(jnp.concatenate([upper, to_ref, to_all], axis=1), cat_scr[...]) + ext_scr[...]
        df = dlogf / f - dk_scr[...]
        dp_ref[:, hk:2 * hk] = (df * (1.0 - lbv) * sig_f * (1.0 - sig_f)).astype(BF16)
        dp_ref[:, 0:hk] = (dq_scr[...] * (sig_q * (1.0 + qx * (1.0 - sig_q)))).astype(BF16)
        dlb_ref[...] += jnp.sum(df * (1.0 - sig_f), axis=0, keepdims=True)
        don_ref[...] += don

    def rev(i):
        return nt - 1 - i

    out = pl.pallas_call(
        body, grid=(nt,),
        in_specs=[pl.BlockSpec((t, 4 * hk), lambda i: (rev(i), 0)), pl.BlockSpec((1, hk), lambda i: (0, 0)),
                  pl.BlockSpec((1, dd), lambda i: (0, 0)), pl.BlockSpec((t, hk), lambda i: (rev(i), 0)),
                  pl.BlockSpec((nc, hh, dd, dd), lambda i: (rev(i), 0, 0, 0)),
                  pl.BlockSpec((t, hk), lambda i: (rev(i), 0))] + r_in_specs,
        out_specs=[pl.BlockSpec((t, 4 * hk), lambda i: (rev(i), 0)), pl.BlockSpec((1, hk), lambda i: (0, 0)),
                   pl.BlockSpec((1, dd), lambda i: (0, 0))] + r_out_specs,
        out_shape=[jax.ShapeDtypeStruct((s, 4 * hk), BF16), jax.ShapeDtypeStruct((1, hk), F32),
                   jax.ShapeDtypeStruct((1, dd), F32)] + r_out,
        scratch_shapes=[pltpu.VMEM((hh, dd, dd), F32), pltpu.VMEM((3 * t, hk), F32), pltpu.VMEM((t, hk), F32),
                        pltpu.VMEM((t, hk), F32), pltpu.VMEM((t, hk), F32)] + r_scr,
        compiler_params=_params("arbitrary"), name=name)(proj, lb, onorm, o, states, dy, *r_in)
    return out[0], out[1], out[2], out[3:]


def _adamw_update(w, g, m, v):
    nm = ADAM_B1 * m + (1.0 - ADAM_B1) * g
    nv = ADAM_B2 * v + (1.0 - ADAM_B2) * (g * g)
    m_hat = nm / (1.0 - ADAM_B1 ** ADAM_STEP)
    v_hat = nv / (1.0 - ADAM_B2 ** ADAM_STEP)
    return -ADAM_LR * (m_hat / (jnp.sqrt(v_hat) + ADAM_EPS) + ADAM_WD * w), nm, nv


def _adamw(w, g, m, v, name):
    rows, cols = w.shape
    tr = _divisor_tile(rows, 256, 8)

    def body(w_ref, g_ref, m_ref, v_ref, d_ref, nm_ref, nv_ref):
        d_ref[...], nm_ref[...], nv_ref[...] = _adamw_update(w_ref[...], g_ref[...], m_ref[...], v_ref[...])

    blk = pl.BlockSpec((tr, cols), lambda i: (i, 0))
    shp = jax.ShapeDtypeStruct((rows, cols), F32)
    return pl.pallas_call(
        body, grid=(rows // tr,), in_specs=[blk] * 4, out_specs=(blk,) * 3, out_shape=(shp,) * 3,
        compiler_params=_params("parallel"), name=name)(w, g, m, v)


def _adamw_layers(w, gs, m, v, name):
    ll, rows, cols = w.shape
    tr = _divisor_tile(rows, 256, 8)

    def body(w_ref, m_ref, v_ref, *refs):
        g_refs, (g_out, d_ref, nm_ref, nv_ref) = refs[:ll], refs[ll:]
        layer = pl.program_id(0)
        g = g_refs[0][...]
        for k in range(1, ll):
            g = jnp.where(layer == k, g_refs[k][...], g)
        g_out[...] = g
        d_ref[...], nm_ref[...], nv_ref[...] = _adamw_update(w_ref[...], g, m_ref[...], v_ref[...])

    stacked = pl.BlockSpec((None, tr, cols), lambda l, i: (l, i, 0))

    def one(k):
        return pl.BlockSpec((tr, cols), lambda l, i: (jnp.where(l == k, i, 0), 0))

    shp = jax.ShapeDtypeStruct(w.shape, F32)
    return pl.pallas_call(
        body, grid=(ll, rows // tr), in_specs=[stacked] * 3 + [one(k) for k in range(ll)], out_specs=(stacked,) * 4,
        out_shape=(shp,) * 4, compiler_params=_params("arbitrary", "arbitrary"), name=name)(w, m, v, *gs)


_HBM = pl.BlockSpec(memory_space=pltpu.HBM)
_MESH = pl.DeviceIdType.MESH


class _GatherRide:
    def __init__(self, xs):
        self.operands = [xs]
        self.out_shapes = [jax.ShapeDtypeStruct((N_DEV,) + xs.shape, xs.dtype)]
        self.scratch = [pltpu.SemaphoreType.DMA((7,)), pltpu.SemaphoreType.DMA((7,)), pltpu.SemaphoreType.DMA]

    @staticmethod
    def _parts(x_ref, out_ref, send_sems, recv_sems, local_sem):
        x, y, c = lax.axis_index("x"), lax.axis_index("y"), lax.axis_index("c")
        me, sibling = (x, y, c), (x, y, 1 - c)
        chips = [(1 - x, y), (x, 1 - y), (1 - x, 1 - y)]

        def slot(px, py, pc):
            return out_ref.at[4 * px + 2 * py + pc]

        def copy(k, block, to, src=None):
            return pltpu.make_async_remote_copy(
                src_ref=slot(*block) if src is None else src, dst_ref=slot(*block), send_sem=send_sems.at[k],
                recv_sem=recv_sems.at[k], device_id=to, device_id_type=_MESH)

        mine = pltpu.make_async_copy(x_ref, slot(*me), local_sem)
        first = [copy(0, me, sibling, src=x_ref)] + [copy(1 + j, me, (*chip, c), src=x_ref) for j, chip in enumerate(chips)]
        passed = [copy(4 + j, (*chip, c), sibling) for j, chip in enumerate(chips)]
        landed = [copy(1 + j, (*chip, c), me) for j, chip in enumerate(chips)]
        from_sibling = [copy(0, sibling, me)] + [copy(4 + j, (*chip, 1 - c), me) for j, chip in enumerate(chips)]
        return mine, first, passed, landed, from_sibling

    def start(self, *refs):
        mine, first, _, _, _ = self._parts(*refs)
        mine.start()
        for cp in first:
            cp.start()

    def middle(self, *refs):
        _, _, passed, landed, _ = self._parts(*refs)
        for got, fwd in zip(landed, passed):
            got.wait_recv()
            fwd.start()

    def finish(self, *refs):
        mine, first, passed, _, from_sibling = self._parts(*refs)
        for cp in from_sibling:
            cp.wait_recv()
        for cp in first + passed:
            cp.wait_send()
        mine.wait()


class _ExchangeRide:
    def __init__(self, send):
        self.operands = [send]
        self.out_shapes = [jax.ShapeDtypeStruct(send.shape, send.dtype)]
        self.scratch = [pltpu.SemaphoreType.DMA((7,)), pltpu.SemaphoreType.DMA((7,)), pltpu.SemaphoreType.DMA]

    @staticmethod
    def _parts(s_ref, land_ref, send_sems, recv_sems, local_sem):
        x, y, c = lax.axis_index("x"), lax.axis_index("y"), lax.axis_index("c")
        me = 4 * x + 2 * y + c
        own = pltpu.make_async_copy(s_ref.at[me], land_ref.at[me], local_sem)
        sends, recvs = [], []
        for rel in range(1, N_DEV):
            px = 1 - x if rel & 4 else x
            py = 1 - y if rel & 2 else y
            pc = 1 - c if rel & 1 else c
            peer = 4 * px + 2 * py + pc
            sends.append(pltpu.make_async_remote_copy(
                src_ref=s_ref.at[peer], dst_ref=land_ref.at[me], send_sem=send_sems.at[rel - 1],
                recv_sem=recv_sems.at[rel - 1], device_id=(px, py, pc), device_id_type=_MESH))
            recvs.append(pltpu.make_async_remote_copy(
                src_ref=s_ref.at[me], dst_ref=land_ref.at[peer], send_sem=send_sems.at[rel - 1],
                recv_sem=recv_sems.at[rel - 1], device_id=(px, py, pc), device_id_type=_MESH))
        return own, sends, recvs

    def start(self, *refs):
        own, sends, _ = self._parts(*refs)
        own.start()
        for cp in sends:
            cp.start()

    def middle(self, *refs):
        pass

    def finish(self, *refs):
        own, sends, recvs = self._parts(*refs)
        for cp in recvs:
            cp.wait_recv()
        for cp in sends:
            cp.wait_send()
        own.wait()


def _run_alone(rider, name):
    def body(*refs):
        rider.start(*refs)
        rider.middle(*refs)
        rider.finish(*refs)

    return pl.pallas_call(
        body, out_shape=rider.out_shapes, in_specs=[_HBM] * len(rider.operands), out_specs=[_HBM] * len(rider.out_shapes),
        scratch_shapes=rider.scratch, name=name)(*rider.operands)[0]


def _all_gather(xs, name):
    return _run_alone(_GatherRide(xs), name)


def _exchange(send, name):
    return _run_alone(_ExchangeRide(send), name)


def _sum_slots(parts, name):
    _, rows, cols = parts.shape
    tr = _divisor_tile(rows, 256, 16)

    def body(p_ref, o_ref):
        acc = p_ref[0].astype(F32)
        for slot in range(1, N_DEV):
            acc = acc + p_ref[slot].astype(F32)
        o_ref[...] = acc

    return pl.pallas_call(
        body, grid=(rows // tr,), in_specs=[pl.BlockSpec((N_DEV, tr, cols), lambda i: (0, i, 0))],
        out_specs=pl.BlockSpec((tr, cols), lambda i: (i, 0)), out_shape=jax.ShapeDtypeStruct((rows, cols), F32),
        compiler_params=_params("parallel"), name=name)(parts)


def _mlp_fwd(h, g_pre, g_post, w1, w2, tag):
    a = _rms_fwd(h, g_pre, None, BF16, f"{tag}_norm")
    u, r2 = _mm(a, w1, "nn", 512, 1024, f"{tag}_up", out_dtypes=(F32, BF16),
                epi=lambda acc: (acc, jnp.square(jnp.maximum(acc, 0.0))))
    z = _mm(r2, w2, "nn", 512, 512, f"{tag}_down")
    out = _rms_fwd(z, g_post, h, F32, f"{tag}_out")
    return out, (h, a, u, r2, z)


def _mlp_bwd(dh, saved, g_pre, g_post, w1, w2, tag):
    h, a, u, r2, z = saved
    dz, dg_post = _rms_bwd(z, g_post, dh, None, BF16, f"{tag}_dout")
    du = _mm(dz, w2, "nt", 512, 1024, f"{tag}_ddown", out_dtypes=(BF16,), extras=(u,),
             epi=lambda acc, uu: (acc * (2.0 * jnp.maximum(uu, 0.0)),))
    dw2 = _mm(r2, dz, "tn", 512, 512, f"{tag}_dw2", out_dtypes=(BF16,), shard="rows")
    dw1 = _mm(a, du, "tn", 512, 512, f"{tag}_dw1", out_dtypes=(BF16,), shard="cols")
    da = _mm(du, w1, "nt", 512, 512, f"{tag}_dup")
    dh_in, dg_pre = _rms_bwd(h, g_pre, da, dh, F32, f"{tag}_dnorm")
    return dh_in, dg_pre, dg_post, dw1, dw2


def _hgrn_layer_fwd(h, g_pre, g_post, lb, onorm, w_in, w_o, tag):
    a = _rms_fwd(h, g_pre, None, BF16, f"{tag}_norm")
    proj = _mm(a, w_in, "nn", 512, 1024, f"{tag}_in")
    y, o, states = _hgrn_fwd(proj, lb, onorm, f"{tag}_scan")
    m = _mm(y, w_o, "nn", 512, 1024, f"{tag}_o")
    out = _rms_fwd(m, g_post, h, F32, f"{tag}_out")
    return out, (h, a, proj, y, o, states, m)


def _hgrn_layer_bwd(dh, saved, g_pre, g_post, lb, onorm, w_in, w_o, tag, rider=None):
    h, a, proj, y, o, states, m = saved
    dm, dg_post = _rms_bwd(m, g_post, dh, None, BF16, f"{tag}_dout")
    dy = _mm(dm, w_o, "nt", 512, 1024, f"{tag}_do")
    dw_o = _mm(y, dm, "tn", 128, 1024, f"{tag}_dwo", out_dtypes=(BF16,), shard="rows")
    dproj, dlb, donorm, rode = _hgrn_bwd(proj, lb, onorm, o, states, dy, f"{tag}_dscan", rider)
    dw_in = _mm(a, dproj, "tn", 512, 512, f"{tag}_dwin", out_dtypes=(BF16,), shard="cols")
    da = _mm(dproj, w_in, "nt", 512, 512, f"{tag}_din")
    dh_in, dg_pre = _rms_bwd(h, g_pre, da, dh, F32, f"{tag}_dnorm")
    return dh_in, dg_pre, dg_post, dlb, donorm, dw_in, dw_o, rode


def _mla_layer_fwd(h, g_pre, g_post, cos, sin, w_in, qn, kvn, w_uq, w_ukv, w_o, tag, rider=None):
    a = _rms_fwd(h, g_pre, None, BF16, f"{tag}_norm")
    proj = _mm(a, w_in, "nn", 512, MLA_IN, f"{tag}_in")
    cqn, ckvn, q, k, v = _mla_qkv(proj, qn, kvn, w_uq, w_ukv, cos, sin, f"{tag}_qkv")
    o, lse, rode = _attn_fwd(q, k, v, f"{tag}_attn", rider)
    m = _mm(o, w_o, "nn", 512, 1024, f"{tag}_o")
    out = _rms_fwd(m, g_post, h, F32, f"{tag}_out")
    return out, (h, a, proj, cqn, ckvn, q, k, v, o, lse, m), rode


def _mla_layer_bwd(dh, saved, g_pre, g_post, cos, sin, w_in, qn, kvn, w_uq, w_ukv, w_o, tag, rider=None):
    h, a, proj, cqn, ckvn, q, k, v, o, lse, m = saved
    hh, s = q.shape[0], q.shape[1]
    dm, dg_post = _rms_bwd(m, g_post, dh, None, BF16, f"{tag}_dout")
    do = _mm(dm, w_o, "nt", 512, 1024, f"{tag}_do")
    dw_o = _mm(o, dm, "tn", 128, 1024, f"{tag}_dwo", out_dtypes=(BF16,), shard="rows")
    delta = _attn_delta(do, o, f"{tag}_delta")
    dq, dk, dv, rode = _attn_bwd(q, k, v, do, lse.reshape(hh, 1, s), delta.reshape(hh, 1, s), f"{tag}_dattn", rider)
    dqe, dkve, dproj, dqn, dkvn = _mla_bwd_mid(dq, dk, dv, cos, sin, proj, qn, kvn, w_uq, w_ukv, f"{tag}_dqkv")
    dw_uq = _mm(cqn, dqe, "tn", MLA_Q_LORA, 768, f"{tag}_dwuq", out_dtypes=(BF16,))
    dw_ukv = _mm(ckvn, dkve, "tn", MLA_KV_LORA, 256, f"{tag}_dwukv", out_dtypes=(BF16,), shard="cols")
    dw_in = _mm(a, dproj, "tn", 128, MLA_IN, f"{tag}_dwin", out_dtypes=(BF16,), shard="rows")
    da = _mm(dproj, w_in, "nt", 512, 1024, f"{tag}_din")
    dh_in, dg_pre = _rms_bwd(h, g_pre, da, dh, F32, f"{tag}_dnorm")
    dw_uq = dw_uq.reshape(MLA_Q_LORA, N_DEV, -1).transpose(1, 0, 2)
    return dh_in, dg_pre, dg_post, dqn, dkvn, dw_in, dw_uq, dw_ukv, dw_o, rode


_CUT = dict(mla_w_in="rows", mla_w_uq="cols", mla_w_ukv="cols", mla_w_o="rows", hgrn_w_in="cols", hgrn_w_o="rows",
            mlp_w1="cols", mlp_w2="rows")
_FLAT_COLS = 1024


def _unit(layer, kind):
    slot = layer // 2
    if kind == "mla":
        return [("mla_w_in", slot), ("mla_w_uq", slot), ("mla_w_ukv", slot), ("mla_w_o", slot)]
    if kind == "hgrn":
        return [("hgrn_w_in", slot), ("hgrn_w_o", slot)]
    return [("mlp_w1", layer), ("mlp_w2", layer)]


_UNITS = [_unit(layer, kind) for layer in range(DEPTH) for kind in (("mla", "hgrn")[layer % 2], "mlp")]
_GATHER_GROUPS = [[0], [1, 2, 3, 4], [5, 6, 7]]
_EXCHANGE_GROUPS = [[7], [6, 5], [4, 3], [2, 1], [0]]


def _entries(group):
    return [e for u in group for e in _UNITS[u]]


def _pack_shards(weights, group):
    flat = jnp.concatenate([weights[name][idx].astype(BF16).reshape(-1) for name, idx in _entries(group)])
    return flat.reshape(-1, _FLAT_COLS)


def _unpack_gathered(gathered, weights, group):
    flat = gathered.reshape(N_DEV, -1)
    out, off = {}, 0
    for name, idx in _entries(group):
        r, c = weights[name].shape[1:]
        blk = flat[:, off:off + r * c].reshape(N_DEV, r, c)
        off += r * c
        out[(name, idx)] = blk.reshape(N_DEV * r, c) if _CUT[name] == "rows" else blk.transpose(1, 0, 2).reshape(r, N_DEV * c)
    return out


def _pack_partials(partials, group):
    flat = jnp.concatenate([partials[e].reshape(N_DEV, -1) for e in _entries(group)], axis=1)
    return flat.reshape(N_DEV, -1, _FLAT_COLS)


def _split_summed(flat, weights, group):
    flat = flat.reshape(-1)
    out, off = {}, 0
    for name, idx in _entries(group):
        r, c = weights[name].shape[1:]
        out[(name, idx)] = flat[off:off + r * c].reshape(r, c)
        off += r * c
    return out


def _adamw_nd(w, g, m, v, name):
    shape = w.shape
    c = shape[-1]
    d, nm, nv = _adamw(w.reshape(-1, c), g.reshape(-1, c), m.reshape(-1, c), v.reshape(-1, c), name)
    return d.reshape(shape), nm.reshape(shape), nv.reshape(shape)


def kernel(x, positions, norm_gains, mla_w_in, mla_q_norm, mla_kv_norm, mla_w_uq, mla_w_ukv, mla_w_o, hgrn_w_in, hgrn_lb_logits, hgrn_o_norm, hgrn_w_o, mlp_w1, mlp_w2, loss_target, m_norm_gains, m_mla_w_in, m_mla_q_norm, m_mla_kv_norm, m_mla_w_uq, m_mla_w_ukv, m_mla_w_o, m_hgrn_w_in, m_hgrn_lb_logits, m_hgrn_o_norm, m_hgrn_w_o, m_mlp_w1, m_mlp_w2, v_norm_gains, v_mla_w_in, v_mla_q_norm, v_mla_kv_norm, v_mla_w_uq, v_mla_w_ukv, v_mla_w_o, v_hgrn_w_in, v_hgrn_lb_logits, v_hgrn_o_norm, v_hgrn_w_o, v_mlp_w1, v_mlp_w2):
    weights = dict(norm_gains=norm_gains, mla_w_in=mla_w_in, mla_q_norm=mla_q_norm, mla_kv_norm=mla_kv_norm,
                   mla_w_uq=mla_w_uq, mla_w_ukv=mla_w_ukv, mla_w_o=mla_w_o, hgrn_w_in=hgrn_w_in,
                   hgrn_lb_logits=hgrn_lb_logits, hgrn_o_norm=hgrn_o_norm, hgrn_w_o=hgrn_w_o, mlp_w1=mlp_w1, mlp_w2=mlp_w2)
    mom_m = dict(norm_gains=m_norm_gains, mla_w_in=m_mla_w_in, mla_q_norm=m_mla_q_norm, mla_kv_norm=m_mla_kv_norm,
                 mla_w_uq=m_mla_w_uq, mla_w_ukv=m_mla_w_ukv, mla_w_o=m_mla_w_o, hgrn_w_in=m_hgrn_w_in,
                 hgrn_lb_logits=m_hgrn_lb_logits, hgrn_o_norm=m_hgrn_o_norm, hgrn_w_o=m_hgrn_w_o, mlp_w1=m_mlp_w1, mlp_w2=m_mlp_w2)
    mom_v = dict(norm_gains=v_norm_gains, mla_w_in=v_mla_w_in, mla_q_norm=v_mla_q_norm, mla_kv_norm=v_mla_kv_norm,
                 mla_w_uq=v_mla_w_uq, mla_w_ukv=v_mla_w_ukv, mla_w_o=v_mla_w_o, hgrn_w_in=v_hgrn_w_in,
                 hgrn_lb_logits=v_hgrn_lb_logits, hgrn_o_norm=v_hgrn_o_norm, hgrn_w_o=v_hgrn_w_o, mlp_w1=v_mlp_w1, mlp_w2=v_mlp_w2)
    order = list(weights)
    seq = x.shape[1]
    h = x.reshape(seq, D_MODEL)
    target = loss_target.reshape(seq, D_MODEL)

    full = _unpack_gathered(_all_gather(_pack_shards(weights, _GATHER_GROUPS[0]), "gather_first"), weights, _GATHER_GROUPS[0])
    gains = _all_gather(norm_gains.reshape(DEPTH * 4, D_MODEL // N_DEV), "gather_gains")
    gains = gains.transpose(1, 0, 2).reshape(DEPTH, 4, 1, D_MODEL)
    gather_behind = {0: _GATHER_GROUPS[1], 2: _GATHER_GROUPS[2]}

    cos, sin = _rope_tables(positions.reshape(seq, 1), "rope_tables")
    lower = _lb_fwd(hgrn_lb_logits, "lower_bounds")

    def mixer_args(layer):
        slot = layer // 2
        if layer % 2 == 0:
            return (cos, sin, full[("mla_w_in", slot)], mla_q_norm[slot:slot + 1], mla_kv_norm[slot:slot + 1],
                    full[("mla_w_uq", slot)], full[("mla_w_ukv", slot)], full[("mla_w_o", slot)])
        return (lower[layer:layer + 1], hgrn_o_norm[slot:slot + 1], full[("hgrn_w_in", slot)], full[("hgrn_w_o", slot)])

    saved = []
    for layer in range(DEPTH):
        g = gains[layer]
        if layer % 2 == 0:
            group = gather_behind[layer]
            h, sv_mix, rode = _mla_layer_fwd(h, g[0], g[1], *mixer_args(layer), f"l{layer}_mla",
                                             _GatherRide(_pack_shards(weights, group)))
            full.update(_unpack_gathered(rode[0], weights, group))
        else:
            h, sv_mix = _hgrn_layer_fwd(h, g[0], g[1], *mixer_args(layer), f"l{layer}_hgrn")
        h, sv_mlp = _mlp_fwd(h, g[2], g[3], full[("mlp_w1", layer)], full[("mlp_w2", layer)], f"l{layer}_mlp")
        saved.append((sv_mix, sv_mlp))

    loss_part, dh = _loss(h, target, "loss")
    loss = lax.psum(loss_part[0, 0], AXES)

    zero_row = jnp.zeros((1, D_MODEL), F32)
    dgains = [[None] * 4 for _ in range(DEPTH)]
    dlower = [zero_row] * DEPTH
    partials, gshard = {}, {}
    dqn, dkvn, donorm = [None] * 2, [None] * 2, [None] * 2
    pending = list(_EXCHANGE_GROUPS)

    def next_exchange():
        group = pending.pop(0)
        return group, _ExchangeRide(_pack_partials(partials, group))

    def landed(group, land, tag):
        gshard.update(_split_summed(_sum_slots(land, f"sum_grads_{tag}"), weights, group))

    for layer in range(DEPTH - 1, -1, -1):
        slot = layer // 2
        g = gains[layer]
        sv_mix, sv_mlp = saved[layer]
        dh, dgains[layer][2], dgains[layer][3], partials[("mlp_w1", layer)], partials[("mlp_w2", layer)] = _mlp_bwd(
            dh, sv_mlp, g[2], g[3], full[("mlp_w1", layer)], full[("mlp_w2", layer)], f"l{layer}_mlp")
        group, rider = next_exchange()
        if layer % 2 == 0:
            (dh, dgains[layer][0], dgains[layer][1], dqn[slot], dkvn[slot], partials[("mla_w_in", slot)],
             partials[("mla_w_uq", slot)], partials[("mla_w_ukv", slot)], partials[("mla_w_o", slot)], rode) = _mla_layer_bwd(
                dh, sv_mix, g[0], g[1], *mixer_args(layer), f"l{layer}_mla", rider)
        else:
            (dh, dgains[layer][0], dgains[layer][1], dlower[layer], donorm[slot], partials[("hgrn_w_in", slot)],
             partials[("hgrn_w_o", slot)], rode) = _hgrn_layer_bwd(dh, sv_mix, g[0], g[1], *mixer_args(layer), f"l{layer}_hgrn", rider)
        landed(group, rode[0], f"l{layer}")
    group, rider = next_exchange()
    landed(group, _run_alone(rider, "exchange_last"), "last")
    grad_x = dh.reshape(x.shape)
    dlogits = _lb_bwd(hgrn_lb_logits, jnp.concatenate(dlower, axis=0), "lower_bounds_bwd")

    pad = jnp.zeros((1, D_MODEL - 2 * MLA_KV_LORA), F32)
    pad2 = jnp.zeros((1, D_MODEL - 2 * HGRN_D), F32)
    small = jnp.concatenate(
        [jnp.concatenate([gg for row in dgains for gg in row], axis=0), jnp.concatenate(dqn, axis=1),
         jnp.concatenate(dkvn + [pad], axis=1), dlogits, jnp.concatenate(donorm + [pad2], axis=1), zero_row], axis=0)
    small = _sum_slots(_all_gather(small, "gather_small_grads"), "sum_small_grads")
    me = 4 * lax.axis_index("x") + 2 * lax.axis_index("y") + lax.axis_index("c")
    n_g = DEPTH * 4
    width = D_MODEL // N_DEV
    grads = {}
    grads["norm_gains"] = lax.dynamic_slice(small[:n_g], (0, me * width), (n_g, width)).reshape(DEPTH, 4, width)
    grads["mla_q_norm"] = small[n_g].reshape(2, MLA_Q_LORA)
    grads["mla_kv_norm"] = small[n_g + 1, :2 * MLA_KV_LORA].reshape(2, MLA_KV_LORA)
    grads["hgrn_lb_logits"] = small[n_g + 2:n_g + 2 + DEPTH]
    grads["hgrn_o_norm"] = small[n_g + 2 + DEPTH, :2 * HGRN_D].reshape(2, HGRN_D)

    deltas, new_m, new_v = {}, {}, {}
    for name in order:
        if name in _CUT:
            gs = [gshard[(name, idx)] for idx in range(weights[name].shape[0])]
            grads[name], deltas[name], new_m[name], new_v[name] = _adamw_layers(
                weights[name], gs, mom_m[name], mom_v[name], f"adamw_{name}")
        else:
            deltas[name], new_m[name], new_v[name] = _adamw_nd(weights[name], grads[name], mom_m[name], mom_v[name], f"adamw_{name}")
    return (loss, grad_x, *[grads[n] for n in order], *[deltas[n] for n in order], *[new_m[n] for n in order],
            *[new_v[n] for n in order])
```

```python
import numpy as np
import jax
import jax.numpy as jnp
from jax import lax
from jax.experimental import pallas as pl
from jax.experimental.pallas import tpu as pltpu

F32, BF16 = jnp.float32, jnp.bfloat16

N_DEV = 8
AXES = ("x", "y", "c")
D_MODEL = 1024
DEPTH = 4
MLA_HEADS = 8
MLA_Q_LORA = 512
MLA_KV_LORA = 256
MLA_NOPE = 128
MLA_ROPE = 64
MLA_V = 128
MLA_QK = MLA_NOPE + MLA_ROPE
MLA_IN = MLA_Q_LORA + MLA_KV_LORA + MLA_ROPE
ROPE_BASE = 10000.0
HGRN_HEADS = 8
HGRN_D = 128
HGRN_CHUNK = 32
D_FF = 4 * D_MODEL
EPS = 1e-6
LOG2_E = 1.4426950408889634
ADAM_LR, ADAM_B1, ADAM_B2, ADAM_EPS, ADAM_WD, ADAM_STEP = 0.001, 0.9, 0.999, 1e-08, 0.01, 10

V7X_VMEM_LIMIT_BYTES = 56 * 1024 * 1024

NN = (((1,), (0,)), ((), ()))
NT = (((1,), (1,)), ((), ()))
TN = (((0,), (0,)), ((), ()))
_DIMS = {"nn": NN, "nt": NT, "tn": TN}


def _params(*sem):
    return pltpu.CompilerParams(dimension_semantics=sem, vmem_limit_bytes=V7X_VMEM_LIMIT_BYTES)


def _dot(a, b, dims=NN):
    return lax.dot_general(a, b, dims, preferred_element_type=F32)


def _dot_f32(a, b):
    return lax.dot_general(a, b, NN, precision=lax.Precision.HIGHEST, preferred_element_type=F32)


def _rstd(x):
    return lax.rsqrt(jnp.mean(x * x, axis=-1, keepdims=True) + EPS)


def _rms_bwd_rows(x, g, dy):
    r = _rstd(x)
    xh = x * r
    dyg = dy * g
    dx = r * (dyg - xh * jnp.mean(dyg * xh, axis=-1, keepdims=True))
    return dx, dy * xh


def _row_tile(n, want):
    t = min(n, want)
    assert n % t == 0, (n, t)
    return t


def _divisor_tile(n, cap, mult):
    for t in range(min(cap, n) - min(cap, n) % mult, 0, -mult):
        if n % t == 0:
            return t
    return n


def _rms_fwd(x, g, res, out_dtype, name):
    s, d = x.shape
    ts = _row_tile(s, 512)

    def body(x_ref, g_ref, *rest):
        xf = x_ref[...]
        y = xf * _rstd(xf) * g_ref[...]
        if res is not None:
            y = rest[0][...] + y
        rest[-1][...] = y.astype(out_dtype)

    row = pl.BlockSpec((ts, d), lambda i: (i, 0))
    vec = pl.BlockSpec((1, d), lambda i: (0, 0))
    ins = [x, g] + ([res] if res is not None else [])
    return pl.pallas_call(
        body, grid=(s // ts,), in_specs=[row, vec] + ([row] if res is not None else []), out_specs=row,
        out_shape=jax.ShapeDtypeStruct((s, d), out_dtype), compiler_params=_params("parallel"), name=name)(*ins)


def _rms_bwd(x, g, dy, res, out_dtype, name):
    s, d = x.shape
    ts = _row_tile(s, 512)

    def body(x_ref, g_ref, dy_ref, *rest):
        dx_ref, dg_ref = rest[-2:]
        dx, dg = _rms_bwd_rows(x_ref[...], g_ref[...], dy_ref[...].astype(F32))
        if res is not None:
            dx = rest[0][...] + dx
        dx_ref[...] = dx.astype(out_dtype)

        @pl.when(pl.program_id(0) == 0)
        def _():
            dg_ref[...] = jnp.zeros_like(dg_ref)

        dg_ref[...] += jnp.sum(dg, axis=0, keepdims=True)

    row = pl.BlockSpec((ts, d), lambda i: (i, 0))
    vec = pl.BlockSpec((1, d), lambda i: (0, 0))
    ins = [x, g, dy] + ([res] if res is not None else [])
    return pl.pallas_call(
        body, grid=(s // ts,), in_specs=[row, vec, row] + ([row] if res is not None else []), out_specs=(row, vec),
        out_shape=(jax.ShapeDtypeStruct((s, d), out_dtype), jax.ShapeDtypeStruct((1, d), F32)),
        compiler_params=_params("arbitrary"), name=name)(*ins)


def _mm(a, b, mode, tm, tn, name, out_dtypes=(F32,), shard=None, epi=None, extras=()):
    if mode == "tn":
        k, m = a.shape
        a_spec = pl.BlockSpec((k, tm), lambda i, j: (0, i))
    else:
        m, k = a.shape
        a_spec = pl.BlockSpec((tm, k), lambda i, j: (i, 0))
    if mode == "nt":
        n = b.shape[0]
        b_spec = pl.BlockSpec((tn, k), lambda i, j: (j, 0))
    else:
        n = b.shape[1]
        b_spec = pl.BlockSpec((k, tn), lambda i, j: (0, j))
    assert m % tm == 0 and n % tn == 0, (name, m, tm, n, tn)
    tile = pl.BlockSpec((tm, tn), lambda i, j: (i, j))
    if shard == "rows":
        per = m // N_DEV // tm
        out_specs = [pl.BlockSpec((None, tm, tn), lambda i, j: (i // per, i % per, j))]
        out_shape = [jax.ShapeDtypeStruct((N_DEV, m // N_DEV, n), out_dtypes[0])]
    elif shard == "cols":
        per = n // N_DEV // tn
        out_specs = [pl.BlockSpec((None, tm, tn), lambda i, j: (j // per, i, j % per))]
        out_shape = [jax.ShapeDtypeStruct((N_DEV, m, n // N_DEV), out_dtypes[0])]
    else:
        out_specs = [tile for _ in out_dtypes]
        out_shape = [jax.ShapeDtypeStruct((m, n), dt) for dt in out_dtypes]
    n_ex = len(extras)

    def body(a_ref, b_ref, *refs):
        acc = _dot(a_ref[...].astype(BF16), b_ref[...].astype(BF16), _DIMS[mode])
        vals = (acc,) if epi is None else epi(acc, *[r[...] for r in refs[:n_ex]])
        for o_ref, val in zip(refs[n_ex:], vals):
            o_ref[...] = val.astype(o_ref.dtype)

    out = pl.pallas_call(
        body, grid=(m // tm, n // tn), in_specs=[a_spec, b_spec] + [tile] * n_ex, out_specs=out_specs,
        out_shape=out_shape, compiler_params=_params("parallel", "parallel"), name=name)(a, b, *extras)
    return out[0] if len(out) == 1 else out


def _rope_tables(pos, name):
    s = pos.shape[0]
    half = MLA_ROPE // 2
    inv_freq = jnp.asarray(np.power(np.float32(ROPE_BASE), -np.arange(0, MLA_ROPE, 2, dtype=np.float32) / MLA_ROPE)
                           .astype(np.float32).reshape(1, half))

    def body(p_ref, f_ref, c_ref, s_ref):
        ang = p_ref[...].astype(F32) * f_ref[...]
        c_ref[...] = jnp.cos(ang)
        s_ref[...] = jnp.sin(ang)

    return pl.pallas_call(
        body, out_shape=(jax.ShapeDtypeStruct((s, half), F32), jax.ShapeDtypeStruct((s, half), F32)), name=name)(pos, inv_freq)


def _lb_softmax(logits):
    m = jnp.max(logits, axis=0, keepdims=True)
    e = jnp.exp(logits - m)
    return e / jnp.sum(e, axis=0, keepdims=True)


def _lb_fwd(logits, name):
    def body(l_ref, o_ref):
        p = _lb_softmax(l_ref[...])
        acc = jnp.zeros_like(p[0:1])
        o_ref[0:1, :] = acc
        for layer in range(1, DEPTH):
            acc = acc + p[layer:layer + 1]
            o_ref[layer:layer + 1, :] = acc

    return pl.pallas_call(body, out_shape=jax.ShapeDtypeStruct(logits.shape, F32), name=name)(logits)


def _lb_bwd(logits, dlb, name):
    def body(l_ref, d_ref, o_ref):
        p = _lb_softmax(l_ref[...])
        d = d_ref[...]
        dp = [jnp.zeros_like(d[0:1])] * DEPTH
        run = jnp.zeros_like(d[0:1])
        for layer in range(DEPTH - 1, 0, -1):
            run = run + d[layer:layer + 1]
            dp[layer] = run
        inner = sum(p[layer:layer + 1] * dp[layer] for layer in range(DEPTH))
        for layer in range(DEPTH):
            o_ref[layer:layer + 1, :] = p[layer:layer + 1] * (dp[layer] - inner)

    return pl.pallas_call(body, out_shape=jax.ShapeDtypeStruct(logits.shape, F32), name=name)(logits, dlb)


def _loss(y, target, name):
    s, d = y.shape
    ts = _row_tile(s, 512)

    def body(y_ref, t_ref, l_ref, dy_ref):
        e = y_ref[...] - t_ref[...]
        dy_ref[...] = e / d

        @pl.when(pl.program_id(0) == 0)
        def _():
            l_ref[...] = jnp.zeros_like(l_ref)

        l_ref[...] += 0.5 * jnp.sum(jnp.mean(e * e, axis=-1, keepdims=True), axis=0, keepdims=True)

    row = pl.BlockSpec((ts, d), lambda i: (i, 0))
    return pl.pallas_call(
        body, grid=(s // ts,), in_specs=[row, row], out_specs=(pl.BlockSpec((1, 1), lambda i: (0, 0)), row),
        out_shape=(jax.ShapeDtypeStruct((1, 1), F32), jax.ShapeDtypeStruct((s, d), F32)),
        compiler_params=_params("arbitrary"), name=name)(y, target)


def _rope(t1, t2, cos, sin):
    return t1 * cos - t2 * sin, t1 * sin + t2 * cos


def _rope_bwd(d1, d2, cos, sin):
    return d1 * cos + d2 * sin, d2 * cos - d1 * sin


def _mla_qkv(proj, qn, kvn, w_uq, w_ukv, cos, sin, name):
    s = proj.shape[0]
    ts = _row_tile(s, 256)
    hh, half = MLA_HEADS, MLA_ROPE // 2

    def body(p_ref, qn_ref, kvn_ref, wq_ref, wkv_ref, c_ref, s_ref, cq_ref, ckv_ref, q_ref, k_ref, v_ref):
        p = p_ref[...]
        cq, ckv, kr = p[:, :MLA_Q_LORA], p[:, MLA_Q_LORA:MLA_Q_LORA + MLA_KV_LORA], p[:, MLA_Q_LORA + MLA_KV_LORA:]
        cqn = (cq * _rstd(cq) * qn_ref[...]).astype(BF16)
        ckvn = (ckv * _rstd(ckv) * kvn_ref[...]).astype(BF16)
        cq_ref[...] = cqn
        ckv_ref[...] = ckvn
        qe = _dot(cqn, wq_ref[...])
        kve = _dot(ckvn, wkv_ref[...])
        cos_, sin_ = c_ref[...], s_ref[...]
        k1, k2 = _rope(kr[:, :half], kr[:, half:], cos_, sin_)
        k1, k2 = k1.astype(BF16), k2.astype(BF16)
        for h in range(hh):
            b = h * MLA_QK
            q_ref[h, :, 0:MLA_NOPE] = qe[:, b:b + MLA_NOPE].astype(BF16)
            q1, q2 = _rope(qe[:, b + MLA_NOPE:b + MLA_NOPE + half], qe[:, b + MLA_NOPE + half:b + MLA_QK], cos_, sin_)
            q_ref[h, :, MLA_NOPE:MLA_NOPE + half] = q1.astype(BF16)
            q_ref[h, :, MLA_NOPE + half:MLA_QK] = q2.astype(BF16)
            b = h * (MLA_NOPE + MLA_V)
            k_ref[h, :, 0:MLA_NOPE] = kve[:, b:b + MLA_NOPE].astype(BF16)
            k_ref[h, :, MLA_NOPE:MLA_NOPE + half] = k1
            k_ref[h, :, MLA_NOPE + half:MLA_QK] = k2
            v_ref[h] = kve[:, b + MLA_NOPE:b + MLA_NOPE + MLA_V].astype(BF16)

    def row(w):
        return pl.BlockSpec((ts, w), lambda i: (i, 0))

    def full(shape):
        return pl.BlockSpec(shape, lambda i: (0,) * len(shape))

    def heads(w):
        return pl.BlockSpec((hh, ts, w), lambda i: (0, i, 0))

    return pl.pallas_call(
        body, grid=(s // ts,),
        in_specs=[row(MLA_IN), full(qn.shape), full(kvn.shape), full(w_uq.shape), full(w_ukv.shape), row(half), row(half)],
        out_specs=(row(MLA_Q_LORA), row(MLA_KV_LORA), heads(MLA_QK), heads(MLA_QK), heads(MLA_V)),
        out_shape=(jax.ShapeDtypeStruct((s, MLA_Q_LORA), BF16), jax.ShapeDtypeStruct((s, MLA_KV_LORA), BF16),
                   jax.ShapeDtypeStruct((hh, s, MLA_QK), BF16), jax.ShapeDtypeStruct((hh, s, MLA_QK), BF16),
                   jax.ShapeDtypeStruct((hh, s, MLA_V), BF16)),
        compiler_params=_params("parallel"), name=name)(proj, qn, kvn, w_uq, w_ukv, cos, sin)


ATTN_BLOCK = 1024
ATTN_FWD_TILE = (256, 512)
ATTN_BWD_TILE = (512, 512)


def _attn_block(s):
    return _row_tile(s, ATTN_BLOCK)


def _tile_sees(diag, q0, tq, k0, tk):
    if not diag:
        return True, False
    return k0 <= q0 + tq - 1, k0 + tk - 1 > q0


def _causal_pairs(nb, kv_major):
    if kv_major:
        pairs = [(i, j) for j in range(nb) for i in range(j, nb)]
    else:
        pairs = [(i, j) for i in range(nb) for j in range(i + 1)]
    return (jnp.asarray(np.array([p[0] for p in pairs], np.int32)), jnp.asarray(np.array([p[1] for p in pairs], np.int32)))


def _ride(rider, step, total, refs):
    if rider is None:
        return

    @pl.when(step == 0)
    def _():
        rider.start(*refs)

    @pl.when(step == (total * 3) // 5)
    def _():
        rider.middle(*refs)

    @pl.when(step == total - 1)
    def _():
        rider.finish(*refs)


def _rider_specs(rider):
    if rider is None:
        return [], [], [], [], []
    return (list(rider.operands), [_HBM] * len(rider.operands), list(rider.out_shapes), [_HBM] * len(rider.out_shapes),
            list(rider.scratch))


def _attn_fwd(q, k, v, name, rider=None):
    hh, s, _ = q.shape
    blk = _attn_block(s)
    tq, tk = min(blk, ATTN_FWD_TILE[0]), min(blk, ATTN_FWD_TILE[1])
    nb = s // blk
    it, jt = _causal_pairs(nb, kv_major=False)
    npair = int(it.shape[0])
    scale = MLA_QK ** -0.5
    c2 = scale * LOG2_E
    r_in, r_in_specs, r_out, r_out_specs, r_scr = _rider_specs(rider)
    n_rin, n_rout, n_rscr = len(r_in), len(r_out), len(r_scr)

    def body(it_ref, jt_ref, q_ref, k_ref, v_ref, *refs):
        r_refs = refs[:n_rin] + refs[n_rin + 2:n_rin + 2 + n_rout] + refs[len(refs) - n_rscr:]
        o_ref, lse_ref = refs[n_rin:n_rin + 2]
        m_scr, acc_scr, v_scr = refs[n_rin + 2 + n_rout:n_rin + 2 + n_rout + 3]
        h, t = pl.program_id(0), pl.program_id(1)
        step = h * npair + t
        _ride(rider, step, hh * npair, r_refs)
        i, j = it_ref[t], jt_ref[t]

        @pl.when(j == 0)
        def _():
            m_scr[...] = jnp.full_like(m_scr, -jnp.inf)
            acc_scr[...] = jnp.zeros_like(acc_scr)
            v_scr[:, MLA_V:] = jnp.ones((blk, MLA_V), BF16)

        def block(diag):
            v_scr[:, :MLA_V] = v_ref[...]
            for k0 in range(0, blk, tk):
                kb, vb = k_ref[k0:k0 + tk, :], v_scr[k0:k0 + tk, :]
                for q0 in range(0, blk, tq):
                    visible, needs_mask = _tile_sees(diag, q0, tq, k0, tk)
                    if not visible:
                        continue
                    rows = slice(q0, q0 + tq)
                    sc = _dot(q_ref[rows, :], kb, NT)
                    if needs_mask:
                        qpos = q0 + lax.broadcasted_iota(jnp.int32, (tq, tk), 0)
                        kpos = k0 + lax.broadcasted_iota(jnp.int32, (tq, tk), 1)
                        sc = jnp.where(qpos >= kpos, sc, -jnp.inf)
                    m_prev = m_scr[rows, :]
                    m_new = jnp.maximum(m_prev, jnp.max(sc, axis=-1, keepdims=True))
                    alpha = jnp.exp2((m_prev - m_new) * c2)
                    p = jnp.exp2((sc - m_new) * c2)
                    acc_scr[rows, :] = alpha * acc_scr[rows, :] + _dot(p.astype(BF16), vb)
                    m_scr[rows, :] = m_new

        @pl.when(j < i)
        def _():
            block(False)

        @pl.when(j == i)
        def _():
            block(True)
            acc = acc_scr[...]
            l = acc[:, MLA_V:MLA_V + 1]
            o_ref[...] = acc[:, :MLA_V] / l
            lse_ref[...] = m_scr[...] * scale + jnp.log(l)

    grid_spec = pltpu.PrefetchScalarGridSpec(
        num_scalar_prefetch=2, grid=(hh, npair),
        in_specs=[pl.BlockSpec((None, blk, MLA_QK), lambda h, t, it_, jt_: (h, it_[t], 0)),
                  pl.BlockSpec((None, blk, MLA_QK), lambda h, t, it_, jt_: (h, jt_[t], 0)),
                  pl.BlockSpec((None, blk, MLA_V), lambda h, t, it_, jt_: (h, jt_[t], 0))] + r_in_specs,
        out_specs=[pl.BlockSpec((blk, MLA_V), lambda h, t, it_, jt_: (it_[t], h)),
                   pl.BlockSpec((None, blk, 1), lambda h, t, it_, jt_: (h, it_[t], 0))] + r_out_specs,
        scratch_shapes=[pltpu.VMEM((blk, 1), F32), pltpu.VMEM((blk, 2 * MLA_V), F32),
                        pltpu.VMEM((blk, 2 * MLA_V), BF16)] + r_scr)
    out = pl.pallas_call(
        body, grid_spec=grid_spec,
        out_shape=[jax.ShapeDtypeStruct((s, hh * MLA_V), F32), jax.ShapeDtypeStruct((hh, s, 1), F32)] + r_out,
        compiler_params=_params("arbitrary", "arbitrary"), name=name)(it, jt, q, k, v, *r_in)
    return out[0], out[1], out[2:]


def _attn_delta(do, o, name):
    s = do.shape[0]
    ts = _row_tile(s, 512)
    hh = MLA_HEADS

    def body(do_ref, o_ref, d_ref):
        prod = do_ref[...] * o_ref[...]
        for h in range(hh):
            d_ref[h] = jnp.sum(prod[:, h * MLA_V:(h + 1) * MLA_V], axis=-1, keepdims=True)

    row = pl.BlockSpec((ts, hh * MLA_V), lambda i: (i, 0))
    return pl.pallas_call(
        body, grid=(s // ts,), in_specs=[row, row], out_specs=pl.BlockSpec((hh, ts, 1), lambda i: (0, i, 0)),
        out_shape=jax.ShapeDtypeStruct((hh, s, 1), F32), compiler_params=_params("parallel"), name=name)(do, o)


def _attn_bwd(q, k, v, do, lse_row, delta_row, name, rider=None):
    hh, s, _ = q.shape
    blk = _attn_block(s)
    tq, tk = min(blk, ATTN_BWD_TILE[0]), min(blk, ATTN_BWD_TILE[1])
    nb = s // blk
    it, jt = _causal_pairs(nb, kv_major=True)
    npair = int(it.shape[0])
    scale = MLA_QK ** -0.5
    c2 = scale * LOG2_E
    r_in, r_in_specs, r_out, r_out_specs, r_scr = _rider_specs(rider)
    n_rin, n_rout, n_rscr = len(r_in), len(r_out), len(r_scr)

    def body(it_ref, jt_ref, q_ref, k_ref, v_ref, do_ref, lse_ref, dl_ref, *refs):
        r_refs = refs[:n_rin] + refs[n_rin + 3:n_rin + 3 + n_rout] + refs[len(refs) - n_rscr:]
        dq_ref, dk_ref, dv_ref = refs[n_rin:n_rin + 3]
        h, t = pl.program_id(0), pl.program_id(1)
        step = h * npair + t
        _ride(rider, step, hh * npair, r_refs)
        i, j = it_ref[t], jt_ref[t]

        @pl.when(t == 0)
        def _():
            dq_ref[...] = jnp.zeros_like(dq_ref)

        def block(diag):
            if diag:
                dk_ref[...] = jnp.zeros_like(dk_ref)
                dv_ref[...] = jnp.zeros_like(dv_ref)
            for q0 in range(0, blk, tq):
                qb = q_ref[q0:q0 + tq, :]
                dob = do_ref[q0:q0 + tq, :].astype(BF16)
                lse2 = lse_ref[:, q0:q0 + tq] * LOG2_E
                dl = dl_ref[:, q0:q0 + tq]
                dq = None
                for k0 in range(0, blk, tk):
                    visible, needs_mask = _tile_sees(diag, q0, tq, k0, tk)
                    if not visible:
                        continue
                    kb, vb = k_ref[k0:k0 + tk, :], v_ref[k0:k0 + tk, :]
                    pt = jnp.exp2(_dot(kb, qb, NT) * c2 - lse2)
                    if needs_mask:
                        kpos = k0 + lax.broadcasted_iota(jnp.int32, (tk, tq), 0)
                        qpos = q0 + lax.broadcasted_iota(jnp.int32, (tk, tq), 1)
                        pt = jnp.where(qpos >= kpos, pt, 0.0)
                    dv_ref[k0:k0 + tk, :] += _dot(pt.astype(BF16), dob)
                    dpt = _dot(vb, dob, NT)
                    dst = (pt * (dpt - dl) * scale).astype(BF16)
                    dk_ref[k0:k0 + tk, :] += _dot(dst, qb)
                    part = _dot(dst, kb, TN)
                    dq = part if dq is None else dq + part
                rows = pl.ds(pl.multiple_of(i * blk + q0, tq), tq)
                dq_ref[rows, :] += dq

        @pl.when(i == j)
        def _():
            block(True)

        @pl.when(i > j)
        def _():
            block(False)

    grid_spec = pltpu.PrefetchScalarGridSpec(
        num_scalar_prefetch=2, grid=(hh, npair),
        in_specs=[pl.BlockSpec((None, blk, MLA_QK), lambda h, t, it_, jt_: (h, it_[t], 0)),
                  pl.BlockSpec((None, blk, MLA_QK), lambda h, t, it_, jt_: (h, jt_[t], 0)),
                  pl.BlockSpec((None, blk, MLA_V), lambda h, t, it_, jt_: (h, jt_[t], 0)),
                  pl.BlockSpec((blk, MLA_V), lambda h, t, it_, jt_: (it_[t], h)),
                  pl.BlockSpec((None, 1, blk), lambda h, t, it_, jt_: (h, 0, it_[t])),
                  pl.BlockSpec((None, 1, blk), lambda h, t, it_, jt_: (h, 0, it_[t]))] + r_in_specs,
        out_specs=[pl.BlockSpec((None, s, MLA_QK), lambda h, t, it_, jt_: (h, 0, 0)),
                   pl.BlockSpec((None, blk, MLA_QK), lambda h, t, it_, jt_: (h, jt_[t], 0)),
                   pl.BlockSpec((None, blk, MLA_V), lambda h, t, it_, jt_: (h, jt_[t], 0))] + r_out_specs,
        scratch_shapes=r_scr)
    out = pl.pallas_call(
        body, grid_spec=grid_spec,
        out_shape=[jax.ShapeDtypeStruct((hh, s, MLA_QK), F32), jax.ShapeDtypeStruct((hh, s, MLA_QK), F32),
                   jax.ShapeDtypeStruct((hh, s, MLA_V), F32)] + r_out,
        compiler_params=_params("arbitrary", "arbitrary"), name=name)(it, jt, q, k, v, do, lse_row, delta_row, *r_in)
    return out[0], out[1], out[2], out[3:]


def _mla_bwd_mid(dq, dk, dv, cos, sin, proj, qn, kvn, w_uq, w_ukv, name):
    s = proj.shape[0]
    ts = _row_tile(s, 256)
    hh, half = MLA_HEADS, MLA_ROPE // 2
    nq, nkv = hh * MLA_QK, hh * (MLA_NOPE + MLA_V)

    def body(dq_ref, dk_ref, dv_ref, c_ref, s_ref, p_ref, qn_ref, kvn_ref, wq_ref, wkv_ref,
             dqe_ref, dkve_ref, dp_ref, dqn_ref, dkvn_ref):
        cos_, sin_ = c_ref[...], s_ref[...]
        dkr1 = jnp.zeros((ts, half), F32)
        dkr2 = jnp.zeros((ts, half), F32)
        for h in range(hh):
            dqh, dkh = dq_ref[h], dk_ref[h]
            b = h * MLA_QK
            dqe_ref[:, b:b + MLA_NOPE] = dqh[:, :MLA_NOPE].astype(BF16)
            d1, d2 = _rope_bwd(dqh[:, MLA_NOPE:MLA_NOPE + half], dqh[:, MLA_NOPE + half:], cos_, sin_)
            dqe_ref[:, b + MLA_NOPE:b + MLA_NOPE + half] = d1.astype(BF16)
            dqe_ref[:, b + MLA_NOPE + half:b + MLA_QK] = d2.astype(BF16)
            b = h * (MLA_NOPE + MLA_V)
            dkve_ref[:, b:b + MLA_NOPE] = dkh[:, :MLA_NOPE].astype(BF16)
            dkve_ref[:, b + MLA_NOPE:b + MLA_NOPE + MLA_V] = dv_ref[h].astype(BF16)
            dkr1 = dkr1 + dkh[:, MLA_NOPE:MLA_NOPE + half]
            dkr2 = dkr2 + dkh[:, MLA_NOPE + half:]
        dkr1, dkr2 = _rope_bwd(dkr1, dkr2, cos_, sin_)
        dcqn = _dot(dqe_ref[...], wq_ref[...], NT)
        dckvn = _dot(dkve_ref[...], wkv_ref[...], NT)
        p = p_ref[...]
        dcq, dqn = _rms_bwd_rows(p[:, :MLA_Q_LORA], qn_ref[...], dcqn)
        dckv, dkvn = _rms_bwd_rows(p[:, MLA_Q_LORA:MLA_Q_LORA + MLA_KV_LORA], kvn_ref[...], dckvn)
        dp_ref[:, :MLA_Q_LORA] = dcq.astype(BF16)
        dp_ref[:, MLA_Q_LORA:MLA_Q_LORA + MLA_KV_LORA] = dckv.astype(BF16)
        dp_ref[:, MLA_Q_LORA + MLA_KV_LORA:MLA_Q_LORA + MLA_KV_LORA + half] = dkr1.astype(BF16)
        dp_ref[:, MLA_Q_LORA + MLA_KV_LORA + half:] = dkr2.astype(BF16)

        @pl.when(pl.program_id(0) == 0)
        def _():
            dqn_ref[...] = jnp.zeros_like(dqn_ref)
            dkvn_ref[...] = jnp.zeros_like(dkvn_ref)

        dqn_ref[...] += jnp.sum(dqn, axis=0, keepdims=True)
        dkvn_ref[...] += jnp.sum(dkvn, axis=0, keepdims=True)

    def row(w):
        return pl.BlockSpec((ts, w), lambda i: (i, 0))

    def full(shape):
        return pl.BlockSpec(shape, lambda i: (0,) * len(shape))

    def heads(w):
        return pl.BlockSpec((hh, ts, w), lambda i: (0, i, 0))

    return pl.pallas_call(
        body, grid=(s // ts,),
        in_specs=[heads(MLA_QK), heads(MLA_QK), heads(MLA_V), row(half), row(half), row(MLA_IN),
                  full(qn.shape), full(kvn.shape), full(w_uq.shape), full(w_ukv.shape)],
        out_specs=(row(nq), row(nkv), row(MLA_IN), full(qn.shape), full(kvn.shape)),
        out_shape=(jax.ShapeDtypeStruct((s, nq), BF16), jax.ShapeDtypeStruct((s, nkv), BF16),
                   jax.ShapeDtypeStruct((s, MLA_IN), BF16), jax.ShapeDtypeStruct(qn.shape, F32),
                   jax.ShapeDtypeStruct(kvn.shape, F32)),
        compiler_params=_params("arbitrary"), name=name)(dq, dk, dv, cos, sin, proj, qn, kvn, w_uq, w_ukv)


HGRN_TILE = 128


def _chunk_masks(t):
    r = lax.broadcasted_iota(jnp.int32, (t, t), 0)
    c = lax.broadcasted_iota(jnp.int32, (t, t), 1)
    same = (r // HGRN_CHUNK) == (c // HGRN_CHUNK)
    return r, c, same


def _hgrn_gates(p, lb):
    hk = HGRN_HEADS * HGRN_D
    qx, fx, ix, gx = p[:, :hk], p[:, hk:2 * hk], p[:, 2 * hk:3 * hk], p[:, 3 * hk:]
    sig_f = jax.nn.sigmoid(fx)
    f = lb + (1.0 - lb) * sig_f
    sig_q = jax.nn.sigmoid(qx)
    t = p.shape[0]
    r, c, same = _chunk_masks(t)
    lower = jnp.where(same & (c <= r), 1.0, 0.0).astype(F32)
    b = _dot_f32(lower, jnp.log(f))
    b3 = b.reshape(t // HGRN_CHUNK, HGRN_CHUNK, hk)
    bref = jnp.broadcast_to(b3[:, HGRN_CHUNK // 2:HGRN_CHUNK // 2 + 1, :], b3.shape).reshape(t, hk)
    blast = jnp.broadcast_to(b3[:, HGRN_CHUNK - 1:, :], b3.shape).reshape(t, hk)
    return qx, ix, gx, sig_f, f, sig_q, b, bref, blast


def _hgrn_fwd(proj, lb, onorm, name):
    s = proj.shape[0]
    t = _row_tile(s, HGRN_TILE)
    nc = t // HGRN_CHUNK
    hh, dd, hk = HGRN_HEADS, HGRN_D, HGRN_HEADS * HGRN_D

    def body(p_ref, lb_ref, on_ref, y_ref, o_ref, st_ref, st_scr):
        @pl.when(pl.program_id(0) == 0)
        def _():
            st_scr[...] = jnp.zeros_like(st_scr)

        qx, ix, gx, _, f, sig_q, b, bref, blast = _hgrn_gates(p_ref[...], lb_ref[...])
        q = qx * sig_q
        k = 1.0 - f
        r, c, same = _chunk_masks(t)
        causal = same & (c <= r)
        for h in range(hh):
            sl = slice(h * dd, (h + 1) * dd)
            bh, brefh, blasth, qh, kh = b[:, sl], bref[:, sl], blast[:, sl], q[:, sl], k[:, sl]
            vh = ix[:, sl].astype(BF16)
            q_rel = (qh * jnp.exp(bh - brefh)).astype(BF16)
            k_rel = (kh * jnp.exp(brefh - bh)).astype(BF16)
            a = jnp.where(causal, _dot(q_rel, k_rel, NT), 0.0)
            o_intra = _dot(a.astype(BF16), vh)
            q_dec = (qh * jnp.exp(bh)).astype(BF16)
            k_dec = (kh * jnp.exp(blasth - bh)).astype(BF16)
            dec = jnp.exp(blasth)
            pieces = []
            for ci in range(nc):
                rows = slice(ci * HGRN_CHUNK, (ci + 1) * HGRN_CHUNK)
                st = st_scr[h]
                st_ref[ci, h] = st
                pieces.append(_dot(q_dec[rows], st.astype(BF16), NT))
                st_scr[h] = st * dec[ci * HGRN_CHUNK:ci * HGRN_CHUNK + 1, :] + _dot(vh[rows], k_dec[rows], TN)
            oh = o_intra + jnp.concatenate(pieces, axis=0)
            o_ref[:, sl] = oh
            gate = gx[:, sl] * jax.nn.sigmoid(gx[:, sl])
            y_ref[:, sl] = (oh * _rstd(oh) * on_ref[...] * gate).astype(BF16)

    return pl.pallas_call(
        body, grid=(s // t,),
        in_specs=[pl.BlockSpec((t, 4 * hk), lambda i: (i, 0)), pl.BlockSpec((1, hk), lambda i: (0, 0)),
                  pl.BlockSpec((1, dd), lambda i: (0, 0))],
        out_specs=(pl.BlockSpec((t, hk), lambda i: (i, 0)), pl.BlockSpec((t, hk), lambda i: (i, 0)),
                   pl.BlockSpec((nc, hh, dd, dd), lambda i: (i, 0, 0, 0))),
        out_shape=(jax.ShapeDtypeStruct((s, hk), BF16), jax.ShapeDtypeStruct((s, hk), F32),
                   jax.ShapeDtypeStruct((s // HGRN_CHUNK, hh, dd, dd), F32)),
        scratch_shapes=[pltpu.VMEM((hh, dd, dd), F32)],
        compiler_params=_params("arbitrary"), name=name)(proj, lb, onorm)


def _hgrn_bwd(proj, lb, onorm, o, states, dy, name, rider=None):
    s = proj.shape[0]
    t = _row_tile(s, HGRN_TILE)
    nt = s // t
    nc = t // HGRN_CHUNK
    hh, dd, hk = HGRN_HEADS, HGRN_D, HGRN_HEADS * HGRN_D
    r_in, r_in_specs, r_out, r_out_specs, r_scr = _rider_specs(rider)
    n_rin, n_rout, n_rscr = len(r_in), len(r_out), len(r_scr)

    def body(p_ref, lb_ref, on_ref, o_ref, st_ref, dy_ref, *refs):
        r_refs = refs[:n_rin] + refs[n_rin + 3:n_rin + 3 + n_rout] + refs[len(refs) - n_rscr:]
        dp_ref, dlb_ref, don_ref = refs[n_rin:n_rin + 3]
        dst_scr, cat_scr, ext_scr, dk_scr, dq_scr = refs[n_rin + 3 + n_rout:n_rin + 3 + n_rout + 5]
        _ride(rider, pl.program_id(0), nt, r_refs)

        @pl.when(pl.program_id(0) == 0)
        def _():
            dst_scr[...] = jnp.zeros_like(dst_scr)
            dlb_ref[...] = jnp.zeros_like(dlb_ref)
            don_ref[...] = jnp.zeros_like(don_ref)

        lbv = lb_ref[...]
        qx, ix, gx, sig_f, f, sig_q, b, bref, blast = _hgrn_gates(p_ref[...], lbv)
        q = qx * sig_q
        k = 1.0 - f
        r, c, same = _chunk_masks(t)
        causal = same & (c <= r)
        on = on_ref[...]
        don = jnp.zeros((1, dd), F32)
        for h in range(hh):
            sl = slice(h * dd, (h + 1) * dd)
            oh = o_ref[:, sl]
            dyh = dy_ref[:, sl]
            gxh = gx[:, sl]
            sig_g = jax.nn.sigmoid(gxh)
            rs = _rstd(oh)
            dgate = dyh * (oh * rs * on)
            dp_ref[:, 3 * hk + h * dd:3 * hk + (h + 1) * dd] = (dgate * (sig_g * (1.0 + gxh * (1.0 - sig_g)))).astype(BF16)
            do, donh = _rms_bwd_rows(oh, on, dyh * (gxh * sig_g))
            don = don + jnp.sum(donh, axis=0, keepdims=True)
            dob = do.astype(BF16)
            bh, brefh, blasth, qh, kh = b[:, sl], bref[:, sl], blast[:, sl], q[:, sl], k[:, sl]
            vh = ix[:, sl].astype(BF16)
            e_qr, e_kr, e_qd, e_kd = jnp.exp(bh - brefh), jnp.exp(brefh - bh), jnp.exp(bh), jnp.exp(blasth - bh)
            dec = jnp.exp(blasth)
            q_rel, k_rel, q_dec, k_dec = qh * e_qr, kh * e_kr, qh * e_qd, kh * e_kd
            q_relb, k_relb, q_decb, k_decb = q_rel.astype(BF16), k_rel.astype(BF16), q_dec.astype(BF16), k_dec.astype(BF16)
            a = jnp.where(causal, _dot(q_relb, k_relb, NT), 0.0).astype(BF16)
            dv = _dot(a, dob, TN)
            da = jnp.where(causal, _dot(dob, vh, NT), 0.0).astype(BF16)
            dq_rel = _dot(da, k_relb)
            dk_rel = _dot(da, q_relb, TN)
            dq_dec, dk_dec, dv_inter, ddec = [None] * nc, [None] * nc, [None] * nc, [None] * nc
            for ci in range(nc - 1, -1, -1):
                rows = slice(ci * HGRN_CHUNK, (ci + 1) * HGRN_CHUNK)
                st = st_ref[ci, h]
                dst = dst_scr[h]
                dstb = dst.astype(BF16)
                dq_dec[ci] = _dot(dob[rows], st.astype(BF16))
                dk_dec[ci] = _dot(vh[rows], dstb)
                dv_inter[ci] = _dot(k_decb[rows], dstb, NT)
                ddec[ci] = jnp.broadcast_to(jnp.sum(dst * st, axis=0, keepdims=True), (HGRN_CHUNK, dd))
                dst_scr[h] = dst * dec[ci * HGRN_CHUNK:ci * HGRN_CHUNK + 1, :] + _dot(dob[rows], q_decb[rows], TN)
            dq_dec = jnp.concatenate(dq_dec, axis=0)
            dk_dec = jnp.concatenate(dk_dec, axis=0)
            dv = dv + jnp.concatenate(dv_inter, axis=0)
            ddec = jnp.concatenate(ddec, axis=0)
            dp_ref[:, 2 * hk + h * dd:2 * hk + (h + 1) * dd] = dv.astype(BF16)
            dq_scr[:, sl] = dq_rel * e_qr + dq_dec * e_qd
            dk_scr[:, sl] = dk_rel * e_kr + dk_dec * e_kd
            g_qr, g_kr, g_qd, g_kd = dq_rel * q_rel, dk_rel * k_rel, dq_dec * q_dec, dk_dec * k_dec
            cat_scr[0:t, sl] = g_qr - g_kr + g_qd - g_kd
            cat_scr[t:2 * t, sl] = g_kr - g_qr
            cat_scr[2 * t:3 * t, sl] = g_kd
            ext_scr[:, sl] = ddec * dec
        upper = jnp.where(same & (c >= r), 1.0, 0.0).astype(F32)
        to_ref = jnp.where(same & (r % HGRN_CHUNK <= HGRN_CHUNK // 2), 1.0, 0.0).astype(F32)
        to_all = jnp.where(same, 1.0, 0.0).astype(F32)
        dlogf = _dot_f32(jnp.concatenate([upper, to_ref, to_all], axis=1), cat_scr[...]) + ext_scr[...]
        df = dlogf / f - dk_scr[...]
        dp_ref[:, hk:2 * hk] = (df * (1.0 - lbv) * sig_f * (1.0 - sig_f)).astype(BF16)
        dp_ref[:, 0:hk] = (dq_scr[...] * (sig_q * (1.0 + qx * (1.0 - sig_q)))).astype(BF16)
        dlb_ref[...] += jnp.sum(df * (1.0 - sig_f), axis=0, keepdims=True)
        don_ref[...] += don

    def rev(i):
        return nt - 1 - i

    out = pl.pallas_call(
        body, grid=(nt,),
        in_specs=[pl.BlockSpec((t, 4 * hk), lambda i: (rev(i), 0)), pl.BlockSpec((1, hk), lambda i: (0, 0)),
                  pl.BlockSpec((1, dd), lambda i: (0, 0)), pl.BlockSpec((t, hk), lambda i: (rev(i), 0)),
                  pl.BlockSpec((nc, hh, dd, dd), lambda i: (rev(i), 0, 0, 0)),
                  pl.BlockSpec((t, hk), lambda i: (rev(i), 0))] + r_in_specs,
        out_specs=[pl.BlockSpec((t, 4 * hk), lambda i: (rev(i), 0)), pl.BlockSpec((1, hk), lambda i: (0, 0)),
                   pl.BlockSpec((1, dd), lambda i: (0, 0))] + r_out_specs,
        out_shape=[jax.ShapeDtypeStruct((s, 4 * hk), BF16), jax.ShapeDtypeStruct((1, hk), F32),
                   jax.ShapeDtypeStruct((1, dd), F32)] + r_out,
        scratch_shapes=[pltpu.VMEM((hh, dd, dd), F32), pltpu.VMEM((3 * t, hk), F32), pltpu.VMEM((t, hk), F32),
                        pltpu.VMEM((t, hk), F32), pltpu.VMEM((t, hk), F32)] + r_scr,
        compiler_params=_params("arbitrary"), name=name)(proj, lb, onorm, o, states, dy, *r_in)
    return out[0], out[1], out[2], out[3:]


def _adamw_update(w, g, m, v):
    nm = ADAM_B1 * m + (1.0 - ADAM_B1) * g
    nv = ADAM_B2 * v + (1.0 - ADAM_B2) * (g * g)
    m_hat = nm / (1.0 - ADAM_B1 ** ADAM_STEP)
    v_hat = nv / (1.0 - ADAM_B2 ** ADAM_STEP)
    return -ADAM_LR * (m_hat / (jnp.sqrt(v_hat) + ADAM_EPS) + ADAM_WD * w), nm, nv


def _adamw(w, g, m, v, name):
    rows, cols = w.shape
    tr = _divisor_tile(rows, 256, 8)

    def body(w_ref, g_ref, m_ref, v_ref, d_ref, nm_ref, nv_ref):
        d_ref[...], nm_ref[...], nv_ref[...] = _adamw_update(w_ref[...], g_ref[...], m_ref[...], v_ref[...])

    blk = pl.BlockSpec((tr, cols), lambda i: (i, 0))
    shp = jax.ShapeDtypeStruct((rows, cols), F32)
    return pl.pallas_call(
        body, grid=(rows // tr,), in_specs=[blk] * 4, out_specs=(blk,) * 3, out_shape=(shp,) * 3,
        compiler_params=_params("parallel"), name=name)(w, g, m, v)


ADAMW_BLOCK_ELEMS = 128 * 1024


def _adamw_layers(w, lands, m, v, name):
    ll, rows, cols = w.shape
    tr = _divisor_tile(rows, max(16, ADAMW_BLOCK_ELEMS // cols), 16)

    def body(w_ref, m_ref, v_ref, *refs):
        land_refs, (g_out, d_ref, nm_ref, nv_ref) = refs[:ll], refs[ll:]
        layer = pl.program_id(0)
        for k in range(ll):
            @pl.when(layer == k)
            def _(k=k):
                g = land_refs[k][0].astype(F32)
                for slot in range(1, N_DEV):
                    g = g + land_refs[k][slot].astype(F32)
                g_out[...] = g

        d_ref[...], nm_ref[...], nv_ref[...] = _adamw_update(w_ref[...], g_out[...], m_ref[...], v_ref[...])

    stacked = pl.BlockSpec((None, tr, cols), lambda l, i: (l, i, 0))

    def one(k):
        return pl.BlockSpec((N_DEV, tr, cols), lambda l, i: (0, jnp.where(l == k, i, 0), 0))

    shp = jax.ShapeDtypeStruct(w.shape, F32)
    return pl.pallas_call(
        body, grid=(ll, rows // tr), in_specs=[stacked] * 3 + [one(k) for k in range(ll)], out_specs=(stacked,) * 4,
        out_shape=(shp,) * 4, compiler_params=_params("arbitrary", "arbitrary"), name=name)(w, m, v, *lands)


_HBM = pl.BlockSpec(memory_space=pltpu.HBM)
_MESH = pl.DeviceIdType.MESH


class _GatherRide:
    def __init__(self, blocks, cuts):
        self.operands = list(blocks)
        self.cuts = list(cuts)
        self.out_shapes = []
        for b, cut in zip(blocks, cuts):
            r, c = b.shape
            shape = {"rows": (N_DEV * r, c), "cols": (r, N_DEV * c), "slots": (N_DEV, r, c)}[cut]
            self.out_shapes.append(jax.ShapeDtypeStruct(shape, b.dtype))
        n = len(blocks)
        self.scratch = [pltpu.SemaphoreType.DMA((7 * n,)), pltpu.SemaphoreType.DMA((7 * n,)), pltpu.SemaphoreType.DMA((n,))]

    def _parts(self, *refs):
        n = len(self.operands)
        x_refs, out_refs = refs[:n], refs[n:2 * n]
        send_sems, recv_sems, local_sems = refs[2 * n:]
        x, y, c = lax.axis_index("x"), lax.axis_index("y"), lax.axis_index("c")
        me, sibling = (x, y, c), (x, y, 1 - c)
        chips = [(1 - x, y), (x, 1 - y), (1 - x, 1 - y)]
        mine, first, passed, landed, from_sibling = [], [], [], [], []
        for e in range(n):
            x_ref, out_ref, cut = x_refs[e], out_refs[e], self.cuts[e]
            r, cc = x_ref.shape

            def place(px, py, pc, out_ref=out_ref, cut=cut, r=r, cc=cc):
                p = 4 * px + 2 * py + pc
                if cut == "rows":
                    return out_ref.at[pl.ds(pl.multiple_of(p * r, r), r), :]
                if cut == "cols":
                    return out_ref.at[:, pl.ds(pl.multiple_of(p * cc, cc), cc)]
                return out_ref.at[p]

            def copy(k, block, to, src=None, place=place, e=e):
                return pltpu.make_async_remote_copy(
                    src_ref=place(*block) if src is None else src, dst_ref=place(*block), send_sem=send_sems.at[7 * e + k],
                    recv_sem=recv_sems.at[7 * e + k], device_id=to, device_id_type=_MESH)

            mine.append(pltpu.make_async_copy(x_ref, place(*me), local_sems.at[e]))
            first += [copy(0, me, sibling, src=x_ref)] + [copy(1 + j, me, (*chip, c), src=x_ref) for j, chip in enumerate(chips)]
            passed += [copy(4 + j, (*chip, c), sibling) for j, chip in enumerate(chips)]
            landed += [copy(1 + j, (*chip, c), me) for j, chip in enumerate(chips)]
            from_sibling += [copy(0, sibling, me)] + [copy(4 + j, (*chip, 1 - c), me) for j, chip in enumerate(chips)]
        return mine, first, passed, landed, from_sibling

    def start(self, *refs):
        mine, first, _, _, _ = self._parts(*refs)
        for cp in mine + first:
            cp.start()

    def middle(self, *refs):
        _, _, passed, landed, _ = self._parts(*refs)
        for got, fwd in zip(landed, passed):
            got.wait_recv()
            fwd.start()

    def finish(self, *refs):
        mine, first, passed, _, from_sibling = self._parts(*refs)
        for cp in from_sibling:
            cp.wait_recv()
        for cp in first + passed:
            cp.wait_send()
        for cp in mine:
            cp.wait()


class _ExchangeRide:
    def __init__(self, sends):
        self.operands = list(sends)
        self.out_shapes = [jax.ShapeDtypeStruct(s.shape, s.dtype) for s in sends]
        n = len(sends)
        self.scratch = [pltpu.SemaphoreType.DMA((7 * n,)), pltpu.SemaphoreType.DMA((7 * n,)), pltpu.SemaphoreType.DMA((n,))]

    def _parts(self, *refs):
        n = len(self.operands)
        s_refs, land_refs = refs[:n], refs[n:2 * n]
        send_sems, recv_sems, local_sems = refs[2 * n:]
        x, y, c = lax.axis_index("x"), lax.axis_index("y"), lax.axis_index("c")
        me = 4 * x + 2 * y + c
        own, sends, recvs = [], [], []
        for e in range(n):
            s_ref, land_ref = s_refs[e], land_refs[e]
            own.append(pltpu.make_async_copy(s_ref.at[me], land_ref.at[me], local_sems.at[e]))
            for rel in range(1, N_DEV):
                px = 1 - x if rel & 4 else x
                py = 1 - y if rel & 2 else y
                pc = 1 - c if rel & 1 else c
                peer = 4 * px + 2 * py + pc
                k = 7 * e + rel - 1
                sends.append(pltpu.make_async_remote_copy(
                    src_ref=s_ref.at[peer], dst_ref=land_ref.at[me], send_sem=send_sems.at[k], recv_sem=recv_sems.at[k],
                    device_id=(px, py, pc), device_id_type=_MESH))
                recvs.append(pltpu.make_async_remote_copy(
                    src_ref=s_ref.at[me], dst_ref=land_ref.at[peer], send_sem=send_sems.at[k], recv_sem=recv_sems.at[k],
                    device_id=(px, py, pc), device_id_type=_MESH))
        return own, sends, recvs

    def start(self, *refs):
        own, sends, _ = self._parts(*refs)
        for cp in own + sends:
            cp.start()

    def middle(self, *refs):
        pass

    def finish(self, *refs):
        own, sends, recvs = self._parts(*refs)
        for cp in recvs:
            cp.wait_recv()
        for cp in sends:
            cp.wait_send()
        for cp in own:
            cp.wait()


def _run_alone(rider, name):
    def body(*refs):
        rider.start(*refs)
        rider.middle(*refs)
        rider.finish(*refs)

    return pl.pallas_call(
        body, out_shape=rider.out_shapes, in_specs=[_HBM] * len(rider.operands), out_specs=[_HBM] * len(rider.out_shapes),
        scratch_shapes=rider.scratch, name=name)(*rider.operands)


def _all_gather(xs, name):
    return _run_alone(_GatherRide([xs], ["slots"]), name)[0]


def _sum_slots(parts, name):
    _, rows, cols = parts.shape
    tr = _divisor_tile(rows, 256, 16)

    def body(p_ref, o_ref):
        acc = p_ref[0].astype(F32)
        for slot in range(1, N_DEV):
            acc = acc + p_ref[slot].astype(F32)
        o_ref[...] = acc

    return pl.pallas_call(
        body, grid=(rows // tr,), in_specs=[pl.BlockSpec((N_DEV, tr, cols), lambda i: (0, i, 0))],
        out_specs=pl.BlockSpec((tr, cols), lambda i: (i, 0)), out_shape=jax.ShapeDtypeStruct((rows, cols), F32),
        compiler_params=_params("parallel"), name=name)(parts)


def _mlp_fwd(h, g_pre, g_post, w1, w2, tag):
    a = _rms_fwd(h, g_pre, None, BF16, f"{tag}_norm")
    u, r2 = _mm(a, w1, "nn", 512, 1024, f"{tag}_up", out_dtypes=(F32, BF16),
                epi=lambda acc: (acc, jnp.square(jnp.maximum(acc, 0.0))))
    z = _mm(r2, w2, "nn", 512, 512, f"{tag}_down")
    out = _rms_fwd(z, g_post, h, F32, f"{tag}_out")
    return out, (h, a, u, r2, z)


def _mlp_bwd(dh, saved, g_pre, g_post, w1, w2, tag):
    h, a, u, r2, z = saved
    dz, dg_post = _rms_bwd(z, g_post, dh, None, BF16, f"{tag}_dout")
    du = _mm(dz, w2, "nt", 512, 1024, f"{tag}_ddown", out_dtypes=(BF16,), extras=(u,),
             epi=lambda acc, uu: (acc * (2.0 * jnp.maximum(uu, 0.0)),))
    dw2 = _mm(r2, dz, "tn", 512, 512, f"{tag}_dw2", out_dtypes=(BF16,), shard="rows")
    dw1 = _mm(a, du, "tn", 512, 512, f"{tag}_dw1", out_dtypes=(BF16,), shard="cols")
    da = _mm(du, w1, "nt", 512, 512, f"{tag}_dup")
    dh_in, dg_pre = _rms_bwd(h, g_pre, da, dh, F32, f"{tag}_dnorm")
    return dh_in, dg_pre, dg_post, dw1, dw2


def _hgrn_layer_fwd(h, g_pre, g_post, lb, onorm, w_in, w_o, tag):
    a = _rms_fwd(h, g_pre, None, BF16, f"{tag}_norm")
    proj = _mm(a, w_in, "nn", 512, 1024, f"{tag}_in")
    y, o, states = _hgrn_fwd(proj, lb, onorm, f"{tag}_scan")
    m = _mm(y, w_o, "nn", 512, 1024, f"{tag}_o")
    out = _rms_fwd(m, g_post, h, F32, f"{tag}_out")
    return out, (h, a, proj, y, o, states, m)


def _hgrn_layer_bwd(dh, saved, g_pre, g_post, lb, onorm, w_in, w_o, tag, rider=None):
    h, a, proj, y, o, states, m = saved
    dm, dg_post = _rms_bwd(m, g_post, dh, None, BF16, f"{tag}_dout")
    dy = _mm(dm, w_o, "nt", 512, 1024, f"{tag}_do")
    dw_o = _mm(y, dm, "tn", 128, 1024, f"{tag}_dwo", out_dtypes=(BF16,), shard="rows")
    dproj, dlb, donorm, rode = _hgrn_bwd(proj, lb, onorm, o, states, dy, f"{tag}_dscan", rider)
    dw_in = _mm(a, dproj, "tn", 512, 512, f"{tag}_dwin", out_dtypes=(BF16,), shard="cols")
    da = _mm(dproj, w_in, "nt", 512, 512, f"{tag}_din")
    dh_in, dg_pre = _rms_bwd(h, g_pre, da, dh, F32, f"{tag}_dnorm")
    return dh_in, dg_pre, dg_post, dlb, donorm, dw_in, dw_o, rode


def _mla_layer_fwd(h, g_pre, g_post, cos, sin, w_in, qn, kvn, w_uq, w_ukv, w_o, tag, rider=None):
    a = _rms_fwd(h, g_pre, None, BF16, f"{tag}_norm")
    proj = _mm(a, w_in, "nn", 512, MLA_IN, f"{tag}_in")
    cqn, ckvn, q, k, v = _mla_qkv(proj, qn, kvn, w_uq, w_ukv, cos, sin, f"{tag}_qkv")
    o, lse, rode = _attn_fwd(q, k, v, f"{tag}_attn", rider)
    m = _mm(o, w_o, "nn", 512, 1024, f"{tag}_o")
    out = _rms_fwd(m, g_post, h, F32, f"{tag}_out")
    return out, (h, a, proj, cqn, ckvn, q, k, v, o, lse, m), rode


def _mla_layer_bwd(dh, saved, g_pre, g_post, cos, sin, w_in, qn, kvn, w_uq, w_ukv, w_o, tag, rider=None):
    h, a, proj, cqn, ckvn, q, k, v, o, lse, m = saved
    hh, s = q.shape[0], q.shape[1]
    dm, dg_post = _rms_bwd(m, g_post, dh, None, BF16, f"{tag}_dout")
    do = _mm(dm, w_o, "nt", 512, 1024, f"{tag}_do")
    dw_o = _mm(o, dm, "tn", 128, 1024, f"{tag}_dwo", out_dtypes=(BF16,), shard="rows")
    delta = _attn_delta(do, o, f"{tag}_delta")
    dq, dk, dv, rode = _attn_bwd(q, k, v, do, lse.reshape(hh, 1, s), delta.reshape(hh, 1, s), f"{tag}_dattn", rider)
    dqe, dkve, dproj, dqn, dkvn = _mla_bwd_mid(dq, dk, dv, cos, sin, proj, qn, kvn, w_uq, w_ukv, f"{tag}_dqkv")
    dw_uq = _mm(cqn, dqe, "tn", MLA_Q_LORA, 768, f"{tag}_dwuq", out_dtypes=(BF16,))
    dw_ukv = _mm(ckvn, dkve, "tn", MLA_KV_LORA, 256, f"{tag}_dwukv", out_dtypes=(BF16,), shard="cols")
    dw_in = _mm(a, dproj, "tn", 128, MLA_IN, f"{tag}_dwin", out_dtypes=(BF16,), shard="rows")
    da = _mm(dproj, w_in, "nt", 512, 1024, f"{tag}_din")
    dh_in, dg_pre = _rms_bwd(h, g_pre, da, dh, F32, f"{tag}_dnorm")
    dw_uq = dw_uq.reshape(MLA_Q_LORA, N_DEV, -1).transpose(1, 0, 2)
    return dh_in, dg_pre, dg_post, dqn, dkvn, dw_in, dw_uq, dw_ukv, dw_o, rode


_CUT = dict(mla_w_in="rows", mla_w_uq="cols", mla_w_ukv="cols", mla_w_o="rows", hgrn_w_in="cols", hgrn_w_o="rows",
            mlp_w1="cols", mlp_w2="rows")


def _unit(layer, kind):
    slot = layer // 2
    if kind == "mla":
        return [("mla_w_in", slot), ("mla_w_uq", slot), ("mla_w_ukv", slot), ("mla_w_o", slot)]
    if kind == "hgrn":
        return [("hgrn_w_in", slot), ("hgrn_w_o", slot)]
    return [("mlp_w1", layer), ("mlp_w2", layer)]


_UNITS = [_unit(layer, kind) for layer in range(DEPTH) for kind in (("mla", "hgrn")[layer % 2], "mlp")]
_GATHER_GROUPS = [[0], [1, 2, 3, 4], [5, 6, 7]]
_EXCHANGE_GROUPS = [[7], [6, 5], [4, 3], [2, 1], [0]]


def _entries(group):
    return [e for u in group for e in _UNITS[u]]


def _gather_cut(name):
    return "slots" if name == "mla_w_uq" else _CUT[name]


def _gather_rider(weights, group):
    ents = _entries(group)
    return _GatherRide([weights[name][idx].astype(BF16) for name, idx in ents], [_gather_cut(name) for name, _ in ents])


def _gathered(outs, group):
    res = {}
    for (name, idx), out in zip(_entries(group), outs):
        if _gather_cut(name) == "slots":
            out = out.transpose(1, 0, 2).reshape(out.shape[1], -1)
        res[(name, idx)] = out
    return res


def _adamw_nd(w, g, m, v, name):
    shape = w.shape
    c = shape[-1]
    d, nm, nv = _adamw(w.reshape(-1, c), g.reshape(-1, c), m.reshape(-1, c), v.reshape(-1, c), name)
    return d.reshape(shape), nm.reshape(shape), nv.reshape(shape)


def kernel(x, positions, norm_gains, mla_w_in, mla_q_norm, mla_kv_norm, mla_w_uq, mla_w_ukv, mla_w_o, hgrn_w_in, hgrn_lb_logits, hgrn_o_norm, hgrn_w_o, mlp_w1, mlp_w2, loss_target, m_norm_gains, m_mla_w_in, m_mla_q_norm, m_mla_kv_norm, m_mla_w_uq, m_mla_w_ukv, m_mla_w_o, m_hgrn_w_in, m_hgrn_lb_logits, m_hgrn_o_norm, m_hgrn_w_o, m_mlp_w1, m_mlp_w2, v_norm_gains, v_mla_w_in, v_mla_q_norm, v_mla_kv_norm, v_mla_w_uq, v_mla_w_ukv, v_mla_w_o, v_hgrn_w_in, v_hgrn_lb_logits, v_hgrn_o_norm, v_hgrn_w_o, v_mlp_w1, v_mlp_w2):
    weights = dict(norm_gains=norm_gains, mla_w_in=mla_w_in, mla_q_norm=mla_q_norm, mla_kv_norm=mla_kv_norm,
                   mla_w_uq=mla_w_uq, mla_w_ukv=mla_w_ukv, mla_w_o=mla_w_o, hgrn_w_in=hgrn_w_in,
                   hgrn_lb_logits=hgrn_lb_logits, hgrn_o_norm=hgrn_o_norm, hgrn_w_o=hgrn_w_o, mlp_w1=mlp_w1, mlp_w2=mlp_w2)
    mom_m = dict(norm_gains=m_norm_gains, mla_w_in=m_mla_w_in, mla_q_norm=m_mla_q_norm, mla_kv_norm=m_mla_kv_norm,
                 mla_w_uq=m_mla_w_uq, mla_w_ukv=m_mla_w_ukv, mla_w_o=m_mla_w_o, hgrn_w_in=m_hgrn_w_in,
                 hgrn_lb_logits=m_hgrn_lb_logits, hgrn_o_norm=m_hgrn_o_norm, hgrn_w_o=m_hgrn_w_o, mlp_w1=m_mlp_w1, mlp_w2=m_mlp_w2)
    mom_v = dict(norm_gains=v_norm_gains, mla_w_in=v_mla_w_in, mla_q_norm=v_mla_q_norm, mla_kv_norm=v_mla_kv_norm,
                 mla_w_uq=v_mla_w_uq, mla_w_ukv=v_mla_w_ukv, mla_w_o=v_mla_w_o, hgrn_w_in=v_hgrn_w_in,
                 hgrn_lb_logits=v_hgrn_lb_logits, hgrn_o_norm=v_hgrn_o_norm, hgrn_w_o=v_hgrn_w_o, mlp_w1=v_mlp_w1, mlp_w2=v_mlp_w2)
    order = list(weights)
    seq = x.shape[1]
    h = x.reshape(seq, D_MODEL)
    target = loss_target.reshape(seq, D_MODEL)

    full = _gathered(_run_alone(_gather_rider(weights, _GATHER_GROUPS[0]), "gather_first"), _GATHER_GROUPS[0])
    gains = _all_gather(norm_gains.reshape(DEPTH * 4, D_MODEL // N_DEV), "gather_gains")
    gains = gains.transpose(1, 0, 2).reshape(DEPTH, 4, 1, D_MODEL)
    gather_behind = {0: _GATHER_GROUPS[1], 2: _GATHER_GROUPS[2]}

    cos, sin = _rope_tables(positions.reshape(seq, 1), "rope_tables")
    lower = _lb_fwd(hgrn_lb_logits, "lower_bounds")

    def mixer_args(layer):
        slot = layer // 2
        if layer % 2 == 0:
            return (cos, sin, full[("mla_w_in", slot)], mla_q_norm[slot:slot + 1], mla_kv_norm[slot:slot + 1],
                    full[("mla_w_uq", slot)], full[("mla_w_ukv", slot)], full[("mla_w_o", slot)])
        return (lower[layer:layer + 1], hgrn_o_norm[slot:slot + 1], full[("hgrn_w_in", slot)], full[("hgrn_w_o", slot)])

    saved = []
    for layer in range(DEPTH):
        g = gains[layer]
        if layer % 2 == 0:
            group = gather_behind[layer]
            h, sv_mix, rode = _mla_layer_fwd(h, g[0], g[1], *mixer_args(layer), f"l{layer}_mla",
                                             _gather_rider(weights, group))
            full.update(_gathered(rode, group))
        else:
            h, sv_mix = _hgrn_layer_fwd(h, g[0], g[1], *mixer_args(layer), f"l{layer}_hgrn")
        h, sv_mlp = _mlp_fwd(h, g[2], g[3], full[("mlp_w1", layer)], full[("mlp_w2", layer)], f"l{layer}_mlp")
        saved.append((sv_mix, sv_mlp))

    loss_part, dh = _loss(h, target, "loss")
    loss = lax.psum(loss_part[0, 0], AXES)

    zero_row = jnp.zeros((1, D_MODEL), F32)
    dgains = [[None] * 4 for _ in range(DEPTH)]
    dlower = [zero_row] * DEPTH
    partials, lands = {}, {}
    dqn, dkvn, donorm = [None] * 2, [None] * 2, [None] * 2
    pending = list(_EXCHANGE_GROUPS)

    def next_exchange():
        group = pending.pop(0)
        return group, _ExchangeRide([partials[e] for e in _entries(group)])

    def landed(group, outs):
        lands.update(zip(_entries(group), outs))

    for layer in range(DEPTH - 1, -1, -1):
        slot = layer // 2
        g = gains[layer]
        sv_mix, sv_mlp = saved[layer]
        dh, dgains[layer][2], dgains[layer][3], partials[("mlp_w1", layer)], partials[("mlp_w2", layer)] = _mlp_bwd(
            dh, sv_mlp, g[2], g[3], full[("mlp_w1", layer)], full[("mlp_w2", layer)], f"l{layer}_mlp")
        group, rider = next_exchange()
        if layer % 2 == 0:
            (dh, dgains[layer][0], dgains[layer][1], dqn[slot], dkvn[slot], partials[("mla_w_in", slot)],
             partials[("mla_w_uq", slot)], partials[("mla_w_ukv", slot)], partials[("mla_w_o", slot)], rode) = _mla_layer_bwd(
                dh, sv_mix, g[0], g[1], *mixer_args(layer), f"l{layer}_mla", rider)
        else:
            (dh, dgains[layer][0], dgains[layer][1], dlower[layer], donorm[slot], partials[("hgrn_w_in", slot)],
             partials[("hgrn_w_o", slot)], rode) = _hgrn_layer_bwd(dh, sv_mix, g[0], g[1], *mixer_args(layer), f"l{layer}_hgrn", rider)
        landed(group, rode)
    group, rider = next_exchange()
    landed(group, _run_alone(rider, "exchange_last"))
    grad_x = dh.reshape(x.shape)
    dlogits = _lb_bwd(hgrn_lb_logits, jnp.concatenate(dlower, axis=0), "lower_bounds_bwd")

    pad = jnp.zeros((1, D_MODEL - 2 * MLA_KV_LORA), F32)
    pad2 = jnp.zeros((1, D_MODEL - 2 * HGRN_D), F32)
    small = jnp.concatenate(
        [jnp.concatenate([gg for row in dgains for gg in row], axis=0), jnp.concatenate(dqn, axis=1),
         jnp.concatenate(dkvn + [pad], axis=1), dlogits, jnp.concatenate(donorm + [pad2], axis=1), zero_row], axis=0)
    small = _sum_slots(_all_gather(small, "gather_small_grads"), "sum_small_grads")
    me = 4 * lax.axis_index("x") + 2 * lax.axis_index("y") + lax.axis_index("c")
    n_g = DEPTH * 4
    width = D_MODEL // N_DEV
    grads = {}
    grads["norm_gains"] = lax.dynamic_slice(small[:n_g], (0, me * width), (n_g, width)).reshape(DEPTH, 4, width)
    grads["mla_q_norm"] = small[n_g].reshape(2, MLA_Q_LORA)
    grads["mla_kv_norm"] = small[n_g + 1, :2 * MLA_KV_LORA].reshape(2, MLA_KV_LORA)
    grads["hgrn_lb_logits"] = small[n_g + 2:n_g + 2 + DEPTH]
    grads["hgrn_o_norm"] = small[n_g + 2 + DEPTH, :2 * HGRN_D].reshape(2, HGRN_D)

    deltas, new_m, new_v = {}, {}, {}
    for name in order:
        if name in _CUT:
            per_layer = [lands[(name, idx)] for idx in range(weights[name].shape[0])]
            grads[name], deltas[name], new_m[name], new_v[name] = _adamw_layers(
                weights[name], per_layer, mom_m[name], mom_v[name], f"adamw_{name}")
        else:
            deltas[name], new_m[name], new_v[name] = _adamw_nd(weights[name], grads[name], mom_m[name], mom_v[name], f"adamw_{name}")
    return (loss, grad_x, *[grads[n] for n in order], *[deltas[n] for n in order], *[new_m[n] for n in order],
            *[new_v[n] for n in order])
```

```python
import numpy as np
import jax
import jax.numpy as jnp
from jax import lax
from jax.experimental import pallas as pl
from jax.experimental.pallas import tpu as pltpu

F32, BF16 = jnp.float32, jnp.bfloat16

N_DEV = 8
AXES = ("x", "y", "c")
D_MODEL = 1024
DEPTH = 4
MLA_HEADS = 8
MLA_Q_LORA = 512
MLA_KV_LORA = 256
MLA_NOPE = 128
MLA_ROPE = 64
MLA_V = 128
MLA_QK = MLA_NOPE + MLA_ROPE
MLA_IN = MLA_Q_LORA + MLA_KV_LORA + MLA_ROPE
ROPE_BASE = 10000.0
HGRN_HEADS = 8
HGRN_D = 128
HGRN_CHUNK = 32
D_FF = 4 * D_MODEL
EPS = 1e-6
LOG2_E = 1.4426950408889634
ADAM_LR, ADAM_B1, ADAM_B2, ADAM_EPS, ADAM_WD, ADAM_STEP = 0.001, 0.9, 0.999, 1e-08, 0.01, 10

V7X_VMEM_LIMIT_BYTES = 56 * 1024 * 1024

NN = (((1,), (0,)), ((), ()))
NT = (((1,), (1,)), ((), ()))
TN = (((0,), (0,)), ((), ()))
_DIMS = {"nn": NN, "nt": NT, "tn": TN}


def _params(*sem):
    return pltpu.CompilerParams(dimension_semantics=sem, vmem_limit_bytes=V7X_VMEM_LIMIT_BYTES)


def _dot(a, b, dims=NN):
    return lax.dot_general(a, b, dims, preferred_element_type=F32)


def _dot_f32(a, b):
    return lax.dot_general(a, b, NN, precision=lax.Precision.HIGHEST, preferred_element_type=F32)


def _rstd(x):
    return lax.rsqrt(jnp.mean(x * x, axis=-1, keepdims=True) + EPS)


def _rms_bwd_rows(x, g, dy):
    r = _rstd(x)
    xh = x * r
    dyg = dy * g
    dx = r * (dyg - xh * jnp.mean(dyg * xh, axis=-1, keepdims=True))
    return dx, dy * xh


def _row_tile(n, want):
    t = min(n, want)
    assert n % t == 0, (n, t)
    return t


def _divisor_tile(n, cap, mult):
    for t in range(min(cap, n) - min(cap, n) % mult, 0, -mult):
        if n % t == 0:
            return t
    return n


def _rms_fwd(x, g, res, out_dtype, name):
    s, d = x.shape
    ts = _row_tile(s, 512)

    def body(x_ref, g_ref, *rest):
        xf = x_ref[...]
        y = xf * _rstd(xf) * g_ref[...]
        if res is not None:
            y = rest[0][...] + y
        rest[-1][...] = y.astype(out_dtype)

    row = pl.BlockSpec((ts, d), lambda i: (i, 0))
    vec = pl.BlockSpec((1, d), lambda i: (0, 0))
    ins = [x, g] + ([res] if res is not None else [])
    return pl.pallas_call(
        body, grid=(s // ts,), in_specs=[row, vec] + ([row] if res is not None else []), out_specs=row,
        out_shape=jax.ShapeDtypeStruct((s, d), out_dtype), compiler_params=_params("parallel"), name=name)(*ins)


def _rms_bwd(x, g, dy, res, out_dtype, name):
    s, d = x.shape
    ts = _row_tile(s, 512)

    def body(x_ref, g_ref, dy_ref, *rest):
        dx_ref, dg_ref = rest[-2:]
        dx, dg = _rms_bwd_rows(x_ref[...], g_ref[...], dy_ref[...].astype(F32))
        if res is not None:
            dx = rest[0][...] + dx
        dx_ref[...] = dx.astype(out_dtype)

        @pl.when(pl.program_id(0) == 0)
        def _():
            dg_ref[...] = jnp.zeros_like(dg_ref)

        dg_ref[...] += jnp.sum(dg, axis=0, keepdims=True)

    row = pl.BlockSpec((ts, d), lambda i: (i, 0))
    vec = pl.BlockSpec((1, d), lambda i: (0, 0))
    ins = [x, g, dy] + ([res] if res is not None else [])
    return pl.pallas_call(
        body, grid=(s // ts,), in_specs=[row, vec, row] + ([row] if res is not None else []), out_specs=(row, vec),
        out_shape=(jax.ShapeDtypeStruct((s, d), out_dtype), jax.ShapeDtypeStruct((1, d), F32)),
        compiler_params=_params("arbitrary"), name=name)(*ins)


def _mm(a, b, mode, tm, tn, name, out_dtypes=(F32,), shard=None, epi=None, extras=(), rider=None):
    if mode == "tn":
        k, m = a.shape
        a_spec = pl.BlockSpec((k, tm), lambda i, j: (0, i))
    else:
        m, k = a.shape
        a_spec = pl.BlockSpec((tm, k), lambda i, j: (i, 0))
    if mode == "nt":
        n = b.shape[0]
        b_spec = pl.BlockSpec((tn, k), lambda i, j: (j, 0))
    else:
        n = b.shape[1]
        b_spec = pl.BlockSpec((k, tn), lambda i, j: (0, j))
    assert m % tm == 0 and n % tn == 0, (name, m, tm, n, tn)
    tile = pl.BlockSpec((tm, tn), lambda i, j: (i, j))
    if shard == "rows":
        per = m // N_DEV // tm
        out_specs = [pl.BlockSpec((None, tm, tn), lambda i, j: (i // per, i % per, j))]
        out_shape = [jax.ShapeDtypeStruct((N_DEV, m // N_DEV, n), out_dtypes[0])]
    elif shard == "cols":
        per = n // N_DEV // tn
        out_specs = [pl.BlockSpec((None, tm, tn), lambda i, j: (j // per, i, j % per))]
        out_shape = [jax.ShapeDtypeStruct((N_DEV, m, n // N_DEV), out_dtypes[0])]
    else:
        out_specs = [tile for _ in out_dtypes]
        out_shape = [jax.ShapeDtypeStruct((m, n), dt) for dt in out_dtypes]
    n_ex, n_out = len(extras), len(out_shape)
    r_in, r_in_specs, r_out, r_out_specs, r_scr = _rider_specs(rider)
    n_rin, n_rout = len(r_in), len(r_out)
    grid = (m // tm, n // tn)

    def body(a_ref, b_ref, *refs):
        ex_refs = refs[:n_ex]
        o_refs = refs[n_ex + n_rin:n_ex + n_rin + n_out]
        r_refs = refs[n_ex:n_ex + n_rin] + refs[n_ex + n_rin + n_out:]
        _ride(rider, pl.program_id(0) * grid[1] + pl.program_id(1), grid[0] * grid[1], r_refs)
        acc = _dot(a_ref[...].astype(BF16), b_ref[...].astype(BF16), _DIMS[mode])
        vals = (acc,) if epi is None else epi(acc, *[r[...] for r in ex_refs])
        for o_ref, val in zip(o_refs, vals):
            o_ref[...] = val.astype(o_ref.dtype)

    sem = ("parallel", "parallel") if rider is None else ("arbitrary", "arbitrary")
    out = pl.pallas_call(
        body, grid=grid, in_specs=[a_spec, b_spec] + [tile] * n_ex + r_in_specs, out_specs=out_specs + r_out_specs,
        out_shape=out_shape + r_out, scratch_shapes=r_scr, compiler_params=_params(*sem), name=name)(a, b, *extras, *r_in)
    res = out[0] if n_out == 1 else out[:n_out]
    return res if rider is None else (res, out[n_out:])


def _rope_tables(pos, name):
    s = pos.shape[0]
    half = MLA_ROPE // 2
    inv_freq = jnp.asarray(np.power(np.float32(ROPE_BASE), -np.arange(0, MLA_ROPE, 2, dtype=np.float32) / MLA_ROPE)
                           .astype(np.float32).reshape(1, half))

    def body(p_ref, f_ref, c_ref, s_ref):
        ang = p_ref[...].astype(F32) * f_ref[...]
        c_ref[...] = jnp.cos(ang)
        s_ref[...] = jnp.sin(ang)

    return pl.pallas_call(
        body, out_shape=(jax.ShapeDtypeStruct((s, half), F32), jax.ShapeDtypeStruct((s, half), F32)), name=name)(pos, inv_freq)


def _lb_softmax(logits):
    m = jnp.max(logits, axis=0, keepdims=True)
    e = jnp.exp(logits - m)
    return e / jnp.sum(e, axis=0, keepdims=True)


def _lb_fwd(logits, name):
    def body(l_ref, o_ref):
        p = _lb_softmax(l_ref[...])
        acc = jnp.zeros_like(p[0:1])
        o_ref[0:1, :] = acc
        for layer in range(1, DEPTH):
            acc = acc + p[layer:layer + 1]
            o_ref[layer:layer + 1, :] = acc

    return pl.pallas_call(body, out_shape=jax.ShapeDtypeStruct(logits.shape, F32), name=name)(logits)


def _lb_bwd(logits, dlb, name):
    def body(l_ref, d_ref, o_ref):
        p = _lb_softmax(l_ref[...])
        d = d_ref[...]
        dp = [jnp.zeros_like(d[0:1])] * DEPTH
        run = jnp.zeros_like(d[0:1])
        for layer in range(DEPTH - 1, 0, -1):
            run = run + d[layer:layer + 1]
            dp[layer] = run
        inner = sum(p[layer:layer + 1] * dp[layer] for layer in range(DEPTH))
        for layer in range(DEPTH):
            o_ref[layer:layer + 1, :] = p[layer:layer + 1] * (dp[layer] - inner)

    return pl.pallas_call(body, out_shape=jax.ShapeDtypeStruct(logits.shape, F32), name=name)(logits, dlb)


def _loss(y, target, name):
    s, d = y.shape
    ts = _row_tile(s, 512)

    def body(y_ref, t_ref, l_ref, dy_ref):
        e = y_ref[...] - t_ref[...]
        dy_ref[...] = e / d

        @pl.when(pl.program_id(0) == 0)
        def _():
            l_ref[...] = jnp.zeros_like(l_ref)

        l_ref[...] += 0.5 * jnp.sum(jnp.mean(e * e, axis=-1, keepdims=True), axis=0, keepdims=True)

    row = pl.BlockSpec((ts, d), lambda i: (i, 0))
    return pl.pallas_call(
        body, grid=(s // ts,), in_specs=[row, row], out_specs=(pl.BlockSpec((1, 1), lambda i: (0, 0)), row),
        out_shape=(jax.ShapeDtypeStruct((1, 1), F32), jax.ShapeDtypeStruct((s, d), F32)),
        compiler_params=_params("arbitrary"), name=name)(y, target)


def _rope(t1, t2, cos, sin):
    return t1 * cos - t2 * sin, t1 * sin + t2 * cos


def _rope_bwd(d1, d2, cos, sin):
    return d1 * cos + d2 * sin, d2 * cos - d1 * sin


def _mla_qkv(proj, qn, kvn, w_uq, w_ukv, cos, sin, name):
    s = proj.shape[0]
    ts = _row_tile(s, 256)
    hh, half = MLA_HEADS, MLA_ROPE // 2

    def body(p_ref, qn_ref, kvn_ref, wq_ref, wkv_ref, c_ref, s_ref, cq_ref, ckv_ref, q_ref, k_ref, v_ref):
        p = p_ref[...]
        cq, ckv, kr = p[:, :MLA_Q_LORA], p[:, MLA_Q_LORA:MLA_Q_LORA + MLA_KV_LORA], p[:, MLA_Q_LORA + MLA_KV_LORA:]
        cqn = (cq * _rstd(cq) * qn_ref[...]).astype(BF16)
        ckvn = (ckv * _rstd(ckv) * kvn_ref[...]).astype(BF16)
        cq_ref[...] = cqn
        ckv_ref[...] = ckvn
        qe = _dot(cqn, wq_ref[...])
        kve = _dot(ckvn, wkv_ref[...])
        cos_, sin_ = c_ref[...], s_ref[...]
        k1, k2 = _rope(kr[:, :half], kr[:, half:], cos_, sin_)
        k1, k2 = k1.astype(BF16), k2.astype(BF16)
        for h in range(hh):
            b = h * MLA_QK
            q_ref[h, :, 0:MLA_NOPE] = qe[:, b:b + MLA_NOPE].astype(BF16)
            q1, q2 = _rope(qe[:, b + MLA_NOPE:b + MLA_NOPE + half], qe[:, b + MLA_NOPE + half:b + MLA_QK], cos_, sin_)
            q_ref[h, :, MLA_NOPE:MLA_NOPE + half] = q1.astype(BF16)
            q_ref[h, :, MLA_NOPE + half:MLA_QK] = q2.astype(BF16)
            b = h * (MLA_NOPE + MLA_V)
            k_ref[h, :, 0:MLA_NOPE] = kve[:, b:b + MLA_NOPE].astype(BF16)
            k_ref[h, :, MLA_NOPE:MLA_NOPE + half] = k1
            k_ref[h, :, MLA_NOPE + half:MLA_QK] = k2
            v_ref[h] = kve[:, b + MLA_NOPE:b + MLA_NOPE + MLA_V].astype(BF16)

    def row(w):
        return pl.BlockSpec((ts, w), lambda i: (i, 0))

    def full(shape):
        return pl.BlockSpec(shape, lambda i: (0,) * len(shape))

    def heads(w):
        return pl.BlockSpec((hh, ts, w), lambda i: (0, i, 0))

    return pl.pallas_call(
        body, grid=(s // ts,),
        in_specs=[row(MLA_IN), full(qn.shape), full(kvn.shape), full(w_uq.shape), full(w_ukv.shape), row(half), row(half)],
        out_specs=(row(MLA_Q_LORA), row(MLA_KV_LORA), heads(MLA_QK), heads(MLA_QK), heads(MLA_V)),
        out_shape=(jax.ShapeDtypeStruct((s, MLA_Q_LORA), BF16), jax.ShapeDtypeStruct((s, MLA_KV_LORA), BF16),
                   jax.ShapeDtypeStruct((hh, s, MLA_QK), BF16), jax.ShapeDtypeStruct((hh, s, MLA_QK), BF16),
                   jax.ShapeDtypeStruct((hh, s, MLA_V), BF16)),
        compiler_params=_params("parallel"), name=name)(proj, qn, kvn, w_uq, w_ukv, cos, sin)


ATTN_BLOCK = 1024
ATTN_FWD_TILE = (256, 512)
ATTN_BWD_TILE = (512, 512)


def _attn_block(s):
    return _row_tile(s, ATTN_BLOCK)


def _tile_sees(diag, q0, tq, k0, tk):
    if not diag:
        return True, False
    return k0 <= q0 + tq - 1, k0 + tk - 1 > q0


def _causal_pairs(nb, kv_major):
    if kv_major:
        pairs = [(i, j) for j in range(nb) for i in range(j, nb)]
    else:
        pairs = [(i, j) for i in range(nb) for j in range(i + 1)]
    return (jnp.asarray(np.array([p[0] for p in pairs], np.int32)), jnp.asarray(np.array([p[1] for p in pairs], np.int32)))


def _ride(rider, step, total, refs):
    if rider is None:
        return

    @pl.when(step == 0)
    def _():
        rider.start(*refs)

    @pl.when(step == (total * 3) // 5)
    def _():
        rider.middle(*refs)

    @pl.when(step == total - 1)
    def _():
        rider.finish(*refs)


def _rider_specs(rider):
    if rider is None:
        return [], [], [], [], []
    return (list(rider.operands), [_HBM] * len(rider.operands), list(rider.out_shapes), [_HBM] * len(rider.out_shapes),
            list(rider.scratch))


def _attn_fwd(q, k, v, name, rider=None):
    hh, s, _ = q.shape
    blk = _attn_block(s)
    tq, tk = min(blk, ATTN_FWD_TILE[0]), min(blk, ATTN_FWD_TILE[1])
    nb = s // blk
    it, jt = _causal_pairs(nb, kv_major=False)
    npair = int(it.shape[0])
    scale = MLA_QK ** -0.5
    c2 = scale * LOG2_E
    r_in, r_in_specs, r_out, r_out_specs, r_scr = _rider_specs(rider)
    n_rin, n_rout, n_rscr = len(r_in), len(r_out), len(r_scr)

    def body(it_ref, jt_ref, q_ref, k_ref, v_ref, *refs):
        r_refs = refs[:n_rin] + refs[n_rin + 2:n_rin + 2 + n_rout] + refs[len(refs) - n_rscr:]
        o_ref, lse_ref = refs[n_rin:n_rin + 2]
        m_scr, acc_scr, v_scr = refs[n_rin + 2 + n_rout:n_rin + 2 + n_rout + 3]
        h, t = pl.program_id(0), pl.program_id(1)
        step = h * npair + t
        _ride(rider, step, hh * npair, r_refs)
        i, j = it_ref[t], jt_ref[t]

        @pl.when(j == 0)
        def _():
            m_scr[...] = jnp.full_like(m_scr, -jnp.inf)
            acc_scr[...] = jnp.zeros_like(acc_scr)
            v_scr[:, MLA_V:] = jnp.ones((blk, MLA_V), BF16)

        def block(diag):
            v_scr[:, :MLA_V] = v_ref[...]
            for k0 in range(0, blk, tk):
                kb, vb = k_ref[k0:k0 + tk, :], v_scr[k0:k0 + tk, :]
                for q0 in range(0, blk, tq):
                    visible, needs_mask = _tile_sees(diag, q0, tq, k0, tk)
                    if not visible:
                        continue
                    rows = slice(q0, q0 + tq)
                    sc = _dot(q_ref[rows, :], kb, NT)
                    if needs_mask:
                        qpos = q0 + lax.broadcasted_iota(jnp.int32, (tq, tk), 0)
                        kpos = k0 + lax.broadcasted_iota(jnp.int32, (tq, tk), 1)
                        sc = jnp.where(qpos >= kpos, sc, -jnp.inf)
                    m_prev = m_scr[rows, :]
                    m_new = jnp.maximum(m_prev, jnp.max(sc, axis=-1, keepdims=True))
                    alpha = jnp.exp2((m_prev - m_new) * c2)
                    p = jnp.exp2((sc - m_new) * c2)
                    acc_scr[rows, :] = alpha * acc_scr[rows, :] + _dot(p.astype(BF16), vb)
                    m_scr[rows, :] = m_new

        @pl.when(j < i)
        def _():
            block(False)

        @pl.when(j == i)
        def _():
            block(True)
            acc = acc_scr[...]
            l = acc[:, MLA_V:MLA_V + 1]
            o_ref[...] = acc[:, :MLA_V] / l
            lse_ref[...] = m_scr[...] * scale + jnp.log(l)

    grid_spec = pltpu.PrefetchScalarGridSpec(
        num_scalar_prefetch=2, grid=(hh, npair),
        in_specs=[pl.BlockSpec((None, blk, MLA_QK), lambda h, t, it_, jt_: (h, it_[t], 0)),
                  pl.BlockSpec((None, blk, MLA_QK), lambda h, t, it_, jt_: (h, jt_[t], 0)),
                  pl.BlockSpec((None, blk, MLA_V), lambda h, t, it_, jt_: (h, jt_[t], 0))] + r_in_specs,
        out_specs=[pl.BlockSpec((blk, MLA_V), lambda h, t, it_, jt_: (it_[t], h)),
                   pl.BlockSpec((None, blk, 1), lambda h, t, it_, jt_: (h, it_[t], 0))] + r_out_specs,
        scratch_shapes=[pltpu.VMEM((blk, 1), F32), pltpu.VMEM((blk, 2 * MLA_V), F32),
                        pltpu.VMEM((blk, 2 * MLA_V), BF16)] + r_scr)
    out = pl.pallas_call(
        body, grid_spec=grid_spec,
        out_shape=[jax.ShapeDtypeStruct((s, hh * MLA_V), F32), jax.ShapeDtypeStruct((hh, s, 1), F32)] + r_out,
        compiler_params=_params("arbitrary", "arbitrary"), name=name)(it, jt, q, k, v, *r_in)
    return out[0], out[1], out[2:]


def _attn_delta(do, o, name):
    s = do.shape[0]
    ts = _row_tile(s, 512)
    hh = MLA_HEADS

    def body(do_ref, o_ref, d_ref):
        prod = do_ref[...] * o_ref[...]
        for h in range(hh):
            d_ref[h] = jnp.sum(prod[:, h * MLA_V:(h + 1) * MLA_V], axis=-1, keepdims=True)

    row = pl.BlockSpec((ts, hh * MLA_V), lambda i: (i, 0))
    return pl.pallas_call(
        body, grid=(s // ts,), in_specs=[row, row], out_specs=pl.BlockSpec((hh, ts, 1), lambda i: (0, i, 0)),
        out_shape=jax.ShapeDtypeStruct((hh, s, 1), F32), compiler_params=_params("parallel"), name=name)(do, o)


def _attn_bwd(q, k, v, do, lse_row, delta_row, name, rider=None):
    hh, s, _ = q.shape
    blk = _attn_block(s)
    tq, tk = min(blk, ATTN_BWD_TILE[0]), min(blk, ATTN_BWD_TILE[1])
    nb = s // blk
    it, jt = _causal_pairs(nb, kv_major=True)
    npair = int(it.shape[0])
    scale = MLA_QK ** -0.5
    c2 = scale * LOG2_E
    r_in, r_in_specs, r_out, r_out_specs, r_scr = _rider_specs(rider)
    n_rin, n_rout, n_rscr = len(r_in), len(r_out), len(r_scr)

    def body(it_ref, jt_ref, q_ref, k_ref, v_ref, do_ref, lse_ref, dl_ref, *refs):
        r_refs = refs[:n_rin] + refs[n_rin + 3:n_rin + 3 + n_rout] + refs[len(refs) - n_rscr:]
        dq_ref, dk_ref, dv_ref = refs[n_rin:n_rin + 3]
        h, t = pl.program_id(0), pl.program_id(1)
        step = h * npair + t
        _ride(rider, step, hh * npair, r_refs)
        i, j = it_ref[t], jt_ref[t]

        @pl.when(t == 0)
        def _():
            dq_ref[...] = jnp.zeros_like(dq_ref)

        def block(diag):
            if diag:
                dk_ref[...] = jnp.zeros_like(dk_ref)
                dv_ref[...] = jnp.zeros_like(dv_ref)
            for q0 in range(0, blk, tq):
                qb = q_ref[q0:q0 + tq, :]
                dob = do_ref[q0:q0 + tq, :].astype(BF16)
                lse2 = lse_ref[:, q0:q0 + tq] * LOG2_E
                dl = dl_ref[:, q0:q0 + tq]
                dq = None
                for k0 in range(0, blk, tk):
                    visible, needs_mask = _tile_sees(diag, q0, tq, k0, tk)
                    if not visible:
                        continue
                    kb, vb = k_ref[k0:k0 + tk, :], v_ref[k0:k0 + tk, :]
                    pt = jnp.exp2(_dot(kb, qb, NT) * c2 - lse2)
                    if needs_mask:
                        kpos = k0 + lax.broadcasted_iota(jnp.int32, (tk, tq), 0)
                        qpos = q0 + lax.broadcasted_iota(jnp.int32, (tk, tq), 1)
                        pt = jnp.where(qpos >= kpos, pt, 0.0)
                    dv_ref[k0:k0 + tk, :] += _dot(pt.astype(BF16), dob)
                    dpt = _dot(vb, dob, NT)
                    dst = (pt * (dpt - dl) * scale).astype(BF16)
                    dk_ref[k0:k0 + tk, :] += _dot(dst, qb)
                    part = _dot(dst, kb, TN)
                    dq = part if dq is None else dq + part
                rows = pl.ds(pl.multiple_of(i * blk + q0, tq), tq)
                dq_ref[rows, :] += dq

        @pl.when(i == j)
        def _():
            block(True)

        @pl.when(i > j)
        def _():
            block(False)

    grid_spec = pltpu.PrefetchScalarGridSpec(
        num_scalar_prefetch=2, grid=(hh, npair),
        in_specs=[pl.BlockSpec((None, blk, MLA_QK), lambda h, t, it_, jt_: (h, it_[t], 0)),
                  pl.BlockSpec((None, blk, MLA_QK), lambda h, t, it_, jt_: (h, jt_[t], 0)),
                  pl.BlockSpec((None, blk, MLA_V), lambda h, t, it_, jt_: (h, jt_[t], 0)),
                  pl.BlockSpec((blk, MLA_V), lambda h, t, it_, jt_: (it_[t], h)),
                  pl.BlockSpec((None, 1, blk), lambda h, t, it_, jt_: (h, 0, it_[t])),
                  pl.BlockSpec((None, 1, blk), lambda h, t, it_, jt_: (h, 0, it_[t]))] + r_in_specs,
        out_specs=[pl.BlockSpec((None, s, MLA_QK), lambda h, t, it_, jt_: (h, 0, 0)),
                   pl.BlockSpec((None, blk, MLA_QK), lambda h, t, it_, jt_: (h, jt_[t], 0)),
                   pl.BlockSpec((None, blk, MLA_V), lambda h, t, it_, jt_: (h, jt_[t], 0))] + r_out_specs,
        scratch_shapes=r_scr)
    out = pl.pallas_call(
        body, grid_spec=grid_spec,
        out_shape=[jax.ShapeDtypeStruct((hh, s, MLA_QK), F32), jax.ShapeDtypeStruct((hh, s, MLA_QK), F32),
                   jax.ShapeDtypeStruct((hh, s, MLA_V), F32)] + r_out,
        compiler_params=_params("arbitrary", "arbitrary"), name=name)(it, jt, q, k, v, do, lse_row, delta_row, *r_in)
    return out[0], out[1], out[2], out[3:]


def _mla_bwd_mid(dq, dk, dv, cos, sin, proj, qn, kvn, w_uq, w_ukv, name):
    s = proj.shape[0]
    ts = _row_tile(s, 256)
    hh, half = MLA_HEADS, MLA_ROPE // 2
    nq, nkv = hh * MLA_QK, hh * (MLA_NOPE + MLA_V)

    def body(dq_ref, dk_ref, dv_ref, c_ref, s_ref, p_ref, qn_ref, kvn_ref, wq_ref, wkv_ref,
             dqe_ref, dkve_ref, dp_ref, dqn_ref, dkvn_ref):
        cos_, sin_ = c_ref[...], s_ref[...]
        dkr1 = jnp.zeros((ts, half), F32)
        dkr2 = jnp.zeros((ts, half), F32)
        for h in range(hh):
            dqh, dkh = dq_ref[h], dk_ref[h]
            b = h * MLA_QK
            dqe_ref[:, b:b + MLA_NOPE] = dqh[:, :MLA_NOPE].astype(BF16)
            d1, d2 = _rope_bwd(dqh[:, MLA_NOPE:MLA_NOPE + half], dqh[:, MLA_NOPE + half:], cos_, sin_)
            dqe_ref[:, b + MLA_NOPE:b + MLA_NOPE + half] = d1.astype(BF16)
            dqe_ref[:, b + MLA_NOPE + half:b + MLA_QK] = d2.astype(BF16)
            b = h * (MLA_NOPE + MLA_V)
            dkve_ref[:, b:b + MLA_NOPE] = dkh[:, :MLA_NOPE].astype(BF16)
            dkve_ref[:, b + MLA_NOPE:b + MLA_NOPE + MLA_V] = dv_ref[h].astype(BF16)
            dkr1 = dkr1 + dkh[:, MLA_NOPE:MLA_NOPE + half]
            dkr2 = dkr2 + dkh[:, MLA_NOPE + half:]
        dkr1, dkr2 = _rope_bwd(dkr1, dkr2, cos_, sin_)
        dcqn = _dot(dqe_ref[...], wq_ref[...], NT)
        dckvn = _dot(dkve_ref[...], wkv_ref[...], NT)
        p = p_ref[...]
        dcq, dqn = _rms_bwd_rows(p[:, :MLA_Q_LORA], qn_ref[...], dcqn)
        dckv, dkvn = _rms_bwd_rows(p[:, MLA_Q_LORA:MLA_Q_LORA + MLA_KV_LORA], kvn_ref[...], dckvn)
        dp_ref[:, :MLA_Q_LORA] = dcq.astype(BF16)
        dp_ref[:, MLA_Q_LORA:MLA_Q_LORA + MLA_KV_LORA] = dckv.astype(BF16)
        dp_ref[:, MLA_Q_LORA + MLA_KV_LORA:MLA_Q_LORA + MLA_KV_LORA + half] = dkr1.astype(BF16)
        dp_ref[:, MLA_Q_LORA + MLA_KV_LORA + half:] = dkr2.astype(BF16)

        @pl.when(pl.program_id(0) == 0)
        def _():
            dqn_ref[...] = jnp.zeros_like(dqn_ref)
            dkvn_ref[...] = jnp.zeros_like(dkvn_ref)

        dqn_ref[...] += jnp.sum(dqn, axis=0, keepdims=True)
        dkvn_ref[...] += jnp.sum(dkvn, axis=0, keepdims=True)

    def row(w):
        return pl.BlockSpec((ts, w), lambda i: (i, 0))

    def full(shape):
        return pl.BlockSpec(shape, lambda i: (0,) * len(shape))

    def heads(w):
        return pl.BlockSpec((hh, ts, w), lambda i: (0, i, 0))

    return pl.pallas_call(
        body, grid=(s // ts,),
        in_specs=[heads(MLA_QK), heads(MLA_QK), heads(MLA_V), row(half), row(half), row(MLA_IN),
                  full(qn.shape), full(kvn.shape), full(w_uq.shape), full(w_ukv.shape)],
        out_specs=(row(nq), row(nkv), row(MLA_IN), full(qn.shape), full(kvn.shape)),
        out_shape=(jax.ShapeDtypeStruct((s, nq), BF16), jax.ShapeDtypeStruct((s, nkv), BF16),
                   jax.ShapeDtypeStruct((s, MLA_IN), BF16), jax.ShapeDtypeStruct(qn.shape, F32),
                   jax.ShapeDtypeStruct(kvn.shape, F32)),
        compiler_params=_params("arbitrary"), name=name)(dq, dk, dv, cos, sin, proj, qn, kvn, w_uq, w_ukv)


HGRN_TILE = 128


def _chunk_masks(t):
    r = lax.broadcasted_iota(jnp.int32, (t, t), 0)
    c = lax.broadcasted_iota(jnp.int32, (t, t), 1)
    same = (r // HGRN_CHUNK) == (c // HGRN_CHUNK)
    return r, c, same


def _hgrn_gates(p, lb):
    hk = HGRN_HEADS * HGRN_D
    qx, fx, ix, gx = p[:, :hk], p[:, hk:2 * hk], p[:, 2 * hk:3 * hk], p[:, 3 * hk:]
    sig_f = jax.nn.sigmoid(fx)
    f = lb + (1.0 - lb) * sig_f
    sig_q = jax.nn.sigmoid(qx)
    t = p.shape[0]
    r, c, same = _chunk_masks(t)
    lower = jnp.where(same & (c <= r), 1.0, 0.0).astype(F32)
    b = _dot_f32(lower, jnp.log(f))
    b3 = b.reshape(t // HGRN_CHUNK, HGRN_CHUNK, hk)
    bref = jnp.broadcast_to(b3[:, HGRN_CHUNK // 2:HGRN_CHUNK // 2 + 1, :], b3.shape).reshape(t, hk)
    blast = jnp.broadcast_to(b3[:, HGRN_CHUNK - 1:, :], b3.shape).reshape(t, hk)
    return qx, ix, gx, sig_f, f, sig_q, b, bref, blast


def _hgrn_fwd(proj, lb, onorm, name, rider=None):
    s = proj.shape[0]
    t = _row_tile(s, HGRN_TILE)
    nc = t // HGRN_CHUNK
    hh, dd, hk = HGRN_HEADS, HGRN_D, HGRN_HEADS * HGRN_D
    r_in, r_in_specs, r_out, r_out_specs, r_scr = _rider_specs(rider)
    n_rin, n_rout = len(r_in), len(r_out)

    def body(p_ref, lb_ref, on_ref, *refs):
        y_ref, o_ref, st_ref = refs[n_rin:n_rin + 3]
        st_scr = refs[n_rin + 3 + n_rout]
        _ride(rider, pl.program_id(0), s // t, refs[:n_rin] + refs[n_rin + 3:n_rin + 3 + n_rout] + refs[n_rin + 4 + n_rout:])

        @pl.when(pl.program_id(0) == 0)
        def _():
            st_scr[...] = jnp.zeros_like(st_scr)

        qx, ix, gx, _, f, sig_q, b, bref, blast = _hgrn_gates(p_ref[...], lb_ref[...])
        q = qx * sig_q
        k = 1.0 - f
        r, c, same = _chunk_masks(t)
        causal = same & (c <= r)
        for h in range(hh):
            sl = slice(h * dd, (h + 1) * dd)
            bh, brefh, blasth, qh, kh = b[:, sl], bref[:, sl], blast[:, sl], q[:, sl], k[:, sl]
            vh = ix[:, sl].astype(BF16)
            q_rel = (qh * jnp.exp(bh - brefh)).astype(BF16)
            k_rel = (kh * jnp.exp(brefh - bh)).astype(BF16)
            a = jnp.where(causal, _dot(q_rel, k_rel, NT), 0.0)
            o_intra = _dot(a.astype(BF16), vh)
            q_dec = (qh * jnp.exp(bh)).astype(BF16)
            k_dec = (kh * jnp.exp(blasth - bh)).astype(BF16)
            dec = jnp.exp(blasth)
            pieces = []
            for ci in range(nc):
                rows = slice(ci * HGRN_CHUNK, (ci + 1) * HGRN_CHUNK)
                st = st_scr[h]
                st_ref[ci, h] = st
                pieces.append(_dot(q_dec[rows], st.astype(BF16), NT))
                st_scr[h] = st * dec[ci * HGRN_CHUNK:ci * HGRN_CHUNK + 1, :] + _dot(vh[rows], k_dec[rows], TN)
            oh = o_intra + jnp.concatenate(pieces, axis=0)
            o_ref[:, sl] = oh
            gate = gx[:, sl] * jax.nn.sigmoid(gx[:, sl])
            y_ref[:, sl] = (oh * _rstd(oh) * on_ref[...] * gate).astype(BF16)

    out = pl.pallas_call(
        body, grid=(s // t,),
        in_specs=[pl.BlockSpec((t, 4 * hk), lambda i: (i, 0)), pl.BlockSpec((1, hk), lambda i: (0, 0)),
                  pl.BlockSpec((1, dd), lambda i: (0, 0))] + r_in_specs,
        out_specs=[pl.BlockSpec((t, hk), lambda i: (i, 0)), pl.BlockSpec((t, hk), lambda i: (i, 0)),
                   pl.BlockSpec((nc, hh, dd, dd), lambda i: (i, 0, 0, 0))] + r_out_specs,
        out_shape=[jax.ShapeDtypeStruct((s, hk), BF16), jax.ShapeDtypeStruct((s, hk), F32),
                   jax.ShapeDtypeStruct((s // HGRN_CHUNK, hh, dd, dd), F32)] + r_out,
        scratch_shapes=[pltpu.VMEM((hh, dd, dd), F32)] + r_scr,
        compiler_params=_params("arbitrary"), name=name)(proj, lb, onorm, *r_in)
    return out[0], out[1], out[2], out[3:]


def _hgrn_bwd(proj, lb, onorm, o, states, dy, name, rider=None):
    s = proj.shape[0]
    t = _row_tile(s, HGRN_TILE)
    nt = s // t
    nc = t // HGRN_CHUNK
    hh, dd, hk = HGRN_HEADS, HGRN_D, HGRN_HEADS * HGRN_D
    r_in, r_in_specs, r_out, r_out_specs, r_scr = _rider_specs(rider)
    n_rin, n_rout, n_rscr = len(r_in), len(r_out), len(r_scr)

    def body(p_ref, lb_ref, on_ref, o_ref, st_ref, dy_ref, *refs):
        r_refs = refs[:n_rin] + refs[n_rin + 3:n_rin + 3 + n_rout] + refs[len(refs) - n_rscr:]
        dp_ref, dlb_ref, don_ref = refs[n_rin:n_rin + 3]
        dst_scr, cat_scr, ext_scr, dk_scr, dq_scr = refs[n_rin + 3 + n_rout:n_rin + 3 + n_rout + 5]
        _ride(rider, pl.program_id(0), nt, r_refs)

        @pl.when(pl.program_id(0) == 0)
        def _():
            dst_scr[...] = jnp.zeros_like(dst_scr)
            dlb_ref[...] = jnp.zeros_like(dlb_ref)
            don_ref[...] = jnp.zeros_like(don_ref)

        lbv = lb_ref[...]
        qx, ix, gx, sig_f, f, sig_q, b, bref, blast = _hgrn_gates(p_ref[...], lbv)
        q = qx * sig_q
        k = 1.0 - f
        r, c, same = _chunk_masks(t)
        causal = same & (c <= r)
        on = on_ref[...]
        don = jnp.zeros((1, dd), F32)
        for h in range(hh):
            sl = slice(h * dd, (h + 1) * dd)
            oh = o_ref[:, sl]
            dyh = dy_ref[:, sl]
            gxh = gx[:, sl]
            sig_g = jax.nn.sigmoid(gxh)
            rs = _rstd(oh)
            dgate = dyh * (oh * rs * on)
            dp_ref[:, 3 * hk + h * dd:3 * hk + (h + 1) * dd] = (dgate * (sig_g * (1.0 + gxh * (1.0 - sig_g)))).astype(BF16)
            do, donh = _rms_bwd_rows(oh, on, dyh * (gxh * sig_g))
            don = don + jnp.sum(donh, axis=0, keepdims=True)
            dob = do.astype(BF16)
            bh, brefh, blasth, qh, kh = b[:, sl], bref[:, sl], blast[:, sl], q[:, sl], k[:, sl]
            vh = ix[:, sl].astype(BF16)
            e_qr, e_kr, e_qd, e_kd = jnp.exp(bh - brefh), jnp.exp(brefh - bh), jnp.exp(bh), jnp.exp(blasth - bh)
            dec = jnp.exp(blasth)
            q_rel, k_rel, q_dec, k_dec = qh * e_qr, kh * e_kr, qh * e_qd, kh * e_kd
            q_relb, k_relb, q_decb, k_decb = q_rel.astype(BF16), k_rel.astype(BF16), q_dec.astype(BF16), k_dec.astype(BF16)
            a = jnp.where(causal, _dot(q_relb, k_relb, NT), 0.0).astype(BF16)
            dv = _dot(a, dob, TN)
            da = jnp.where(causal, _dot(dob, vh, NT), 0.0).astype(BF16)
            dq_rel = _dot(da, k_relb)
            dk_rel = _dot(da, q_relb, TN)
            dq_dec, dk_dec, dv_inter, ddec = [None] * nc, [None] * nc, [None] * nc, [None] * nc
            for ci in range(nc - 1, -1, -1):
                rows = slice(ci * HGRN_CHUNK, (ci + 1) * HGRN_CHUNK)
                st = st_ref[ci, h]
                dst = dst_scr[h]
                dstb = dst.astype(BF16)
                dq_dec[ci] = _dot(dob[rows], st.astype(BF16))
                dk_dec[ci] = _dot(vh[rows], dstb)
                dv_inter[ci] = _dot(k_decb[rows], dstb, NT)
                ddec[ci] = jnp.broadcast_to(jnp.sum(dst * st, axis=0, keepdims=True), (HGRN_CHUNK, dd))
                dst_scr[h] = dst * dec[ci * HGRN_CHUNK:ci * HGRN_CHUNK + 1, :] + _dot(dob[rows], q_decb[rows], TN)
            dq_dec = jnp.concatenate(dq_dec, axis=0)
            dk_dec = jnp.concatenate(dk_dec, axis=0)
            dv = dv + jnp.concatenate(dv_inter, axis=0)
            ddec = jnp.concatenate(ddec, axis=0)
            dp_ref[:, 2 * hk + h * dd:2 * hk + (h + 1) * dd] = dv.astype(BF16)
            dq_scr[:, sl] = dq_rel * e_qr + dq_dec * e_qd
            dk_scr[:, sl] = dk_rel * e_kr + dk_dec * e_kd
            g_qr, g_kr, g_qd, g_kd = dq_rel * q_rel, dk_rel * k_rel, dq_dec * q_dec, dk_dec * k_dec
            cat_scr[0:t, sl] = g_qr - g_kr + g_qd - g_kd
            cat_scr[t:2 * t, sl] = g_kr - g_qr
            cat_scr[2 * t:3 * t, sl] = g_kd
            ext_scr[:, sl] = ddec * dec
        upper = jnp.where(same & (c >= r), 1.0, 0.0).astype(F32)
        to_ref = jnp.where(same & (r % HGRN_CHUNK <= HGRN_CHUNK // 2), 1.0, 0.0).astype(F32)
        to_all = jnp.where(same, 1.0, 0.0).astype(F32)
        dlogf = _dot_f32(jnp.concatenate([upper, to_ref, to_all], axis=1), cat_scr[...]) + ext_scr[...]
        df = dlogf / f - dk_scr[...]
        dp_ref[:, hk:2 * hk] = (df * (1.0 - lbv) * sig_f * (1.0 - sig_f)).astype(BF16)
        dp_ref[:, 0:hk] = (dq_scr[...] * (sig_q * (1.0 + qx * (1.0 - sig_q)))).astype(BF16)
        dlb_ref[...] += jnp.sum(df * (1.0 - sig_f), axis=0, keepdims=True)
        don_ref[...] += don

    def rev(i):
        return nt - 1 - i

    out = pl.pallas_call(
        body, grid=(nt,),
        in_specs=[pl.BlockSpec((t, 4 * hk), lambda i: (rev(i), 0)), pl.BlockSpec((1, hk), lambda i: (0, 0)),
                  pl.BlockSpec((1, dd), lambda i: (0, 0)), pl.BlockSpec((t, hk), lambda i: (rev(i), 0)),
                  pl.BlockSpec((nc, hh, dd, dd), lambda i: (rev(i), 0, 0, 0)),
                  pl.BlockSpec((t, hk), lambda i: (rev(i), 0))] + r_in_specs,
        out_specs=[pl.BlockSpec((t, 4 * hk), lambda i: (rev(i), 0)), pl.BlockSpec((1, hk), lambda i: (0, 0)),
                   pl.BlockSpec((1, dd), lambda i: (0, 0))] + r_out_specs,
        out_shape=[jax.ShapeDtypeStruct((s, 4 * hk), BF16), jax.ShapeDtypeStruct((1, hk), F32),
                   jax.ShapeDtypeStruct((1, dd), F32)] + r_out,
        scratch_shapes=[pltpu.VMEM((hh, dd, dd), F32), pltpu.VMEM((3 * t, hk), F32), pltpu.VMEM((t, hk), F32),
                        pltpu.VMEM((t, hk), F32), pltpu.VMEM((t, hk), F32)] + r_scr,
        compiler_params=_params("arbitrary"), name=name)(proj, lb, onorm, o, states, dy, *r_in)
    return out[0], out[1], out[2], out[3:]


def _adamw_update(w, g, m, v):
    nm = ADAM_B1 * m + (1.0 - ADAM_B1) * g
    nv = ADAM_B2 * v + (1.0 - ADAM_B2) * (g * g)
    m_hat = nm / (1.0 - ADAM_B1 ** ADAM_STEP)
    v_hat = nv / (1.0 - ADAM_B2 ** ADAM_STEP)
    return -ADAM_LR * (m_hat / (jnp.sqrt(v_hat) + ADAM_EPS) + ADAM_WD * w), nm, nv


def _adamw(w, g, m, v, name):
    rows, cols = w.shape
    tr = _divisor_tile(rows, 256, 8)

    def body(w_ref, g_ref, m_ref, v_ref, d_ref, nm_ref, nv_ref):
        d_ref[...], nm_ref[...], nv_ref[...] = _adamw_update(w_ref[...], g_ref[...], m_ref[...], v_ref[...])

    blk = pl.BlockSpec((tr, cols), lambda i: (i, 0))
    shp = jax.ShapeDtypeStruct((rows, cols), F32)
    return pl.pallas_call(
        body, grid=(rows // tr,), in_specs=[blk] * 4, out_specs=(blk,) * 3, out_shape=(shp,) * 3,
        compiler_params=_params("parallel"), name=name)(w, g, m, v)


ADAMW_BLOCK_ELEMS = 128 * 1024


def _adamw_layers(w, lands, m, v, name):
    ll, rows, cols = w.shape
    tr = _divisor_tile(rows, max(16, ADAMW_BLOCK_ELEMS // cols), 16)

    def body(w_ref, m_ref, v_ref, *refs):
        land_refs, (g_out, d_ref, nm_ref, nv_ref) = refs[:ll], refs[ll:]
        layer = pl.program_id(0)
        for k in range(ll):
            @pl.when(layer == k)
            def _(k=k):
                g = land_refs[k][0].astype(F32)
                for slot in range(1, N_DEV):
                    g = g + land_refs[k][slot].astype(F32)
                g_out[...] = g

        d_ref[...], nm_ref[...], nv_ref[...] = _adamw_update(w_ref[...], g_out[...], m_ref[...], v_ref[...])

    stacked = pl.BlockSpec((None, tr, cols), lambda l, i: (l, i, 0))

    def one(k):
        return pl.BlockSpec((N_DEV, tr, cols), lambda l, i: (0, jnp.where(l == k, i, 0), 0))

    shp = jax.ShapeDtypeStruct(w.shape, F32)
    return pl.pallas_call(
        body, grid=(ll, rows // tr), in_specs=[stacked] * 3 + [one(k) for k in range(ll)], out_specs=(stacked,) * 4,
        out_shape=(shp,) * 4, compiler_params=_params("arbitrary", "arbitrary"), name=name)(w, m, v, *lands)


_HBM = pl.BlockSpec(memory_space=pltpu.HBM)
_MESH = pl.DeviceIdType.MESH


class _GatherRide:
    def __init__(self, blocks, cuts):
        self.operands = list(blocks)
        self.cuts = list(cuts)
        self.out_shapes = []
        for b, cut in zip(blocks, cuts):
            r, c = b.shape
            shape = {"rows": (N_DEV * r, c), "cols": (r, N_DEV * c), "slots": (N_DEV, r, c)}[cut]
            self.out_shapes.append(jax.ShapeDtypeStruct(shape, b.dtype))
        n = len(blocks)
        self.scratch = [pltpu.SemaphoreType.DMA((7 * n,)), pltpu.SemaphoreType.DMA((7 * n,)), pltpu.SemaphoreType.DMA((n,))]

    def _parts(self, *refs):
        n = len(self.operands)
        x_refs, out_refs = refs[:n], refs[n:2 * n]
        send_sems, recv_sems, local_sems = refs[2 * n:]
        x, y, c = lax.axis_index("x"), lax.axis_index("y"), lax.axis_index("c")
        me, sibling = (x, y, c), (x, y, 1 - c)
        chips = [(1 - x, y), (x, 1 - y), (1 - x, 1 - y)]
        mine, first, passed, landed, from_sibling = [], [], [], [], []
        for e in range(n):
            x_ref, out_ref, cut = x_refs[e], out_refs[e], self.cuts[e]
            r, cc = x_ref.shape

            def place(px, py, pc, out_ref=out_ref, cut=cut, r=r, cc=cc):
                p = 4 * px + 2 * py + pc
                if cut == "rows":
                    return out_ref.at[pl.ds(pl.multiple_of(p * r, r), r), :]
                if cut == "cols":
                    return out_ref.at[:, pl.ds(pl.multiple_of(p * cc, cc), cc)]
                return out_ref.at[p]

            def copy(k, block, to, src=None, place=place, e=e):
                return pltpu.make_async_remote_copy(
                    src_ref=place(*block) if src is None else src, dst_ref=place(*block), send_sem=send_sems.at[7 * e + k],
                    recv_sem=recv_sems.at[7 * e + k], device_id=to, device_id_type=_MESH)

            mine.append(pltpu.make_async_copy(x_ref, place(*me), local_sems.at[e]))
            first += [copy(0, me, sibling, src=x_ref)] + [copy(1 + j, me, (*chip, c), src=x_ref) for j, chip in enumerate(chips)]
            passed += [copy(4 + j, (*chip, c), sibling) for j, chip in enumerate(chips)]
            landed += [copy(1 + j, (*chip, c), me) for j, chip in enumerate(chips)]
            from_sibling += [copy(0, sibling, me)] + [copy(4 + j, (*chip, 1 - c), me) for j, chip in enumerate(chips)]
        return mine, first, passed, landed, from_sibling

    def start(self, *refs):
        mine, first, _, _, _ = self._parts(*refs)
        for cp in mine + first:
            cp.start()

    def middle(self, *refs):
        _, _, passed, landed, _ = self._parts(*refs)
        for got, fwd in zip(landed, passed):
            got.wait_recv()
            fwd.start()

    def finish(self, *refs):
        mine, first, passed, _, from_sibling = self._parts(*refs)
        for cp in from_sibling:
            cp.wait_recv()
        for cp in first + passed:
            cp.wait_send()
        for cp in mine:
            cp.wait()


class _ExchangeRide:
    def __init__(self, sends):
        self.operands = list(sends)
        self.out_shapes = [jax.ShapeDtypeStruct(s.shape, s.dtype) for s in sends]
        n = len(sends)
        self.scratch = [pltpu.SemaphoreType.DMA((7 * n,)), pltpu.SemaphoreType.DMA((7 * n,)), pltpu.SemaphoreType.DMA((n,))]

    def _parts(self, *refs):
        n = len(self.operands)
        s_refs, land_refs = refs[:n], refs[n:2 * n]
        send_sems, recv_sems, local_sems = refs[2 * n:]
        x, y, c = lax.axis_index("x"), lax.axis_index("y"), lax.axis_index("c")
        me = 4 * x + 2 * y + c
        own, sends, recvs = [], [], []
        for e in range(n):
            s_ref, land_ref = s_refs[e], land_refs[e]
            own.append(pltpu.make_async_copy(s_ref.at[me], land_ref.at[me], local_sems.at[e]))
            for rel in range(1, N_DEV):
                px = 1 - x if rel & 4 else x
                py = 1 - y if rel & 2 else y
                pc = 1 - c if rel & 1 else c
                peer = 4 * px + 2 * py + pc
                k = 7 * e + rel - 1
                sends.append(pltpu.make_async_remote_copy(
                    src_ref=s_ref.at[peer], dst_ref=land_ref.at[me], send_sem=send_sems.at[k], recv_sem=recv_sems.at[k],
                    device_id=(px, py, pc), device_id_type=_MESH))
                recvs.append(pltpu.make_async_remote_copy(
                    src_ref=s_ref.at[me], dst_ref=land_ref.at[peer], send_sem=send_sems.at[k], recv_sem=recv_sems.at[k],
                    device_id=(px, py, pc), device_id_type=_MESH))
        return own, sends, recvs

    def start(self, *refs):
        own, sends, _ = self._parts(*refs)
        for cp in own + sends:
            cp.start()

    def middle(self, *refs):
        pass

    def finish(self, *refs):
        own, sends, recvs = self._parts(*refs)
        for cp in recvs:
            cp.wait_recv()
        for cp in sends:
            cp.wait_send()
        for cp in own:
            cp.wait()


def _run_alone(rider, name):
    def body(*refs):
        rider.start(*refs)
        rider.middle(*refs)
        rider.finish(*refs)

    return pl.pallas_call(
        body, out_shape=rider.out_shapes, in_specs=[_HBM] * len(rider.operands), out_specs=[_HBM] * len(rider.out_shapes),
        scratch_shapes=rider.scratch, name=name)(*rider.operands)


def _all_gather(xs, name):
    return _run_alone(_GatherRide([xs], ["slots"]), name)[0]


def _sum_slots(parts, name):
    _, rows, cols = parts.shape
    tr = _divisor_tile(rows, 256, 16)

    def body(p_ref, o_ref):
        acc = p_ref[0].astype(F32)
        for slot in range(1, N_DEV):
            acc = acc + p_ref[slot].astype(F32)
        o_ref[...] = acc

    return pl.pallas_call(
        body, grid=(rows // tr,), in_specs=[pl.BlockSpec((N_DEV, tr, cols), lambda i: (0, i, 0))],
        out_specs=pl.BlockSpec((tr, cols), lambda i: (i, 0)), out_shape=jax.ShapeDtypeStruct((rows, cols), F32),
        compiler_params=_params("parallel"), name=name)(parts)


def _carry(rode, key, riders, call):
    rider = riders.get(key)
    res = call(rider)
    if rider is None:
        return res
    res, rode[key] = res
    return res


def _mlp_fwd(h, g_pre, g_post, w1, w2, tag, riders):
    rode = {}
    a = _rms_fwd(h, g_pre, None, BF16, f"{tag}_norm")
    u, r2 = _carry(rode, "up", riders, lambda r: _mm(
        a, w1, "nn", 512, 1024, f"{tag}_up", out_dtypes=(F32, BF16),
        epi=lambda acc: (acc, jnp.square(jnp.maximum(acc, 0.0))), rider=r))
    z = _carry(rode, "down", riders, lambda r: _mm(r2, w2, "nn", 512, 512, f"{tag}_down", rider=r))
    out = _rms_fwd(z, g_post, h, F32, f"{tag}_out")
    return out, (h, a, u, r2, z), rode


def _mlp_bwd(dh, saved, g_pre, g_post, w1, w2, tag, riders):
    h, a, u, r2, z = saved
    rode = {}
    dz, dg_post = _rms_bwd(z, g_post, dh, None, BF16, f"{tag}_dout")
    du = _carry(rode, "ddown", riders, lambda r: _mm(
        dz, w2, "nt", 512, 1024, f"{tag}_ddown", out_dtypes=(BF16,), extras=(u,),
        epi=lambda acc, uu: (acc * (2.0 * jnp.maximum(uu, 0.0)),), rider=r))
    dw2 = _carry(rode, "dw2", riders, lambda r: _mm(
        r2, dz, "tn", 512, 512, f"{tag}_dw2", out_dtypes=(BF16,), shard="rows", rider=r))
    dw1 = _mm(a, du, "tn", 512, 512, f"{tag}_dw1", out_dtypes=(BF16,), shard="cols")
    da = _mm(du, w1, "nt", 512, 512, f"{tag}_dup")
    dh_in, dg_pre = _rms_bwd(h, g_pre, da, dh, F32, f"{tag}_dnorm")
    return dh_in, dg_pre, dg_post, dw1, dw2, rode


def _hgrn_layer_fwd(h, g_pre, g_post, lb, onorm, w_in, w_o, tag, riders):
    rode = {}
    a = _rms_fwd(h, g_pre, None, BF16, f"{tag}_norm")
    proj = _carry(rode, "in", riders, lambda r: _mm(a, w_in, "nn", 512, 1024, f"{tag}_in", rider=r))
    y, o, states, brought = _hgrn_fwd(proj, lb, onorm, f"{tag}_scan", riders.get("scan"))
    if "scan" in riders:
        rode["scan"] = brought
    m = _mm(y, w_o, "nn", 512, 1024, f"{tag}_o")
    out = _rms_fwd(m, g_post, h, F32, f"{tag}_out")
    return out, (h, a, proj, y, o, states, m), rode


def _hgrn_layer_bwd(dh, saved, g_pre, g_post, lb, onorm, w_in, w_o, tag, rider=None):
    h, a, proj, y, o, states, m = saved
    dm, dg_post = _rms_bwd(m, g_post, dh, None, BF16, f"{tag}_dout")
    dy = _mm(dm, w_o, "nt", 512, 1024, f"{tag}_do")
    dw_o = _mm(y, dm, "tn", 128, 1024, f"{tag}_dwo", out_dtypes=(BF16,), shard="rows")
    dproj, dlb, donorm, rode = _hgrn_bwd(proj, lb, onorm, o, states, dy, f"{tag}_dscan", rider)
    dw_in = _mm(a, dproj, "tn", 512, 512, f"{tag}_dwin", out_dtypes=(BF16,), shard="cols")
    da = _mm(dproj, w_in, "nt", 512, 512, f"{tag}_din")
    dh_in, dg_pre = _rms_bwd(h, g_pre, da, dh, F32, f"{tag}_dnorm")
    return dh_in, dg_pre, dg_post, dlb, donorm, dw_in, dw_o, rode


def _mla_layer_fwd(h, g_pre, g_post, cos, sin, w_in, qn, kvn, w_uq, w_ukv, w_o, tag, rider=None):
    a = _rms_fwd(h, g_pre, None, BF16, f"{tag}_norm")
    proj = _mm(a, w_in, "nn", 512, MLA_IN, f"{tag}_in")
    cqn, ckvn, q, k, v = _mla_qkv(proj, qn, kvn, w_uq, w_ukv, cos, sin, f"{tag}_qkv")
    o, lse, rode = _attn_fwd(q, k, v, f"{tag}_attn", rider)
    m = _mm(o, w_o, "nn", 512, 1024, f"{tag}_o")
    out = _rms_fwd(m, g_post, h, F32, f"{tag}_out")
    return out, (h, a, proj, cqn, ckvn, q, k, v, o, lse, m), rode


def _mla_layer_bwd(dh, saved, g_pre, g_post, cos, sin, w_in, qn, kvn, w_uq, w_ukv, w_o, tag, rider=None):
    h, a, proj, cqn, ckvn, q, k, v, o, lse, m = saved
    hh, s = q.shape[0], q.shape[1]
    dm, dg_post = _rms_bwd(m, g_post, dh, None, BF16, f"{tag}_dout")
    do = _mm(dm, w_o, "nt", 512, 1024, f"{tag}_do")
    dw_o = _mm(o, dm, "tn", 128, 1024, f"{tag}_dwo", out_dtypes=(BF16,), shard="rows")
    delta = _attn_delta(do, o, f"{tag}_delta")
    dq, dk, dv, rode = _attn_bwd(q, k, v, do, lse.reshape(hh, 1, s), delta.reshape(hh, 1, s), f"{tag}_dattn", rider)
    dqe, dkve, dproj, dqn, dkvn = _mla_bwd_mid(dq, dk, dv, cos, sin, proj, qn, kvn, w_uq, w_ukv, f"{tag}_dqkv")
    dw_uq = _mm(cqn, dqe, "tn", MLA_Q_LORA, 768, f"{tag}_dwuq", out_dtypes=(BF16,))
    dw_ukv = _mm(ckvn, dkve, "tn", MLA_KV_LORA, 256, f"{tag}_dwukv", out_dtypes=(BF16,), shard="cols")
    dw_in = _mm(a, dproj, "tn", 128, MLA_IN, f"{tag}_dwin", out_dtypes=(BF16,), shard="rows")
    da = _mm(dproj, w_in, "nt", 512, 1024, f"{tag}_din")
    dh_in, dg_pre = _rms_bwd(h, g_pre, da, dh, F32, f"{tag}_dnorm")
    dw_uq = dw_uq.reshape(MLA_Q_LORA, N_DEV, -1).transpose(1, 0, 2)
    return dh_in, dg_pre, dg_post, dqn, dkvn, dw_in, dw_uq, dw_ukv, dw_o, rode


_CUT = dict(mla_w_in="rows", mla_w_uq="cols", mla_w_ukv="cols", mla_w_o="rows", hgrn_w_in="cols", hgrn_w_o="rows",
            mlp_w1="cols", mlp_w2="rows")


def _unit(layer, kind):
    slot = layer // 2
    if kind == "mla":
        return [("mla_w_in", slot), ("mla_w_uq", slot), ("mla_w_ukv", slot), ("mla_w_o", slot)]
    if kind == "hgrn":
        return [("hgrn_w_in", slot), ("hgrn_w_o", slot)]
    return [("mlp_w1", layer), ("mlp_w2", layer)]


_GATHER_FIRST = _unit(0, "mla")
_GATHER_PLAN = {
    (0, "attn"): _unit(0, "mlp") + _unit(1, "hgrn"),
    (0, "up"): [("mlp_w1", 1)],
    (0, "down"): [("mlp_w2", 1)],
    (1, "in"): _unit(2, "mla"),
    (1, "scan"): _unit(2, "mlp"),
    (2, "attn"): _unit(3, "hgrn") + _unit(3, "mlp"),
}
_EXCHANGE_PLAN = {
    (3, "dscan"): _unit(3, "mlp"),
    (2, "ddown"): [("hgrn_w_in", 1)],
    (2, "dw2"): [("hgrn_w_o", 1)],
    (2, "dattn"): _unit(2, "mlp"),
    (1, "ddown"): _unit(2, "mla"),
    (1, "dscan"): _unit(1, "mlp"),
    (0, "ddown"): [("hgrn_w_in", 0)],
    (0, "dw2"): [("hgrn_w_o", 0)],
    (0, "dattn"): _unit(0, "mlp"),
}
_EXCHANGE_LAST = _unit(0, "mla")


def _gather_cut(name):
    return "slots" if name == "mla_w_uq" else _CUT[name]


def _gather_rider(weights, ents):
    return _GatherRide([weights[name][idx].astype(BF16) for name, idx in ents], [_gather_cut(name) for name, _ in ents])


def _gathered(outs, ents):
    res = {}
    for (name, idx), out in zip(ents, outs):
        if _gather_cut(name) == "slots":
            out = out.transpose(1, 0, 2).reshape(out.shape[1], -1)
        res[(name, idx)] = out
    return res


def _adamw_nd(w, g, m, v, name):
    shape = w.shape
    c = shape[-1]
    d, nm, nv = _adamw(w.reshape(-1, c), g.reshape(-1, c), m.reshape(-1, c), v.reshape(-1, c), name)
    return d.reshape(shape), nm.reshape(shape), nv.reshape(shape)


def kernel(x, positions, norm_gains, mla_w_in, mla_q_norm, mla_kv_norm, mla_w_uq, mla_w_ukv, mla_w_o, hgrn_w_in, hgrn_lb_logits, hgrn_o_norm, hgrn_w_o, mlp_w1, mlp_w2, loss_target, m_norm_gains, m_mla_w_in, m_mla_q_norm, m_mla_kv_norm, m_mla_w_uq, m_mla_w_ukv, m_mla_w_o, m_hgrn_w_in, m_hgrn_lb_logits, m_hgrn_o_norm, m_hgrn_w_o, m_mlp_w1, m_mlp_w2, v_norm_gains, v_mla_w_in, v_mla_q_norm, v_mla_kv_norm, v_mla_w_uq, v_mla_w_ukv, v_mla_w_o, v_hgrn_w_in, v_hgrn_lb_logits, v_hgrn_o_norm, v_hgrn_w_o, v_mlp_w1, v_mlp_w2):
    weights = dict(norm_gains=norm_gains, mla_w_in=mla_w_in, mla_q_norm=mla_q_norm, mla_kv_norm=mla_kv_norm,
                   mla_w_uq=mla_w_uq, mla_w_ukv=mla_w_ukv, mla_w_o=mla_w_o, hgrn_w_in=hgrn_w_in,
                   hgrn_lb_logits=hgrn_lb_logits, hgrn_o_norm=hgrn_o_norm, hgrn_w_o=hgrn_w_o, mlp_w1=mlp_w1, mlp_w2=mlp_w2)
    mom_m = dict(norm_gains=m_norm_gains, mla_w_in=m_mla_w_in, mla_q_norm=m_mla_q_norm, mla_kv_norm=m_mla_kv_norm,
                 mla_w_uq=m_mla_w_uq, mla_w_ukv=m_mla_w_ukv, mla_w_o=m_mla_w_o, hgrn_w_in=m_hgrn_w_in,
                 hgrn_lb_logits=m_hgrn_lb_logits, hgrn_o_norm=m_hgrn_o_norm, hgrn_w_o=m_hgrn_w_o, mlp_w1=m_mlp_w1, mlp_w2=m_mlp_w2)
    mom_v = dict(norm_gains=v_norm_gains, mla_w_in=v_mla_w_in, mla_q_norm=v_mla_q_norm, mla_kv_norm=v_mla_kv_norm,
                 mla_w_uq=v_mla_w_uq, mla_w_ukv=v_mla_w_ukv, mla_w_o=v_mla_w_o, hgrn_w_in=v_hgrn_w_in,
                 hgrn_lb_logits=v_hgrn_lb_logits, hgrn_o_norm=v_hgrn_o_norm, hgrn_w_o=v_hgrn_w_o, mlp_w1=v_mlp_w1, mlp_w2=v_mlp_w2)
    order = list(weights)
    seq = x.shape[1]
    h = x.reshape(seq, D_MODEL)
    target = loss_target.reshape(seq, D_MODEL)

    full = _gathered(_run_alone(_gather_rider(weights, _GATHER_FIRST), "gather_first"), _GATHER_FIRST)
    gains = _all_gather(norm_gains.reshape(DEPTH * 4, D_MODEL // N_DEV), "gather_gains")
    gains = gains.transpose(1, 0, 2).reshape(DEPTH, 4, 1, D_MODEL)

    def gather_riders(layer, keys):
        return {k: _gather_rider(weights, _GATHER_PLAN[(layer, k)]) for k in keys if (layer, k) in _GATHER_PLAN}

    def arrived(layer, rode):
        for k, outs in rode.items():
            full.update(_gathered(outs, _GATHER_PLAN[(layer, k)]))

    cos, sin = _rope_tables(positions.reshape(seq, 1), "rope_tables")
    lower = _lb_fwd(hgrn_lb_logits, "lower_bounds")

    def mixer_args(layer):
        slot = layer // 2
        if layer % 2 == 0:
            return (cos, sin, full[("mla_w_in", slot)], mla_q_norm[slot:slot + 1], mla_kv_norm[slot:slot + 1],
                    full[("mla_w_uq", slot)], full[("mla_w_ukv", slot)], full[("mla_w_o", slot)])
        return (lower[layer:layer + 1], hgrn_o_norm[slot:slot + 1], full[("hgrn_w_in", slot)], full[("hgrn_w_o", slot)])

    saved = []
    for layer in range(DEPTH):
        g = gains[layer]
        if layer % 2 == 0:
            h, sv_mix, brought = _mla_layer_fwd(h, g[0], g[1], *mixer_args(layer), f"l{layer}_mla",
                                                gather_riders(layer, ["attn"]).get("attn"))
            arrived(layer, {"attn": brought} if (layer, "attn") in _GATHER_PLAN else {})
        else:
            h, sv_mix, rode = _hgrn_layer_fwd(h, g[0], g[1], *mixer_args(layer), f"l{layer}_hgrn",
                                              gather_riders(layer, ["in", "scan"]))
            arrived(layer, rode)
        h, sv_mlp, rode = _mlp_fwd(h, g[2], g[3], full[("mlp_w1", layer)], full[("mlp_w2", layer)], f"l{layer}_mlp",
                                   gather_riders(layer, ["up", "down"]))
        arrived(layer, rode)
        saved.append((sv_mix, sv_mlp))

    loss_part, dh = _loss(h, target, "loss")
    loss = lax.psum(loss_part[0, 0], AXES)

    zero_row = jnp.zeros((1, D_MODEL), F32)
    dgains = [[None] * 4 for _ in range(DEPTH)]
    dlower = [zero_row] * DEPTH
    partials, lands = {}, {}
    dqn, dkvn, donorm = [None] * 2, [None] * 2, [None] * 2

    def exchange_riders(layer, keys):
        return {k: _ExchangeRide([partials[e] for e in _EXCHANGE_PLAN[(layer, k)]]) for k in keys
                if (layer, k) in _EXCHANGE_PLAN}

    def landed(layer, rode):
        for k, outs in rode.items():
            lands.update(zip(_EXCHANGE_PLAN[(layer, k)], outs))

    for layer in range(DEPTH - 1, -1, -1):
        slot = layer // 2
        g = gains[layer]
        sv_mix, sv_mlp = saved[layer]
        dh, dgains[layer][2], dgains[layer][3], partials[("mlp_w1", layer)], partials[("mlp_w2", layer)], rode = _mlp_bwd(
            dh, sv_mlp, g[2], g[3], full[("mlp_w1", layer)], full[("mlp_w2", layer)], f"l{layer}_mlp",
            exchange_riders(layer, ["ddown", "dw2"]))
        landed(layer, rode)
        key = "dattn" if layer % 2 == 0 else "dscan"
        rider = exchange_riders(layer, [key]).get(key)
        if layer % 2 == 0:
            (dh, dgains[layer][0], dgains[layer][1], dqn[slot], dkvn[slot], partials[("mla_w_in", slot)],
             partials[("mla_w_uq", slot)], partials[("mla_w_ukv", slot)], partials[("mla_w_o", slot)], brought) = _mla_layer_bwd(
                dh, sv_mix, g[0], g[1], *mixer_args(layer), f"l{layer}_mla", rider)
        else:
            (dh, dgains[layer][0], dgains[layer][1], dlower[layer], donorm[slot], partials[("hgrn_w_in", slot)],
             partials[("hgrn_w_o", slot)], brought) = _hgrn_layer_bwd(dh, sv_mix, g[0], g[1], *mixer_args(layer), f"l{layer}_hgrn", rider)
        landed(layer, {key: brought} if rider is not None else {})
    lands.update(zip(_EXCHANGE_LAST, _run_alone(_ExchangeRide([partials[e] for e in _EXCHANGE_LAST]), "exchange_last")))
    grad_x = dh.reshape(x.shape)
    dlogits = _lb_bwd(hgrn_lb_logits, jnp.concatenate(dlower, axis=0), "lower_bounds_bwd")

    pad = jnp.zeros((1, D_MODEL - 2 * MLA_KV_LORA), F32)
    pad2 = jnp.zeros((1, D_MODEL - 2 * HGRN_D), F32)
    small = jnp.concatenate(
        [jnp.concatenate([gg for row in dgains for gg in row], axis=0), jnp.concatenate(dqn, axis=1),
         jnp.concatenate(dkvn + [pad], axis=1), dlogits, jnp.concatenate(donorm + [pad2], axis=1), zero_row], axis=0)
    small = _sum_slots(_all_gather(small, "gather_small_grads"), "sum_small_grads")
    me = 4 * lax.axis_index("x") + 2 * lax.axis_index("y") + lax.axis_index("c")
    n_g = DEPTH * 4
    width = D_MODEL // N_DEV
    grads = {}
    grads["norm_gains"] = lax.dynamic_slice(small[:n_g], (0, me * width), (n_g, width)).reshape(DEPTH, 4, width)
    grads["mla_q_norm"] = small[n_g].reshape(2, MLA_Q_LORA)
    grads["mla_kv_norm"] = small[n_g + 1, :2 * MLA_KV_LORA].reshape(2, MLA_KV_LORA)
    grads["hgrn_lb_logits"] = small[n_g + 2:n_g + 2 + DEPTH]
    grads["hgrn_o_norm"] = small[n_g + 2 + DEPTH, :2 * HGRN_D].reshape(2, HGRN_D)

    deltas, new_m, new_v = {}, {}, {}
    for name in order:
        if name in _CUT:
            per_layer = [lands[(name, idx)] for idx in range(weights[name].shape[0])]
            grads[name], deltas[name], new_m[name], new_v[name] = _adamw_layers(
                weights[name], per_layer, mom_m[name], mom_v[name], f"adamw_{name}")
        else:
            deltas[name], new_m[name], new_v[name] = _adamw_nd(weights[name], grads[name], mom_m[name], mom_v[name], f"adamw_{name}")
    return (loss, grad_x, *[grads[n] for n in order], *[deltas[n] for n in order], *[new_m[n] for n in order],
            *[new_v[n] for n in order])
```

```python
import numpy as np
import jax
import jax.numpy as jnp
from jax import lax
from jax.experimental import pallas as pl
from jax.experimental.pallas import tpu as pltpu

F32, BF16 = jnp.float32, jnp.bfloat16

N_DEV = 8
AXES = ("x", "y", "c")
D_MODEL = 1024
DEPTH = 4
MLA_HEADS = 8
MLA_Q_LORA = 512
MLA_KV_LORA = 256
MLA_NOPE = 128
MLA_ROPE = 64
MLA_V = 128
MLA_QK = MLA_NOPE + MLA_ROPE
MLA_IN = MLA_Q_LORA + MLA_KV_LORA + MLA_ROPE
ROPE_BASE = 10000.0
HGRN_HEADS = 8
HGRN_D = 128
HGRN_CHUNK = 32
D_FF = 4 * D_MODEL
EPS = 1e-6
LOG2_E = 1.4426950408889634
ADAM_LR, ADAM_B1, ADAM_B2, ADAM_EPS, ADAM_WD, ADAM_STEP = 0.001, 0.9, 0.999, 1e-08, 0.01, 10

V7X_VMEM_LIMIT_BYTES = 56 * 1024 * 1024

NN = (((1,), (0,)), ((), ()))
NT = (((1,), (1,)), ((), ()))
TN = (((0,), (0,)), ((), ()))
_DIMS = {"nn": NN, "nt": NT, "tn": TN}


def _params(*sem):
    return pltpu.CompilerParams(dimension_semantics=sem, vmem_limit_bytes=V7X_VMEM_LIMIT_BYTES)


def _dot(a, b, dims=NN):
    return lax.dot_general(a, b, dims, preferred_element_type=F32)


def _dot_f32(a, b):
    return lax.dot_general(a, b, NN, precision=lax.Precision.HIGHEST, preferred_element_type=F32)


def _rstd(x):
    return lax.rsqrt(jnp.mean(x * x, axis=-1, keepdims=True) + EPS)


def _rms_bwd_rows(x, g, dy):
    r = _rstd(x)
    xh = x * r
    dyg = dy * g
    dx = r * (dyg - xh * jnp.mean(dyg * xh, axis=-1, keepdims=True))
    return dx, dy * xh


def _row_tile(n, want):
    t = min(n, want)
    assert n % t == 0, (n, t)
    return t


def _divisor_tile(n, cap, mult):
    for t in range(min(cap, n) - min(cap, n) % mult, 0, -mult):
        if n % t == 0:
            return t
    return n


def _rms_fwd(x, g, res, out_dtype, name):
    s, d = x.shape
    ts = _row_tile(s, 512)

    def body(x_ref, g_ref, *rest):
        xf = x_ref[...]
        y = xf * _rstd(xf) * g_ref[...]
        if res is not None:
            y = rest[0][...] + y
        rest[-1][...] = y.astype(out_dtype)

    row = pl.BlockSpec((ts, d), lambda i: (i, 0))
    vec = pl.BlockSpec((1, d), lambda i: (0, 0))
    ins = [x, g] + ([res] if res is not None else [])
    return pl.pallas_call(
        body, grid=(s // ts,), in_specs=[row, vec] + ([row] if res is not None else []), out_specs=row,
        out_shape=jax.ShapeDtypeStruct((s, d), out_dtype), compiler_params=_params("parallel"), name=name)(*ins)


def _rms_bwd(x, g, dy, res, out_dtype, name):
    s, d = x.shape
    ts = _row_tile(s, 512)

    def body(x_ref, g_ref, dy_ref, *rest):
        dx_ref, dg_ref = rest[-2:]
        dx, dg = _rms_bwd_rows(x_ref[...], g_ref[...], dy_ref[...].astype(F32))
        if res is not None:
            dx = rest[0][...] + dx
        dx_ref[...] = dx.astype(out_dtype)

        @pl.when(pl.program_id(0) == 0)
        def _():
            dg_ref[...] = jnp.zeros_like(dg_ref)

        dg_ref[...] += jnp.sum(dg, axis=0, keepdims=True)

    row = pl.BlockSpec((ts, d), lambda i: (i, 0))
    vec = pl.BlockSpec((1, d), lambda i: (0, 0))
    ins = [x, g, dy] + ([res] if res is not None else [])
    return pl.pallas_call(
        body, grid=(s // ts,), in_specs=[row, vec, row] + ([row] if res is not None else []), out_specs=(row, vec),
        out_shape=(jax.ShapeDtypeStruct((s, d), out_dtype), jax.ShapeDtypeStruct((1, d), F32)),
        compiler_params=_params("arbitrary"), name=name)(*ins)


def _mm(a, b, mode, tm, tn, name, out_dtypes=(F32,), shard=None, epi=None, extras=(), rider=None):
    if mode == "tn":
        k, m = a.shape
        a_spec = pl.BlockSpec((k, tm), lambda i, j: (0, i))
    else:
        m, k = a.shape
        a_spec = pl.BlockSpec((tm, k), lambda i, j: (i, 0))
    if mode == "nt":
        n = b.shape[0]
        b_spec = pl.BlockSpec((tn, k), lambda i, j: (j, 0))
    else:
        n = b.shape[1]
        b_spec = pl.BlockSpec((k, tn), lambda i, j: (0, j))
    assert m % tm == 0 and n % tn == 0, (name, m, tm, n, tn)
    tile = pl.BlockSpec((tm, tn), lambda i, j: (i, j))
    if shard == "rows":
        per = m // N_DEV // tm
        out_specs = [pl.BlockSpec((None, tm, tn), lambda i, j: (i // per, i % per, j))]
        out_shape = [jax.ShapeDtypeStruct((N_DEV, m // N_DEV, n), out_dtypes[0])]
    elif shard == "cols":
        per = n // N_DEV // tn
        out_specs = [pl.BlockSpec((None, tm, tn), lambda i, j: (j // per, i, j % per))]
        out_shape = [jax.ShapeDtypeStruct((N_DEV, m, n // N_DEV), out_dtypes[0])]
    else:
        out_specs = [tile for _ in out_dtypes]
        out_shape = [jax.ShapeDtypeStruct((m, n), dt) for dt in out_dtypes]
    n_ex, n_out = len(extras), len(out_shape)
    r_in, r_in_specs, r_out, r_out_specs, r_scr = _rider_specs(rider)
    n_rin, n_rout = len(r_in), len(r_out)
    grid = (m // tm, n // tn)

    def body(a_ref, b_ref, *refs):
        ex_refs = refs[:n_ex]
        o_refs = refs[n_ex + n_rin:n_ex + n_rin + n_out]
        r_refs = refs[n_ex:n_ex + n_rin] + refs[n_ex + n_rin + n_out:]
        _ride(rider, pl.program_id(0) * grid[1] + pl.program_id(1), grid[0] * grid[1], r_refs)
        acc = _dot(a_ref[...].astype(BF16), b_ref[...].astype(BF16), _DIMS[mode])
        vals = (acc,) if epi is None else epi(acc, *[r[...] for r in ex_refs])
        for o_ref, val in zip(o_refs, vals):
            o_ref[...] = val.astype(o_ref.dtype)

    sem = ("parallel", "parallel") if rider is None else ("arbitrary", "arbitrary")
    out = pl.pallas_call(
        body, grid=grid, in_specs=[a_spec, b_spec] + [tile] * n_ex + r_in_specs, out_specs=out_specs + r_out_specs,
        out_shape=out_shape + r_out, scratch_shapes=r_scr, compiler_params=_params(*sem), name=name)(a, b, *extras, *r_in)
    res = out[0] if n_out == 1 else out[:n_out]
    return res if rider is None else (res, out[n_out:])


def _norm_mm(x, g, b, tm, tn, name, out_dtypes=(F32,), epi=None, rider=None):
    m, k = x.shape
    n = b.shape[1]
    assert m % tm == 0 and n % tn == 0, (name, m, tm, n, tn)
    grid = (m // tm, n // tn)
    n_out = len(out_dtypes)
    r_in, r_in_specs, r_out, r_out_specs, r_scr = _rider_specs(rider)
    n_rin = len(r_in)

    def body(x_ref, g_ref, b_ref, *refs):
        a_ref = refs[n_rin]
        o_refs = refs[n_rin + 1:n_rin + 1 + n_out]
        _ride(rider, pl.program_id(0) * grid[1] + pl.program_id(1), grid[0] * grid[1], refs[:n_rin] + refs[n_rin + 1 + n_out:])

        @pl.when(pl.program_id(1) == 0)
        def _():
            xf = x_ref[...]
            a_ref[...] = (xf * _rstd(xf) * g_ref[...]).astype(BF16)

        acc = _dot(a_ref[...], b_ref[...].astype(BF16))
        vals = (acc,) if epi is None else epi(acc)
        for o_ref, val in zip(o_refs, vals):
            o_ref[...] = val.astype(o_ref.dtype)

    row = pl.BlockSpec((tm, k), lambda i, j: (i, 0))
    tile = pl.BlockSpec((tm, tn), lambda i, j: (i, j))
    out = pl.pallas_call(
        body, grid=grid,
        in_specs=[row, pl.BlockSpec((1, k), lambda i, j: (0, 0)), pl.BlockSpec((k, tn), lambda i, j: (0, j))] + r_in_specs,
        out_specs=[row] + [tile] * n_out + r_out_specs,
        out_shape=[jax.ShapeDtypeStruct((m, k), BF16)] + [jax.ShapeDtypeStruct((m, n), dt) for dt in out_dtypes] + r_out,
        scratch_shapes=r_scr, compiler_params=_params("arbitrary", "arbitrary"), name=name)(x, g, b, *r_in)
    return out[:1 + n_out], out[1 + n_out:]


def _mm_norm_res(a, b, g, res, tm, name, rider=None):
    m, k = a.shape
    n = b.shape[1]
    assert m % tm == 0, (name, m, tm)
    r_in, r_in_specs, r_out, r_out_specs, r_scr = _rider_specs(rider)
    n_rin = len(r_in)

    def body(a_ref, b_ref, g_ref, res_ref, *refs):
        z_ref, o_ref = refs[n_rin:n_rin + 2]
        _ride(rider, pl.program_id(0), m // tm, refs[:n_rin] + refs[n_rin + 2:])
        z = _dot(a_ref[...].astype(BF16), b_ref[...].astype(BF16))
        z_ref[...] = z
        o_ref[...] = res_ref[...] + z * _rstd(z) * g_ref[...]

    row = pl.BlockSpec((tm, n), lambda i: (i, 0))
    out = pl.pallas_call(
        body, grid=(m // tm,),
        in_specs=[pl.BlockSpec((tm, k), lambda i: (i, 0)), pl.BlockSpec((k, n), lambda i: (0, 0)),
                  pl.BlockSpec((1, n), lambda i: (0, 0)), row] + r_in_specs,
        out_specs=[row, row] + r_out_specs,
        out_shape=[jax.ShapeDtypeStruct((m, n), F32), jax.ShapeDtypeStruct((m, n), F32)] + r_out,
        scratch_shapes=r_scr, compiler_params=_params("arbitrary"), name=name)(a, b, g, res, *r_in)
    return out[0], out[1], out[2:]


def _rmsbwd_mm(x, g, dy, b, tm, tn, name, out_dtypes=(F32,), epi=None, extras=(), rider=None):
    m, k = x.shape
    n = b.shape[0]
    assert m % tm == 0 and n % tn == 0, (name, m, tm, n, tn)
    grid = (m // tm, n // tn)
    n_ex, n_out = len(extras), len(out_dtypes)
    r_in, r_in_specs, r_out, r_out_specs, r_scr = _rider_specs(rider)
    n_rin = len(r_in)

    def body(x_ref, g_ref, dy_ref, b_ref, *refs):
        ex_refs = refs[:n_ex]
        dx_ref, dg_ref = refs[n_ex + n_rin:n_ex + n_rin + 2]
        o_refs = refs[n_ex + n_rin + 2:n_ex + n_rin + 2 + n_out]
        i, j = pl.program_id(0), pl.program_id(1)
        _ride(rider, i * grid[1] + j, grid[0] * grid[1], refs[n_ex:n_ex + n_rin] + refs[n_ex + n_rin + 2 + n_out:])

        @pl.when((i == 0) & (j == 0))
        def _():
            dg_ref[...] = jnp.zeros_like(dg_ref)

        @pl.when(j == 0)
        def _():
            dx, dg = _rms_bwd_rows(x_ref[...], g_ref[...], dy_ref[...])
            dx_ref[...] = dx.astype(BF16)
            dg_ref[...] += jnp.sum(dg, axis=0, keepdims=True)

        acc = _dot(dx_ref[...], b_ref[...].astype(BF16), NT)
        vals = (acc,) if epi is None else epi(acc, *[r[...] for r in ex_refs])
        for o_ref, val in zip(o_refs, vals):
            o_ref[...] = val.astype(o_ref.dtype)

    row = pl.BlockSpec((tm, k), lambda i, j: (i, 0))
    vec = pl.BlockSpec((1, k), lambda i, j: (0, 0))
    tile = pl.BlockSpec((tm, tn), lambda i, j: (i, j))
    out = pl.pallas_call(
        body, grid=grid,
        in_specs=[row, vec, row, pl.BlockSpec((tn, k), lambda i, j: (j, 0))] + [tile] * n_ex + r_in_specs,
        out_specs=[row, vec] + [tile] * n_out + r_out_specs,
        out_shape=[jax.ShapeDtypeStruct((m, k), BF16), jax.ShapeDtypeStruct((1, k), F32)]
        + [jax.ShapeDtypeStruct((m, n), dt) for dt in out_dtypes] + r_out,
        scratch_shapes=r_scr, compiler_params=_params("arbitrary", "arbitrary"), name=name)(x, g, dy, b, *extras, *r_in)
    return out[:2 + n_out], out[2 + n_out:]


def _mm_rmsbwd_res(a, b, x, g, res, tm, name):
    m, k = a.shape
    n = b.shape[0]
    assert m % tm == 0, (name, m, tm)

    def body(a_ref, b_ref, x_ref, g_ref, res_ref, o_ref, dg_ref):
        @pl.when(pl.program_id(0) == 0)
        def _():
            dg_ref[...] = jnp.zeros_like(dg_ref)

        da = _dot(a_ref[...].astype(BF16), b_ref[...].astype(BF16), NT)
        dx, dg = _rms_bwd_rows(x_ref[...], g_ref[...], da)
        o_ref[...] = res_ref[...] + dx
        dg_ref[...] += jnp.sum(dg, axis=0, keepdims=True)

    row = pl.BlockSpec((tm, n), lambda i: (i, 0))
    vec = pl.BlockSpec((1, n), lambda i: (0, 0))
    return pl.pallas_call(
        body, grid=(m // tm,),
        in_specs=[pl.BlockSpec((tm, k), lambda i: (i, 0)), pl.BlockSpec((n, k), lambda i: (0, 0)), row, vec, row],
        out_specs=(row, vec), out_shape=(jax.ShapeDtypeStruct((m, n), F32), jax.ShapeDtypeStruct((1, n), F32)),
        compiler_params=_params("arbitrary"), name=name)(a, b, x, g, res)


def _rope_tables(pos, name):
    s = pos.shape[0]
    half = MLA_ROPE // 2
    inv_freq = jnp.asarray(np.power(np.float32(ROPE_BASE), -np.arange(0, MLA_ROPE, 2, dtype=np.float32) / MLA_ROPE)
                           .astype(np.float32).reshape(1, half))

    def body(p_ref, f_ref, c_ref, s_ref):
        ang = p_ref[...].astype(F32) * f_ref[...]
        c_ref[...] = jnp.cos(ang)
        s_ref[...] = jnp.sin(ang)

    return pl.pallas_call(
        body, out_shape=(jax.ShapeDtypeStruct((s, half), F32), jax.ShapeDtypeStruct((s, half), F32)), name=name)(pos, inv_freq)


def _lb_softmax(logits):
    m = jnp.max(logits, axis=0, keepdims=True)
    e = jnp.exp(logits - m)
    return e / jnp.sum(e, axis=0, keepdims=True)


def _lb_fwd(logits, name):
    def body(l_ref, o_ref):
        p = _lb_softmax(l_ref[...])
        acc = jnp.zeros_like(p[0:1])
        o_ref[0:1, :] = acc
        for layer in range(1, DEPTH):
            acc = acc + p[layer:layer + 1]
            o_ref[layer:layer + 1, :] = acc

    return pl.pallas_call(body, out_shape=jax.ShapeDtypeStruct(logits.shape, F32), name=name)(logits)


def _lb_bwd(logits, dlb, name):
    def body(l_ref, d_ref, o_ref):
        p = _lb_softmax(l_ref[...])
        d = d_ref[...]
        dp = [jnp.zeros_like(d[0:1])] * DEPTH
        run = jnp.zeros_like(d[0:1])
        for layer in range(DEPTH - 1, 0, -1):
            run = run + d[layer:layer + 1]
            dp[layer] = run
        inner = sum(p[layer:layer + 1] * dp[layer] for layer in range(DEPTH))
        for layer in range(DEPTH):
            o_ref[layer:layer + 1, :] = p[layer:layer + 1] * (dp[layer] - inner)

    return pl.pallas_call(body, out_shape=jax.ShapeDtypeStruct(logits.shape, F32), name=name)(logits, dlb)


def _loss(y, target, name):
    s, d = y.shape
    ts = _row_tile(s, 512)

    def body(y_ref, t_ref, l_ref, dy_ref):
        e = y_ref[...] - t_ref[...]
        dy_ref[...] = e / d

        @pl.when(pl.program_id(0) == 0)
        def _():
            l_ref[...] = jnp.zeros_like(l_ref)

        l_ref[...] += 0.5 * jnp.sum(jnp.mean(e * e, axis=-1, keepdims=True), axis=0, keepdims=True)

    row = pl.BlockSpec((ts, d), lambda i: (i, 0))
    return pl.pallas_call(
        body, grid=(s // ts,), in_specs=[row, row], out_specs=(pl.BlockSpec((1, 1), lambda i: (0, 0)), row),
        out_shape=(jax.ShapeDtypeStruct((1, 1), F32), jax.ShapeDtypeStruct((s, d), F32)),
        compiler_params=_params("arbitrary"), name=name)(y, target)


def _rope(t1, t2, cos, sin):
    return t1 * cos - t2 * sin, t1 * sin + t2 * cos


def _rope_bwd(d1, d2, cos, sin):
    return d1 * cos + d2 * sin, d2 * cos - d1 * sin


def _mla_qkv(proj, qn, kvn, w_uq, w_ukv, cos, sin, name):
    s = proj.shape[0]
    ts = _row_tile(s, 256)
    hh, half = MLA_HEADS, MLA_ROPE // 2

    def body(p_ref, qn_ref, kvn_ref, wq_ref, wkv_ref, c_ref, s_ref, cq_ref, ckv_ref, q_ref, k_ref, v_ref):
        p = p_ref[...]
        cq, ckv, kr = p[:, :MLA_Q_LORA], p[:, MLA_Q_LORA:MLA_Q_LORA + MLA_KV_LORA], p[:, MLA_Q_LORA + MLA_KV_LORA:]
        cqn = (cq * _rstd(cq) * qn_ref[...]).astype(BF16)
        ckvn = (ckv * _rstd(ckv) * kvn_ref[...]).astype(BF16)
        cq_ref[...] = cqn
        ckv_ref[...] = ckvn
        qe = _dot(cqn, wq_ref[...])
        kve = _dot(ckvn, wkv_ref[...])
        cos_, sin_ = c_ref[...], s_ref[...]
        k1, k2 = _rope(kr[:, :half], kr[:, half:], cos_, sin_)
        k1, k2 = k1.astype(BF16), k2.astype(BF16)
        for h in range(hh):
            b = h * MLA_QK
            q_ref[h, :, 0:MLA_NOPE] = qe[:, b:b + MLA_NOPE].astype(BF16)
            q1, q2 = _rope(qe[:, b + MLA_NOPE:b + MLA_NOPE + half], qe[:, b + MLA_NOPE + half:b + MLA_QK], cos_, sin_)
            q_ref[h, :, MLA_NOPE:MLA_NOPE + half] = q1.astype(BF16)
            q_ref[h, :, MLA_NOPE + half:MLA_QK] = q2.astype(BF16)
            b = h * (MLA_NOPE + MLA_V)
            k_ref[h, :, 0:MLA_NOPE] = kve[:, b:b + MLA_NOPE].astype(BF16)
            k_ref[h, :, MLA_NOPE:MLA_NOPE + half] = k1
            k_ref[h, :, MLA_NOPE + half:MLA_QK] = k2
            v_ref[h] = kve[:, b + MLA_NOPE:b + MLA_NOPE + MLA_V].astype(BF16)

    def row(w):
        return pl.BlockSpec((ts, w), lambda i: (i, 0))

    def full(shape):
        return pl.BlockSpec(shape, lambda i: (0,) * len(shape))

    def heads(w):
        return pl.BlockSpec((hh, ts, w), lambda i: (0, i, 0))

    return pl.pallas_call(
        body, grid=(s // ts,),
        in_specs=[row(MLA_IN), full(qn.shape), full(kvn.shape), full(w_uq.shape), full(w_ukv.shape), row(half), row(half)],
        out_specs=(row(MLA_Q_LORA), row(MLA_KV_LORA), heads(MLA_QK), heads(MLA_QK), heads(MLA_V)),
        out_shape=(jax.ShapeDtypeStruct((s, MLA_Q_LORA), BF16), jax.ShapeDtypeStruct((s, MLA_KV_LORA), BF16),
                   jax.ShapeDtypeStruct((hh, s, MLA_QK), BF16), jax.ShapeDtypeStruct((hh, s, MLA_QK), BF16),
                   jax.ShapeDtypeStruct((hh, s, MLA_V), BF16)),
        compiler_params=_params("parallel"), name=name)(proj, qn, kvn, w_uq, w_ukv, cos, sin)


ATTN_BLOCK = 1024
ATTN_FWD_TILE = (256, 512)
ATTN_BWD_TILE = (512, 512)


def _attn_block(s):
    return _row_tile(s, ATTN_BLOCK)


def _tile_sees(diag, q0, tq, k0, tk):
    if not diag:
        return True, False
    return k0 <= q0 + tq - 1, k0 + tk - 1 > q0


def _causal_pairs(nb, kv_major):
    if kv_major:
        pairs = [(i, j) for j in range(nb) for i in range(j, nb)]
    else:
        pairs = [(i, j) for i in range(nb) for j in range(i + 1)]
    return (jnp.asarray(np.array([p[0] for p in pairs], np.int32)), jnp.asarray(np.array([p[1] for p in pairs], np.int32)))


def _ride(rider, step, total, refs):
    if rider is None:
        return

    @pl.when(step == 0)
    def _():
        rider.start(*refs)

    @pl.when(step == (total * 7) // 8)
    def _():
        rider.middle(*refs)

    @pl.when(step == total - 1)
    def _():
        rider.finish(*refs)


def _rider_specs(rider):
    if rider is None:
        return [], [], [], [], []
    return (list(rider.operands), [_HBM] * len(rider.operands), list(rider.out_shapes), [_HBM] * len(rider.out_shapes),
            list(rider.scratch))


def _attn_fwd(q, k, v, name, rider=None):
    hh, s, _ = q.shape
    blk = _attn_block(s)
    tq, tk = min(blk, ATTN_FWD_TILE[0]), min(blk, ATTN_FWD_TILE[1])
    nb = s // blk
    it, jt = _causal_pairs(nb, kv_major=False)
    npair = int(it.shape[0])
    scale = MLA_QK ** -0.5
    c2 = scale * LOG2_E
    r_in, r_in_specs, r_out, r_out_specs, r_scr = _rider_specs(rider)
    n_rin, n_rout, n_rscr = len(r_in), len(r_out), len(r_scr)

    def body(it_ref, jt_ref, q_ref, k_ref, v_ref, *refs):
        r_refs = refs[:n_rin] + refs[n_rin + 2:n_rin + 2 + n_rout] + refs[len(refs) - n_rscr:]
        o_ref, lse_ref = refs[n_rin:n_rin + 2]
        m_scr, acc_scr, v_scr = refs[n_rin + 2 + n_rout:n_rin + 2 + n_rout + 3]
        h, t = pl.program_id(0), pl.program_id(1)
        step = h * npair + t
        _ride(rider, step, hh * npair, r_refs)
        i, j = it_ref[t], jt_ref[t]

        @pl.when(j == 0)
        def _():
            m_scr[...] = jnp.full_like(m_scr, -jnp.inf)
            acc_scr[...] = jnp.zeros_like(acc_scr)
            v_scr[:, MLA_V:] = jnp.ones((blk, MLA_V), BF16)

        def block(diag):
            v_scr[:, :MLA_V] = v_ref[...]
            for k0 in range(0, blk, tk):
                kb, vb = k_ref[k0:k0 + tk, :], v_scr[k0:k0 + tk, :]
                for q0 in range(0, blk, tq):
                    visible, needs_mask = _tile_sees(diag, q0, tq, k0, tk)
                    if not visible:
                        continue
                    rows = slice(q0, q0 + tq)
                    sc = _dot(q_ref[rows, :], kb, NT)
                    if needs_mask:
                        qpos = q0 + lax.broadcasted_iota(jnp.int32, (tq, tk), 0)
                        kpos = k0 + lax.broadcasted_iota(jnp.int32, (tq, tk), 1)
                        sc = jnp.where(qpos >= kpos, sc, -jnp.inf)
                    m_prev = m_scr[rows, :]
                    m_new = jnp.maximum(m_prev, jnp.max(sc, axis=-1, keepdims=True))
                    alpha = jnp.exp2((m_prev - m_new) * c2)
                    p = jnp.exp2((sc - m_new) * c2)
                    acc_scr[rows, :] = alpha * acc_scr[rows, :] + _dot(p.astype(BF16), vb)
                    m_scr[rows, :] = m_new

        @pl.when(j < i)
        def _():
            block(False)

        @pl.when(j == i)
        def _():
            block(True)
            acc = acc_scr[...]
            l = acc[:, MLA_V:MLA_V + 1]
            o_ref[...] = acc[:, :MLA_V] / l
            lse_ref[...] = m_scr[...] * scale + jnp.log(l)

    grid_spec = pltpu.PrefetchScalarGridSpec(
        num_scalar_prefetch=2, grid=(hh, npair),
        in_specs=[pl.BlockSpec((None, blk, MLA_QK), lambda h, t, it_, jt_: (h, it_[t], 0)),
                  pl.BlockSpec((None, blk, MLA_QK), lambda h, t, it_, jt_: (h, jt_[t], 0)),
                  pl.BlockSpec((None, blk, MLA_V), lambda h, t, it_, jt_: (h, jt_[t], 0))] + r_in_specs,
        out_specs=[pl.BlockSpec((blk, MLA_V), lambda h, t, it_, jt_: (it_[t], h)),
                   pl.BlockSpec((None, blk, 1), lambda h, t, it_, jt_: (h, it_[t], 0))] + r_out_specs,
        scratch_shapes=[pltpu.VMEM((blk, 1), F32), pltpu.VMEM((blk, 2 * MLA_V), F32),
                        pltpu.VMEM((blk, 2 * MLA_V), BF16)] + r_scr)
    out = pl.pallas_call(
        body, grid_spec=grid_spec,
        out_shape=[jax.ShapeDtypeStruct((s, hh * MLA_V), F32), jax.ShapeDtypeStruct((hh, s, 1), F32)] + r_out,
        compiler_params=_params("arbitrary", "arbitrary"), name=name)(it, jt, q, k, v, *r_in)
    return out[0], out[1], out[2:]


def _attn_delta(do, o, name):
    s = do.shape[0]
    ts = _row_tile(s, 512)
    hh = MLA_HEADS

    def body(do_ref, o_ref, d_ref):
        prod = do_ref[...] * o_ref[...]
        for h in range(hh):
            d_ref[h] = jnp.sum(prod[:, h * MLA_V:(h + 1) * MLA_V], axis=-1, keepdims=True)

    row = pl.BlockSpec((ts, hh * MLA_V), lambda i: (i, 0))
    return pl.pallas_call(
        body, grid=(s // ts,), in_specs=[row, row], out_specs=pl.BlockSpec((hh, ts, 1), lambda i: (0, i, 0)),
        out_shape=jax.ShapeDtypeStruct((hh, s, 1), F32), compiler_params=_params("parallel"), name=name)(do, o)


def _attn_bwd(q, k, v, do, lse_row, delta_row, name, rider=None):
    hh, s, _ = q.shape
    blk = _attn_block(s)
    tq, tk = min(blk, ATTN_BWD_TILE[0]), min(blk, ATTN_BWD_TILE[1])
    nb = s // blk
    it, jt = _causal_pairs(nb, kv_major=True)
    npair = int(it.shape[0])
    scale = MLA_QK ** -0.5
    c2 = scale * LOG2_E
    r_in, r_in_specs, r_out, r_out_specs, r_scr = _rider_specs(rider)
    n_rin, n_rout, n_rscr = len(r_in), len(r_out), len(r_scr)

    def body(it_ref, jt_ref, q_ref, k_ref, v_ref, do_ref, lse_ref, dl_ref, *refs):
        r_refs = refs[:n_rin] + refs[n_rin + 3:n_rin + 3 + n_rout] + refs[len(refs) - n_rscr:]
        dq_ref, dk_ref, dv_ref = refs[n_rin:n_rin + 3]
        h, t = pl.program_id(0), pl.program_id(1)
        step = h * npair + t
        _ride(rider, step, hh * npair, r_refs)
        i, j = it_ref[t], jt_ref[t]

        @pl.when(t == 0)
        def _():
            dq_ref[...] = jnp.zeros_like(dq_ref)

        def block(diag):
            if diag:
                dk_ref[...] = jnp.zeros_like(dk_ref)
                dv_ref[...] = jnp.zeros_like(dv_ref)
            for q0 in range(0, blk, tq):
                qb = q_ref[q0:q0 + tq, :]
                dob = do_ref[q0:q0 + tq, :].astype(BF16)
                lse2 = lse_ref[:, q0:q0 + tq] * LOG2_E
                dl = dl_ref[:, q0:q0 + tq]
                dq = None
                for k0 in range(0, blk, tk):
                    visible, needs_mask = _tile_sees(diag, q0, tq, k0, tk)
                    if not visible:
                        continue
                    kb, vb = k_ref[k0:k0 + tk, :], v_ref[k0:k0 + tk, :]
                    pt = jnp.exp2(_dot(kb, qb, NT) * c2 - lse2)
                    if needs_mask:
                        kpos = k0 + lax.broadcasted_iota(jnp.int32, (tk, tq), 0)
                        qpos = q0 + lax.broadcasted_iota(jnp.int32, (tk, tq), 1)
                        pt = jnp.where(qpos >= kpos, pt, 0.0)
                    dv_ref[k0:k0 + tk, :] += _dot(pt.astype(BF16), dob)
                    dpt = _dot(vb, dob, NT)
                    dst = (pt * (dpt - dl) * scale).astype(BF16)
                    dk_ref[k0:k0 + tk, :] += _dot(dst, qb)
                    part = _dot(dst, kb, TN)
                    dq = part if dq is None else dq + part
                rows = pl.ds(pl.multiple_of(i * blk + q0, tq), tq)
                dq_ref[rows, :] += dq

        @pl.when(i == j)
        def _():
            block(True)

        @pl.when(i > j)
        def _():
            block(False)

    grid_spec = pltpu.PrefetchScalarGridSpec(
        num_scalar_prefetch=2, grid=(hh, npair),
        in_specs=[pl.BlockSpec((None, blk, MLA_QK), lambda h, t, it_, jt_: (h, it_[t], 0)),
                  pl.BlockSpec((None, blk, MLA_QK), lambda h, t, it_, jt_: (h, jt_[t], 0)),
                  pl.BlockSpec((None, blk, MLA_V), lambda h, t, it_, jt_: (h, jt_[t], 0)),
                  pl.BlockSpec((blk, MLA_V), lambda h, t, it_, jt_: (it_[t], h)),
                  pl.BlockSpec((None, 1, blk), lambda h, t, it_, jt_: (h, 0, it_[t])),
                  pl.BlockSpec((None, 1, blk), lambda h, t, it_, jt_: (h, 0, it_[t]))] + r_in_specs,
        out_specs=[pl.BlockSpec((None, s, MLA_QK), lambda h, t, it_, jt_: (h, 0, 0)),
                   pl.BlockSpec((None, blk, MLA_QK), lambda h, t, it_, jt_: (h, jt_[t], 0)),
                   pl.BlockSpec((None, blk, MLA_V), lambda h, t, it_, jt_: (h, jt_[t], 0))] + r_out_specs,
        scratch_shapes=r_scr)
    out = pl.pallas_call(
        body, grid_spec=grid_spec,
        out_shape=[jax.ShapeDtypeStruct((hh, s, MLA_QK), F32), jax.ShapeDtypeStruct((hh, s, MLA_QK), F32),
                   jax.ShapeDtypeStruct((hh, s, MLA_V), F32)] + r_out,
        compiler_params=_params("arbitrary", "arbitrary"), name=name)(it, jt, q, k, v, do, lse_row, delta_row, *r_in)
    return out[0], out[1], out[2], out[3:]


def _mla_bwd_mid(dq, dk, dv, cos, sin, proj, qn, kvn, w_uq, w_ukv, name):
    s = proj.shape[0]
    ts = _row_tile(s, 256)
    hh, half = MLA_HEADS, MLA_ROPE // 2
    nq, nkv = hh * MLA_QK, hh * (MLA_NOPE + MLA_V)

    def body(dq_ref, dk_ref, dv_ref, c_ref, s_ref, p_ref, qn_ref, kvn_ref, wq_ref, wkv_ref,
             dqe_ref, dkve_ref, dp_ref, dqn_ref, dkvn_ref):
        cos_, sin_ = c_ref[...], s_ref[...]
        dkr1 = jnp.zeros((ts, half), F32)
        dkr2 = jnp.zeros((ts, half), F32)
        for h in range(hh):
            dqh, dkh = dq_ref[h], dk_ref[h]
            b = h * MLA_QK
            dqe_ref[:, b:b + MLA_NOPE] = dqh[:, :MLA_NOPE].astype(BF16)
            d1, d2 = _rope_bwd(dqh[:, MLA_NOPE:MLA_NOPE + half], dqh[:, MLA_NOPE + half:], cos_, sin_)
            dqe_ref[:, b + MLA_NOPE:b + MLA_NOPE + half] = d1.astype(BF16)
            dqe_ref[:, b + MLA_NOPE + half:b + MLA_QK] = d2.astype(BF16)
            b = h * (MLA_NOPE + MLA_V)
            dkve_ref[:, b:b + MLA_NOPE] = dkh[:, :MLA_NOPE].astype(BF16)
            dkve_ref[:, b + MLA_NOPE:b + MLA_NOPE + MLA_V] = dv_ref[h].astype(BF16)
            dkr1 = dkr1 + dkh[:, MLA_NOPE:MLA_NOPE + half]
            dkr2 = dkr2 + dkh[:, MLA_NOPE + half:]
        dkr1, dkr2 = _rope_bwd(dkr1, dkr2, cos_, sin_)
        dcqn = _dot(dqe_ref[...], wq_ref[...], NT)
        dckvn = _dot(dkve_ref[...], wkv_ref[...], NT)
        p = p_ref[...]
        dcq, dqn = _rms_bwd_rows(p[:, :MLA_Q_LORA], qn_ref[...], dcqn)
        dckv, dkvn = _rms_bwd_rows(p[:, MLA_Q_LORA:MLA_Q_LORA + MLA_KV_LORA], kvn_ref[...], dckvn)
        dp_ref[:, :MLA_Q_LORA] = dcq.astype(BF16)
        dp_ref[:, MLA_Q_LORA:MLA_Q_LORA + MLA_KV_LORA] = dckv.astype(BF16)
        dp_ref[:, MLA_Q_LORA + MLA_KV_LORA:MLA_Q_LORA + MLA_KV_LORA + half] = dkr1.astype(BF16)
        dp_ref[:, MLA_Q_LORA + MLA_KV_LORA + half:] = dkr2.astype(BF16)

        @pl.when(pl.program_id(0) == 0)
        def _():
            dqn_ref[...] = jnp.zeros_like(dqn_ref)
            dkvn_ref[...] = jnp.zeros_like(dkvn_ref)

        dqn_ref[...] += jnp.sum(dqn, axis=0, keepdims=True)
        dkvn_ref[...] += jnp.sum(dkvn, axis=0, keepdims=True)

    def row(w):
        return pl.BlockSpec((ts, w), lambda i: (i, 0))

    def full(shape):
        return pl.BlockSpec(shape, lambda i: (0,) * len(shape))

    def heads(w):
        return pl.BlockSpec((hh, ts, w), lambda i: (0, i, 0))

    return pl.pallas_call(
        body, grid=(s // ts,),
        in_specs=[heads(MLA_QK), heads(MLA_QK), heads(MLA_V), row(half), row(half), row(MLA_IN),
                  full(qn.shape), full(kvn.shape), full(w_uq.shape), full(w_ukv.shape)],
        out_specs=(row(nq), row(nkv), row(MLA_IN), full(qn.shape), full(kvn.shape)),
        out_shape=(jax.ShapeDtypeStruct((s, nq), BF16), jax.ShapeDtypeStruct((s, nkv), BF16),
                   jax.ShapeDtypeStruct((s, MLA_IN), BF16), jax.ShapeDtypeStruct(qn.shape, F32),
                   jax.ShapeDtypeStruct(kvn.shape, F32)),
        compiler_params=_params("arbitrary"), name=name)(dq, dk, dv, cos, sin, proj, qn, kvn, w_uq, w_ukv)


HGRN_TILE = 128


def _chunk_masks(t):
    r = lax.broadcasted_iota(jnp.int32, (t, t), 0)
    c = lax.broadcasted_iota(jnp.int32, (t, t), 1)
    same = (r // HGRN_CHUNK) == (c // HGRN_CHUNK)
    return r, c, same


def _hgrn_gates(p, lb):
    hk = HGRN_HEADS * HGRN_D
    qx, fx, ix, gx = p[:, :hk], p[:, hk:2 * hk], p[:, 2 * hk:3 * hk], p[:, 3 * hk:]
    sig_f = jax.nn.sigmoid(fx)
    f = lb + (1.0 - lb) * sig_f
    sig_q = jax.nn.sigmoid(qx)
    t = p.shape[0]
    r, c, same = _chunk_masks(t)
    lower = jnp.where(same & (c <= r), 1.0, 0.0).astype(F32)
    b = _dot_f32(lower, jnp.log(f))
    b3 = b.reshape(t // HGRN_CHUNK, HGRN_CHUNK, hk)
    bref = jnp.broadcast_to(b3[:, HGRN_CHUNK // 2:HGRN_CHUNK // 2 + 1, :], b3.shape).reshape(t, hk)
    blast = jnp.broadcast_to(b3[:, HGRN_CHUNK - 1:, :], b3.shape).reshape(t, hk)
    return qx, ix, gx, sig_f, f, sig_q, b, bref, blast


def _hgrn_fwd(proj, lb, onorm, name, rider=None):
    s = proj.shape[0]
    t = _row_tile(s, HGRN_TILE)
    nc = t // HGRN_CHUNK
    hh, dd, hk = HGRN_HEADS, HGRN_D, HGRN_HEADS * HGRN_D
    r_in, r_in_specs, r_out, r_out_specs, r_scr = _rider_specs(rider)
    n_rin, n_rout = len(r_in), len(r_out)

    def body(p_ref, lb_ref, on_ref, *refs):
        y_ref, o_ref, st_ref = refs[n_rin:n_rin + 3]
        st_scr = refs[n_rin + 3 + n_rout]
        _ride(rider, pl.program_id(0), s // t, refs[:n_rin] + refs[n_rin + 3:n_rin + 3 + n_rout] + refs[n_rin + 4 + n_rout:])

        @pl.when(pl.program_id(0) == 0)
        def _():
            st_scr[...] = jnp.zeros_like(st_scr)

        qx, ix, gx, _, f, sig_q, b, bref, blast = _hgrn_gates(p_ref[...], lb_ref[...])
        q = qx * sig_q
        k = 1.0 - f
        r, c, same = _chunk_masks(t)
        causal = same & (c <= r)
        for h in range(hh):
            sl = slice(h * dd, (h + 1) * dd)
            bh, brefh, blasth, qh, kh = b[:, sl], bref[:, sl], blast[:, sl], q[:, sl], k[:, sl]
            vh = ix[:, sl].astype(BF16)
            q_rel = (qh * jnp.exp(bh - brefh)).astype(BF16)
            k_rel = (kh * jnp.exp(brefh - bh)).astype(BF16)
            a = jnp.where(causal, _dot(q_rel, k_rel, NT), 0.0)
            o_intra = _dot(a.astype(BF16), vh)
            q_dec = (qh * jnp.exp(bh)).astype(BF16)
            k_dec = (kh * jnp.exp(blasth - bh)).astype(BF16)
            dec = jnp.exp(blasth)
            pieces = []
            for ci in range(nc):
                rows = slice(ci * HGRN_CHUNK, (ci + 1) * HGRN_CHUNK)
                st = st_scr[h]
                st_ref[ci, h] = st
                pieces.append(_dot(q_dec[rows], st.astype(BF16), NT))
                st_scr[h] = st * dec[ci * HGRN_CHUNK:ci * HGRN_CHUNK + 1, :] + _dot(vh[rows], k_dec[rows], TN)
            oh = o_intra + jnp.concatenate(pieces, axis=0)
            o_ref[:, sl] = oh
            gate = gx[:, sl] * jax.nn.sigmoid(gx[:, sl])
            y_ref[:, sl] = (oh * _rstd(oh) * on_ref[...] * gate).astype(BF16)

    out = pl.pallas_call(
        body, grid=(s // t,),
        in_specs=[pl.BlockSpec((t, 4 * hk), lambda i: (i, 0)), pl.BlockSpec((1, hk), lambda i: (0, 0)),
                  pl.BlockSpec((1, dd), lambda i: (0, 0))] + r_in_specs,
        out_specs=[pl.BlockSpec((t, hk), lambda i: (i, 0)), pl.BlockSpec((t, hk), lambda i: (i, 0)),
                   pl.BlockSpec((nc, hh, dd, dd), lambda i: (i, 0, 0, 0))] + r_out_specs,
        out_shape=[jax.ShapeDtypeStruct((s, hk), BF16), jax.ShapeDtypeStruct((s, hk), F32),
                   jax.ShapeDtypeStruct((s // HGRN_CHUNK, hh, dd, dd), F32)] + r_out,
        scratch_shapes=[pltpu.VMEM((hh, dd, dd), F32)] + r_scr,
        compiler_params=_params("arbitrary"), name=name)(proj, lb, onorm, *r_in)
    return out[0], out[1], out[2], out[3:]


def _hgrn_bwd(proj, lb, onorm, o, states, dy, name, rider=None):
    s = proj.shape[0]
    t = _row_tile(s, HGRN_TILE)
    nt = s // t
    nc = t // HGRN_CHUNK
    hh, dd, hk = HGRN_HEADS, HGRN_D, HGRN_HEADS * HGRN_D
    r_in, r_in_specs, r_out, r_out_specs, r_scr = _rider_specs(rider)
    n_rin, n_rout, n_rscr = len(r_in), len(r_out), len(r_scr)

    def body(p_ref, lb_ref, on_ref, o_ref, st_ref, dy_ref, *refs):
        r_refs = refs[:n_rin] + refs[n_rin + 3:n_rin + 3 + n_rout] + refs[len(refs) - n_rscr:]
        dp_ref, dlb_ref, don_ref = refs[n_rin:n_rin + 3]
        dst_scr, cat_scr, ext_scr, dk_scr, dq_scr = refs[n_rin + 3 + n_rout:n_rin + 3 + n_rout + 5]
        _ride(rider, pl.program_id(0), nt, r_refs)

        @pl.when(pl.program_id(0) == 0)
        def _():
            dst_scr[...] = jnp.zeros_like(dst_scr)
            dlb_ref[...] = jnp.zeros_like(dlb_ref)
            don_ref[...] = jnp.zeros_like(don_ref)

        lbv = lb_ref[...]
        qx, ix, gx, sig_f, f, sig_q, b, bref, blast = _hgrn_gates(p_ref[...], lbv)
        q = qx * sig_q
        k = 1.0 - f
        r, c, same = _chunk_masks(t)
        causal = same & (c <= r)
        on = on_ref[...]
        don = jnp.zeros((1, dd), F32)
        for h in range(hh):
            sl = slice(h * dd, (h + 1) * dd)
            oh = o_ref[:, sl]
            dyh = dy_ref[:, sl]
            gxh = gx[:, sl]
            sig_g = jax.nn.sigmoid(gxh)
            rs = _rstd(oh)
            dgate = dyh * (oh * rs * on)
            dp_ref[:, 3 * hk + h * dd:3 * hk + (h + 1) * dd] = (dgate * (sig_g * (1.0 + gxh * (1.0 - sig_g)))).astype(BF16)
            do, donh = _rms_bwd_rows(oh, on, dyh * (gxh * sig_g))
            don = don + jnp.sum(donh, axis=0, keepdims=True)
            dob = do.astype(BF16)
            bh, brefh, blasth, qh, kh = b[:, sl], bref[:, sl], blast[:, sl], q[:, sl], k[:, sl]
            vh = ix[:, sl].astype(BF16)
            e_qr, e_kr, e_qd, e_kd = jnp.exp(bh - brefh), jnp.exp(brefh - bh), jnp.exp(bh), jnp.exp(blasth - bh)
            dec = jnp.exp(blasth)
            q_rel, k_rel, q_dec, k_dec = qh * e_qr, kh * e_kr, qh * e_qd, kh * e_kd
            q_relb, k_relb, q_decb, k_decb = q_rel.astype(BF16), k_rel.astype(BF16), q_dec.astype(BF16), k_dec.astype(BF16)
            a = jnp.where(causal, _dot(q_relb, k_relb, NT), 0.0).astype(BF16)
            dv = _dot(a, dob, TN)
            da = jnp.where(causal, _dot(dob, vh, NT), 0.0).astype(BF16)
            dq_rel = _dot(da, k_relb)
            dk_rel = _dot(da, q_relb, TN)
            dq_dec, dk_dec, dv_inter, ddec = [None] * nc, [None] * nc, [None] * nc, [None] * nc
            for ci in range(nc - 1, -1, -1):
                rows = slice(ci * HGRN_CHUNK, (ci + 1) * HGRN_CHUNK)
                st = st_ref[ci, h]
                dst = dst_scr[h]
                dstb = dst.astype(BF16)
                dq_dec[ci] = _dot(dob[rows], st.astype(BF16))
                dk_dec[ci] = _dot(vh[rows], dstb)
                dv_inter[ci] = _dot(k_decb[rows], dstb, NT)
                ddec[ci] = jnp.broadcast_to(jnp.sum(dst * st, axis=0, keepdims=True), (HGRN_CHUNK, dd))
                dst_scr[h] = dst * dec[ci * HGRN_CHUNK:ci * HGRN_CHUNK + 1, :] + _dot(dob[rows], q_decb[rows], TN)
            dq_dec = jnp.concatenate(dq_dec, axis=0)
            dk_dec = jnp.concatenate(dk_dec, axis=0)
            dv = dv + jnp.concatenate(dv_inter, axis=0)
            ddec = jnp.concatenate(ddec, axis=0)
            dp_ref[:, 2 * hk + h * dd:2 * hk + (h + 1) * dd] = dv.astype(BF16)
            dq_scr[:, sl] = dq_rel * e_qr + dq_dec * e_qd
            dk_scr[:, sl] = dk_rel * e_kr + dk_dec * e_kd
            g_qr, g_kr, g_qd, g_kd = dq_rel * q_rel, dk_rel * k_rel, dq_dec * q_dec, dk_dec * k_dec
            cat_scr[0:t, sl] = g_qr - g_kr + g_qd - g_kd
            cat_scr[t:2 * t, sl] = g_kr - g_qr
            cat_scr[2 * t:3 * t, sl] = g_kd
            ext_scr[:, sl] = ddec * dec
        upper = jnp.where(same & (c >= r), 1.0, 0.0).astype(F32)
        to_ref = jnp.where(same & (r % HGRN_CHUNK <= HGRN_CHUNK // 2), 1.0, 0.0).astype(F32)
        to_all = jnp.where(same, 1.0, 0.0).astype(F32)
        dlogf = _dot_f32(jnp.concatenate([upper, to_ref, to_all], axis=1), cat_scr[...]) + ext_scr[...]
        df = dlogf / f - dk_scr[...]
        dp_ref[:, hk:2 * hk] = (df * (1.0 - lbv) * sig_f * (1.0 - sig_f)).astype(BF16)
        dp_ref[:, 0:hk] = (dq_scr[...] * (sig_q * (1.0 + qx * (1.0 - sig_q)))).astype(BF16)
        dlb_ref[...] += jnp.sum(df * (1.0 - sig_f), axis=0, keepdims=True)
        don_ref[...] += don

    def rev(i):
        return nt - 1 - i

    out = pl.pallas_call(
        body, grid=(nt,),
        in_specs=[pl.BlockSpec((t, 4 * hk), lambda i: (rev(i), 0)), pl.BlockSpec((1, hk), lambda i: (0, 0)),
                  pl.BlockSpec((1, dd), lambda i: (0, 0)), pl.BlockSpec((t, hk), lambda i: (rev(i), 0)),
                  pl.BlockSpec((nc, hh, dd, dd), lambda i: (rev(i), 0, 0, 0)),
                  pl.BlockSpec((t, hk), lambda i: (rev(i), 0))] + r_in_specs,
        out_specs=[pl.BlockSpec((t, 4 * hk), lambda i: (rev(i), 0)), pl.BlockSpec((1, hk), lambda i: (0, 0)),
                   pl.BlockSpec((1, dd), lambda i: (0, 0))] + r_out_specs,
        out_shape=[jax.ShapeDtypeStruct((s, 4 * hk), BF16), jax.ShapeDtypeStruct((1, hk), F32),
                   jax.ShapeDtypeStruct((1, dd), F32)] + r_out,
        scratch_shapes=[pltpu.VMEM((hh, dd, dd), F32), pltpu.VMEM((3 * t, hk), F32), pltpu.VMEM((t, hk), F32),
                        pltpu.VMEM((t, hk), F32), pltpu.VMEM((t, hk), F32)] + r_scr,
        compiler_params=_params("arbitrary"), name=name)(proj, lb, onorm, o, states, dy, *r_in)
    return out[0], out[1], out[2], out[3:]


def _adamw_update(w, g, m, v):
    nm = ADAM_B1 * m + (1.0 - ADAM_B1) * g
    nv = ADAM_B2 * v + (1.0 - ADAM_B2) * (g * g)
    m_hat = nm / (1.0 - ADAM_B1 ** ADAM_STEP)
    v_hat = nv / (1.0 - ADAM_B2 ** ADAM_STEP)
    return -ADAM_LR * (m_hat / (jnp.sqrt(v_hat) + ADAM_EPS) + ADAM_WD * w), nm, nv


def _adamw(w, g, m, v, name):
    rows, cols = w.shape
    tr = _divisor_tile(rows, 256, 8)

    def body(w_ref, g_ref, m_ref, v_ref, d_ref, nm_ref, nv_ref):
        d_ref[...], nm_ref[...], nv_ref[...] = _adamw_update(w_ref[...], g_ref[...], m_ref[...], v_ref[...])

    blk = pl.BlockSpec((tr, cols), lambda i: (i, 0))
    shp = jax.ShapeDtypeStruct((rows, cols), F32)
    return pl.pallas_call(
        body, grid=(rows // tr,), in_specs=[blk] * 4, out_specs=(blk,) * 3, out_shape=(shp,) * 3,
        compiler_params=_params("parallel"), name=name)(w, g, m, v)


ADAMW_BLOCK_ELEMS = 128 * 1024


def _adamw_layers(w, lands, m, v, name):
    ll, rows, cols = w.shape
    tr = _divisor_tile(rows, max(16, ADAMW_BLOCK_ELEMS // cols), 16)

    def body(w_ref, m_ref, v_ref, *refs):
        land_refs, (g_out, d_ref, nm_ref, nv_ref) = refs[:ll], refs[ll:]
        layer = pl.program_id(0)
        for k in range(ll):
            @pl.when(layer == k)
            def _(k=k):
                g = land_refs[k][0].astype(F32)
                for slot in range(1, N_DEV):
                    g = g + land_refs[k][slot].astype(F32)
                g_out[...] = g

        d_ref[...], nm_ref[...], nv_ref[...] = _adamw_update(w_ref[...], g_out[...], m_ref[...], v_ref[...])

    stacked = pl.BlockSpec((None, tr, cols), lambda l, i: (l, i, 0))

    def one(k):
        return pl.BlockSpec((N_DEV, tr, cols), lambda l, i: (0, jnp.where(l == k, i, 0), 0))

    shp = jax.ShapeDtypeStruct(w.shape, F32)
    return pl.pallas_call(
        body, grid=(ll, rows // tr), in_specs=[stacked] * 3 + [one(k) for k in range(ll)], out_specs=(stacked,) * 4,
        out_shape=(shp,) * 4, compiler_params=_params("arbitrary", "arbitrary"), name=name)(w, m, v, *lands)


_HBM = pl.BlockSpec(memory_space=pltpu.HBM)
_MESH = pl.DeviceIdType.MESH


class _GatherRide:
    def __init__(self, blocks, cuts):
        self.operands = list(blocks)
        self.cuts = list(cuts)
        self.out_shapes = []
        for b, cut in zip(blocks, cuts):
            r, c = b.shape
            shape = {"rows": (N_DEV * r, c), "cols": (r, N_DEV * c), "slots": (N_DEV, r, c)}[cut]
            self.out_shapes.append(jax.ShapeDtypeStruct(shape, b.dtype))
        n = len(blocks)
        self.scratch = [pltpu.SemaphoreType.DMA((7 * n,)), pltpu.SemaphoreType.DMA((7 * n,)), pltpu.SemaphoreType.DMA((n,))]

    def _parts(self, *refs):
        n = len(self.operands)
        x_refs, out_refs = refs[:n], refs[n:2 * n]
        send_sems, recv_sems, local_sems = refs[2 * n:]
        x, y, c = lax.axis_index("x"), lax.axis_index("y"), lax.axis_index("c")
        me, sibling = (x, y, c), (x, y, 1 - c)
        chips = [(1 - x, y), (x, 1 - y), (1 - x, 1 - y)]
        mine, first, passed, landed, from_sibling = [], [], [], [], []
        for e in range(n):
            x_ref, out_ref, cut = x_refs[e], out_refs[e], self.cuts[e]
            r, cc = x_ref.shape

            def place(px, py, pc, out_ref=out_ref, cut=cut, r=r, cc=cc):
                p = 4 * px + 2 * py + pc
                if cut == "rows":
                    return out_ref.at[pl.ds(pl.multiple_of(p * r, r), r), :]
                if cut == "cols":
                    return out_ref.at[:, pl.ds(pl.multiple_of(p * cc, cc), cc)]
                return out_ref.at[p]

            def copy(k, block, to, src=None, place=place, e=e):
                return pltpu.make_async_remote_copy(
                    src_ref=place(*block) if src is None else src, dst_ref=place(*block), send_sem=send_sems.at[7 * e + k],
                    recv_sem=recv_sems.at[7 * e + k], device_id=to, device_id_type=_MESH)

            mine.append(pltpu.make_async_copy(x_ref, place(*me), local_sems.at[e]))
            first += [copy(0, me, sibling, src=x_ref)] + [copy(1 + j, me, (*chip, c), src=x_ref) for j, chip in enumerate(chips)]
            passed += [copy(4 + j, (*chip, c), sibling) for j, chip in enumerate(chips)]
            landed += [copy(1 + j, (*chip, c), me) for j, chip in enumerate(chips)]
            from_sibling += [copy(0, sibling, me)] + [copy(4 + j, (*chip, 1 - c), me) for j, chip in enumerate(chips)]
        return mine, first, passed, landed, from_sibling

    def start(self, *refs):
        mine, first, _, _, _ = self._parts(*refs)
        for cp in mine + first:
            cp.start()

    def middle(self, *refs):
        _, _, passed, landed, _ = self._parts(*refs)
        for got, fwd in zip(landed, passed):
            got.wait_recv()
            fwd.start()

    def finish(self, *refs):
        mine, first, passed, _, from_sibling = self._parts(*refs)
        for cp in from_sibling:
            cp.wait_recv()
        for cp in first + passed:
            cp.wait_send()
        for cp in mine:
            cp.wait()


class _ExchangeRide:
    def __init__(self, sends):
        self.operands = list(sends)
        self.out_shapes = [jax.ShapeDtypeStruct(s.shape, s.dtype) for s in sends]
        n = len(sends)
        self.scratch = [pltpu.SemaphoreType.DMA((7 * n,)), pltpu.SemaphoreType.DMA((7 * n,)), pltpu.SemaphoreType.DMA((n,))]

    def _parts(self, *refs):
        n = len(self.operands)
        s_refs, land_refs = refs[:n], refs[n:2 * n]
        send_sems, recv_sems, local_sems = refs[2 * n:]
        x, y, c = lax.axis_index("x"), lax.axis_index("y"), lax.axis_index("c")
        me = 4 * x + 2 * y + c
        own, sends, recvs = [], [], []
        for e in range(n):
            s_ref, land_ref = s_refs[e], land_refs[e]
            own.append(pltpu.make_async_copy(s_ref.at[me], land_ref.at[me], local_sems.at[e]))
            for rel in range(1, N_DEV):
                px = 1 - x if rel & 4 else x
                py = 1 - y if rel & 2 else y
                pc = 1 - c if rel & 1 else c
                peer = 4 * px + 2 * py + pc
                k = 7 * e + rel - 1
                sends.append(pltpu.make_async_remote_copy(
                    src_ref=s_ref.at[peer], dst_ref=land_ref.at[me], send_sem=send_sems.at[k], recv_sem=recv_sems.at[k],
                    device_id=(px, py, pc), device_id_type=_MESH))
                recvs.append(pltpu.make_async_remote_copy(
                    src_ref=s_ref.at[me], dst_ref=land_ref.at[peer], send_sem=send_sems.at[k], recv_sem=recv_sems.at[k],
                    device_id=(px, py, pc), device_id_type=_MESH))
        return own, sends, recvs

    def start(self, *refs):
        own, sends, _ = self._parts(*refs)
        for cp in own + sends:
            cp.start()

    def middle(self, *refs):
        pass

    def finish(self, *refs):
        own, sends, recvs = self._parts(*refs)
        for cp in recvs:
            cp.wait_recv()
        for cp in sends:
            cp.wait_send()
        for cp in own:
            cp.wait()


def _run_alone(rider, name):
    def body(*refs):
        rider.start(*refs)
        rider.middle(*refs)
        rider.finish(*refs)

    return pl.pallas_call(
        body, out_shape=rider.out_shapes, in_specs=[_HBM] * len(rider.operands), out_specs=[_HBM] * len(rider.out_shapes),
        scratch_shapes=rider.scratch, name=name)(*rider.operands)


def _all_gather(xs, name):
    return _run_alone(_GatherRide([xs], ["slots"]), name)[0]


def _sum_slots(parts, name):
    _, rows, cols = parts.shape
    tr = _divisor_tile(rows, 256, 16)

    def body(p_ref, o_ref):
        acc = p_ref[0].astype(F32)
        for slot in range(1, N_DEV):
            acc = acc + p_ref[slot].astype(F32)
        o_ref[...] = acc

    return pl.pallas_call(
        body, grid=(rows // tr,), in_specs=[pl.BlockSpec((N_DEV, tr, cols), lambda i: (0, i, 0))],
        out_specs=pl.BlockSpec((tr, cols), lambda i: (i, 0)), out_shape=jax.ShapeDtypeStruct((rows, cols), F32),
        compiler_params=_params("parallel"), name=name)(parts)


def _carry(rode, key, riders, call):
    rider = riders.get(key)
    res = call(rider)
    if rider is None:
        return res
    res, rode[key] = res
    return res


def _kept(rode, key, riders, brought):
    if key in riders:
        rode[key] = brought


def _mlp_fwd(h, g_pre, g_post, w1, w2, tag, riders):
    rode = {}
    (a, u, r2), brought = _norm_mm(h, g_pre, w1, 512, 1024, f"{tag}_up", out_dtypes=(F32, BF16),
                                   epi=lambda acc: (acc, jnp.square(jnp.maximum(acc, 0.0))), rider=riders.get("up"))
    _kept(rode, "up", riders, brought)
    z, out, brought = _mm_norm_res(r2, w2, g_post, h, 512, f"{tag}_down", riders.get("down"))
    _kept(rode, "down", riders, brought)
    return out, (h, a, u, r2, z), rode


def _mlp_bwd(dh, saved, g_pre, g_post, w1, w2, tag, riders):
    h, a, u, r2, z = saved
    rode = {}
    (dz, dg_post, du), brought = _rmsbwd_mm(
        z, g_post, dh, w2, 512, 1024, f"{tag}_ddown", out_dtypes=(BF16,), extras=(u,),
        epi=lambda acc, uu: (acc * (2.0 * jnp.maximum(uu, 0.0)),), rider=riders.get("ddown"))
    _kept(rode, "ddown", riders, brought)
    dw2 = _carry(rode, "dw2", riders, lambda r: _mm(
        r2, dz, "tn", 512, 512, f"{tag}_dw2", out_dtypes=(BF16,), shard="rows", rider=r))
    dw1 = _mm(a, du, "tn", 512, 512, f"{tag}_dw1", out_dtypes=(BF16,), shard="cols")
    dh_in, dg_pre = _mm_rmsbwd_res(du, w1, h, g_pre, dh, 512, f"{tag}_dup")
    return dh_in, dg_pre, dg_post, dw1, dw2, rode


def _hgrn_layer_fwd(h, g_pre, g_post, lb, onorm, w_in, w_o, tag, riders):
    rode = {}
    (a, proj), brought = _norm_mm(h, g_pre, w_in, 512, 1024, f"{tag}_in", rider=riders.get("in"))
    _kept(rode, "in", riders, brought)
    y, o, states, brought = _hgrn_fwd(proj, lb, onorm, f"{tag}_scan", riders.get("scan"))
    _kept(rode, "scan", riders, brought)
    m, out, _ = _mm_norm_res(y, w_o, g_post, h, 512, f"{tag}_o")
    return out, (h, a, proj, y, o, states, m), rode


def _hgrn_layer_bwd(dh, saved, g_pre, g_post, lb, onorm, w_in, w_o, tag, rider=None):
    h, a, proj, y, o, states, m = saved
    (dm, dg_post, dy), _ = _rmsbwd_mm(m, g_post, dh, w_o, 512, 1024, f"{tag}_do")
    dw_o = _mm(y, dm, "tn", 128, 1024, f"{tag}_dwo", out_dtypes=(BF16,), shard="rows")
    dproj, dlb, donorm, rode = _hgrn_bwd(proj, lb, onorm, o, states, dy, f"{tag}_dscan", rider)
    dw_in = _mm(a, dproj, "tn", 512, 512, f"{tag}_dwin", out_dtypes=(BF16,), shard="cols")
    dh_in, dg_pre = _mm_rmsbwd_res(dproj, w_in, h, g_pre, dh, 512, f"{tag}_din")
    return dh_in, dg_pre, dg_post, dlb, donorm, dw_in, dw_o, rode


def _mla_layer_fwd(h, g_pre, g_post, cos, sin, w_in, qn, kvn, w_uq, w_ukv, w_o, tag, rider=None):
    (a, proj), _ = _norm_mm(h, g_pre, w_in, 512, MLA_IN, f"{tag}_in")
    cqn, ckvn, q, k, v = _mla_qkv(proj, qn, kvn, w_uq, w_ukv, cos, sin, f"{tag}_qkv")
    o, lse, rode = _attn_fwd(q, k, v, f"{tag}_attn", rider)
    m, out, _ = _mm_norm_res(o, w_o, g_post, h, 512, f"{tag}_o")
    return out, (h, a, proj, cqn, ckvn, q, k, v, o, lse, m), rode


def _mla_layer_bwd(dh, saved, g_pre, g_post, cos, sin, w_in, qn, kvn, w_uq, w_ukv, w_o, tag, rider=None):
    h, a, proj, cqn, ckvn, q, k, v, o, lse, m = saved
    hh, s = q.shape[0], q.shape[1]
    (dm, dg_post, do), _ = _rmsbwd_mm(m, g_post, dh, w_o, 512, 1024, f"{tag}_do")
    dw_o = _mm(o, dm, "tn", 128, 1024, f"{tag}_dwo", out_dtypes=(BF16,), shard="rows")
    delta = _attn_delta(do, o, f"{tag}_delta")
    dq, dk, dv, rode = _attn_bwd(q, k, v, do, lse.reshape(hh, 1, s), delta.reshape(hh, 1, s), f"{tag}_dattn", rider)
    dqe, dkve, dproj, dqn, dkvn = _mla_bwd_mid(dq, dk, dv, cos, sin, proj, qn, kvn, w_uq, w_ukv, f"{tag}_dqkv")
    dw_uq = _mm(cqn, dqe, "tn", MLA_Q_LORA, 768, f"{tag}_dwuq", out_dtypes=(BF16,))
    dw_ukv = _mm(ckvn, dkve, "tn", MLA_KV_LORA, 256, f"{tag}_dwukv", out_dtypes=(BF16,), shard="cols")
    dw_in = _mm(a, dproj, "tn", 128, MLA_IN, f"{tag}_dwin", out_dtypes=(BF16,), shard="rows")
    dh_in, dg_pre = _mm_rmsbwd_res(dproj, w_in, h, g_pre, dh, 512, f"{tag}_din")
    dw_uq = dw_uq.reshape(MLA_Q_LORA, N_DEV, -1).transpose(1, 0, 2)
    return dh_in, dg_pre, dg_post, dqn, dkvn, dw_in, dw_uq, dw_ukv, dw_o, rode


_CUT = dict(mla_w_in="rows", mla_w_uq="cols", mla_w_ukv="cols", mla_w_o="rows", hgrn_w_in="cols", hgrn_w_o="rows",
            mlp_w1="cols", mlp_w2="rows")


def _unit(layer, kind):
    slot = layer // 2
    if kind == "mla":
        return [("mla_w_in", slot), ("mla_w_uq", slot), ("mla_w_ukv", slot), ("mla_w_o", slot)]
    if kind == "hgrn":
        return [("hgrn_w_in", slot), ("hgrn_w_o", slot)]
    return [("mlp_w1", layer), ("mlp_w2", layer)]


_GATHER_FIRST = _unit(0, "mla")
_GATHER_PLAN = {
    (0, "attn"): _unit(0, "mlp") + _unit(1, "hgrn"),
    (0, "up"): [("mlp_w1", 1)],
    (0, "down"): [("mlp_w2", 1)],
    (1, "in"): _unit(2, "mla"),
    (1, "scan"): _unit(2, "mlp"),
    (2, "attn"): _unit(3, "hgrn") + _unit(3, "mlp"),
}
_EXCHANGE_PLAN = {
    (3, "dscan"): _unit(3, "mlp"),
    (2, "ddown"): [("hgrn_w_in", 1)],
    (2, "dw2"): [("hgrn_w_o", 1)],
    (2, "dattn"): _unit(2, "mlp"),
    (1, "ddown"): _unit(2, "mla"),
    (1, "dscan"): _unit(1, "mlp"),
    (0, "ddown"): [("hgrn_w_in", 0)],
    (0, "dw2"): [("hgrn_w_o", 0)],
    (0, "dattn"): _unit(0, "mlp"),
}
_EXCHANGE_LAST = _unit(0, "mla")


def _gather_cut(name):
    return "slots" if name == "mla_w_uq" else _CUT[name]


def _gather_rider(weights, ents):
    return _GatherRide([weights[name][idx].astype(BF16) for name, idx in ents], [_gather_cut(name) for name, _ in ents])


def _gathered(outs, ents):
    res = {}
    for (name, idx), out in zip(ents, outs):
        if _gather_cut(name) == "slots":
            out = out.transpose(1, 0, 2).reshape(out.shape[1], -1)
        res[(name, idx)] = out
    return res


def _adamw_nd(w, g, m, v, name):
    shape = w.shape
    c = shape[-1]
    d, nm, nv = _adamw(w.reshape(-1, c), g.reshape(-1, c), m.reshape(-1, c), v.reshape(-1, c), name)
    return d.reshape(shape), nm.reshape(shape), nv.reshape(shape)


def kernel(x, positions, norm_gains, mla_w_in, mla_q_norm, mla_kv_norm, mla_w_uq, mla_w_ukv, mla_w_o, hgrn_w_in, hgrn_lb_logits, hgrn_o_norm, hgrn_w_o, mlp_w1, mlp_w2, loss_target, m_norm_gains, m_mla_w_in, m_mla_q_norm, m_mla_kv_norm, m_mla_w_uq, m_mla_w_ukv, m_mla_w_o, m_hgrn_w_in, m_hgrn_lb_logits, m_hgrn_o_norm, m_hgrn_w_o, m_mlp_w1, m_mlp_w2, v_norm_gains, v_mla_w_in, v_mla_q_norm, v_mla_kv_norm, v_mla_w_uq, v_mla_w_ukv, v_mla_w_o, v_hgrn_w_in, v_hgrn_lb_logits, v_hgrn_o_norm, v_hgrn_w_o, v_mlp_w1, v_mlp_w2):
    weights = dict(norm_gains=norm_gains, mla_w_in=mla_w_in, mla_q_norm=mla_q_norm, mla_kv_norm=mla_kv_norm,
                   mla_w_uq=mla_w_uq, mla_w_ukv=mla_w_ukv, mla_w_o=mla_w_o, hgrn_w_in=hgrn_w_in,
                   hgrn_lb_logits=hgrn_lb_logits, hgrn_o_norm=hgrn_o_norm, hgrn_w_o=hgrn_w_o, mlp_w1=mlp_w1, mlp_w2=mlp_w2)
    mom_m = dict(norm_gains=m_norm_gains, mla_w_in=m_mla_w_in, mla_q_norm=m_mla_q_norm, mla_kv_norm=m_mla_kv_norm,
                 mla_w_uq=m_mla_w_uq, mla_w_ukv=m_mla_w_ukv, mla_w_o=m_mla_w_o, hgrn_w_in=m_hgrn_w_in,
                 hgrn_lb_logits=m_hgrn_lb_logits, hgrn_o_norm=m_hgrn_o_norm, hgrn_w_o=m_hgrn_w_o, mlp_w1=m_mlp_w1, mlp_w2=m_mlp_w2)
    mom_v = dict(norm_gains=v_norm_gains, mla_w_in=v_mla_w_in, mla_q_norm=v_mla_q_norm, mla_kv_norm=v_mla_kv_norm,
                 mla_w_uq=v_mla_w_uq, mla_w_ukv=v_mla_w_ukv, mla_w_o=v_mla_w_o, hgrn_w_in=v_hgrn_w_in,
                 hgrn_lb_logits=v_hgrn_lb_logits, hgrn_o_norm=v_hgrn_o_norm, hgrn_w_o=v_hgrn_w_o, mlp_w1=v_mlp_w1, mlp_w2=v_mlp_w2)
    order = list(weights)
    seq = x.shape[1]
    h = x.reshape(seq, D_MODEL)
    target = loss_target.reshape(seq, D_MODEL)

    full = _gathered(_run_alone(_gather_rider(weights, _GATHER_FIRST), "gather_first"), _GATHER_FIRST)
    gains = _all_gather(norm_gains.reshape(DEPTH * 4, D_MODEL // N_DEV), "gather_gains")
    gains = gains.transpose(1, 0, 2).reshape(DEPTH, 4, 1, D_MODEL)

    def gather_riders(layer, keys):
        return {k: _gather_rider(weights, _GATHER_PLAN[(layer, k)]) for k in keys if (layer, k) in _GATHER_PLAN}

    def arrived(layer, rode):
        for k, outs in rode.items():
            full.update(_gathered(outs, _GATHER_PLAN[(layer, k)]))

    cos, sin = _rope_tables(positions.reshape(seq, 1), "rope_tables")
    lower = _lb_fwd(hgrn_lb_logits, "lower_bounds")

    def mixer_args(layer):
        slot = layer // 2
        if layer % 2 == 0:
            return (cos, sin, full[("mla_w_in", slot)], mla_q_norm[slot:slot + 1], mla_kv_norm[slot:slot + 1],
                    full[("mla_w_uq", slot)], full[("mla_w_ukv", slot)], full[("mla_w_o", slot)])
        return (lower[layer:layer + 1], hgrn_o_norm[slot:slot + 1], full[("hgrn_w_in", slot)], full[("hgrn_w_o", slot)])

    saved = []
    for layer in range(DEPTH):
        g = gains[layer]
        if layer % 2 == 0:
            h, sv_mix, brought = _mla_layer_fwd(h, g[0], g[1], *mixer_args(layer), f"l{layer}_mla",
                                                gather_riders(layer, ["attn"]).get("attn"))
            arrived(layer, {"attn": brought} if (layer, "attn") in _GATHER_PLAN else {})
        else:
            h, sv_mix, rode = _hgrn_layer_fwd(h, g[0], g[1], *mixer_args(layer), f"l{layer}_hgrn",
                                              gather_riders(layer, ["in", "scan"]))
            arrived(layer, rode)
        h, sv_mlp, rode = _mlp_fwd(h, g[2], g[3], full[("mlp_w1", layer)], full[("mlp_w2", layer)], f"l{layer}_mlp",
                                   gather_riders(layer, ["up", "down"]))
        arrived(layer, rode)
        saved.append((sv_mix, sv_mlp))

    loss_part, dh = _loss(h, target, "loss")
    loss = lax.psum(loss_part[0, 0], AXES)

    zero_row = jnp.zeros((1, D_MODEL), F32)
    dgains = [[None] * 4 for _ in range(DEPTH)]
    dlower = [zero_row] * DEPTH
    partials, lands = {}, {}
    dqn, dkvn, donorm = [None] * 2, [None] * 2, [None] * 2

    def exchange_riders(layer, keys):
        return {k: _ExchangeRide([partials[e] for e in _EXCHANGE_PLAN[(layer, k)]]) for k in keys
                if (layer, k) in _EXCHANGE_PLAN}

    def landed(layer, rode):
        for k, outs in rode.items():
            lands.update(zip(_EXCHANGE_PLAN[(layer, k)], outs))

    for layer in range(DEPTH - 1, -1, -1):
        slot = layer // 2
        g = gains[layer]
        sv_mix, sv_mlp = saved[layer]
        dh, dgains[layer][2], dgains[layer][3], partials[("mlp_w1", layer)], partials[("mlp_w2", layer)], rode = _mlp_bwd(
            dh, sv_mlp, g[2], g[3], full[("mlp_w1", layer)], full[("mlp_w2", layer)], f"l{layer}_mlp",
            exchange_riders(layer, ["ddown", "dw2"]))
        landed(layer, rode)
        key = "dattn" if layer % 2 == 0 else "dscan"
        rider = exchange_riders(layer, [key]).get(key)
        if layer % 2 == 0:
            (dh, dgains[layer][0], dgains[layer][1], dqn[slot], dkvn[slot], partials[("mla_w_in", slot)],
             partials[("mla_w_uq", slot)], partials[("mla_w_ukv", slot)], partials[("mla_w_o", slot)], brought) = _mla_layer_bwd(
                dh, sv_mix, g[0], g[1], *mixer_args(layer), f"l{layer}_mla", rider)
        else:
            (dh, dgains[layer][0], dgains[layer][1], dlower[layer], donorm[slot], partials[("hgrn_w_in", slot)],
             partials[("hgrn_w_o", slot)], brought) = _hgrn_layer_bwd(dh, sv_mix, g[0], g[1], *mixer_args(layer), f"l{layer}_hgrn", rider)
        landed(layer, {key: brought} if rider is not None else {})
    lands.update(zip(_EXCHANGE_LAST, _run_alone(_ExchangeRide([partials[e] for e in _EXCHANGE_LAST]), "exchange_last")))
    grad_x = dh.reshape(x.shape)
    dlogits = _lb_bwd(hgrn_lb_logits, jnp.concatenate(dlower, axis=0), "lower_bounds_bwd")

    pad = jnp.zeros((1, D_MODEL - 2 * MLA_KV_LORA), F32)
    pad2 = jnp.zeros((1, D_MODEL - 2 * HGRN_D), F32)
    small = jnp.concatenate(
        [jnp.concatenate([gg for row in dgains for gg in row], axis=0), jnp.concatenate(dqn, axis=1),
         jnp.concatenate(dkvn + [pad], axis=1), dlogits, jnp.concatenate(donorm + [pad2], axis=1), zero_row], axis=0)
    small = _sum_slots(_all_gather(small, "gather_small_grads"), "sum_small_grads")
    me = 4 * lax.axis_index("x") + 2 * lax.axis_index("y") + lax.axis_index("c")
    n_g = DEPTH * 4
    width = D_MODEL // N_DEV
    grads = {}
    grads["norm_gains"] = lax.dynamic_slice(small[:n_g], (0, me * width), (n_g, width)).reshape(DEPTH, 4, width)
    grads["mla_q_norm"] = small[n_g].reshape(2, MLA_Q_LORA)
    grads["mla_kv_norm"] = small[n_g + 1, :2 * MLA_KV_LORA].reshape(2, MLA_KV_LORA)
    grads["hgrn_lb_logits"] = small[n_g + 2:n_g + 2 + DEPTH]
    grads["hgrn_o_norm"] = small[n_g + 2 + DEPTH, :2 * HGRN_D].reshape(2, HGRN_D)

    deltas, new_m, new_v = {}, {}, {}
    for name in order:
        if name in _CUT:
            per_layer = [lands[(name, idx)] for idx in range(weights[name].shape[0])]
            grads[name], deltas[name], new_m[name], new_v[name] = _adamw_layers(
                weights[name], per_layer, mom_m[name], mom_v[name], f"adamw_{name}")
        else:
            deltas[name], new_m[name], new_v[name] = _adamw_nd(weights[name], grads[name], mom_m[name], mom_v[name], f"adamw_{name}")
    return (loss, grad_x, *[grads[n] for n in order], *[deltas[n] for n in order], *[new_m[n] for n in order],
            *[new_v[n] for n in order])
```

```python
import numpy as np
import jax
import jax.numpy as jnp
from jax import lax
from jax.experimental import pallas as pl
from jax.experimental.pallas import tpu as pltpu

F32, BF16 = jnp.float32, jnp.bfloat16

N_DEV = 8
AXES = ("x", "y", "c")
D_MODEL = 1024
DEPTH = 4
MLA_HEADS = 8
MLA_Q_LORA = 512
MLA_KV_LORA = 256
MLA_NOPE = 128
MLA_ROPE = 64
MLA_V = 128
MLA_QK = MLA_NOPE + MLA_ROPE
MLA_IN = MLA_Q_LORA + MLA_KV_LORA + MLA_ROPE
ROPE_BASE = 10000.0
HGRN_HEADS = 8
HGRN_D = 128
HGRN_CHUNK = 32
D_FF = 4 * D_MODEL
EPS = 1e-6
LOG2_E = 1.4426950408889634
ADAM_LR, ADAM_B1, ADAM_B2, ADAM_EPS, ADAM_WD, ADAM_STEP = 0.001, 0.9, 0.999, 1e-08, 0.01, 10

V7X_VMEM_LIMIT_BYTES = 56 * 1024 * 1024

NN = (((1,), (0,)), ((), ()))
NT = (((1,), (1,)), ((), ()))
TN = (((0,), (0,)), ((), ()))
_DIMS = {"nn": NN, "nt": NT, "tn": TN}


def _params(*sem):
    return pltpu.CompilerParams(dimension_semantics=sem, vmem_limit_bytes=V7X_VMEM_LIMIT_BYTES)


def _dot(a, b, dims=NN):
    return lax.dot_general(a, b, dims, preferred_element_type=F32)


def _dot_select(sel, x, pieces):
    sel = sel.astype(BF16)
    acc, rest = None, x
    for _ in range(pieces):
        term = rest.astype(BF16)
        part = _dot(sel, term)
        acc = part if acc is None else acc + part
        rest = rest - term.astype(F32)
    return acc


def _rstd(x):
    return lax.rsqrt(jnp.mean(x * x, axis=-1, keepdims=True) + EPS)


def _rms_bwd_rows(x, g, dy):
    r = _rstd(x)
    xh = x * r
    dyg = dy * g
    dx = r * (dyg - xh * jnp.mean(dyg * xh, axis=-1, keepdims=True))
    return dx, dy * xh


def _row_tile(n, want):
    t = min(n, want)
    assert n % t == 0, (n, t)
    return t


def _divisor_tile(n, cap, mult):
    for t in range(min(cap, n) - min(cap, n) % mult, 0, -mult):
        if n % t == 0:
            return t
    return n


def _rms_fwd(x, g, res, out_dtype, name):
    s, d = x.shape
    ts = _row_tile(s, 512)

    def body(x_ref, g_ref, *rest):
        xf = x_ref[...]
        y = xf * _rstd(xf) * g_ref[...]
        if res is not None:
            y = rest[0][...] + y
        rest[-1][...] = y.astype(out_dtype)

    row = pl.BlockSpec((ts, d), lambda i: (i, 0))
    vec = pl.BlockSpec((1, d), lambda i: (0, 0))
    ins = [x, g] + ([res] if res is not None else [])
    return pl.pallas_call(
        body, grid=(s // ts,), in_specs=[row, vec] + ([row] if res is not None else []), out_specs=row,
        out_shape=jax.ShapeDtypeStruct((s, d), out_dtype), compiler_params=_params("parallel"), name=name)(*ins)


def _rms_bwd(x, g, dy, res, out_dtype, name):
    s, d = x.shape
    ts = _row_tile(s, 512)

    def body(x_ref, g_ref, dy_ref, *rest):
        dx_ref, dg_ref = rest[-2:]
        dx, dg = _rms_bwd_rows(x_ref[...], g_ref[...], dy_ref[...].astype(F32))
        if res is not None:
            dx = rest[0][...] + dx
        dx_ref[...] = dx.astype(out_dtype)

        @pl.when(pl.program_id(0) == 0)
        def _():
            dg_ref[...] = jnp.zeros_like(dg_ref)

        dg_ref[...] += jnp.sum(dg, axis=0, keepdims=True)

    row = pl.BlockSpec((ts, d), lambda i: (i, 0))
    vec = pl.BlockSpec((1, d), lambda i: (0, 0))
    ins = [x, g, dy] + ([res] if res is not None else [])
    return pl.pallas_call(
        body, grid=(s // ts,), in_specs=[row, vec, row] + ([row] if res is not None else []), out_specs=(row, vec),
        out_shape=(jax.ShapeDtypeStruct((s, d), out_dtype), jax.ShapeDtypeStruct((1, d), F32)),
        compiler_params=_params("arbitrary"), name=name)(*ins)


def _mm(a, b, mode, tm, tn, name, out_dtypes=(F32,), shard=None, epi=None, extras=(), rider=None):
    if mode == "tn":
        k, m = a.shape
        a_spec = pl.BlockSpec((k, tm), lambda i, j: (0, i))
    else:
        m, k = a.shape
        a_spec = pl.BlockSpec((tm, k), lambda i, j: (i, 0))
    if mode == "nt":
        n = b.shape[0]
        b_spec = pl.BlockSpec((tn, k), lambda i, j: (j, 0))
    else:
        n = b.shape[1]
        b_spec = pl.BlockSpec((k, tn), lambda i, j: (0, j))
    assert m % tm == 0 and n % tn == 0, (name, m, tm, n, tn)
    tile = pl.BlockSpec((tm, tn), lambda i, j: (i, j))
    if shard == "rows":
        per = m // N_DEV // tm
        out_specs = [pl.BlockSpec((None, tm, tn), lambda i, j: (i // per, i % per, j))]
        out_shape = [jax.ShapeDtypeStruct((N_DEV, m // N_DEV, n), out_dtypes[0])]
    elif shard == "cols":
        per = n // N_DEV // tn
        out_specs = [pl.BlockSpec((None, tm, tn), lambda i, j: (j // per, i, j % per))]
        out_shape = [jax.ShapeDtypeStruct((N_DEV, m, n // N_DEV), out_dtypes[0])]
    else:
        out_specs = [tile for _ in out_dtypes]
        out_shape = [jax.ShapeDtypeStruct((m, n), dt) for dt in out_dtypes]
    n_ex, n_out = len(extras), len(out_shape)
    r_in, r_in_specs, r_out, r_out_specs, r_scr = _rider_specs(rider)
    n_rin, n_rout = len(r_in), len(r_out)
    grid = (m // tm, n // tn)

    def body(a_ref, b_ref, *refs):
        ex_refs = refs[:n_ex]
        o_refs = refs[n_ex + n_rin:n_ex + n_rin + n_out]
        r_refs = refs[n_ex:n_ex + n_rin] + refs[n_ex + n_rin + n_out:]
        _ride(rider, pl.program_id(0) * grid[1] + pl.program_id(1), grid[0] * grid[1], r_refs)
        acc = _dot(a_ref[...].astype(BF16), b_ref[...].astype(BF16), _DIMS[mode])
        vals = (acc,) if epi is None else epi(acc, *[r[...] for r in ex_refs])
        for o_ref, val in zip(o_refs, vals):
            o_ref[...] = val.astype(o_ref.dtype)

    sem = ("parallel", "parallel") if rider is None else ("arbitrary", "arbitrary")
    out = pl.pallas_call(
        body, grid=grid, in_specs=[a_spec, b_spec] + [tile] * n_ex + r_in_specs, out_specs=out_specs + r_out_specs,
        out_shape=out_shape + r_out, scratch_shapes=r_scr, compiler_params=_params(*sem), name=name)(a, b, *extras, *r_in)
    res = out[0] if n_out == 1 else out[:n_out]
    return res if rider is None else (res, out[n_out:])


def _norm_mm(x, g, b, tm, tn, name, out_dtypes=(F32,), epi=None, rider=None):
    m, k = x.shape
    n = b.shape[1]
    assert m % tm == 0 and n % tn == 0, (name, m, tm, n, tn)
    grid = (m // tm, n // tn)
    n_out = len(out_dtypes)
    r_in, r_in_specs, r_out, r_out_specs, r_scr = _rider_specs(rider)
    n_rin = len(r_in)

    def body(x_ref, g_ref, b_ref, *refs):
        a_ref = refs[n_rin]
        o_refs = refs[n_rin + 1:n_rin + 1 + n_out]
        _ride(rider, pl.program_id(0) * grid[1] + pl.program_id(1), grid[0] * grid[1], refs[:n_rin] + refs[n_rin + 1 + n_out:])

        @pl.when(pl.program_id(1) == 0)
        def _():
            xf = x_ref[...]
            a_ref[...] = (xf * _rstd(xf) * g_ref[...]).astype(BF16)

        acc = _dot(a_ref[...], b_ref[...].astype(BF16))
        vals = (acc,) if epi is None else epi(acc)
        for o_ref, val in zip(o_refs, vals):
            o_ref[...] = val.astype(o_ref.dtype)

    row = pl.BlockSpec((tm, k), lambda i, j: (i, 0))
    tile = pl.BlockSpec((tm, tn), lambda i, j: (i, j))
    out = pl.pallas_call(
        body, grid=grid,
        in_specs=[row, pl.BlockSpec((1, k), lambda i, j: (0, 0)), pl.BlockSpec((k, tn), lambda i, j: (0, j))] + r_in_specs,
        out_specs=[row] + [tile] * n_out + r_out_specs,
        out_shape=[jax.ShapeDtypeStruct((m, k), BF16)] + [jax.ShapeDtypeStruct((m, n), dt) for dt in out_dtypes] + r_out,
        scratch_shapes=r_scr, compiler_params=_params("arbitrary", "arbitrary"), name=name)(x, g, b, *r_in)
    return out[:1 + n_out], out[1 + n_out:]


def _mm_norm_res(a, b, g, res, tm, name, rider=None):
    m, k = a.shape
    n = b.shape[1]
    assert m % tm == 0, (name, m, tm)
    r_in, r_in_specs, r_out, r_out_specs, r_scr = _rider_specs(rider)
    n_rin = len(r_in)

    def body(a_ref, b_ref, g_ref, res_ref, *refs):
        z_ref, o_ref = refs[n_rin:n_rin + 2]
        _ride(rider, pl.program_id(0), m // tm, refs[:n_rin] + refs[n_rin + 2:])
        z = _dot(a_ref[...].astype(BF16), b_ref[...].astype(BF16))
        z_ref[...] = z
        o_ref[...] = res_ref[...] + z * _rstd(z) * g_ref[...]

    row = pl.BlockSpec((tm, n), lambda i: (i, 0))
    out = pl.pallas_call(
        body, grid=(m // tm,),
        in_specs=[pl.BlockSpec((tm, k), lambda i: (i, 0)), pl.BlockSpec((k, n), lambda i: (0, 0)),
                  pl.BlockSpec((1, n), lambda i: (0, 0)), row] + r_in_specs,
        out_specs=[row, row] + r_out_specs,
        out_shape=[jax.ShapeDtypeStruct((m, n), F32), jax.ShapeDtypeStruct((m, n), F32)] + r_out,
        scratch_shapes=r_scr, compiler_params=_params("arbitrary"), name=name)(a, b, g, res, *r_in)
    return out[0], out[1], out[2:]


def _rmsbwd_mm(x, g, dy, b, tm, tn, name, out_dtypes=(F32,), epi=None, extras=(), rider=None):
    m, k = x.shape
    n = b.shape[0]
    assert m % tm == 0 and n % tn == 0, (name, m, tm, n, tn)
    grid = (m // tm, n // tn)
    n_ex, n_out = len(extras), len(out_dtypes)
    r_in, r_in_specs, r_out, r_out_specs, r_scr = _rider_specs(rider)
    n_rin = len(r_in)

    def body(x_ref, g_ref, dy_ref, b_ref, *refs):
        ex_refs = refs[:n_ex]
        dx_ref, dg_ref = refs[n_ex + n_rin:n_ex + n_rin + 2]
        o_refs = refs[n_ex + n_rin + 2:n_ex + n_rin + 2 + n_out]
        i, j = pl.program_id(0), pl.program_id(1)
        _ride(rider, i * grid[1] + j, grid[0] * grid[1], refs[n_ex:n_ex + n_rin] + refs[n_ex + n_rin + 2 + n_out:])

        @pl.when((i == 0) & (j == 0))
        def _():
            dg_ref[...] = jnp.zeros_like(dg_ref)

        @pl.when(j == 0)
        def _():
            dx, dg = _rms_bwd_rows(x_ref[...], g_ref[...], dy_ref[...])
            dx_ref[...] = dx.astype(BF16)
            dg_ref[...] += jnp.sum(dg, axis=0, keepdims=True)

        acc = _dot(dx_ref[...], b_ref[...].astype(BF16), NT)
        vals = (acc,) if epi is None else epi(acc, *[r[...] for r in ex_refs])
        for o_ref, val in zip(o_refs, vals):
            o_ref[...] = val.astype(o_ref.dtype)

    row = pl.BlockSpec((tm, k), lambda i, j: (i, 0))
    vec = pl.BlockSpec((1, k), lambda i, j: (0, 0))
    tile = pl.BlockSpec((tm, tn), lambda i, j: (i, j))
    out = pl.pallas_call(
        body, grid=grid,
        in_specs=[row, vec, row, pl.BlockSpec((tn, k), lambda i, j: (j, 0))] + [tile] * n_ex + r_in_specs,
        out_specs=[row, vec] + [tile] * n_out + r_out_specs,
        out_shape=[jax.ShapeDtypeStruct((m, k), BF16), jax.ShapeDtypeStruct((1, k), F32)]
        + [jax.ShapeDtypeStruct((m, n), dt) for dt in out_dtypes] + r_out,
        scratch_shapes=r_scr, compiler_params=_params("arbitrary", "arbitrary"), name=name)(x, g, dy, b, *extras, *r_in)
    return out[:2 + n_out], out[2 + n_out:]


def _mm_rmsbwd_res(a, b, x, g, res, tm, name):
    m, k = a.shape
    n = b.shape[0]
    assert m % tm == 0, (name, m, tm)

    def body(a_ref, b_ref, x_ref, g_ref, res_ref, o_ref, dg_ref):
        @pl.when(pl.program_id(0) == 0)
        def _():
            dg_ref[...] = jnp.zeros_like(dg_ref)

        da = _dot(a_ref[...].astype(BF16), b_ref[...].astype(BF16), NT)
        dx, dg = _rms_bwd_rows(x_ref[...], g_ref[...], da)
        o_ref[...] = res_ref[...] + dx
        dg_ref[...] += jnp.sum(dg, axis=0, keepdims=True)

    row = pl.BlockSpec((tm, n), lambda i: (i, 0))
    vec = pl.BlockSpec((1, n), lambda i: (0, 0))
    return pl.pallas_call(
        body, grid=(m // tm,),
        in_specs=[pl.BlockSpec((tm, k), lambda i: (i, 0)), pl.BlockSpec((n, k), lambda i: (0, 0)), row, vec, row],
        out_specs=(row, vec), out_shape=(jax.ShapeDtypeStruct((m, n), F32), jax.ShapeDtypeStruct((1, n), F32)),
        compiler_params=_params("arbitrary"), name=name)(a, b, x, g, res)


def _rope_tables(pos, name):
    s = pos.shape[0]
    half = MLA_ROPE // 2
    inv_freq = jnp.asarray(np.power(np.float32(ROPE_BASE), -np.arange(0, MLA_ROPE, 2, dtype=np.float32) / MLA_ROPE)
                           .astype(np.float32).reshape(1, half))

    def body(p_ref, f_ref, c_ref, s_ref):
        ang = p_ref[...].astype(F32) * f_ref[...]
        c_ref[...] = jnp.cos(ang)
        s_ref[...] = jnp.sin(ang)

    return pl.pallas_call(
        body, out_shape=(jax.ShapeDtypeStruct((s, half), F32), jax.ShapeDtypeStruct((s, half), F32)), name=name)(pos, inv_freq)


def _lb_softmax(logits):
    m = jnp.max(logits, axis=0, keepdims=True)
    e = jnp.exp(logits - m)
    return e / jnp.sum(e, axis=0, keepdims=True)


def _lb_fwd(logits, name):
    def body(l_ref, o_ref):
        p = _lb_softmax(l_ref[...])
        acc = jnp.zeros_like(p[0:1])
        o_ref[0:1, :] = acc
        for layer in range(1, DEPTH):
            acc = acc + p[layer:layer + 1]
            o_ref[layer:layer + 1, :] = acc

    return pl.pallas_call(body, out_shape=jax.ShapeDtypeStruct(logits.shape, F32), name=name)(logits)


def _lb_bwd(logits, dlb, name):
    def body(l_ref, d_ref, o_ref):
        p = _lb_softmax(l_ref[...])
        d = d_ref[...]
        dp = [jnp.zeros_like(d[0:1])] * DEPTH
        run = jnp.zeros_like(d[0:1])
        for layer in range(DEPTH - 1, 0, -1):
            run = run + d[layer:layer + 1]
            dp[layer] = run
        inner = sum(p[layer:layer + 1] * dp[layer] for layer in range(DEPTH))
        for layer in range(DEPTH):
            o_ref[layer:layer + 1, :] = p[layer:layer + 1] * (dp[layer] - inner)

    return pl.pallas_call(body, out_shape=jax.ShapeDtypeStruct(logits.shape, F32), name=name)(logits, dlb)


def _loss(y, target, name):
    s, d = y.shape
    ts = _row_tile(s, 512)

    def body(y_ref, t_ref, l_ref, dy_ref):
        e = y_ref[...] - t_ref[...]
        dy_ref[...] = e / d

        @pl.when(pl.program_id(0) == 0)
        def _():
            l_ref[...] = jnp.zeros_like(l_ref)

        l_ref[...] += 0.5 * jnp.sum(jnp.mean(e * e, axis=-1, keepdims=True), axis=0, keepdims=True)

    row = pl.BlockSpec((ts, d), lambda i: (i, 0))
    return pl.pallas_call(
        body, grid=(s // ts,), in_specs=[row, row], out_specs=(pl.BlockSpec((1, 1), lambda i: (0, 0)), row),
        out_shape=(jax.ShapeDtypeStruct((1, 1), F32), jax.ShapeDtypeStruct((s, d), F32)),
        compiler_params=_params("arbitrary"), name=name)(y, target)


def _rope(t1, t2, cos, sin):
    return t1 * cos - t2 * sin, t1 * sin + t2 * cos


def _rope_bwd(d1, d2, cos, sin):
    return d1 * cos + d2 * sin, d2 * cos - d1 * sin


def _mla_qkv(proj, qn, kvn, w_uq, w_ukv, cos, sin, name):
    s = proj.shape[0]
    ts = _row_tile(s, 256)
    hh, half = MLA_HEADS, MLA_ROPE // 2

    def body(p_ref, qn_ref, kvn_ref, wq_ref, wkv_ref, c_ref, s_ref, cq_ref, ckv_ref, q_ref, k_ref, v_ref):
        p = p_ref[...]
        cq, ckv, kr = p[:, :MLA_Q_LORA], p[:, MLA_Q_LORA:MLA_Q_LORA + MLA_KV_LORA], p[:, MLA_Q_LORA + MLA_KV_LORA:]
        cqn = (cq * _rstd(cq) * qn_ref[...]).astype(BF16)
        ckvn = (ckv * _rstd(ckv) * kvn_ref[...]).astype(BF16)
        cq_ref[...] = cqn
        ckv_ref[...] = ckvn
        qe = _dot(cqn, wq_ref[...])
        kve = _dot(ckvn, wkv_ref[...])
        cos_, sin_ = c_ref[...], s_ref[...]
        k1, k2 = _rope(kr[:, :half], kr[:, half:], cos_, sin_)
        k1, k2 = k1.astype(BF16), k2.astype(BF16)
        for h in range(hh):
            b = h * MLA_QK
            q_ref[h, :, 0:MLA_NOPE] = qe[:, b:b + MLA_NOPE].astype(BF16)
            q1, q2 = _rope(qe[:, b + MLA_NOPE:b + MLA_NOPE + half], qe[:, b + MLA_NOPE + half:b + MLA_QK], cos_, sin_)
            q_ref[h, :, MLA_NOPE:MLA_NOPE + half] = q1.astype(BF16)
            q_ref[h, :, MLA_NOPE + half:MLA_QK] = q2.astype(BF16)
            b = h * (MLA_NOPE + MLA_V)
            k_ref[h, :, 0:MLA_NOPE] = kve[:, b:b + MLA_NOPE].astype(BF16)
            k_ref[h, :, MLA_NOPE:MLA_NOPE + half] = k1
            k_ref[h, :, MLA_NOPE + half:MLA_QK] = k2
            v_ref[h] = kve[:, b + MLA_NOPE:b + MLA_NOPE + MLA_V].astype(BF16)

    def row(w):
        return pl.BlockSpec((ts, w), lambda i: (i, 0))

    def full(shape):
        return pl.BlockSpec(shape, lambda i: (0,) * len(shape))

    def heads(w):
        return pl.BlockSpec((hh, ts, w), lambda i: (0, i, 0))

    return pl.pallas_call(
        body, grid=(s // ts,),
        in_specs=[row(MLA_IN), full(qn.shape), full(kvn.shape), full(w_uq.shape), full(w_ukv.shape), row(half), row(half)],
        out_specs=(row(MLA_Q_LORA), row(MLA_KV_LORA), heads(MLA_QK), heads(MLA_QK), heads(MLA_V)),
        out_shape=(jax.ShapeDtypeStruct((s, MLA_Q_LORA), BF16), jax.ShapeDtypeStruct((s, MLA_KV_LORA), BF16),
                   jax.ShapeDtypeStruct((hh, s, MLA_QK), BF16), jax.ShapeDtypeStruct((hh, s, MLA_QK), BF16),
                   jax.ShapeDtypeStruct((hh, s, MLA_V), BF16)),
        compiler_params=_params("parallel"), name=name)(proj, qn, kvn, w_uq, w_ukv, cos, sin)


ATTN_BLOCK = 1024
ATTN_FWD_TILE = (256, 512)
ATTN_BWD_TILE = (512, 512)


def _attn_block(s):
    return _row_tile(s, ATTN_BLOCK)


def _tile_sees(diag, q0, tq, k0, tk):
    if not diag:
        return True, False
    return k0 <= q0 + tq - 1, k0 + tk - 1 > q0


def _causal_pairs(nb, kv_major):
    if kv_major:
        pairs = [(i, j) for j in range(nb) for i in range(j, nb)]
    else:
        pairs = [(i, j) for i in range(nb) for j in range(i + 1)]
    return (jnp.asarray(np.array([p[0] for p in pairs], np.int32)), jnp.asarray(np.array([p[1] for p in pairs], np.int32)))


def _ride(rider, step, total, refs):
    if rider is None:
        return

    @pl.when(step == 0)
    def _():
        rider.start(*refs)

    @pl.when(step == (total * 7) // 8)
    def _():
        rider.middle(*refs)

    @pl.when(step == total - 1)
    def _():
        rider.finish(*refs)


def _rider_specs(rider):
    if rider is None:
        return [], [], [], [], []
    return (list(rider.operands), [_HBM] * len(rider.operands), list(rider.out_shapes), [_HBM] * len(rider.out_shapes),
            list(rider.scratch))


def _attn_fwd(q, k, v, name, rider=None):
    hh, s, _ = q.shape
    blk = _attn_block(s)
    tq, tk = min(blk, ATTN_FWD_TILE[0]), min(blk, ATTN_FWD_TILE[1])
    nb = s // blk
    it, jt = _causal_pairs(nb, kv_major=False)
    npair = int(it.shape[0])
    scale = MLA_QK ** -0.5
    c2 = scale * LOG2_E
    r_in, r_in_specs, r_out, r_out_specs, r_scr = _rider_specs(rider)
    n_rin, n_rout, n_rscr = len(r_in), len(r_out), len(r_scr)

    def body(it_ref, jt_ref, q_ref, k_ref, v_ref, *refs):
        r_refs = refs[:n_rin] + refs[n_rin + 2:n_rin + 2 + n_rout] + refs[len(refs) - n_rscr:]
        o_ref, lse_ref = refs[n_rin:n_rin + 2]
        m_scr, acc_scr, v_scr = refs[n_rin + 2 + n_rout:n_rin + 2 + n_rout + 3]
        h, t = pl.program_id(0), pl.program_id(1)
        step = h * npair + t
        _ride(rider, step, hh * npair, r_refs)
        i, j = it_ref[t], jt_ref[t]

        @pl.when(j == 0)
        def _():
            m_scr[...] = jnp.full_like(m_scr, -jnp.inf)
            acc_scr[...] = jnp.zeros_like(acc_scr)
            v_scr[:, MLA_V:] = jnp.ones((blk, MLA_V), BF16)

        def block(diag):
            v_scr[:, :MLA_V] = v_ref[...]
            for k0 in range(0, blk, tk):
                kb, vb = k_ref[k0:k0 + tk, :], v_scr[k0:k0 + tk, :]
                for q0 in range(0, blk, tq):
                    visible, needs_mask = _tile_sees(diag, q0, tq, k0, tk)
                    if not visible:
                        continue
                    rows = slice(q0, q0 + tq)
                    sc = _dot(q_ref[rows, :], kb, NT)
                    if needs_mask:
                        qpos = q0 + lax.broadcasted_iota(jnp.int32, (tq, tk), 0)
                        kpos = k0 + lax.broadcasted_iota(jnp.int32, (tq, tk), 1)
                        sc = jnp.where(qpos >= kpos, sc, -jnp.inf)
                    m_prev = m_scr[rows, :]
                    m_new = jnp.maximum(m_prev, jnp.max(sc, axis=-1, keepdims=True))
                    alpha = jnp.exp2((m_prev - m_new) * c2)
                    p = jnp.exp2((sc - m_new) * c2)
                    acc_scr[rows, :] = alpha * acc_scr[rows, :] + _dot(p.astype(BF16), vb)
                    m_scr[rows, :] = m_new

        @pl.when(j < i)
        def _():
            block(False)

        @pl.when(j == i)
        def _():
            block(True)
            acc = acc_scr[...]
            l = acc[:, MLA_V:MLA_V + 1]
            o_ref[...] = acc[:, :MLA_V] / l
            lse_ref[...] = m_scr[...] * scale + jnp.log(l)

    grid_spec = pltpu.PrefetchScalarGridSpec(
        num_scalar_prefetch=2, grid=(hh, npair),
        in_specs=[pl.BlockSpec((None, blk, MLA_QK), lambda h, t, it_, jt_: (h, it_[t], 0)),
                  pl.BlockSpec((None, blk, MLA_QK), lambda h, t, it_, jt_: (h, jt_[t], 0)),
                  pl.BlockSpec((None, blk, MLA_V), lambda h, t, it_, jt_: (h, jt_[t], 0))] + r_in_specs,
        out_specs=[pl.BlockSpec((blk, MLA_V), lambda h, t, it_, jt_: (it_[t], h)),
                   pl.BlockSpec((None, blk, 1), lambda h, t, it_, jt_: (h, it_[t], 0))] + r_out_specs,
        scratch_shapes=[pltpu.VMEM((blk, 1), F32), pltpu.VMEM((blk, 2 * MLA_V), F32),
                        pltpu.VMEM((blk, 2 * MLA_V), BF16)] + r_scr)
    out = pl.pallas_call(
        body, grid_spec=grid_spec,
        out_shape=[jax.ShapeDtypeStruct((s, hh * MLA_V), F32), jax.ShapeDtypeStruct((hh, s, 1), F32)] + r_out,
        compiler_params=_params("arbitrary", "arbitrary"), name=name)(it, jt, q, k, v, *r_in)
    return out[0], out[1], out[2:]


def _attn_delta(do, o, name):
    s = do.shape[0]
    ts = _row_tile(s, 512)
    hh = MLA_HEADS

    def body(do_ref, o_ref, d_ref):
        prod = do_ref[...] * o_ref[...]
        for h in range(hh):
            d_ref[h] = jnp.sum(prod[:, h * MLA_V:(h + 1) * MLA_V], axis=-1, keepdims=True)

    row = pl.BlockSpec((ts, hh * MLA_V), lambda i: (i, 0))
    return pl.pallas_call(
        body, grid=(s // ts,), in_specs=[row, row], out_specs=pl.BlockSpec((hh, ts, 1), lambda i: (0, i, 0)),
        out_shape=jax.ShapeDtypeStruct((hh, s, 1), F32), compiler_params=_params("parallel"), name=name)(do, o)


def _attn_bwd(q, k, v, do, lse_row, delta_row, name, rider=None):
    hh, s, _ = q.shape
    blk = _attn_block(s)
    tq, tk = min(blk, ATTN_BWD_TILE[0]), min(blk, ATTN_BWD_TILE[1])
    nb = s // blk
    it, jt = _causal_pairs(nb, kv_major=True)
    npair = int(it.shape[0])
    scale = MLA_QK ** -0.5
    c2 = scale * LOG2_E
    r_in, r_in_specs, r_out, r_out_specs, r_scr = _rider_specs(rider)
    n_rin, n_rout, n_rscr = len(r_in), len(r_out), len(r_scr)

    def body(it_ref, jt_ref, q_ref, k_ref, v_ref, do_ref, lse_ref, dl_ref, *refs):
        r_refs = refs[:n_rin] + refs[n_rin + 3:n_rin + 3 + n_rout] + refs[len(refs) - n_rscr:]
        dq_ref, dk_ref, dv_ref = refs[n_rin:n_rin + 3]
        h, t = pl.program_id(0), pl.program_id(1)
        step = h * npair + t
        _ride(rider, step, hh * npair, r_refs)
        i, j = it_ref[t], jt_ref[t]

        @pl.when(t == 0)
        def _():
            dq_ref[...] = jnp.zeros_like(dq_ref)

        def block(diag):
            if diag:
                dk_ref[...] = jnp.zeros_like(dk_ref)
                dv_ref[...] = jnp.zeros_like(dv_ref)
            for q0 in range(0, blk, tq):
                qb = q_ref[q0:q0 + tq, :]
                dob = do_ref[q0:q0 + tq, :].astype(BF16)
                lse2 = lse_ref[:, q0:q0 + tq] * LOG2_E
                dl = dl_ref[:, q0:q0 + tq]
                dq = None
                for k0 in range(0, blk, tk):
                    visible, needs_mask = _tile_sees(diag, q0, tq, k0, tk)
                    if not visible:
                        continue
                    kb, vb = k_ref[k0:k0 + tk, :], v_ref[k0:k0 + tk, :]
                    pt = jnp.exp2(_dot(kb, qb, NT) * c2 - lse2)
                    if needs_mask:
                        kpos = k0 + lax.broadcasted_iota(jnp.int32, (tk, tq), 0)
                        qpos = q0 + lax.broadcasted_iota(jnp.int32, (tk, tq), 1)
                        pt = jnp.where(qpos >= kpos, pt, 0.0)
                    dv_ref[k0:k0 + tk, :] += _dot(pt.astype(BF16), dob)
                    dpt = _dot(vb, dob, NT)
                    dst = (pt * (dpt - dl) * scale).astype(BF16)
                    dk_ref[k0:k0 + tk, :] += _dot(dst, qb)
                    part = _dot(dst, kb, TN)
                    dq = part if dq is None else dq + part
                rows = pl.ds(pl.multiple_of(i * blk + q0, tq), tq)
                dq_ref[rows, :] += dq

        @pl.when(i == j)
        def _():
            block(True)

        @pl.when(i > j)
        def _():
            block(False)

    grid_spec = pltpu.PrefetchScalarGridSpec(
        num_scalar_prefetch=2, grid=(hh, npair),
        in_specs=[pl.BlockSpec((None, blk, MLA_QK), lambda h, t, it_, jt_: (h, it_[t], 0)),
                  pl.BlockSpec((None, blk, MLA_QK), lambda h, t, it_, jt_: (h, jt_[t], 0)),
                  pl.BlockSpec((None, blk, MLA_V), lambda h, t, it_, jt_: (h, jt_[t], 0)),
                  pl.BlockSpec((blk, MLA_V), lambda h, t, it_, jt_: (it_[t], h)),
                  pl.BlockSpec((None, 1, blk), lambda h, t, it_, jt_: (h, 0, it_[t])),
                  pl.BlockSpec((None, 1, blk), lambda h, t, it_, jt_: (h, 0, it_[t]))] + r_in_specs,
        out_specs=[pl.BlockSpec((None, s, MLA_QK), lambda h, t, it_, jt_: (h, 0, 0)),
                   pl.BlockSpec((None, blk, MLA_QK), lambda h, t, it_, jt_: (h, jt_[t], 0)),
                   pl.BlockSpec((None, blk, MLA_V), lambda h, t, it_, jt_: (h, jt_[t], 0))] + r_out_specs,
        scratch_shapes=r_scr)
    out = pl.pallas_call(
        body, grid_spec=grid_spec,
        out_shape=[jax.ShapeDtypeStruct((hh, s, MLA_QK), F32), jax.ShapeDtypeStruct((hh, s, MLA_QK), F32),
                   jax.ShapeDtypeStruct((hh, s, MLA_V), F32)] + r_out,
        compiler_params=_params("arbitrary", "arbitrary"), name=name)(it, jt, q, k, v, do, lse_row, delta_row, *r_in)
    return out[0], out[1], out[2], out[3:]


def _mla_bwd_mid(dq, dk, dv, cos, sin, proj, qn, kvn, w_uq, w_ukv, name):
    s = proj.shape[0]
    ts = _row_tile(s, 256)
    hh, half = MLA_HEADS, MLA_ROPE // 2
    nq, nkv = hh * MLA_QK, hh * (MLA_NOPE + MLA_V)

    def body(dq_ref, dk_ref, dv_ref, c_ref, s_ref, p_ref, qn_ref, kvn_ref, wq_ref, wkv_ref,
             dqe_ref, dkve_ref, dp_ref, dqn_ref, dkvn_ref):
        cos_, sin_ = c_ref[...], s_ref[...]
        dkr1 = jnp.zeros((ts, half), F32)
        dkr2 = jnp.zeros((ts, half), F32)
        for h in range(hh):
            dqh, dkh = dq_ref[h], dk_ref[h]
            b = h * MLA_QK
            dqe_ref[:, b:b + MLA_NOPE] = dqh[:, :MLA_NOPE].astype(BF16)
            d1, d2 = _rope_bwd(dqh[:, MLA_NOPE:MLA_NOPE + half], dqh[:, MLA_NOPE + half:], cos_, sin_)
            dqe_ref[:, b + MLA_NOPE:b + MLA_NOPE + half] = d1.astype(BF16)
            dqe_ref[:, b + MLA_NOPE + half:b + MLA_QK] = d2.astype(BF16)
            b = h * (MLA_NOPE + MLA_V)
            dkve_ref[:, b:b + MLA_NOPE] = dkh[:, :MLA_NOPE].astype(BF16)
            dkve_ref[:, b + MLA_NOPE:b + MLA_NOPE + MLA_V] = dv_ref[h].astype(BF16)
            dkr1 = dkr1 + dkh[:, MLA_NOPE:MLA_NOPE + half]
            dkr2 = dkr2 + dkh[:, MLA_NOPE + half:]
        dkr1, dkr2 = _rope_bwd(dkr1, dkr2, cos_, sin_)
        dcqn = _dot(dqe_ref[...], wq_ref[...], NT)
        dckvn = _dot(dkve_ref[...], wkv_ref[...], NT)
        p = p_ref[...]
        dcq, dqn = _rms_bwd_rows(p[:, :MLA_Q_LORA], qn_ref[...], dcqn)
        dckv, dkvn = _rms_bwd_rows(p[:, MLA_Q_LORA:MLA_Q_LORA + MLA_KV_LORA], kvn_ref[...], dckvn)
        dp_ref[:, :MLA_Q_LORA] = dcq.astype(BF16)
        dp_ref[:, MLA_Q_LORA:MLA_Q_LORA + MLA_KV_LORA] = dckv.astype(BF16)
        dp_ref[:, MLA_Q_LORA + MLA_KV_LORA:MLA_Q_LORA + MLA_KV_LORA + half] = dkr1.astype(BF16)
        dp_ref[:, MLA_Q_LORA + MLA_KV_LORA + half:] = dkr2.astype(BF16)

        @pl.when(pl.program_id(0) == 0)
        def _():
            dqn_ref[...] = jnp.zeros_like(dqn_ref)
            dkvn_ref[...] = jnp.zeros_like(dkvn_ref)

        dqn_ref[...] += jnp.sum(dqn, axis=0, keepdims=True)
        dkvn_ref[...] += jnp.sum(dkvn, axis=0, keepdims=True)

    def row(w):
        return pl.BlockSpec((ts, w), lambda i: (i, 0))

    def full(shape):
        return pl.BlockSpec(shape, lambda i: (0,) * len(shape))

    def heads(w):
        return pl.BlockSpec((hh, ts, w), lambda i: (0, i, 0))

    return pl.pallas_call(
        body, grid=(s // ts,),
        in_specs=[heads(MLA_QK), heads(MLA_QK), heads(MLA_V), row(half), row(half), row(MLA_IN),
                  full(qn.shape), full(kvn.shape), full(w_uq.shape), full(w_ukv.shape)],
        out_specs=(row(nq), row(nkv), row(MLA_IN), full(qn.shape), full(kvn.shape)),
        out_shape=(jax.ShapeDtypeStruct((s, nq), BF16), jax.ShapeDtypeStruct((s, nkv), BF16),
                   jax.ShapeDtypeStruct((s, MLA_IN), BF16), jax.ShapeDtypeStruct(qn.shape, F32),
                   jax.ShapeDtypeStruct(kvn.shape, F32)),
        compiler_params=_params("arbitrary"), name=name)(dq, dk, dv, cos, sin, proj, qn, kvn, w_uq, w_ukv)


HGRN_TILE = 128


def _chunk_masks(t):
    r = lax.broadcasted_iota(jnp.int32, (t, t), 0)
    c = lax.broadcasted_iota(jnp.int32, (t, t), 1)
    same = (r // HGRN_CHUNK) == (c // HGRN_CHUNK)
    return r, c, same


def _hgrn_gates(p, lb):
    hk = HGRN_HEADS * HGRN_D
    qx, fx, ix, gx = p[:, :hk], p[:, hk:2 * hk], p[:, 2 * hk:3 * hk], p[:, 3 * hk:]
    sig_f = jax.nn.sigmoid(fx)
    f = lb + (1.0 - lb) * sig_f
    sig_q = jax.nn.sigmoid(qx)
    t = p.shape[0]
    r, c, same = _chunk_masks(t)
    lower = jnp.where(same & (c <= r), 1.0, 0.0).astype(F32)
    b = _dot_select(lower, jnp.log(f), 3)
    b3 = b.reshape(t // HGRN_CHUNK, HGRN_CHUNK, hk)
    bref = jnp.broadcast_to(b3[:, HGRN_CHUNK // 2:HGRN_CHUNK // 2 + 1, :], b3.shape).reshape(t, hk)
    blast = jnp.broadcast_to(b3[:, HGRN_CHUNK - 1:, :], b3.shape).reshape(t, hk)
    return qx, ix, gx, sig_f, f, sig_q, b, bref, blast


def _hgrn_fwd(proj, lb, onorm, name, rider=None):
    s = proj.shape[0]
    t = _row_tile(s, HGRN_TILE)
    nc = t // HGRN_CHUNK
    hh, dd, hk = HGRN_HEADS, HGRN_D, HGRN_HEADS * HGRN_D
    r_in, r_in_specs, r_out, r_out_specs, r_scr = _rider_specs(rider)
    n_rin, n_rout = len(r_in), len(r_out)

    def body(p_ref, lb_ref, on_ref, *refs):
        y_ref, o_ref, st_ref = refs[n_rin:n_rin + 3]
        st_scr = refs[n_rin + 3 + n_rout]
        _ride(rider, pl.program_id(0), s // t, refs[:n_rin] + refs[n_rin + 3:n_rin + 3 + n_rout] + refs[n_rin + 4 + n_rout:])

        @pl.when(pl.program_id(0) == 0)
        def _():
            st_scr[...] = jnp.zeros_like(st_scr)

        qx, ix, gx, _, f, sig_q, b, bref, blast = _hgrn_gates(p_ref[...], lb_ref[...])
        q = qx * sig_q
        k = 1.0 - f
        r, c, same = _chunk_masks(t)
        causal = same & (c <= r)
        for h in range(hh):
            sl = slice(h * dd, (h + 1) * dd)
            bh, brefh, blasth, qh, kh = b[:, sl], bref[:, sl], blast[:, sl], q[:, sl], k[:, sl]
            vh = ix[:, sl].astype(BF16)
            q_rel = (qh * jnp.exp(bh - brefh)).astype(BF16)
            k_rel = (kh * jnp.exp(brefh - bh)).astype(BF16)
            a = jnp.where(causal, _dot(q_rel, k_rel, NT), 0.0)
            o_intra = _dot(a.astype(BF16), vh)
            q_dec = (qh * jnp.exp(bh)).astype(BF16)
            k_dec = (kh * jnp.exp(blasth - bh)).astype(BF16)
            dec = jnp.exp(blasth)
            pieces = []
            for ci in range(nc):
                rows = slice(ci * HGRN_CHUNK, (ci + 1) * HGRN_CHUNK)
                st = st_scr[h]
                st_ref[ci, h] = st
                pieces.append(_dot(q_dec[rows], st.astype(BF16), NT))
                st_scr[h] = st * dec[ci * HGRN_CHUNK:ci * HGRN_CHUNK + 1, :] + _dot(vh[rows], k_dec[rows], TN)
            oh = o_intra + jnp.concatenate(pieces, axis=0)
            o_ref[:, sl] = oh
            gate = gx[:, sl] * jax.nn.sigmoid(gx[:, sl])
            y_ref[:, sl] = (oh * _rstd(oh) * on_ref[...] * gate).astype(BF16)

    out = pl.pallas_call(
        body, grid=(s // t,),
        in_specs=[pl.BlockSpec((t, 4 * hk), lambda i: (i, 0)), pl.BlockSpec((1, hk), lambda i: (0, 0)),
                  pl.BlockSpec((1, dd), lambda i: (0, 0))] + r_in_specs,
        out_specs=[pl.BlockSpec((t, hk), lambda i: (i, 0)), pl.BlockSpec((t, hk), lambda i: (i, 0)),
                   pl.BlockSpec((nc, hh, dd, dd), lambda i: (i, 0, 0, 0))] + r_out_specs,
        out_shape=[jax.ShapeDtypeStruct((s, hk), BF16), jax.ShapeDtypeStruct((s, hk), F32),
                   jax.ShapeDtypeStruct((s // HGRN_CHUNK, hh, dd, dd), F32)] + r_out,
        scratch_shapes=[pltpu.VMEM((hh, dd, dd), F32)] + r_scr,
        compiler_params=_params("arbitrary"), name=name)(proj, lb, onorm, *r_in)
    return out[0], out[1], out[2], out[3:]


def _hgrn_bwd(proj, lb, onorm, o, states, dy, name, rider=None):
    s = proj.shape[0]
    t = _row_tile(s, HGRN_TILE)
    nt = s // t
    nc = t // HGRN_CHUNK
    hh, dd, hk = HGRN_HEADS, HGRN_D, HGRN_HEADS * HGRN_D
    r_in, r_in_specs, r_out, r_out_specs, r_scr = _rider_specs(rider)
    n_rin, n_rout, n_rscr = len(r_in), len(r_out), len(r_scr)

    def body(p_ref, lb_ref, on_ref, o_ref, st_ref, dy_ref, *refs):
        r_refs = refs[:n_rin] + refs[n_rin + 3:n_rin + 3 + n_rout] + refs[len(refs) - n_rscr:]
        dp_ref, dlb_ref, don_ref = refs[n_rin:n_rin + 3]
        dst_scr, cat_scr, ext_scr, dk_scr, dq_scr = refs[n_rin + 3 + n_rout:n_rin + 3 + n_rout + 5]
        _ride(rider, pl.program_id(0), nt, r_refs)

        @pl.when(pl.program_id(0) == 0)
        def _():
            dst_scr[...] = jnp.zeros_like(dst_scr)
            dlb_ref[...] = jnp.zeros_like(dlb_ref)
            don_ref[...] = jnp.zeros_like(don_ref)

        lbv = lb_ref[...]
        qx, ix, gx, sig_f, f, sig_q, b, bref, blast = _hgrn_gates(p_ref[...], lbv)
        q = qx * sig_q
        k = 1.0 - f
        r, c, same = _chunk_masks(t)
        causal = same & (c <= r)
        on = on_ref[...]
        don = jnp.zeros((1, dd), F32)
        for h in range(hh):
            sl = slice(h * dd, (h + 1) * dd)
            oh = o_ref[:, sl]
            dyh = dy_ref[:, sl]
            gxh = gx[:, sl]
            sig_g = jax.nn.sigmoid(gxh)
            rs = _rstd(oh)
            dgate = dyh * (oh * rs * on)
            dp_ref[:, 3 * hk + h * dd:3 * hk + (h + 1) * dd] = (dgate * (sig_g * (1.0 + gxh * (1.0 - sig_g)))).astype(BF16)
            do, donh = _rms_bwd_rows(oh, on, dyh * (gxh * sig_g))
            don = don + jnp.sum(donh, axis=0, keepdims=True)
            dob = do.astype(BF16)
            bh, brefh, blasth, qh, kh = b[:, sl], bref[:, sl], blast[:, sl], q[:, sl], k[:, sl]
            vh = ix[:, sl].astype(BF16)
            e_qr, e_kr, e_qd, e_kd = jnp.exp(bh - brefh), jnp.exp(brefh - bh), jnp.exp(bh), jnp.exp(blasth - bh)
            dec = jnp.exp(blasth)
            q_rel, k_rel, q_dec, k_dec = qh * e_qr, kh * e_kr, qh * e_qd, kh * e_kd
            q_relb, k_relb, q_decb, k_decb = q_rel.astype(BF16), k_rel.astype(BF16), q_dec.astype(BF16), k_dec.astype(BF16)
            a = jnp.where(causal, _dot(q_relb, k_relb, NT), 0.0).astype(BF16)
            dv = _dot(a, dob, TN)
            da = jnp.where(causal, _dot(dob, vh, NT), 0.0).astype(BF16)
            dq_rel = _dot(da, k_relb)
            dk_rel = _dot(da, q_relb, TN)
            dq_dec, dk_dec, dv_inter, ddec = [None] * nc, [None] * nc, [None] * nc, [None] * nc
            for ci in range(nc - 1, -1, -1):
                rows = slice(ci * HGRN_CHUNK, (ci + 1) * HGRN_CHUNK)
                st = st_ref[ci, h]
                dst = dst_scr[h]
                dstb = dst.astype(BF16)
                dq_dec[ci] = _dot(dob[rows], st.astype(BF16))
                dk_dec[ci] = _dot(vh[rows], dstb)
                dv_inter[ci] = _dot(k_decb[rows], dstb, NT)
                ddec[ci] = jnp.broadcast_to(jnp.sum(dst * st, axis=0, keepdims=True), (HGRN_CHUNK, dd))
                dst_scr[h] = dst * dec[ci * HGRN_CHUNK:ci * HGRN_CHUNK + 1, :] + _dot(dob[rows], q_decb[rows], TN)
            dq_dec = jnp.concatenate(dq_dec, axis=0)
            dk_dec = jnp.concatenate(dk_dec, axis=0)
            dv = dv + jnp.concatenate(dv_inter, axis=0)
            ddec = jnp.concatenate(ddec, axis=0)
            dp_ref[:, 2 * hk + h * dd:2 * hk + (h + 1) * dd] = dv.astype(BF16)
            dq_scr[:, sl] = dq_rel * e_qr + dq_dec * e_qd
            dk_scr[:, sl] = dk_rel * e_kr + dk_dec * e_kd
            g_qr, g_kr, g_qd, g_kd = dq_rel * q_rel, dk_rel * k_rel, dq_dec * q_dec, dk_dec * k_dec
            cat_scr[0:t, sl] = g_qr - g_kr + g_qd - g_kd
            cat_scr[t:2 * t, sl] = g_kr - g_qr
            cat_scr[2 * t:3 * t, sl] = g_kd
            ext_scr[:, sl] = ddec * dec
        upper = jnp.where(same & (c >= r), 1.0, 0.0).astype(F32)
        to_ref = jnp.where(same & (r % HGRN_CHUNK <= HGRN_CHUNK // 2), 1.0, 0.0).astype(F32)
        to_all = jnp.where(same, 1.0, 0.0).astype(F32)
        dlogf = _dot_select(jnp.concatenate([upper, to_ref, to_all], axis=1), cat_scr[...], 2) + ext_scr[...]
        df = dlogf / f - dk_scr[...]
        dp_ref[:, hk:2 * hk] = (df * (1.0 - lbv) * sig_f * (1.0 - sig_f)).astype(BF16)
        dp_ref[:, 0:hk] = (dq_scr[...] * (sig_q * (1.0 + qx * (1.0 - sig_q)))).astype(BF16)
        dlb_ref[...] += jnp.sum(df * (1.0 - sig_f), axis=0, keepdims=True)
        don_ref[...] += don

    def rev(i):
        return nt - 1 - i

    out = pl.pallas_call(
        body, grid=(nt,),
        in_specs=[pl.BlockSpec((t, 4 * hk), lambda i: (rev(i), 0)), pl.BlockSpec((1, hk), lambda i: (0, 0)),
                  pl.BlockSpec((1, dd), lambda i: (0, 0)), pl.BlockSpec((t, hk), lambda i: (rev(i), 0)),
                  pl.BlockSpec((nc, hh, dd, dd), lambda i: (rev(i), 0, 0, 0)),
                  pl.BlockSpec((t, hk), lambda i: (rev(i), 0))] + r_in_specs,
        out_specs=[pl.BlockSpec((t, 4 * hk), lambda i: (rev(i), 0)), pl.BlockSpec((1, hk), lambda i: (0, 0)),
                   pl.BlockSpec((1, dd), lambda i: (0, 0))] + r_out_specs,
        out_shape=[jax.ShapeDtypeStruct((s, 4 * hk), BF16), jax.ShapeDtypeStruct((1, hk), F32),
                   jax.ShapeDtypeStruct((1, dd), F32)] + r_out,
        scratch_shapes=[pltpu.VMEM((hh, dd, dd), F32), pltpu.VMEM((3 * t, hk), F32), pltpu.VMEM((t, hk), F32),
                        pltpu.VMEM((t, hk), F32), pltpu.VMEM((t, hk), F32)] + r_scr,
        compiler_params=_params("arbitrary"), name=name)(proj, lb, onorm, o, states, dy, *r_in)
    return out[0], out[1], out[2], out[3:]


def _adamw_update(w, g, m, v):
    nm = ADAM_B1 * m + (1.0 - ADAM_B1) * g
    nv = ADAM_B2 * v + (1.0 - ADAM_B2) * (g * g)
    m_hat = nm / (1.0 - ADAM_B1 ** ADAM_STEP)
    v_hat = nv / (1.0 - ADAM_B2 ** ADAM_STEP)
    return -ADAM_LR * (m_hat / (jnp.sqrt(v_hat) + ADAM_EPS) + ADAM_WD * w), nm, nv


def _adamw(w, g, m, v, name):
    rows, cols = w.shape
    tr = _divisor_tile(rows, 256, 8)

    def body(w_ref, g_ref, m_ref, v_ref, d_ref, nm_ref, nv_ref):
        d_ref[...], nm_ref[...], nv_ref[...] = _adamw_update(w_ref[...], g_ref[...], m_ref[...], v_ref[...])

    blk = pl.BlockSpec((tr, cols), lambda i: (i, 0))
    shp = jax.ShapeDtypeStruct((rows, cols), F32)
    return pl.pallas_call(
        body, grid=(rows // tr,), in_specs=[blk] * 4, out_specs=(blk,) * 3, out_shape=(shp,) * 3,
        compiler_params=_params("parallel"), name=name)(w, g, m, v)


ADAMW_BLOCK_ELEMS = 128 * 1024


def _adamw_layers(w, lands, m, v, name, rider=None):
    ll, rows, cols = w.shape
    tr = _divisor_tile(rows, max(16, ADAMW_BLOCK_ELEMS // cols), 16)
    r_in, r_in_specs, r_out, r_out_specs, r_scr = _rider_specs(rider)
    n_rin = len(r_in)

    def body(w_ref, m_ref, v_ref, *refs):
        land_refs = refs[:ll]
        g_out, d_ref, nm_ref, nv_ref = refs[ll + n_rin:ll + n_rin + 4]
        layer = pl.program_id(0)
        _ride(rider, layer * (rows // tr) + pl.program_id(1), ll * (rows // tr), refs[ll:ll + n_rin] + refs[ll + n_rin + 4:])
        for k in range(ll):
            @pl.when(layer == k)
            def _(k=k):
                g = land_refs[k][0].astype(F32)
                for slot in range(1, N_DEV):
                    g = g + land_refs[k][slot].astype(F32)
                g_out[...] = g

        d_ref[...], nm_ref[...], nv_ref[...] = _adamw_update(w_ref[...], g_out[...], m_ref[...], v_ref[...])

    stacked = pl.BlockSpec((None, tr, cols), lambda l, i: (l, i, 0))

    def one(k):
        return pl.BlockSpec((N_DEV, tr, cols), lambda l, i: (0, jnp.where(l == k, i, 0), 0))

    shp = jax.ShapeDtypeStruct(w.shape, F32)
    out = pl.pallas_call(
        body, grid=(ll, rows // tr), in_specs=[stacked] * 3 + [one(k) for k in range(ll)] + r_in_specs,
        out_specs=[stacked] * 4 + r_out_specs, out_shape=[shp] * 4 + r_out, scratch_shapes=r_scr,
        compiler_params=_params("arbitrary", "arbitrary"), name=name)(w, m, v, *lands, *r_in)
    return out[0], out[1], out[2], out[3], out[4:]


_HBM = pl.BlockSpec(memory_space=pltpu.HBM)
_MESH = pl.DeviceIdType.MESH


class _GatherRide:
    def __init__(self, blocks, cuts):
        self.operands = list(blocks)
        self.cuts = list(cuts)
        self.out_shapes = []
        for b, cut in zip(blocks, cuts):
            r, c = b.shape
            shape = {"rows": (N_DEV * r, c), "cols": (r, N_DEV * c), "slots": (N_DEV, r, c)}[cut]
            self.out_shapes.append(jax.ShapeDtypeStruct(shape, b.dtype))
        n = len(blocks)
        self.scratch = [pltpu.SemaphoreType.DMA((7 * n,)), pltpu.SemaphoreType.DMA((7 * n,)), pltpu.SemaphoreType.DMA((n,))]

    def _parts(self, *refs):
        n = len(self.operands)
        x_refs, out_refs = refs[:n], refs[n:2 * n]
        send_sems, recv_sems, local_sems = refs[2 * n:]
        x, y, c = lax.axis_index("x"), lax.axis_index("y"), lax.axis_index("c")
        me, sibling = (x, y, c), (x, y, 1 - c)
        chips = [(1 - x, y), (x, 1 - y), (1 - x, 1 - y)]
        mine, first, passed, landed, from_sibling = [], [], [], [], []
        for e in range(n):
            x_ref, out_ref, cut = x_refs[e], out_refs[e], self.cuts[e]
            r, cc = x_ref.shape

            def place(px, py, pc, out_ref=out_ref, cut=cut, r=r, cc=cc):
                p = 4 * px + 2 * py + pc
                if cut == "rows":
                    return out_ref.at[pl.ds(pl.multiple_of(p * r, r), r), :]
                if cut == "cols":
                    return out_ref.at[:, pl.ds(pl.multiple_of(p * cc, cc), cc)]
                return out_ref.at[p]

            def copy(k, block, to, src=None, place=place, e=e):
                return pltpu.make_async_remote_copy(
                    src_ref=place(*block) if src is None else src, dst_ref=place(*block), send_sem=send_sems.at[7 * e + k],
                    recv_sem=recv_sems.at[7 * e + k], device_id=to, device_id_type=_MESH)

            mine.append(pltpu.make_async_copy(x_ref, place(*me), local_sems.at[e]))
            first += [copy(0, me, sibling, src=x_ref)] + [copy(1 + j, me, (*chip, c), src=x_ref) for j, chip in enumerate(chips)]
            passed += [copy(4 + j, (*chip, c), sibling) for j, chip in enumerate(chips)]
            landed += [copy(1 + j, (*chip, c), me) for j, chip in enumerate(chips)]
            from_sibling += [copy(0, sibling, me)] + [copy(4 + j, (*chip, 1 - c), me) for j, chip in enumerate(chips)]
        return mine, first, passed, landed, from_sibling

    def start(self, *refs):
        mine, first, _, _, _ = self._parts(*refs)
        for cp in mine + first:
            cp.start()

    def middle(self, *refs):
        _, _, passed, landed, _ = self._parts(*refs)
        for got, fwd in zip(landed, passed):
            got.wait_recv()
            fwd.start()

    def finish(self, *refs):
        mine, first, passed, _, from_sibling = self._parts(*refs)
        for cp in from_sibling:
            cp.wait_recv()
        for cp in first + passed:
            cp.wait_send()
        for cp in mine:
            cp.wait()


class _ExchangeRide:
    def __init__(self, sends):
        self.operands = list(sends)
        self.out_shapes = [jax.ShapeDtypeStruct(s.shape, s.dtype) for s in sends]
        n = len(sends)
        self.scratch = [pltpu.SemaphoreType.DMA((7 * n,)), pltpu.SemaphoreType.DMA((7 * n,)), pltpu.SemaphoreType.DMA((n,))]

    def _parts(self, *refs):
        n = len(self.operands)
        s_refs, land_refs = refs[:n], refs[n:2 * n]
        send_sems, recv_sems, local_sems = refs[2 * n:]
        x, y, c = lax.axis_index("x"), lax.axis_index("y"), lax.axis_index("c")
        me = 4 * x + 2 * y + c
        own, sends, recvs = [], [], []
        for e in range(n):
            s_ref, land_ref = s_refs[e], land_refs[e]
            own.append(pltpu.make_async_copy(s_ref.at[me], land_ref.at[me], local_sems.at[e]))
            for rel in range(1, N_DEV):
                px = 1 - x if rel & 4 else x
                py = 1 - y if rel & 2 else y
                pc = 1 - c if rel & 1 else c
                peer = 4 * px + 2 * py + pc
                k = 7 * e + rel - 1
                sends.append(pltpu.make_async_remote_copy(
                    src_ref=s_ref.at[peer], dst_ref=land_ref.at[me], send_sem=send_sems.at[k], recv_sem=recv_sems.at[k],
                    device_id=(px, py, pc), device_id_type=_MESH))
                recvs.append(pltpu.make_async_remote_copy(
                    src_ref=s_ref.at[me], dst_ref=land_ref.at[peer], send_sem=send_sems.at[k], recv_sem=recv_sems.at[k],
                    device_id=(px, py, pc), device_id_type=_MESH))
        return own, sends, recvs

    def start(self, *refs):
        own, sends, _ = self._parts(*refs)
        for cp in own + sends:
            cp.start()

    def middle(self, *refs):
        pass

    def finish(self, *refs):
        own, sends, recvs = self._parts(*refs)
        for cp in recvs:
            cp.wait_recv()
        for cp in sends:
            cp.wait_send()
        for cp in own:
            cp.wait()


def _run_alone(rider, name):
    def body(*refs):
        rider.start(*refs)
        rider.middle(*refs)
        rider.finish(*refs)

    return pl.pallas_call(
        body, out_shape=rider.out_shapes, in_specs=[_HBM] * len(rider.operands), out_specs=[_HBM] * len(rider.out_shapes),
        scratch_shapes=rider.scratch, name=name)(*rider.operands)


def _all_gather(xs, name):
    return _run_alone(_GatherRide([xs], ["slots"]), name)[0]


def _sum_slots(parts, name):
    _, rows, cols = parts.shape
    tr = _divisor_tile(rows, 256, 16)

    def body(p_ref, o_ref):
        acc = p_ref[0].astype(F32)
        for slot in range(1, N_DEV):
            acc = acc + p_ref[slot].astype(F32)
        o_ref[...] = acc

    return pl.pallas_call(
        body, grid=(rows // tr,), in_specs=[pl.BlockSpec((N_DEV, tr, cols), lambda i: (0, i, 0))],
        out_specs=pl.BlockSpec((tr, cols), lambda i: (i, 0)), out_shape=jax.ShapeDtypeStruct((rows, cols), F32),
        compiler_params=_params("parallel"), name=name)(parts)


def _carry(rode, key, riders, call):
    rider = riders.get(key)
    res = call(rider)
    if rider is None:
        return res
    res, rode[key] = res
    return res


def _kept(rode, key, riders, brought):
    if key in riders:
        rode[key] = brought


def _mlp_fwd(h, g_pre, g_post, w1, w2, tag, riders):
    rode = {}
    tm = _row_tile(h.shape[0], 1024)
    (a, r2), brought = _norm_mm(h, g_pre, w1, tm, 1024, f"{tag}_up", out_dtypes=(BF16,),
                                epi=lambda acc: (jnp.square(jnp.maximum(acc, 0.0)),), rider=riders.get("up"))
    _kept(rode, "up", riders, brought)
    z, out, brought = _mm_norm_res(r2, w2, g_post, h, 512, f"{tag}_down", riders.get("down"))
    _kept(rode, "down", riders, brought)
    return out, (h, a, r2, z), rode


def _mlp_bwd(dh, saved, g_pre, g_post, w1, w2, tag, riders):
    h, a, r2, z = saved
    rode = {}
    tm = _row_tile(h.shape[0], 1024)
    (dz, dg_post, du), brought = _rmsbwd_mm(
        z, g_post, dh, w2, tm, 1024, f"{tag}_ddown", out_dtypes=(BF16,), extras=(r2,),
        epi=lambda acc, rr: (acc * (2.0 * jnp.sqrt(rr.astype(F32))),), rider=riders.get("ddown"))
    _kept(rode, "ddown", riders, brought)
    dw2 = _carry(rode, "dw2", riders, lambda r: _mm(
        r2, dz, "tn", 512, 512, f"{tag}_dw2", out_dtypes=(BF16,), shard="rows", rider=r))
    dw1 = _mm(a, du, "tn", 512, 512, f"{tag}_dw1", out_dtypes=(BF16,), shard="cols")
    dh_in, dg_pre = _mm_rmsbwd_res(du, w1, h, g_pre, dh, 512, f"{tag}_dup")
    return dh_in, dg_pre, dg_post, dw1, dw2, rode


def _hgrn_layer_fwd(h, g_pre, g_post, lb, onorm, w_in, w_o, tag, riders):
    rode = {}
    (a, proj), brought = _norm_mm(h, g_pre, w_in, 512, 1024, f"{tag}_in", rider=riders.get("in"))
    _kept(rode, "in", riders, brought)
    y, o, states, brought = _hgrn_fwd(proj, lb, onorm, f"{tag}_scan", riders.get("scan"))
    _kept(rode, "scan", riders, brought)
    m, out, _ = _mm_norm_res(y, w_o, g_post, h, 512, f"{tag}_o")
    return out, (h, a, proj, y, o, states, m), rode


def _hgrn_layer_bwd(dh, saved, g_pre, g_post, lb, onorm, w_in, w_o, tag, rider=None):
    h, a, proj, y, o, states, m = saved
    (dm, dg_post, dy), _ = _rmsbwd_mm(m, g_post, dh, w_o, 512, 1024, f"{tag}_do")
    dw_o = _mm(y, dm, "tn", 128, 1024, f"{tag}_dwo", out_dtypes=(BF16,), shard="rows")
    dproj, dlb, donorm, rode = _hgrn_bwd(proj, lb, onorm, o, states, dy, f"{tag}_dscan", rider)
    dw_in = _mm(a, dproj, "tn", 512, 512, f"{tag}_dwin", out_dtypes=(BF16,), shard="cols")
    dh_in, dg_pre = _mm_rmsbwd_res(dproj, w_in, h, g_pre, dh, 512, f"{tag}_din")
    return dh_in, dg_pre, dg_post, dlb, donorm, dw_in, dw_o, rode


def _mla_layer_fwd(h, g_pre, g_post, cos, sin, w_in, late, qn, kvn, tag, riders):
    rode = {}
    (a, proj), brought = _norm_mm(h, g_pre, w_in, 512, MLA_IN, f"{tag}_in", rider=riders.get("in"))
    _kept(rode, "in", riders, brought)
    w_uq, w_ukv, _ = late(rode)
    cqn, ckvn, q, k, v = _mla_qkv(proj, qn, kvn, w_uq, w_ukv, cos, sin, f"{tag}_qkv")
    o, lse, brought = _attn_fwd(q, k, v, f"{tag}_attn", riders.get("attn"))
    _kept(rode, "attn", riders, brought)
    _, _, w_o = late(rode)
    m, out, _ = _mm_norm_res(o, w_o, g_post, h, 512, f"{tag}_o")
    return out, (h, a, proj, cqn, ckvn, q, k, v, o, lse, m), rode


def _mla_layer_bwd(dh, saved, g_pre, g_post, cos, sin, w_in, qn, kvn, w_uq, w_ukv, w_o, tag, rider=None, own_w_o_rides=False):
    h, a, proj, cqn, ckvn, q, k, v, o, lse, m = saved
    hh, s = q.shape[0], q.shape[1]
    (dm, dg_post, do), _ = _rmsbwd_mm(m, g_post, dh, w_o, 512, 1024, f"{tag}_do")
    dw_o = _mm(o, dm, "tn", 128, 1024, f"{tag}_dwo", out_dtypes=(BF16,), shard="rows")
    if own_w_o_rides:
        rider = _ExchangeRide(list(rider.operands) + [dw_o])
    delta = _attn_delta(do, o, f"{tag}_delta")
    dq, dk, dv, rode = _attn_bwd(q, k, v, do, lse.reshape(hh, 1, s), delta.reshape(hh, 1, s), f"{tag}_dattn", rider)
    dqe, dkve, dproj, dqn, dkvn = _mla_bwd_mid(dq, dk, dv, cos, sin, proj, qn, kvn, w_uq, w_ukv, f"{tag}_dqkv")
    dw_uq = _mm(cqn, dqe, "tn", MLA_Q_LORA, 768, f"{tag}_dwuq", out_dtypes=(BF16,))
    dw_ukv = _mm(ckvn, dkve, "tn", MLA_KV_LORA, 256, f"{tag}_dwukv", out_dtypes=(BF16,), shard="cols")
    dw_in = _mm(a, dproj, "tn", 128, MLA_IN, f"{tag}_dwin", out_dtypes=(BF16,), shard="rows")
    dh_in, dg_pre = _mm_rmsbwd_res(dproj, w_in, h, g_pre, dh, 512, f"{tag}_din")
    dw_uq = dw_uq.reshape(MLA_Q_LORA, N_DEV, -1).transpose(1, 0, 2)
    return dh_in, dg_pre, dg_post, dqn, dkvn, dw_in, dw_uq, dw_ukv, dw_o, rode


_CUT = dict(mla_w_in="rows", mla_w_uq="cols", mla_w_ukv="cols", mla_w_o="rows", hgrn_w_in="cols", hgrn_w_o="rows",
            mlp_w1="cols", mlp_w2="rows")


def _unit(layer, kind):
    slot = layer // 2
    if kind == "mla":
        return [("mla_w_in", slot), ("mla_w_uq", slot), ("mla_w_ukv", slot), ("mla_w_o", slot)]
    if kind == "hgrn":
        return [("hgrn_w_in", slot), ("hgrn_w_o", slot)]
    return [("mlp_w1", layer), ("mlp_w2", layer)]


_GATHER_FIRST = [("mla_w_in", 0)]
_GATHER_PLAN = {
    (0, "in"): [("mla_w_uq", 0), ("mla_w_ukv", 0)],
    (0, "attn"): [("mla_w_o", 0)] + _unit(0, "mlp") + _unit(1, "hgrn"),
    (0, "up"): [("mlp_w1", 1)],
    (0, "down"): [("mlp_w2", 1)],
    (1, "in"): _unit(2, "mla"),
    (1, "scan"): _unit(2, "mlp"),
    (2, "attn"): _unit(3, "hgrn") + _unit(3, "mlp"),
}
_EXCHANGE_PLAN = {
    (3, "dscan"): _unit(3, "mlp"),
    (2, "ddown"): [("hgrn_w_in", 1)],
    (2, "dw2"): [("hgrn_w_o", 1)],
    (2, "dattn"): _unit(2, "mlp"),
    (1, "ddown"): _unit(2, "mla"),
    (1, "dscan"): _unit(1, "mlp"),
    (0, "ddown"): [("hgrn_w_in", 0)],
    (0, "dw2"): [("hgrn_w_o", 0)],
    (0, "dattn"): _unit(0, "mlp"),
}
_EXCHANGE_LAST = [("mla_w_in", 0), ("mla_w_uq", 0), ("mla_w_ukv", 0)]
_EXCHANGE_LAST_CARRIER = "mlp_w1"


def _gather_cut(name):
    return "slots" if name == "mla_w_uq" else _CUT[name]


def _gather_rider(weights, ents):
    return _GatherRide([weights[name][idx].astype(BF16) for name, idx in ents], [_gather_cut(name) for name, _ in ents])


def _gathered(outs, ents):
    res = {}
    for (name, idx), out in zip(ents, outs):
        if _gather_cut(name) == "slots":
            out = out.transpose(1, 0, 2).reshape(out.shape[1], -1)
        res[(name, idx)] = out
    return res


def _adamw_nd(w, g, m, v, name):
    shape = w.shape
    c = shape[-1]
    d, nm, nv = _adamw(w.reshape(-1, c), g.reshape(-1, c), m.reshape(-1, c), v.reshape(-1, c), name)
    return d.reshape(shape), nm.reshape(shape), nv.reshape(shape)


def kernel(x, positions, norm_gains, mla_w_in, mla_q_norm, mla_kv_norm, mla_w_uq, mla_w_ukv, mla_w_o, hgrn_w_in, hgrn_lb_logits, hgrn_o_norm, hgrn_w_o, mlp_w1, mlp_w2, loss_target, m_norm_gains, m_mla_w_in, m_mla_q_norm, m_mla_kv_norm, m_mla_w_uq, m_mla_w_ukv, m_mla_w_o, m_hgrn_w_in, m_hgrn_lb_logits, m_hgrn_o_norm, m_hgrn_w_o, m_mlp_w1, m_mlp_w2, v_norm_gains, v_mla_w_in, v_mla_q_norm, v_mla_kv_norm, v_mla_w_uq, v_mla_w_ukv, v_mla_w_o, v_hgrn_w_in, v_hgrn_lb_logits, v_hgrn_o_norm, v_hgrn_w_o, v_mlp_w1, v_mlp_w2):
    weights = dict(norm_gains=norm_gains, mla_w_in=mla_w_in, mla_q_norm=mla_q_norm, mla_kv_norm=mla_kv_norm,
                   mla_w_uq=mla_w_uq, mla_w_ukv=mla_w_ukv, mla_w_o=mla_w_o, hgrn_w_in=hgrn_w_in,
                   hgrn_lb_logits=hgrn_lb_logits, hgrn_o_norm=hgrn_o_norm, hgrn_w_o=hgrn_w_o, mlp_w1=mlp_w1, mlp_w2=mlp_w2)
    mom_m = dict(norm_gains=m_norm_gains, mla_w_in=m_mla_w_in, mla_q_norm=m_mla_q_norm, mla_kv_norm=m_mla_kv_norm,
                 mla_w_uq=m_mla_w_uq, mla_w_ukv=m_mla_w_ukv, mla_w_o=m_mla_w_o, hgrn_w_in=m_hgrn_w_in,
                 hgrn_lb_logits=m_hgrn_lb_logits, hgrn_o_norm=m_hgrn_o_norm, hgrn_w_o=m_hgrn_w_o, mlp_w1=m_mlp_w1, mlp_w2=m_mlp_w2)
    mom_v = dict(norm_gains=v_norm_gains, mla_w_in=v_mla_w_in, mla_q_norm=v_mla_q_norm, mla_kv_norm=v_mla_kv_norm,
                 mla_w_uq=v_mla_w_uq, mla_w_ukv=v_mla_w_ukv, mla_w_o=v_mla_w_o, hgrn_w_in=v_hgrn_w_in,
                 hgrn_lb_logits=v_hgrn_lb_logits, hgrn_o_norm=v_hgrn_o_norm, hgrn_w_o=v_hgrn_w_o, mlp_w1=v_mlp_w1, mlp_w2=v_mlp_w2)
    order = list(weights)
    seq = x.shape[1]
    h = x.reshape(seq, D_MODEL)
    target = loss_target.reshape(seq, D_MODEL)

    full = _gathered(_run_alone(_gather_rider(weights, _GATHER_FIRST), "gather_first"), _GATHER_FIRST)
    gains = _all_gather(norm_gains.reshape(DEPTH * 4, D_MODEL // N_DEV), "gather_gains")
    gains = gains.transpose(1, 0, 2).reshape(DEPTH, 4, 1, D_MODEL)

    def gather_riders(layer, keys):
        return {k: _gather_rider(weights, _GATHER_PLAN[(layer, k)]) for k in keys if (layer, k) in _GATHER_PLAN}

    seen = set()

    def arrived(layer, rode):
        for k, outs in rode.items():
            if (layer, k) not in seen:
                seen.add((layer, k))
                full.update(_gathered(outs, _GATHER_PLAN[(layer, k)]))

    def late_weights(layer):
        def late(rode):
            arrived(layer, rode)
            return tuple(full.get((name, layer // 2)) for name in ("mla_w_uq", "mla_w_ukv", "mla_w_o"))
        return late

    cos, sin = _rope_tables(positions.reshape(seq, 1), "rope_tables")
    lower = _lb_fwd(hgrn_lb_logits, "lower_bounds")

    def mixer_args(layer):
        slot = layer // 2
        if layer % 2 == 0:
            return (cos, sin, full[("mla_w_in", slot)], mla_q_norm[slot:slot + 1], mla_kv_norm[slot:slot + 1],
                    full[("mla_w_uq", slot)], full[("mla_w_ukv", slot)], full[("mla_w_o", slot)])
        return (lower[layer:layer + 1], hgrn_o_norm[slot:slot + 1], full[("hgrn_w_in", slot)], full[("hgrn_w_o", slot)])

    saved = []
    for layer in range(DEPTH):
        g = gains[layer]
        if layer % 2 == 0:
            slot = layer // 2
            h, sv_mix, rode = _mla_layer_fwd(
                h, g[0], g[1], cos, sin, full[("mla_w_in", slot)], late_weights(layer), mla_q_norm[slot:slot + 1],
                mla_kv_norm[slot:slot + 1], f"l{layer}_mla", gather_riders(layer, ["in", "attn"]))
            arrived(layer, rode)
        else:
            h, sv_mix, rode = _hgrn_layer_fwd(h, g[0], g[1], *mixer_args(layer), f"l{layer}_hgrn",
                                              gather_riders(layer, ["in", "scan"]))
            arrived(layer, rode)
        h, sv_mlp, rode = _mlp_fwd(h, g[2], g[3], full[("mlp_w1", layer)], full[("mlp_w2", layer)], f"l{layer}_mlp",
                                   gather_riders(layer, ["up", "down"]))
        arrived(layer, rode)
        saved.append((sv_mix, sv_mlp))

    loss_part, dh = _loss(h, target, "loss")
    loss = lax.psum(loss_part[0, 0], AXES)

    zero_row = jnp.zeros((1, D_MODEL), F32)
    dgains = [[None] * 4 for _ in range(DEPTH)]
    dlower = [zero_row] * DEPTH
    partials, lands = {}, {}
    dqn, dkvn, donorm = [None] * 2, [None] * 2, [None] * 2

    def exchange_riders(layer, keys):
        return {k: _ExchangeRide([partials[e] for e in _EXCHANGE_PLAN[(layer, k)]]) for k in keys
                if (layer, k) in _EXCHANGE_PLAN}

    def landed(layer, rode):
        for k, outs in rode.items():
            lands.update(zip(_EXCHANGE_PLAN[(layer, k)], outs))

    for layer in range(DEPTH - 1, -1, -1):
        slot = layer // 2
        g = gains[layer]
        sv_mix, sv_mlp = saved[layer]
        dh, dgains[layer][2], dgains[layer][3], partials[("mlp_w1", layer)], partials[("mlp_w2", layer)], rode = _mlp_bwd(
            dh, sv_mlp, g[2], g[3], full[("mlp_w1", layer)], full[("mlp_w2", layer)], f"l{layer}_mlp",
            exchange_riders(layer, ["ddown", "dw2"]))
        landed(layer, rode)
        key = "dattn" if layer % 2 == 0 else "dscan"
        rider = exchange_riders(layer, [key]).get(key)
        if layer % 2 == 0:
            (dh, dgains[layer][0], dgains[layer][1], dqn[slot], dkvn[slot], partials[("mla_w_in", slot)],
             partials[("mla_w_uq", slot)], partials[("mla_w_ukv", slot)], partials[("mla_w_o", slot)], brought) = _mla_layer_bwd(
                dh, sv_mix, g[0], g[1], *mixer_args(layer), f"l{layer}_mla", rider, own_w_o_rides=(layer == 0))
            if layer == 0:
                lands[("mla_w_o", 0)] = brought[-1]
        else:
            (dh, dgains[layer][0], dgains[layer][1], dlower[layer], donorm[slot], partials[("hgrn_w_in", slot)],
             partials[("hgrn_w_o", slot)], brought) = _hgrn_layer_bwd(dh, sv_mix, g[0], g[1], *mixer_args(layer), f"l{layer}_hgrn", rider)
        landed(layer, {key: brought} if rider is not None else {})
    grad_x = dh.reshape(x.shape)
    dlogits = _lb_bwd(hgrn_lb_logits, jnp.concatenate(dlower, axis=0), "lower_bounds_bwd")

    pad = jnp.zeros((1, D_MODEL - 2 * MLA_KV_LORA), F32)
    pad2 = jnp.zeros((1, D_MODEL - 2 * HGRN_D), F32)
    small = jnp.concatenate(
        [jnp.concatenate([gg for row in dgains for gg in row], axis=0), jnp.concatenate(dqn, axis=1),
         jnp.concatenate(dkvn + [pad], axis=1), dlogits, jnp.concatenate(donorm + [pad2], axis=1), zero_row], axis=0)
    small = _sum_slots(_all_gather(small, "gather_small_grads"), "sum_small_grads")
    me = 4 * lax.axis_index("x") + 2 * lax.axis_index("y") + lax.axis_index("c")
    n_g = DEPTH * 4
    width = D_MODEL // N_DEV
    grads = {}
    grads["norm_gains"] = lax.dynamic_slice(small[:n_g], (0, me * width), (n_g, width)).reshape(DEPTH, 4, width)
    grads["mla_q_norm"] = small[n_g].reshape(2, MLA_Q_LORA)
    grads["mla_kv_norm"] = small[n_g + 1, :2 * MLA_KV_LORA].reshape(2, MLA_KV_LORA)
    grads["hgrn_lb_logits"] = small[n_g + 2:n_g + 2 + DEPTH]
    grads["hgrn_o_norm"] = small[n_g + 2 + DEPTH, :2 * HGRN_D].reshape(2, HGRN_D)

    deltas, new_m, new_v = {}, {}, {}
    for name in [_EXCHANGE_LAST_CARRIER] + [n for n in order if n != _EXCHANGE_LAST_CARRIER]:
        if name in _CUT:
            per_layer = [lands[(name, idx)] for idx in range(weights[name].shape[0])]
            last = _ExchangeRide([partials[e] for e in _EXCHANGE_LAST]) if name == _EXCHANGE_LAST_CARRIER else None
            grads[name], deltas[name], new_m[name], new_v[name], brought = _adamw_layers(
                weights[name], per_layer, mom_m[name], mom_v[name], f"adamw_{name}", last)
            if last is not None:
                lands.update(zip(_EXCHANGE_LAST, brought))
        else:
            deltas[name], new_m[name], new_v[name] = _adamw_nd(weights[name], grads[name], mom_m[name], mom_v[name], f"adamw_{name}")
    return (loss, grad_x, *[grads[n] for n in order], *[deltas[n] for n in order], *[new_m[n] for n in order],
            *[new_v[n] for n in order])
```

```python
import numpy as np
import jax
import jax.numpy as jnp
from jax import lax
from jax.experimental import pallas as pl
from jax.experimental.pallas import tpu as pltpu

F32, BF16 = jnp.float32, jnp.bfloat16

N_DEV = 8
AXES = ("x", "y", "c")
D_MODEL = 1024
DEPTH = 4
MLA_HEADS = 8
MLA_Q_LORA = 512
MLA_KV_LORA = 256
MLA_NOPE = 128
MLA_ROPE = 64
MLA_V = 128
MLA_QK = MLA_NOPE + MLA_ROPE
MLA_IN = MLA_Q_LORA + MLA_KV_LORA + MLA_ROPE
ROPE_BASE = 10000.0
HGRN_HEADS = 8
HGRN_D = 128
HGRN_CHUNK = 32
D_FF = 4 * D_MODEL
EPS = 1e-6
LOG2_E = 1.4426950408889634
ADAM_LR, ADAM_B1, ADAM_B2, ADAM_EPS, ADAM_WD, ADAM_STEP = 0.001, 0.9, 0.999, 1e-08, 0.01, 10

V7X_VMEM_LIMIT_BYTES = 56 * 1024 * 1024

NN = (((1,), (0,)), ((), ()))
NT = (((1,), (1,)), ((), ()))
TN = (((0,), (0,)), ((), ()))
_DIMS = {"nn": NN, "nt": NT, "tn": TN}


def _params(*sem):
    return pltpu.CompilerParams(dimension_semantics=sem, vmem_limit_bytes=V7X_VMEM_LIMIT_BYTES)


def _dot(a, b, dims=NN):
    return lax.dot_general(a, b, dims, preferred_element_type=F32)


def _dot_select(sel, x, pieces):
    sel = sel.astype(BF16)
    acc, rest = None, x
    for _ in range(pieces):
        term = rest.astype(BF16)
        part = _dot(sel, term)
        acc = part if acc is None else acc + part
        rest = rest - term.astype(F32)
    return acc


def _rstd(x):
    return lax.rsqrt(jnp.mean(x * x, axis=-1, keepdims=True) + EPS)


def _rms_bwd_rows(x, g, dy):
    r = _rstd(x)
    xh = x * r
    dyg = dy * g
    dx = r * (dyg - xh * jnp.mean(dyg * xh, axis=-1, keepdims=True))
    return dx, dy * xh


def _row_tile(n, want):
    t = min(n, want)
    assert n % t == 0, (n, t)
    return t


def _divisor_tile(n, cap, mult):
    for t in range(min(cap, n) - min(cap, n) % mult, 0, -mult):
        if n % t == 0:
            return t
    return n


def _rms_fwd(x, g, res, out_dtype, name):
    s, d = x.shape
    ts = _row_tile(s, 512)

    def body(x_ref, g_ref, *rest):
        xf = x_ref[...]
        y = xf * _rstd(xf) * g_ref[...]
        if res is not None:
            y = rest[0][...] + y
        rest[-1][...] = y.astype(out_dtype)

    row = pl.BlockSpec((ts, d), lambda i: (i, 0))
    vec = pl.BlockSpec((1, d), lambda i: (0, 0))
    ins = [x, g] + ([res] if res is not None else [])
    return pl.pallas_call(
        body, grid=(s // ts,), in_specs=[row, vec] + ([row] if res is not None else []), out_specs=row,
        out_shape=jax.ShapeDtypeStruct((s, d), out_dtype), compiler_params=_params("parallel"), name=name)(*ins)


def _rms_bwd(x, g, dy, res, out_dtype, name):
    s, d = x.shape
    ts = _row_tile(s, 512)

    def body(x_ref, g_ref, dy_ref, *rest):
        dx_ref, dg_ref = rest[-2:]
        dx, dg = _rms_bwd_rows(x_ref[...], g_ref[...], dy_ref[...].astype(F32))
        if res is not None:
            dx = rest[0][...] + dx
        dx_ref[...] = dx.astype(out_dtype)

        @pl.when(pl.program_id(0) == 0)
        def _():
            dg_ref[...] = jnp.zeros_like(dg_ref)

        dg_ref[...] += jnp.sum(dg, axis=0, keepdims=True)

    row = pl.BlockSpec((ts, d), lambda i: (i, 0))
    vec = pl.BlockSpec((1, d), lambda i: (0, 0))
    ins = [x, g, dy] + ([res] if res is not None else [])
    return pl.pallas_call(
        body, grid=(s // ts,), in_specs=[row, vec, row] + ([row] if res is not None else []), out_specs=(row, vec),
        out_shape=(jax.ShapeDtypeStruct((s, d), out_dtype), jax.ShapeDtypeStruct((1, d), F32)),
        compiler_params=_params("arbitrary"), name=name)(*ins)


def _mm(a, b, mode, tm, tn, name, out_dtypes=(F32,), shard=None, epi=None, extras=(), rider=None):
    if mode == "tn":
        k, m = a.shape
        a_spec = pl.BlockSpec((k, tm), lambda i, j: (0, i))
    else:
        m, k = a.shape
        a_spec = pl.BlockSpec((tm, k), lambda i, j: (i, 0))
    if mode == "nt":
        n = b.shape[0]
        b_spec = pl.BlockSpec((tn, k), lambda i, j: (j, 0))
    else:
        n = b.shape[1]
        b_spec = pl.BlockSpec((k, tn), lambda i, j: (0, j))
    assert m % tm == 0 and n % tn == 0, (name, m, tm, n, tn)
    tile = pl.BlockSpec((tm, tn), lambda i, j: (i, j))
    if shard == "rows":
        per = m // N_DEV // tm
        out_specs = [pl.BlockSpec((None, tm, tn), lambda i, j: (i // per, i % per, j))]
        out_shape = [jax.ShapeDtypeStruct((N_DEV, m // N_DEV, n), out_dtypes[0])]
    elif shard == "cols":
        per = n // N_DEV // tn
        out_specs = [pl.BlockSpec((None, tm, tn), lambda i, j: (j // per, i, j % per))]
        out_shape = [jax.ShapeDtypeStruct((N_DEV, m, n // N_DEV), out_dtypes[0])]
    else:
        out_specs = [tile for _ in out_dtypes]
        out_shape = [jax.ShapeDtypeStruct((m, n), dt) for dt in out_dtypes]
    n_ex, n_out = len(extras), len(out_shape)
    r_in, r_in_specs, r_out, r_out_specs, r_scr = _rider_specs(rider)
    n_rin, n_rout = len(r_in), len(r_out)
    grid = (m // tm, n // tn)

    def body(a_ref, b_ref, *refs):
        ex_refs = refs[:n_ex]
        o_refs = refs[n_ex + n_rin:n_ex + n_rin + n_out]
        r_refs = refs[n_ex:n_ex + n_rin] + refs[n_ex + n_rin + n_out:]
        _ride(rider, pl.program_id(0) * grid[1] + pl.program_id(1), grid[0] * grid[1], r_refs)
        acc = _dot(a_ref[...].astype(BF16), b_ref[...].astype(BF16), _DIMS[mode])
        vals = (acc,) if epi is None else epi(acc, *[r[...] for r in ex_refs])
        for o_ref, val in zip(o_refs, vals):
            o_ref[...] = val.astype(o_ref.dtype)

    sem = ("parallel", "parallel") if rider is None else ("arbitrary", "arbitrary")
    out = pl.pallas_call(
        body, grid=grid, in_specs=[a_spec, b_spec] + [tile] * n_ex + r_in_specs, out_specs=out_specs + r_out_specs,
        out_shape=out_shape + r_out, scratch_shapes=r_scr, compiler_params=_params(*sem), name=name)(a, b, *extras, *r_in)
    res = out[0] if n_out == 1 else out[:n_out]
    return res if rider is None else (res, out[n_out:])


def _norm_mm(x, g, b, tm, tn, name, out_dtypes=(F32,), epi=None, rider=None):
    m, k = x.shape
    n = b.shape[1]
    assert m % tm == 0 and n % tn == 0, (name, m, tm, n, tn)
    grid = (m // tm, n // tn)
    n_out = len(out_dtypes)
    r_in, r_in_specs, r_out, r_out_specs, r_scr = _rider_specs(rider)
    n_rin = len(r_in)

    def body(x_ref, g_ref, b_ref, *refs):
        a_ref = refs[n_rin]
        o_refs = refs[n_rin + 1:n_rin + 1 + n_out]
        _ride(rider, pl.program_id(0) * grid[1] + pl.program_id(1), grid[0] * grid[1], refs[:n_rin] + refs[n_rin + 1 + n_out:])

        @pl.when(pl.program_id(1) == 0)
        def _():
            xf = x_ref[...]
            a_ref[...] = (xf * _rstd(xf) * g_ref[...]).astype(BF16)

        acc = _dot(a_ref[...], b_ref[...].astype(BF16))
        vals = (acc,) if epi is None else epi(acc)
        for o_ref, val in zip(o_refs, vals):
            o_ref[...] = val.astype(o_ref.dtype)

    row = pl.BlockSpec((tm, k), lambda i, j: (i, 0))
    tile = pl.BlockSpec((tm, tn), lambda i, j: (i, j))
    out = pl.pallas_call(
        body, grid=grid,
        in_specs=[row, pl.BlockSpec((1, k), lambda i, j: (0, 0)), pl.BlockSpec((k, tn), lambda i, j: (0, j))] + r_in_specs,
        out_specs=[row] + [tile] * n_out + r_out_specs,
        out_shape=[jax.ShapeDtypeStruct((m, k), BF16)] + [jax.ShapeDtypeStruct((m, n), dt) for dt in out_dtypes] + r_out,
        scratch_shapes=r_scr, compiler_params=_params("arbitrary", "arbitrary"), name=name)(x, g, b, *r_in)
    return out[:1 + n_out], out[1 + n_out:]


def _mm_norm_res(a, b, g, res, tm, name, rider=None):
    m, k = a.shape
    n = b.shape[1]
    assert m % tm == 0, (name, m, tm)
    r_in, r_in_specs, r_out, r_out_specs, r_scr = _rider_specs(rider)
    n_rin = len(r_in)

    def body(a_ref, b_ref, g_ref, res_ref, *refs):
        z_ref, o_ref = refs[n_rin:n_rin + 2]
        _ride(rider, pl.program_id(0), m // tm, refs[:n_rin] + refs[n_rin + 2:])
        z = _dot(a_ref[...].astype(BF16), b_ref[...].astype(BF16))
        z_ref[...] = z
        o_ref[...] = res_ref[...] + z * _rstd(z) * g_ref[...]

    row = pl.BlockSpec((tm, n), lambda i: (i, 0))
    out = pl.pallas_call(
        body, grid=(m // tm,),
        in_specs=[pl.BlockSpec((tm, k), lambda i: (i, 0)), pl.BlockSpec((k, n), lambda i: (0, 0)),
                  pl.BlockSpec((1, n), lambda i: (0, 0)), row] + r_in_specs,
        out_specs=[row, row] + r_out_specs,
        out_shape=[jax.ShapeDtypeStruct((m, n), F32), jax.ShapeDtypeStruct((m, n), F32)] + r_out,
        scratch_shapes=r_scr, compiler_params=_params("arbitrary"), name=name)(a, b, g, res, *r_in)
    return out[0], out[1], out[2:]


def _rmsbwd_mm(x, g, dy, b, tm, tn, name, out_dtypes=(F32,), epi=None, extras=(), rider=None):
    m, k = x.shape
    n = b.shape[0]
    assert m % tm == 0 and n % tn == 0, (name, m, tm, n, tn)
    grid = (m // tm, n // tn)
    n_ex, n_out = len(extras), len(out_dtypes)
    r_in, r_in_specs, r_out, r_out_specs, r_scr = _rider_specs(rider)
    n_rin = len(r_in)

    def body(x_ref, g_ref, dy_ref, b_ref, *refs):
        ex_refs = refs[:n_ex]
        dx_ref, dg_ref = refs[n_ex + n_rin:n_ex + n_rin + 2]
        o_refs = refs[n_ex + n_rin + 2:n_ex + n_rin + 2 + n_out]
        i, j = pl.program_id(0), pl.program_id(1)
        _ride(rider, i * grid[1] + j, grid[0] * grid[1], refs[n_ex:n_ex + n_rin] + refs[n_ex + n_rin + 2 + n_out:])

        @pl.when((i == 0) & (j == 0))
        def _():
            dg_ref[...] = jnp.zeros_like(dg_ref)

        @pl.when(j == 0)
        def _():
            dx, dg = _rms_bwd_rows(x_ref[...], g_ref[...], dy_ref[...])
            dx_ref[...] = dx.astype(BF16)
            dg_ref[...] += jnp.sum(dg, axis=0, keepdims=True)

        acc = _dot(dx_ref[...], b_ref[...].astype(BF16), NT)
        vals = (acc,) if epi is None else epi(acc, *[r[...] for r in ex_refs])
        for o_ref, val in zip(o_refs, vals):
            o_ref[...] = val.astype(o_ref.dtype)

    row = pl.BlockSpec((tm, k), lambda i, j: (i, 0))
    vec = pl.BlockSpec((1, k), lambda i, j: (0, 0))
    tile = pl.BlockSpec((tm, tn), lambda i, j: (i, j))
    out = pl.pallas_call(
        body, grid=grid,
        in_specs=[row, vec, row, pl.BlockSpec((tn, k), lambda i, j: (j, 0))] + [tile] * n_ex + r_in_specs,
        out_specs=[row, vec] + [tile] * n_out + r_out_specs,
        out_shape=[jax.ShapeDtypeStruct((m, k), BF16), jax.ShapeDtypeStruct((1, k), F32)]
        + [jax.ShapeDtypeStruct((m, n), dt) for dt in out_dtypes] + r_out,
        scratch_shapes=r_scr, compiler_params=_params("arbitrary", "arbitrary"), name=name)(x, g, dy, b, *extras, *r_in)
    return out[:2 + n_out], out[2 + n_out:]


def _mm_rmsbwd_res(a, b, x, g, res, tm, name):
    m, k = a.shape
    n = b.shape[0]
    assert m % tm == 0, (name, m, tm)

    def body(a_ref, b_ref, x_ref, g_ref, res_ref, o_ref, dg_ref):
        @pl.when(pl.program_id(0) == 0)
        def _():
            dg_ref[...] = jnp.zeros_like(dg_ref)

        da = _dot(a_ref[...].astype(BF16), b_ref[...].astype(BF16), NT)
        dx, dg = _rms_bwd_rows(x_ref[...], g_ref[...], da)
        o_ref[...] = res_ref[...] + dx
        dg_ref[...] += jnp.sum(dg, axis=0, keepdims=True)

    row = pl.BlockSpec((tm, n), lambda i: (i, 0))
    vec = pl.BlockSpec((1, n), lambda i: (0, 0))
    return pl.pallas_call(
        body, grid=(m // tm,),
        in_specs=[pl.BlockSpec((tm, k), lambda i: (i, 0)), pl.BlockSpec((n, k), lambda i: (0, 0)), row, vec, row],
        out_specs=(row, vec), out_shape=(jax.ShapeDtypeStruct((m, n), F32), jax.ShapeDtypeStruct((1, n), F32)),
        compiler_params=_params("arbitrary"), name=name)(a, b, x, g, res)


def _rope_tables(pos, name):
    s = pos.shape[0]
    half = MLA_ROPE // 2
    inv_freq = jnp.asarray(np.power(np.float32(ROPE_BASE), -np.arange(0, MLA_ROPE, 2, dtype=np.float32) / MLA_ROPE)
                           .astype(np.float32).reshape(1, half))

    def body(p_ref, f_ref, c_ref, s_ref):
        ang = p_ref[...].astype(F32) * f_ref[...]
        c_ref[...] = jnp.cos(ang)
        s_ref[...] = jnp.sin(ang)

    return pl.pallas_call(
        body, out_shape=(jax.ShapeDtypeStruct((s, half), F32), jax.ShapeDtypeStruct((s, half), F32)), name=name)(pos, inv_freq)


def _lb_softmax(logits):
    m = jnp.max(logits, axis=0, keepdims=True)
    e = jnp.exp(logits - m)
    return e / jnp.sum(e, axis=0, keepdims=True)


def _lb_fwd(logits, name):
    def body(l_ref, o_ref):
        p = _lb_softmax(l_ref[...])
        acc = jnp.zeros_like(p[0:1])
        o_ref[0:1, :] = acc
        for layer in range(1, DEPTH):
            acc = acc + p[layer:layer + 1]
            o_ref[layer:layer + 1, :] = acc

    return pl.pallas_call(body, out_shape=jax.ShapeDtypeStruct(logits.shape, F32), name=name)(logits)


def _lb_bwd(logits, dlb, name):
    def body(l_ref, d_ref, o_ref):
        p = _lb_softmax(l_ref[...])
        d = d_ref[...]
        dp = [jnp.zeros_like(d[0:1])] * DEPTH
        run = jnp.zeros_like(d[0:1])
        for layer in range(DEPTH - 1, 0, -1):
            run = run + d[layer:layer + 1]
            dp[layer] = run
        inner = sum(p[layer:layer + 1] * dp[layer] for layer in range(DEPTH))
        for layer in range(DEPTH):
            o_ref[layer:layer + 1, :] = p[layer:layer + 1] * (dp[layer] - inner)

    return pl.pallas_call(body, out_shape=jax.ShapeDtypeStruct(logits.shape, F32), name=name)(logits, dlb)


def _loss(y, target, name):
    s, d = y.shape
    ts = _row_tile(s, 512)

    def body(y_ref, t_ref, l_ref, dy_ref):
        e = y_ref[...] - t_ref[...]
        dy_ref[...] = e / d

        @pl.when(pl.program_id(0) == 0)
        def _():
            l_ref[...] = jnp.zeros_like(l_ref)

        l_ref[...] += 0.5 * jnp.sum(jnp.mean(e * e, axis=-1, keepdims=True), axis=0, keepdims=True)

    row = pl.BlockSpec((ts, d), lambda i: (i, 0))
    return pl.pallas_call(
        body, grid=(s // ts,), in_specs=[row, row], out_specs=(pl.BlockSpec((1, 1), lambda i: (0, 0)), row),
        out_shape=(jax.ShapeDtypeStruct((1, 1), F32), jax.ShapeDtypeStruct((s, d), F32)),
        compiler_params=_params("arbitrary"), name=name)(y, target)


def _rope(t1, t2, cos, sin):
    return t1 * cos - t2 * sin, t1 * sin + t2 * cos


def _rope_bwd(d1, d2, cos, sin):
    return d1 * cos + d2 * sin, d2 * cos - d1 * sin


def _mla_qkv(proj, qn, kvn, w_uq, w_ukv, cos, sin, name):
    s = proj.shape[0]
    ts = _row_tile(s, 256)
    hh, half = MLA_HEADS, MLA_ROPE // 2

    def body(p_ref, qn_ref, kvn_ref, wq_ref, wkv_ref, c_ref, s_ref, cq_ref, ckv_ref, q_ref, k_ref, v_ref):
        p = p_ref[...]
        cq, ckv, kr = p[:, :MLA_Q_LORA], p[:, MLA_Q_LORA:MLA_Q_LORA + MLA_KV_LORA], p[:, MLA_Q_LORA + MLA_KV_LORA:]
        cqn = (cq * _rstd(cq) * qn_ref[...]).astype(BF16)
        ckvn = (ckv * _rstd(ckv) * kvn_ref[...]).astype(BF16)
        cq_ref[...] = cqn
        ckv_ref[...] = ckvn
        qe = _dot(cqn, wq_ref[...])
        kve = _dot(ckvn, wkv_ref[...])
        cos_, sin_ = c_ref[...], s_ref[...]
        k1, k2 = _rope(kr[:, :half], kr[:, half:], cos_, sin_)
        k1, k2 = k1.astype(BF16), k2.astype(BF16)
        for h in range(hh):
            b = h * MLA_QK
            q_ref[h, :, 0:MLA_NOPE] = qe[:, b:b + MLA_NOPE].astype(BF16)
            q1, q2 = _rope(qe[:, b + MLA_NOPE:b + MLA_NOPE + half], qe[:, b + MLA_NOPE + half:b + MLA_QK], cos_, sin_)
            q_ref[h, :, MLA_NOPE:MLA_NOPE + half] = q1.astype(BF16)
            q_ref[h, :, MLA_NOPE + half:MLA_QK] = q2.astype(BF16)
            b = h * (MLA_NOPE + MLA_V)
            k_ref[h, :, 0:MLA_NOPE] = kve[:, b:b + MLA_NOPE].astype(BF16)
            k_ref[h, :, MLA_NOPE:MLA_NOPE + half] = k1
            k_ref[h, :, MLA_NOPE + half:MLA_QK] = k2
            v_ref[h] = kve[:, b + MLA_NOPE:b + MLA_NOPE + MLA_V].astype(BF16)

    def row(w):
        return pl.BlockSpec((ts, w), lambda i: (i, 0))

    def full(shape):
        return pl.BlockSpec(shape, lambda i: (0,) * len(shape))

    def heads(w):
        return pl.BlockSpec((hh, ts, w), lambda i: (0, i, 0))

    return pl.pallas_call(
        body, grid=(s // ts,),
        in_specs=[row(MLA_IN), full(qn.shape), full(kvn.shape), full(w_uq.shape), full(w_ukv.shape), row(half), row(half)],
        out_specs=(row(MLA_Q_LORA), row(MLA_KV_LORA), heads(MLA_QK), heads(MLA_QK), heads(MLA_V)),
        out_shape=(jax.ShapeDtypeStruct((s, MLA_Q_LORA), BF16), jax.ShapeDtypeStruct((s, MLA_KV_LORA), BF16),
                   jax.ShapeDtypeStruct((hh, s, MLA_QK), BF16), jax.ShapeDtypeStruct((hh, s, MLA_QK), BF16),
                   jax.ShapeDtypeStruct((hh, s, MLA_V), BF16)),
        compiler_params=_params("parallel"), name=name)(proj, qn, kvn, w_uq, w_ukv, cos, sin)


ATTN_BLOCK = 1024
ATTN_FWD_TILE = (256, 512)
ATTN_BWD_TILE = (512, 512)


def _attn_block(s):
    return _row_tile(s, ATTN_BLOCK)


def _tile_sees(diag, q0, tq, k0, tk):
    if not diag:
        return True, False
    return k0 <= q0 + tq - 1, k0 + tk - 1 > q0


def _causal_pairs(nb, kv_major):
    if kv_major:
        pairs = [(i, j) for j in range(nb) for i in range(j, nb)]
    else:
        pairs = [(i, j) for i in range(nb) for j in range(i + 1)]
    return (jnp.asarray(np.array([p[0] for p in pairs], np.int32)), jnp.asarray(np.array([p[1] for p in pairs], np.int32)))


def _ride(rider, step, total, refs):
    if rider is None:
        return

    @pl.when(step == 0)
    def _():
        rider.start(*refs)

    @pl.when(step == (total * 7) // 8)
    def _():
        rider.middle(*refs)

    @pl.when(step == total - 1)
    def _():
        rider.finish(*refs)


def _rider_specs(rider):
    if rider is None:
        return [], [], [], [], []
    return (list(rider.operands), [_HBM] * len(rider.operands), list(rider.out_shapes), [_HBM] * len(rider.out_shapes),
            list(rider.scratch))


def _attn_fwd(q, k, v, name, rider=None):
    hh, s, _ = q.shape
    blk = _attn_block(s)
    tq, tk = min(blk, ATTN_FWD_TILE[0]), min(blk, ATTN_FWD_TILE[1])
    nb = s // blk
    it, jt = _causal_pairs(nb, kv_major=False)
    npair = int(it.shape[0])
    scale = MLA_QK ** -0.5
    c2 = scale * LOG2_E
    r_in, r_in_specs, r_out, r_out_specs, r_scr = _rider_specs(rider)
    n_rin, n_rout, n_rscr = len(r_in), len(r_out), len(r_scr)

    def body(it_ref, jt_ref, q_ref, k_ref, v_ref, *refs):
        r_refs = refs[:n_rin] + refs[n_rin + 2:n_rin + 2 + n_rout] + refs[len(refs) - n_rscr:]
        o_ref, lse_ref = refs[n_rin:n_rin + 2]
        m_scr, acc_scr, v_scr = refs[n_rin + 2 + n_rout:n_rin + 2 + n_rout + 3]
        h, t = pl.program_id(0), pl.program_id(1)
        step = h * npair + t
        _ride(rider, step, hh * npair, r_refs)
        i, j = it_ref[t], jt_ref[t]

        @pl.when(j == 0)
        def _():
            m_scr[...] = jnp.full_like(m_scr, -jnp.inf)
            acc_scr[...] = jnp.zeros_like(acc_scr)
            v_scr[:, MLA_V:] = jnp.ones((blk, MLA_V), BF16)

        def block(diag):
            v_scr[:, :MLA_V] = v_ref[...]
            for k0 in range(0, blk, tk):
                kb, vb = k_ref[k0:k0 + tk, :], v_scr[k0:k0 + tk, :]
                for q0 in range(0, blk, tq):
                    visible, needs_mask = _tile_sees(diag, q0, tq, k0, tk)
                    if not visible:
                        continue
                    rows = slice(q0, q0 + tq)
                    sc = _dot(q_ref[rows, :], kb, NT)
                    if needs_mask:
                        qpos = q0 + lax.broadcasted_iota(jnp.int32, (tq, tk), 0)
                        kpos = k0 + lax.broadcasted_iota(jnp.int32, (tq, tk), 1)
                        sc = jnp.where(qpos >= kpos, sc, -jnp.inf)
                    m_prev = m_scr[rows, :]
                    m_new = jnp.maximum(m_prev, jnp.max(sc, axis=-1, keepdims=True))
                    alpha = jnp.exp2((m_prev - m_new) * c2)
                    p = jnp.exp2((sc - m_new) * c2)
                    acc_scr[rows, :] = alpha * acc_scr[rows, :] + _dot(p.astype(BF16), vb)
                    m_scr[rows, :] = m_new

        @pl.when(j < i)
        def _():
            block(False)

        @pl.when(j == i)
        def _():
            block(True)
            acc = acc_scr[...]
            l = acc[:, MLA_V:MLA_V + 1]
            o_ref[...] = acc[:, :MLA_V] / l
            lse_ref[...] = m_scr[...] * scale + jnp.log(l)

    grid_spec = pltpu.PrefetchScalarGridSpec(
        num_scalar_prefetch=2, grid=(hh, npair),
        in_specs=[pl.BlockSpec((None, blk, MLA_QK), lambda h, t, it_, jt_: (h, it_[t], 0)),
                  pl.BlockSpec((None, blk, MLA_QK), lambda h, t, it_, jt_: (h, jt_[t], 0)),
                  pl.BlockSpec((None, blk, MLA_V), lambda h, t, it_, jt_: (h, jt_[t], 0))] + r_in_specs,
        out_specs=[pl.BlockSpec((blk, MLA_V), lambda h, t, it_, jt_: (it_[t], h)),
                   pl.BlockSpec((None, blk, 1), lambda h, t, it_, jt_: (h, it_[t], 0))] + r_out_specs,
        scratch_shapes=[pltpu.VMEM((blk, 1), F32), pltpu.VMEM((blk, 2 * MLA_V), F32),
                        pltpu.VMEM((blk, 2 * MLA_V), BF16)] + r_scr)
    out = pl.pallas_call(
        body, grid_spec=grid_spec,
        out_shape=[jax.ShapeDtypeStruct((s, hh * MLA_V), F32), jax.ShapeDtypeStruct((hh, s, 1), F32)] + r_out,
        compiler_params=_params("arbitrary", "arbitrary"), name=name)(it, jt, q, k, v, *r_in)
    return out[0], out[1], out[2:]


def _attn_delta(do, o, name):
    s = do.shape[0]
    ts = _row_tile(s, 512)
    hh = MLA_HEADS

    def body(do_ref, o_ref, d_ref):
        prod = do_ref[...] * o_ref[...]
        for h in range(hh):
            d_ref[h] = jnp.sum(prod[:, h * MLA_V:(h + 1) * MLA_V], axis=-1, keepdims=True)

    row = pl.BlockSpec((ts, hh * MLA_V), lambda i: (i, 0))
    return pl.pallas_call(
        body, grid=(s // ts,), in_specs=[row, row], out_specs=pl.BlockSpec((hh, ts, 1), lambda i: (0, i, 0)),
        out_shape=jax.ShapeDtypeStruct((hh, s, 1), F32), compiler_params=_params("parallel"), name=name)(do, o)


def _attn_bwd(q, k, v, do, lse_row, delta_row, name, rider=None):
    hh, s, _ = q.shape
    blk = _attn_block(s)
    tq, tk = min(blk, ATTN_BWD_TILE[0]), min(blk, ATTN_BWD_TILE[1])
    nb = s // blk
    it, jt = _causal_pairs(nb, kv_major=True)
    npair = int(it.shape[0])
    scale = MLA_QK ** -0.5
    c2 = scale * LOG2_E
    r_in, r_in_specs, r_out, r_out_specs, r_scr = _rider_specs(rider)
    n_rin, n_rout, n_rscr = len(r_in), len(r_out), len(r_scr)

    def body(it_ref, jt_ref, q_ref, k_ref, v_ref, do_ref, lse_ref, dl_ref, *refs):
        r_refs = refs[:n_rin] + refs[n_rin + 3:n_rin + 3 + n_rout] + refs[len(refs) - n_rscr:]
        dq_ref, dk_ref, dv_ref = refs[n_rin:n_rin + 3]
        h, t = pl.program_id(0), pl.program_id(1)
        step = h * npair + t
        _ride(rider, step, hh * npair, r_refs)
        i, j = it_ref[t], jt_ref[t]

        @pl.when(t == 0)
        def _():
            dq_ref[...] = jnp.zeros_like(dq_ref)

        def block(diag):
            if diag:
                dk_ref[...] = jnp.zeros_like(dk_ref)
                dv_ref[...] = jnp.zeros_like(dv_ref)
            for q0 in range(0, blk, tq):
                qb = q_ref[q0:q0 + tq, :]
                dob = do_ref[q0:q0 + tq, :].astype(BF16)
                lse2 = lse_ref[:, q0:q0 + tq] * LOG2_E
                dl = dl_ref[:, q0:q0 + tq]
                dq = None
                for k0 in range(0, blk, tk):
                    visible, needs_mask = _tile_sees(diag, q0, tq, k0, tk)
                    if not visible:
                        continue
                    kb, vb = k_ref[k0:k0 + tk, :], v_ref[k0:k0 + tk, :]
                    pt = jnp.exp2(_dot(kb, qb, NT) * c2 - lse2)
                    if needs_mask:
                        kpos = k0 + lax.broadcasted_iota(jnp.int32, (tk, tq), 0)
                        qpos = q0 + lax.broadcasted_iota(jnp.int32, (tk, tq), 1)
                        pt = jnp.where(qpos >= kpos, pt, 0.0)
                    dv_ref[k0:k0 + tk, :] += _dot(pt.astype(BF16), dob)
                    dpt = _dot(vb, dob, NT)
                    dst = (pt * (dpt - dl) * scale).astype(BF16)
                    dk_ref[k0:k0 + tk, :] += _dot(dst, qb)
                    part = _dot(dst, kb, TN)
                    dq = part if dq is None else dq + part
                rows = pl.ds(pl.multiple_of(i * blk + q0, tq), tq)
                dq_ref[rows, :] += dq

        @pl.when(i == j)
        def _():
            block(True)

        @pl.when(i > j)
        def _():
            block(False)

    grid_spec = pltpu.PrefetchScalarGridSpec(
        num_scalar_prefetch=2, grid=(hh, npair),
        in_specs=[pl.BlockSpec((None, blk, MLA_QK), lambda h, t, it_, jt_: (h, it_[t], 0)),
                  pl.BlockSpec((None, blk, MLA_QK), lambda h, t, it_, jt_: (h, jt_[t], 0)),
                  pl.BlockSpec((None, blk, MLA_V), lambda h, t, it_, jt_: (h, jt_[t], 0)),
                  pl.BlockSpec((blk, MLA_V), lambda h, t, it_, jt_: (it_[t], h)),
                  pl.BlockSpec((None, 1, blk), lambda h, t, it_, jt_: (h, 0, it_[t])),
                  pl.BlockSpec((None, 1, blk), lambda h, t, it_, jt_: (h, 0, it_[t]))] + r_in_specs,
        out_specs=[pl.BlockSpec((None, s, MLA_QK), lambda h, t, it_, jt_: (h, 0, 0)),
                   pl.BlockSpec((None, blk, MLA_QK), lambda h, t, it_, jt_: (h, jt_[t], 0)),
                   pl.BlockSpec((None, blk, MLA_V), lambda h, t, it_, jt_: (h, jt_[t], 0))] + r_out_specs,
        scratch_shapes=r_scr)
    out = pl.pallas_call(
        body, grid_spec=grid_spec,
        out_shape=[jax.ShapeDtypeStruct((hh, s, MLA_QK), F32), jax.ShapeDtypeStruct((hh, s, MLA_QK), F32),
                   jax.ShapeDtypeStruct((hh, s, MLA_V), F32)] + r_out,
        compiler_params=_params("arbitrary", "arbitrary"), name=name)(it, jt, q, k, v, do, lse_row, delta_row, *r_in)
    return out[0], out[1], out[2], out[3:]


def _mla_bwd_mid(dq, dk, dv, cos, sin, proj, qn, kvn, w_uq, w_ukv, name):
    s = proj.shape[0]
    ts = _row_tile(s, 256)
    hh, half = MLA_HEADS, MLA_ROPE // 2
    nq, nkv = hh * MLA_QK, hh * (MLA_NOPE + MLA_V)

    def body(dq_ref, dk_ref, dv_ref, c_ref, s_ref, p_ref, qn_ref, kvn_ref, wq_ref, wkv_ref,
             dqe_ref, dkve_ref, dp_ref, dqn_ref, dkvn_ref):
        cos_, sin_ = c_ref[...], s_ref[...]
        dkr1 = jnp.zeros((ts, half), F32)
        dkr2 = jnp.zeros((ts, half), F32)
        for h in range(hh):
            dqh, dkh = dq_ref[h], dk_ref[h]
            b = h * MLA_QK
            dqe_ref[:, b:b + MLA_NOPE] = dqh[:, :MLA_NOPE].astype(BF16)
            d1, d2 = _rope_bwd(dqh[:, MLA_NOPE:MLA_NOPE + half], dqh[:, MLA_NOPE + half:], cos_, sin_)
            dqe_ref[:, b + MLA_NOPE:b + MLA_NOPE + half] = d1.astype(BF16)
            dqe_ref[:, b + MLA_NOPE + half:b + MLA_QK] = d2.astype(BF16)
            b = h * (MLA_NOPE + MLA_V)
            dkve_ref[:, b:b + MLA_NOPE] = dkh[:, :MLA_NOPE].astype(BF16)
            dkve_ref[:, b + MLA_NOPE:b + MLA_NOPE + MLA_V] = dv_ref[h].astype(BF16)
            dkr1 = dkr1 + dkh[:, MLA_NOPE:MLA_NOPE + half]
            dkr2 = dkr2 + dkh[:, MLA_NOPE + half:]
        dkr1, dkr2 = _rope_bwd(dkr1, dkr2, cos_, sin_)
        dcqn = _dot(dqe_ref[...], wq_ref[...], NT)
        dckvn = _dot(dkve_ref[...], wkv_ref[...], NT)
        p = p_ref[...]
        dcq, dqn = _rms_bwd_rows(p[:, :MLA_Q_LORA], qn_ref[...], dcqn)
        dckv, dkvn = _rms_bwd_rows(p[:, MLA_Q_LORA:MLA_Q_LORA + MLA_KV_LORA], kvn_ref[...], dckvn)
        dp_ref[:, :MLA_Q_LORA] = dcq.astype(BF16)
        dp_ref[:, MLA_Q_LORA:MLA_Q_LORA + MLA_KV_LORA] = dckv.astype(BF16)
        dp_ref[:, MLA_Q_LORA + MLA_KV_LORA:MLA_Q_LORA + MLA_KV_LORA + half] = dkr1.astype(BF16)
        dp_ref[:, MLA_Q_LORA + MLA_KV_LORA + half:] = dkr2.astype(BF16)

        @pl.when(pl.program_id(0) == 0)
        def _():
            dqn_ref[...] = jnp.zeros_like(dqn_ref)
            dkvn_ref[...] = jnp.zeros_like(dkvn_ref)

        dqn_ref[...] += jnp.sum(dqn, axis=0, keepdims=True)
        dkvn_ref[...] += jnp.sum(dkvn, axis=0, keepdims=True)

    def row(w):
        return pl.BlockSpec((ts, w), lambda i: (i, 0))

    def full(shape):
        return pl.BlockSpec(shape, lambda i: (0,) * len(shape))

    def heads(w):
        return pl.BlockSpec((hh, ts, w), lambda i: (0, i, 0))

    return pl.pallas_call(
        body, grid=(s // ts,),
        in_specs=[heads(MLA_QK), heads(MLA_QK), heads(MLA_V), row(half), row(half), row(MLA_IN),
                  full(qn.shape), full(kvn.shape), full(w_uq.shape), full(w_ukv.shape)],
        out_specs=(row(nq), row(nkv), row(MLA_IN), full(qn.shape), full(kvn.shape)),
        out_shape=(jax.ShapeDtypeStruct((s, nq), BF16), jax.ShapeDtypeStruct((s, nkv), BF16),
                   jax.ShapeDtypeStruct((s, MLA_IN), BF16), jax.ShapeDtypeStruct(qn.shape, F32),
                   jax.ShapeDtypeStruct(kvn.shape, F32)),
        compiler_params=_params("arbitrary"), name=name)(dq, dk, dv, cos, sin, proj, qn, kvn, w_uq, w_ukv)


HGRN_TILE = 128


def _chunk_masks(t):
    r = lax.broadcasted_iota(jnp.int32, (t, t), 0)
    c = lax.broadcasted_iota(jnp.int32, (t, t), 1)
    same = (r // HGRN_CHUNK) == (c // HGRN_CHUNK)
    return r, c, same


def _hgrn_gates(p, lb):
    hk = HGRN_HEADS * HGRN_D
    qx, fx, ix, gx = p[:, :hk], p[:, hk:2 * hk], p[:, 2 * hk:3 * hk], p[:, 3 * hk:]
    sig_f = jax.nn.sigmoid(fx)
    f = lb + (1.0 - lb) * sig_f
    sig_q = jax.nn.sigmoid(qx)
    t = p.shape[0]
    r, c, same = _chunk_masks(t)
    lower = jnp.where(same & (c <= r), 1.0, 0.0).astype(F32)
    b = _dot_select(lower, jnp.log(f), 3)
    b3 = b.reshape(t // HGRN_CHUNK, HGRN_CHUNK, hk)
    bref = jnp.broadcast_to(b3[:, HGRN_CHUNK // 2:HGRN_CHUNK // 2 + 1, :], b3.shape).reshape(t, hk)
    blast = jnp.broadcast_to(b3[:, HGRN_CHUNK - 1:, :], b3.shape).reshape(t, hk)
    return qx, ix, gx, sig_f, f, sig_q, b, bref, blast


def _hgrn_fwd(proj, lb, onorm, name, rider=None):
    s = proj.shape[0]
    t = _row_tile(s, HGRN_TILE)
    nc = t // HGRN_CHUNK
    hh, dd, hk = HGRN_HEADS, HGRN_D, HGRN_HEADS * HGRN_D
    r_in, r_in_specs, r_out, r_out_specs, r_scr = _rider_specs(rider)
    n_rin, n_rout = len(r_in), len(r_out)

    def body(p_ref, lb_ref, on_ref, *refs):
        y_ref, o_ref, st_ref = refs[n_rin:n_rin + 3]
        st_scr = refs[n_rin + 3 + n_rout]
        _ride(rider, pl.program_id(0), s // t, refs[:n_rin] + refs[n_rin + 3:n_rin + 3 + n_rout] + refs[n_rin + 4 + n_rout:])

        @pl.when(pl.program_id(0) == 0)
        def _():
            st_scr[...] = jnp.zeros_like(st_scr)

        qx, ix, gx, _, f, sig_q, b, bref, blast = _hgrn_gates(p_ref[...], lb_ref[...])
        q = qx * sig_q
        k = 1.0 - f
        r, c, same = _chunk_masks(t)
        causal = same & (c <= r)
        for h in range(hh):
            sl = slice(h * dd, (h + 1) * dd)
            bh, brefh, blasth, qh, kh = b[:, sl], bref[:, sl], blast[:, sl], q[:, sl], k[:, sl]
            vh = ix[:, sl].astype(BF16)
            q_rel = (qh * jnp.exp(bh - brefh)).astype(BF16)
            k_rel = (kh * jnp.exp(brefh - bh)).astype(BF16)
            a = jnp.where(causal, _dot(q_rel, k_rel, NT), 0.0)
            o_intra = _dot(a.astype(BF16), vh)
            q_dec = (qh * jnp.exp(bh)).astype(BF16)
            k_dec = (kh * jnp.exp(blasth - bh)).astype(BF16)
            dec = jnp.exp(blasth)
            pieces = []
            for ci in range(nc):
                rows = slice(ci * HGRN_CHUNK, (ci + 1) * HGRN_CHUNK)
                st = st_scr[h]
                if ci == 0:
                    st_ref[h] = st
                pieces.append(_dot(q_dec[rows], st.astype(BF16), NT))
                st_scr[h] = st * dec[ci * HGRN_CHUNK:ci * HGRN_CHUNK + 1, :] + _dot(vh[rows], k_dec[rows], TN)
            oh = o_intra + jnp.concatenate(pieces, axis=0)
            o_ref[:, sl] = oh
            gate = gx[:, sl] * jax.nn.sigmoid(gx[:, sl])
            y_ref[:, sl] = (oh * _rstd(oh) * on_ref[...] * gate).astype(BF16)

    out = pl.pallas_call(
        body, grid=(s // t,),
        in_specs=[pl.BlockSpec((t, 4 * hk), lambda i: (i, 0)), pl.BlockSpec((1, hk), lambda i: (0, 0)),
                  pl.BlockSpec((1, dd), lambda i: (0, 0))] + r_in_specs,
        out_specs=[pl.BlockSpec((t, hk), lambda i: (i, 0)), pl.BlockSpec((t, hk), lambda i: (i, 0)),
                   pl.BlockSpec((None, hh, dd, dd), lambda i: (i, 0, 0, 0))] + r_out_specs,
        out_shape=[jax.ShapeDtypeStruct((s, hk), BF16), jax.ShapeDtypeStruct((s, hk), F32),
                   jax.ShapeDtypeStruct((s // t, hh, dd, dd), F32)] + r_out,
        scratch_shapes=[pltpu.VMEM((hh, dd, dd), F32)] + r_scr,
        compiler_params=_params("arbitrary"), name=name)(proj, lb, onorm, *r_in)
    return out[0], out[1], out[2], out[3:]


def _hgrn_bwd(proj, lb, onorm, o, states, dy, name, rider=None):
    s = proj.shape[0]
    t = _row_tile(s, HGRN_TILE)
    nt = s // t
    nc = t // HGRN_CHUNK
    hh, dd, hk = HGRN_HEADS, HGRN_D, HGRN_HEADS * HGRN_D
    r_in, r_in_specs, r_out, r_out_specs, r_scr = _rider_specs(rider)
    n_rin, n_rout, n_rscr = len(r_in), len(r_out), len(r_scr)

    def body(p_ref, lb_ref, on_ref, o_ref, st_ref, dy_ref, *refs):
        r_refs = refs[:n_rin] + refs[n_rin + 3:n_rin + 3 + n_rout] + refs[len(refs) - n_rscr:]
        dp_ref, dlb_ref, don_ref = refs[n_rin:n_rin + 3]
        dst_scr, cat_scr, ext_scr, dk_scr, dq_scr = refs[n_rin + 3 + n_rout:n_rin + 3 + n_rout + 5]
        _ride(rider, pl.program_id(0), nt, r_refs)

        @pl.when(pl.program_id(0) == 0)
        def _():
            dst_scr[...] = jnp.zeros_like(dst_scr)
            dlb_ref[...] = jnp.zeros_like(dlb_ref)
            don_ref[...] = jnp.zeros_like(don_ref)

        lbv = lb_ref[...]
        qx, ix, gx, sig_f, f, sig_q, b, bref, blast = _hgrn_gates(p_ref[...], lbv)
        q = qx * sig_q
        k = 1.0 - f
        r, c, same = _chunk_masks(t)
        causal = same & (c <= r)
        on = on_ref[...]
        don = jnp.zeros((1, dd), F32)
        for h in range(hh):
            sl = slice(h * dd, (h + 1) * dd)
            oh = o_ref[:, sl]
            dyh = dy_ref[:, sl]
            gxh = gx[:, sl]
            sig_g = jax.nn.sigmoid(gxh)
            rs = _rstd(oh)
            dgate = dyh * (oh * rs * on)
            dp_ref[:, 3 * hk + h * dd:3 * hk + (h + 1) * dd] = (dgate * (sig_g * (1.0 + gxh * (1.0 - sig_g)))).astype(BF16)
            do, donh = _rms_bwd_rows(oh, on, dyh * (gxh * sig_g))
            don = don + jnp.sum(donh, axis=0, keepdims=True)
            dob = do.astype(BF16)
            bh, brefh, blasth, qh, kh = b[:, sl], bref[:, sl], blast[:, sl], q[:, sl], k[:, sl]
            vh = ix[:, sl].astype(BF16)
            e_qr, e_kr, e_qd, e_kd = jnp.exp(bh - brefh), jnp.exp(brefh - bh), jnp.exp(bh), jnp.exp(blasth - bh)
            dec = jnp.exp(blasth)
            q_rel, k_rel, q_dec, k_dec = qh * e_qr, kh * e_kr, qh * e_qd, kh * e_kd
            q_relb, k_relb, q_decb, k_decb = q_rel.astype(BF16), k_rel.astype(BF16), q_dec.astype(BF16), k_dec.astype(BF16)
            a = jnp.where(causal, _dot(q_relb, k_relb, NT), 0.0).astype(BF16)
            dv = _dot(a, dob, TN)
            da = jnp.where(causal, _dot(dob, vh, NT), 0.0).astype(BF16)
            dq_rel = _dot(da, k_relb)
            dk_rel = _dot(da, q_relb, TN)
            sts = [st_ref[h]]
            for ci in range(nc - 1):
                rows = slice(ci * HGRN_CHUNK, (ci + 1) * HGRN_CHUNK)
                sts.append(sts[-1] * dec[ci * HGRN_CHUNK:ci * HGRN_CHUNK + 1, :] + _dot(vh[rows], k_decb[rows], TN))
            dq_dec, dk_dec, dv_inter, ddec = [None] * nc, [None] * nc, [None] * nc, [None] * nc
            for ci in range(nc - 1, -1, -1):
                rows = slice(ci * HGRN_CHUNK, (ci + 1) * HGRN_CHUNK)
                st = sts[ci]
                dst = dst_scr[h]
                dstb = dst.astype(BF16)
                dq_dec[ci] = _dot(dob[rows], st.astype(BF16))
                dk_dec[ci] = _dot(vh[rows], dstb)
                dv_inter[ci] = _dot(k_decb[rows], dstb, NT)
                ddec[ci] = jnp.broadcast_to(jnp.sum(dst * st, axis=0, keepdims=True), (HGRN_CHUNK, dd))
                dst_scr[h] = dst * dec[ci * HGRN_CHUNK:ci * HGRN_CHUNK + 1, :] + _dot(dob[rows], q_decb[rows], TN)
            dq_dec = jnp.concatenate(dq_dec, axis=0)
            dk_dec = jnp.concatenate(dk_dec, axis=0)
            dv = dv + jnp.concatenate(dv_inter, axis=0)
            ddec = jnp.concatenate(ddec, axis=0)
            dp_ref[:, 2 * hk + h * dd:2 * hk + (h + 1) * dd] = dv.astype(BF16)
            dq_scr[:, sl] = dq_rel * e_qr + dq_dec * e_qd
            dk_scr[:, sl] = dk_rel * e_kr + dk_dec * e_kd
            g_qr, g_kr, g_qd, g_kd = dq_rel * q_rel, dk_rel * k_rel, dq_dec * q_dec, dk_dec * k_dec
            cat_scr[0:t, sl] = g_qr - g_kr + g_qd - g_kd
            cat_scr[t:2 * t, sl] = g_kr - g_qr
            cat_scr[2 * t:3 * t, sl] = g_kd
            ext_scr[:, sl] = ddec * dec
        upper = jnp.where(same & (c >= r), 1.0, 0.0).astype(F32)
        to_ref = jnp.where(same & (r % HGRN_CHUNK <= HGRN_CHUNK // 2), 1.0, 0.0).astype(F32)
        to_all = jnp.where(same, 1.0, 0.0).astype(F32)
        dlogf = _dot_select(jnp.concatenate([upper, to_ref, to_all], axis=1), cat_scr[...], 2) + ext_scr[...]
        df = dlogf / f - dk_scr[...]
        dp_ref[:, hk:2 * hk] = (df * (1.0 - lbv) * sig_f * (1.0 - sig_f)).astype(BF16)
        dp_ref[:, 0:hk] = (dq_scr[...] * (sig_q * (1.0 + qx * (1.0 - sig_q)))).astype(BF16)
        dlb_ref[...] += jnp.sum(df * (1.0 - sig_f), axis=0, keepdims=True)
        don_ref[...] += don

    def rev(i):
        return nt - 1 - i

    out = pl.pallas_call(
        body, grid=(nt,),
        in_specs=[pl.BlockSpec((t, 4 * hk), lambda i: (rev(i), 0)), pl.BlockSpec((1, hk), lambda i: (0, 0)),
                  pl.BlockSpec((1, dd), lambda i: (0, 0)), pl.BlockSpec((t, hk), lambda i: (rev(i), 0)),
                  pl.BlockSpec((None, hh, dd, dd), lambda i: (rev(i), 0, 0, 0)),
                  pl.BlockSpec((t, hk), lambda i: (rev(i), 0))] + r_in_specs,
        out_specs=[pl.BlockSpec((t, 4 * hk), lambda i: (rev(i), 0)), pl.BlockSpec((1, hk), lambda i: (0, 0)),
                   pl.BlockSpec((1, dd), lambda i: (0, 0))] + r_out_specs,
        out_shape=[jax.ShapeDtypeStruct((s, 4 * hk), BF16), jax.ShapeDtypeStruct((1, hk), F32),
                   jax.ShapeDtypeStruct((1, dd), F32)] + r_out,
        scratch_shapes=[pltpu.VMEM((hh, dd, dd), F32), pltpu.VMEM((3 * t, hk), F32), pltpu.VMEM((t, hk), F32),
                        pltpu.VMEM((t, hk), F32), pltpu.VMEM((t, hk), F32)] + r_scr,
        compiler_params=_params("arbitrary"), name=name)(proj, lb, onorm, o, states, dy, *r_in)
    return out[0], out[1], out[2], out[3:]


def _adamw_update(w, g, m, v):
    nm = ADAM_B1 * m + (1.0 - ADAM_B1) * g
    nv = ADAM_B2 * v + (1.0 - ADAM_B2) * (g * g)
    m_hat = nm / (1.0 - ADAM_B1 ** ADAM_STEP)
    v_hat = nv / (1.0 - ADAM_B2 ** ADAM_STEP)
    return -ADAM_LR * (m_hat / (jnp.sqrt(v_hat) + ADAM_EPS) + ADAM_WD * w), nm, nv


def _adamw(w, g, m, v, name):
    rows, cols = w.shape
    tr = _divisor_tile(rows, 256, 8)

    def body(w_ref, g_ref, m_ref, v_ref, d_ref, nm_ref, nv_ref):
        d_ref[...], nm_ref[...], nv_ref[...] = _adamw_update(w_ref[...], g_ref[...], m_ref[...], v_ref[...])

    blk = pl.BlockSpec((tr, cols), lambda i: (i, 0))
    shp = jax.ShapeDtypeStruct((rows, cols), F32)
    return pl.pallas_call(
        body, grid=(rows // tr,), in_specs=[blk] * 4, out_specs=(blk,) * 3, out_shape=(shp,) * 3,
        compiler_params=_params("parallel"), name=name)(w, g, m, v)


ADAMW_BLOCK_ELEMS = 128 * 1024


def _adamw_layers(w, lands, m, v, name, rider=None):
    ll, rows, cols = w.shape
    tr = _divisor_tile(rows, max(16, ADAMW_BLOCK_ELEMS // cols), 16)
    r_in, r_in_specs, r_out, r_out_specs, r_scr = _rider_specs(rider)
    n_rin = len(r_in)

    def body(w_ref, m_ref, v_ref, *refs):
        land_refs = refs[:ll]
        g_out, d_ref, nm_ref, nv_ref = refs[ll + n_rin:ll + n_rin + 4]
        layer = pl.program_id(0)
        _ride(rider, layer * (rows // tr) + pl.program_id(1), ll * (rows // tr), refs[ll:ll + n_rin] + refs[ll + n_rin + 4:])
        for k in range(ll):
            @pl.when(layer == k)
            def _(k=k):
                g = land_refs[k][0].astype(F32)
                for slot in range(1, N_DEV):
                    g = g + land_refs[k][slot].astype(F32)
                g_out[...] = g

        d_ref[...], nm_ref[...], nv_ref[...] = _adamw_update(w_ref[...], g_out[...], m_ref[...], v_ref[...])

    stacked = pl.BlockSpec((None, tr, cols), lambda l, i: (l, i, 0))

    def one(k):
        return pl.BlockSpec((N_DEV, tr, cols), lambda l, i: (0, jnp.where(l == k, i, 0), 0))

    shp = jax.ShapeDtypeStruct(w.shape, F32)
    out = pl.pallas_call(
        body, grid=(ll, rows // tr), in_specs=[stacked] * 3 + [one(k) for k in range(ll)] + r_in_specs,
        out_specs=[stacked] * 4 + r_out_specs, out_shape=[shp] * 4 + r_out, scratch_shapes=r_scr,
        compiler_params=_params("arbitrary", "arbitrary"), name=name)(w, m, v, *lands, *r_in)
    return out[0], out[1], out[2], out[3], out[4:]


_HBM = pl.BlockSpec(memory_space=pltpu.HBM)
_MESH = pl.DeviceIdType.MESH


class _GatherRide:
    def __init__(self, blocks, cuts):
        self.operands = list(blocks)
        self.cuts = list(cuts)
        self.out_shapes = []
        for b, cut in zip(blocks, cuts):
            r, c = b.shape
            shape = {"rows": (N_DEV * r, c), "cols": (r, N_DEV * c), "slots": (N_DEV, r, c)}[cut]
            self.out_shapes.append(jax.ShapeDtypeStruct(shape, b.dtype))
        n = len(blocks)
        self.scratch = [pltpu.SemaphoreType.DMA((7 * n,)), pltpu.SemaphoreType.DMA((7 * n,)), pltpu.SemaphoreType.DMA((n,))]

    def _parts(self, *refs):
        n = len(self.operands)
        x_refs, out_refs = refs[:n], refs[n:2 * n]
        send_sems, recv_sems, local_sems = refs[2 * n:]
        x, y, c = lax.axis_index("x"), lax.axis_index("y"), lax.axis_index("c")
        me, sibling = (x, y, c), (x, y, 1 - c)
        chips = [(1 - x, y), (x, 1 - y), (1 - x, 1 - y)]
        mine, first, passed, landed, from_sibling = [], [], [], [], []
        for e in range(n):
            x_ref, out_ref, cut = x_refs[e], out_refs[e], self.cuts[e]
            r, cc = x_ref.shape

            def place(px, py, pc, out_ref=out_ref, cut=cut, r=r, cc=cc):
                p = 4 * px + 2 * py + pc
                if cut == "rows":
                    return out_ref.at[pl.ds(pl.multiple_of(p * r, r), r), :]
                if cut == "cols":
                    return out_ref.at[:, pl.ds(pl.multiple_of(p * cc, cc), cc)]
                return out_ref.at[p]

            def copy(k, block, to, src=None, place=place, e=e):
                return pltpu.make_async_remote_copy(
                    src_ref=place(*block) if src is None else src, dst_ref=place(*block), send_sem=send_sems.at[7 * e + k],
                    recv_sem=recv_sems.at[7 * e + k], device_id=to, device_id_type=_MESH)

            mine.append(pltpu.make_async_copy(x_ref, place(*me), local_sems.at[e]))
            first += [copy(0, me, sibling, src=x_ref)] + [copy(1 + j, me, (*chip, c), src=x_ref) for j, chip in enumerate(chips)]
            passed += [copy(4 + j, (*chip, c), sibling) for j, chip in enumerate(chips)]
            landed += [copy(1 + j, (*chip, c), me) for j, chip in enumerate(chips)]
            from_sibling += [copy(0, sibling, me)] + [copy(4 + j, (*chip, 1 - c), me) for j, chip in enumerate(chips)]
        return mine, first, passed, landed, from_sibling

    def start(self, *refs):
        mine, first, _, _, _ = self._parts(*refs)
        for cp in mine + first:
            cp.start()

    def middle(self, *refs):
        _, _, passed, landed, _ = self._parts(*refs)
        for got, fwd in zip(landed, passed):
            got.wait_recv()
            fwd.start()

    def finish(self, *refs):
        mine, first, passed, _, from_sibling = self._parts(*refs)
        for cp in from_sibling:
            cp.wait_recv()
        for cp in first + passed:
            cp.wait_send()
        for cp in mine:
            cp.wait()


class _ExchangeRide:
    def __init__(self, sends):
        self.operands = list(sends)
        self.out_shapes = [jax.ShapeDtypeStruct(s.shape, s.dtype) for s in sends]
        n = len(sends)
        self.scratch = [pltpu.SemaphoreType.DMA((7 * n,)), pltpu.SemaphoreType.DMA((7 * n,)), pltpu.SemaphoreType.DMA((n,))]

    def _parts(self, *refs):
        n = len(self.operands)
        s_refs, land_refs = refs[:n], refs[n:2 * n]
        send_sems, recv_sems, local_sems = refs[2 * n:]
        x, y, c = lax.axis_index("x"), lax.axis_index("y"), lax.axis_index("c")
        me = 4 * x + 2 * y + c
        own, sends, recvs = [], [], []
        for e in range(n):
            s_ref, land_ref = s_refs[e], land_refs[e]
            own.append(pltpu.make_async_copy(s_ref.at[me], land_ref.at[me], local_sems.at[e]))
            for rel in range(1, N_DEV):
                px = 1 - x if rel & 4 else x
                py = 1 - y if rel & 2 else y
                pc = 1 - c if rel & 1 else c
                peer = 4 * px + 2 * py + pc
                k = 7 * e + rel - 1
                sends.append(pltpu.make_async_remote_copy(
                    src_ref=s_ref.at[peer], dst_ref=land_ref.at[me], send_sem=send_sems.at[k], recv_sem=recv_sems.at[k],
                    device_id=(px, py, pc), device_id_type=_MESH))
                recvs.append(pltpu.make_async_remote_copy(
                    src_ref=s_ref.at[me], dst_ref=land_ref.at[peer], send_sem=send_sems.at[k], recv_sem=recv_sems.at[k],
                    device_id=(px, py, pc), device_id_type=_MESH))
        return own, sends, recvs

    def start(self, *refs):
        own, sends, _ = self._parts(*refs)
        for cp in own + sends:
            cp.start()

    def middle(self, *refs):
        pass

    def finish(self, *refs):
        own, sends, recvs = self._parts(*refs)
        for cp in recvs:
            cp.wait_recv()
        for cp in sends:
            cp.wait_send()
        for cp in own:
            cp.wait()


def _run_alone(rider, name):
    def body(*refs):
        rider.start(*refs)
        rider.middle(*refs)
        rider.finish(*refs)

    return pl.pallas_call(
        body, out_shape=rider.out_shapes, in_specs=[_HBM] * len(rider.operands), out_specs=[_HBM] * len(rider.out_shapes),
        scratch_shapes=rider.scratch, name=name)(*rider.operands)


def _all_gather(xs, name):
    return _run_alone(_GatherRide([xs], ["slots"]), name)[0]


def _sum_slots(parts, name):
    _, rows, cols = parts.shape
    tr = _divisor_tile(rows, 256, 16)

    def body(p_ref, o_ref):
        acc = p_ref[0].astype(F32)
        for slot in range(1, N_DEV):
            acc = acc + p_ref[slot].astype(F32)
        o_ref[...] = acc

    return pl.pallas_call(
        body, grid=(rows // tr,), in_specs=[pl.BlockSpec((N_DEV, tr, cols), lambda i: (0, i, 0))],
        out_specs=pl.BlockSpec((tr, cols), lambda i: (i, 0)), out_shape=jax.ShapeDtypeStruct((rows, cols), F32),
        compiler_params=_params("parallel"), name=name)(parts)


def _carry(rode, key, riders, call):
    rider = riders.get(key)
    res = call(rider)
    if rider is None:
        return res
    res, rode[key] = res
    return res


def _kept(rode, key, riders, brought):
    if key in riders:
        rode[key] = brought


def _mlp_fwd(h, g_pre, g_post, w1, w2, tag, riders):
    rode = {}
    tm = _row_tile(h.shape[0], 2048)
    (a, r2), brought = _norm_mm(h, g_pre, w1, tm, 1024, f"{tag}_up", out_dtypes=(BF16,),
                                epi=lambda acc: (jnp.square(jnp.maximum(acc, 0.0)),), rider=riders.get("up"))
    _kept(rode, "up", riders, brought)
    z, out, brought = _mm_norm_res(r2, w2, g_post, h, 512, f"{tag}_down", riders.get("down"))
    _kept(rode, "down", riders, brought)
    return out, (h, a, r2, z), rode


def _mlp_bwd(dh, saved, g_pre, g_post, w1, w2, tag, riders):
    h, a, r2, z = saved
    rode = {}
    tm = _row_tile(h.shape[0], 1024)
    (dz, dg_post, du), brought = _rmsbwd_mm(
        z, g_post, dh, w2, tm, 1024, f"{tag}_ddown", out_dtypes=(BF16,), extras=(r2,),
        epi=lambda acc, rr: (acc * (2.0 * jnp.sqrt(rr.astype(F32))),), rider=riders.get("ddown"))
    _kept(rode, "ddown", riders, brought)
    dw2 = _carry(rode, "dw2", riders, lambda r: _mm(
        r2, dz, "tn", 512, 1024, f"{tag}_dw2", out_dtypes=(BF16,), shard="rows", rider=r))
    dw1 = _mm(a, du, "tn", 1024, 512, f"{tag}_dw1", out_dtypes=(BF16,), shard="cols")
    dh_in, dg_pre = _mm_rmsbwd_res(du, w1, h, g_pre, dh, 512, f"{tag}_dup")
    return dh_in, dg_pre, dg_post, dw1, dw2, rode


def _hgrn_layer_fwd(h, g_pre, g_post, lb, onorm, w_in, w_o, tag, riders):
    rode = {}
    (a, proj), brought = _norm_mm(h, g_pre, w_in, _row_tile(h.shape[0], 2048), 1024, f"{tag}_in", rider=riders.get("in"))
    _kept(rode, "in", riders, brought)
    y, o, states, brought = _hgrn_fwd(proj, lb, onorm, f"{tag}_scan", riders.get("scan"))
    _kept(rode, "scan", riders, brought)
    m, out, _ = _mm_norm_res(y, w_o, g_post, h, 512, f"{tag}_o")
    return out, (h, a, proj, y, o, states, m), rode


def _hgrn_layer_bwd(dh, saved, g_pre, g_post, lb, onorm, w_in, w_o, tag, rider=None):
    h, a, proj, y, o, states, m = saved
    (dm, dg_post, dy), _ = _rmsbwd_mm(m, g_post, dh, w_o, 512, 1024, f"{tag}_do")
    dw_o = _mm(y, dm, "tn", 128, 1024, f"{tag}_dwo", out_dtypes=(BF16,), shard="rows")
    dproj, dlb, donorm, rode = _hgrn_bwd(proj, lb, onorm, o, states, dy, f"{tag}_dscan", rider)
    dw_in = _mm(a, dproj, "tn", 1024, 512, f"{tag}_dwin", out_dtypes=(BF16,), shard="cols")
    dh_in, dg_pre = _mm_rmsbwd_res(dproj, w_in, h, g_pre, dh, 512, f"{tag}_din")
    return dh_in, dg_pre, dg_post, dlb, donorm, dw_in, dw_o, rode


def _mla_layer_fwd(h, g_pre, g_post, cos, sin, w_in, late, qn, kvn, tag, riders):
    rode = {}
    (a, proj), brought = _norm_mm(h, g_pre, w_in, 512, MLA_IN, f"{tag}_in", rider=riders.get("in"))
    _kept(rode, "in", riders, brought)
    w_uq, w_ukv, _ = late(rode)
    cqn, ckvn, q, k, v = _mla_qkv(proj, qn, kvn, w_uq, w_ukv, cos, sin, f"{tag}_qkv")
    o, lse, brought = _attn_fwd(q, k, v, f"{tag}_attn", riders.get("attn"))
    _kept(rode, "attn", riders, brought)
    _, _, w_o = late(rode)
    m, out, _ = _mm_norm_res(o, w_o, g_post, h, 512, f"{tag}_o")
    return out, (h, a, proj, cqn, ckvn, q, k, v, o, lse, m), rode


def _mla_layer_bwd(dh, saved, g_pre, g_post, cos, sin, w_in, qn, kvn, w_uq, w_ukv, w_o, tag, rider=None, own_w_o_rides=False):
    h, a, proj, cqn, ckvn, q, k, v, o, lse, m = saved
    hh, s = q.shape[0], q.shape[1]
    (dm, dg_post, do), _ = _rmsbwd_mm(m, g_post, dh, w_o, 512, 1024, f"{tag}_do")
    dw_o = _mm(o, dm, "tn", 128, 1024, f"{tag}_dwo", out_dtypes=(BF16,), shard="rows")
    if own_w_o_rides:
        rider = _ExchangeRide(list(rider.operands) + [dw_o])
    delta = _attn_delta(do, o, f"{tag}_delta")
    dq, dk, dv, rode = _attn_bwd(q, k, v, do, lse.reshape(hh, 1, s), delta.reshape(hh, 1, s), f"{tag}_dattn", rider)
    dqe, dkve, dproj, dqn, dkvn = _mla_bwd_mid(dq, dk, dv, cos, sin, proj, qn, kvn, w_uq, w_ukv, f"{tag}_dqkv")
    dw_uq = _mm(cqn, dqe, "tn", MLA_Q_LORA, 768, f"{tag}_dwuq", out_dtypes=(BF16,))
    dw_ukv = _mm(ckvn, dkve, "tn", MLA_KV_LORA, 256, f"{tag}_dwukv", out_dtypes=(BF16,), shard="cols")
    dw_in = _mm(a, dproj, "tn", 128, MLA_IN, f"{tag}_dwin", out_dtypes=(BF16,), shard="rows")
    dh_in, dg_pre = _mm_rmsbwd_res(dproj, w_in, h, g_pre, dh, 512, f"{tag}_din")
    dw_uq = dw_uq.reshape(MLA_Q_LORA, N_DEV, -1).transpose(1, 0, 2)
    return dh_in, dg_pre, dg_post, dqn, dkvn, dw_in, dw_uq, dw_ukv, dw_o, rode


_CUT = dict(mla_w_in="rows", mla_w_uq="cols", mla_w_ukv="cols", mla_w_o="rows", hgrn_w_in="cols", hgrn_w_o="rows",
            mlp_w1="cols", mlp_w2="rows")


def _unit(layer, kind):
    slot = layer // 2
    if kind == "mla":
        return [("mla_w_in", slot), ("mla_w_uq", slot), ("mla_w_ukv", slot), ("mla_w_o", slot)]
    if kind == "hgrn":
        return [("hgrn_w_in", slot), ("hgrn_w_o", slot)]
    return [("mlp_w1", layer), ("mlp_w2", layer)]


_GATHER_FIRST = [("mla_w_in", 0)]
_GATHER_PLAN = {
    (0, "in"): [("mla_w_uq", 0), ("mla_w_ukv", 0)],
    (0, "attn"): [("mla_w_o", 0)] + _unit(0, "mlp") + _unit(1, "hgrn"),
    (0, "up"): [("mlp_w1", 1)],
    (0, "down"): [("mlp_w2", 1)],
    (1, "in"): _unit(2, "mla"),
    (1, "scan"): _unit(2, "mlp"),
    (2, "attn"): _unit(3, "hgrn") + _unit(3, "mlp"),
}
_EXCHANGE_PLAN = {
    (3, "dscan"): _unit(3, "mlp"),
    (2, "ddown"): [("hgrn_w_in", 1)],
    (2, "dw2"): [("hgrn_w_o", 1)],
    (2, "dattn"): _unit(2, "mlp"),
    (1, "ddown"): _unit(2, "mla"),
    (1, "dscan"): _unit(1, "mlp"),
    (0, "ddown"): [("hgrn_w_in", 0)],
    (0, "dw2"): [("hgrn_w_o", 0)],
    (0, "dattn"): _unit(0, "mlp"),
}
_EXCHANGE_LAST = [("mla_w_in", 0), ("mla_w_uq", 0), ("mla_w_ukv", 0)]
_EXCHANGE_LAST_CARRIER = "mlp_w1"


def _gather_cut(name):
    return "slots" if name == "mla_w_uq" else _CUT[name]


def _gather_rider(weights, ents):
    return _GatherRide([weights[name][idx].astype(BF16) for name, idx in ents], [_gather_cut(name) for name, _ in ents])


def _gathered(outs, ents):
    res = {}
    for (name, idx), out in zip(ents, outs):
        if _gather_cut(name) == "slots":
            out = out.transpose(1, 0, 2).reshape(out.shape[1], -1)
        res[(name, idx)] = out
    return res


def _adamw_nd(w, g, m, v, name):
    shape = w.shape
    c = shape[-1]
    d, nm, nv = _adamw(w.reshape(-1, c), g.reshape(-1, c), m.reshape(-1, c), v.reshape(-1, c), name)
    return d.reshape(shape), nm.reshape(shape), nv.reshape(shape)


def kernel(x, positions, norm_gains, mla_w_in, mla_q_norm, mla_kv_norm, mla_w_uq, mla_w_ukv, mla_w_o, hgrn_w_in, hgrn_lb_logits, hgrn_o_norm, hgrn_w_o, mlp_w1, mlp_w2, loss_target, m_norm_gains, m_mla_w_in, m_mla_q_norm, m_mla_kv_norm, m_mla_w_uq, m_mla_w_ukv, m_mla_w_o, m_hgrn_w_in, m_hgrn_lb_logits, m_hgrn_o_norm, m_hgrn_w_o, m_mlp_w1, m_mlp_w2, v_norm_gains, v_mla_w_in, v_mla_q_norm, v_mla_kv_norm, v_mla_w_uq, v_mla_w_ukv, v_mla_w_o, v_hgrn_w_in, v_hgrn_lb_logits, v_hgrn_o_norm, v_hgrn_w_o, v_mlp_w1, v_mlp_w2):
    weights = dict(norm_gains=norm_gains, mla_w_in=mla_w_in, mla_q_norm=mla_q_norm, mla_kv_norm=mla_kv_norm,
                   mla_w_uq=mla_w_uq, mla_w_ukv=mla_w_ukv, mla_w_o=mla_w_o, hgrn_w_in=hgrn_w_in,
                   hgrn_lb_logits=hgrn_lb_logits, hgrn_o_norm=hgrn_o_norm, hgrn_w_o=hgrn_w_o, mlp_w1=mlp_w1, mlp_w2=mlp_w2)
    mom_m = dict(norm_gains=m_norm_gains, mla_w_in=m_mla_w_in, mla_q_norm=m_mla_q_norm, mla_kv_norm=m_mla_kv_norm,
                 mla_w_uq=m_mla_w_uq, mla_w_ukv=m_mla_w_ukv, mla_w_o=m_mla_w_o, hgrn_w_in=m_hgrn_w_in,
                 hgrn_lb_logits=m_hgrn_lb_logits, hgrn_o_norm=m_hgrn_o_norm, hgrn_w_o=m_hgrn_w_o, mlp_w1=m_mlp_w1, mlp_w2=m_mlp_w2)
    mom_v = dict(norm_gains=v_norm_gains, mla_w_in=v_mla_w_in, mla_q_norm=v_mla_q_norm, mla_kv_norm=v_mla_kv_norm,
                 mla_w_uq=v_mla_w_uq, mla_w_ukv=v_mla_w_ukv, mla_w_o=v_mla_w_o, hgrn_w_in=v_hgrn_w_in,
                 hgrn_lb_logits=v_hgrn_lb_logits, hgrn_o_norm=v_hgrn_o_norm, hgrn_w_o=v_hgrn_w_o, mlp_w1=v_mlp_w1, mlp_w2=v_mlp_w2)
    order = list(weights)
    seq = x.shape[1]
    h = x.reshape(seq, D_MODEL)
    target = loss_target.reshape(seq, D_MODEL)

    full = _gathered(_run_alone(_gather_rider(weights, _GATHER_FIRST), "gather_first"), _GATHER_FIRST)
    gains = _all_gather(norm_gains.reshape(DEPTH * 4, D_MODEL // N_DEV), "gather_gains")
    gains = gains.transpose(1, 0, 2).reshape(DEPTH, 4, 1, D_MODEL)

    def gather_riders(layer, keys):
        return {k: _gather_rider(weights, _GATHER_PLAN[(layer, k)]) for k in keys if (layer, k) in _GATHER_PLAN}

    seen = set()

    def arrived(layer, rode):
        for k, outs in rode.items():
            if (layer, k) not in seen:
                seen.add((layer, k))
                full.update(_gathered(outs, _GATHER_PLAN[(layer, k)]))

    def late_weights(layer):
        def late(rode):
            arrived(layer, rode)
            return tuple(full.get((name, layer // 2)) for name in ("mla_w_uq", "mla_w_ukv", "mla_w_o"))
        return late

    cos, sin = _rope_tables(positions.reshape(seq, 1), "rope_tables")
    lower = _lb_fwd(hgrn_lb_logits, "lower_bounds")

    def mixer_args(layer):
        slot = layer // 2
        if layer % 2 == 0:
            return (cos, sin, full[("mla_w_in", slot)], mla_q_norm[slot:slot + 1], mla_kv_norm[slot:slot + 1],
                    full[("mla_w_uq", slot)], full[("mla_w_ukv", slot)], full[("mla_w_o", slot)])
        return (lower[layer:layer + 1], hgrn_o_norm[slot:slot + 1], full[("hgrn_w_in", slot)], full[("hgrn_w_o", slot)])

    saved = []
    for layer in range(DEPTH):
        g = gains[layer]
        if layer % 2 == 0:
            slot = layer // 2
            h, sv_mix, rode = _mla_layer_fwd(
                h, g[0], g[1], cos, sin, full[("mla_w_in", slot)], late_weights(layer), mla_q_norm[slot:slot + 1],
                mla_kv_norm[slot:slot + 1], f"l{layer}_mla", gather_riders(layer, ["in", "attn"]))
            arrived(layer, rode)
        else:
            h, sv_mix, rode = _hgrn_layer_fwd(h, g[0], g[1], *mixer_args(layer), f"l{layer}_hgrn",
                                              gather_riders(layer, ["in", "scan"]))
            arrived(layer, rode)
        h, sv_mlp, rode = _mlp_fwd(h, g[2], g[3], full[("mlp_w1", layer)], full[("mlp_w2", layer)], f"l{layer}_mlp",
                                   gather_riders(layer, ["up", "down"]))
        arrived(layer, rode)
        saved.append((sv_mix, sv_mlp))

    loss_part, dh = _loss(h, target, "loss")
    loss = lax.psum(loss_part[0, 0], AXES)

    zero_row = jnp.zeros((1, D_MODEL), F32)
    dgains = [[None] * 4 for _ in range(DEPTH)]
    dlower = [zero_row] * DEPTH
    partials, lands = {}, {}
    dqn, dkvn, donorm = [None] * 2, [None] * 2, [None] * 2

    def exchange_riders(layer, keys):
        return {k: _ExchangeRide([partials[e] for e in _EXCHANGE_PLAN[(layer, k)]]) for k in keys
                if (layer, k) in _EXCHANGE_PLAN}

    def landed(layer, rode):
        for k, outs in rode.items():
            lands.update(zip(_EXCHANGE_PLAN[(layer, k)], outs))

    for layer in range(DEPTH - 1, -1, -1):
        slot = layer // 2
        g = gains[layer]
        sv_mix, sv_mlp = saved[layer]
        dh, dgains[layer][2], dgains[layer][3], partials[("mlp_w1", layer)], partials[("mlp_w2", layer)], rode = _mlp_bwd(
            dh, sv_mlp, g[2], g[3], full[("mlp_w1", layer)], full[("mlp_w2", layer)], f"l{layer}_mlp",
            exchange_riders(layer, ["ddown", "dw2"]))
        landed(layer, rode)
        key = "dattn" if layer % 2 == 0 else "dscan"
        rider = exchange_riders(layer, [key]).get(key)
        if layer % 2 == 0:
            (dh, dgains[layer][0], dgains[layer][1], dqn[slot], dkvn[slot], partials[("mla_w_in", slot)],
             partials[("mla_w_uq", slot)], partials[("mla_w_ukv", slot)], partials[("mla_w_o", slot)], brought) = _mla_layer_bwd(
                dh, sv_mix, g[0], g[1], *mixer_args(layer), f"l{layer}_mla", rider, own_w_o_rides=(layer == 0))
            if layer == 0:
                lands[("mla_w_o", 0)] = brought[-1]
        else:
            (dh, dgains[layer][0], dgains[layer][1], dlower[layer], donorm[slot], partials[("hgrn_w_in", slot)],
             partials[("hgrn_w_o", slot)], brought) = _hgrn_layer_bwd(dh, sv_mix, g[0], g[1], *mixer_args(layer), f"l{layer}_hgrn", rider)
        landed(layer, {key: brought} if rider is not None else {})
    grad_x = dh.reshape(x.shape)
    dlogits = _lb_bwd(hgrn_lb_logits, jnp.concatenate(dlower, axis=0), "lower_bounds_bwd")

    pad = jnp.zeros((1, D_MODEL - 2 * MLA_KV_LORA), F32)
    pad2 = jnp.zeros((1, D_MODEL - 2 * HGRN_D), F32)
    small = jnp.concatenate(
        [jnp.concatenate([gg for row in dgains for gg in row], axis=0), jnp.concatenate(dqn, axis=1),
         jnp.concatenate(dkvn + [pad], axis=1), dlogits, jnp.concatenate(donorm + [pad2], axis=1), zero_row], axis=0)
    small = _sum_slots(_all_gather(small, "gather_small_grads"), "sum_small_grads")
    me = 4 * lax.axis_index("x") + 2 * lax.axis_index("y") + lax.axis_index("c")
    n_g = DEPTH * 4
    width = D_MODEL // N_DEV
    grads = {}
    grads["norm_gains"] = lax.dynamic_slice(small[:n_g], (0, me * width), (n_g, width)).reshape(DEPTH, 4, width)
    grads["mla_q_norm"] = small[n_g].reshape(2, MLA_Q_LORA)
    grads["mla_kv_norm"] = small[n_g + 1, :2 * MLA_KV_LORA].reshape(2, MLA_KV_LORA)
    grads["hgrn_lb_logits"] = small[n_g + 2:n_g + 2 + DEPTH]
    grads["hgrn_o_norm"] = small[n_g + 2 + DEPTH, :2 * HGRN_D].reshape(2, HGRN_D)

    deltas, new_m, new_v = {}, {}, {}
    for name in [_EXCHANGE_LAST_CARRIER] + [n for n in order if n != _EXCHANGE_LAST_CARRIER]:
        if name in _CUT:
            per_layer = [lands[(name, idx)] for idx in range(weights[name].shape[0])]
            last = _ExchangeRide([partials[e] for e in _EXCHANGE_LAST]) if name == _EXCHANGE_LAST_CARRIER else None
            grads[name], deltas[name], new_m[name], new_v[name], brought = _adamw_layers(
                weights[name], per_layer, mom_m[name], mom_v[name], f"adamw_{name}", last)
            if last is not None:
                lands.update(zip(_EXCHANGE_LAST, brought))
        else:
            deltas[name], new_m[name], new_v[name] = _adamw_nd(weights[name], grads[name], mom_m[name], mom_v[name], f"adamw_{name}")
    return (loss, grad_x, *[grads[n] for n in order], *[deltas[n] for n in order], *[new_m[n] for n in order],
            *[new_v[n] for n in order])
```

```python
import numpy as np
import jax
import jax.numpy as jnp
from jax import lax
from jax.experimental import pallas as pl
from jax.experimental.pallas import tpu as pltpu

F32, BF16 = jnp.float32, jnp.bfloat16

N_DEV = 8
AXES = ("x", "y", "c")
D_MODEL = 1024
DEPTH = 4
MLA_HEADS = 8
MLA_Q_LORA = 512
MLA_KV_LORA = 256
MLA_NOPE = 128
MLA_ROPE = 64
MLA_V = 128
MLA_QK = MLA_NOPE + MLA_ROPE
MLA_IN = MLA_Q_LORA + MLA_KV_LORA + MLA_ROPE
ROPE_BASE = 10000.0
HGRN_HEADS = 8
HGRN_D = 128
HGRN_CHUNK = 32
D_FF = 4 * D_MODEL
EPS = 1e-6
LOG2_E = 1.4426950408889634
ADAM_LR, ADAM_B1, ADAM_B2, ADAM_EPS, ADAM_WD, ADAM_STEP = 0.001, 0.9, 0.999, 1e-08, 0.01, 10

V7X_VMEM_LIMIT_BYTES = 56 * 1024 * 1024

NN = (((1,), (0,)), ((), ()))
NT = (((1,), (1,)), ((), ()))
TN = (((0,), (0,)), ((), ()))
_DIMS = {"nn": NN, "nt": NT, "tn": TN}


def _params(*sem):
    return pltpu.CompilerParams(dimension_semantics=sem, vmem_limit_bytes=V7X_VMEM_LIMIT_BYTES)


def _dot(a, b, dims=NN):
    return lax.dot_general(a, b, dims, preferred_element_type=F32)


def _dot_select(sel, x, pieces, dims=NN):
    sel = sel.astype(BF16)
    acc, rest = None, x
    for _ in range(pieces):
        term = rest.astype(BF16)
        part = _dot(sel, term, dims)
        acc = part if acc is None else acc + part
        rest = rest - term.astype(F32)
    return acc


def _rstd(x):
    return lax.rsqrt(jnp.mean(x * x, axis=-1, keepdims=True) + EPS)


def _rms_bwd_rows(x, g, dy):
    r = _rstd(x)
    xh = x * r
    dyg = dy * g
    dx = r * (dyg - xh * jnp.mean(dyg * xh, axis=-1, keepdims=True))
    return dx, dy * xh


def _row_tile(n, want):
    t = min(n, want)
    assert n % t == 0, (n, t)
    return t


def _divisor_tile(n, cap, mult):
    for t in range(min(cap, n) - min(cap, n) % mult, 0, -mult):
        if n % t == 0:
            return t
    return n


def _rms_fwd(x, g, res, out_dtype, name):
    s, d = x.shape
    ts = _row_tile(s, 512)

    def body(x_ref, g_ref, *rest):
        xf = x_ref[...]
        y = xf * _rstd(xf) * g_ref[...]
        if res is not None:
            y = rest[0][...] + y
        rest[-1][...] = y.astype(out_dtype)

    row = pl.BlockSpec((ts, d), lambda i: (i, 0))
    vec = pl.BlockSpec((1, d), lambda i: (0, 0))
    ins = [x, g] + ([res] if res is not None else [])
    return pl.pallas_call(
        body, grid=(s // ts,), in_specs=[row, vec] + ([row] if res is not None else []), out_specs=row,
        out_shape=jax.ShapeDtypeStruct((s, d), out_dtype), compiler_params=_params("parallel"), name=name)(*ins)


def _rms_bwd(x, g, dy, res, out_dtype, name):
    s, d = x.shape
    ts = _row_tile(s, 512)

    def body(x_ref, g_ref, dy_ref, *rest):
        dx_ref, dg_ref = rest[-2:]
        dx, dg = _rms_bwd_rows(x_ref[...], g_ref[...], dy_ref[...].astype(F32))
        if res is not None:
            dx = rest[0][...] + dx
        dx_ref[...] = dx.astype(out_dtype)

        @pl.when(pl.program_id(0) == 0)
        def _():
            dg_ref[...] = jnp.zeros_like(dg_ref)

        dg_ref[...] += jnp.sum(dg, axis=0, keepdims=True)

    row = pl.BlockSpec((ts, d), lambda i: (i, 0))
    vec = pl.BlockSpec((1, d), lambda i: (0, 0))
    ins = [x, g, dy] + ([res] if res is not None else [])
    return pl.pallas_call(
        body, grid=(s // ts,), in_specs=[row, vec, row] + ([row] if res is not None else []), out_specs=(row, vec),
        out_shape=(jax.ShapeDtypeStruct((s, d), out_dtype), jax.ShapeDtypeStruct((1, d), F32)),
        compiler_params=_params("arbitrary"), name=name)(*ins)


def _mm(a, b, mode, tm, tn, name, out_dtypes=(F32,), shard=None, epi=None, extras=(), rider=None):
    if mode == "tn":
        k, m = a.shape
        a_spec = pl.BlockSpec((k, tm), lambda i, j: (0, i))
    else:
        m, k = a.shape
        a_spec = pl.BlockSpec((tm, k), lambda i, j: (i, 0))
    if mode == "nt":
        n = b.shape[0]
        b_spec = pl.BlockSpec((tn, k), lambda i, j: (j, 0))
    else:
        n = b.shape[1]
        b_spec = pl.BlockSpec((k, tn), lambda i, j: (0, j))
    assert m % tm == 0 and n % tn == 0, (name, m, tm, n, tn)
    tile = pl.BlockSpec((tm, tn), lambda i, j: (i, j))
    if shard == "rows":
        per = m // N_DEV // tm
        out_specs = [pl.BlockSpec((None, tm, tn), lambda i, j: (i // per, i % per, j))]
        out_shape = [jax.ShapeDtypeStruct((N_DEV, m // N_DEV, n), out_dtypes[0])]
    elif shard == "cols":
        per = n // N_DEV // tn
        out_specs = [pl.BlockSpec((None, tm, tn), lambda i, j: (j // per, i, j % per))]
        out_shape = [jax.ShapeDtypeStruct((N_DEV, m, n // N_DEV), out_dtypes[0])]
    else:
        out_specs = [tile for _ in out_dtypes]
        out_shape = [jax.ShapeDtypeStruct((m, n), dt) for dt in out_dtypes]
    n_ex, n_out = len(extras), len(out_shape)
    r_in, r_in_specs, r_out, r_out_specs, r_scr = _rider_specs(rider)
    n_rin, n_rout = len(r_in), len(r_out)
    grid = (m // tm, n // tn)

    def body(a_ref, b_ref, *refs):
        ex_refs = refs[:n_ex]
        o_refs = refs[n_ex + n_rin:n_ex + n_rin + n_out]
        r_refs = refs[n_ex:n_ex + n_rin] + refs[n_ex + n_rin + n_out:]
        _ride(rider, pl.program_id(0) * grid[1] + pl.program_id(1), grid[0] * grid[1], r_refs)
        acc = _dot(a_ref[...].astype(BF16), b_ref[...].astype(BF16), _DIMS[mode])
        vals = (acc,) if epi is None else epi(acc, *[r[...] for r in ex_refs])
        for o_ref, val in zip(o_refs, vals):
            o_ref[...] = val.astype(o_ref.dtype)

    sem = ("parallel", "parallel") if rider is None else ("arbitrary", "arbitrary")
    out = pl.pallas_call(
        body, grid=grid, in_specs=[a_spec, b_spec] + [tile] * n_ex + r_in_specs, out_specs=out_specs + r_out_specs,
        out_shape=out_shape + r_out, scratch_shapes=r_scr, compiler_params=_params(*sem), name=name)(a, b, *extras, *r_in)
    res = out[0] if n_out == 1 else out[:n_out]
    return res if rider is None else (res, out[n_out:])


def _norm_mm(x, g, b, tm, tn, name, out_dtypes=(F32,), epi=None, rider=None):
    m, k = x.shape
    n = b.shape[1]
    assert m % tm == 0 and n % tn == 0, (name, m, tm, n, tn)
    grid = (m // tm, n // tn)
    n_out = len(out_dtypes)
    r_in, r_in_specs, r_out, r_out_specs, r_scr = _rider_specs(rider)
    n_rin = len(r_in)

    def body(x_ref, g_ref, b_ref, *refs):
        a_ref = refs[n_rin]
        o_refs = refs[n_rin + 1:n_rin + 1 + n_out]
        _ride(rider, pl.program_id(0) * grid[1] + pl.program_id(1), grid[0] * grid[1], refs[:n_rin] + refs[n_rin + 1 + n_out:])

        @pl.when(pl.program_id(1) == 0)
        def _():
            xf = x_ref[...]
            a_ref[...] = (xf * _rstd(xf) * g_ref[...]).astype(BF16)

        acc = _dot(a_ref[...], b_ref[...].astype(BF16))
        vals = (acc,) if epi is None else epi(acc)
        for o_ref, val in zip(o_refs, vals):
            o_ref[...] = val.astype(o_ref.dtype)

    row = pl.BlockSpec((tm, k), lambda i, j: (i, 0))
    tile = pl.BlockSpec((tm, tn), lambda i, j: (i, j))
    out = pl.pallas_call(
        body, grid=grid,
        in_specs=[row, pl.BlockSpec((1, k), lambda i, j: (0, 0)), pl.BlockSpec((k, tn), lambda i, j: (0, j))] + r_in_specs,
        out_specs=[row] + [tile] * n_out + r_out_specs,
        out_shape=[jax.ShapeDtypeStruct((m, k), BF16)] + [jax.ShapeDtypeStruct((m, n), dt) for dt in out_dtypes] + r_out,
        scratch_shapes=r_scr, compiler_params=_params("arbitrary", "arbitrary"), name=name)(x, g, b, *r_in)
    return out[:1 + n_out], out[1 + n_out:]


def _mm_norm_res(a, b, g, res, tm, name, rider=None):
    m, k = a.shape
    n = b.shape[1]
    assert m % tm == 0, (name, m, tm)
    r_in, r_in_specs, r_out, r_out_specs, r_scr = _rider_specs(rider)
    n_rin = len(r_in)

    def body(a_ref, b_ref, g_ref, res_ref, *refs):
        z_ref, o_ref = refs[n_rin:n_rin + 2]
        _ride(rider, pl.program_id(0), m // tm, refs[:n_rin] + refs[n_rin + 2:])
        z = _dot(a_ref[...].astype(BF16), b_ref[...].astype(BF16))
        z_ref[...] = z
        o_ref[...] = res_ref[...] + z * _rstd(z) * g_ref[...]

    row = pl.BlockSpec((tm, n), lambda i: (i, 0))
    out = pl.pallas_call(
        body, grid=(m // tm,),
        in_specs=[pl.BlockSpec((tm, k), lambda i: (i, 0)), pl.BlockSpec((k, n), lambda i: (0, 0)),
                  pl.BlockSpec((1, n), lambda i: (0, 0)), row] + r_in_specs,
        out_specs=[row, row] + r_out_specs,
        out_shape=[jax.ShapeDtypeStruct((m, n), F32), jax.ShapeDtypeStruct((m, n), F32)] + r_out,
        scratch_shapes=r_scr, compiler_params=_params("arbitrary"), name=name)(a, b, g, res, *r_in)
    return out[0], out[1], out[2:]


def _rmsbwd_mm(x, g, dy, b, tm, tn, name, out_dtypes=(F32,), epi=None, extras=(), rider=None, group_sums=None):
    m, k = x.shape
    n = b.shape[0]
    assert m % tm == 0 and n % tn == 0, (name, m, tm, n, tn)
    grid = (m // tm, n // tn)
    n_ex, n_out = len(extras), len(out_dtypes)
    r_in, r_in_specs, r_out, r_out_specs, r_scr = _rider_specs(rider)
    n_rin = len(r_in)
    n_gs = 0 if group_sums is None else 1
    if n_gs:
        assert tn == n and epi is None
        extras = tuple(extras) + (group_sums[0],)
        n_ex += 1

    def body(x_ref, g_ref, dy_ref, b_ref, *refs):
        ex_refs = refs[:n_ex]
        dx_ref, dg_ref = refs[n_ex + n_rin:n_ex + n_rin + 2]
        o_refs = refs[n_ex + n_rin + 2:n_ex + n_rin + 2 + n_out]
        i, j = pl.program_id(0), pl.program_id(1)
        _ride(rider, i * grid[1] + j, grid[0] * grid[1], refs[n_ex:n_ex + n_rin] + refs[n_ex + n_rin + 2 + n_out + n_gs:])

        @pl.when((i == 0) & (j == 0))
        def _():
            dg_ref[...] = jnp.zeros_like(dg_ref)

        @pl.when(j == 0)
        def _():
            dx, dg = _rms_bwd_rows(x_ref[...], g_ref[...], dy_ref[...])
            dx_ref[...] = dx.astype(BF16)
            dg_ref[...] += jnp.sum(dg, axis=0, keepdims=True)

        acc = _dot(dx_ref[...], b_ref[...].astype(BF16), NT)
        if n_gs:
            groups = group_sums[1]
            col = lax.broadcasted_iota(jnp.int32, (groups, n), 1) // (n // groups)
            sel = jnp.where(col == lax.broadcasted_iota(jnp.int32, (groups, n), 0), 1.0, 0.0)
            refs[n_ex + n_rin + 2 + n_out][...] = _dot_select(sel, acc * ex_refs[-1][...], 3, NT)
            vals = (acc,)
        else:
            vals = (acc,) if epi is None else epi(acc, *[r[...] for r in ex_refs])
        for o_ref, val in zip(o_refs, vals):
            o_ref[...] = val.astype(o_ref.dtype)

    row = pl.BlockSpec((tm, k), lambda i, j: (i, 0))
    vec = pl.BlockSpec((1, k), lambda i, j: (0, 0))
    tile = pl.BlockSpec((tm, tn), lambda i, j: (i, j))
    gs_specs = [pl.BlockSpec((group_sums[1], tm), lambda i, j: (0, i))] if n_gs else []
    gs_shape = [jax.ShapeDtypeStruct((group_sums[1], m), F32)] if n_gs else []
    out = pl.pallas_call(
        body, grid=grid,
        in_specs=[row, vec, row, pl.BlockSpec((tn, k), lambda i, j: (j, 0))] + [tile] * n_ex + r_in_specs,
        out_specs=[row, vec] + [tile] * n_out + gs_specs + r_out_specs,
        out_shape=[jax.ShapeDtypeStruct((m, k), BF16), jax.ShapeDtypeStruct((1, k), F32)]
        + [jax.ShapeDtypeStruct((m, n), dt) for dt in out_dtypes] + gs_shape + r_out,
        scratch_shapes=r_scr, compiler_params=_params("arbitrary", "arbitrary"), name=name)(x, g, dy, b, *extras, *r_in)
    return out[:2 + n_out + n_gs], out[2 + n_out + n_gs:]


def _mm_rmsbwd_res(a, b, x, g, res, tm, name, rider=None):
    m, k = a.shape
    n = b.shape[0]
    assert m % tm == 0, (name, m, tm)
    r_in, r_in_specs, r_out, r_out_specs, r_scr = _rider_specs(rider)
    n_rin = len(r_in)

    def body(a_ref, b_ref, x_ref, g_ref, res_ref, *refs):
        o_ref, dg_ref = refs[n_rin:n_rin + 2]
        _ride(rider, pl.program_id(0), m // tm, refs[:n_rin] + refs[n_rin + 2:])

        @pl.when(pl.program_id(0) == 0)
        def _():
            dg_ref[...] = jnp.zeros_like(dg_ref)

        da = _dot(a_ref[...].astype(BF16), b_ref[...].astype(BF16), NT)
        dx, dg = _rms_bwd_rows(x_ref[...], g_ref[...], da)
        o_ref[...] = res_ref[...] + dx
        dg_ref[...] += jnp.sum(dg, axis=0, keepdims=True)

    row = pl.BlockSpec((tm, n), lambda i: (i, 0))
    vec = pl.BlockSpec((1, n), lambda i: (0, 0))
    out = pl.pallas_call(
        body, grid=(m // tm,),
        in_specs=[pl.BlockSpec((tm, k), lambda i: (i, 0)), pl.BlockSpec((n, k), lambda i: (0, 0)), row, vec, row] + r_in_specs,
        out_specs=[row, vec] + r_out_specs,
        out_shape=[jax.ShapeDtypeStruct((m, n), F32), jax.ShapeDtypeStruct((1, n), F32)] + r_out,
        scratch_shapes=r_scr, compiler_params=_params("arbitrary"), name=name)(a, b, x, g, res, *r_in)
    return out[0], out[1], out[2:]


def _rope_tables(pos, name):
    s = pos.shape[0]
    half = MLA_ROPE // 2
    inv_freq = jnp.asarray(np.power(np.float32(ROPE_BASE), -np.arange(0, MLA_ROPE, 2, dtype=np.float32) / MLA_ROPE)
                           .astype(np.float32).reshape(1, half))

    def body(p_ref, f_ref, c_ref, s_ref):
        ang = p_ref[...].astype(F32) * f_ref[...]
        c_ref[...] = jnp.cos(ang)
        s_ref[...] = jnp.sin(ang)

    return pl.pallas_call(
        body, out_shape=(jax.ShapeDtypeStruct((s, half), F32), jax.ShapeDtypeStruct((s, half), F32)), name=name)(pos, inv_freq)


def _lb_softmax(logits):
    m = jnp.max(logits, axis=0, keepdims=True)
    e = jnp.exp(logits - m)
    return e / jnp.sum(e, axis=0, keepdims=True)


def _lb_fwd(logits, name):
    def body(l_ref, o_ref):
        p = _lb_softmax(l_ref[...])
        acc = jnp.zeros_like(p[0:1])
        o_ref[0:1, :] = acc
        for layer in range(1, DEPTH):
            acc = acc + p[layer:layer + 1]
            o_ref[layer:layer + 1, :] = acc

    return pl.pallas_call(body, out_shape=jax.ShapeDtypeStruct(logits.shape, F32), name=name)(logits)


def _lb_bwd(logits, dlb, name):
    def body(l_ref, d_ref, o_ref):
        p = _lb_softmax(l_ref[...])
        d = d_ref[...]
        dp = [jnp.zeros_like(d[0:1])] * DEPTH
        run = jnp.zeros_like(d[0:1])
        for layer in range(DEPTH - 1, 0, -1):
            run = run + d[layer:layer + 1]
            dp[layer] = run
        inner = sum(p[layer:layer + 1] * dp[layer] for layer in range(DEPTH))
        for layer in range(DEPTH):
            o_ref[layer:layer + 1, :] = p[layer:layer + 1] * (dp[layer] - inner)

    return pl.pallas_call(body, out_shape=jax.ShapeDtypeStruct(logits.shape, F32), name=name)(logits, dlb)


def _loss(y, target, name):
    s, d = y.shape
    ts = _row_tile(s, 512)

    def body(y_ref, t_ref, l_ref, dy_ref):
        e = y_ref[...] - t_ref[...]
        dy_ref[...] = e / d

        @pl.when(pl.program_id(0) == 0)
        def _():
            l_ref[...] = jnp.zeros_like(l_ref)

        l_ref[...] += 0.5 * jnp.sum(jnp.mean(e * e, axis=-1, keepdims=True), axis=0, keepdims=True)

    row = pl.BlockSpec((ts, d), lambda i: (i, 0))
    return pl.pallas_call(
        body, grid=(s // ts,), in_specs=[row, row], out_specs=(pl.BlockSpec((1, 1), lambda i: (0, 0)), row),
        out_shape=(jax.ShapeDtypeStruct((1, 1), F32), jax.ShapeDtypeStruct((s, d), F32)),
        compiler_params=_params("arbitrary"), name=name)(y, target)


def _rope(t1, t2, cos, sin):
    return t1 * cos - t2 * sin, t1 * sin + t2 * cos


def _rope_bwd(d1, d2, cos, sin):
    return d1 * cos + d2 * sin, d2 * cos - d1 * sin


def _mla_qkv(proj, qn, kvn, w_uq, w_ukv, cos, sin, name):
    s = proj.shape[0]
    ts = _row_tile(s, 256)
    hh, half = MLA_HEADS, MLA_ROPE // 2

    def body(p_ref, qn_ref, kvn_ref, wq_ref, wkv_ref, c_ref, s_ref, cq_ref, ckv_ref, q_ref, k_ref, v_ref):
        p = p_ref[...]
        cq, ckv, kr = p[:, :MLA_Q_LORA], p[:, MLA_Q_LORA:MLA_Q_LORA + MLA_KV_LORA], p[:, MLA_Q_LORA + MLA_KV_LORA:]
        cqn = (cq * _rstd(cq) * qn_ref[...]).astype(BF16)
        ckvn = (ckv * _rstd(ckv) * kvn_ref[...]).astype(BF16)
        cq_ref[...] = cqn
        ckv_ref[...] = ckvn
        qe = _dot(cqn, wq_ref[...])
        kve = _dot(ckvn, wkv_ref[...])
        cos_, sin_ = c_ref[...], s_ref[...]
        k1, k2 = _rope(kr[:, :half], kr[:, half:], cos_, sin_)
        k1, k2 = k1.astype(BF16), k2.astype(BF16)
        for h in range(hh):
            b = h * MLA_QK
            q_ref[h, :, 0:MLA_NOPE] = qe[:, b:b + MLA_NOPE].astype(BF16)
            q1, q2 = _rope(qe[:, b + MLA_NOPE:b + MLA_NOPE + half], qe[:, b + MLA_NOPE + half:b + MLA_QK], cos_, sin_)
            q_ref[h, :, MLA_NOPE:MLA_NOPE + half] = q1.astype(BF16)
            q_ref[h, :, MLA_NOPE + half:MLA_QK] = q2.astype(BF16)
            b = h * (MLA_NOPE + MLA_V)
            k_ref[h, :, 0:MLA_NOPE] = kve[:, b:b + MLA_NOPE].astype(BF16)
            k_ref[h, :, MLA_NOPE:MLA_NOPE + half] = k1
            k_ref[h, :, MLA_NOPE + half:MLA_QK] = k2
            v_ref[h] = kve[:, b + MLA_NOPE:b + MLA_NOPE + MLA_V].astype(BF16)

    def row(w):
        return pl.BlockSpec((ts, w), lambda i: (i, 0))

    def full(shape):
        return pl.BlockSpec(shape, lambda i: (0,) * len(shape))

    def heads(w):
        return pl.BlockSpec((hh, ts, w), lambda i: (0, i, 0))

    return pl.pallas_call(
        body, grid=(s // ts,),
        in_specs=[row(MLA_IN), full(qn.shape), full(kvn.shape), full(w_uq.shape), full(w_ukv.shape), row(half), row(half)],
        out_specs=(row(MLA_Q_LORA), row(MLA_KV_LORA), heads(MLA_QK), heads(MLA_QK), heads(MLA_V)),
        out_shape=(jax.ShapeDtypeStruct((s, MLA_Q_LORA), BF16), jax.ShapeDtypeStruct((s, MLA_KV_LORA), BF16),
                   jax.ShapeDtypeStruct((hh, s, MLA_QK), BF16), jax.ShapeDtypeStruct((hh, s, MLA_QK), BF16),
                   jax.ShapeDtypeStruct((hh, s, MLA_V), BF16)),
        compiler_params=_params("parallel"), name=name)(proj, qn, kvn, w_uq, w_ukv, cos, sin)


ATTN_BLOCK = 1024
ATTN_FWD_TILE = (256, 512)
ATTN_BWD_TILE = (512, 512)


def _attn_block(s):
    return _row_tile(s, ATTN_BLOCK)


def _tile_sees(diag, q0, tq, k0, tk):
    if not diag:
        return True, False
    return k0 <= q0 + tq - 1, k0 + tk - 1 > q0


def _causal_pairs(nb, kv_major):
    if kv_major:
        pairs = [(i, j) for j in range(nb) for i in range(j, nb)]
    else:
        pairs = [(i, j) for i in range(nb) for j in range(i + 1)]
    return (jnp.asarray(np.array([p[0] for p in pairs], np.int32)), jnp.asarray(np.array([p[1] for p in pairs], np.int32)))


def _ride(rider, step, total, refs):
    if rider is None:
        return

    @pl.when(step == 0)
    def _():
        rider.start(*refs)

    @pl.when(step == (total * 7) // 8)
    def _():
        rider.middle(*refs)

    @pl.when(step == total - 1)
    def _():
        rider.finish(*refs)


def _rider_specs(rider):
    if rider is None:
        return [], [], [], [], []
    return (list(rider.operands), [_HBM] * len(rider.operands), list(rider.out_shapes), [_HBM] * len(rider.out_shapes),
            list(rider.scratch))


def _attn_fwd(q, k, v, name, rider=None):
    hh, s, _ = q.shape
    blk = _attn_block(s)
    tq, tk = min(blk, ATTN_FWD_TILE[0]), min(blk, ATTN_FWD_TILE[1])
    nb = s // blk
    it, jt = _causal_pairs(nb, kv_major=False)
    npair = int(it.shape[0])
    scale = MLA_QK ** -0.5
    c2 = scale * LOG2_E
    r_in, r_in_specs, r_out, r_out_specs, r_scr = _rider_specs(rider)
    n_rin, n_rout, n_rscr = len(r_in), len(r_out), len(r_scr)

    def body(it_ref, jt_ref, q_ref, k_ref, v_ref, *refs):
        r_refs = refs[:n_rin] + refs[n_rin + 2:n_rin + 2 + n_rout] + refs[len(refs) - n_rscr:]
        o_ref, lse_ref = refs[n_rin:n_rin + 2]
        m_scr, acc_scr, v_scr = refs[n_rin + 2 + n_rout:n_rin + 2 + n_rout + 3]
        h, t = pl.program_id(0), pl.program_id(1)
        step = h * npair + t
        _ride(rider, step, hh * npair, r_refs)
        i, j = it_ref[t], jt_ref[t]

        @pl.when(j == 0)
        def _():
            m_scr[...] = jnp.full_like(m_scr, -jnp.inf)
            acc_scr[...] = jnp.zeros_like(acc_scr)
            v_scr[:, MLA_V:] = jnp.ones((blk, MLA_V), BF16)

        def block(diag):
            v_scr[:, :MLA_V] = v_ref[...]
            for k0 in range(0, blk, tk):
                kb, vb = k_ref[k0:k0 + tk, :], v_scr[k0:k0 + tk, :]
                for q0 in range(0, blk, tq):
                    visible, needs_mask = _tile_sees(diag, q0, tq, k0, tk)
                    if not visible:
                        continue
                    rows = slice(q0, q0 + tq)
                    sc = _dot(q_ref[rows, :], kb, NT)
                    if needs_mask:
                        qpos = q0 + lax.broadcasted_iota(jnp.int32, (tq, tk), 0)
                        kpos = k0 + lax.broadcasted_iota(jnp.int32, (tq, tk), 1)
                        sc = jnp.where(qpos >= kpos, sc, -jnp.inf)
                    m_prev = m_scr[rows, :]
                    m_new = jnp.maximum(m_prev, jnp.max(sc, axis=-1, keepdims=True))
                    alpha = jnp.exp2((m_prev - m_new) * c2)
                    p = jnp.exp2((sc - m_new) * c2)
                    acc_scr[rows, :] = alpha * acc_scr[rows, :] + _dot(p.astype(BF16), vb)
                    m_scr[rows, :] = m_new

        @pl.when(j < i)
        def _():
            block(False)

        @pl.when(j == i)
        def _():
            block(True)
            acc = acc_scr[...]
            l = acc[:, MLA_V:MLA_V + 1]
            o_ref[...] = acc[:, :MLA_V] / l
            lse_ref[...] = m_scr[...] * scale + jnp.log(l)

    grid_spec = pltpu.PrefetchScalarGridSpec(
        num_scalar_prefetch=2, grid=(hh, npair),
        in_specs=[pl.BlockSpec((None, blk, MLA_QK), lambda h, t, it_, jt_: (h, it_[t], 0)),
                  pl.BlockSpec((None, blk, MLA_QK), lambda h, t, it_, jt_: (h, jt_[t], 0)),
                  pl.BlockSpec((None, blk, MLA_V), lambda h, t, it_, jt_: (h, jt_[t], 0))] + r_in_specs,
        out_specs=[pl.BlockSpec((blk, MLA_V), lambda h, t, it_, jt_: (it_[t], h)),
                   pl.BlockSpec((None, blk, 1), lambda h, t, it_, jt_: (h, it_[t], 0))] + r_out_specs,
        scratch_shapes=[pltpu.VMEM((blk, 1), F32), pltpu.VMEM((blk, 2 * MLA_V), F32),
                        pltpu.VMEM((blk, 2 * MLA_V), BF16)] + r_scr)
    out = pl.pallas_call(
        body, grid_spec=grid_spec,
        out_shape=[jax.ShapeDtypeStruct((s, hh * MLA_V), F32), jax.ShapeDtypeStruct((hh, s, 1), F32)] + r_out,
        compiler_params=_params("arbitrary", "arbitrary"), name=name)(it, jt, q, k, v, *r_in)
    return out[0], out[1], out[2:]


def _attn_bwd(q, k, v, do, lse_row, delta_row, name, rider=None):
    hh, s, _ = q.shape
    blk = _attn_block(s)
    tq, tk = min(blk, ATTN_BWD_TILE[0]), min(blk, ATTN_BWD_TILE[1])
    nb = s // blk
    it, jt = _causal_pairs(nb, kv_major=True)
    npair = int(it.shape[0])
    scale = MLA_QK ** -0.5
    c2 = scale * LOG2_E
    r_in, r_in_specs, r_out, r_out_specs, r_scr = _rider_specs(rider)
    n_rin, n_rout, n_rscr = len(r_in), len(r_out), len(r_scr)

    def body(it_ref, jt_ref, q_ref, k_ref, v_ref, do_ref, lse_ref, dl_ref, *refs):
        r_refs = refs[:n_rin] + refs[n_rin + 3:n_rin + 3 + n_rout] + refs[len(refs) - n_rscr:]
        dq_ref, dk_ref, dv_ref = refs[n_rin:n_rin + 3]
        h, t = pl.program_id(0), pl.program_id(1)
        step = h * npair + t
        _ride(rider, step, hh * npair, r_refs)
        i, j = it_ref[t], jt_ref[t]

        @pl.when(t == 0)
        def _():
            dq_ref[...] = jnp.zeros_like(dq_ref)

        def block(diag):
            if diag:
                dk_ref[...] = jnp.zeros_like(dk_ref)
                dv_ref[...] = jnp.zeros_like(dv_ref)
            for q0 in range(0, blk, tq):
                qb = q_ref[q0:q0 + tq, :]
                dob = do_ref[q0:q0 + tq, :].astype(BF16)
                lse2 = lse_ref[:, q0:q0 + tq] * LOG2_E
                dl = dl_ref[pl.ds(h, 1), q0:q0 + tq]
                dq = None
                for k0 in range(0, blk, tk):
                    visible, needs_mask = _tile_sees(diag, q0, tq, k0, tk)
                    if not visible:
                        continue
                    kb, vb = k_ref[k0:k0 + tk, :], v_ref[k0:k0 + tk, :]
                    pt = jnp.exp2(_dot(kb, qb, NT) * c2 - lse2)
                    if needs_mask:
                        kpos = k0 + lax.broadcasted_iota(jnp.int32, (tk, tq), 0)
                        qpos = q0 + lax.broadcasted_iota(jnp.int32, (tk, tq), 1)
                        pt = jnp.where(qpos >= kpos, pt, 0.0)
                    dv_ref[k0:k0 + tk, :] += _dot(pt.astype(BF16), dob)
                    dpt = _dot(vb, dob, NT)
                    dst = (pt * (dpt - dl) * scale).astype(BF16)
                    dk_ref[k0:k0 + tk, :] += _dot(dst, qb)
                    part = _dot(dst, kb, TN)
                    dq = part if dq is None else dq + part
                rows = pl.ds(pl.multiple_of(i * blk + q0, tq), tq)
                dq_ref[rows, :] += dq

        @pl.when(i == j)
        def _():
            block(True)

        @pl.when(i > j)
        def _():
            block(False)

    grid_spec = pltpu.PrefetchScalarGridSpec(
        num_scalar_prefetch=2, grid=(hh, npair),
        in_specs=[pl.BlockSpec((None, blk, MLA_QK), lambda h, t, it_, jt_: (h, it_[t], 0)),
                  pl.BlockSpec((None, blk, MLA_QK), lambda h, t, it_, jt_: (h, jt_[t], 0)),
                  pl.BlockSpec((None, blk, MLA_V), lambda h, t, it_, jt_: (h, jt_[t], 0)),
                  pl.BlockSpec((blk, MLA_V), lambda h, t, it_, jt_: (it_[t], h)),
                  pl.BlockSpec((None, 1, blk), lambda h, t, it_, jt_: (h, 0, it_[t])),
                  pl.BlockSpec((hh, blk), lambda h, t, it_, jt_: (0, it_[t]))] + r_in_specs,
        out_specs=[pl.BlockSpec((None, s, MLA_QK), lambda h, t, it_, jt_: (h, 0, 0)),
                   pl.BlockSpec((None, blk, MLA_QK), lambda h, t, it_, jt_: (h, jt_[t], 0)),
                   pl.BlockSpec((None, blk, MLA_V), lambda h, t, it_, jt_: (h, jt_[t], 0))] + r_out_specs,
        scratch_shapes=r_scr)
    out = pl.pallas_call(
        body, grid_spec=grid_spec,
        out_shape=[jax.ShapeDtypeStruct((hh, s, MLA_QK), F32), jax.ShapeDtypeStruct((hh, s, MLA_QK), F32),
                   jax.ShapeDtypeStruct((hh, s, MLA_V), F32)] + r_out,
        compiler_params=_params("arbitrary", "arbitrary"), name=name)(it, jt, q, k, v, do, lse_row, delta_row, *r_in)
    return out[0], out[1], out[2], out[3:]


def _mla_bwd_mid(dq, dk, dv, cos, sin, proj, qn, kvn, w_uq, w_ukv, name):
    s = proj.shape[0]
    ts = _row_tile(s, 256)
    hh, half = MLA_HEADS, MLA_ROPE // 2
    nq, nkv = hh * MLA_QK, hh * (MLA_NOPE + MLA_V)

    def body(dq_ref, dk_ref, dv_ref, c_ref, s_ref, p_ref, qn_ref, kvn_ref, wq_ref, wkv_ref,
             dqe_ref, dkve_ref, dp_ref, dqn_ref, dkvn_ref):
        cos_, sin_ = c_ref[...], s_ref[...]
        dkr1 = jnp.zeros((ts, half), F32)
        dkr2 = jnp.zeros((ts, half), F32)
        for h in range(hh):
            dqh, dkh = dq_ref[h], dk_ref[h]
            b = h * MLA_QK
            dqe_ref[:, b:b + MLA_NOPE] = dqh[:, :MLA_NOPE].astype(BF16)
            d1, d2 = _rope_bwd(dqh[:, MLA_NOPE:MLA_NOPE + half], dqh[:, MLA_NOPE + half:], cos_, sin_)
            dqe_ref[:, b + MLA_NOPE:b + MLA_NOPE + half] = d1.astype(BF16)
            dqe_ref[:, b + MLA_NOPE + half:b + MLA_QK] = d2.astype(BF16)
            b = h * (MLA_NOPE + MLA_V)
            dkve_ref[:, b:b + MLA_NOPE] = dkh[:, :MLA_NOPE].astype(BF16)
            dkve_ref[:, b + MLA_NOPE:b + MLA_NOPE + MLA_V] = dv_ref[h].astype(BF16)
            dkr1 = dkr1 + dkh[:, MLA_NOPE:MLA_NOPE + half]
            dkr2 = dkr2 + dkh[:, MLA_NOPE + half:]
        dkr1, dkr2 = _rope_bwd(dkr1, dkr2, cos_, sin_)
        dcqn = _dot(dqe_ref[...], wq_ref[...], NT)
        dckvn = _dot(dkve_ref[...], wkv_ref[...], NT)
        p = p_ref[...]
        dcq, dqn = _rms_bwd_rows(p[:, :MLA_Q_LORA], qn_ref[...], dcqn)
        dckv, dkvn = _rms_bwd_rows(p[:, MLA_Q_LORA:MLA_Q_LORA + MLA_KV_LORA], kvn_ref[...], dckvn)
        dp_ref[:, :MLA_Q_LORA] = dcq.astype(BF16)
        dp_ref[:, MLA_Q_LORA:MLA_Q_LORA + MLA_KV_LORA] = dckv.astype(BF16)
        dp_ref[:, MLA_Q_LORA + MLA_KV_LORA:MLA_Q_LORA + MLA_KV_LORA + half] = dkr1.astype(BF16)
        dp_ref[:, MLA_Q_LORA + MLA_KV_LORA + half:] = dkr2.astype(BF16)

        @pl.when(pl.program_id(0) == 0)
        def _():
            dqn_ref[...] = jnp.zeros_like(dqn_ref)
            dkvn_ref[...] = jnp.zeros_like(dkvn_ref)

        dqn_ref[...] += jnp.sum(dqn, axis=0, keepdims=True)
        dkvn_ref[...] += jnp.sum(dkvn, axis=0, keepdims=True)

    def row(w):
        return pl.BlockSpec((ts, w), lambda i: (i, 0))

    def full(shape):
        return pl.BlockSpec(shape, lambda i: (0,) * len(shape))

    def heads(w):
        return pl.BlockSpec((hh, ts, w), lambda i: (0, i, 0))

    return pl.pallas_call(
        body, grid=(s // ts,),
        in_specs=[heads(MLA_QK), heads(MLA_QK), heads(MLA_V), row(half), row(half), row(MLA_IN),
                  full(qn.shape), full(kvn.shape), full(w_uq.shape), full(w_ukv.shape)],
        out_specs=(row(nq), row(nkv), row(MLA_IN), full(qn.shape), full(kvn.shape)),
        out_shape=(jax.ShapeDtypeStruct((s, nq), BF16), jax.ShapeDtypeStruct((s, nkv), BF16),
                   jax.ShapeDtypeStruct((s, MLA_IN), BF16), jax.ShapeDtypeStruct(qn.shape, F32),
                   jax.ShapeDtypeStruct(kvn.shape, F32)),
        compiler_params=_params("arbitrary"), name=name)(dq, dk, dv, cos, sin, proj, qn, kvn, w_uq, w_ukv)


HGRN_TILE = 128


def _chunk_masks(t):
    r = lax.broadcasted_iota(jnp.int32, (t, t), 0)
    c = lax.broadcasted_iota(jnp.int32, (t, t), 1)
    same = (r // HGRN_CHUNK) == (c // HGRN_CHUNK)
    return r, c, same


def _hgrn_gates(p, lb):
    hk = HGRN_HEADS * HGRN_D
    qx, fx, ix, gx = p[:, :hk], p[:, hk:2 * hk], p[:, 2 * hk:3 * hk], p[:, 3 * hk:]
    sig_f = jax.nn.sigmoid(fx)
    f = lb + (1.0 - lb) * sig_f
    sig_q = jax.nn.sigmoid(qx)
    t = p.shape[0]
    r, c, same = _chunk_masks(t)
    lower = jnp.where(same & (c <= r), 1.0, 0.0).astype(F32)
    b = _dot_select(lower, jnp.log(f), 3)
    b3 = b.reshape(t // HGRN_CHUNK, HGRN_CHUNK, hk)
    bref = jnp.broadcast_to(b3[:, HGRN_CHUNK // 2:HGRN_CHUNK // 2 + 1, :], b3.shape).reshape(t, hk)
    blast = jnp.broadcast_to(b3[:, HGRN_CHUNK - 1:, :], b3.shape).reshape(t, hk)
    return qx, ix, gx, sig_f, f, sig_q, b, bref, blast


def _hgrn_fwd(proj, lb, onorm, name, rider=None):
    s = proj.shape[0]
    t = _row_tile(s, HGRN_TILE)
    nc = t // HGRN_CHUNK
    hh, dd, hk = HGRN_HEADS, HGRN_D, HGRN_HEADS * HGRN_D
    r_in, r_in_specs, r_out, r_out_specs, r_scr = _rider_specs(rider)
    n_rin, n_rout = len(r_in), len(r_out)

    def body(p_ref, lb_ref, on_ref, *refs):
        y_ref, o_ref, st_ref = refs[n_rin:n_rin + 3]
        st_scr = refs[n_rin + 3 + n_rout]
        _ride(rider, pl.program_id(0), s // t, refs[:n_rin] + refs[n_rin + 3:n_rin + 3 + n_rout] + refs[n_rin + 4 + n_rout:])

        @pl.when(pl.program_id(0) == 0)
        def _():
            st_scr[...] = jnp.zeros_like(st_scr)

        qx, ix, gx, _, f, sig_q, b, bref, blast = _hgrn_gates(p_ref[...], lb_ref[...])
        q = qx * sig_q
        k = 1.0 - f
        r, c, same = _chunk_masks(t)
        causal = same & (c <= r)
        for h in range(hh):
            sl = slice(h * dd, (h + 1) * dd)
            bh, brefh, blasth, qh, kh = b[:, sl], bref[:, sl], blast[:, sl], q[:, sl], k[:, sl]
            vh = ix[:, sl].astype(BF16)
            q_rel = (qh * jnp.exp(bh - brefh)).astype(BF16)
            k_rel = (kh * jnp.exp(brefh - bh)).astype(BF16)
            a = jnp.where(causal, _dot(q_rel, k_rel, NT), 0.0)
            o_intra = _dot(a.astype(BF16), vh)
            q_dec = (qh * jnp.exp(bh)).astype(BF16)
            k_dec = (kh * jnp.exp(blasth - bh)).astype(BF16)
            dec = jnp.exp(blasth)
            pieces = []
            for ci in range(nc):
                rows = slice(ci * HGRN_CHUNK, (ci + 1) * HGRN_CHUNK)
                st = st_scr[h]
                if ci == 0:
                    st_ref[h] = st
                pieces.append(_dot(q_dec[rows], st.astype(BF16), NT))
                st_scr[h] = st * dec[ci * HGRN_CHUNK:ci * HGRN_CHUNK + 1, :] + _dot(vh[rows], k_dec[rows], TN)
            oh = o_intra + jnp.concatenate(pieces, axis=0)
            o_ref[:, sl] = oh
            gate = gx[:, sl] * jax.nn.sigmoid(gx[:, sl])
            y_ref[:, sl] = (oh * _rstd(oh) * on_ref[...] * gate).astype(BF16)

    out = pl.pallas_call(
        body, grid=(s // t,),
        in_specs=[pl.BlockSpec((t, 4 * hk), lambda i: (i, 0)), pl.BlockSpec((1, hk), lambda i: (0, 0)),
                  pl.BlockSpec((1, dd), lambda i: (0, 0))] + r_in_specs,
        out_specs=[pl.BlockSpec((t, hk), lambda i: (i, 0)), pl.BlockSpec((t, hk), lambda i: (i, 0)),
                   pl.BlockSpec((None, hh, dd, dd), lambda i: (i, 0, 0, 0))] + r_out_specs,
        out_shape=[jax.ShapeDtypeStruct((s, hk), BF16), jax.ShapeDtypeStruct((s, hk), F32),
                   jax.ShapeDtypeStruct((s // t, hh, dd, dd), F32)] + r_out,
        scratch_shapes=[pltpu.VMEM((hh, dd, dd), F32)] + r_scr,
        compiler_params=_params("arbitrary"), name=name)(proj, lb, onorm, *r_in)
    return out[0], out[1], out[2], out[3:]


def _hgrn_bwd(proj, lb, onorm, o, states, dy, name, rider=None):
    s = proj.shape[0]
    t = _row_tile(s, HGRN_TILE)
    nt = s // t
    nc = t // HGRN_CHUNK
    hh, dd, hk = HGRN_HEADS, HGRN_D, HGRN_HEADS * HGRN_D
    r_in, r_in_specs, r_out, r_out_specs, r_scr = _rider_specs(rider)
    n_rin, n_rout, n_rscr = len(r_in), len(r_out), len(r_scr)

    def body(p_ref, lb_ref, on_ref, o_ref, st_ref, dy_ref, *refs):
        r_refs = refs[:n_rin] + refs[n_rin + 3:n_rin + 3 + n_rout] + refs[len(refs) - n_rscr:]
        dp_ref, dlb_ref, don_ref = refs[n_rin:n_rin + 3]
        dst_scr, cat_scr, ext_scr, dk_scr, dq_scr = refs[n_rin + 3 + n_rout:n_rin + 3 + n_rout + 5]
        _ride(rider, pl.program_id(0), nt, r_refs)

        @pl.when(pl.program_id(0) == 0)
        def _():
            dst_scr[...] = jnp.zeros_like(dst_scr)
            dlb_ref[...] = jnp.zeros_like(dlb_ref)
            don_ref[...] = jnp.zeros_like(don_ref)

        lbv = lb_ref[...]
        qx, ix, gx, sig_f, f, sig_q, b, bref, blast = _hgrn_gates(p_ref[...], lbv)
        q = qx * sig_q
        k = 1.0 - f
        r, c, same = _chunk_masks(t)
        causal = same & (c <= r)
        on = on_ref[...]
        don = jnp.zeros((1, dd), F32)
        for h in range(hh):
            sl = slice(h * dd, (h + 1) * dd)
            oh = o_ref[:, sl]
            dyh = dy_ref[:, sl]
            gxh = gx[:, sl]
            sig_g = jax.nn.sigmoid(gxh)
            rs = _rstd(oh)
            dgate = dyh * (oh * rs * on)
            dp_ref[:, 3 * hk + h * dd:3 * hk + (h + 1) * dd] = (dgate * (sig_g * (1.0 + gxh * (1.0 - sig_g)))).astype(BF16)
            do, donh = _rms_bwd_rows(oh, on, dyh * (gxh * sig_g))
            don = don + jnp.sum(donh, axis=0, keepdims=True)
            dob = do.astype(BF16)
            bh, brefh, blasth, qh, kh = b[:, sl], bref[:, sl], blast[:, sl], q[:, sl], k[:, sl]
            vh = ix[:, sl].astype(BF16)
            e_qr, e_kr, e_qd, e_kd = jnp.exp(bh - brefh), jnp.exp(brefh - bh), jnp.exp(bh), jnp.exp(blasth - bh)
            dec = jnp.exp(blasth)
            q_rel, k_rel, q_dec, k_dec = qh * e_qr, kh * e_kr, qh * e_qd, kh * e_kd
            q_relb, k_relb, q_decb, k_decb = q_rel.astype(BF16), k_rel.astype(BF16), q_dec.astype(BF16), k_dec.astype(BF16)
            a = jnp.where(causal, _dot(q_relb, k_relb, NT), 0.0).astype(BF16)
            dv = _dot(a, dob, TN)
            da = jnp.where(causal, _dot(dob, vh, NT), 0.0).astype(BF16)
            dq_rel = _dot(da, k_relb)
            dk_rel = _dot(da, q_relb, TN)
            sts = [st_ref[h]]
            for ci in range(nc - 1):
                rows = slice(ci * HGRN_CHUNK, (ci + 1) * HGRN_CHUNK)
                sts.append(sts[-1] * dec[ci * HGRN_CHUNK:ci * HGRN_CHUNK + 1, :] + _dot(vh[rows], k_decb[rows], TN))
            dq_dec, dk_dec, dv_inter, ddec = [None] * nc, [None] * nc, [None] * nc, [None] * nc
            for ci in range(nc - 1, -1, -1):
                rows = slice(ci * HGRN_CHUNK, (ci + 1) * HGRN_CHUNK)
                st = sts[ci]
                dst = dst_scr[h]
                dstb = dst.astype(BF16)
                dq_dec[ci] = _dot(dob[rows], st.astype(BF16))
                dk_dec[ci] = _dot(vh[rows], dstb)
                dv_inter[ci] = _dot(k_decb[rows], dstb, NT)
                ddec[ci] = jnp.broadcast_to(jnp.sum(dst * st, axis=0, keepdims=True), (HGRN_CHUNK, dd))
                dst_scr[h] = dst * dec[ci * HGRN_CHUNK:ci * HGRN_CHUNK + 1, :] + _dot(dob[rows], q_decb[rows], TN)
            dq_dec = jnp.concatenate(dq_dec, axis=0)
            dk_dec = jnp.concatenate(dk_dec, axis=0)
            dv = dv + jnp.concatenate(dv_inter, axis=0)
            ddec = jnp.concatenate(ddec, axis=0)
            dp_ref[:, 2 * hk + h * dd:2 * hk + (h + 1) * dd] = dv.astype(BF16)
            dq_scr[:, sl] = dq_rel * e_qr + dq_dec * e_qd
            dk_scr[:, sl] = dk_rel * e_kr + dk_dec * e_kd
            g_qr, g_kr, g_qd, g_kd = dq_rel * q_rel, dk_rel * k_rel, dq_dec * q_dec, dk_dec * k_dec
            cat_scr[0:t, sl] = g_qr - g_kr + g_qd - g_kd
            cat_scr[t:2 * t, sl] = g_kr - g_qr
            cat_scr[2 * t:3 * t, sl] = g_kd
            ext_scr[:, sl] = ddec * dec
        upper = jnp.where(same & (c >= r), 1.0, 0.0).astype(F32)
        to_ref = jnp.where(same & (r % HGRN_CHUNK <= HGRN_CHUNK // 2), 1.0, 0.0).astype(F32)
        to_all = jnp.where(same, 1.0, 0.0).astype(F32)
        dlogf = _dot_select(jnp.concatenate([upper, to_ref, to_all], axis=1), cat_scr[...], 2) + ext_scr[...]
        df = dlogf / f - dk_scr[...]
        dp_ref[:, hk:2 * hk] = (df * (1.0 - lbv) * sig_f * (1.0 - sig_f)).astype(BF16)
        dp_ref[:, 0:hk] = (dq_scr[...] * (sig_q * (1.0 + qx * (1.0 - sig_q)))).astype(BF16)
        dlb_ref[...] += jnp.sum(df * (1.0 - sig_f), axis=0, keepdims=True)
        don_ref[...] += don

    def rev(i):
        return nt - 1 - i

    out = pl.pallas_call(
        body, grid=(nt,),
        in_specs=[pl.BlockSpec((t, 4 * hk), lambda i: (rev(i), 0)), pl.BlockSpec((1, hk), lambda i: (0, 0)),
                  pl.BlockSpec((1, dd), lambda i: (0, 0)), pl.BlockSpec((t, hk), lambda i: (rev(i), 0)),
                  pl.BlockSpec((None, hh, dd, dd), lambda i: (rev(i), 0, 0, 0)),
                  pl.BlockSpec((t, hk), lambda i: (rev(i), 0))] + r_in_specs,
        out_specs=[pl.BlockSpec((t, 4 * hk), lambda i: (rev(i), 0)), pl.BlockSpec((1, hk), lambda i: (0, 0)),
                   pl.BlockSpec((1, dd), lambda i: (0, 0))] + r_out_specs,
        out_shape=[jax.ShapeDtypeStruct((s, 4 * hk), BF16), jax.ShapeDtypeStruct((1, hk), F32),
                   jax.ShapeDtypeStruct((1, dd), F32)] + r_out,
        scratch_shapes=[pltpu.VMEM((hh, dd, dd), F32), pltpu.VMEM((3 * t, hk), F32), pltpu.VMEM((t, hk), F32),
                        pltpu.VMEM((t, hk), F32), pltpu.VMEM((t, hk), F32)] + r_scr,
        compiler_params=_params("arbitrary"), name=name)(proj, lb, onorm, o, states, dy, *r_in)
    return out[0], out[1], out[2], out[3:]


def _adamw_update(w, g, m, v):
    nm = ADAM_B1 * m + (1.0 - ADAM_B1) * g
    nv = ADAM_B2 * v + (1.0 - ADAM_B2) * (g * g)
    m_hat = nm / (1.0 - ADAM_B1 ** ADAM_STEP)
    v_hat = nv / (1.0 - ADAM_B2 ** ADAM_STEP)
    return -ADAM_LR * (m_hat / (jnp.sqrt(v_hat) + ADAM_EPS) + ADAM_WD * w), nm, nv


def _adamw(w, g, m, v, name):
    rows, cols = w.shape
    tr = _divisor_tile(rows, 256, 8)

    def body(w_ref, g_ref, m_ref, v_ref, d_ref, nm_ref, nv_ref):
        d_ref[...], nm_ref[...], nv_ref[...] = _adamw_update(w_ref[...], g_ref[...], m_ref[...], v_ref[...])

    blk = pl.BlockSpec((tr, cols), lambda i: (i, 0))
    shp = jax.ShapeDtypeStruct((rows, cols), F32)
    return pl.pallas_call(
        body, grid=(rows // tr,), in_specs=[blk] * 4, out_specs=(blk,) * 3, out_shape=(shp,) * 3,
        compiler_params=_params("parallel"), name=name)(w, g, m, v)


ADAMW_BLOCK_ELEMS = 128 * 1024


def _adamw_layers(w, lands, m, v, name, rider=None):
    ll, rows, cols = w.shape
    tr = _divisor_tile(rows, max(16, ADAMW_BLOCK_ELEMS // cols), 16)
    r_in, r_in_specs, r_out, r_out_specs, r_scr = _rider_specs(rider)
    n_rin = len(r_in)

    def body(w_ref, m_ref, v_ref, *refs):
        land_refs = refs[:ll]
        g_out, d_ref, nm_ref, nv_ref = refs[ll + n_rin:ll + n_rin + 4]
        layer = pl.program_id(0)
        _ride(rider, layer * (rows // tr) + pl.program_id(1), ll * (rows // tr), refs[ll:ll + n_rin] + refs[ll + n_rin + 4:])
        for k in range(ll):
            @pl.when(layer == k)
            def _(k=k):
                g = land_refs[k][0].astype(F32)
                for slot in range(1, N_DEV):
                    g = g + land_refs[k][slot].astype(F32)
                g_out[...] = g

        d_ref[...], nm_ref[...], nv_ref[...] = _adamw_update(w_ref[...], g_out[...], m_ref[...], v_ref[...])

    stacked = pl.BlockSpec((None, tr, cols), lambda l, i: (l, i, 0))

    def one(k):
        return pl.BlockSpec((N_DEV, tr, cols), lambda l, i: (0, jnp.where(l == k, i, 0), 0))

    shp = jax.ShapeDtypeStruct(w.shape, F32)
    out = pl.pallas_call(
        body, grid=(ll, rows // tr), in_specs=[stacked] * 3 + [one(k) for k in range(ll)] + r_in_specs,
        out_specs=[stacked] * 4 + r_out_specs, out_shape=[shp] * 4 + r_out, scratch_shapes=r_scr,
        compiler_params=_params("arbitrary", "arbitrary"), name=name)(w, m, v, *lands, *r_in)
    return out[0], out[1], out[2], out[3], out[4:]


_HBM = pl.BlockSpec(memory_space=pltpu.HBM)
_MESH = pl.DeviceIdType.MESH


class _GatherRide:
    def __init__(self, blocks, cuts):
        self.operands = list(blocks)
        self.cuts = list(cuts)
        self.out_shapes = []
        for b, cut in zip(blocks, cuts):
            r, c = b.shape
            shape = {"rows": (N_DEV * r, c), "cols": (r, N_DEV * c), "slots": (N_DEV, r, c)}[cut]
            self.out_shapes.append(jax.ShapeDtypeStruct(shape, b.dtype))
        n = len(blocks)
        self.scratch = [pltpu.SemaphoreType.DMA((7 * n,)), pltpu.SemaphoreType.DMA((7 * n,)), pltpu.SemaphoreType.DMA((n,))]

    def _parts(self, *refs):
        n = len(self.operands)
        x_refs, out_refs = refs[:n], refs[n:2 * n]
        send_sems, recv_sems, local_sems = refs[2 * n:]
        x, y, c = lax.axis_index("x"), lax.axis_index("y"), lax.axis_index("c")
        me, sibling = (x, y, c), (x, y, 1 - c)
        chips = [(1 - x, y), (x, 1 - y), (1 - x, 1 - y)]
        mine, first, passed, landed, from_sibling = [], [], [], [], []
        for e in range(n):
            x_ref, out_ref, cut = x_refs[e], out_refs[e], self.cuts[e]
            r, cc = x_ref.shape

            def place(px, py, pc, out_ref=out_ref, cut=cut, r=r, cc=cc):
                p = 4 * px + 2 * py + pc
                if cut == "rows":
                    return out_ref.at[pl.ds(pl.multiple_of(p * r, r), r), :]
                if cut == "cols":
                    return out_ref.at[:, pl.ds(pl.multiple_of(p * cc, cc), cc)]
                return out_ref.at[p]

            def copy(k, block, to, src=None, place=place, e=e):
                return pltpu.make_async_remote_copy(
                    src_ref=place(*block) if src is None else src, dst_ref=place(*block), send_sem=send_sems.at[7 * e + k],
                    recv_sem=recv_sems.at[7 * e + k], device_id=to, device_id_type=_MESH)

            mine.append(pltpu.make_async_copy(x_ref, place(*me), local_sems.at[e]))
            first += [copy(0, me, sibling, src=x_ref)] + [copy(1 + j, me, (*chip, c), src=x_ref) for j, chip in enumerate(chips)]
            passed += [copy(4 + j, (*chip, c), sibling) for j, chip in enumerate(chips)]
            landed += [copy(1 + j, (*chip, c), me) for j, chip in enumerate(chips)]
            from_sibling += [copy(0, sibling, me)] + [copy(4 + j, (*chip, 1 - c), me) for j, chip in enumerate(chips)]
        return mine, first, passed, landed, from_sibling

    def start(self, *refs):
        mine, first, _, _, _ = self._parts(*refs)
        for cp in mine + first:
            cp.start()

    def middle(self, *refs):
        _, _, passed, landed, _ = self._parts(*refs)
        for got, fwd in zip(landed, passed):
            got.wait_recv()
            fwd.start()

    def finish(self, *refs):
        mine, first, passed, _, from_sibling = self._parts(*refs)
        for cp in from_sibling:
            cp.wait_recv()
        for cp in first + passed:
            cp.wait_send()
        for cp in mine:
            cp.wait()


class _ExchangeRide:
    def __init__(self, sends):
        self.operands = list(sends)
        self.out_shapes = [jax.ShapeDtypeStruct(s.shape, s.dtype) for s in sends]
        n = len(sends)
        self.scratch = [pltpu.SemaphoreType.DMA((7 * n,)), pltpu.SemaphoreType.DMA((7 * n,)), pltpu.SemaphoreType.DMA((n,))]

    def _parts(self, *refs):
        n = len(self.operands)
        s_refs, land_refs = refs[:n], refs[n:2 * n]
        send_sems, recv_sems, local_sems = refs[2 * n:]
        x, y, c = lax.axis_index("x"), lax.axis_index("y"), lax.axis_index("c")
        me = 4 * x + 2 * y + c
        own, sends, recvs = [], [], []
        for e in range(n):
            s_ref, land_ref = s_refs[e], land_refs[e]
            own.append(pltpu.make_async_copy(s_ref.at[me], land_ref.at[me], local_sems.at[e]))
            for rel in range(1, N_DEV):
                px = 1 - x if rel & 4 else x
                py = 1 - y if rel & 2 else y
                pc = 1 - c if rel & 1 else c
                peer = 4 * px + 2 * py + pc
                k = 7 * e + rel - 1
                sends.append(pltpu.make_async_remote_copy(
                    src_ref=s_ref.at[peer], dst_ref=land_ref.at[me], send_sem=send_sems.at[k], recv_sem=recv_sems.at[k],
                    device_id=(px, py, pc), device_id_type=_MESH))
                recvs.append(pltpu.make_async_remote_copy(
                    src_ref=s_ref.at[me], dst_ref=land_ref.at[peer], send_sem=send_sems.at[k], recv_sem=recv_sems.at[k],
                    device_id=(px, py, pc), device_id_type=_MESH))
        return own, sends, recvs

    def start(self, *refs):
        own, sends, _ = self._parts(*refs)
        for cp in own + sends:
            cp.start()

    def middle(self, *refs):
        pass

    def finish(self, *refs):
        own, sends, recvs = self._parts(*refs)
        for cp in recvs:
            cp.wait_recv()
        for cp in sends:
            cp.wait_send()
        for cp in own:
            cp.wait()


def _run_alone(rider, name):
    def body(*refs):
        rider.start(*refs)
        rider.middle(*refs)
        rider.finish(*refs)

    return pl.pallas_call(
        body, out_shape=rider.out_shapes, in_specs=[_HBM] * len(rider.operands), out_specs=[_HBM] * len(rider.out_shapes),
        scratch_shapes=rider.scratch, name=name)(*rider.operands)


def _all_gather(xs, name):
    return _run_alone(_GatherRide([xs], ["slots"]), name)[0]


def _sum_slots(parts, name):
    _, rows, cols = parts.shape
    tr = _divisor_tile(rows, 256, 16)

    def body(p_ref, o_ref):
        acc = p_ref[0].astype(F32)
        for slot in range(1, N_DEV):
            acc = acc + p_ref[slot].astype(F32)
        o_ref[...] = acc

    return pl.pallas_call(
        body, grid=(rows // tr,), in_specs=[pl.BlockSpec((N_DEV, tr, cols), lambda i: (0, i, 0))],
        out_specs=pl.BlockSpec((tr, cols), lambda i: (i, 0)), out_shape=jax.ShapeDtypeStruct((rows, cols), F32),
        compiler_params=_params("parallel"), name=name)(parts)


def _carry(rode, key, riders, call):
    rider = riders.get(key)
    res = call(rider)
    if rider is None:
        return res
    res, rode[key] = res
    return res


def _kept(rode, key, riders, brought):
    if key in riders:
        rode[key] = brought


def _mlp_fwd(h, g_pre, g_post, w1, w2, tag, riders):
    rode = {}
    tm = _row_tile(h.shape[0], 2048)
    (a, r2), brought = _norm_mm(h, g_pre, w1, tm, 1024, f"{tag}_up", out_dtypes=(BF16,),
                                epi=lambda acc: (jnp.square(jnp.maximum(acc, 0.0)),), rider=riders.get("up"))
    _kept(rode, "up", riders, brought)
    z, out, brought = _mm_norm_res(r2, w2, g_post, h, 512, f"{tag}_down", riders.get("down"))
    _kept(rode, "down", riders, brought)
    return out, (h, a, r2, z), rode


def _mlp_bwd(dh, saved, g_pre, g_post, w1, w2, tag, riders):
    h, a, r2, z = saved
    rode = {}
    tm = _row_tile(h.shape[0], 1024)
    (dz, dg_post, du), brought = _rmsbwd_mm(
        z, g_post, dh, w2, tm, 1024, f"{tag}_ddown", out_dtypes=(BF16,), extras=(r2,),
        epi=lambda acc, rr: (acc * (2.0 * jnp.sqrt(rr.astype(F32))),), rider=riders.get("ddown"))
    _kept(rode, "ddown", riders, brought)
    dw2 = _carry(rode, "dw2", riders, lambda r: _mm(
        r2, dz, "tn", 512, 1024, f"{tag}_dw2", out_dtypes=(BF16,), shard="rows", rider=r))
    dw1 = _mm(a, du, "tn", 1024, 512, f"{tag}_dw1", out_dtypes=(BF16,), shard="cols")
    dh_in, dg_pre, _ = _mm_rmsbwd_res(du, w1, h, g_pre, dh, 512, f"{tag}_dup")
    return dh_in, dg_pre, dg_post, dw1, dw2, rode


def _hgrn_layer_fwd(h, g_pre, g_post, lb, onorm, w_in, w_o, tag, riders):
    rode = {}
    (a, proj), brought = _norm_mm(h, g_pre, w_in, _row_tile(h.shape[0], 2048), 1024, f"{tag}_in", rider=riders.get("in"))
    _kept(rode, "in", riders, brought)
    y, o, states, brought = _hgrn_fwd(proj, lb, onorm, f"{tag}_scan", riders.get("scan"))
    _kept(rode, "scan", riders, brought)
    m, out, _ = _mm_norm_res(y, w_o, g_post, h, 512, f"{tag}_o")
    return out, (h, a, proj, y, o, states, m), rode


def _hgrn_layer_bwd(dh, saved, g_pre, g_post, lb, onorm, w_in, w_o, tag, rider=None):
    h, a, proj, y, o, states, m = saved
    (dm, dg_post, dy), _ = _rmsbwd_mm(m, g_post, dh, w_o, 512, 1024, f"{tag}_do")
    dw_o = _mm(y, dm, "tn", 128, 1024, f"{tag}_dwo", out_dtypes=(BF16,), shard="rows")
    dproj, dlb, donorm, rode = _hgrn_bwd(proj, lb, onorm, o, states, dy, f"{tag}_dscan", rider)
    dw_in = _mm(a, dproj, "tn", 1024, 512, f"{tag}_dwin", out_dtypes=(BF16,), shard="cols")
    dh_in, dg_pre, _ = _mm_rmsbwd_res(dproj, w_in, h, g_pre, dh, 512, f"{tag}_din")
    return dh_in, dg_pre, dg_post, dlb, donorm, dw_in, dw_o, rode


def _mla_layer_fwd(h, g_pre, g_post, cos, sin, w_in, late, qn, kvn, tag, riders):
    rode = {}
    (a, proj), brought = _norm_mm(h, g_pre, w_in, 512, MLA_IN, f"{tag}_in", rider=riders.get("in"))
    _kept(rode, "in", riders, brought)
    w_uq, w_ukv, _ = late(rode)
    cqn, ckvn, q, k, v = _mla_qkv(proj, qn, kvn, w_uq, w_ukv, cos, sin, f"{tag}_qkv")
    o, lse, brought = _attn_fwd(q, k, v, f"{tag}_attn", riders.get("attn"))
    _kept(rode, "attn", riders, brought)
    _, _, w_o = late(rode)
    m, out, _ = _mm_norm_res(o, w_o, g_post, h, 512, f"{tag}_o")
    return out, (h, a, proj, cqn, ckvn, q, k, v, o, lse, m), rode


def _mla_layer_bwd(dh, saved, g_pre, g_post, cos, sin, w_in, qn, kvn, w_uq, w_ukv, w_o, tag, rider=None, own_ride=False):
    h, a, proj, cqn, ckvn, q, k, v, o, lse, m = saved
    hh, s = q.shape[0], q.shape[1]
    own = {}
    (dm, dg_post, do, delta), _ = _rmsbwd_mm(m, g_post, dh, w_o, 512, 1024, f"{tag}_do", group_sums=(o, hh))
    dw_o = _mm(o, dm, "tn", 128, 1024, f"{tag}_dwo", out_dtypes=(BF16,), shard="rows")
    if own_ride:
        rider = _ExchangeRide(list(rider.operands) + [dw_o])
    dq, dk, dv, rode = _attn_bwd(q, k, v, do, lse.reshape(hh, 1, s), delta, f"{tag}_dattn", rider)
    if own_ride:
        own["mla_w_o"] = rode[-1]
    dqe, dkve, dproj, dqn, dkvn = _mla_bwd_mid(dq, dk, dv, cos, sin, proj, qn, kvn, w_uq, w_ukv, f"{tag}_dqkv")
    dw_uq = _mm(cqn, dqe, "tn", MLA_Q_LORA, 768, f"{tag}_dwuq", out_dtypes=(BF16,))
    dw_uq = dw_uq.reshape(MLA_Q_LORA, N_DEV, -1).transpose(1, 0, 2)
    dw_ukv = _mm(ckvn, dkve, "tn", MLA_KV_LORA, 256, f"{tag}_dwukv", out_dtypes=(BF16,), shard="cols")
    dw_in = _mm(a, dproj, "tn", 128, MLA_IN, f"{tag}_dwin", out_dtypes=(BF16,), shard="rows",
                rider=_ExchangeRide([dw_uq, dw_ukv]) if own_ride else None)
    if own_ride:
        dw_in, (own["mla_w_uq"], own["mla_w_ukv"]) = dw_in
    dh_in, dg_pre, brought = _mm_rmsbwd_res(dproj, w_in, h, g_pre, dh, 512, f"{tag}_din",
                                             _ExchangeRide([dw_in]) if own_ride else None)
    if own_ride:
        own["mla_w_in"] = brought[0]
    return dh_in, dg_pre, dg_post, dqn, dkvn, dw_in, dw_uq, dw_ukv, dw_o, rode, own


_CUT = dict(mla_w_in="rows", mla_w_uq="cols", mla_w_ukv="cols", mla_w_o="rows", hgrn_w_in="cols", hgrn_w_o="rows",
            mlp_w1="cols", mlp_w2="rows")


def _unit(layer, kind):
    slot = layer // 2
    if kind == "mla":
        return [("mla_w_in", slot), ("mla_w_uq", slot), ("mla_w_ukv", slot), ("mla_w_o", slot)]
    if kind == "hgrn":
        return [("hgrn_w_in", slot), ("hgrn_w_o", slot)]
    return [("mlp_w1", layer), ("mlp_w2", layer)]


_GATHER_FIRST = [("mla_w_in", 0)]
_GATHER_PLAN = {
    (0, "in"): [("mla_w_uq", 0), ("mla_w_ukv", 0)],
    (0, "attn"): [("mla_w_o", 0)] + _unit(0, "mlp") + _unit(1, "hgrn"),
    (0, "up"): [("mlp_w1", 1)],
    (0, "down"): [("mlp_w2", 1)],
    (1, "in"): _unit(2, "mla"),
    (1, "scan"): _unit(2, "mlp"),
    (2, "attn"): _unit(3, "hgrn") + _unit(3, "mlp"),
}
_EXCHANGE_PLAN = {
    (3, "dscan"): _unit(3, "mlp"),
    (2, "ddown"): [("hgrn_w_in", 1)],
    (2, "dw2"): [("hgrn_w_o", 1)],
    (2, "dattn"): _unit(2, "mlp"),
    (1, "ddown"): _unit(2, "mla"),
    (1, "dscan"): _unit(1, "mlp"),
    (0, "ddown"): [("hgrn_w_in", 0)],
    (0, "dw2"): [("hgrn_w_o", 0)],
    (0, "dattn"): _unit(0, "mlp"),
}


def _gather_cut(name):
    return "slots" if name == "mla_w_uq" else _CUT[name]


def _gather_rider(weights, ents):
    return _GatherRide([weights[name][idx].astype(BF16) for name, idx in ents], [_gather_cut(name) for name, _ in ents])


def _gathered(outs, ents):
    res = {}
    for (name, idx), out in zip(ents, outs):
        if _gather_cut(name) == "slots":
            out = out.transpose(1, 0, 2).reshape(out.shape[1], -1)
        res[(name, idx)] = out
    return res


def _adamw_nd(w, g, m, v, name):
    shape = w.shape
    c = shape[-1]
    d, nm, nv = _adamw(w.reshape(-1, c), g.reshape(-1, c), m.reshape(-1, c), v.reshape(-1, c), name)
    return d.reshape(shape), nm.reshape(shape), nv.reshape(shape)


def kernel(x, positions, norm_gains, mla_w_in, mla_q_norm, mla_kv_norm, mla_w_uq, mla_w_ukv, mla_w_o, hgrn_w_in, hgrn_lb_logits, hgrn_o_norm, hgrn_w_o, mlp_w1, mlp_w2, loss_target, m_norm_gains, m_mla_w_in, m_mla_q_norm, m_mla_kv_norm, m_mla_w_uq, m_mla_w_ukv, m_mla_w_o, m_hgrn_w_in, m_hgrn_lb_logits, m_hgrn_o_norm, m_hgrn_w_o, m_mlp_w1, m_mlp_w2, v_norm_gains, v_mla_w_in, v_mla_q_norm, v_mla_kv_norm, v_mla_w_uq, v_mla_w_ukv, v_mla_w_o, v_hgrn_w_in, v_hgrn_lb_logits, v_hgrn_o_norm, v_hgrn_w_o, v_mlp_w1, v_mlp_w2):
    weights = dict(norm_gains=norm_gains, mla_w_in=mla_w_in, mla_q_norm=mla_q_norm, mla_kv_norm=mla_kv_norm,
                   mla_w_uq=mla_w_uq, mla_w_ukv=mla_w_ukv, mla_w_o=mla_w_o, hgrn_w_in=hgrn_w_in,
                   hgrn_lb_logits=hgrn_lb_logits, hgrn_o_norm=hgrn_o_norm, hgrn_w_o=hgrn_w_o, mlp_w1=mlp_w1, mlp_w2=mlp_w2)
    mom_m = dict(norm_gains=m_norm_gains, mla_w_in=m_mla_w_in, mla_q_norm=m_mla_q_norm, mla_kv_norm=m_mla_kv_norm,
                 mla_w_uq=m_mla_w_uq, mla_w_ukv=m_mla_w_ukv, mla_w_o=m_mla_w_o, hgrn_w_in=m_hgrn_w_in,
                 hgrn_lb_logits=m_hgrn_lb_logits, hgrn_o_norm=m_hgrn_o_norm, hgrn_w_o=m_hgrn_w_o, mlp_w1=m_mlp_w1, mlp_w2=m_mlp_w2)
    mom_v = dict(norm_gains=v_norm_gains, mla_w_in=v_mla_w_in, mla_q_norm=v_mla_q_norm, mla_kv_norm=v_mla_kv_norm,
                 mla_w_uq=v_mla_w_uq, mla_w_ukv=v_mla_w_ukv, mla_w_o=v_mla_w_o, hgrn_w_in=v_hgrn_w_in,
                 hgrn_lb_logits=v_hgrn_lb_logits, hgrn_o_norm=v_hgrn_o_norm, hgrn_w_o=v_hgrn_w_o, mlp_w1=v_mlp_w1, mlp_w2=v_mlp_w2)
    order = list(weights)
    seq = x.shape[1]
    h = x.reshape(seq, D_MODEL)
    target = loss_target.reshape(seq, D_MODEL)

    full = _gathered(_run_alone(_gather_rider(weights, _GATHER_FIRST), "gather_first"), _GATHER_FIRST)
    gains = _all_gather(norm_gains.reshape(DEPTH * 4, D_MODEL // N_DEV), "gather_gains")
    gains = gains.transpose(1, 0, 2).reshape(DEPTH, 4, 1, D_MODEL)

    def gather_riders(layer, keys):
        return {k: _gather_rider(weights, _GATHER_PLAN[(layer, k)]) for k in keys if (layer, k) in _GATHER_PLAN}

    seen = set()

    def arrived(layer, rode):
        for k, outs in rode.items():
            if (layer, k) not in seen:
                seen.add((layer, k))
                full.update(_gathered(outs, _GATHER_PLAN[(layer, k)]))

    def late_weights(layer):
        def late(rode):
            arrived(layer, rode)
            return tuple(full.get((name, layer // 2)) for name in ("mla_w_uq", "mla_w_ukv", "mla_w_o"))
        return late

    cos, sin = _rope_tables(positions.reshape(seq, 1), "rope_tables")
    lower = _lb_fwd(hgrn_lb_logits, "lower_bounds")

    def mixer_args(layer):
        slot = layer // 2
        if layer % 2 == 0:
            return (cos, sin, full[("mla_w_in", slot)], mla_q_norm[slot:slot + 1], mla_kv_norm[slot:slot + 1],
                    full[("mla_w_uq", slot)], full[("mla_w_ukv", slot)], full[("mla_w_o", slot)])
        return (lower[layer:layer + 1], hgrn_o_norm[slot:slot + 1], full[("hgrn_w_in", slot)], full[("hgrn_w_o", slot)])

    saved = []
    for layer in range(DEPTH):
        g = gains[layer]
        if layer % 2 == 0:
            slot = layer // 2
            h, sv_mix, rode = _mla_layer_fwd(
                h, g[0], g[1], cos, sin, full[("mla_w_in", slot)], late_weights(layer), mla_q_norm[slot:slot + 1],
                mla_kv_norm[slot:slot + 1], f"l{layer}_mla", gather_riders(layer, ["in", "attn"]))
            arrived(layer, rode)
        else:
            h, sv_mix, rode = _hgrn_layer_fwd(h, g[0], g[1], *mixer_args(layer), f"l{layer}_hgrn",
                                              gather_riders(layer, ["in", "scan"]))
            arrived(layer, rode)
        h, sv_mlp, rode = _mlp_fwd(h, g[2], g[3], full[("mlp_w1", layer)], full[("mlp_w2", layer)], f"l{layer}_mlp",
                                   gather_riders(layer, ["up", "down"]))
        arrived(layer, rode)
        saved.append((sv_mix, sv_mlp))

    loss_part, dh = _loss(h, target, "loss")
    loss = lax.psum(loss_part[0, 0], AXES)

    zero_row = jnp.zeros((1, D_MODEL), F32)
    dgains = [[None] * 4 for _ in range(DEPTH)]
    dlower = [zero_row] * DEPTH
    partials, lands = {}, {}
    dqn, dkvn, donorm = [None] * 2, [None] * 2, [None] * 2

    def exchange_riders(layer, keys):
        return {k: _ExchangeRide([partials[e] for e in _EXCHANGE_PLAN[(layer, k)]]) for k in keys
                if (layer, k) in _EXCHANGE_PLAN}

    def landed(layer, rode):
        for k, outs in rode.items():
            lands.update(zip(_EXCHANGE_PLAN[(layer, k)], outs))

    for layer in range(DEPTH - 1, -1, -1):
        slot = layer // 2
        g = gains[layer]
        sv_mix, sv_mlp = saved[layer]
        dh, dgains[layer][2], dgains[layer][3], partials[("mlp_w1", layer)], partials[("mlp_w2", layer)], rode = _mlp_bwd(
            dh, sv_mlp, g[2], g[3], full[("mlp_w1", layer)], full[("mlp_w2", layer)], f"l{layer}_mlp",
            exchange_riders(layer, ["ddown", "dw2"]))
        landed(layer, rode)
        key = "dattn" if layer % 2 == 0 else "dscan"
        rider = exchange_riders(layer, [key]).get(key)
        if layer % 2 == 0:
            (dh, dgains[layer][0], dgains[layer][1], dqn[slot], dkvn[slot], partials[("mla_w_in", slot)],
             partials[("mla_w_uq", slot)], partials[("mla_w_ukv", slot)], partials[("mla_w_o", slot)], brought,
             own) = _mla_layer_bwd(dh, sv_mix, g[0], g[1], *mixer_args(layer), f"l{layer}_mla", rider, own_ride=(layer == 0))
            lands.update({(name, slot): land for name, land in own.items()})
        else:
            (dh, dgains[layer][0], dgains[layer][1], dlower[layer], donorm[slot], partials[("hgrn_w_in", slot)],
             partials[("hgrn_w_o", slot)], brought) = _hgrn_layer_bwd(dh, sv_mix, g[0], g[1], *mixer_args(layer), f"l{layer}_hgrn", rider)
        landed(layer, {key: brought} if rider is not None else {})
    grad_x = dh.reshape(x.shape)
    dlogits = _lb_bwd(hgrn_lb_logits, jnp.concatenate(dlower, axis=0), "lower_bounds_bwd")

    pad = jnp.zeros((1, D_MODEL - 2 * MLA_KV_LORA), F32)
    pad2 = jnp.zeros((1, D_MODEL - 2 * HGRN_D), F32)
    small = jnp.concatenate(
        [jnp.concatenate([gg for row in dgains for gg in row], axis=0), jnp.concatenate(dqn, axis=1),
         jnp.concatenate(dkvn + [pad], axis=1), dlogits, jnp.concatenate(donorm + [pad2], axis=1), zero_row], axis=0)
    small = _sum_slots(_all_gather(small, "gather_small_grads"), "sum_small_grads")
    me = 4 * lax.axis_index("x") + 2 * lax.axis_index("y") + lax.axis_index("c")
    n_g = DEPTH * 4
    width = D_MODEL // N_DEV
    grads = {}
    grads["norm_gains"] = lax.dynamic_slice(small[:n_g], (0, me * width), (n_g, width)).reshape(DEPTH, 4, width)
    grads["mla_q_norm"] = small[n_g].reshape(2, MLA_Q_LORA)
    grads["mla_kv_norm"] = small[n_g + 1, :2 * MLA_KV_LORA].reshape(2, MLA_KV_LORA)
    grads["hgrn_lb_logits"] = small[n_g + 2:n_g + 2 + DEPTH]
    grads["hgrn_o_norm"] = small[n_g + 2 + DEPTH, :2 * HGRN_D].reshape(2, HGRN_D)

    deltas, new_m, new_v = {}, {}, {}
    for name in order:
        if name in _CUT:
            per_layer = [lands[(name, idx)] for idx in range(weights[name].shape[0])]
            grads[name], deltas[name], new_m[name], new_v[name], _ = _adamw_layers(
                weights[name], per_layer, mom_m[name], mom_v[name], f"adamw_{name}")
        else:
            deltas[name], new_m[name], new_v[name] = _adamw_nd(weights[name], grads[name], mom_m[name], mom_v[name], f"adamw_{name}")
    return (loss, grad_x, *[grads[n] for n in order], *[deltas[n] for n in order], *[new_m[n] for n in order],
            *[new_v[n] for n in order])
```

```python
import numpy as np
import jax
import jax.numpy as jnp
from jax import lax
from jax.experimental import pallas as pl
from jax.experimental.pallas import tpu as pltpu

F32, BF16 = jnp.float32, jnp.bfloat16

N_DEV = 8
AXES = ("x", "y", "c")
D_MODEL = 1024
DEPTH = 4
MLA_HEADS = 8
MLA_Q_LORA = 512
MLA_KV_LORA = 256
MLA_NOPE = 128
MLA_ROPE = 64
MLA_V = 128
MLA_QK = MLA_NOPE + MLA_ROPE
MLA_IN = MLA_Q_LORA + MLA_KV_LORA + MLA_ROPE
ROPE_BASE = 10000.0
HGRN_HEADS = 8
HGRN_D = 128
HGRN_CHUNK = 32
D_FF = 4 * D_MODEL
EPS = 1e-6
LOG2_E = 1.4426950408889634
ADAM_LR, ADAM_B1, ADAM_B2, ADAM_EPS, ADAM_WD, ADAM_STEP = 0.001, 0.9, 0.999, 1e-08, 0.01, 10

V7X_VMEM_LIMIT_BYTES = 56 * 1024 * 1024

NN = (((1,), (0,)), ((), ()))
NT = (((1,), (1,)), ((), ()))
TN = (((0,), (0,)), ((), ()))
_DIMS = {"nn": NN, "nt": NT, "tn": TN}


def _params(*sem):
    return pltpu.CompilerParams(dimension_semantics=sem, vmem_limit_bytes=V7X_VMEM_LIMIT_BYTES)


def _dot(a, b, dims=NN):
    return lax.dot_general(a, b, dims, preferred_element_type=F32)


def _dot_select(sel, x, pieces, dims=NN):
    sel = sel.astype(BF16)
    acc, rest = None, x
    for _ in range(pieces):
        term = rest.astype(BF16)
        part = _dot(sel, term, dims)
        acc = part if acc is None else acc + part
        rest = rest - term.astype(F32)
    return acc


def _rstd(x):
    return lax.rsqrt(jnp.mean(x * x, axis=-1, keepdims=True) + EPS)


def _rms_bwd_rows(x, g, dy):
    r = _rstd(x)
    xh = x * r
    dyg = dy * g
    dx = r * (dyg - xh * jnp.mean(dyg * xh, axis=-1, keepdims=True))
    return dx, dy * xh


def _row_tile(n, want):
    t = min(n, want)
    assert n % t == 0, (n, t)
    return t


def _divisor_tile(n, cap, mult):
    for t in range(min(cap, n) - min(cap, n) % mult, 0, -mult):
        if n % t == 0:
            return t
    return n


def _rms_fwd(x, g, res, out_dtype, name):
    s, d = x.shape
    ts = _row_tile(s, 512)

    def body(x_ref, g_ref, *rest):
        xf = x_ref[...]
        y = xf * _rstd(xf) * g_ref[...]
        if res is not None:
            y = rest[0][...] + y
        rest[-1][...] = y.astype(out_dtype)

    row = pl.BlockSpec((ts, d), lambda i: (i, 0))
    vec = pl.BlockSpec((1, d), lambda i: (0, 0))
    ins = [x, g] + ([res] if res is not None else [])
    return pl.pallas_call(
        body, grid=(s // ts,), in_specs=[row, vec] + ([row] if res is not None else []), out_specs=row,
        out_shape=jax.ShapeDtypeStruct((s, d), out_dtype), compiler_params=_params("parallel"), name=name)(*ins)


def _rms_bwd(x, g, dy, res, out_dtype, name):
    s, d = x.shape
    ts = _row_tile(s, 512)

    def body(x_ref, g_ref, dy_ref, *rest):
        dx_ref, dg_ref = rest[-2:]
        dx, dg = _rms_bwd_rows(x_ref[...], g_ref[...], dy_ref[...].astype(F32))
        if res is not None:
            dx = rest[0][...] + dx
        dx_ref[...] = dx.astype(out_dtype)

        @pl.when(pl.program_id(0) == 0)
        def _():
            dg_ref[...] = jnp.zeros_like(dg_ref)

        dg_ref[...] += jnp.sum(dg, axis=0, keepdims=True)

    row = pl.BlockSpec((ts, d), lambda i: (i, 0))
    vec = pl.BlockSpec((1, d), lambda i: (0, 0))
    ins = [x, g, dy] + ([res] if res is not None else [])
    return pl.pallas_call(
        body, grid=(s // ts,), in_specs=[row, vec, row] + ([row] if res is not None else []), out_specs=(row, vec),
        out_shape=(jax.ShapeDtypeStruct((s, d), out_dtype), jax.ShapeDtypeStruct((1, d), F32)),
        compiler_params=_params("arbitrary"), name=name)(*ins)


def _mm(a, b, mode, tm, tn, name, out_dtypes=(F32,), shard=None, epi=None, extras=(), rider=None):
    if mode == "tn":
        k, m = a.shape
        a_spec = pl.BlockSpec((k, tm), lambda i, j: (0, i))
    else:
        m, k = a.shape
        a_spec = pl.BlockSpec((tm, k), lambda i, j: (i, 0))
    if mode == "nt":
        n = b.shape[0]
        b_spec = pl.BlockSpec((tn, k), lambda i, j: (j, 0))
    else:
        n = b.shape[1]
        b_spec = pl.BlockSpec((k, tn), lambda i, j: (0, j))
    assert m % tm == 0 and n % tn == 0, (name, m, tm, n, tn)
    tile = pl.BlockSpec((tm, tn), lambda i, j: (i, j))
    if shard == "rows":
        per = m // N_DEV // tm
        out_specs = [pl.BlockSpec((None, tm, tn), lambda i, j: (i // per, i % per, j))]
        out_shape = [jax.ShapeDtypeStruct((N_DEV, m // N_DEV, n), out_dtypes[0])]
    elif shard == "cols":
        per = n // N_DEV // tn
        out_specs = [pl.BlockSpec((None, tm, tn), lambda i, j: (j // per, i, j % per))]
        out_shape = [jax.ShapeDtypeStruct((N_DEV, m, n // N_DEV), out_dtypes[0])]
    else:
        out_specs = [tile for _ in out_dtypes]
        out_shape = [jax.ShapeDtypeStruct((m, n), dt) for dt in out_dtypes]
    n_ex, n_out = len(extras), len(out_shape)
    r_in, r_in_specs, r_out, r_out_specs, r_scr = _rider_specs(rider)
    n_rin, n_rout = len(r_in), len(r_out)
    grid = (m // tm, n // tn)

    def body(a_ref, b_ref, *refs):
        ex_refs = refs[:n_ex]
        o_refs = refs[n_ex + n_rin:n_ex + n_rin + n_out]
        r_refs = refs[n_ex:n_ex + n_rin] + refs[n_ex + n_rin + n_out:]
        _ride(rider, pl.program_id(0) * grid[1] + pl.program_id(1), grid[0] * grid[1], r_refs)
        acc = _dot(a_ref[...].astype(BF16), b_ref[...].astype(BF16), _DIMS[mode])
        vals = (acc,) if epi is None else epi(acc, *[r[...] for r in ex_refs])
        for o_ref, val in zip(o_refs, vals):
            o_ref[...] = val.astype(o_ref.dtype)

    sem = ("parallel", "parallel") if rider is None else ("arbitrary", "arbitrary")
    out = pl.pallas_call(
        body, grid=grid, in_specs=[a_spec, b_spec] + [tile] * n_ex + r_in_specs, out_specs=out_specs + r_out_specs,
        out_shape=out_shape + r_out, scratch_shapes=r_scr, compiler_params=_params(*sem), name=name)(a, b, *extras, *r_in)
    res = out[0] if n_out == 1 else out[:n_out]
    return res if rider is None else (res, out[n_out:])


def _norm_mm(x, g, b, tm, tn, name, out_dtypes=(F32,), epi=None, rider=None):
    m, k = x.shape
    n = b.shape[1]
    assert m % tm == 0 and n % tn == 0, (name, m, tm, n, tn)
    grid = (m // tm, n // tn)
    n_out = len(out_dtypes)
    r_in, r_in_specs, r_out, r_out_specs, r_scr = _rider_specs(rider)
    n_rin = len(r_in)

    def body(x_ref, g_ref, b_ref, *refs):
        a_ref = refs[n_rin]
        o_refs = refs[n_rin + 1:n_rin + 1 + n_out]
        _ride(rider, pl.program_id(0) * grid[1] + pl.program_id(1), grid[0] * grid[1], refs[:n_rin] + refs[n_rin + 1 + n_out:])

        @pl.when(pl.program_id(1) == 0)
        def _():
            xf = x_ref[...]
            a_ref[...] = (xf * _rstd(xf) * g_ref[...]).astype(BF16)

        acc = _dot(a_ref[...], b_ref[...].astype(BF16))
        vals = (acc,) if epi is None else epi(acc)
        for o_ref, val in zip(o_refs, vals):
            o_ref[...] = val.astype(o_ref.dtype)

    row = pl.BlockSpec((tm, k), lambda i, j: (i, 0))
    tile = pl.BlockSpec((tm, tn), lambda i, j: (i, j))
    out = pl.pallas_call(
        body, grid=grid,
        in_specs=[row, pl.BlockSpec((1, k), lambda i, j: (0, 0)), pl.BlockSpec((k, tn), lambda i, j: (0, j))] + r_in_specs,
        out_specs=[row] + [tile] * n_out + r_out_specs,
        out_shape=[jax.ShapeDtypeStruct((m, k), BF16)] + [jax.ShapeDtypeStruct((m, n), dt) for dt in out_dtypes] + r_out,
        scratch_shapes=r_scr, compiler_params=_params("arbitrary", "arbitrary"), name=name)(x, g, b, *r_in)
    return out[:1 + n_out], out[1 + n_out:]


def _mm_norm_res(a, b, g, res, tm, name, rider=None):
    m, k = a.shape
    n = b.shape[1]
    assert m % tm == 0, (name, m, tm)
    r_in, r_in_specs, r_out, r_out_specs, r_scr = _rider_specs(rider)
    n_rin = len(r_in)

    def body(a_ref, b_ref, g_ref, res_ref, *refs):
        z_ref, o_ref = refs[n_rin:n_rin + 2]
        _ride(rider, pl.program_id(0), m // tm, refs[:n_rin] + refs[n_rin + 2:])
        z = _dot(a_ref[...].astype(BF16), b_ref[...].astype(BF16))
        z_ref[...] = z
        o_ref[...] = res_ref[...] + z * _rstd(z) * g_ref[...]

    row = pl.BlockSpec((tm, n), lambda i: (i, 0))
    out = pl.pallas_call(
        body, grid=(m // tm,),
        in_specs=[pl.BlockSpec((tm, k), lambda i: (i, 0)), pl.BlockSpec((k, n), lambda i: (0, 0)),
                  pl.BlockSpec((1, n), lambda i: (0, 0)), row] + r_in_specs,
        out_specs=[row, row] + r_out_specs,
        out_shape=[jax.ShapeDtypeStruct((m, n), F32), jax.ShapeDtypeStruct((m, n), F32)] + r_out,
        scratch_shapes=r_scr, compiler_params=_params("arbitrary"), name=name)(a, b, g, res, *r_in)
    return out[0], out[1], out[2:]


def _rmsbwd_mm(x, g, dy, b, tm, tn, name, out_dtypes=(F32,), epi=None, extras=(), rider=None, group_sums=None):
    m, k = x.shape
    n = b.shape[0]
    assert m % tm == 0 and n % tn == 0, (name, m, tm, n, tn)
    grid = (m // tm, n // tn)
    n_ex, n_out = len(extras), len(out_dtypes)
    r_in, r_in_specs, r_out, r_out_specs, r_scr = _rider_specs(rider)
    n_rin = len(r_in)
    n_gs = 0 if group_sums is None else 1
    if n_gs:
        assert tn == n and epi is None
        extras = tuple(extras) + (group_sums[0],)
        n_ex += 1

    def body(x_ref, g_ref, dy_ref, b_ref, *refs):
        ex_refs = refs[:n_ex]
        dx_ref, dg_ref = refs[n_ex + n_rin:n_ex + n_rin + 2]
        o_refs = refs[n_ex + n_rin + 2:n_ex + n_rin + 2 + n_out]
        i, j = pl.program_id(0), pl.program_id(1)
        _ride(rider, i * grid[1] + j, grid[0] * grid[1], refs[n_ex:n_ex + n_rin] + refs[n_ex + n_rin + 2 + n_out + n_gs:])

        @pl.when((i == 0) & (j == 0))
        def _():
            dg_ref[...] = jnp.zeros_like(dg_ref)

        @pl.when(j == 0)
        def _():
            dx, dg = _rms_bwd_rows(x_ref[...], g_ref[...], dy_ref[...])
            dx_ref[...] = dx.astype(BF16)
            dg_ref[...] += jnp.sum(dg, axis=0, keepdims=True)

        acc = _dot(dx_ref[...], b_ref[...].astype(BF16), NT)
        if n_gs:
            groups = group_sums[1]
            col = lax.broadcasted_iota(jnp.int32, (groups, n), 1) // (n // groups)
            sel = jnp.where(col == lax.broadcasted_iota(jnp.int32, (groups, n), 0), 1.0, 0.0)
            refs[n_ex + n_rin + 2 + n_out][...] = _dot_select(sel, acc * ex_refs[-1][...], 3, NT)
            vals = (acc,)
        else:
            vals = (acc,) if epi is None else epi(acc, *[r[...] for r in ex_refs])
        for o_ref, val in zip(o_refs, vals):
            o_ref[...] = val.astype(o_ref.dtype)

    row = pl.BlockSpec((tm, k), lambda i, j: (i, 0))
    vec = pl.BlockSpec((1, k), lambda i, j: (0, 0))
    tile = pl.BlockSpec((tm, tn), lambda i, j: (i, j))
    gs_specs = [pl.BlockSpec((group_sums[1], tm), lambda i, j: (0, i))] if n_gs else []
    gs_shape = [jax.ShapeDtypeStruct((group_sums[1], m), F32)] if n_gs else []
    out = pl.pallas_call(
        body, grid=grid,
        in_specs=[row, vec, row, pl.BlockSpec((tn, k), lambda i, j: (j, 0))] + [tile] * n_ex + r_in_specs,
        out_specs=[row, vec] + [tile] * n_out + gs_specs + r_out_specs,
        out_shape=[jax.ShapeDtypeStruct((m, k), BF16), jax.ShapeDtypeStruct((1, k), F32)]
        + [jax.ShapeDtypeStruct((m, n), dt) for dt in out_dtypes] + gs_shape + r_out,
        scratch_shapes=r_scr, compiler_params=_params("arbitrary", "arbitrary"), name=name)(x, g, dy, b, *extras, *r_in)
    return out[:2 + n_out + n_gs], out[2 + n_out + n_gs:]


def _mm_rmsbwd_res(a, b, x, g, res, tm, name, rider=None):
    m, k = a.shape
    n = b.shape[0]
    assert m % tm == 0, (name, m, tm)
    r_in, r_in_specs, r_out, r_out_specs, r_scr = _rider_specs(rider)
    n_rin = len(r_in)

    def body(a_ref, b_ref, x_ref, g_ref, res_ref, *refs):
        o_ref, dg_ref = refs[n_rin:n_rin + 2]
        _ride(rider, pl.program_id(0), m // tm, refs[:n_rin] + refs[n_rin + 2:])

        @pl.when(pl.program_id(0) == 0)
        def _():
            dg_ref[...] = jnp.zeros_like(dg_ref)

        da = _dot(a_ref[...].astype(BF16), b_ref[...].astype(BF16), NT)
        dx, dg = _rms_bwd_rows(x_ref[...], g_ref[...], da)
        o_ref[...] = res_ref[...] + dx
        dg_ref[...] += jnp.sum(dg, axis=0, keepdims=True)

    row = pl.BlockSpec((tm, n), lambda i: (i, 0))
    vec = pl.BlockSpec((1, n), lambda i: (0, 0))
    out = pl.pallas_call(
        body, grid=(m // tm,),
        in_specs=[pl.BlockSpec((tm, k), lambda i: (i, 0)), pl.BlockSpec((n, k), lambda i: (0, 0)), row, vec, row] + r_in_specs,
        out_specs=[row, vec] + r_out_specs,
        out_shape=[jax.ShapeDtypeStruct((m, n), F32), jax.ShapeDtypeStruct((1, n), F32)] + r_out,
        scratch_shapes=r_scr, compiler_params=_params("arbitrary"), name=name)(a, b, x, g, res, *r_in)
    return out[0], out[1], out[2:]


def _rope_tables(pos, name):
    s = pos.shape[0]
    half = MLA_ROPE // 2
    inv_freq = jnp.asarray(np.power(np.float32(ROPE_BASE), -np.arange(0, MLA_ROPE, 2, dtype=np.float32) / MLA_ROPE)
                           .astype(np.float32).reshape(1, half))

    def body(p_ref, f_ref, c_ref, s_ref):
        ang = p_ref[...].astype(F32) * f_ref[...]
        c_ref[...] = jnp.cos(ang)
        s_ref[...] = jnp.sin(ang)

    return pl.pallas_call(
        body, out_shape=(jax.ShapeDtypeStruct((s, half), F32), jax.ShapeDtypeStruct((s, half), F32)), name=name)(pos, inv_freq)


def _lb_softmax(logits):
    m = jnp.max(logits, axis=0, keepdims=True)
    e = jnp.exp(logits - m)
    return e / jnp.sum(e, axis=0, keepdims=True)


def _lb_fwd(logits, name):
    def body(l_ref, o_ref):
        p = _lb_softmax(l_ref[...])
        acc = jnp.zeros_like(p[0:1])
        o_ref[0:1, :] = acc
        for layer in range(1, DEPTH):
            acc = acc + p[layer:layer + 1]
            o_ref[layer:layer + 1, :] = acc

    return pl.pallas_call(body, out_shape=jax.ShapeDtypeStruct(logits.shape, F32), name=name)(logits)


def _lb_bwd(logits, dlb, name):
    def body(l_ref, d_ref, o_ref):
        p = _lb_softmax(l_ref[...])
        d = d_ref[...]
        dp = [jnp.zeros_like(d[0:1])] * DEPTH
        run = jnp.zeros_like(d[0:1])
        for layer in range(DEPTH - 1, 0, -1):
            run = run + d[layer:layer + 1]
            dp[layer] = run
        inner = sum(p[layer:layer + 1] * dp[layer] for layer in range(DEPTH))
        for layer in range(DEPTH):
            o_ref[layer:layer + 1, :] = p[layer:layer + 1] * (dp[layer] - inner)

    return pl.pallas_call(body, out_shape=jax.ShapeDtypeStruct(logits.shape, F32), name=name)(logits, dlb)


def _loss(y, target, name):
    s, d = y.shape
    ts = _row_tile(s, 512)

    def body(y_ref, t_ref, l_ref, dy_ref):
        e = y_ref[...] - t_ref[...]
        dy_ref[...] = e / d

        @pl.when(pl.program_id(0) == 0)
        def _():
            l_ref[...] = jnp.zeros_like(l_ref)

        l_ref[...] += 0.5 * jnp.sum(jnp.mean(e * e, axis=-1, keepdims=True), axis=0, keepdims=True)

    row = pl.BlockSpec((ts, d), lambda i: (i, 0))
    return pl.pallas_call(
        body, grid=(s // ts,), in_specs=[row, row], out_specs=(pl.BlockSpec((1, 1), lambda i: (0, 0)), row),
        out_shape=(jax.ShapeDtypeStruct((1, 1), F32), jax.ShapeDtypeStruct((s, d), F32)),
        compiler_params=_params("arbitrary"), name=name)(y, target)


def _rope(t1, t2, cos, sin):
    return t1 * cos - t2 * sin, t1 * sin + t2 * cos


def _rope_bwd(d1, d2, cos, sin):
    return d1 * cos + d2 * sin, d2 * cos - d1 * sin


def _mla_qkv(proj, qn, kvn, w_uq, w_ukv, cos, sin, name):
    s = proj.shape[0]
    ts = _row_tile(s, 256)
    hh, half = MLA_HEADS, MLA_ROPE // 2

    def body(p_ref, qn_ref, kvn_ref, wq_ref, wkv_ref, c_ref, s_ref, cq_ref, ckv_ref, q_ref, k_ref, v_ref):
        p = p_ref[...]
        cq, ckv, kr = p[:, :MLA_Q_LORA], p[:, MLA_Q_LORA:MLA_Q_LORA + MLA_KV_LORA], p[:, MLA_Q_LORA + MLA_KV_LORA:]
        cqn = (cq * _rstd(cq) * qn_ref[...]).astype(BF16)
        ckvn = (ckv * _rstd(ckv) * kvn_ref[...]).astype(BF16)
        cq_ref[...] = cqn
        ckv_ref[...] = ckvn
        qe = _dot(cqn, wq_ref[...])
        kve = _dot(ckvn, wkv_ref[...])
        cos_, sin_ = c_ref[...], s_ref[...]
        k1, k2 = _rope(kr[:, :half], kr[:, half:], cos_, sin_)
        k1, k2 = k1.astype(BF16), k2.astype(BF16)
        for h in range(hh):
            b = h * MLA_QK
            q_ref[h, :, 0:MLA_NOPE] = qe[:, b:b + MLA_NOPE].astype(BF16)
            q1, q2 = _rope(qe[:, b + MLA_NOPE:b + MLA_NOPE + half], qe[:, b + MLA_NOPE + half:b + MLA_QK], cos_, sin_)
            q_ref[h, :, MLA_NOPE:MLA_NOPE + half] = q1.astype(BF16)
            q_ref[h, :, MLA_NOPE + half:MLA_QK] = q2.astype(BF16)
            b = h * (MLA_NOPE + MLA_V)
            k_ref[h, :, 0:MLA_NOPE] = kve[:, b:b + MLA_NOPE].astype(BF16)
            k_ref[h, :, MLA_NOPE:MLA_NOPE + half] = k1
            k_ref[h, :, MLA_NOPE + half:MLA_QK] = k2
            v_ref[h] = kve[:, b + MLA_NOPE:b + MLA_NOPE + MLA_V].astype(BF16)

    def row(w):
        return pl.BlockSpec((ts, w), lambda i: (i, 0))

    def full(shape):
        return pl.BlockSpec(shape, lambda i: (0,) * len(shape))

    def heads(w):
        return pl.BlockSpec((hh, ts, w), lambda i: (0, i, 0))

    return pl.pallas_call(
        body, grid=(s // ts,),
        in_specs=[row(MLA_IN), full(qn.shape), full(kvn.shape), full(w_uq.shape), full(w_ukv.shape), row(half), row(half)],
        out_specs=(row(MLA_Q_LORA), row(MLA_KV_LORA), heads(MLA_QK), heads(MLA_QK), heads(MLA_V)),
        out_shape=(jax.ShapeDtypeStruct((s, MLA_Q_LORA), BF16), jax.ShapeDtypeStruct((s, MLA_KV_LORA), BF16),
                   jax.ShapeDtypeStruct((hh, s, MLA_QK), BF16), jax.ShapeDtypeStruct((hh, s, MLA_QK), BF16),
                   jax.ShapeDtypeStruct((hh, s, MLA_V), BF16)),
        compiler_params=_params("parallel"), name=name)(proj, qn, kvn, w_uq, w_ukv, cos, sin)


ATTN_BLOCK = 1024
ATTN_FWD_TILE = (256, 1024)
ATTN_BWD_TILE = (512, 512)


def _attn_block(s):
    return _row_tile(s, ATTN_BLOCK)


def _tile_sees(diag, q0, tq, k0, tk):
    if not diag:
        return True, False
    return k0 <= q0 + tq - 1, k0 + tk - 1 > q0


def _causal_pairs(nb, kv_major):
    if kv_major:
        pairs = [(i, j) for j in range(nb) for i in range(j, nb)]
    else:
        pairs = [(i, j) for i in range(nb) for j in range(i + 1)]
    return (jnp.asarray(np.array([p[0] for p in pairs], np.int32)), jnp.asarray(np.array([p[1] for p in pairs], np.int32)))


def _ride(rider, step, total, refs):
    if rider is None:
        return

    @pl.when(step == 0)
    def _():
        rider.start(*refs)

    @pl.when(step == (total * 7) // 8)
    def _():
        rider.middle(*refs)

    @pl.when(step == total - 1)
    def _():
        rider.finish(*refs)


def _rider_specs(rider):
    if rider is None:
        return [], [], [], [], []
    return (list(rider.operands), [_HBM] * len(rider.operands), list(rider.out_shapes), [_HBM] * len(rider.out_shapes),
            list(rider.scratch))


def _attn_fwd(q, k, v, name, rider=None):
    hh, s, _ = q.shape
    blk = _attn_block(s)
    tq, tk = min(blk, ATTN_FWD_TILE[0]), min(blk, ATTN_FWD_TILE[1])
    nb = s // blk
    it, jt = _causal_pairs(nb, kv_major=False)
    npair = int(it.shape[0])
    scale = MLA_QK ** -0.5
    c2 = scale * LOG2_E
    r_in, r_in_specs, r_out, r_out_specs, r_scr = _rider_specs(rider)
    n_rin, n_rout, n_rscr = len(r_in), len(r_out), len(r_scr)

    def body(it_ref, jt_ref, q_ref, k_ref, v_ref, *refs):
        r_refs = refs[:n_rin] + refs[n_rin + 2:n_rin + 2 + n_rout] + refs[len(refs) - n_rscr:]
        o_ref, lse_ref = refs[n_rin:n_rin + 2]
        m_scr, acc_scr, v_scr = refs[n_rin + 2 + n_rout:n_rin + 2 + n_rout + 3]
        h, t = pl.program_id(0), pl.program_id(1)
        step = h * npair + t
        _ride(rider, step, hh * npair, r_refs)
        i, j = it_ref[t], jt_ref[t]

        @pl.when(j == 0)
        def _():
            m_scr[...] = jnp.full_like(m_scr, -jnp.inf)
            acc_scr[...] = jnp.zeros_like(acc_scr)
            v_scr[:, MLA_V:] = jnp.ones((blk, MLA_V), BF16)

        def block(diag):
            v_scr[:, :MLA_V] = v_ref[...]
            for k0 in range(0, blk, tk):
                kb, vb = k_ref[k0:k0 + tk, :], v_scr[k0:k0 + tk, :]
                for q0 in range(0, blk, tq):
                    visible, needs_mask = _tile_sees(diag, q0, tq, k0, tk)
                    if not visible:
                        continue
                    rows = slice(q0, q0 + tq)
                    sc = _dot(q_ref[rows, :], kb, NT)
                    if needs_mask:
                        qpos = q0 + lax.broadcasted_iota(jnp.int32, (tq, tk), 0)
                        kpos = k0 + lax.broadcasted_iota(jnp.int32, (tq, tk), 1)
                        sc = jnp.where(qpos >= kpos, sc, -jnp.inf)
                    m_prev = m_scr[rows, :]
                    m_new = jnp.maximum(m_prev, jnp.max(sc, axis=-1, keepdims=True))
                    alpha = jnp.exp2((m_prev - m_new) * c2)
                    p = jnp.exp2((sc - m_new) * c2)
                    acc_scr[rows, :] = alpha * acc_scr[rows, :] + _dot(p.astype(BF16), vb)
                    m_scr[rows, :] = m_new

        @pl.when(j < i)
        def _():
            block(False)

        @pl.when(j == i)
        def _():
            block(True)
            acc = acc_scr[...]
            l = acc[:, MLA_V:MLA_V + 1]
            o_ref[...] = acc[:, :MLA_V] / l
            lse_ref[...] = m_scr[...] * scale + jnp.log(l)

    grid_spec = pltpu.PrefetchScalarGridSpec(
        num_scalar_prefetch=2, grid=(hh, npair),
        in_specs=[pl.BlockSpec((None, blk, MLA_QK), lambda h, t, it_, jt_: (h, it_[t], 0)),
                  pl.BlockSpec((None, blk, MLA_QK), lambda h, t, it_, jt_: (h, jt_[t], 0)),
                  pl.BlockSpec((None, blk, MLA_V), lambda h, t, it_, jt_: (h, jt_[t], 0))] + r_in_specs,
        out_specs=[pl.BlockSpec((blk, MLA_V), lambda h, t, it_, jt_: (it_[t], h)),
                   pl.BlockSpec((None, blk, 1), lambda h, t, it_, jt_: (h, it_[t], 0))] + r_out_specs,
        scratch_shapes=[pltpu.VMEM((blk, 1), F32), pltpu.VMEM((blk, 2 * MLA_V), F32),
                        pltpu.VMEM((blk, 2 * MLA_V), BF16)] + r_scr)
    out = pl.pallas_call(
        body, grid_spec=grid_spec,
        out_shape=[jax.ShapeDtypeStruct((s, hh * MLA_V), F32), jax.ShapeDtypeStruct((hh, s, 1), F32)] + r_out,
        compiler_params=_params("arbitrary", "arbitrary"), name=name)(it, jt, q, k, v, *r_in)
    return out[0], out[1], out[2:]


def _attn_bwd(q, k, v, do, lse_row, delta_row, name, rider=None):
    hh, s, _ = q.shape
    blk = _attn_block(s)
    tq, tk = min(blk, ATTN_BWD_TILE[0]), min(blk, ATTN_BWD_TILE[1])
    nb = s // blk
    it, jt = _causal_pairs(nb, kv_major=True)
    npair = int(it.shape[0])
    scale = MLA_QK ** -0.5
    c2 = scale * LOG2_E
    r_in, r_in_specs, r_out, r_out_specs, r_scr = _rider_specs(rider)
    n_rin, n_rout, n_rscr = len(r_in), len(r_out), len(r_scr)

    def body(it_ref, jt_ref, q_ref, k_ref, v_ref, do_ref, lse_ref, dl_ref, *refs):
        r_refs = refs[:n_rin] + refs[n_rin + 3:n_rin + 3 + n_rout] + refs[len(refs) - n_rscr:]
        dq_out, dk_out, dv_out = refs[n_rin:n_rin + 3]
        dq_ref, dk_ref, dv_ref = refs[n_rin + 3 + n_rout:n_rin + 3 + n_rout + 3]
        h, t = pl.program_id(0), pl.program_id(1)
        step = h * npair + t
        _ride(rider, step, hh * npair, r_refs)
        i, j = it_ref[t], jt_ref[t]

        @pl.when(t == 0)
        def _():
            dq_ref[...] = jnp.zeros_like(dq_ref)

        def block(diag):
            if diag:
                dk_ref[...] = jnp.zeros_like(dk_ref)
                dv_ref[...] = jnp.zeros_like(dv_ref)
            for q0 in range(0, blk, tq):
                qb = q_ref[q0:q0 + tq, :]
                dob = do_ref[q0:q0 + tq, :].astype(BF16)
                lse2 = lse_ref[:, q0:q0 + tq] * LOG2_E
                dl = dl_ref[pl.ds(h, 1), q0:q0 + tq]
                dq = None
                for k0 in range(0, blk, tk):
                    visible, needs_mask = _tile_sees(diag, q0, tq, k0, tk)
                    if not visible:
                        continue
                    kb, vb = k_ref[k0:k0 + tk, :], v_ref[k0:k0 + tk, :]
                    pt = jnp.exp2(_dot(kb, qb, NT) * c2 - lse2)
                    if needs_mask:
                        kpos = k0 + lax.broadcasted_iota(jnp.int32, (tk, tq), 0)
                        qpos = q0 + lax.broadcasted_iota(jnp.int32, (tk, tq), 1)
                        pt = jnp.where(qpos >= kpos, pt, 0.0)
                    dv_ref[k0:k0 + tk, :] += _dot(pt.astype(BF16), dob)
                    dpt = _dot(vb, dob, NT)
                    dst = (pt * (dpt - dl) * scale).astype(BF16)
                    dk_ref[k0:k0 + tk, :] += _dot(dst, qb)
                    part = _dot(dst, kb, TN)
                    dq = part if dq is None else dq + part
                rows = pl.ds(pl.multiple_of(i * blk + q0, tq), tq)
                dq_ref[rows, :] += dq

        @pl.when(i == j)
        def _():
            block(True)

        @pl.when(i > j)
        def _():
            block(False)

        @pl.when(i == nb - 1)
        def _():
            dk_out[...] = dk_ref[...].astype(BF16)
            dv_out[...] = dv_ref[...].astype(BF16)

        @pl.when(t == npair - 1)
        def _():
            dq_out[...] = dq_ref[...].astype(BF16)

    grid_spec = pltpu.PrefetchScalarGridSpec(
        num_scalar_prefetch=2, grid=(hh, npair),
        in_specs=[pl.BlockSpec((None, blk, MLA_QK), lambda h, t, it_, jt_: (h, it_[t], 0)),
                  pl.BlockSpec((None, blk, MLA_QK), lambda h, t, it_, jt_: (h, jt_[t], 0)),
                  pl.BlockSpec((None, blk, MLA_V), lambda h, t, it_, jt_: (h, jt_[t], 0)),
                  pl.BlockSpec((blk, MLA_V), lambda h, t, it_, jt_: (it_[t], h)),
                  pl.BlockSpec((None, 1, blk), lambda h, t, it_, jt_: (h, 0, it_[t])),
                  pl.BlockSpec((hh, blk), lambda h, t, it_, jt_: (0, it_[t]))] + r_in_specs,
        out_specs=[pl.BlockSpec((None, s, MLA_QK), lambda h, t, it_, jt_: (h, 0, 0)),
                   pl.BlockSpec((None, blk, MLA_QK), lambda h, t, it_, jt_: (h, jt_[t], 0)),
                   pl.BlockSpec((None, blk, MLA_V), lambda h, t, it_, jt_: (h, jt_[t], 0))] + r_out_specs,
        scratch_shapes=[pltpu.VMEM((s, MLA_QK), F32), pltpu.VMEM((blk, MLA_QK), F32), pltpu.VMEM((blk, MLA_V), F32)] + r_scr)
    out = pl.pallas_call(
        body, grid_spec=grid_spec,
        out_shape=[jax.ShapeDtypeStruct((hh, s, MLA_QK), BF16), jax.ShapeDtypeStruct((hh, s, MLA_QK), BF16),
                   jax.ShapeDtypeStruct((hh, s, MLA_V), BF16)] + r_out,
        compiler_params=_params("arbitrary", "arbitrary"), name=name)(it, jt, q, k, v, do, lse_row, delta_row, *r_in)
    return out[0], out[1], out[2], out[3:]


def _mla_bwd_mid(dq, dk, dv, cos, sin, proj, qn, kvn, w_uq, w_ukv, name):
    s = proj.shape[0]
    ts = _row_tile(s, 256)
    hh, half = MLA_HEADS, MLA_ROPE // 2
    nq, nkv = hh * MLA_QK, hh * (MLA_NOPE + MLA_V)

    def body(dq_ref, dk_ref, dv_ref, c_ref, s_ref, p_ref, qn_ref, kvn_ref, wq_ref, wkv_ref,
             dqe_ref, dkve_ref, dp_ref, dqn_ref, dkvn_ref):
        cos_, sin_ = c_ref[...], s_ref[...]
        dkr1 = jnp.zeros((ts, half), F32)
        dkr2 = jnp.zeros((ts, half), F32)
        for h in range(hh):
            dqh, dkh = dq_ref[h], dk_ref[h]
            b = h * MLA_QK
            dqe_ref[:, b:b + MLA_NOPE] = dqh[:, :MLA_NOPE].astype(BF16)
            d1, d2 = _rope_bwd(dqh[:, MLA_NOPE:MLA_NOPE + half], dqh[:, MLA_NOPE + half:], cos_, sin_)
            dqe_ref[:, b + MLA_NOPE:b + MLA_NOPE + half] = d1.astype(BF16)
            dqe_ref[:, b + MLA_NOPE + half:b + MLA_QK] = d2.astype(BF16)
            b = h * (MLA_NOPE + MLA_V)
            dkve_ref[:, b:b + MLA_NOPE] = dkh[:, :MLA_NOPE].astype(BF16)
            dkve_ref[:, b + MLA_NOPE:b + MLA_NOPE + MLA_V] = dv_ref[h].astype(BF16)
            dkr1 = dkr1 + dkh[:, MLA_NOPE:MLA_NOPE + half]
            dkr2 = dkr2 + dkh[:, MLA_NOPE + half:]
        dkr1, dkr2 = _rope_bwd(dkr1, dkr2, cos_, sin_)
        dcqn = _dot(dqe_ref[...], wq_ref[...], NT)
        dckvn = _dot(dkve_ref[...], wkv_ref[...], NT)
        p = p_ref[...]
        dcq, dqn = _rms_bwd_rows(p[:, :MLA_Q_LORA], qn_ref[...], dcqn)
        dckv, dkvn = _rms_bwd_rows(p[:, MLA_Q_LORA:MLA_Q_LORA + MLA_KV_LORA], kvn_ref[...], dckvn)
        dp_ref[:, :MLA_Q_LORA] = dcq.astype(BF16)
        dp_ref[:, MLA_Q_LORA:MLA_Q_LORA + MLA_KV_LORA] = dckv.astype(BF16)
        dp_ref[:, MLA_Q_LORA + MLA_KV_LORA:MLA_Q_LORA + MLA_KV_LORA + half] = dkr1.astype(BF16)
        dp_ref[:, MLA_Q_LORA + MLA_KV_LORA + half:] = dkr2.astype(BF16)

        @pl.when(pl.program_id(0) == 0)
        def _():
            dqn_ref[...] = jnp.zeros_like(dqn_ref)
            dkvn_ref[...] = jnp.zeros_like(dkvn_ref)

        dqn_ref[...] += jnp.sum(dqn, axis=0, keepdims=True)
        dkvn_ref[...] += jnp.sum(dkvn, axis=0, keepdims=True)

    def row(w):
        return pl.BlockSpec((ts, w), lambda i: (i, 0))

    def full(shape):
        return pl.BlockSpec(shape, lambda i: (0,) * len(shape))

    def heads(w):
        return pl.BlockSpec((hh, ts, w), lambda i: (0, i, 0))

    return pl.pallas_call(
        body, grid=(s // ts,),
        in_specs=[heads(MLA_QK), heads(MLA_QK), heads(MLA_V), row(half), row(half), row(MLA_IN),
                  full(qn.shape), full(kvn.shape), full(w_uq.shape), full(w_ukv.shape)],
        out_specs=(row(nq), row(nkv), row(MLA_IN), full(qn.shape), full(kvn.shape)),
        out_shape=(jax.ShapeDtypeStruct((s, nq), BF16), jax.ShapeDtypeStruct((s, nkv), BF16),
                   jax.ShapeDtypeStruct((s, MLA_IN), BF16), jax.ShapeDtypeStruct(qn.shape, F32),
                   jax.ShapeDtypeStruct(kvn.shape, F32)),
        compiler_params=_params("arbitrary"), name=name)(dq, dk, dv, cos, sin, proj, qn, kvn, w_uq, w_ukv)


HGRN_TILE = 256


def _chunk_masks(t):
    r = lax.broadcasted_iota(jnp.int32, (t, t), 0)
    c = lax.broadcasted_iota(jnp.int32, (t, t), 1)
    same = (r // HGRN_CHUNK) == (c // HGRN_CHUNK)
    return r, c, same


def _hgrn_gates(p, lb):
    hk = HGRN_HEADS * HGRN_D
    qx, fx, ix, gx = p[:, :hk], p[:, hk:2 * hk], p[:, 2 * hk:3 * hk], p[:, 3 * hk:]
    sig_f = jax.nn.sigmoid(fx)
    f = lb + (1.0 - lb) * sig_f
    sig_q = jax.nn.sigmoid(qx)
    t = p.shape[0]
    r, c, same = _chunk_masks(t)
    lower = jnp.where(same & (c <= r), 1.0, 0.0).astype(F32)
    b = _dot_select(lower, jnp.log(f), 3)
    b3 = b.reshape(t // HGRN_CHUNK, HGRN_CHUNK, hk)
    bref = jnp.broadcast_to(b3[:, HGRN_CHUNK // 2:HGRN_CHUNK // 2 + 1, :], b3.shape).reshape(t, hk)
    blast = jnp.broadcast_to(b3[:, HGRN_CHUNK - 1:, :], b3.shape).reshape(t, hk)
    return qx, ix, gx, sig_f, f, sig_q, b, bref, blast


def _hgrn_fwd(proj, lb, onorm, name, rider=None):
    s = proj.shape[0]
    t = _row_tile(s, HGRN_TILE)
    nc = t // HGRN_CHUNK
    hh, dd, hk = HGRN_HEADS, HGRN_D, HGRN_HEADS * HGRN_D
    r_in, r_in_specs, r_out, r_out_specs, r_scr = _rider_specs(rider)
    n_rin, n_rout = len(r_in), len(r_out)

    def body(p_ref, lb_ref, on_ref, *refs):
        y_ref, o_ref, st_ref = refs[n_rin:n_rin + 3]
        st_scr = refs[n_rin + 3 + n_rout]
        _ride(rider, pl.program_id(0), s // t, refs[:n_rin] + refs[n_rin + 3:n_rin + 3 + n_rout] + refs[n_rin + 4 + n_rout:])

        @pl.when(pl.program_id(0) == 0)
        def _():
            st_scr[...] = jnp.zeros_like(st_scr)

        qx, ix, gx, _, f, sig_q, b, bref, blast = _hgrn_gates(p_ref[...], lb_ref[...])
        q = qx * sig_q
        k = 1.0 - f
        r, c, same = _chunk_masks(t)
        causal = same & (c <= r)
        for h in range(hh):
            sl = slice(h * dd, (h + 1) * dd)
            bh, brefh, blasth, qh, kh = b[:, sl], bref[:, sl], blast[:, sl], q[:, sl], k[:, sl]
            vh = ix[:, sl].astype(BF16)
            q_rel = (qh * jnp.exp(bh - brefh)).astype(BF16)
            k_rel = (kh * jnp.exp(brefh - bh)).astype(BF16)
            a = jnp.where(causal, _dot(q_rel, k_rel, NT), 0.0)
            o_intra = _dot(a.astype(BF16), vh)
            q_dec = (qh * jnp.exp(bh)).astype(BF16)
            k_dec = (kh * jnp.exp(blasth - bh)).astype(BF16)
            dec = jnp.exp(blasth)
            pieces = []
            for ci in range(nc):
                rows = slice(ci * HGRN_CHUNK, (ci + 1) * HGRN_CHUNK)
                st = st_scr[h]
                if ci == 0:
                    st_ref[h] = st
                pieces.append(_dot(q_dec[rows], st.astype(BF16), NT))
                st_scr[h] = st * dec[ci * HGRN_CHUNK:ci * HGRN_CHUNK + 1, :] + _dot(vh[rows], k_dec[rows], TN)
            oh = o_intra + jnp.concatenate(pieces, axis=0)
            o_ref[:, sl] = oh
            gate = gx[:, sl] * jax.nn.sigmoid(gx[:, sl])
            y_ref[:, sl] = (oh * _rstd(oh) * on_ref[...] * gate).astype(BF16)

    out = pl.pallas_call(
        body, grid=(s // t,),
        in_specs=[pl.BlockSpec((t, 4 * hk), lambda i: (i, 0)), pl.BlockSpec((1, hk), lambda i: (0, 0)),
                  pl.BlockSpec((1, dd), lambda i: (0, 0))] + r_in_specs,
        out_specs=[pl.BlockSpec((t, hk), lambda i: (i, 0)), pl.BlockSpec((t, hk), lambda i: (i, 0)),
                   pl.BlockSpec((None, hh, dd, dd), lambda i: (i, 0, 0, 0))] + r_out_specs,
        out_shape=[jax.ShapeDtypeStruct((s, hk), BF16), jax.ShapeDtypeStruct((s, hk), F32),
                   jax.ShapeDtypeStruct((s // t, hh, dd, dd), F32)] + r_out,
        scratch_shapes=[pltpu.VMEM((hh, dd, dd), F32)] + r_scr,
        compiler_params=_params("arbitrary"), name=name)(proj, lb, onorm, *r_in)
    return out[0], out[1], out[2], out[3:]


def _hgrn_bwd(proj, lb, onorm, o, states, dy, name, rider=None):
    s = proj.shape[0]
    t = _row_tile(s, HGRN_TILE)
    nt = s // t
    nc = t // HGRN_CHUNK
    hh, dd, hk = HGRN_HEADS, HGRN_D, HGRN_HEADS * HGRN_D
    r_in, r_in_specs, r_out, r_out_specs, r_scr = _rider_specs(rider)
    n_rin, n_rout, n_rscr = len(r_in), len(r_out), len(r_scr)

    def body(p_ref, lb_ref, on_ref, o_ref, st_ref, dy_ref, *refs):
        r_refs = refs[:n_rin] + refs[n_rin + 3:n_rin + 3 + n_rout] + refs[len(refs) - n_rscr:]
        dp_ref, dlb_ref, don_ref = refs[n_rin:n_rin + 3]
        dst_scr, cat_scr, ext_scr, dk_scr, dq_scr = refs[n_rin + 3 + n_rout:n_rin + 3 + n_rout + 5]
        _ride(rider, pl.program_id(0), nt, r_refs)

        @pl.when(pl.program_id(0) == 0)
        def _():
            dst_scr[...] = jnp.zeros_like(dst_scr)
            dlb_ref[...] = jnp.zeros_like(dlb_ref)
            don_ref[...] = jnp.zeros_like(don_ref)

        lbv = lb_ref[...]
        qx, ix, gx, sig_f, f, sig_q, b, bref, blast = _hgrn_gates(p_ref[...], lbv)
        q = qx * sig_q
        k = 1.0 - f
        r, c, same = _chunk_masks(t)
        causal = same & (c <= r)
        on = on_ref[...]
        don = jnp.zeros((1, dd), F32)
        for h in range(hh):
            sl = slice(h * dd, (h + 1) * dd)
            oh = o_ref[:, sl]
            dyh = dy_ref[:, sl]
            gxh = gx[:, sl]
            sig_g = jax.nn.sigmoid(gxh)
            rs = _rstd(oh)
            dgate = dyh * (oh * rs * on)
            dp_ref[:, 3 * hk + h * dd:3 * hk + (h + 1) * dd] = (dgate * (sig_g * (1.0 + gxh * (1.0 - sig_g)))).astype(BF16)
            do, donh = _rms_bwd_rows(oh, on, dyh * (gxh * sig_g))
            don = don + jnp.sum(donh, axis=0, keepdims=True)
            dob = do.astype(BF16)
            bh, brefh, blasth, qh, kh = b[:, sl], bref[:, sl], blast[:, sl], q[:, sl], k[:, sl]
            vh = ix[:, sl].astype(BF16)
            e_qr, e_kr, e_qd, e_kd = jnp.exp(bh - brefh), jnp.exp(brefh - bh), jnp.exp(bh), jnp.exp(blasth - bh)
            dec = jnp.exp(blasth)
            q_rel, k_rel, q_dec, k_dec = qh * e_qr, kh * e_kr, qh * e_qd, kh * e_kd
            q_relb, k_relb, q_decb, k_decb = q_rel.astype(BF16), k_rel.astype(BF16), q_dec.astype(BF16), k_dec.astype(BF16)
            a = jnp.where(causal, _dot(q_relb, k_relb, NT), 0.0).astype(BF16)
            dv = _dot(a, dob, TN)
            da = jnp.where(causal, _dot(dob, vh, NT), 0.0).astype(BF16)
            dq_rel = _dot(da, k_relb)
            dk_rel = _dot(da, q_relb, TN)
            sts = [st_ref[h]]
            for ci in range(nc - 1):
                rows = slice(ci * HGRN_CHUNK, (ci + 1) * HGRN_CHUNK)
                sts.append(sts[-1] * dec[ci * HGRN_CHUNK:ci * HGRN_CHUNK + 1, :] + _dot(vh[rows], k_decb[rows], TN))
            dq_dec, dk_dec, dv_inter, ddec = [None] * nc, [None] * nc, [None] * nc, [None] * nc
            for ci in range(nc - 1, -1, -1):
                rows = slice(ci * HGRN_CHUNK, (ci + 1) * HGRN_CHUNK)
                st = sts[ci]
                dst = dst_scr[h]
                dstb = dst.astype(BF16)
                dq_dec[ci] = _dot(dob[rows], st.astype(BF16))
                dk_dec[ci] = _dot(vh[rows], dstb)
                dv_inter[ci] = _dot(k_decb[rows], dstb, NT)
                ddec[ci] = jnp.broadcast_to(jnp.sum(dst * st, axis=0, keepdims=True), (HGRN_CHUNK, dd))
                dst_scr[h] = dst * dec[ci * HGRN_CHUNK:ci * HGRN_CHUNK + 1, :] + _dot(dob[rows], q_decb[rows], TN)
            dq_dec = jnp.concatenate(dq_dec, axis=0)
            dk_dec = jnp.concatenate(dk_dec, axis=0)
            dv = dv + jnp.concatenate(dv_inter, axis=0)
            ddec = jnp.concatenate(ddec, axis=0)
            dp_ref[:, 2 * hk + h * dd:2 * hk + (h + 1) * dd] = dv.astype(BF16)
            dq_scr[:, sl] = dq_rel * e_qr + dq_dec * e_qd
            dk_scr[:, sl] = dk_rel * e_kr + dk_dec * e_kd
            g_qr, g_kr, g_qd, g_kd = dq_rel * q_rel, dk_rel * k_rel, dq_dec * q_dec, dk_dec * k_dec
            cat_scr[0:t, sl] = g_qr - g_kr + g_qd - g_kd
            cat_scr[t:2 * t, sl] = g_kr - g_qr
            cat_scr[2 * t:3 * t, sl] = g_kd
            ext_scr[:, sl] = ddec * dec
        upper = jnp.where(same & (c >= r), 1.0, 0.0).astype(F32)
        to_ref = jnp.where(same & (r % HGRN_CHUNK <= HGRN_CHUNK // 2), 1.0, 0.0).astype(F32)
        to_all = jnp.where(same, 1.0, 0.0).astype(F32)
        dlogf = _dot_select(jnp.concatenate([upper, to_ref, to_all], axis=1), cat_scr[...], 2) + ext_scr[...]
        df = dlogf / f - dk_scr[...]
        dp_ref[:, hk:2 * hk] = (df * (1.0 - lbv) * sig_f * (1.0 - sig_f)).astype(BF16)
        dp_ref[:, 0:hk] = (dq_scr[...] * (sig_q * (1.0 + qx * (1.0 - sig_q)))).astype(BF16)
        dlb_ref[...] += jnp.sum(df * (1.0 - sig_f), axis=0, keepdims=True)
        don_ref[...] += don

    def rev(i):
        return nt - 1 - i

    out = pl.pallas_call(
        body, grid=(nt,),
        in_specs=[pl.BlockSpec((t, 4 * hk), lambda i: (rev(i), 0)), pl.BlockSpec((1, hk), lambda i: (0, 0)),
                  pl.BlockSpec((1, dd), lambda i: (0, 0)), pl.BlockSpec((t, hk), lambda i: (rev(i), 0)),
                  pl.BlockSpec((None, hh, dd, dd), lambda i: (rev(i), 0, 0, 0)),
                  pl.BlockSpec((t, hk), lambda i: (rev(i), 0))] + r_in_specs,
        out_specs=[pl.BlockSpec((t, 4 * hk), lambda i: (rev(i), 0)), pl.BlockSpec((1, hk), lambda i: (0, 0)),
                   pl.BlockSpec((1, dd), lambda i: (0, 0))] + r_out_specs,
        out_shape=[jax.ShapeDtypeStruct((s, 4 * hk), BF16), jax.ShapeDtypeStruct((1, hk), F32),
                   jax.ShapeDtypeStruct((1, dd), F32)] + r_out,
        scratch_shapes=[pltpu.VMEM((hh, dd, dd), F32), pltpu.VMEM((3 * t, hk), F32), pltpu.VMEM((t, hk), F32),
                        pltpu.VMEM((t, hk), F32), pltpu.VMEM((t, hk), F32)] + r_scr,
        compiler_params=_params("arbitrary"), name=name)(proj, lb, onorm, o, states, dy, *r_in)
    return out[0], out[1], out[2], out[3:]


def _adamw_update(w, g, m, v):
    nm = ADAM_B1 * m + (1.0 - ADAM_B1) * g
    nv = ADAM_B2 * v + (1.0 - ADAM_B2) * (g * g)
    m_hat = nm / (1.0 - ADAM_B1 ** ADAM_STEP)
    v_hat = nv / (1.0 - ADAM_B2 ** ADAM_STEP)
    return -ADAM_LR * (m_hat / (jnp.sqrt(v_hat) + ADAM_EPS) + ADAM_WD * w), nm, nv


def _adamw(w, g, m, v, name):
    rows, cols = w.shape
    tr = _divisor_tile(rows, 256, 8)

    def body(w_ref, g_ref, m_ref, v_ref, d_ref, nm_ref, nv_ref):
        d_ref[...], nm_ref[...], nv_ref[...] = _adamw_update(w_ref[...], g_ref[...], m_ref[...], v_ref[...])

    blk = pl.BlockSpec((tr, cols), lambda i: (i, 0))
    shp = jax.ShapeDtypeStruct((rows, cols), F32)
    return pl.pallas_call(
        body, grid=(rows // tr,), in_specs=[blk] * 4, out_specs=(blk,) * 3, out_shape=(shp,) * 3,
        compiler_params=_params("parallel"), name=name)(w, g, m, v)


ADAMW_BLOCK_ELEMS = 128 * 1024


def _adamw_layers(w, lands, m, v, name, rider=None):
    ll, rows, cols = w.shape
    tr = _divisor_tile(rows, max(16, ADAMW_BLOCK_ELEMS // cols), 16)
    r_in, r_in_specs, r_out, r_out_specs, r_scr = _rider_specs(rider)
    n_rin = len(r_in)

    def body(w_ref, m_ref, v_ref, *refs):
        land_refs = refs[:ll]
        g_out, d_ref, nm_ref, nv_ref = refs[ll + n_rin:ll + n_rin + 4]
        layer = pl.program_id(0)
        _ride(rider, layer * (rows // tr) + pl.program_id(1), ll * (rows // tr), refs[ll:ll + n_rin] + refs[ll + n_rin + 4:])
        for k in range(ll):
            @pl.when(layer == k)
            def _(k=k):
                g = land_refs[k][0].astype(F32)
                for slot in range(1, N_DEV):
                    g = g + land_refs[k][slot].astype(F32)
                g_out[...] = g

        d_ref[...], nm_ref[...], nv_ref[...] = _adamw_update(w_ref[...], g_out[...], m_ref[...], v_ref[...])

    stacked = pl.BlockSpec((None, tr, cols), lambda l, i: (l, i, 0))

    def one(k):
        return pl.BlockSpec((N_DEV, tr, cols), lambda l, i: (0, jnp.where(l == k, i, 0), 0))

    shp = jax.ShapeDtypeStruct(w.shape, F32)
    out = pl.pallas_call(
        body, grid=(ll, rows // tr), in_specs=[stacked] * 3 + [one(k) for k in range(ll)] + r_in_specs,
        out_specs=[stacked] * 4 + r_out_specs, out_shape=[shp] * 4 + r_out, scratch_shapes=r_scr,
        compiler_params=_params("arbitrary", "arbitrary"), name=name)(w, m, v, *lands, *r_in)
    return out[0], out[1], out[2], out[3], out[4:]


_HBM = pl.BlockSpec(memory_space=pltpu.HBM)
_MESH = pl.DeviceIdType.MESH


class _GatherRide:
    def __init__(self, blocks, cuts):
        self.operands = list(blocks)
        self.cuts = list(cuts)
        self.out_shapes = []
        for b, cut in zip(blocks, cuts):
            r, c = b.shape
            shape = {"rows": (N_DEV * r, c), "cols": (r, N_DEV * c), "slots": (N_DEV, r, c)}[cut]
            self.out_shapes.append(jax.ShapeDtypeStruct(shape, b.dtype))
        n = len(blocks)
        self.scratch = [pltpu.SemaphoreType.DMA((7 * n,)), pltpu.SemaphoreType.DMA((7 * n,)), pltpu.SemaphoreType.DMA((n,))]

    def _parts(self, *refs):
        n = len(self.operands)
        x_refs, out_refs = refs[:n], refs[n:2 * n]
        send_sems, recv_sems, local_sems = refs[2 * n:]
        x, y, c = lax.axis_index("x"), lax.axis_index("y"), lax.axis_index("c")
        me, sibling = (x, y, c), (x, y, 1 - c)
        chips = [(1 - x, y), (x, 1 - y), (1 - x, 1 - y)]
        mine, first, passed, landed, from_sibling = [], [], [], [], []
        for e in range(n):
            x_ref, out_ref, cut = x_refs[e], out_refs[e], self.cuts[e]
            r, cc = x_ref.shape

            def place(px, py, pc, out_ref=out_ref, cut=cut, r=r, cc=cc):
                p = 4 * px + 2 * py + pc
                if cut == "rows":
                    return out_ref.at[pl.ds(pl.multiple_of(p * r, r), r), :]
                if cut == "cols":
                    return out_ref.at[:, pl.ds(pl.multiple_of(p * cc, cc), cc)]
                return out_ref.at[p]

            def copy(k, block, to, src=None, place=place, e=e):
                return pltpu.make_async_remote_copy(
                    src_ref=place(*block) if src is None else src, dst_ref=place(*block), send_sem=send_sems.at[7 * e + k],
                    recv_sem=recv_sems.at[7 * e + k], device_id=to, device_id_type=_MESH)

            mine.append(pltpu.make_async_copy(x_ref, place(*me), local_sems.at[e]))
            first += [copy(0, me, sibling, src=x_ref)] + [copy(1 + j, me, (*chip, c), src=x_ref) for j, chip in enumerate(chips)]
            passed += [copy(4 + j, (*chip, c), sibling) for j, chip in enumerate(chips)]
            landed += [copy(1 + j, (*chip, c), me) for j, chip in enumerate(chips)]
            from_sibling += [copy(0, sibling, me)] + [copy(4 + j, (*chip, 1 - c), me) for j, chip in enumerate(chips)]
        return mine, first, passed, landed, from_sibling

    def start(self, *refs):
        mine, first, _, _, _ = self._parts(*refs)
        for cp in mine + first:
            cp.start()

    def middle(self, *refs):
        _, _, passed, landed, _ = self._parts(*refs)
        for got, fwd in zip(landed, passed):
            got.wait_recv()
            fwd.start()

    def finish(self, *refs):
        mine, first, passed, _, from_sibling = self._parts(*refs)
        for cp in from_sibling:
            cp.wait_recv()
        for cp in first + passed:
            cp.wait_send()
        for cp in mine:
            cp.wait()


class _ExchangeRide:
    def __init__(self, sends):
        self.operands = list(sends)
        self.out_shapes = [jax.ShapeDtypeStruct(s.shape, s.dtype) for s in sends]
        n = len(sends)
        self.scratch = [pltpu.SemaphoreType.DMA((7 * n,)), pltpu.SemaphoreType.DMA((7 * n,)), pltpu.SemaphoreType.DMA((n,))]

    def _parts(self, *refs):
        n = len(self.operands)
        s_refs, land_refs = refs[:n], refs[n:2 * n]
        send_sems, recv_sems, local_sems = refs[2 * n:]
        x, y, c = lax.axis_index("x"), lax.axis_index("y"), lax.axis_index("c")
        me = 4 * x + 2 * y + c
        own, sends, recvs = [], [], []
        for e in range(n):
            s_ref, land_ref = s_refs[e], land_refs[e]
            own.append(pltpu.make_async_copy(s_ref.at[me], land_ref.at[me], local_sems.at[e]))
            for rel in range(1, N_DEV):
                px = 1 - x if rel & 4 else x
                py = 1 - y if rel & 2 else y
                pc = 1 - c if rel & 1 else c
                peer = 4 * px + 2 * py + pc
                k = 7 * e + rel - 1
                sends.append(pltpu.make_async_remote_copy(
                    src_ref=s_ref.at[peer], dst_ref=land_ref.at[me], send_sem=send_sems.at[k], recv_sem=recv_sems.at[k],
                    device_id=(px, py, pc), device_id_type=_MESH))
                recvs.append(pltpu.make_async_remote_copy(
                    src_ref=s_ref.at[me], dst_ref=land_ref.at[peer], send_sem=send_sems.at[k], recv_sem=recv_sems.at[k],
                    device_id=(px, py, pc), device_id_type=_MESH))
        return own, sends, recvs

    def start(self, *refs):
        own, sends, _ = self._parts(*refs)
        for cp in own + sends:
            cp.start()

    def middle(self, *refs):
        pass

    def finish(self, *refs):
        own, sends, recvs = self._parts(*refs)
        for cp in recvs:
            cp.wait_recv()
        for cp in sends:
            cp.wait_send()
        for cp in own:
            cp.wait()


def _run_alone(rider, name):
    def body(*refs):
        rider.start(*refs)
        rider.middle(*refs)
        rider.finish(*refs)

    return pl.pallas_call(
        body, out_shape=rider.out_shapes, in_specs=[_HBM] * len(rider.operands), out_specs=[_HBM] * len(rider.out_shapes),
        scratch_shapes=rider.scratch, name=name)(*rider.operands)


def _all_gather(xs, name):
    return _run_alone(_GatherRide([xs], ["slots"]), name)[0]


def _sum_slots(parts, name):
    _, rows, cols = parts.shape
    tr = _divisor_tile(rows, 256, 16)

    def body(p_ref, o_ref):
        acc = p_ref[0].astype(F32)
        for slot in range(1, N_DEV):
            acc = acc + p_ref[slot].astype(F32)
        o_ref[...] = acc

    return pl.pallas_call(
        body, grid=(rows // tr,), in_specs=[pl.BlockSpec((N_DEV, tr, cols), lambda i: (0, i, 0))],
        out_specs=pl.BlockSpec((tr, cols), lambda i: (i, 0)), out_shape=jax.ShapeDtypeStruct((rows, cols), F32),
        compiler_params=_params("parallel"), name=name)(parts)


def _carry(rode, key, riders, call):
    rider = riders.get(key)
    res = call(rider)
    if rider is None:
        return res
    res, rode[key] = res
    return res


def _kept(rode, key, riders, brought):
    if key in riders:
        rode[key] = brought


def _mlp_fwd(h, g_pre, g_post, w1, w2, tag, riders):
    rode = {}
    tm = _row_tile(h.shape[0], 2048)
    (a, r2), brought = _norm_mm(h, g_pre, w1, tm, 1024, f"{tag}_up", out_dtypes=(BF16,),
                                epi=lambda acc: (jnp.square(jnp.maximum(acc, 0.0)),), rider=riders.get("up"))
    _kept(rode, "up", riders, brought)
    z, out, brought = _mm_norm_res(r2, w2, g_post, h, 512, f"{tag}_down", riders.get("down"))
    _kept(rode, "down", riders, brought)
    return out, (h, a, r2, z), rode


def _mlp_bwd(dh, saved, g_pre, g_post, w1, w2, tag, riders):
    h, a, r2, z = saved
    rode = {}
    tm = _row_tile(h.shape[0], 1024)
    (dz, dg_post, du), brought = _rmsbwd_mm(
        z, g_post, dh, w2, tm, 1024, f"{tag}_ddown", out_dtypes=(BF16,), extras=(r2,),
        epi=lambda acc, rr: (acc * (2.0 * jnp.sqrt(rr.astype(F32))),), rider=riders.get("ddown"))
    _kept(rode, "ddown", riders, brought)
    dw2 = _carry(rode, "dw2", riders, lambda r: _mm(
        r2, dz, "tn", 512, 1024, f"{tag}_dw2", out_dtypes=(BF16,), shard="rows", rider=r))
    dw1 = _mm(a, du, "tn", 1024, 512, f"{tag}_dw1", out_dtypes=(BF16,), shard="cols")
    dh_in, dg_pre, _ = _mm_rmsbwd_res(du, w1, h, g_pre, dh, 512, f"{tag}_dup")
    return dh_in, dg_pre, dg_post, dw1, dw2, rode


def _hgrn_layer_fwd(h, g_pre, g_post, lb, onorm, w_in, w_o, tag, riders):
    rode = {}
    (a, proj), brought = _norm_mm(h, g_pre, w_in, _row_tile(h.shape[0], 2048), 1024, f"{tag}_in", rider=riders.get("in"))
    _kept(rode, "in", riders, brought)
    y, o, states, brought = _hgrn_fwd(proj, lb, onorm, f"{tag}_scan", riders.get("scan"))
    _kept(rode, "scan", riders, brought)
    m, out, _ = _mm_norm_res(y, w_o, g_post, h, 512, f"{tag}_o")
    return out, (h, a, proj, y, o, states, m), rode


def _hgrn_layer_bwd(dh, saved, g_pre, g_post, lb, onorm, w_in, w_o, tag, rider=None):
    h, a, proj, y, o, states, m = saved
    (dm, dg_post, dy), _ = _rmsbwd_mm(m, g_post, dh, w_o, 512, 1024, f"{tag}_do")
    dw_o = _mm(y, dm, "tn", 128, 1024, f"{tag}_dwo", out_dtypes=(BF16,), shard="rows")
    dproj, dlb, donorm, rode = _hgrn_bwd(proj, lb, onorm, o, states, dy, f"{tag}_dscan", rider)
    dw_in = _mm(a, dproj, "tn", 1024, 512, f"{tag}_dwin", out_dtypes=(BF16,), shard="cols")
    dh_in, dg_pre, _ = _mm_rmsbwd_res(dproj, w_in, h, g_pre, dh, 512, f"{tag}_din")
    return dh_in, dg_pre, dg_post, dlb, donorm, dw_in, dw_o, rode


def _mla_layer_fwd(h, g_pre, g_post, cos, sin, w_in, late, qn, kvn, tag, riders):
    rode = {}
    (a, proj), brought = _norm_mm(h, g_pre, w_in, 512, MLA_IN, f"{tag}_in", rider=riders.get("in"))
    _kept(rode, "in", riders, brought)
    w_uq, w_ukv, _ = late(rode)
    cqn, ckvn, q, k, v = _mla_qkv(proj, qn, kvn, w_uq, w_ukv, cos, sin, f"{tag}_qkv")
    o, lse, brought = _attn_fwd(q, k, v, f"{tag}_attn", riders.get("attn"))
    _kept(rode, "attn", riders, brought)
    _, _, w_o = late(rode)
    m, out, _ = _mm_norm_res(o, w_o, g_post, h, 512, f"{tag}_o")
    return out, (h, a, proj, cqn, ckvn, q, k, v, o, lse, m), rode


def _mla_layer_bwd(dh, saved, g_pre, g_post, cos, sin, w_in, qn, kvn, w_uq, w_ukv, w_o, tag, rider=None, own_ride=False):
    h, a, proj, cqn, ckvn, q, k, v, o, lse, m = saved
    hh, s = q.shape[0], q.shape[1]
    own = {}
    (dm, dg_post, do, delta), _ = _rmsbwd_mm(m, g_post, dh, w_o, 512, 1024, f"{tag}_do", group_sums=(o, hh))
    dw_o = _mm(o, dm, "tn", 128, 1024, f"{tag}_dwo", out_dtypes=(BF16,), shard="rows")
    if own_ride:
        rider = _ExchangeRide(list(rider.operands) + [dw_o])
    dq, dk, dv, rode = _attn_bwd(q, k, v, do, lse.reshape(hh, 1, s), delta, f"{tag}_dattn", rider)
    if own_ride:
        own["mla_w_o"] = rode[-1]
    dqe, dkve, dproj, dqn, dkvn = _mla_bwd_mid(dq, dk, dv, cos, sin, proj, qn, kvn, w_uq, w_ukv, f"{tag}_dqkv")
    dw_uq = _mm(cqn, dqe, "tn", MLA_Q_LORA, 768, f"{tag}_dwuq", out_dtypes=(BF16,))
    dw_uq = dw_uq.reshape(MLA_Q_LORA, N_DEV, -1).transpose(1, 0, 2)
    dw_ukv = _mm(ckvn, dkve, "tn", MLA_KV_LORA, 256, f"{tag}_dwukv", out_dtypes=(BF16,), shard="cols")
    dw_in = _mm(a, dproj, "tn", 128, MLA_IN, f"{tag}_dwin", out_dtypes=(BF16,), shard="rows",
                rider=_ExchangeRide([dw_uq, dw_ukv]) if own_ride else None)
    if own_ride:
        dw_in, (own["mla_w_uq"], own["mla_w_ukv"]) = dw_in
    dh_in, dg_pre, brought = _mm_rmsbwd_res(dproj, w_in, h, g_pre, dh, 512, f"{tag}_din",
                                             _ExchangeRide([dw_in]) if own_ride else None)
    if own_ride:
        own["mla_w_in"] = brought[0]
    return dh_in, dg_pre, dg_post, dqn, dkvn, dw_in, dw_uq, dw_ukv, dw_o, rode, own


_CUT = dict(mla_w_in="rows", mla_w_uq="cols", mla_w_ukv="cols", mla_w_o="rows", hgrn_w_in="cols", hgrn_w_o="rows",
            mlp_w1="cols", mlp_w2="rows")


def _unit(layer, kind):
    slot = layer // 2
    if kind == "mla":
        return [("mla_w_in", slot), ("mla_w_uq", slot), ("mla_w_ukv", slot), ("mla_w_o", slot)]
    if kind == "hgrn":
        return [("hgrn_w_in", slot), ("hgrn_w_o", slot)]
    return [("mlp_w1", layer), ("mlp_w2", layer)]


_GATHER_FIRST = [("mla_w_in", 0)]
_GATHER_PLAN = {
    (0, "in"): [("mla_w_uq", 0), ("mla_w_ukv", 0)],
    (0, "attn"): [("mla_w_o", 0)] + _unit(0, "mlp") + _unit(1, "hgrn"),
    (0, "up"): [("mlp_w1", 1)],
    (0, "down"): [("mlp_w2", 1)],
    (1, "in"): _unit(2, "mla"),
    (1, "scan"): _unit(2, "mlp"),
    (2, "attn"): _unit(3, "hgrn") + _unit(3, "mlp"),
}
_EXCHANGE_PLAN = {
    (3, "dscan"): _unit(3, "mlp"),
    (2, "ddown"): [("hgrn_w_in", 1)],
    (2, "dw2"): [("hgrn_w_o", 1)],
    (2, "dattn"): _unit(2, "mlp"),
    (1, "ddown"): _unit(2, "mla"),
    (1, "dscan"): _unit(1, "mlp"),
    (0, "ddown"): [("hgrn_w_in", 0)],
    (0, "dw2"): [("hgrn_w_o", 0)],
    (0, "dattn"): _unit(0, "mlp"),
}


def _gather_cut(name):
    return "slots" if name == "mla_w_uq" else _CUT[name]


def _gather_rider(weights, ents):
    return _GatherRide([weights[name][idx].astype(BF16) for name, idx in ents], [_gather_cut(name) for name, _ in ents])


def _gathered(outs, ents):
    res = {}
    for (name, idx), out in zip(ents, outs):
        if _gather_cut(name) == "slots":
            out = out.transpose(1, 0, 2).reshape(out.shape[1], -1)
        res[(name, idx)] = out
    return res


def _adamw_nd(w, g, m, v, name):
    shape = w.shape
    c = shape[-1]
    d, nm, nv = _adamw(w.reshape(-1, c), g.reshape(-1, c), m.reshape(-1, c), v.reshape(-1, c), name)
    return d.reshape(shape), nm.reshape(shape), nv.reshape(shape)


def kernel(x, positions, norm_gains, mla_w_in, mla_q_norm, mla_kv_norm, mla_w_uq, mla_w_ukv, mla_w_o, hgrn_w_in, hgrn_lb_logits, hgrn_o_norm, hgrn_w_o, mlp_w1, mlp_w2, loss_target, m_norm_gains, m_mla_w_in, m_mla_q_norm, m_mla_kv_norm, m_mla_w_uq, m_mla_w_ukv, m_mla_w_o, m_hgrn_w_in, m_hgrn_lb_logits, m_hgrn_o_norm, m_hgrn_w_o, m_mlp_w1, m_mlp_w2, v_norm_gains, v_mla_w_in, v_mla_q_norm, v_mla_kv_norm, v_mla_w_uq, v_mla_w_ukv, v_mla_w_o, v_hgrn_w_in, v_hgrn_lb_logits, v_hgrn_o_norm, v_hgrn_w_o, v_mlp_w1, v_mlp_w2):
    weights = dict(norm_gains=norm_gains, mla_w_in=mla_w_in, mla_q_norm=mla_q_norm, mla_kv_norm=mla_kv_norm,
                   mla_w_uq=mla_w_uq, mla_w_ukv=mla_w_ukv, mla_w_o=mla_w_o, hgrn_w_in=hgrn_w_in,
                   hgrn_lb_logits=hgrn_lb_logits, hgrn_o_norm=hgrn_o_norm, hgrn_w_o=hgrn_w_o, mlp_w1=mlp_w1, mlp_w2=mlp_w2)
    mom_m = dict(norm_gains=m_norm_gains, mla_w_in=m_mla_w_in, mla_q_norm=m_mla_q_norm, mla_kv_norm=m_mla_kv_norm,
                 mla_w_uq=m_mla_w_uq, mla_w_ukv=m_mla_w_ukv, mla_w_o=m_mla_w_o, hgrn_w_in=m_hgrn_w_in,
                 hgrn_lb_logits=m_hgrn_lb_logits, hgrn_o_norm=m_hgrn_o_norm, hgrn_w_o=m_hgrn_w_o, mlp_w1=m_mlp_w1, mlp_w2=m_mlp_w2)
    mom_v = dict(norm_gains=v_norm_gains, mla_w_in=v_mla_w_in, mla_q_norm=v_mla_q_norm, mla_kv_norm=v_mla_kv_norm,
                 mla_w_uq=v_mla_w_uq, mla_w_ukv=v_mla_w_ukv, mla_w_o=v_mla_w_o, hgrn_w_in=v_hgrn_w_in,
                 hgrn_lb_logits=v_hgrn_lb_logits, hgrn_o_norm=v_hgrn_o_norm, hgrn_w_o=v_hgrn_w_o, mlp_w1=v_mlp_w1, mlp_w2=v_mlp_w2)
    order = list(weights)
    seq = x.shape[1]
    h = x.reshape(seq, D_MODEL)
    target = loss_target.reshape(seq, D_MODEL)

    full = _gathered(_run_alone(_gather_rider(weights, _GATHER_FIRST), "gather_first"), _GATHER_FIRST)
    gains = _all_gather(norm_gains.reshape(DEPTH * 4, D_MODEL // N_DEV), "gather_gains")
    gains = gains.transpose(1, 0, 2).reshape(DEPTH, 4, 1, D_MODEL)

    def gather_riders(layer, keys):
        return {k: _gather_rider(weights, _GATHER_PLAN[(layer, k)]) for k in keys if (layer, k) in _GATHER_PLAN}

    seen = set()

    def arrived(layer, rode):
        for k, outs in rode.items():
            if (layer, k) not in seen:
                seen.add((layer, k))
                full.update(_gathered(outs, _GATHER_PLAN[(layer, k)]))

    def late_weights(layer):
        def late(rode):
            arrived(layer, rode)
            return tuple(full.get((name, layer // 2)) for name in ("mla_w_uq", "mla_w_ukv", "mla_w_o"))
        return late

    cos, sin = _rope_tables(positions.reshape(seq, 1), "rope_tables")
    lower = _lb_fwd(hgrn_lb_logits, "lower_bounds")

    def mixer_args(layer):
        slot = layer // 2
        if layer % 2 == 0:
            return (cos, sin, full[("mla_w_in", slot)], mla_q_norm[slot:slot + 1], mla_kv_norm[slot:slot + 1],
                    full[("mla_w_uq", slot)], full[("mla_w_ukv", slot)], full[("mla_w_o", slot)])
        return (lower[layer:layer + 1], hgrn_o_norm[slot:slot + 1], full[("hgrn_w_in", slot)], full[("hgrn_w_o", slot)])

    saved = []
    for layer in range(DEPTH):
        g = gains[layer]
        if layer % 2 == 0:
            slot = layer // 2
            h, sv_mix, rode = _mla_layer_fwd(
                h, g[0], g[1], cos, sin, full[("mla_w_in", slot)], late_weights(layer), mla_q_norm[slot:slot + 1],
                mla_kv_norm[slot:slot + 1], f"l{layer}_mla", gather_riders(layer, ["in", "attn"]))
            arrived(layer, rode)
        else:
            h, sv_mix, rode = _hgrn_layer_fwd(h, g[0], g[1], *mixer_args(layer), f"l{layer}_hgrn",
                                              gather_riders(layer, ["in", "scan"]))
            arrived(layer, rode)
        h, sv_mlp, rode = _mlp_fwd(h, g[2], g[3], full[("mlp_w1", layer)], full[("mlp_w2", layer)], f"l{layer}_mlp",
                                   gather_riders(layer, ["up", "down"]))
        arrived(layer, rode)
        saved.append((sv_mix, sv_mlp))

    loss_part, dh = _loss(h, target, "loss")
    loss = lax.psum(loss_part[0, 0], AXES)

    zero_row = jnp.zeros((1, D_MODEL), F32)
    dgains = [[None] * 4 for _ in range(DEPTH)]
    dlower = [zero_row] * DEPTH
    partials, lands = {}, {}
    dqn, dkvn, donorm = [None] * 2, [None] * 2, [None] * 2

    def exchange_riders(layer, keys):
        return {k: _ExchangeRide([partials[e] for e in _EXCHANGE_PLAN[(layer, k)]]) for k in keys
                if (layer, k) in _EXCHANGE_PLAN}

    def landed(layer, rode):
        for k, outs in rode.items():
            lands.update(zip(_EXCHANGE_PLAN[(layer, k)], outs))

    for layer in range(DEPTH - 1, -1, -1):
        slot = layer // 2
        g = gains[layer]
        sv_mix, sv_mlp = saved[layer]
        dh, dgains[layer][2], dgains[layer][3], partials[("mlp_w1", layer)], partials[("mlp_w2", layer)], rode = _mlp_bwd(
            dh, sv_mlp, g[2], g[3], full[("mlp_w1", layer)], full[("mlp_w2", layer)], f"l{layer}_mlp",
            exchange_riders(layer, ["ddown", "dw2"]))
        landed(layer, rode)
        key = "dattn" if layer % 2 == 0 else "dscan"
        rider = exchange_riders(layer, [key]).get(key)
        if layer % 2 == 0:
            (dh, dgains[layer][0], dgains[layer][1], dqn[slot], dkvn[slot], partials[("mla_w_in", slot)],
             partials[("mla_w_uq", slot)], partials[("mla_w_ukv", slot)], partials[("mla_w_o", slot)], brought,
             own) = _mla_layer_bwd(dh, sv_mix, g[0], g[1], *mixer_args(layer), f"l{layer}_mla", rider, own_ride=(layer == 0))
            lands.update({(name, slot): land for name, land in own.items()})
        else:
            (dh, dgains[layer][0], dgains[layer][1], dlower[layer], donorm[slot], partials[("hgrn_w_in", slot)],
             partials[("hgrn_w_o", slot)], brought) = _hgrn_layer_bwd(dh, sv_mix, g[0], g[1], *mixer_args(layer), f"l{layer}_hgrn", rider)
        landed(layer, {key: brought} if rider is not None else {})
    grad_x = dh.reshape(x.shape)
    dlogits = _lb_bwd(hgrn_lb_logits, jnp.concatenate(dlower, axis=0), "lower_bounds_bwd")

    pad = jnp.zeros((1, D_MODEL - 2 * MLA_KV_LORA), F32)
    pad2 = jnp.zeros((1, D_MODEL - 2 * HGRN_D), F32)
    small = jnp.concatenate(
        [jnp.concatenate([gg for row in dgains for gg in row], axis=0), jnp.concatenate(dqn, axis=1),
         jnp.concatenate(dkvn + [pad], axis=1), dlogits, jnp.concatenate(donorm + [pad2], axis=1), zero_row], axis=0)
    small = _sum_slots(_all_gather(small, "gather_small_grads"), "sum_small_grads")
    me = 4 * lax.axis_index("x") + 2 * lax.axis_index("y") + lax.axis_index("c")
    n_g = DEPTH * 4
    width = D_MODEL // N_DEV
    grads = {}
    grads["norm_gains"] = lax.dynamic_slice(small[:n_g], (0, me * width), (n_g, width)).reshape(DEPTH, 4, width)
    grads["mla_q_norm"] = small[n_g].reshape(2, MLA_Q_LORA)
    grads["mla_kv_norm"] = small[n_g + 1, :2 * MLA_KV_LORA].reshape(2, MLA_KV_LORA)
    grads["hgrn_lb_logits"] = small[n_g + 2:n_g + 2 + DEPTH]
    grads["hgrn_o_norm"] = small[n_g + 2 + DEPTH, :2 * HGRN_D].reshape(2, HGRN_D)

    deltas, new_m, new_v = {}, {}, {}
    for name in order:
        if name in _CUT:
            per_layer = [lands[(name, idx)] for idx in range(weights[name].shape[0])]
            grads[name], deltas[name], new_m[name], new_v[name], _ = _adamw_layers(
                weights[name], per_layer, mom_m[name], mom_v[name], f"adamw_{name}")
        else:
            deltas[name], new_m[name], new_v[name] = _adamw_nd(weights[name], grads[name], mom_m[name], mom_v[name], f"adamw_{name}")
    return (loss, grad_x, *[grads[n] for n in order], *[deltas[n] for n in order], *[new_m[n] for n in order],
            *[new_v[n] for n in order])
```

```python
import numpy as np
import jax
import jax.numpy as jnp
from jax import lax
from jax.experimental import pallas as pl
from jax.experimental.pallas import tpu as pltpu

F32, BF16 = jnp.float32, jnp.bfloat16

N_DEV = 8
AXES = ("x", "y", "c")
D_MODEL = 1024
DEPTH = 4
MLA_HEADS = 8
MLA_Q_LORA = 512
MLA_KV_LORA = 256
MLA_NOPE = 128
MLA_ROPE = 64
MLA_V = 128
MLA_QK = MLA_NOPE + MLA_ROPE
MLA_IN = MLA_Q_LORA + MLA_KV_LORA + MLA_ROPE
ROPE_BASE = 10000.0
HGRN_HEADS = 8
HGRN_D = 128
HGRN_CHUNK = 32
D_FF = 4 * D_MODEL
EPS = 1e-6
LOG2_E = 1.4426950408889634
ADAM_LR, ADAM_B1, ADAM_B2, ADAM_EPS, ADAM_WD, ADAM_STEP = 0.001, 0.9, 0.999, 1e-08, 0.01, 10

V7X_VMEM_LIMIT_BYTES = 56 * 1024 * 1024

NN = (((1,), (0,)), ((), ()))
NT = (((1,), (1,)), ((), ()))
TN = (((0,), (0,)), ((), ()))
_DIMS = {"nn": NN, "nt": NT, "tn": TN}


def _params(*sem):
    return pltpu.CompilerParams(dimension_semantics=sem, vmem_limit_bytes=V7X_VMEM_LIMIT_BYTES)


def _dot(a, b, dims=NN):
    return lax.dot_general(a, b, dims, preferred_element_type=F32)


def _dot_select(sel, x, pieces, dims=NN):
    sel = sel.astype(BF16)
    acc, rest = None, x
    for _ in range(pieces):
        term = rest.astype(BF16)
        part = _dot(sel, term, dims)
        acc = part if acc is None else acc + part
        rest = rest - term.astype(F32)
    return acc


def _rstd(x):
    return lax.rsqrt(jnp.mean(x * x, axis=-1, keepdims=True) + EPS)


def _rms_bwd_rows(x, g, dy):
    r = _rstd(x)
    xh = x * r
    dyg = dy * g
    dx = r * (dyg - xh * jnp.mean(dyg * xh, axis=-1, keepdims=True))
    return dx, dy * xh


def _row_tile(n, want):
    t = min(n, want)
    assert n % t == 0, (n, t)
    return t


def _divisor_tile(n, cap, mult):
    for t in range(min(cap, n) - min(cap, n) % mult, 0, -mult):
        if n % t == 0:
            return t
    return n


def _rms_fwd(x, g, res, out_dtype, name):
    s, d = x.shape
    ts = _row_tile(s, 512)

    def body(x_ref, g_ref, *rest):
        xf = x_ref[...]
        y = xf * _rstd(xf) * g_ref[...]
        if res is not None:
            y = rest[0][...] + y
        rest[-1][...] = y.astype(out_dtype)

    row = pl.BlockSpec((ts, d), lambda i: (i, 0))
    vec = pl.BlockSpec((1, d), lambda i: (0, 0))
    ins = [x, g] + ([res] if res is not None else [])
    return pl.pallas_call(
        body, grid=(s // ts,), in_specs=[row, vec] + ([row] if res is not None else []), out_specs=row,
        out_shape=jax.ShapeDtypeStruct((s, d), out_dtype), compiler_params=_params("parallel"), name=name)(*ins)


def _rms_bwd(x, g, dy, res, out_dtype, name):
    s, d = x.shape
    ts = _row_tile(s, 512)

    def body(x_ref, g_ref, dy_ref, *rest):
        dx_ref, dg_ref = rest[-2:]
        dx, dg = _rms_bwd_rows(x_ref[...], g_ref[...], dy_ref[...].astype(F32))
        if res is not None:
            dx = rest[0][...] + dx
        dx_ref[...] = dx.astype(out_dtype)

        @pl.when(pl.program_id(0) == 0)
        def _():
            dg_ref[...] = jnp.zeros_like(dg_ref)

        dg_ref[...] += jnp.sum(dg, axis=0, keepdims=True)

    row = pl.BlockSpec((ts, d), lambda i: (i, 0))
    vec = pl.BlockSpec((1, d), lambda i: (0, 0))
    ins = [x, g, dy] + ([res] if res is not None else [])
    return pl.pallas_call(
        body, grid=(s // ts,), in_specs=[row, vec, row] + ([row] if res is not None else []), out_specs=(row, vec),
        out_shape=(jax.ShapeDtypeStruct((s, d), out_dtype), jax.ShapeDtypeStruct((1, d), F32)),
        compiler_params=_params("arbitrary"), name=name)(*ins)


def _mm(a, b, mode, tm, tn, name, out_dtypes=(F32,), shard=None, epi=None, extras=(), rider=None):
    if mode == "tn":
        k, m = a.shape
        a_spec = pl.BlockSpec((k, tm), lambda i, j: (0, i))
    else:
        m, k = a.shape
        a_spec = pl.BlockSpec((tm, k), lambda i, j: (i, 0))
    if mode == "nt":
        n = b.shape[0]
        b_spec = pl.BlockSpec((tn, k), lambda i, j: (j, 0))
    else:
        n = b.shape[1]
        b_spec = pl.BlockSpec((k, tn), lambda i, j: (0, j))
    assert m % tm == 0 and n % tn == 0, (name, m, tm, n, tn)
    tile = pl.BlockSpec((tm, tn), lambda i, j: (i, j))
    if shard == "rows":
        per = m // N_DEV // tm
        out_specs = [pl.BlockSpec((None, tm, tn), lambda i, j: (i // per, i % per, j))]
        out_shape = [jax.ShapeDtypeStruct((N_DEV, m // N_DEV, n), out_dtypes[0])]
    elif shard == "cols":
        per = n // N_DEV // tn
        out_specs = [pl.BlockSpec((None, tm, tn), lambda i, j: (j // per, i, j % per))]
        out_shape = [jax.ShapeDtypeStruct((N_DEV, m, n // N_DEV), out_dtypes[0])]
    else:
        out_specs = [tile for _ in out_dtypes]
        out_shape = [jax.ShapeDtypeStruct((m, n), dt) for dt in out_dtypes]
    n_ex, n_out = len(extras), len(out_shape)
    r_in, r_in_specs, r_out, r_out_specs, r_scr = _rider_specs(rider)
    n_rin, n_rout = len(r_in), len(r_out)
    grid = (m // tm, n // tn)

    def body(a_ref, b_ref, *refs):
        ex_refs = refs[:n_ex]
        o_refs = refs[n_ex + n_rin:n_ex + n_rin + n_out]
        r_refs = refs[n_ex:n_ex + n_rin] + refs[n_ex + n_rin + n_out:]
        _ride(rider, pl.program_id(0) * grid[1] + pl.program_id(1), grid[0] * grid[1], r_refs)
        acc = _dot(a_ref[...].astype(BF16), b_ref[...].astype(BF16), _DIMS[mode])
        vals = (acc,) if epi is None else epi(acc, *[r[...] for r in ex_refs])
        for o_ref, val in zip(o_refs, vals):
            o_ref[...] = val.astype(o_ref.dtype)

    sem = ("parallel", "parallel") if rider is None else ("arbitrary", "arbitrary")
    out = pl.pallas_call(
        body, grid=grid, in_specs=[a_spec, b_spec] + [tile] * n_ex + r_in_specs, out_specs=out_specs + r_out_specs,
        out_shape=out_shape + r_out, scratch_shapes=r_scr, compiler_params=_params(*sem), name=name)(a, b, *extras, *r_in)
    res = out[0] if n_out == 1 else out[:n_out]
    return res if rider is None else (res, out[n_out:])


def _norm_mm(x, g, b, tm, tn, name, out_dtypes=(F32,), epi=None, rider=None):
    m, k = x.shape
    n = b.shape[1]
    assert m % tm == 0 and n % tn == 0, (name, m, tm, n, tn)
    grid = (m // tm, n // tn)
    n_out = len(out_dtypes)
    r_in, r_in_specs, r_out, r_out_specs, r_scr = _rider_specs(rider)
    n_rin = len(r_in)

    def body(x_ref, g_ref, b_ref, *refs):
        a_ref = refs[n_rin]
        o_refs = refs[n_rin + 1:n_rin + 1 + n_out]
        _ride(rider, pl.program_id(0) * grid[1] + pl.program_id(1), grid[0] * grid[1], refs[:n_rin] + refs[n_rin + 1 + n_out:])

        @pl.when(pl.program_id(1) == 0)
        def _():
            xf = x_ref[...]
            a_ref[...] = (xf * _rstd(xf) * g_ref[...]).astype(BF16)

        acc = _dot(a_ref[...], b_ref[...].astype(BF16))
        vals = (acc,) if epi is None else epi(acc)
        for o_ref, val in zip(o_refs, vals):
            o_ref[...] = val.astype(o_ref.dtype)

    row = pl.BlockSpec((tm, k), lambda i, j: (i, 0))
    tile = pl.BlockSpec((tm, tn), lambda i, j: (i, j))
    out = pl.pallas_call(
        body, grid=grid,
        in_specs=[row, pl.BlockSpec((1, k), lambda i, j: (0, 0)), pl.BlockSpec((k, tn), lambda i, j: (0, j))] + r_in_specs,
        out_specs=[row] + [tile] * n_out + r_out_specs,
        out_shape=[jax.ShapeDtypeStruct((m, k), BF16)] + [jax.ShapeDtypeStruct((m, n), dt) for dt in out_dtypes] + r_out,
        scratch_shapes=r_scr, compiler_params=_params("arbitrary", "arbitrary"), name=name)(x, g, b, *r_in)
    return out[:1 + n_out], out[1 + n_out:]


def _mm_norm_res(a, b, g, res, tm, name, rider=None):
    m, k = a.shape
    n = b.shape[1]
    assert m % tm == 0, (name, m, tm)
    r_in, r_in_specs, r_out, r_out_specs, r_scr = _rider_specs(rider)
    n_rin = len(r_in)

    def body(a_ref, b_ref, g_ref, res_ref, *refs):
        z_ref, o_ref = refs[n_rin:n_rin + 2]
        _ride(rider, pl.program_id(0), m // tm, refs[:n_rin] + refs[n_rin + 2:])
        z = _dot(a_ref[...].astype(BF16), b_ref[...].astype(BF16))
        z_ref[...] = z
        o_ref[...] = res_ref[...] + z * _rstd(z) * g_ref[...]

    row = pl.BlockSpec((tm, n), lambda i: (i, 0))
    out = pl.pallas_call(
        body, grid=(m // tm,),
        in_specs=[pl.BlockSpec((tm, k), lambda i: (i, 0)), pl.BlockSpec((k, n), lambda i: (0, 0)),
                  pl.BlockSpec((1, n), lambda i: (0, 0)), row] + r_in_specs,
        out_specs=[row, row] + r_out_specs,
        out_shape=[jax.ShapeDtypeStruct((m, n), F32), jax.ShapeDtypeStruct((m, n), F32)] + r_out,
        scratch_shapes=r_scr, compiler_params=_params("arbitrary"), name=name)(a, b, g, res, *r_in)
    return out[0], out[1], out[2:]


def _rmsbwd_mm(x, g, dy, b, tm, tn, name, out_dtypes=(F32,), epi=None, extras=(), rider=None, group_sums=None):
    m, k = x.shape
    n = b.shape[0]
    assert m % tm == 0 and n % tn == 0, (name, m, tm, n, tn)
    grid = (m // tm, n // tn)
    n_ex, n_out = len(extras), len(out_dtypes)
    r_in, r_in_specs, r_out, r_out_specs, r_scr = _rider_specs(rider)
    n_rin = len(r_in)
    n_gs = 0 if group_sums is None else 1
    if n_gs:
        assert tn == n and epi is None
        extras = tuple(extras) + (group_sums[0],)
        n_ex += 1

    def body(x_ref, g_ref, dy_ref, b_ref, *refs):
        ex_refs = refs[:n_ex]
        dx_ref, dg_ref = refs[n_ex + n_rin:n_ex + n_rin + 2]
        o_refs = refs[n_ex + n_rin + 2:n_ex + n_rin + 2 + n_out]
        i, j = pl.program_id(0), pl.program_id(1)
        _ride(rider, i * grid[1] + j, grid[0] * grid[1], refs[n_ex:n_ex + n_rin] + refs[n_ex + n_rin + 2 + n_out + n_gs:])

        @pl.when((i == 0) & (j == 0))
        def _():
            dg_ref[...] = jnp.zeros_like(dg_ref)

        @pl.when(j == 0)
        def _():
            dx, dg = _rms_bwd_rows(x_ref[...], g_ref[...], dy_ref[...])
            dx_ref[...] = dx.astype(BF16)
            dg_ref[...] += jnp.sum(dg, axis=0, keepdims=True)

        acc = _dot(dx_ref[...], b_ref[...].astype(BF16), NT)
        if n_gs:
            groups = group_sums[1]
            col = lax.broadcasted_iota(jnp.int32, (groups, n), 1) // (n // groups)
            sel = jnp.where(col == lax.broadcasted_iota(jnp.int32, (groups, n), 0), 1.0, 0.0)
            refs[n_ex + n_rin + 2 + n_out][...] = _dot_select(sel, acc * ex_refs[-1][...], 3, NT)
            vals = (acc,)
        else:
            vals = (acc,) if epi is None else epi(acc, *[r[...] for r in ex_refs])
        for o_ref, val in zip(o_refs, vals):
            o_ref[...] = val.astype(o_ref.dtype)

    row = pl.BlockSpec((tm, k), lambda i, j: (i, 0))
    vec = pl.BlockSpec((1, k), lambda i, j: (0, 0))
    tile = pl.BlockSpec((tm, tn), lambda i, j: (i, j))
    gs_specs = [pl.BlockSpec((group_sums[1], tm), lambda i, j: (0, i))] if n_gs else []
    gs_shape = [jax.ShapeDtypeStruct((group_sums[1], m), F32)] if n_gs else []
    out = pl.pallas_call(
        body, grid=grid,
        in_specs=[row, vec, row, pl.BlockSpec((tn, k), lambda i, j: (j, 0))] + [tile] * n_ex + r_in_specs,
        out_specs=[row, vec] + [tile] * n_out + gs_specs + r_out_specs,
        out_shape=[jax.ShapeDtypeStruct((m, k), BF16), jax.ShapeDtypeStruct((1, k), F32)]
        + [jax.ShapeDtypeStruct((m, n), dt) for dt in out_dtypes] + gs_shape + r_out,
        scratch_shapes=r_scr, compiler_params=_params("arbitrary", "arbitrary"), name=name)(x, g, dy, b, *extras, *r_in)
    return out[:2 + n_out + n_gs], out[2 + n_out + n_gs:]


def _mm_rmsbwd_res(a, b, x, g, res, tm, name, rider=None):
    m, k = a.shape
    n = b.shape[0]
    assert m % tm == 0, (name, m, tm)
    r_in, r_in_specs, r_out, r_out_specs, r_scr = _rider_specs(rider)
    n_rin = len(r_in)

    def body(a_ref, b_ref, x_ref, g_ref, res_ref, *refs):
        o_ref, dg_ref = refs[n_rin:n_rin + 2]
        _ride(rider, pl.program_id(0), m // tm, refs[:n_rin] + refs[n_rin + 2:])

        @pl.when(pl.program_id(0) == 0)
        def _():
            dg_ref[...] = jnp.zeros_like(dg_ref)

        da = _dot(a_ref[...].astype(BF16), b_ref[...].astype(BF16), NT)
        dx, dg = _rms_bwd_rows(x_ref[...], g_ref[...], da)
        o_ref[...] = res_ref[...] + dx
        dg_ref[...] += jnp.sum(dg, axis=0, keepdims=True)

    row = pl.BlockSpec((tm, n), lambda i: (i, 0))
    vec = pl.BlockSpec((1, n), lambda i: (0, 0))
    out = pl.pallas_call(
        body, grid=(m // tm,),
        in_specs=[pl.BlockSpec((tm, k), lambda i: (i, 0)), pl.BlockSpec((n, k), lambda i: (0, 0)), row, vec, row] + r_in_specs,
        out_specs=[row, vec] + r_out_specs,
        out_shape=[jax.ShapeDtypeStruct((m, n), F32), jax.ShapeDtypeStruct((1, n), F32)] + r_out,
        scratch_shapes=r_scr, compiler_params=_params("arbitrary"), name=name)(a, b, x, g, res, *r_in)
    return out[0], out[1], out[2:]


def _rope_tables(pos, name):
    s = pos.shape[0]
    half = MLA_ROPE // 2
    inv_freq = jnp.asarray(np.power(np.float32(ROPE_BASE), -np.arange(0, MLA_ROPE, 2, dtype=np.float32) / MLA_ROPE)
                           .astype(np.float32).reshape(1, half))

    def body(p_ref, f_ref, c_ref, s_ref):
        ang = p_ref[...].astype(F32) * f_ref[...]
        c_ref[...] = jnp.cos(ang)
        s_ref[...] = jnp.sin(ang)

    return pl.pallas_call(
        body, out_shape=(jax.ShapeDtypeStruct((s, half), F32), jax.ShapeDtypeStruct((s, half), F32)), name=name)(pos, inv_freq)


def _lb_softmax(logits):
    m = jnp.max(logits, axis=0, keepdims=True)
    e = jnp.exp(logits - m)
    return e / jnp.sum(e, axis=0, keepdims=True)


def _lb_fwd(logits, name):
    def body(l_ref, o_ref):
        p = _lb_softmax(l_ref[...])
        acc = jnp.zeros_like(p[0:1])
        o_ref[0:1, :] = acc
        for layer in range(1, DEPTH):
            acc = acc + p[layer:layer + 1]
            o_ref[layer:layer + 1, :] = acc

    return pl.pallas_call(body, out_shape=jax.ShapeDtypeStruct(logits.shape, F32), name=name)(logits)


def _lb_bwd(logits, dlb, name):
    def body(l_ref, d_ref, o_ref):
        p = _lb_softmax(l_ref[...])
        d = d_ref[...]
        dp = [jnp.zeros_like(d[0:1])] * DEPTH
        run = jnp.zeros_like(d[0:1])
        for layer in range(DEPTH - 1, 0, -1):
            run = run + d[layer:layer + 1]
            dp[layer] = run
        inner = sum(p[layer:layer + 1] * dp[layer] for layer in range(DEPTH))
        for layer in range(DEPTH):
            o_ref[layer:layer + 1, :] = p[layer:layer + 1] * (dp[layer] - inner)

    return pl.pallas_call(body, out_shape=jax.ShapeDtypeStruct(logits.shape, F32), name=name)(logits, dlb)


def _loss(y, target, name):
    s, d = y.shape
    ts = _row_tile(s, 512)

    def body(y_ref, t_ref, l_ref, dy_ref):
        e = y_ref[...] - t_ref[...]
        dy_ref[...] = e / d

        @pl.when(pl.program_id(0) == 0)
        def _():
            l_ref[...] = jnp.zeros_like(l_ref)

        l_ref[...] += 0.5 * jnp.sum(jnp.mean(e * e, axis=-1, keepdims=True), axis=0, keepdims=True)

    row = pl.BlockSpec((ts, d), lambda i: (i, 0))
    return pl.pallas_call(
        body, grid=(s // ts,), in_specs=[row, row], out_specs=(pl.BlockSpec((1, 1), lambda i: (0, 0)), row),
        out_shape=(jax.ShapeDtypeStruct((1, 1), F32), jax.ShapeDtypeStruct((s, d), F32)),
        compiler_params=_params("arbitrary"), name=name)(y, target)


def _rope(t1, t2, cos, sin):
    return t1 * cos - t2 * sin, t1 * sin + t2 * cos


def _rope_bwd(d1, d2, cos, sin):
    return d1 * cos + d2 * sin, d2 * cos - d1 * sin


def _mla_qkv(proj, qn, kvn, w_uq, w_ukv, cos, sin, name):
    s = proj.shape[0]
    ts = _row_tile(s, 256)
    hh, half = MLA_HEADS, MLA_ROPE // 2

    def body(p_ref, qn_ref, kvn_ref, wq_ref, wkv_ref, c_ref, s_ref, cq_ref, ckv_ref, q_ref, k_ref, v_ref):
        p = p_ref[...]
        cq, ckv, kr = p[:, :MLA_Q_LORA], p[:, MLA_Q_LORA:MLA_Q_LORA + MLA_KV_LORA], p[:, MLA_Q_LORA + MLA_KV_LORA:]
        cqn = (cq * _rstd(cq) * qn_ref[...]).astype(BF16)
        ckvn = (ckv * _rstd(ckv) * kvn_ref[...]).astype(BF16)
        cq_ref[...] = cqn
        ckv_ref[...] = ckvn
        qe = _dot(cqn, wq_ref[...])
        kve = _dot(ckvn, wkv_ref[...])
        cos_, sin_ = c_ref[...], s_ref[...]
        k1, k2 = _rope(kr[:, :half], kr[:, half:], cos_, sin_)
        k1, k2 = k1.astype(BF16), k2.astype(BF16)
        for h in range(hh):
            b = h * MLA_QK
            q_ref[h, :, 0:MLA_NOPE] = qe[:, b:b + MLA_NOPE].astype(BF16)
            q1, q2 = _rope(qe[:, b + MLA_NOPE:b + MLA_NOPE + half], qe[:, b + MLA_NOPE + half:b + MLA_QK], cos_, sin_)
            q_ref[h, :, MLA_NOPE:MLA_NOPE + half] = q1.astype(BF16)
            q_ref[h, :, MLA_NOPE + half:MLA_QK] = q2.astype(BF16)
            b = h * (MLA_NOPE + MLA_V)
            k_ref[h, :, 0:MLA_NOPE] = kve[:, b:b + MLA_NOPE].astype(BF16)
            k_ref[h, :, MLA_NOPE:MLA_NOPE + half] = k1
            k_ref[h, :, MLA_NOPE + half:MLA_QK] = k2
            v_ref[h] = kve[:, b + MLA_NOPE:b + MLA_NOPE + MLA_V].astype(BF16)

    def row(w):
        return pl.BlockSpec((ts, w), lambda i: (i, 0))

    def full(shape):
        return pl.BlockSpec(shape, lambda i: (0,) * len(shape))

    def heads(w):
        return pl.BlockSpec((hh, ts, w), lambda i: (0, i, 0))

    return pl.pallas_call(
        body, grid=(s // ts,),
        in_specs=[row(MLA_IN), full(qn.shape), full(kvn.shape), full(w_uq.shape), full(w_ukv.shape), row(half), row(half)],
        out_specs=(row(MLA_Q_LORA), row(MLA_KV_LORA), heads(MLA_QK), heads(MLA_QK), heads(MLA_V)),
        out_shape=(jax.ShapeDtypeStruct((s, MLA_Q_LORA), BF16), jax.ShapeDtypeStruct((s, MLA_KV_LORA), BF16),
                   jax.ShapeDtypeStruct((hh, s, MLA_QK), BF16), jax.ShapeDtypeStruct((hh, s, MLA_QK), BF16),
                   jax.ShapeDtypeStruct((hh, s, MLA_V), BF16)),
        compiler_params=_params("parallel"), name=name)(proj, qn, kvn, w_uq, w_ukv, cos, sin)


ATTN_BLOCK = 2048
ATTN_FWD_TILE = (256, 1024)
ATTN_BWD_TILE = (512, 512)


def _attn_block(s):
    return _row_tile(s, ATTN_BLOCK)


def _tile_sees(diag, q0, tq, k0, tk):
    if not diag:
        return True, False
    return k0 <= q0 + tq - 1, k0 + tk - 1 > q0


def _causal_pairs(nb, kv_major):
    if kv_major:
        pairs = [(i, j) for j in range(nb) for i in range(j, nb)]
    else:
        pairs = [(i, j) for i in range(nb) for j in range(i + 1)]
    return (jnp.asarray(np.array([p[0] for p in pairs], np.int32)), jnp.asarray(np.array([p[1] for p in pairs], np.int32)))


def _ride(rider, step, total, refs):
    if rider is None:
        return

    @pl.when(step == 0)
    def _():
        rider.start(*refs)

    @pl.when(step == (total * 7) // 8)
    def _():
        rider.middle(*refs)

    @pl.when(step == total - 1)
    def _():
        rider.finish(*refs)


def _rider_specs(rider):
    if rider is None:
        return [], [], [], [], []
    return (list(rider.operands), [_HBM] * len(rider.operands), list(rider.out_shapes), [_HBM] * len(rider.out_shapes),
            list(rider.scratch))


def _attn_fwd(q, k, v, name, rider=None):
    hh, s, _ = q.shape
    blk = _attn_block(s)
    tq, tk = min(blk, ATTN_FWD_TILE[0]), min(blk, ATTN_FWD_TILE[1])
    nb = s // blk
    it, jt = _causal_pairs(nb, kv_major=False)
    npair = int(it.shape[0])
    scale = MLA_QK ** -0.5
    c2 = scale * LOG2_E
    r_in, r_in_specs, r_out, r_out_specs, r_scr = _rider_specs(rider)
    n_rin, n_rout, n_rscr = len(r_in), len(r_out), len(r_scr)

    def body(it_ref, jt_ref, q_ref, k_ref, v_ref, *refs):
        r_refs = refs[:n_rin] + refs[n_rin + 2:n_rin + 2 + n_rout] + refs[len(refs) - n_rscr:]
        o_ref, lse_ref = refs[n_rin:n_rin + 2]
        m_scr, acc_scr, v_scr = refs[n_rin + 2 + n_rout:n_rin + 2 + n_rout + 3]
        h, t = pl.program_id(0), pl.program_id(1)
        step = h * npair + t
        _ride(rider, step, hh * npair, r_refs)
        i, j = it_ref[t], jt_ref[t]

        @pl.when(j == 0)
        def _():
            m_scr[...] = jnp.full_like(m_scr, -jnp.inf)
            acc_scr[...] = jnp.zeros_like(acc_scr)
            v_scr[:, MLA_V:] = jnp.ones((blk, MLA_V), BF16)

        def block(diag):
            v_scr[:, :MLA_V] = v_ref[...]
            for k0 in range(0, blk, tk):
                kb, vb = k_ref[k0:k0 + tk, :], v_scr[k0:k0 + tk, :]
                for q0 in range(0, blk, tq):
                    visible, needs_mask = _tile_sees(diag, q0, tq, k0, tk)
                    if not visible:
                        continue
                    rows = slice(q0, q0 + tq)
                    sc = _dot(q_ref[rows, :], kb, NT)
                    if needs_mask:
                        qpos = q0 + lax.broadcasted_iota(jnp.int32, (tq, tk), 0)
                        kpos = k0 + lax.broadcasted_iota(jnp.int32, (tq, tk), 1)
                        sc = jnp.where(qpos >= kpos, sc, -jnp.inf)
                    m_prev = m_scr[rows, :]
                    m_new = jnp.maximum(m_prev, jnp.max(sc, axis=-1, keepdims=True))
                    alpha = jnp.exp2((m_prev - m_new) * c2)
                    p = jnp.exp2((sc - m_new) * c2)
                    acc_scr[rows, :] = alpha * acc_scr[rows, :] + _dot(p.astype(BF16), vb)
                    m_scr[rows, :] = m_new

        @pl.when(j < i)
        def _():
            block(False)

        @pl.when(j == i)
        def _():
            block(True)
            acc = acc_scr[...]
            l = acc[:, MLA_V:MLA_V + 1]
            o_ref[...] = acc[:, :MLA_V] / l
            lse_ref[...] = m_scr[...] * scale + jnp.log(l)

    grid_spec = pltpu.PrefetchScalarGridSpec(
        num_scalar_prefetch=2, grid=(hh, npair),
        in_specs=[pl.BlockSpec((None, blk, MLA_QK), lambda h, t, it_, jt_: (h, it_[t], 0)),
                  pl.BlockSpec((None, blk, MLA_QK), lambda h, t, it_, jt_: (h, jt_[t], 0)),
                  pl.BlockSpec((None, blk, MLA_V), lambda h, t, it_, jt_: (h, jt_[t], 0))] + r_in_specs,
        out_specs=[pl.BlockSpec((blk, MLA_V), lambda h, t, it_, jt_: (it_[t], h)),
                   pl.BlockSpec((None, blk, 1), lambda h, t, it_, jt_: (h, it_[t], 0))] + r_out_specs,
        scratch_shapes=[pltpu.VMEM((blk, 1), F32), pltpu.VMEM((blk, 2 * MLA_V), F32),
                        pltpu.VMEM((blk, 2 * MLA_V), BF16)] + r_scr)
    out = pl.pallas_call(
        body, grid_spec=grid_spec,
        out_shape=[jax.ShapeDtypeStruct((s, hh * MLA_V), F32), jax.ShapeDtypeStruct((hh, s, 1), F32)] + r_out,
        compiler_params=_params("arbitrary", "arbitrary"), name=name)(it, jt, q, k, v, *r_in)
    return out[0], out[1], out[2:]


def _attn_bwd(q, k, v, do, lse_row, delta_row, name, rider=None):
    hh, s, _ = q.shape
    blk = _attn_block(s)
    tq, tk = min(blk, ATTN_BWD_TILE[0]), min(blk, ATTN_BWD_TILE[1])
    nb = s // blk
    it, jt = _causal_pairs(nb, kv_major=True)
    npair = int(it.shape[0])
    scale = MLA_QK ** -0.5
    c2 = scale * LOG2_E
    r_in, r_in_specs, r_out, r_out_specs, r_scr = _rider_specs(rider)
    n_rin, n_rout, n_rscr = len(r_in), len(r_out), len(r_scr)

    def body(it_ref, jt_ref, q_ref, k_ref, v_ref, do_ref, lse_ref, dl_ref, *refs):
        r_refs = refs[:n_rin] + refs[n_rin + 3:n_rin + 3 + n_rout] + refs[len(refs) - n_rscr:]
        dq_out, dk_out, dv_out = refs[n_rin:n_rin + 3]
        dq_ref, dk_ref, dv_ref = refs[n_rin + 3 + n_rout:n_rin + 3 + n_rout + 3]
        h, t = pl.program_id(0), pl.program_id(1)
        step = h * npair + t
        _ride(rider, step, hh * npair, r_refs)
        i, j = it_ref[t], jt_ref[t]

        @pl.when(t == 0)
        def _():
            dq_ref[...] = jnp.zeros_like(dq_ref)

        def block(diag):
            if diag:
                dk_ref[...] = jnp.zeros_like(dk_ref)
                dv_ref[...] = jnp.zeros_like(dv_ref)
            for q0 in range(0, blk, tq):
                qb = q_ref[q0:q0 + tq, :]
                dob = do_ref[q0:q0 + tq, :].astype(BF16)
                lse2 = lse_ref[:, q0:q0 + tq] * LOG2_E
                dl = dl_ref[pl.ds(h, 1), q0:q0 + tq]
                dq = None
                for k0 in range(0, blk, tk):
                    visible, needs_mask = _tile_sees(diag, q0, tq, k0, tk)
                    if not visible:
                        continue
                    kb, vb = k_ref[k0:k0 + tk, :], v_ref[k0:k0 + tk, :]
                    pt = jnp.exp2(_dot(kb, qb, NT) * c2 - lse2)
                    if needs_mask:
                        kpos = k0 + lax.broadcasted_iota(jnp.int32, (tk, tq), 0)
                        qpos = q0 + lax.broadcasted_iota(jnp.int32, (tk, tq), 1)
                        pt = jnp.where(qpos >= kpos, pt, 0.0)
                    dv_ref[k0:k0 + tk, :] += _dot(pt.astype(BF16), dob)
                    dpt = _dot(vb, dob, NT)
                    dst = (pt * (dpt - dl) * scale).astype(BF16)
                    dk_ref[k0:k0 + tk, :] += _dot(dst, qb)
                    part = _dot(dst, kb, TN)
                    dq = part if dq is None else dq + part
                rows = pl.ds(pl.multiple_of(i * blk + q0, tq), tq)
                dq_ref[rows, :] += dq

        @pl.when(i == j)
        def _():
            block(True)

        @pl.when(i > j)
        def _():
            block(False)

        @pl.when(i == nb - 1)
        def _():
            dk_out[...] = dk_ref[...].astype(BF16)
            dv_out[...] = dv_ref[...].astype(BF16)

        @pl.when(t == npair - 1)
        def _():
            dq_out[...] = dq_ref[...].astype(BF16)

    grid_spec = pltpu.PrefetchScalarGridSpec(
        num_scalar_prefetch=2, grid=(hh, npair),
        in_specs=[pl.BlockSpec((None, blk, MLA_QK), lambda h, t, it_, jt_: (h, it_[t], 0)),
                  pl.BlockSpec((None, blk, MLA_QK), lambda h, t, it_, jt_: (h, jt_[t], 0)),
                  pl.BlockSpec((None, blk, MLA_V), lambda h, t, it_, jt_: (h, jt_[t], 0)),
                  pl.BlockSpec((blk, MLA_V), lambda h, t, it_, jt_: (it_[t], h)),
                  pl.BlockSpec((None, 1, blk), lambda h, t, it_, jt_: (h, 0, it_[t])),
                  pl.BlockSpec((hh, blk), lambda h, t, it_, jt_: (0, it_[t]))] + r_in_specs,
        out_specs=[pl.BlockSpec((None, s, MLA_QK), lambda h, t, it_, jt_: (h, 0, 0)),
                   pl.BlockSpec((None, blk, MLA_QK), lambda h, t, it_, jt_: (h, jt_[t], 0)),
                   pl.BlockSpec((None, blk, MLA_V), lambda h, t, it_, jt_: (h, jt_[t], 0))] + r_out_specs,
        scratch_shapes=[pltpu.VMEM((s, MLA_QK), F32), pltpu.VMEM((blk, MLA_QK), F32), pltpu.VMEM((blk, MLA_V), F32)] + r_scr)
    out = pl.pallas_call(
        body, grid_spec=grid_spec,
        out_shape=[jax.ShapeDtypeStruct((hh, s, MLA_QK), BF16), jax.ShapeDtypeStruct((hh, s, MLA_QK), BF16),
                   jax.ShapeDtypeStruct((hh, s, MLA_V), BF16)] + r_out,
        compiler_params=_params("arbitrary", "arbitrary"), name=name)(it, jt, q, k, v, do, lse_row, delta_row, *r_in)
    return out[0], out[1], out[2], out[3:]


def _mla_bwd_mid(dq, dk, dv, cos, sin, proj, qn, kvn, w_uq, w_ukv, name):
    s = proj.shape[0]
    ts = _row_tile(s, 256)
    hh, half = MLA_HEADS, MLA_ROPE // 2
    nq, nkv = hh * MLA_QK, hh * (MLA_NOPE + MLA_V)

    def body(dq_ref, dk_ref, dv_ref, c_ref, s_ref, p_ref, qn_ref, kvn_ref, wq_ref, wkv_ref,
             dqe_ref, dkve_ref, dp_ref, dqn_ref, dkvn_ref):
        cos_, sin_ = c_ref[...], s_ref[...]
        dkr1 = jnp.zeros((ts, half), F32)
        dkr2 = jnp.zeros((ts, half), F32)
        for h in range(hh):
            dqh, dkh = dq_ref[h], dk_ref[h]
            b = h * MLA_QK
            dqe_ref[:, b:b + MLA_NOPE] = dqh[:, :MLA_NOPE].astype(BF16)
            d1, d2 = _rope_bwd(dqh[:, MLA_NOPE:MLA_NOPE + half], dqh[:, MLA_NOPE + half:], cos_, sin_)
            dqe_ref[:, b + MLA_NOPE:b + MLA_NOPE + half] = d1.astype(BF16)
            dqe_ref[:, b + MLA_NOPE + half:b + MLA_QK] = d2.astype(BF16)
            b = h * (MLA_NOPE + MLA_V)
            dkve_ref[:, b:b + MLA_NOPE] = dkh[:, :MLA_NOPE].astype(BF16)
            dkve_ref[:, b + MLA_NOPE:b + MLA_NOPE + MLA_V] = dv_ref[h].astype(BF16)
            dkr1 = dkr1 + dkh[:, MLA_NOPE:MLA_NOPE + half]
            dkr2 = dkr2 + dkh[:, MLA_NOPE + half:]
        dkr1, dkr2 = _rope_bwd(dkr1, dkr2, cos_, sin_)
        dcqn = _dot(dqe_ref[...], wq_ref[...], NT)
        dckvn = _dot(dkve_ref[...], wkv_ref[...], NT)
        p = p_ref[...]
        dcq, dqn = _rms_bwd_rows(p[:, :MLA_Q_LORA], qn_ref[...], dcqn)
        dckv, dkvn = _rms_bwd_rows(p[:, MLA_Q_LORA:MLA_Q_LORA + MLA_KV_LORA], kvn_ref[...], dckvn)
        dp_ref[:, :MLA_Q_LORA] = dcq.astype(BF16)
        dp_ref[:, MLA_Q_LORA:MLA_Q_LORA + MLA_KV_LORA] = dckv.astype(BF16)
        dp_ref[:, MLA_Q_LORA + MLA_KV_LORA:MLA_Q_LORA + MLA_KV_LORA + half] = dkr1.astype(BF16)
        dp_ref[:, MLA_Q_LORA + MLA_KV_LORA + half:] = dkr2.astype(BF16)

        @pl.when(pl.program_id(0) == 0)
        def _():
            dqn_ref[...] = jnp.zeros_like(dqn_ref)
            dkvn_ref[...] = jnp.zeros_like(dkvn_ref)

        dqn_ref[...] += jnp.sum(dqn, axis=0, keepdims=True)
        dkvn_ref[...] += jnp.sum(dkvn, axis=0, keepdims=True)

    def row(w):
        return pl.BlockSpec((ts, w), lambda i: (i, 0))

    def full(shape):
        return pl.BlockSpec(shape, lambda i: (0,) * len(shape))

    def heads(w):
        return pl.BlockSpec((hh, ts, w), lambda i: (0, i, 0))

    return pl.pallas_call(
        body, grid=(s // ts,),
        in_specs=[heads(MLA_QK), heads(MLA_QK), heads(MLA_V), row(half), row(half), row(MLA_IN),
                  full(qn.shape), full(kvn.shape), full(w_uq.shape), full(w_ukv.shape)],
        out_specs=(row(nq), row(nkv), row(MLA_IN), full(qn.shape), full(kvn.shape)),
        out_shape=(jax.ShapeDtypeStruct((s, nq), BF16), jax.ShapeDtypeStruct((s, nkv), BF16),
                   jax.ShapeDtypeStruct((s, MLA_IN), BF16), jax.ShapeDtypeStruct(qn.shape, F32),
                   jax.ShapeDtypeStruct(kvn.shape, F32)),
        compiler_params=_params("arbitrary"), name=name)(dq, dk, dv, cos, sin, proj, qn, kvn, w_uq, w_ukv)


HGRN_TILE = 256


def _chunk_masks(t):
    r = lax.broadcasted_iota(jnp.int32, (t, t), 0)
    c = lax.broadcasted_iota(jnp.int32, (t, t), 1)
    same = (r // HGRN_CHUNK) == (c // HGRN_CHUNK)
    return r, c, same


def _hgrn_gates(p, lb):
    hk = HGRN_HEADS * HGRN_D
    qx, fx, ix, gx = p[:, :hk], p[:, hk:2 * hk], p[:, 2 * hk:3 * hk], p[:, 3 * hk:]
    sig_f = jax.nn.sigmoid(fx)
    f = lb + (1.0 - lb) * sig_f
    sig_q = jax.nn.sigmoid(qx)
    t = p.shape[0]
    r, c, same = _chunk_masks(t)
    lower = jnp.where(same & (c <= r), 1.0, 0.0).astype(F32)
    b = _dot_select(lower, jnp.log(f), 3)
    b3 = b.reshape(t // HGRN_CHUNK, HGRN_CHUNK, hk)
    bref = jnp.broadcast_to(b3[:, HGRN_CHUNK // 2:HGRN_CHUNK // 2 + 1, :], b3.shape).reshape(t, hk)
    blast = jnp.broadcast_to(b3[:, HGRN_CHUNK - 1:, :], b3.shape).reshape(t, hk)
    return qx, ix, gx, sig_f, f, sig_q, b, bref, blast


def _hgrn_fwd(proj, lb, onorm, name, rider=None):
    s = proj.shape[0]
    t = _row_tile(s, HGRN_TILE)
    nc = t // HGRN_CHUNK
    hh, dd, hk = HGRN_HEADS, HGRN_D, HGRN_HEADS * HGRN_D
    r_in, r_in_specs, r_out, r_out_specs, r_scr = _rider_specs(rider)
    n_rin, n_rout = len(r_in), len(r_out)

    def body(p_ref, lb_ref, on_ref, *refs):
        y_ref, o_ref, st_ref = refs[n_rin:n_rin + 3]
        st_scr = refs[n_rin + 3 + n_rout]
        _ride(rider, pl.program_id(0), s // t, refs[:n_rin] + refs[n_rin + 3:n_rin + 3 + n_rout] + refs[n_rin + 4 + n_rout:])

        @pl.when(pl.program_id(0) == 0)
        def _():
            st_scr[...] = jnp.zeros_like(st_scr)

        qx, ix, gx, _, f, sig_q, b, bref, blast = _hgrn_gates(p_ref[...], lb_ref[...])
        q = qx * sig_q
        k = 1.0 - f
        r, c, same = _chunk_masks(t)
        causal = same & (c <= r)
        for h in range(hh):
            sl = slice(h * dd, (h + 1) * dd)
            bh, brefh, blasth, qh, kh = b[:, sl], bref[:, sl], blast[:, sl], q[:, sl], k[:, sl]
            vh = ix[:, sl].astype(BF16)
            q_rel = (qh * jnp.exp(bh - brefh)).astype(BF16)
            k_rel = (kh * jnp.exp(brefh - bh)).astype(BF16)
            a = jnp.where(causal, _dot(q_rel, k_rel, NT), 0.0)
            o_intra = _dot(a.astype(BF16), vh)
            q_dec = (qh * jnp.exp(bh)).astype(BF16)
            k_dec = (kh * jnp.exp(blasth - bh)).astype(BF16)
            dec = jnp.exp(blasth)
            pieces = []
            for ci in range(nc):
                rows = slice(ci * HGRN_CHUNK, (ci + 1) * HGRN_CHUNK)
                st = st_scr[h]
                if ci == 0:
                    st_ref[h] = st
                pieces.append(_dot(q_dec[rows], st.astype(BF16), NT))
                st_scr[h] = st * dec[ci * HGRN_CHUNK:ci * HGRN_CHUNK + 1, :] + _dot(vh[rows], k_dec[rows], TN)
            oh = o_intra + jnp.concatenate(pieces, axis=0)
            o_ref[:, sl] = oh
            gate = gx[:, sl] * jax.nn.sigmoid(gx[:, sl])
            y_ref[:, sl] = (oh * _rstd(oh) * on_ref[...] * gate).astype(BF16)

    out = pl.pallas_call(
        body, grid=(s // t,),
        in_specs=[pl.BlockSpec((t, 4 * hk), lambda i: (i, 0)), pl.BlockSpec((1, hk), lambda i: (0, 0)),
                  pl.BlockSpec((1, dd), lambda i: (0, 0))] + r_in_specs,
        out_specs=[pl.BlockSpec((t, hk), lambda i: (i, 0)), pl.BlockSpec((t, hk), lambda i: (i, 0)),
                   pl.BlockSpec((None, hh, dd, dd), lambda i: (i, 0, 0, 0))] + r_out_specs,
        out_shape=[jax.ShapeDtypeStruct((s, hk), BF16), jax.ShapeDtypeStruct((s, hk), F32),
                   jax.ShapeDtypeStruct((s // t, hh, dd, dd), F32)] + r_out,
        scratch_shapes=[pltpu.VMEM((hh, dd, dd), F32)] + r_scr,
        compiler_params=_params("arbitrary"), name=name)(proj, lb, onorm, *r_in)
    return out[0], out[1], out[2], out[3:]


def _hgrn_bwd(proj, lb, onorm, o, states, dy, name, rider=None):
    s = proj.shape[0]
    t = _row_tile(s, HGRN_TILE)
    nt = s // t
    nc = t // HGRN_CHUNK
    hh, dd, hk = HGRN_HEADS, HGRN_D, HGRN_HEADS * HGRN_D
    r_in, r_in_specs, r_out, r_out_specs, r_scr = _rider_specs(rider)
    n_rin, n_rout, n_rscr = len(r_in), len(r_out), len(r_scr)

    def body(p_ref, lb_ref, on_ref, o_ref, st_ref, dy_ref, *refs):
        r_refs = refs[:n_rin] + refs[n_rin + 3:n_rin + 3 + n_rout] + refs[len(refs) - n_rscr:]
        dp_ref, dlb_ref, don_ref = refs[n_rin:n_rin + 3]
        dst_scr, cat_scr, ext_scr, dk_scr, dq_scr = refs[n_rin + 3 + n_rout:n_rin + 3 + n_rout + 5]
        _ride(rider, pl.program_id(0), nt, r_refs)

        @pl.when(pl.program_id(0) == 0)
        def _():
            dst_scr[...] = jnp.zeros_like(dst_scr)
            dlb_ref[...] = jnp.zeros_like(dlb_ref)
            don_ref[...] = jnp.zeros_like(don_ref)

        lbv = lb_ref[...]
        qx, ix, gx, sig_f, f, sig_q, b, bref, blast = _hgrn_gates(p_ref[...], lbv)
        q = qx * sig_q
        k = 1.0 - f
        r, c, same = _chunk_masks(t)
        causal = same & (c <= r)
        on = on_ref[...]
        don = jnp.zeros((1, dd), F32)
        for h in range(hh):
            sl = slice(h * dd, (h + 1) * dd)
            oh = o_ref[:, sl]
            dyh = dy_ref[:, sl]
            gxh = gx[:, sl]
            sig_g = jax.nn.sigmoid(gxh)
            rs = _rstd(oh)
            dgate = dyh * (oh * rs * on)
            dp_ref[:, 3 * hk + h * dd:3 * hk + (h + 1) * dd] = (dgate * (sig_g * (1.0 + gxh * (1.0 - sig_g)))).astype(BF16)
            do, donh = _rms_bwd_rows(oh, on, dyh * (gxh * sig_g))
            don = don + jnp.sum(donh, axis=0, keepdims=True)
            dob = do.astype(BF16)
            bh, brefh, blasth, qh, kh = b[:, sl], bref[:, sl], blast[:, sl], q[:, sl], k[:, sl]
            vh = ix[:, sl].astype(BF16)
            e_qr, e_kr, e_qd, e_kd = jnp.exp(bh - brefh), jnp.exp(brefh - bh), jnp.exp(bh), jnp.exp(blasth - bh)
            dec = jnp.exp(blasth)
            q_rel, k_rel, q_dec, k_dec = qh * e_qr, kh * e_kr, qh * e_qd, kh * e_kd
            q_relb, k_relb, q_decb, k_decb = q_rel.astype(BF16), k_rel.astype(BF16), q_dec.astype(BF16), k_dec.astype(BF16)
            a = jnp.where(causal, _dot(q_relb, k_relb, NT), 0.0).astype(BF16)
            dv = _dot(a, dob, TN)
            da = jnp.where(causal, _dot(dob, vh, NT), 0.0).astype(BF16)
            dq_rel = _dot(da, k_relb)
            dk_rel = _dot(da, q_relb, TN)
            sts = [st_ref[h]]
            for ci in range(nc - 1):
                rows = slice(ci * HGRN_CHUNK, (ci + 1) * HGRN_CHUNK)
                sts.append(sts[-1] * dec[ci * HGRN_CHUNK:ci * HGRN_CHUNK + 1, :] + _dot(vh[rows], k_decb[rows], TN))
            dq_dec, dk_dec, dv_inter, ddec = [None] * nc, [None] * nc, [None] * nc, [None] * nc
            for ci in range(nc - 1, -1, -1):
                rows = slice(ci * HGRN_CHUNK, (ci + 1) * HGRN_CHUNK)
                st = sts[ci]
                dst = dst_scr[h]
                dstb = dst.astype(BF16)
                dq_dec[ci] = _dot(dob[rows], st.astype(BF16))
                dk_dec[ci] = _dot(vh[rows], dstb)
                dv_inter[ci] = _dot(k_decb[rows], dstb, NT)
                ddec[ci] = jnp.broadcast_to(jnp.sum(dst * st, axis=0, keepdims=True), (HGRN_CHUNK, dd))
                dst_scr[h] = dst * dec[ci * HGRN_CHUNK:ci * HGRN_CHUNK + 1, :] + _dot(dob[rows], q_decb[rows], TN)
            dq_dec = jnp.concatenate(dq_dec, axis=0)
            dk_dec = jnp.concatenate(dk_dec, axis=0)
            dv = dv + jnp.concatenate(dv_inter, axis=0)
            ddec = jnp.concatenate(ddec, axis=0)
            dp_ref[:, 2 * hk + h * dd:2 * hk + (h + 1) * dd] = dv.astype(BF16)
            dq_scr[:, sl] = dq_rel * e_qr + dq_dec * e_qd
            dk_scr[:, sl] = dk_rel * e_kr + dk_dec * e_kd
            g_qr, g_kr, g_qd, g_kd = dq_rel * q_rel, dk_rel * k_rel, dq_dec * q_dec, dk_dec * k_dec
            cat_scr[0:t, sl] = g_qr - g_kr + g_qd - g_kd
            cat_scr[t:2 * t, sl] = g_kr - g_qr
            cat_scr[2 * t:3 * t, sl] = g_kd
            ext_scr[:, sl] = ddec * dec
        upper = jnp.where(same & (c >= r), 1.0, 0.0).astype(F32)
        to_ref = jnp.where(same & (r % HGRN_CHUNK <= HGRN_CHUNK // 2), 1.0, 0.0).astype(F32)
        to_all = jnp.where(same, 1.0, 0.0).astype(F32)
        dlogf = _dot_select(jnp.concatenate([upper, to_ref, to_all], axis=1), cat_scr[...], 2) + ext_scr[...]
        df = dlogf / f - dk_scr[...]
        dp_ref[:, hk:2 * hk] = (df * (1.0 - lbv) * sig_f * (1.0 - sig_f)).astype(BF16)
        dp_ref[:, 0:hk] = (dq_scr[...] * (sig_q * (1.0 + qx * (1.0 - sig_q)))).astype(BF16)
        dlb_ref[...] += jnp.sum(df * (1.0 - sig_f), axis=0, keepdims=True)
        don_ref[...] += don

    def rev(i):
        return nt - 1 - i

    out = pl.pallas_call(
        body, grid=(nt,),
        in_specs=[pl.BlockSpec((t, 4 * hk), lambda i: (rev(i), 0)), pl.BlockSpec((1, hk), lambda i: (0, 0)),
                  pl.BlockSpec((1, dd), lambda i: (0, 0)), pl.BlockSpec((t, hk), lambda i: (rev(i), 0)),
                  pl.BlockSpec((None, hh, dd, dd), lambda i: (rev(i), 0, 0, 0)),
                  pl.BlockSpec((t, hk), lambda i: (rev(i), 0))] + r_in_specs,
        out_specs=[pl.BlockSpec((t, 4 * hk), lambda i: (rev(i), 0)), pl.BlockSpec((1, hk), lambda i: (0, 0)),
                   pl.BlockSpec((1, dd), lambda i: (0, 0))] + r_out_specs,
        out_shape=[jax.ShapeDtypeStruct((s, 4 * hk), BF16), jax.ShapeDtypeStruct((1, hk), F32),
                   jax.ShapeDtypeStruct((1, dd), F32)] + r_out,
        scratch_shapes=[pltpu.VMEM((hh, dd, dd), F32), pltpu.VMEM((3 * t, hk), F32), pltpu.VMEM((t, hk), F32),
                        pltpu.VMEM((t, hk), F32), pltpu.VMEM((t, hk), F32)] + r_scr,
        compiler_params=_params("arbitrary"), name=name)(proj, lb, onorm, o, states, dy, *r_in)
    return out[0], out[1], out[2], out[3:]


def _adamw_update(w, g, m, v):
    nm = ADAM_B1 * m + (1.0 - ADAM_B1) * g
    nv = ADAM_B2 * v + (1.0 - ADAM_B2) * (g * g)
    m_hat = nm / (1.0 - ADAM_B1 ** ADAM_STEP)
    v_hat = nv / (1.0 - ADAM_B2 ** ADAM_STEP)
    return -ADAM_LR * (m_hat / (jnp.sqrt(v_hat) + ADAM_EPS) + ADAM_WD * w), nm, nv


def _adamw(w, g, m, v, name):
    rows, cols = w.shape
    tr = _divisor_tile(rows, 256, 8)

    def body(w_ref, g_ref, m_ref, v_ref, d_ref, nm_ref, nv_ref):
        d_ref[...], nm_ref[...], nv_ref[...] = _adamw_update(w_ref[...], g_ref[...], m_ref[...], v_ref[...])

    blk = pl.BlockSpec((tr, cols), lambda i: (i, 0))
    shp = jax.ShapeDtypeStruct((rows, cols), F32)
    return pl.pallas_call(
        body, grid=(rows // tr,), in_specs=[blk] * 4, out_specs=(blk,) * 3, out_shape=(shp,) * 3,
        compiler_params=_params("parallel"), name=name)(w, g, m, v)


ADAMW_BLOCK_ELEMS = 128 * 1024


def _adamw_layers(w, lands, m, v, name, rider=None):
    ll, rows, cols = w.shape
    tr = _divisor_tile(rows, max(16, ADAMW_BLOCK_ELEMS // cols), 16)
    r_in, r_in_specs, r_out, r_out_specs, r_scr = _rider_specs(rider)
    n_rin = len(r_in)

    def body(w_ref, m_ref, v_ref, *refs):
        land_refs = refs[:ll]
        g_out, d_ref, nm_ref, nv_ref = refs[ll + n_rin:ll + n_rin + 4]
        layer = pl.program_id(0)
        _ride(rider, layer * (rows // tr) + pl.program_id(1), ll * (rows // tr), refs[ll:ll + n_rin] + refs[ll + n_rin + 4:])
        for k in range(ll):
            @pl.when(layer == k)
            def _(k=k):
                g = land_refs[k][0].astype(F32)
                for slot in range(1, N_DEV):
                    g = g + land_refs[k][slot].astype(F32)
                g_out[...] = g

        d_ref[...], nm_ref[...], nv_ref[...] = _adamw_update(w_ref[...], g_out[...], m_ref[...], v_ref[...])

    stacked = pl.BlockSpec((None, tr, cols), lambda l, i: (l, i, 0))

    def one(k):
        return pl.BlockSpec((N_DEV, tr, cols), lambda l, i: (0, jnp.where(l == k, i, 0), 0))

    shp = jax.ShapeDtypeStruct(w.shape, F32)
    out = pl.pallas_call(
        body, grid=(ll, rows // tr), in_specs=[stacked] * 3 + [one(k) for k in range(ll)] + r_in_specs,
        out_specs=[stacked] * 4 + r_out_specs, out_shape=[shp] * 4 + r_out, scratch_shapes=r_scr,
        compiler_params=_params("arbitrary", "arbitrary"), name=name)(w, m, v, *lands, *r_in)
    return out[0], out[1], out[2], out[3], out[4:]


_HBM = pl.BlockSpec(memory_space=pltpu.HBM)
_MESH = pl.DeviceIdType.MESH


class _GatherRide:
    def __init__(self, blocks, cuts):
        self.operands = list(blocks)
        self.cuts = list(cuts)
        self.out_shapes = []
        for b, cut in zip(blocks, cuts):
            r, c = b.shape
            shape = {"rows": (N_DEV * r, c), "cols": (r, N_DEV * c), "slots": (N_DEV, r, c)}[cut]
            self.out_shapes.append(jax.ShapeDtypeStruct(shape, b.dtype))
        n = len(blocks)
        self.scratch = [pltpu.SemaphoreType.DMA((7 * n,)), pltpu.SemaphoreType.DMA((7 * n,)), pltpu.SemaphoreType.DMA((n,))]

    def _parts(self, *refs):
        n = len(self.operands)
        x_refs, out_refs = refs[:n], refs[n:2 * n]
        send_sems, recv_sems, local_sems = refs[2 * n:]
        x, y, c = lax.axis_index("x"), lax.axis_index("y"), lax.axis_index("c")
        me, sibling = (x, y, c), (x, y, 1 - c)
        chips = [(1 - x, y), (x, 1 - y), (1 - x, 1 - y)]
        mine, first, passed, landed, from_sibling = [], [], [], [], []
        for e in range(n):
            x_ref, out_ref, cut = x_refs[e], out_refs[e], self.cuts[e]
            r, cc = x_ref.shape

            def place(px, py, pc, out_ref=out_ref, cut=cut, r=r, cc=cc):
                p = 4 * px + 2 * py + pc
                if cut == "rows":
                    return out_ref.at[pl.ds(pl.multiple_of(p * r, r), r), :]
                if cut == "cols":
                    return out_ref.at[:, pl.ds(pl.multiple_of(p * cc, cc), cc)]
                return out_ref.at[p]

            def copy(k, block, to, src=None, place=place, e=e):
                return pltpu.make_async_remote_copy(
                    src_ref=place(*block) if src is None else src, dst_ref=place(*block), send_sem=send_sems.at[7 * e + k],
                    recv_sem=recv_sems.at[7 * e + k], device_id=to, device_id_type=_MESH)

            mine.append(pltpu.make_async_copy(x_ref, place(*me), local_sems.at[e]))
            first += [copy(0, me, sibling, src=x_ref)] + [copy(1 + j, me, (*chip, c), src=x_ref) for j, chip in enumerate(chips)]
            passed += [copy(4 + j, (*chip, c), sibling) for j, chip in enumerate(chips)]
            landed += [copy(1 + j, (*chip, c), me) for j, chip in enumerate(chips)]
            from_sibling += [copy(0, sibling, me)] + [copy(4 + j, (*chip, 1 - c), me) for j, chip in enumerate(chips)]
        return mine, first, passed, landed, from_sibling

    def start(self, *refs):
        mine, first, _, _, _ = self._parts(*refs)
        for cp in mine + first:
            cp.start()

    def middle(self, *refs):
        _, _, passed, landed, _ = self._parts(*refs)
        for got, fwd in zip(landed, passed):
            got.wait_recv()
            fwd.start()

    def finish(self, *refs):
        mine, first, passed, _, from_sibling = self._parts(*refs)
        for cp in from_sibling:
            cp.wait_recv()
        for cp in first + passed:
            cp.wait_send()
        for cp in mine:
            cp.wait()


class _ExchangeRide:
    def __init__(self, sends):
        self.operands = list(sends)
        self.out_shapes = [jax.ShapeDtypeStruct(s.shape, s.dtype) for s in sends]
        n = len(sends)
        self.scratch = [pltpu.SemaphoreType.DMA((7 * n,)), pltpu.SemaphoreType.DMA((7 * n,)), pltpu.SemaphoreType.DMA((n,))]

    def _parts(self, *refs):
        n = len(self.operands)
        s_refs, land_refs = refs[:n], refs[n:2 * n]
        send_sems, recv_sems, local_sems = refs[2 * n:]
        x, y, c = lax.axis_index("x"), lax.axis_index("y"), lax.axis_index("c")
        me = 4 * x + 2 * y + c
        own, sends, recvs = [], [], []
        for e in range(n):
            s_ref, land_ref = s_refs[e], land_refs[e]
            own.append(pltpu.make_async_copy(s_ref.at[me], land_ref.at[me], local_sems.at[e]))
            for rel in range(1, N_DEV):
                px = 1 - x if rel & 4 else x
                py = 1 - y if rel & 2 else y
                pc = 1 - c if rel & 1 else c
                peer = 4 * px + 2 * py + pc
                k = 7 * e + rel - 1
                sends.append(pltpu.make_async_remote_copy(
                    src_ref=s_ref.at[peer], dst_ref=land_ref.at[me], send_sem=send_sems.at[k], recv_sem=recv_sems.at[k],
                    device_id=(px, py, pc), device_id_type=_MESH))
                recvs.append(pltpu.make_async_remote_copy(
                    src_ref=s_ref.at[me], dst_ref=land_ref.at[peer], send_sem=send_sems.at[k], recv_sem=recv_sems.at[k],
                    device_id=(px, py, pc), device_id_type=_MESH))
        return own, sends, recvs

    def start(self, *refs):
        own, sends, _ = self._parts(*refs)
        for cp in own + sends:
            cp.start()

    def middle(self, *refs):
        pass

    def finish(self, *refs):
        own, sends, recvs = self._parts(*refs)
        for cp in recvs:
            cp.wait_recv()
        for cp in sends:
            cp.wait_send()
        for cp in own:
            cp.wait()


def _run_alone(rider, name):
    def body(*refs):
        rider.start(*refs)
        rider.middle(*refs)
        rider.finish(*refs)

    return pl.pallas_call(
        body, out_shape=rider.out_shapes, in_specs=[_HBM] * len(rider.operands), out_specs=[_HBM] * len(rider.out_shapes),
        scratch_shapes=rider.scratch, name=name)(*rider.operands)


def _all_gather(xs, name):
    return _run_alone(_GatherRide([xs], ["slots"]), name)[0]


def _sum_slots(parts, name):
    _, rows, cols = parts.shape
    tr = _divisor_tile(rows, 256, 16)

    def body(p_ref, o_ref):
        acc = p_ref[0].astype(F32)
        for slot in range(1, N_DEV):
            acc = acc + p_ref[slot].astype(F32)
        o_ref[...] = acc

    return pl.pallas_call(
        body, grid=(rows // tr,), in_specs=[pl.BlockSpec((N_DEV, tr, cols), lambda i: (0, i, 0))],
        out_specs=pl.BlockSpec((tr, cols), lambda i: (i, 0)), out_shape=jax.ShapeDtypeStruct((rows, cols), F32),
        compiler_params=_params("parallel"), name=name)(parts)


def _carry(rode, key, riders, call):
    rider = riders.get(key)
    res = call(rider)
    if rider is None:
        return res
    res, rode[key] = res
    return res


def _kept(rode, key, riders, brought):
    if key in riders:
        rode[key] = brought


def _mlp_fwd(h, g_pre, g_post, w1, w2, tag, riders):
    rode = {}
    tm = _row_tile(h.shape[0], 2048)
    (a, r2), brought = _norm_mm(h, g_pre, w1, tm, 1024, f"{tag}_up", out_dtypes=(BF16,),
                                epi=lambda acc: (jnp.square(jnp.maximum(acc, 0.0)),), rider=riders.get("up"))
    _kept(rode, "up", riders, brought)
    z, out, brought = _mm_norm_res(r2, w2, g_post, h, 512, f"{tag}_down", riders.get("down"))
    _kept(rode, "down", riders, brought)
    return out, (h, a, r2, z), rode


def _mlp_bwd(dh, saved, g_pre, g_post, w1, w2, tag, riders):
    h, a, r2, z = saved
    rode = {}
    tm = _row_tile(h.shape[0], 1024)
    (dz, dg_post, du), brought = _rmsbwd_mm(
        z, g_post, dh, w2, tm, 1024, f"{tag}_ddown", out_dtypes=(BF16,), extras=(r2,),
        epi=lambda acc, rr: (acc * (2.0 * jnp.sqrt(rr.astype(F32))),), rider=riders.get("ddown"))
    _kept(rode, "ddown", riders, brought)
    dw2 = _carry(rode, "dw2", riders, lambda r: _mm(
        r2, dz, "tn", 512, 1024, f"{tag}_dw2", out_dtypes=(BF16,), shard="rows", rider=r))
    dw1 = _mm(a, du, "tn", 1024, 512, f"{tag}_dw1", out_dtypes=(BF16,), shard="cols")
    dh_in, dg_pre, _ = _mm_rmsbwd_res(du, w1, h, g_pre, dh, 512, f"{tag}_dup")
    return dh_in, dg_pre, dg_post, dw1, dw2, rode


def _hgrn_layer_fwd(h, g_pre, g_post, lb, onorm, w_in, w_o, tag, riders):
    rode = {}
    (a, proj), brought = _norm_mm(h, g_pre, w_in, _row_tile(h.shape[0], 2048), 1024, f"{tag}_in", rider=riders.get("in"))
    _kept(rode, "in", riders, brought)
    y, o, states, brought = _hgrn_fwd(proj, lb, onorm, f"{tag}_scan", riders.get("scan"))
    _kept(rode, "scan", riders, brought)
    m, out, _ = _mm_norm_res(y, w_o, g_post, h, 512, f"{tag}_o")
    return out, (h, a, proj, y, o, states, m), rode


def _hgrn_layer_bwd(dh, saved, g_pre, g_post, lb, onorm, w_in, w_o, tag, rider=None):
    h, a, proj, y, o, states, m = saved
    (dm, dg_post, dy), _ = _rmsbwd_mm(m, g_post, dh, w_o, 512, 1024, f"{tag}_do")
    dw_o = _mm(y, dm, "tn", 128, 1024, f"{tag}_dwo", out_dtypes=(BF16,), shard="rows")
    dproj, dlb, donorm, rode = _hgrn_bwd(proj, lb, onorm, o, states, dy, f"{tag}_dscan", rider)
    dw_in = _mm(a, dproj, "tn", 1024, 512, f"{tag}_dwin", out_dtypes=(BF16,), shard="cols")
    dh_in, dg_pre, _ = _mm_rmsbwd_res(dproj, w_in, h, g_pre, dh, 512, f"{tag}_din")
    return dh_in, dg_pre, dg_post, dlb, donorm, dw_in, dw_o, rode


def _mla_layer_fwd(h, g_pre, g_post, cos, sin, w_in, late, qn, kvn, tag, riders):
    rode = {}
    (a, proj), brought = _norm_mm(h, g_pre, w_in, 512, MLA_IN, f"{tag}_in", rider=riders.get("in"))
    _kept(rode, "in", riders, brought)
    w_uq, w_ukv, _ = late(rode)
    cqn, ckvn, q, k, v = _mla_qkv(proj, qn, kvn, w_uq, w_ukv, cos, sin, f"{tag}_qkv")
    o, lse, brought = _attn_fwd(q, k, v, f"{tag}_attn", riders.get("attn"))
    _kept(rode, "attn", riders, brought)
    _, _, w_o = late(rode)
    m, out, _ = _mm_norm_res(o, w_o, g_post, h, 512, f"{tag}_o")
    return out, (h, a, proj, cqn, ckvn, q, k, v, o, lse, m), rode


def _mla_layer_bwd(dh, saved, g_pre, g_post, cos, sin, w_in, qn, kvn, w_uq, w_ukv, w_o, tag, rider=None, own_ride=False):
    h, a, proj, cqn, ckvn, q, k, v, o, lse, m = saved
    hh, s = q.shape[0], q.shape[1]
    own = {}
    (dm, dg_post, do, delta), _ = _rmsbwd_mm(m, g_post, dh, w_o, 512, 1024, f"{tag}_do", group_sums=(o, hh))
    dw_o = _mm(o, dm, "tn", 128, 1024, f"{tag}_dwo", out_dtypes=(BF16,), shard="rows")
    if own_ride:
        rider = _ExchangeRide(list(rider.operands) + [dw_o])
    dq, dk, dv, rode = _attn_bwd(q, k, v, do, lse.reshape(hh, 1, s), delta, f"{tag}_dattn", rider)
    if own_ride:
        own["mla_w_o"] = rode[-1]
    dqe, dkve, dproj, dqn, dkvn = _mla_bwd_mid(dq, dk, dv, cos, sin, proj, qn, kvn, w_uq, w_ukv, f"{tag}_dqkv")
    dw_uq = _mm(cqn, dqe, "tn", MLA_Q_LORA, 768, f"{tag}_dwuq", out_dtypes=(BF16,))
    dw_uq = dw_uq.reshape(MLA_Q_LORA, N_DEV, -1).transpose(1, 0, 2)
    dw_ukv = _mm(ckvn, dkve, "tn", MLA_KV_LORA, 256, f"{tag}_dwukv", out_dtypes=(BF16,), shard="cols")
    dw_in = _mm(a, dproj, "tn", 128, MLA_IN, f"{tag}_dwin", out_dtypes=(BF16,), shard="rows",
                rider=_ExchangeRide([dw_uq, dw_ukv]) if own_ride else None)
    if own_ride:
        dw_in, (own["mla_w_uq"], own["mla_w_ukv"]) = dw_in
    dh_in, dg_pre, brought = _mm_rmsbwd_res(dproj, w_in, h, g_pre, dh, 512, f"{tag}_din",
                                             _ExchangeRide([dw_in]) if own_ride else None)
    if own_ride:
        own["mla_w_in"] = brought[0]
    return dh_in, dg_pre, dg_post, dqn, dkvn, dw_in, dw_uq, dw_ukv, dw_o, rode, own


_CUT = dict(mla_w_in="rows", mla_w_uq="cols", mla_w_ukv="cols", mla_w_o="rows", hgrn_w_in="cols", hgrn_w_o="rows",
            mlp_w1="cols", mlp_w2="rows")


def _unit(layer, kind):
    slot = layer // 2
    if kind == "mla":
        return [("mla_w_in", slot), ("mla_w_uq", slot), ("mla_w_ukv", slot), ("mla_w_o", slot)]
    if kind == "hgrn":
        return [("hgrn_w_in", slot), ("hgrn_w_o", slot)]
    return [("mlp_w1", layer), ("mlp_w2", layer)]


_GATHER_FIRST = [("mla_w_in", 0)]
_GATHER_PLAN = {
    (0, "in"): [("mla_w_uq", 0), ("mla_w_ukv", 0)],
    (0, "attn"): [("mla_w_o", 0)] + _unit(0, "mlp") + _unit(1, "hgrn"),
    (0, "up"): [("mlp_w1", 1)],
    (0, "down"): [("mlp_w2", 1)],
    (1, "in"): _unit(2, "mla"),
    (1, "scan"): _unit(2, "mlp"),
    (2, "attn"): _unit(3, "hgrn") + _unit(3, "mlp"),
}
_EXCHANGE_PLAN = {
    (3, "dscan"): _unit(3, "mlp"),
    (2, "ddown"): [("hgrn_w_in", 1)],
    (2, "dw2"): [("hgrn_w_o", 1)],
    (2, "dattn"): _unit(2, "mlp"),
    (1, "ddown"): _unit(2, "mla"),
    (1, "dscan"): _unit(1, "mlp"),
    (0, "ddown"): [("hgrn_w_in", 0)],
    (0, "dw2"): [("hgrn_w_o", 0)],
    (0, "dattn"): _unit(0, "mlp"),
}


def _gather_cut(name):
    return "slots" if name == "mla_w_uq" else _CUT[name]


def _gather_rider(weights, ents):
    return _GatherRide([weights[name][idx].astype(BF16) for name, idx in ents], [_gather_cut(name) for name, _ in ents])


def _gathered(outs, ents):
    res = {}
    for (name, idx), out in zip(ents, outs):
        if _gather_cut(name) == "slots":
            out = out.transpose(1, 0, 2).reshape(out.shape[1], -1)
        res[(name, idx)] = out
    return res


def _adamw_nd(w, g, m, v, name):
    shape = w.shape
    c = shape[-1]
    d, nm, nv = _adamw(w.reshape(-1, c), g.reshape(-1, c), m.reshape(-1, c), v.reshape(-1, c), name)
    return d.reshape(shape), nm.reshape(shape), nv.reshape(shape)


def kernel(x, positions, norm_gains, mla_w_in, mla_q_norm, mla_kv_norm, mla_w_uq, mla_w_ukv, mla_w_o, hgrn_w_in, hgrn_lb_logits, hgrn_o_norm, hgrn_w_o, mlp_w1, mlp_w2, loss_target, m_norm_gains, m_mla_w_in, m_mla_q_norm, m_mla_kv_norm, m_mla_w_uq, m_mla_w_ukv, m_mla_w_o, m_hgrn_w_in, m_hgrn_lb_logits, m_hgrn_o_norm, m_hgrn_w_o, m_mlp_w1, m_mlp_w2, v_norm_gains, v_mla_w_in, v_mla_q_norm, v_mla_kv_norm, v_mla_w_uq, v_mla_w_ukv, v_mla_w_o, v_hgrn_w_in, v_hgrn_lb_logits, v_hgrn_o_norm, v_hgrn_w_o, v_mlp_w1, v_mlp_w2):
    weights = dict(norm_gains=norm_gains, mla_w_in=mla_w_in, mla_q_norm=mla_q_norm, mla_kv_norm=mla_kv_norm,
                   mla_w_uq=mla_w_uq, mla_w_ukv=mla_w_ukv, mla_w_o=mla_w_o, hgrn_w_in=hgrn_w_in,
                   hgrn_lb_logits=hgrn_lb_logits, hgrn_o_norm=hgrn_o_norm, hgrn_w_o=hgrn_w_o, mlp_w1=mlp_w1, mlp_w2=mlp_w2)
    mom_m = dict(norm_gains=m_norm_gains, mla_w_in=m_mla_w_in, mla_q_norm=m_mla_q_norm, mla_kv_norm=m_mla_kv_norm,
                 mla_w_uq=m_mla_w_uq, mla_w_ukv=m_mla_w_ukv, mla_w_o=m_mla_w_o, hgrn_w_in=m_hgrn_w_in,
                 hgrn_lb_logits=m_hgrn_lb_logits, hgrn_o_norm=m_hgrn_o_norm, hgrn_w_o=m_hgrn_w_o, mlp_w1=m_mlp_w1, mlp_w2=m_mlp_w2)
    mom_v = dict(norm_gains=v_norm_gains, mla_w_in=v_mla_w_in, mla_q_norm=v_mla_q_norm, mla_kv_norm=v_mla_kv_norm,
                 mla_w_uq=v_mla_w_uq, mla_w_ukv=v_mla_w_ukv, mla_w_o=v_mla_w_o, hgrn_w_in=v_hgrn_w_in,
                 hgrn_lb_logits=v_hgrn_lb_logits, hgrn_o_norm=v_hgrn_o_norm, hgrn_w_o=v_hgrn_w_o, mlp_w1=v_mlp_w1, mlp_w2=v_mlp_w2)
    order = list(weights)
    seq = x.shape[1]
    h = x.reshape(seq, D_MODEL)
    target = loss_target.reshape(seq, D_MODEL)

    full = _gathered(_run_alone(_gather_rider(weights, _GATHER_FIRST), "gather_first"), _GATHER_FIRST)
    gains = _all_gather(norm_gains.reshape(DEPTH * 4, D_MODEL // N_DEV), "gather_gains")
    gains = gains.transpose(1, 0, 2).reshape(DEPTH, 4, 1, D_MODEL)

    def gather_riders(layer, keys):
        return {k: _gather_rider(weights, _GATHER_PLAN[(layer, k)]) for k in keys if (layer, k) in _GATHER_PLAN}

    seen = set()

    def arrived(layer, rode):
        for k, outs in rode.items():
            if (layer, k) not in seen:
                seen.add((layer, k))
                full.update(_gathered(outs, _GATHER_PLAN[(layer, k)]))

    def late_weights(layer):
        def late(rode):
            arrived(layer, rode)
            return tuple(full.get((name, layer // 2)) for name in ("mla_w_uq", "mla_w_ukv", "mla_w_o"))
        return late

    cos, sin = _rope_tables(positions.reshape(seq, 1), "rope_tables")
    lower = _lb_fwd(hgrn_lb_logits, "lower_bounds")

    def mixer_args(layer):
        slot = layer // 2
        if layer % 2 == 0:
            return (cos, sin, full[("mla_w_in", slot)], mla_q_norm[slot:slot + 1], mla_kv_norm[slot:slot + 1],
                    full[("mla_w_uq", slot)], full[("mla_w_ukv", slot)], full[("mla_w_o", slot)])
        return (lower[layer:layer + 1], hgrn_o_norm[slot:slot + 1], full[("hgrn_w_in", slot)], full[("hgrn_w_o", slot)])

    saved = []
    for layer in range(DEPTH):
        g = gains[layer]
        if layer % 2 == 0:
            slot = layer // 2
            h, sv_mix, rode = _mla_layer_fwd(
                h, g[0], g[1], cos, sin, full[("mla_w_in", slot)], late_weights(layer), mla_q_norm[slot:slot + 1],
                mla_kv_norm[slot:slot + 1], f"l{layer}_mla", gather_riders(layer, ["in", "attn"]))
            arrived(layer, rode)
        else:
            h, sv_mix, rode = _hgrn_layer_fwd(h, g[0], g[1], *mixer_args(layer), f"l{layer}_hgrn",
                                              gather_riders(layer, ["in", "scan"]))
            arrived(layer, rode)
        h, sv_mlp, rode = _mlp_fwd(h, g[2], g[3], full[("mlp_w1", layer)], full[("mlp_w2", layer)], f"l{layer}_mlp",
                                   gather_riders(layer, ["up", "down"]))
        arrived(layer, rode)
        saved.append((sv_mix, sv_mlp))

    loss_part, dh = _loss(h, target, "loss")
    loss = lax.psum(loss_part[0, 0], AXES)

    zero_row = jnp.zeros((1, D_MODEL), F32)
    dgains = [[None] * 4 for _ in range(DEPTH)]
    dlower = [zero_row] * DEPTH
    partials, lands = {}, {}
    dqn, dkvn, donorm = [None] * 2, [None] * 2, [None] * 2

    def exchange_riders(layer, keys):
        return {k: _ExchangeRide([partials[e] for e in _EXCHANGE_PLAN[(layer, k)]]) for k in keys
                if (layer, k) in _EXCHANGE_PLAN}

    def landed(layer, rode):
        for k, outs in rode.items():
            lands.update(zip(_EXCHANGE_PLAN[(layer, k)], outs))

    for layer in range(DEPTH - 1, -1, -1):
        slot = layer // 2
        g = gains[layer]
        sv_mix, sv_mlp = saved[layer]
        dh, dgains[layer][2], dgains[layer][3], partials[("mlp_w1", layer)], partials[("mlp_w2", layer)], rode = _mlp_bwd(
            dh, sv_mlp, g[2], g[3], full[("mlp_w1", layer)], full[("mlp_w2", layer)], f"l{layer}_mlp",
            exchange_riders(layer, ["ddown", "dw2"]))
        landed(layer, rode)
        key = "dattn" if layer % 2 == 0 else "dscan"
        rider = exchange_riders(layer, [key]).get(key)
        if layer % 2 == 0:
            (dh, dgains[layer][0], dgains[layer][1], dqn[slot], dkvn[slot], partials[("mla_w_in", slot)],
             partials[("mla_w_uq", slot)], partials[("mla_w_ukv", slot)], partials[("mla_w_o", slot)], brought,
             own) = _mla_layer_bwd(dh, sv_mix, g[0], g[1], *mixer_args(layer), f"l{layer}_mla", rider, own_ride=(layer == 0))
            lands.update({(name, slot): land for name, land in own.items()})
        else:
            (dh, dgains[layer][0], dgains[layer][1], dlower[layer], donorm[slot], partials[("hgrn_w_in", slot)],
             partials[("hgrn_w_o", slot)], brought) = _hgrn_layer_bwd(dh, sv_mix, g[0], g[1], *mixer_args(layer), f"l{layer}_hgrn", rider)
        landed(layer, {key: brought} if rider is not None else {})
    grad_x = dh.reshape(x.shape)
    dlogits = _lb_bwd(hgrn_lb_logits, jnp.concatenate(dlower, axis=0), "lower_bounds_bwd")

    pad = jnp.zeros((1, D_MODEL - 2 * MLA_KV_LORA), F32)
    pad2 = jnp.zeros((1, D_MODEL - 2 * HGRN_D), F32)
    small = jnp.concatenate(
        [jnp.concatenate([gg for row in dgains for gg in row], axis=0), jnp.concatenate(dqn, axis=1),
         jnp.concatenate(dkvn + [pad], axis=1), dlogits, jnp.concatenate(donorm + [pad2], axis=1), zero_row], axis=0)
    small = _sum_slots(_all_gather(small, "gather_small_grads"), "sum_small_grads")
    me = 4 * lax.axis_index("x") + 2 * lax.axis_index("y") + lax.axis_index("c")
    n_g = DEPTH * 4
    width = D_MODEL // N_DEV
    grads = {}
    grads["norm_gains"] = lax.dynamic_slice(small[:n_g], (0, me * width), (n_g, width)).reshape(DEPTH, 4, width)
    grads["mla_q_norm"] = small[n_g].reshape(2, MLA_Q_LORA)
    grads["mla_kv_norm"] = small[n_g + 1, :2 * MLA_KV_LORA].reshape(2, MLA_KV_LORA)
    grads["hgrn_lb_logits"] = small[n_g + 2:n_g + 2 + DEPTH]
    grads["hgrn_o_norm"] = small[n_g + 2 + DEPTH, :2 * HGRN_D].reshape(2, HGRN_D)

    deltas, new_m, new_v = {}, {}, {}
    for name in order:
        if name in _CUT:
            per_layer = [lands[(name, idx)] for idx in range(weights[name].shape[0])]
            grads[name], deltas[name], new_m[name], new_v[name], _ = _adamw_layers(
                weights[name], per_layer, mom_m[name], mom_v[name], f"adamw_{name}")
        else:
            deltas[name], new_m[name], new_v[name] = _adamw_nd(weights[name], grads[name], mom_m[name], mom_v[name], f"adamw_{name}")
    return (loss, grad_x, *[grads[n] for n in order], *[deltas[n] for n in order], *[new_m[n] for n in order],
            *[new_v[n] for n in order])
```

```python
import numpy as np
import jax
import jax.numpy as jnp
from jax import lax
from jax.experimental import pallas as pl
from jax.experimental.pallas import tpu as pltpu

F32, BF16 = jnp.float32, jnp.bfloat16

N_DEV = 8
AXES = ("x", "y", "c")
D_MODEL = 1024
DEPTH = 4
MLA_HEADS = 8
MLA_Q_LORA = 512
MLA_KV_LORA = 256
MLA_NOPE = 128
MLA_ROPE = 64
MLA_V = 128
MLA_QK = MLA_NOPE + MLA_ROPE
MLA_IN = MLA_Q_LORA + MLA_KV_LORA + MLA_ROPE
ROPE_BASE = 10000.0
HGRN_HEADS = 8
HGRN_D = 128
HGRN_CHUNK = 32
D_FF = 4 * D_MODEL
EPS = 1e-6
LOG2_E = 1.4426950408889634
ADAM_LR, ADAM_B1, ADAM_B2, ADAM_EPS, ADAM_WD, ADAM_STEP = 0.001, 0.9, 0.999, 1e-08, 0.01, 10

V7X_VMEM_LIMIT_BYTES = 56 * 1024 * 1024

NN = (((1,), (0,)), ((), ()))
NT = (((1,), (1,)), ((), ()))
TN = (((0,), (0,)), ((), ()))
_DIMS = {"nn": NN, "nt": NT, "tn": TN}


def _params(*sem):
    return pltpu.CompilerParams(dimension_semantics=sem, vmem_limit_bytes=V7X_VMEM_LIMIT_BYTES)


def _dot(a, b, dims=NN):
    return lax.dot_general(a, b, dims, preferred_element_type=F32)


def _dot_select(sel, x, pieces, dims=NN):
    sel = sel.astype(BF16)
    acc, rest = None, x
    for _ in range(pieces):
        term = rest.astype(BF16)
        part = _dot(sel, term, dims)
        acc = part if acc is None else acc + part
        rest = rest - term.astype(F32)
    return acc


def _rstd(x):
    return lax.rsqrt(jnp.mean(x * x, axis=-1, keepdims=True) + EPS)


def _rms_bwd_rows(x, g, dy):
    r = _rstd(x)
    xh = x * r
    dyg = dy * g
    dx = r * (dyg - xh * jnp.mean(dyg * xh, axis=-1, keepdims=True))
    return dx, dy * xh


def _row_tile(n, want):
    t = min(n, want)
    assert n % t == 0, (n, t)
    return t


def _divisor_tile(n, cap, mult):
    for t in range(min(cap, n) - min(cap, n) % mult, 0, -mult):
        if n % t == 0:
            return t
    return n


def _rms_fwd(x, g, res, out_dtype, name):
    s, d = x.shape
    ts = _row_tile(s, 512)

    def body(x_ref, g_ref, *rest):
        xf = x_ref[...]
        y = xf * _rstd(xf) * g_ref[...]
        if res is not None:
            y = rest[0][...] + y
        rest[-1][...] = y.astype(out_dtype)

    row = pl.BlockSpec((ts, d), lambda i: (i, 0))
    vec = pl.BlockSpec((1, d), lambda i: (0, 0))
    ins = [x, g] + ([res] if res is not None else [])
    return pl.pallas_call(
        body, grid=(s // ts,), in_specs=[row, vec] + ([row] if res is not None else []), out_specs=row,
        out_shape=jax.ShapeDtypeStruct((s, d), out_dtype), compiler_params=_params("parallel"), name=name)(*ins)


def _rms_bwd(x, g, dy, res, out_dtype, name):
    s, d = x.shape
    ts = _row_tile(s, 512)

    def body(x_ref, g_ref, dy_ref, *rest):
        dx_ref, dg_ref = rest[-2:]
        dx, dg = _rms_bwd_rows(x_ref[...], g_ref[...], dy_ref[...].astype(F32))
        if res is not None:
            dx = rest[0][...] + dx
        dx_ref[...] = dx.astype(out_dtype)

        @pl.when(pl.program_id(0) == 0)
        def _():
            dg_ref[...] = jnp.zeros_like(dg_ref)

        dg_ref[...] += jnp.sum(dg, axis=0, keepdims=True)

    row = pl.BlockSpec((ts, d), lambda i: (i, 0))
    vec = pl.BlockSpec((1, d), lambda i: (0, 0))
    ins = [x, g, dy] + ([res] if res is not None else [])
    return pl.pallas_call(
        body, grid=(s // ts,), in_specs=[row, vec, row] + ([row] if res is not None else []), out_specs=(row, vec),
        out_shape=(jax.ShapeDtypeStruct((s, d), out_dtype), jax.ShapeDtypeStruct((1, d), F32)),
        compiler_params=_params("arbitrary"), name=name)(*ins)


def _mm(a, b, mode, tm, tn, name, out_dtypes=(F32,), shard=None, epi=None, extras=(), rider=None):
    if mode == "tn":
        k, m = a.shape
        a_spec = pl.BlockSpec((k, tm), lambda i, j: (0, i))
    else:
        m, k = a.shape
        a_spec = pl.BlockSpec((tm, k), lambda i, j: (i, 0))
    if mode == "nt":
        n = b.shape[0]
        b_spec = pl.BlockSpec((tn, k), lambda i, j: (j, 0))
    else:
        n = b.shape[1]
        b_spec = pl.BlockSpec((k, tn), lambda i, j: (0, j))
    assert m % tm == 0 and n % tn == 0, (name, m, tm, n, tn)
    tile = pl.BlockSpec((tm, tn), lambda i, j: (i, j))
    if shard == "rows":
        per = m // N_DEV // tm
        out_specs = [pl.BlockSpec((None, tm, tn), lambda i, j: (i // per, i % per, j))]
        out_shape = [jax.ShapeDtypeStruct((N_DEV, m // N_DEV, n), out_dtypes[0])]
    elif shard == "cols":
        per = n // N_DEV // tn
        out_specs = [pl.BlockSpec((None, tm, tn), lambda i, j: (j // per, i, j % per))]
        out_shape = [jax.ShapeDtypeStruct((N_DEV, m, n // N_DEV), out_dtypes[0])]
    else:
        out_specs = [tile for _ in out_dtypes]
        out_shape = [jax.ShapeDtypeStruct((m, n), dt) for dt in out_dtypes]
    n_ex, n_out = len(extras), len(out_shape)
    r_in, r_in_specs, r_out, r_out_specs, r_scr = _rider_specs(rider)
    n_rin, n_rout = len(r_in), len(r_out)
    grid = (m // tm, n // tn)

    def body(a_ref, b_ref, *refs):
        ex_refs = refs[:n_ex]
        o_refs = refs[n_ex + n_rin:n_ex + n_rin + n_out]
        r_refs = refs[n_ex:n_ex + n_rin] + refs[n_ex + n_rin + n_out:]
        _ride(rider, pl.program_id(0) * grid[1] + pl.program_id(1), grid[0] * grid[1], r_refs)
        acc = _dot(a_ref[...].astype(BF16), b_ref[...].astype(BF16), _DIMS[mode])
        vals = (acc,) if epi is None else epi(acc, *[r[...] for r in ex_refs])
        for o_ref, val in zip(o_refs, vals):
            o_ref[...] = val.astype(o_ref.dtype)

    sem = ("parallel", "parallel") if rider is None else ("arbitrary", "arbitrary")
    out = pl.pallas_call(
        body, grid=grid, in_specs=[a_spec, b_spec] + [tile] * n_ex + r_in_specs, out_specs=out_specs + r_out_specs,
        out_shape=out_shape + r_out, scratch_shapes=r_scr, compiler_params=_params(*sem), name=name)(a, b, *extras, *r_in)
    res = out[0] if n_out == 1 else out[:n_out]
    return res if rider is None else (res, out[n_out:])


def _norm_mm(x, g, b, tm, tn, name, out_dtypes=(F32,), epi=None, rider=None):
    m, k = x.shape
    n = b.shape[1]
    assert m % tm == 0 and n % tn == 0, (name, m, tm, n, tn)
    grid = (m // tm, n // tn)
    n_out = len(out_dtypes)
    r_in, r_in_specs, r_out, r_out_specs, r_scr = _rider_specs(rider)
    n_rin = len(r_in)

    def body(x_ref, g_ref, b_ref, *refs):
        a_ref = refs[n_rin]
        o_refs = refs[n_rin + 1:n_rin + 1 + n_out]
        _ride(rider, pl.program_id(0) * grid[1] + pl.program_id(1), grid[0] * grid[1], refs[:n_rin] + refs[n_rin + 1 + n_out:])

        @pl.when(pl.program_id(1) == 0)
        def _():
            xf = x_ref[...]
            a_ref[...] = (xf * _rstd(xf) * g_ref[...]).astype(BF16)

        acc = _dot(a_ref[...], b_ref[...].astype(BF16))
        vals = (acc,) if epi is None else epi(acc)
        for o_ref, val in zip(o_refs, vals):
            o_ref[...] = val.astype(o_ref.dtype)

    row = pl.BlockSpec((tm, k), lambda i, j: (i, 0))
    tile = pl.BlockSpec((tm, tn), lambda i, j: (i, j))
    out = pl.pallas_call(
        body, grid=grid,
        in_specs=[row, pl.BlockSpec((1, k), lambda i, j: (0, 0)), pl.BlockSpec((k, tn), lambda i, j: (0, j))] + r_in_specs,
        out_specs=[row] + [tile] * n_out + r_out_specs,
        out_shape=[jax.ShapeDtypeStruct((m, k), BF16)] + [jax.ShapeDtypeStruct((m, n), dt) for dt in out_dtypes] + r_out,
        scratch_shapes=r_scr, compiler_params=_params("arbitrary", "arbitrary"), name=name)(x, g, b, *r_in)
    return out[:1 + n_out], out[1 + n_out:]


def _mm_norm_res(a, b, g, res, tm, name, rider=None):
    m, k = a.shape
    n = b.shape[1]
    assert m % tm == 0, (name, m, tm)
    r_in, r_in_specs, r_out, r_out_specs, r_scr = _rider_specs(rider)
    n_rin = len(r_in)

    def body(a_ref, b_ref, g_ref, res_ref, *refs):
        z_ref, o_ref = refs[n_rin:n_rin + 2]
        _ride(rider, pl.program_id(0), m // tm, refs[:n_rin] + refs[n_rin + 2:])
        z = _dot(a_ref[...].astype(BF16), b_ref[...].astype(BF16))
        z_ref[...] = z
        o_ref[...] = res_ref[...] + z * _rstd(z) * g_ref[...]

    row = pl.BlockSpec((tm, n), lambda i: (i, 0))
    out = pl.pallas_call(
        body, grid=(m // tm,),
        in_specs=[pl.BlockSpec((tm, k), lambda i: (i, 0)), pl.BlockSpec((k, n), lambda i: (0, 0)),
                  pl.BlockSpec((1, n), lambda i: (0, 0)), row] + r_in_specs,
        out_specs=[row, row] + r_out_specs,
        out_shape=[jax.ShapeDtypeStruct((m, n), F32), jax.ShapeDtypeStruct((m, n), F32)] + r_out,
        scratch_shapes=r_scr, compiler_params=_params("arbitrary"), name=name)(a, b, g, res, *r_in)
    return out[0], out[1], out[2:]


def _rmsbwd_mm(x, g, dy, b, tm, tn, name, out_dtypes=(F32,), epi=None, extras=(), rider=None, group_sums=None):
    m, k = x.shape
    n = b.shape[0]
    assert m % tm == 0 and n % tn == 0, (name, m, tm, n, tn)
    grid = (m // tm, n // tn)
    n_ex, n_out = len(extras), len(out_dtypes)
    r_in, r_in_specs, r_out, r_out_specs, r_scr = _rider_specs(rider)
    n_rin = len(r_in)
    n_gs = 0 if group_sums is None else 1
    if n_gs:
        assert tn == n and epi is None
        extras = tuple(extras) + (group_sums[0],)
        n_ex += 1

    def body(x_ref, g_ref, dy_ref, b_ref, *refs):
        ex_refs = refs[:n_ex]
        dx_ref, dg_ref = refs[n_ex + n_rin:n_ex + n_rin + 2]
        o_refs = refs[n_ex + n_rin + 2:n_ex + n_rin + 2 + n_out]
        i, j = pl.program_id(0), pl.program_id(1)
        _ride(rider, i * grid[1] + j, grid[0] * grid[1], refs[n_ex:n_ex + n_rin] + refs[n_ex + n_rin + 2 + n_out + n_gs:])

        @pl.when((i == 0) & (j == 0))
        def _():
            dg_ref[...] = jnp.zeros_like(dg_ref)

        @pl.when(j == 0)
        def _():
            dx, dg = _rms_bwd_rows(x_ref[...], g_ref[...], dy_ref[...])
            dx_ref[...] = dx.astype(BF16)
            dg_ref[...] += jnp.sum(dg, axis=0, keepdims=True)

        acc = _dot(dx_ref[...], b_ref[...].astype(BF16), NT)
        if n_gs:
            groups = group_sums[1]
            col = lax.broadcasted_iota(jnp.int32, (groups, n), 1) // (n // groups)
            sel = jnp.where(col == lax.broadcasted_iota(jnp.int32, (groups, n), 0), 1.0, 0.0)
            refs[n_ex + n_rin + 2 + n_out][...] = _dot_select(sel, acc * ex_refs[-1][...], 3, NT)
            vals = (acc,)
        else:
            vals = (acc,) if epi is None else epi(acc, *[r[...] for r in ex_refs])
        for o_ref, val in zip(o_refs, vals):
            o_ref[...] = val.astype(o_ref.dtype)

    row = pl.BlockSpec((tm, k), lambda i, j: (i, 0))
    vec = pl.BlockSpec((1, k), lambda i, j: (0, 0))
    tile = pl.BlockSpec((tm, tn), lambda i, j: (i, j))
    gs_specs = [pl.BlockSpec((group_sums[1], tm), lambda i, j: (0, i))] if n_gs else []
    gs_shape = [jax.ShapeDtypeStruct((group_sums[1], m), F32)] if n_gs else []
    out = pl.pallas_call(
        body, grid=grid,
        in_specs=[row, vec, row, pl.BlockSpec((tn, k), lambda i, j: (j, 0))] + [tile] * n_ex + r_in_specs,
        out_specs=[row, vec] + [tile] * n_out + gs_specs + r_out_specs,
        out_shape=[jax.ShapeDtypeStruct((m, k), BF16), jax.ShapeDtypeStruct((1, k), F32)]
        + [jax.ShapeDtypeStruct((m, n), dt) for dt in out_dtypes] + gs_shape + r_out,
        scratch_shapes=r_scr, compiler_params=_params("arbitrary", "arbitrary"), name=name)(x, g, dy, b, *extras, *r_in)
    return out[:2 + n_out + n_gs], out[2 + n_out + n_gs:]


def _mm_rmsbwd_res(a, b, x, g, res, tm, name, rider=None):
    m, k = a.shape
    n = b.shape[0]
    assert m % tm == 0, (name, m, tm)
    r_in, r_in_specs, r_out, r_out_specs, r_scr = _rider_specs(rider)
    n_rin = len(r_in)

    def body(a_ref, b_ref, x_ref, g_ref, res_ref, *refs):
        o_ref, dg_ref = refs[n_rin:n_rin + 2]
        _ride(rider, pl.program_id(0), m // tm, refs[:n_rin] + refs[n_rin + 2:])

        @pl.when(pl.program_id(0) == 0)
        def _():
            dg_ref[...] = jnp.zeros_like(dg_ref)

        da = _dot(a_ref[...].astype(BF16), b_ref[...].astype(BF16), NT)
        dx, dg = _rms_bwd_rows(x_ref[...], g_ref[...], da)
        o_ref[...] = res_ref[...] + dx
        dg_ref[...] += jnp.sum(dg, axis=0, keepdims=True)

    row = pl.BlockSpec((tm, n), lambda i: (i, 0))
    vec = pl.BlockSpec((1, n), lambda i: (0, 0))
    out = pl.pallas_call(
        body, grid=(m // tm,),
        in_specs=[pl.BlockSpec((tm, k), lambda i: (i, 0)), pl.BlockSpec((n, k), lambda i: (0, 0)), row, vec, row] + r_in_specs,
        out_specs=[row, vec] + r_out_specs,
        out_shape=[jax.ShapeDtypeStruct((m, n), F32), jax.ShapeDtypeStruct((1, n), F32)] + r_out,
        scratch_shapes=r_scr, compiler_params=_params("arbitrary"), name=name)(a, b, x, g, res, *r_in)
    return out[0], out[1], out[2:]


def _rope_tables(pos, name):
    s = pos.shape[0]
    half = MLA_ROPE // 2
    inv_freq = jnp.asarray(np.power(np.float32(ROPE_BASE), -np.arange(0, MLA_ROPE, 2, dtype=np.float32) / MLA_ROPE)
                           .astype(np.float32).reshape(1, half))

    def body(p_ref, f_ref, c_ref, s_ref):
        ang = p_ref[...].astype(F32) * f_ref[...]
        c_ref[...] = jnp.cos(ang)
        s_ref[...] = jnp.sin(ang)

    return pl.pallas_call(
        body, out_shape=(jax.ShapeDtypeStruct((s, half), F32), jax.ShapeDtypeStruct((s, half), F32)), name=name)(pos, inv_freq)


def _lb_softmax(logits):
    m = jnp.max(logits, axis=0, keepdims=True)
    e = jnp.exp(logits - m)
    return e / jnp.sum(e, axis=0, keepdims=True)


def _lb_fwd(logits, name):
    def body(l_ref, o_ref):
        p = _lb_softmax(l_ref[...])
        acc = jnp.zeros_like(p[0:1])
        o_ref[0:1, :] = acc
        for layer in range(1, DEPTH):
            acc = acc + p[layer:layer + 1]
            o_ref[layer:layer + 1, :] = acc

    return pl.pallas_call(body, out_shape=jax.ShapeDtypeStruct(logits.shape, F32), name=name)(logits)


def _lb_bwd(logits, dlb, name):
    def body(l_ref, d_ref, o_ref):
        p = _lb_softmax(l_ref[...])
        d = d_ref[...]
        dp = [jnp.zeros_like(d[0:1])] * DEPTH
        run = jnp.zeros_like(d[0:1])
        for layer in range(DEPTH - 1, 0, -1):
            run = run + d[layer:layer + 1]
            dp[layer] = run
        inner = sum(p[layer:layer + 1] * dp[layer] for layer in range(DEPTH))
        for layer in range(DEPTH):
            o_ref[layer:layer + 1, :] = p[layer:layer + 1] * (dp[layer] - inner)

    return pl.pallas_call(body, out_shape=jax.ShapeDtypeStruct(logits.shape, F32), name=name)(logits, dlb)


def _loss(y, target, name):
    s, d = y.shape
    ts = _row_tile(s, 512)

    def body(y_ref, t_ref, l_ref, dy_ref):
        e = y_ref[...] - t_ref[...]
        dy_ref[...] = e / d

        @pl.when(pl.program_id(0) == 0)
        def _():
            l_ref[...] = jnp.zeros_like(l_ref)

        l_ref[...] += 0.5 * jnp.sum(jnp.mean(e * e, axis=-1, keepdims=True), axis=0, keepdims=True)

    row = pl.BlockSpec((ts, d), lambda i: (i, 0))
    return pl.pallas_call(
        body, grid=(s // ts,), in_specs=[row, row], out_specs=(pl.BlockSpec((1, 1), lambda i: (0, 0)), row),
        out_shape=(jax.ShapeDtypeStruct((1, 1), F32), jax.ShapeDtypeStruct((s, d), F32)),
        compiler_params=_params("arbitrary"), name=name)(y, target)


def _rope(t1, t2, cos, sin):
    return t1 * cos - t2 * sin, t1 * sin + t2 * cos


def _rope_bwd(d1, d2, cos, sin):
    return d1 * cos + d2 * sin, d2 * cos - d1 * sin


def _mla_qkv(proj, qn, kvn, w_uq, w_ukv, cos, sin, name):
    s = proj.shape[0]
    ts = _row_tile(s, 256)
    hh, half = MLA_HEADS, MLA_ROPE // 2

    def body(p_ref, qn_ref, kvn_ref, wq_ref, wkv_ref, c_ref, s_ref, cq_ref, ckv_ref, q_ref, k_ref, v_ref):
        p = p_ref[...]
        cq, ckv, kr = p[:, :MLA_Q_LORA], p[:, MLA_Q_LORA:MLA_Q_LORA + MLA_KV_LORA], p[:, MLA_Q_LORA + MLA_KV_LORA:]
        cqn = (cq * _rstd(cq) * qn_ref[...]).astype(BF16)
        ckvn = (ckv * _rstd(ckv) * kvn_ref[...]).astype(BF16)
        cq_ref[...] = cqn
        ckv_ref[...] = ckvn
        qe = _dot(cqn, wq_ref[...])
        kve = _dot(ckvn, wkv_ref[...])
        cos_, sin_ = c_ref[...], s_ref[...]
        k1, k2 = _rope(kr[:, :half], kr[:, half:], cos_, sin_)
        k1, k2 = k1.astype(BF16), k2.astype(BF16)
        for h in range(hh):
            b = h * MLA_QK
            q_ref[h, :, 0:MLA_NOPE] = qe[:, b:b + MLA_NOPE].astype(BF16)
            q1, q2 = _rope(qe[:, b + MLA_NOPE:b + MLA_NOPE + half], qe[:, b + MLA_NOPE + half:b + MLA_QK], cos_, sin_)
            q_ref[h, :, MLA_NOPE:MLA_NOPE + half] = q1.astype(BF16)
            q_ref[h, :, MLA_NOPE + half:MLA_QK] = q2.astype(BF16)
            b = h * (MLA_NOPE + MLA_V)
            k_ref[h, :, 0:MLA_NOPE] = kve[:, b:b + MLA_NOPE].astype(BF16)
            k_ref[h, :, MLA_NOPE:MLA_NOPE + half] = k1
            k_ref[h, :, MLA_NOPE + half:MLA_QK] = k2
            v_ref[h] = kve[:, b + MLA_NOPE:b + MLA_NOPE + MLA_V].astype(BF16)

    def row(w):
        return pl.BlockSpec((ts, w), lambda i: (i, 0))

    def full(shape):
        return pl.BlockSpec(shape, lambda i: (0,) * len(shape))

    def heads(w):
        return pl.BlockSpec((hh, ts, w), lambda i: (0, i, 0))

    return pl.pallas_call(
        body, grid=(s // ts,),
        in_specs=[row(MLA_IN), full(qn.shape), full(kvn.shape), full(w_uq.shape), full(w_ukv.shape), row(half), row(half)],
        out_specs=(row(MLA_Q_LORA), row(MLA_KV_LORA), heads(MLA_QK), heads(MLA_QK), heads(MLA_V)),
        out_shape=(jax.ShapeDtypeStruct((s, MLA_Q_LORA), BF16), jax.ShapeDtypeStruct((s, MLA_KV_LORA), BF16),
                   jax.ShapeDtypeStruct((hh, s, MLA_QK), BF16), jax.ShapeDtypeStruct((hh, s, MLA_QK), BF16),
                   jax.ShapeDtypeStruct((hh, s, MLA_V), BF16)),
        compiler_params=_params("parallel"), name=name)(proj, qn, kvn, w_uq, w_ukv, cos, sin)


ATTN_BLOCK = 2048
ATTN_FWD_TILE = (256, 1024)
ATTN_BWD_TILE = (512, 512)


def _attn_block(s):
    return _row_tile(s, ATTN_BLOCK)


def _tile_sees(diag, q0, tq, k0, tk):
    if not diag:
        return True, False
    return k0 <= q0 + tq - 1, k0 + tk - 1 > q0


def _causal_pairs(nb, kv_major):
    if kv_major:
        pairs = [(i, j) for j in range(nb) for i in range(j, nb)]
    else:
        pairs = [(i, j) for i in range(nb) for j in range(i + 1)]
    return (jnp.asarray(np.array([p[0] for p in pairs], np.int32)), jnp.asarray(np.array([p[1] for p in pairs], np.int32)))


def _ride(rider, step, total, refs):
    if rider is None:
        return

    @pl.when(step == 0)
    def _():
        rider.start(*refs)

    @pl.when(step == (total * 7) // 8)
    def _():
        rider.middle(*refs)

    @pl.when(step == total - 1)
    def _():
        rider.finish(*refs)


def _rider_specs(rider):
    if rider is None:
        return [], [], [], [], []
    return (list(rider.operands), [_HBM] * len(rider.operands), list(rider.out_shapes), [_HBM] * len(rider.out_shapes),
            list(rider.scratch))


def _attn_fwd(q, k, v, name, rider=None):
    hh, s, _ = q.shape
    blk = _attn_block(s)
    tq, tk = min(blk, ATTN_FWD_TILE[0]), min(blk, ATTN_FWD_TILE[1])
    nb = s // blk
    it, jt = _causal_pairs(nb, kv_major=False)
    npair = int(it.shape[0])
    scale = MLA_QK ** -0.5
    c2 = scale * LOG2_E
    r_in, r_in_specs, r_out, r_out_specs, r_scr = _rider_specs(rider)
    n_rin, n_rout, n_rscr = len(r_in), len(r_out), len(r_scr)

    def body(it_ref, jt_ref, q_ref, k_ref, v_ref, *refs):
        r_refs = refs[:n_rin] + refs[n_rin + 2:n_rin + 2 + n_rout] + refs[len(refs) - n_rscr:]
        o_ref, lse_ref = refs[n_rin:n_rin + 2]
        m_scr, acc_scr, v_scr = refs[n_rin + 2 + n_rout:n_rin + 2 + n_rout + 3]
        h, t = pl.program_id(0), pl.program_id(1)
        step = h * npair + t
        _ride(rider, step, hh * npair, r_refs)
        i, j = it_ref[t], jt_ref[t]

        @pl.when(j == 0)
        def _():
            m_scr[...] = jnp.full_like(m_scr, -jnp.inf)
            acc_scr[...] = jnp.zeros_like(acc_scr)
            v_scr[:, MLA_V:] = jnp.ones((blk, MLA_V), BF16)

        def block(diag):
            v_scr[:, :MLA_V] = v_ref[...]
            for k0 in range(0, blk, tk):
                kb, vb = k_ref[k0:k0 + tk, :], v_scr[k0:k0 + tk, :]
                for q0 in range(0, blk, tq):
                    visible, needs_mask = _tile_sees(diag, q0, tq, k0, tk)
                    if not visible:
                        continue
                    rows = slice(q0, q0 + tq)
                    sc = _dot(q_ref[rows, :], kb, NT)
                    if needs_mask:
                        qpos = q0 + lax.broadcasted_iota(jnp.int32, (tq, tk), 0)
                        kpos = k0 + lax.broadcasted_iota(jnp.int32, (tq, tk), 1)
                        sc = jnp.where(qpos >= kpos, sc, -jnp.inf)
                    m_prev = m_scr[rows, :]
                    m_new = jnp.maximum(m_prev, jnp.max(sc, axis=-1, keepdims=True))
                    alpha = jnp.exp2((m_prev - m_new) * c2)
                    p = jnp.exp2((sc - m_new) * c2)
                    acc_scr[rows, :] = alpha * acc_scr[rows, :] + _dot(p.astype(BF16), vb)
                    m_scr[rows, :] = m_new

        @pl.when(j < i)
        def _():
            block(False)

        @pl.when(j == i)
        def _():
            block(True)
            acc = acc_scr[...]
            l = acc[:, MLA_V:MLA_V + 1]
            o_ref[...] = acc[:, :MLA_V] / l
            lse_ref[...] = m_scr[...] * scale + jnp.log(l)

    grid_spec = pltpu.PrefetchScalarGridSpec(
        num_scalar_prefetch=2, grid=(hh, npair),
        in_specs=[pl.BlockSpec((None, blk, MLA_QK), lambda h, t, it_, jt_: (h, it_[t], 0)),
                  pl.BlockSpec((None, blk, MLA_QK), lambda h, t, it_, jt_: (h, jt_[t], 0)),
                  pl.BlockSpec((None, blk, MLA_V), lambda h, t, it_, jt_: (h, jt_[t], 0))] + r_in_specs,
        out_specs=[pl.BlockSpec((blk, MLA_V), lambda h, t, it_, jt_: (it_[t], h)),
                   pl.BlockSpec((None, blk, 1), lambda h, t, it_, jt_: (h, it_[t], 0))] + r_out_specs,
        scratch_shapes=[pltpu.VMEM((blk, 1), F32), pltpu.VMEM((blk, 2 * MLA_V), F32),
                        pltpu.VMEM((blk, 2 * MLA_V), BF16)] + r_scr)
    out = pl.pallas_call(
        body, grid_spec=grid_spec,
        out_shape=[jax.ShapeDtypeStruct((s, hh * MLA_V), F32), jax.ShapeDtypeStruct((hh, s, 1), F32)] + r_out,
        compiler_params=_params("arbitrary", "arbitrary"), name=name)(it, jt, q, k, v, *r_in)
    return out[0], out[1], out[2:]


def _attn_bwd(q, k, v, do, lse_row, delta_row, name, rider=None):
    hh, s, _ = q.shape
    blk = _attn_block(s)
    tq, tk = min(blk, ATTN_BWD_TILE[0]), min(blk, ATTN_BWD_TILE[1])
    nb = s // blk
    it, jt = _causal_pairs(nb, kv_major=True)
    npair = int(it.shape[0])
    scale = MLA_QK ** -0.5
    c2 = scale * LOG2_E
    r_in, r_in_specs, r_out, r_out_specs, r_scr = _rider_specs(rider)
    n_rin, n_rout, n_rscr = len(r_in), len(r_out), len(r_scr)

    def body(it_ref, jt_ref, q_ref, k_ref, v_ref, do_ref, lse_ref, dl_ref, *refs):
        r_refs = refs[:n_rin] + refs[n_rin + 3:n_rin + 3 + n_rout] + refs[len(refs) - n_rscr:]
        dq_out, dk_out, dv_out = refs[n_rin:n_rin + 3]
        dq_ref, dk_ref, dv_ref = refs[n_rin + 3 + n_rout:n_rin + 3 + n_rout + 3]
        h, t = pl.program_id(0), pl.program_id(1)
        step = h * npair + t
        _ride(rider, step, hh * npair, r_refs)
        i, j = it_ref[t], jt_ref[t]

        @pl.when(t == 0)
        def _():
            dq_ref[...] = jnp.zeros_like(dq_ref)

        def block(diag):
            if diag:
                dk_ref[...] = jnp.zeros_like(dk_ref)
                dv_ref[...] = jnp.zeros_like(dv_ref)
            for q0 in range(0, blk, tq):
                qb = q_ref[q0:q0 + tq, :]
                dob = do_ref[q0:q0 + tq, :].astype(BF16)
                lse2 = lse_ref[:, q0:q0 + tq] * LOG2_E
                dl = dl_ref[pl.ds(h, 1), q0:q0 + tq]
                dq = None
                for k0 in range(0, blk, tk):
                    visible, needs_mask = _tile_sees(diag, q0, tq, k0, tk)
                    if not visible:
                        continue
                    kb, vb = k_ref[k0:k0 + tk, :], v_ref[k0:k0 + tk, :]
                    pt = jnp.exp2(_dot(kb, qb, NT) * c2 - lse2)
                    if needs_mask:
                        kpos = k0 + lax.broadcasted_iota(jnp.int32, (tk, tq), 0)
                        qpos = q0 + lax.broadcasted_iota(jnp.int32, (tk, tq), 1)
                        pt = jnp.where(qpos >= kpos, pt, 0.0)
                    dv_ref[k0:k0 + tk, :] += _dot(pt.astype(BF16), dob)
                    dpt = _dot(vb, dob, NT)
                    dst = (pt * (dpt - dl) * scale).astype(BF16)
                    dk_ref[k0:k0 + tk, :] += _dot(dst, qb)
                    part = _dot(dst, kb, TN)
                    dq = part if dq is None else dq + part
                rows = pl.ds(pl.multiple_of(i * blk + q0, tq), tq)
                dq_ref[rows, :] += dq

        @pl.when(i == j)
        def _():
            block(True)

        @pl.when(i > j)
        def _():
            block(False)

        @pl.when(i == nb - 1)
        def _():
            dk_out[...] = dk_ref[...].astype(BF16)
            dv_out[...] = dv_ref[...].astype(BF16)

        @pl.when(t == npair - 1)
        def _():
            dq_out[...] = dq_ref[...].astype(BF16)

    grid_spec = pltpu.PrefetchScalarGridSpec(
        num_scalar_prefetch=2, grid=(hh, npair),
        in_specs=[pl.BlockSpec((None, blk, MLA_QK), lambda h, t, it_, jt_: (h, it_[t], 0)),
                  pl.BlockSpec((None, blk, MLA_QK), lambda h, t, it_, jt_: (h, jt_[t], 0)),
                  pl.BlockSpec((None, blk, MLA_V), lambda h, t, it_, jt_: (h, jt_[t], 0)),
                  pl.BlockSpec((blk, MLA_V), lambda h, t, it_, jt_: (it_[t], h)),
                  pl.BlockSpec((None, 1, blk), lambda h, t, it_, jt_: (h, 0, it_[t])),
                  pl.BlockSpec((hh, blk), lambda h, t, it_, jt_: (0, it_[t]))] + r_in_specs,
        out_specs=[pl.BlockSpec((None, s, MLA_QK), lambda h, t, it_, jt_: (h, 0, 0)),
                   pl.BlockSpec((None, blk, MLA_QK), lambda h, t, it_, jt_: (h, jt_[t], 0)),
                   pl.BlockSpec((None, blk, MLA_V), lambda h, t, it_, jt_: (h, jt_[t], 0))] + r_out_specs,
        scratch_shapes=[pltpu.VMEM((s, MLA_QK), F32), pltpu.VMEM((blk, MLA_QK), F32), pltpu.VMEM((blk, MLA_V), F32)] + r_scr)
    out = pl.pallas_call(
        body, grid_spec=grid_spec,
        out_shape=[jax.ShapeDtypeStruct((hh, s, MLA_QK), BF16), jax.ShapeDtypeStruct((hh, s, MLA_QK), BF16),
                   jax.ShapeDtypeStruct((hh, s, MLA_V), BF16)] + r_out,
        compiler_params=_params("arbitrary", "arbitrary"), name=name)(it, jt, q, k, v, do, lse_row, delta_row, *r_in)
    return out[0], out[1], out[2], out[3:]


def _mla_bwd_mid(dq, dk, dv, cos, sin, proj, qn, kvn, w_uq, w_ukv, name):
    s = proj.shape[0]
    ts = _row_tile(s, 256)
    hh, half = MLA_HEADS, MLA_ROPE // 2
    nq, nkv = hh * MLA_QK, hh * (MLA_NOPE + MLA_V)

    def body(dq_ref, dk_ref, dv_ref, c_ref, s_ref, p_ref, qn_ref, kvn_ref, wq_ref, wkv_ref,
             dqe_ref, dkve_ref, dp_ref, dqn_ref, dkvn_ref):
        cos_, sin_ = c_ref[...], s_ref[...]
        dkr1 = jnp.zeros((ts, half), F32)
        dkr2 = jnp.zeros((ts, half), F32)
        for h in range(hh):
            dqh, dkh = dq_ref[h], dk_ref[h]
            b = h * MLA_QK
            dqe_ref[:, b:b + MLA_NOPE] = dqh[:, :MLA_NOPE].astype(BF16)
            d1, d2 = _rope_bwd(dqh[:, MLA_NOPE:MLA_NOPE + half], dqh[:, MLA_NOPE + half:], cos_, sin_)
            dqe_ref[:, b + MLA_NOPE:b + MLA_NOPE + half] = d1.astype(BF16)
            dqe_ref[:, b + MLA_NOPE + half:b + MLA_QK] = d2.astype(BF16)
            b = h * (MLA_NOPE + MLA_V)
            dkve_ref[:, b:b + MLA_NOPE] = dkh[:, :MLA_NOPE].astype(BF16)
            dkve_ref[:, b + MLA_NOPE:b + MLA_NOPE + MLA_V] = dv_ref[h].astype(BF16)
            dkr1 = dkr1 + dkh[:, MLA_NOPE:MLA_NOPE + half]
            dkr2 = dkr2 + dkh[:, MLA_NOPE + half:]
        dkr1, dkr2 = _rope_bwd(dkr1, dkr2, cos_, sin_)
        dcqn = _dot(dqe_ref[...], wq_ref[...], NT)
        dckvn = _dot(dkve_ref[...], wkv_ref[...], NT)
        p = p_ref[...]
        dcq, dqn = _rms_bwd_rows(p[:, :MLA_Q_LORA], qn_ref[...], dcqn)
        dckv, dkvn = _rms_bwd_rows(p[:, MLA_Q_LORA:MLA_Q_LORA + MLA_KV_LORA], kvn_ref[...], dckvn)
        dp_ref[:, :MLA_Q_LORA] = dcq.astype(BF16)
        dp_ref[:, MLA_Q_LORA:MLA_Q_LORA + MLA_KV_LORA] = dckv.astype(BF16)
        dp_ref[:, MLA_Q_LORA + MLA_KV_LORA:MLA_Q_LORA + MLA_KV_LORA + half] = dkr1.astype(BF16)
        dp_ref[:, MLA_Q_LORA + MLA_KV_LORA + half:] = dkr2.astype(BF16)

        @pl.when(pl.program_id(0) == 0)
        def _():
            dqn_ref[...] = jnp.zeros_like(dqn_ref)
            dkvn_ref[...] = jnp.zeros_like(dkvn_ref)

        dqn_ref[...] += jnp.sum(dqn, axis=0, keepdims=True)
        dkvn_ref[...] += jnp.sum(dkvn, axis=0, keepdims=True)

    def row(w):
        return pl.BlockSpec((ts, w), lambda i: (i, 0))

    def full(shape):
        return pl.BlockSpec(shape, lambda i: (0,) * len(shape))

    def heads(w):
        return pl.BlockSpec((hh, ts, w), lambda i: (0, i, 0))

    return pl.pallas_call(
        body, grid=(s // ts,),
        in_specs=[heads(MLA_QK), heads(MLA_QK), heads(MLA_V), row(half), row(half), row(MLA_IN),
                  full(qn.shape), full(kvn.shape), full(w_uq.shape), full(w_ukv.shape)],
        out_specs=(row(nq), row(nkv), row(MLA_IN), full(qn.shape), full(kvn.shape)),
        out_shape=(jax.ShapeDtypeStruct((s, nq), BF16), jax.ShapeDtypeStruct((s, nkv), BF16),
                   jax.ShapeDtypeStruct((s, MLA_IN), BF16), jax.ShapeDtypeStruct(qn.shape, F32),
                   jax.ShapeDtypeStruct(kvn.shape, F32)),
        compiler_params=_params("arbitrary"), name=name)(dq, dk, dv, cos, sin, proj, qn, kvn, w_uq, w_ukv)


HGRN_TILE = 256


def _chunk_masks(t):
    r = lax.broadcasted_iota(jnp.int32, (t, t), 0)
    c = lax.broadcasted_iota(jnp.int32, (t, t), 1)
    same = (r // HGRN_CHUNK) == (c // HGRN_CHUNK)
    return r, c, same


def _hgrn_gates(p, lb):
    hk = HGRN_HEADS * HGRN_D
    qx, fx, ix, gx = p[:, :hk], p[:, hk:2 * hk], p[:, 2 * hk:3 * hk], p[:, 3 * hk:]
    sig_f = jax.nn.sigmoid(fx)
    f = lb + (1.0 - lb) * sig_f
    sig_q = jax.nn.sigmoid(qx)
    t = p.shape[0]
    r, c, same = _chunk_masks(t)
    lower = jnp.where(same & (c <= r), 1.0, 0.0).astype(F32)
    b = _dot_select(lower, jnp.log(f), 3)
    b3 = b.reshape(t // HGRN_CHUNK, HGRN_CHUNK, hk)
    bref = jnp.broadcast_to(b3[:, HGRN_CHUNK // 2:HGRN_CHUNK // 2 + 1, :], b3.shape).reshape(t, hk)
    blast = jnp.broadcast_to(b3[:, HGRN_CHUNK - 1:, :], b3.shape).reshape(t, hk)
    return qx, ix, gx, sig_f, f, sig_q, b, bref, blast


def _hgrn_fwd(proj, lb, onorm, name, rider=None):
    s = proj.shape[0]
    t = _row_tile(s, HGRN_TILE)
    nc = t // HGRN_CHUNK
    hh, dd, hk = HGRN_HEADS, HGRN_D, HGRN_HEADS * HGRN_D
    r_in, r_in_specs, r_out, r_out_specs, r_scr = _rider_specs(rider)
    n_rin, n_rout = len(r_in), len(r_out)

    def body(p_ref, lb_ref, on_ref, *refs):
        y_ref, o_ref, st_ref = refs[n_rin:n_rin + 3]
        st_scr = refs[n_rin + 3 + n_rout]
        _ride(rider, pl.program_id(0), s // t, refs[:n_rin] + refs[n_rin + 3:n_rin + 3 + n_rout] + refs[n_rin + 4 + n_rout:])

        @pl.when(pl.program_id(0) == 0)
        def _():
            st_scr[...] = jnp.zeros_like(st_scr)

        qx, ix, gx, _, f, sig_q, b, bref, blast = _hgrn_gates(p_ref[...], lb_ref[...])
        q = qx * sig_q
        k = 1.0 - f
        r, c, same = _chunk_masks(t)
        causal = same & (c <= r)
        for h in range(hh):
            sl = slice(h * dd, (h + 1) * dd)
            bh, brefh, blasth, qh, kh = b[:, sl], bref[:, sl], blast[:, sl], q[:, sl], k[:, sl]
            vh = ix[:, sl].astype(BF16)
            q_rel = (qh * jnp.exp(bh - brefh)).astype(BF16)
            k_rel = (kh * jnp.exp(brefh - bh)).astype(BF16)
            a = jnp.where(causal, _dot(q_rel, k_rel, NT), 0.0)
            o_intra = _dot(a.astype(BF16), vh)
            q_dec = (qh * jnp.exp(bh)).astype(BF16)
            k_dec = (kh * jnp.exp(blasth - bh)).astype(BF16)
            dec = jnp.exp(blasth)
            pieces = []
            for ci in range(nc):
                rows = slice(ci * HGRN_CHUNK, (ci + 1) * HGRN_CHUNK)
                st = st_scr[h]
                if ci == 0:
                    st_ref[h] = st
                pieces.append(_dot(q_dec[rows], st.astype(BF16), NT))
                st_scr[h] = st * dec[ci * HGRN_CHUNK:ci * HGRN_CHUNK + 1, :] + _dot(vh[rows], k_dec[rows], TN)
            oh = o_intra + jnp.concatenate(pieces, axis=0)
            o_ref[:, sl] = oh
            gate = gx[:, sl] * jax.nn.sigmoid(gx[:, sl])
            y_ref[:, sl] = (oh * _rstd(oh) * on_ref[...] * gate).astype(BF16)

    out = pl.pallas_call(
        body, grid=(s // t,),
        in_specs=[pl.BlockSpec((t, 4 * hk), lambda i: (i, 0)), pl.BlockSpec((1, hk), lambda i: (0, 0)),
                  pl.BlockSpec((1, dd), lambda i: (0, 0))] + r_in_specs,
        out_specs=[pl.BlockSpec((t, hk), lambda i: (i, 0)), pl.BlockSpec((t, hk), lambda i: (i, 0)),
                   pl.BlockSpec((None, hh, dd, dd), lambda i: (i, 0, 0, 0))] + r_out_specs,
        out_shape=[jax.ShapeDtypeStruct((s, hk), BF16), jax.ShapeDtypeStruct((s, hk), F32),
                   jax.ShapeDtypeStruct((s // t, hh, dd, dd), F32)] + r_out,
        scratch_shapes=[pltpu.VMEM((hh, dd, dd), F32)] + r_scr,
        compiler_params=_params("arbitrary"), name=name)(proj, lb, onorm, *r_in)
    return out[0], out[1], out[2], out[3:]


def _hgrn_bwd(proj, lb, onorm, o, states, dy, name, rider=None):
    s = proj.shape[0]
    t = _row_tile(s, HGRN_TILE)
    nt = s // t
    nc = t // HGRN_CHUNK
    hh, dd, hk = HGRN_HEADS, HGRN_D, HGRN_HEADS * HGRN_D
    r_in, r_in_specs, r_out, r_out_specs, r_scr = _rider_specs(rider)
    n_rin, n_rout, n_rscr = len(r_in), len(r_out), len(r_scr)

    def body(p_ref, lb_ref, on_ref, o_ref, st_ref, dy_ref, *refs):
        r_refs = refs[:n_rin] + refs[n_rin + 3:n_rin + 3 + n_rout] + refs[len(refs) - n_rscr:]
        dp_ref, dlb_ref, don_ref = refs[n_rin:n_rin + 3]
        dst_scr, cat_scr, ext_scr, dk_scr, dq_scr = refs[n_rin + 3 + n_rout:n_rin + 3 + n_rout + 5]
        _ride(rider, pl.program_id(0), nt, r_refs)

        @pl.when(pl.program_id(0) == 0)
        def _():
            dst_scr[...] = jnp.zeros_like(dst_scr)
            dlb_ref[...] = jnp.zeros_like(dlb_ref)
            don_ref[...] = jnp.zeros_like(don_ref)

        lbv = lb_ref[...]
        qx, ix, gx, sig_f, f, sig_q, b, bref, blast = _hgrn_gates(p_ref[...], lbv)
        q = qx * sig_q
        k = 1.0 - f
        r, c, same = _chunk_masks(t)
        causal = same & (c <= r)
        on = on_ref[...]
        don = jnp.zeros((1, dd), F32)
        for h in range(hh):
            sl = slice(h * dd, (h + 1) * dd)
            oh = o_ref[:, sl]
            dyh = dy_ref[:, sl]
            gxh = gx[:, sl]
            sig_g = jax.nn.sigmoid(gxh)
            rs = _rstd(oh)
            dgate = dyh * (oh * rs * on)
            dp_ref[:, 3 * hk + h * dd:3 * hk + (h + 1) * dd] = (dgate * (sig_g * (1.0 + gxh * (1.0 - sig_g)))).astype(BF16)
            do, donh = _rms_bwd_rows(oh, on, dyh * (gxh * sig_g))
            don = don + jnp.sum(donh, axis=0, keepdims=True)
            dob = do.astype(BF16)
            bh, brefh, blasth, qh, kh = b[:, sl], bref[:, sl], blast[:, sl], q[:, sl], k[:, sl]
            vh = ix[:, sl].astype(BF16)
            e_qr, e_kr, e_qd, e_kd = jnp.exp(bh - brefh), jnp.exp(brefh - bh), jnp.exp(bh), jnp.exp(blasth - bh)
            dec = jnp.exp(blasth)
            q_rel, k_rel, q_dec, k_dec = qh * e_qr, kh * e_kr, qh * e_qd, kh * e_kd
            q_relb, k_relb, q_decb, k_decb = q_rel.astype(BF16), k_rel.astype(BF16), q_dec.astype(BF16), k_dec.astype(BF16)
            a = jnp.where(causal, _dot(q_relb, k_relb, NT), 0.0).astype(BF16)
            dv = _dot(a, dob, TN)
            da = jnp.where(causal, _dot(dob, vh, NT), 0.0).astype(BF16)
            dq_rel = _dot(da, k_relb)
            dk_rel = _dot(da, q_relb, TN)
            sts = [st_ref[h]]
            for ci in range(nc - 1):
                rows = slice(ci * HGRN_CHUNK, (ci + 1) * HGRN_CHUNK)
                sts.append(sts[-1] * dec[ci * HGRN_CHUNK:ci * HGRN_CHUNK + 1, :] + _dot(vh[rows], k_decb[rows], TN))
            dq_dec, dk_dec, dv_inter, ddec = [None] * nc, [None] * nc, [None] * nc, [None] * nc
            for ci in range(nc - 1, -1, -1):
                rows = slice(ci * HGRN_CHUNK, (ci + 1) * HGRN_CHUNK)
                st = sts[ci]
                dst = dst_scr[h]
                dstb = dst.astype(BF16)
                dq_dec[ci] = _dot(dob[rows], st.astype(BF16))
                dk_dec[ci] = _dot(vh[rows], dstb)
                dv_inter[ci] = _dot(k_decb[rows], dstb, NT)
                ddec[ci] = jnp.broadcast_to(jnp.sum(dst * st, axis=0, keepdims=True), (HGRN_CHUNK, dd))
                dst_scr[h] = dst * dec[ci * HGRN_CHUNK:ci * HGRN_CHUNK + 1, :] + _dot(dob[rows], q_decb[rows], TN)
            dq_dec = jnp.concatenate(dq_dec, axis=0)
            dk_dec = jnp.concatenate(dk_dec, axis=0)
            dv = dv + jnp.concatenate(dv_inter, axis=0)
            ddec = jnp.concatenate(ddec, axis=0)
            dp_ref[:, 2 * hk + h * dd:2 * hk + (h + 1) * dd] = dv.astype(BF16)
            dq_scr[:, sl] = dq_rel * e_qr + dq_dec * e_qd
            dk_scr[:, sl] = dk_rel * e_kr + dk_dec * e_kd
            g_qr, g_kr, g_qd, g_kd = dq_rel * q_rel, dk_rel * k_rel, dq_dec * q_dec, dk_dec * k_dec
            cat_scr[0:t, sl] = g_qr - g_kr + g_qd - g_kd
            cat_scr[t:2 * t, sl] = g_kr - g_qr
            cat_scr[2 * t:3 * t, sl] = g_kd
            ext_scr[:, sl] = ddec * dec
        upper = jnp.where(same & (c >= r), 1.0, 0.0).astype(F32)
        to_ref = jnp.where(same & (r % HGRN_CHUNK <= HGRN_CHUNK // 2), 1.0, 0.0).astype(F32)
        to_all = jnp.where(same, 1.0, 0.0).astype(F32)
        dlogf = _dot_select(jnp.concatenate([upper, to_ref, to_all], axis=1), cat_scr[...], 2) + ext_scr[...]
        df = dlogf / f - dk_scr[...]
        dp_ref[:, hk:2 * hk] = (df * (1.0 - lbv) * sig_f * (1.0 - sig_f)).astype(BF16)
        dp_ref[:, 0:hk] = (dq_scr[...] * (sig_q * (1.0 + qx * (1.0 - sig_q)))).astype(BF16)
        dlb_ref[...] += jnp.sum(df * (1.0 - sig_f), axis=0, keepdims=True)
        don_ref[...] += don

    def rev(i):
        return nt - 1 - i

    out = pl.pallas_call(
        body, grid=(nt,),
        in_specs=[pl.BlockSpec((t, 4 * hk), lambda i: (rev(i), 0)), pl.BlockSpec((1, hk), lambda i: (0, 0)),
                  pl.BlockSpec((1, dd), lambda i: (0, 0)), pl.BlockSpec((t, hk), lambda i: (rev(i), 0)),
                  pl.BlockSpec((None, hh, dd, dd), lambda i: (rev(i), 0, 0, 0)),
                  pl.BlockSpec((t, hk), lambda i: (rev(i), 0))] + r_in_specs,
        out_specs=[pl.BlockSpec((t, 4 * hk), lambda i: (rev(i), 0)), pl.BlockSpec((1, hk), lambda i: (0, 0)),
                   pl.BlockSpec((1, dd), lambda i: (0, 0))] + r_out_specs,
        out_shape=[jax.ShapeDtypeStruct((s, 4 * hk), BF16), jax.ShapeDtypeStruct((1, hk), F32),
                   jax.ShapeDtypeStruct((1, dd), F32)] + r_out,
        scratch_shapes=[pltpu.VMEM((hh, dd, dd), F32), pltpu.VMEM((3 * t, hk), F32), pltpu.VMEM((t, hk), F32),
                        pltpu.VMEM((t, hk), F32), pltpu.VMEM((t, hk), F32)] + r_scr,
        compiler_params=_params("arbitrary"), name=name)(proj, lb, onorm, o, states, dy, *r_in)
    return out[0], out[1], out[2], out[3:]


def _adamw_update(w, g, m, v):
    nm = ADAM_B1 * m + (1.0 - ADAM_B1) * g
    nv = ADAM_B2 * v + (1.0 - ADAM_B2) * (g * g)
    m_hat = nm / (1.0 - ADAM_B1 ** ADAM_STEP)
    v_hat = nv / (1.0 - ADAM_B2 ** ADAM_STEP)
    return -ADAM_LR * (m_hat / (jnp.sqrt(v_hat) + ADAM_EPS) + ADAM_WD * w), nm, nv


def _adamw(w, g, m, v, name):
    rows, cols = w.shape
    tr = _divisor_tile(rows, 256, 8)

    def body(w_ref, g_ref, m_ref, v_ref, d_ref, nm_ref, nv_ref):
        d_ref[...], nm_ref[...], nv_ref[...] = _adamw_update(w_ref[...], g_ref[...], m_ref[...], v_ref[...])

    blk = pl.BlockSpec((tr, cols), lambda i: (i, 0))
    shp = jax.ShapeDtypeStruct((rows, cols), F32)
    return pl.pallas_call(
        body, grid=(rows // tr,), in_specs=[blk] * 4, out_specs=(blk,) * 3, out_shape=(shp,) * 3,
        compiler_params=_params("parallel"), name=name)(w, g, m, v)


ADAMW_BLOCK_ELEMS = 128 * 1024


def _adamw_layers(w, lands, m, v, name, rider=None):
    ll, rows, cols = w.shape
    tr = _divisor_tile(rows, max(16, ADAMW_BLOCK_ELEMS // cols), 16)
    r_in, r_in_specs, r_out, r_out_specs, r_scr = _rider_specs(rider)
    n_rin = len(r_in)

    def body(w_ref, m_ref, v_ref, *refs):
        land_refs = refs[:ll]
        g_out, d_ref, nm_ref, nv_ref = refs[ll + n_rin:ll + n_rin + 4]
        layer = pl.program_id(0)
        _ride(rider, layer * (rows // tr) + pl.program_id(1), ll * (rows // tr), refs[ll:ll + n_rin] + refs[ll + n_rin + 4:])
        for k in range(ll):
            @pl.when(layer == k)
            def _(k=k):
                g = land_refs[k][0].astype(F32)
                for slot in range(1, N_DEV):
                    g = g + land_refs[k][slot].astype(F32)
                g_out[...] = g

        d_ref[...], nm_ref[...], nv_ref[...] = _adamw_update(w_ref[...], g_out[...], m_ref[...], v_ref[...])

    stacked = pl.BlockSpec((None, tr, cols), lambda l, i: (l, i, 0))

    def one(k):
        return pl.BlockSpec((N_DEV, tr, cols), lambda l, i: (0, jnp.where(l == k, i, 0), 0))

    shp = jax.ShapeDtypeStruct(w.shape, F32)
    out = pl.pallas_call(
        body, grid=(ll, rows // tr), in_specs=[stacked] * 3 + [one(k) for k in range(ll)] + r_in_specs,
        out_specs=[stacked] * 4 + r_out_specs, out_shape=[shp] * 4 + r_out, scratch_shapes=r_scr,
        compiler_params=_params("arbitrary", "arbitrary"), name=name)(w, m, v, *lands, *r_in)
    return out[0], out[1], out[2], out[3], out[4:]


_HBM = pl.BlockSpec(memory_space=pltpu.HBM)
_MESH = pl.DeviceIdType.MESH


class _GatherRide:
    def __init__(self, blocks, cuts):
        self.operands = list(blocks)
        self.cuts = list(cuts)
        self.out_shapes = []
        for b, cut in zip(blocks, cuts):
            r, c = b.shape
            shape = {"rows": (N_DEV * r, c), "cols": (r, N_DEV * c), "slots": (N_DEV, r, c)}[cut]
            self.out_shapes.append(jax.ShapeDtypeStruct(shape, b.dtype))
        n = len(blocks)
        self.scratch = [pltpu.SemaphoreType.DMA((7 * n,)), pltpu.SemaphoreType.DMA((7 * n,)), pltpu.SemaphoreType.DMA((n,))]

    def _parts(self, *refs):
        n = len(self.operands)
        x_refs, out_refs = refs[:n], refs[n:2 * n]
        send_sems, recv_sems, local_sems = refs[2 * n:]
        x, y, c = lax.axis_index("x"), lax.axis_index("y"), lax.axis_index("c")
        me, sibling = (x, y, c), (x, y, 1 - c)
        chips = [(1 - x, y), (x, 1 - y), (1 - x, 1 - y)]
        mine, first, passed, landed, from_sibling = [], [], [], [], []
        for e in range(n):
            x_ref, out_ref, cut = x_refs[e], out_refs[e], self.cuts[e]
            r, cc = x_ref.shape

            def place(px, py, pc, out_ref=out_ref, cut=cut, r=r, cc=cc):
                p = 4 * px + 2 * py + pc
                if cut == "rows":
                    return out_ref.at[pl.ds(pl.multiple_of(p * r, r), r), :]
                if cut == "cols":
                    return out_ref.at[:, pl.ds(pl.multiple_of(p * cc, cc), cc)]
                return out_ref.at[p]

            def copy(k, block, to, src=None, place=place, e=e):
                return pltpu.make_async_remote_copy(
                    src_ref=place(*block) if src is None else src, dst_ref=place(*block), send_sem=send_sems.at[7 * e + k],
                    recv_sem=recv_sems.at[7 * e + k], device_id=to, device_id_type=_MESH)

            mine.append(pltpu.make_async_copy(x_ref, place(*me), local_sems.at[e]))
            first += [copy(0, me, sibling, src=x_ref)] + [copy(1 + j, me, (*chip, c), src=x_ref) for j, chip in enumerate(chips)]
            passed += [copy(4 + j, (*chip, c), sibling) for j, chip in enumerate(chips)]
            landed += [copy(1 + j, (*chip, c), me) for j, chip in enumerate(chips)]
            from_sibling += [copy(0, sibling, me)] + [copy(4 + j, (*chip, 1 - c), me) for j, chip in enumerate(chips)]
        return mine, first, passed, landed, from_sibling

    def start(self, *refs):
        mine, first, _, _, _ = self._parts(*refs)
        for cp in mine + first:
            cp.start()

    def middle(self, *refs):
        _, _, passed, landed, _ = self._parts(*refs)
        for got, fwd in zip(landed, passed):
            got.wait_recv()
            fwd.start()

    def finish(self, *refs):
        mine, first, passed, _, from_sibling = self._parts(*refs)
        for cp in from_sibling:
            cp.wait_recv()
        for cp in first + passed:
            cp.wait_send()
        for cp in mine:
            cp.wait()


class _ExchangeRide:
    def __init__(self, sends):
        self.operands = list(sends)
        self.out_shapes = [jax.ShapeDtypeStruct(s.shape, s.dtype) for s in sends]
        n = len(sends)
        self.scratch = [pltpu.SemaphoreType.DMA((7 * n,)), pltpu.SemaphoreType.DMA((7 * n,)), pltpu.SemaphoreType.DMA((n,))]

    def _parts(self, *refs):
        n = len(self.operands)
        s_refs, land_refs = refs[:n], refs[n:2 * n]
        send_sems, recv_sems, local_sems = refs[2 * n:]
        x, y, c = lax.axis_index("x"), lax.axis_index("y"), lax.axis_index("c")
        me = 4 * x + 2 * y + c
        own, sends, recvs = [], [], []
        for e in range(n):
            s_ref, land_ref = s_refs[e], land_refs[e]
            own.append(pltpu.make_async_copy(s_ref.at[me], land_ref.at[me], local_sems.at[e]))
            for rel in range(1, N_DEV):
                px = 1 - x if rel & 4 else x
                py = 1 - y if rel & 2 else y
                pc = 1 - c if rel & 1 else c
                peer = 4 * px + 2 * py + pc
                k = 7 * e + rel - 1
                sends.append(pltpu.make_async_remote_copy(
                    src_ref=s_ref.at[peer], dst_ref=land_ref.at[me], send_sem=send_sems.at[k], recv_sem=recv_sems.at[k],
                    device_id=(px, py, pc), device_id_type=_MESH))
                recvs.append(pltpu.make_async_remote_copy(
                    src_ref=s_ref.at[me], dst_ref=land_ref.at[peer], send_sem=send_sems.at[k], recv_sem=recv_sems.at[k],
                    device_id=(px, py, pc), device_id_type=_MESH))
        return own, sends, recvs

    def start(self, *refs):
        own, sends, _ = self._parts(*refs)
        for cp in own + sends:
            cp.start()

    def middle(self, *refs):
        pass

    def finish(self, *refs):
        own, sends, recvs = self._parts(*refs)
        for cp in recvs:
            cp.wait_recv()
        for cp in sends:
            cp.wait_send()
        for cp in own:
            cp.wait()


def _run_alone(rider, name):
    def body(*refs):
        rider.start(*refs)
        rider.middle(*refs)
        rider.finish(*refs)

    return pl.pallas_call(
        body, out_shape=rider.out_shapes, in_specs=[_HBM] * len(rider.operands), out_specs=[_HBM] * len(rider.out_shapes),
        scratch_shapes=rider.scratch, name=name)(*rider.operands)


def _all_gather(xs, name):
    return _run_alone(_GatherRide([xs], ["slots"]), name)[0]


def _sum_slots(parts, name):
    _, rows, cols = parts.shape
    tr = _divisor_tile(rows, 256, 16)

    def body(p_ref, o_ref):
        acc = p_ref[0].astype(F32)
        for slot in range(1, N_DEV):
            acc = acc + p_ref[slot].astype(F32)
        o_ref[...] = acc

    return pl.pallas_call(
        body, grid=(rows // tr,), in_specs=[pl.BlockSpec((N_DEV, tr, cols), lambda i: (0, i, 0))],
        out_specs=pl.BlockSpec((tr, cols), lambda i: (i, 0)), out_shape=jax.ShapeDtypeStruct((rows, cols), F32),
        compiler_params=_params("parallel"), name=name)(parts)


def _carry(rode, key, riders, call):
    rider = riders.get(key)
    res = call(rider)
    if rider is None:
        return res
    res, rode[key] = res
    return res


def _kept(rode, key, riders, brought):
    if key in riders:
        rode[key] = brought


def _mlp_fwd(h, g_pre, g_post, w1, w2, tag, riders):
    rode = {}
    tm = _row_tile(h.shape[0], 2048)
    (a, r2), brought = _norm_mm(h, g_pre, w1, tm, 1024, f"{tag}_up", out_dtypes=(BF16,),
                                epi=lambda acc: (jnp.square(jnp.maximum(acc, 0.0)),), rider=riders.get("up"))
    _kept(rode, "up", riders, brought)
    z, out, brought = _mm_norm_res(r2, w2, g_post, h, 512, f"{tag}_down", riders.get("down"))
    _kept(rode, "down", riders, brought)
    return out, (h, a, r2, z), rode


def _mlp_bwd(dh, saved, g_pre, g_post, w1, w2, tag, riders):
    h, a, r2, z = saved
    rode = {}
    tm = _row_tile(h.shape[0], 1024)
    (dz, dg_post, du), brought = _rmsbwd_mm(
        z, g_post, dh, w2, tm, 1024, f"{tag}_ddown", out_dtypes=(BF16,), extras=(r2,),
        epi=lambda acc, rr: (acc * (2.0 * jnp.sqrt(rr.astype(F32))),), rider=riders.get("ddown"))
    _kept(rode, "ddown", riders, brought)
    dw2 = _carry(rode, "dw2", riders, lambda r: _mm(
        r2, dz, "tn", 512, 1024, f"{tag}_dw2", out_dtypes=(BF16,), shard="rows", rider=r))
    dw1 = _mm(a, du, "tn", 1024, 512, f"{tag}_dw1", out_dtypes=(BF16,), shard="cols")
    dh_in, dg_pre, brought = _mm_rmsbwd_res(du, w1, h, g_pre, dh, 512, f"{tag}_dup", riders.get("dup"))
    _kept(rode, "dup", riders, brought)
    return dh_in, dg_pre, dg_post, dw1, dw2, rode


def _hgrn_layer_fwd(h, g_pre, g_post, lb, onorm, w_in, w_o, tag, riders):
    rode = {}
    (a, proj), brought = _norm_mm(h, g_pre, w_in, _row_tile(h.shape[0], 2048), 1024, f"{tag}_in", rider=riders.get("in"))
    _kept(rode, "in", riders, brought)
    y, o, states, brought = _hgrn_fwd(proj, lb, onorm, f"{tag}_scan", riders.get("scan"))
    _kept(rode, "scan", riders, brought)
    m, out, _ = _mm_norm_res(y, w_o, g_post, h, 512, f"{tag}_o")
    return out, (h, a, proj, y, o, states, m), rode


def _hgrn_layer_bwd(dh, saved, g_pre, g_post, lb, onorm, w_in, w_o, tag, rider=None):
    h, a, proj, y, o, states, m = saved
    (dm, dg_post, dy), _ = _rmsbwd_mm(m, g_post, dh, w_o, 512, 1024, f"{tag}_do")
    dw_o = _mm(y, dm, "tn", 128, 1024, f"{tag}_dwo", out_dtypes=(BF16,), shard="rows")
    dproj, dlb, donorm, rode = _hgrn_bwd(proj, lb, onorm, o, states, dy, f"{tag}_dscan", rider)
    dw_in = _mm(a, dproj, "tn", 1024, 512, f"{tag}_dwin", out_dtypes=(BF16,), shard="cols")
    dh_in, dg_pre, _ = _mm_rmsbwd_res(dproj, w_in, h, g_pre, dh, 512, f"{tag}_din")
    return dh_in, dg_pre, dg_post, dlb, donorm, dw_in, dw_o, rode


def _mla_layer_fwd(h, g_pre, g_post, cos, sin, w_in, late, qn, kvn, tag, riders):
    rode = {}
    (a, proj), brought = _norm_mm(h, g_pre, w_in, 512, MLA_IN, f"{tag}_in", rider=riders.get("in"))
    _kept(rode, "in", riders, brought)
    w_uq, w_ukv, _ = late(rode)
    cqn, ckvn, q, k, v = _mla_qkv(proj, qn, kvn, w_uq, w_ukv, cos, sin, f"{tag}_qkv")
    o, lse, brought = _attn_fwd(q, k, v, f"{tag}_attn", riders.get("attn"))
    _kept(rode, "attn", riders, brought)
    _, _, w_o = late(rode)
    m, out, _ = _mm_norm_res(o, w_o, g_post, h, 512, f"{tag}_o")
    return out, (h, a, proj, cqn, ckvn, q, k, v, o, lse, m), rode


def _mla_layer_bwd(dh, saved, g_pre, g_post, cos, sin, w_in, qn, kvn, w_uq, w_ukv, w_o, tag, rider=None, own_ride=False):
    h, a, proj, cqn, ckvn, q, k, v, o, lse, m = saved
    hh, s = q.shape[0], q.shape[1]
    own = {}
    (dm, dg_post, do, delta), _ = _rmsbwd_mm(m, g_post, dh, w_o, 512, 1024, f"{tag}_do", group_sums=(o, hh))
    dw_o = _mm(o, dm, "tn", 128, 1024, f"{tag}_dwo", out_dtypes=(BF16,), shard="rows")
    if own_ride:
        rider = _ExchangeRide(list(rider.operands) + [dw_o])
    dq, dk, dv, rode = _attn_bwd(q, k, v, do, lse.reshape(hh, 1, s), delta, f"{tag}_dattn", rider)
    if own_ride:
        own["mla_w_o"] = rode[-1]
    dqe, dkve, dproj, dqn, dkvn = _mla_bwd_mid(dq, dk, dv, cos, sin, proj, qn, kvn, w_uq, w_ukv, f"{tag}_dqkv")
    dw_uq = _mm(cqn, dqe, "tn", MLA_Q_LORA, 768, f"{tag}_dwuq", out_dtypes=(BF16,))
    dw_uq = dw_uq.reshape(MLA_Q_LORA, N_DEV, -1).transpose(1, 0, 2)
    dw_ukv = _mm(ckvn, dkve, "tn", MLA_KV_LORA, 256, f"{tag}_dwukv", out_dtypes=(BF16,), shard="cols")
    dw_in = _mm(a, dproj, "tn", 128, MLA_IN, f"{tag}_dwin", out_dtypes=(BF16,), shard="rows",
                rider=_ExchangeRide([dw_uq, dw_ukv]) if own_ride else None)
    if own_ride:
        dw_in, (own["mla_w_uq"], own["mla_w_ukv"]) = dw_in
    dh_in, dg_pre, brought = _mm_rmsbwd_res(dproj, w_in, h, g_pre, dh, 512, f"{tag}_din",
                                             _ExchangeRide([dw_in]) if own_ride else None)
    if own_ride:
        own["mla_w_in"] = brought[0]
    return dh_in, dg_pre, dg_post, dqn, dkvn, dw_in, dw_uq, dw_ukv, dw_o, rode, own


_CUT = dict(mla_w_in="rows", mla_w_uq="cols", mla_w_ukv="cols", mla_w_o="rows", hgrn_w_in="cols", hgrn_w_o="rows",
            mlp_w1="cols", mlp_w2="rows")


def _unit(layer, kind):
    slot = layer // 2
    if kind == "mla":
        return [("mla_w_in", slot), ("mla_w_uq", slot), ("mla_w_ukv", slot), ("mla_w_o", slot)]
    if kind == "hgrn":
        return [("hgrn_w_in", slot), ("hgrn_w_o", slot)]
    return [("mlp_w1", layer), ("mlp_w2", layer)]


_GATHER_FIRST = [("mla_w_in", 0)]
_GATHER_PLAN = {
    (0, "in"): [("mla_w_uq", 0), ("mla_w_ukv", 0)],
    (0, "attn"): [("mla_w_o", 0)] + _unit(0, "mlp") + _unit(1, "hgrn"),
    (0, "up"): [("mlp_w1", 1)],
    (0, "down"): [("mlp_w2", 1)],
    (1, "in"): _unit(2, "mla"),
    (1, "scan"): _unit(2, "mlp"),
    (2, "attn"): _unit(3, "hgrn") + _unit(3, "mlp"),
}
_EXCHANGE_PLAN = {
    (3, "dscan"): _unit(3, "mlp"),
    (2, "ddown"): [("hgrn_w_in", 1)],
    (2, "dw2"): [("hgrn_w_o", 1)],
    (2, "dattn"): _unit(2, "mlp"),
    (1, "ddown"): [("mla_w_in", 1), ("mla_w_uq", 1)],
    (1, "dup"): [("mla_w_ukv", 1), ("mla_w_o", 1)],
    (1, "dscan"): _unit(1, "mlp"),
    (0, "ddown"): [("hgrn_w_in", 0)],
    (0, "dw2"): [("hgrn_w_o", 0)],
    (0, "dattn"): _unit(0, "mlp"),
}


def _gather_cut(name):
    return "slots" if name == "mla_w_uq" else _CUT[name]


def _gather_rider(weights, ents):
    return _GatherRide([weights[name][idx].astype(BF16) for name, idx in ents], [_gather_cut(name) for name, _ in ents])


def _gathered(outs, ents):
    res = {}
    for (name, idx), out in zip(ents, outs):
        if _gather_cut(name) == "slots":
            out = out.transpose(1, 0, 2).reshape(out.shape[1], -1)
        res[(name, idx)] = out
    return res


def _adamw_nd(w, g, m, v, name):
    shape = w.shape
    c = shape[-1]
    d, nm, nv = _adamw(w.reshape(-1, c), g.reshape(-1, c), m.reshape(-1, c), v.reshape(-1, c), name)
    return d.reshape(shape), nm.reshape(shape), nv.reshape(shape)


def kernel(x, positions, norm_gains, mla_w_in, mla_q_norm, mla_kv_norm, mla_w_uq, mla_w_ukv, mla_w_o, hgrn_w_in, hgrn_lb_logits, hgrn_o_norm, hgrn_w_o, mlp_w1, mlp_w2, loss_target, m_norm_gains, m_mla_w_in, m_mla_q_norm, m_mla_kv_norm, m_mla_w_uq, m_mla_w_ukv, m_mla_w_o, m_hgrn_w_in, m_hgrn_lb_logits, m_hgrn_o_norm, m_hgrn_w_o, m_mlp_w1, m_mlp_w2, v_norm_gains, v_mla_w_in, v_mla_q_norm, v_mla_kv_norm, v_mla_w_uq, v_mla_w_ukv, v_mla_w_o, v_hgrn_w_in, v_hgrn_lb_logits, v_hgrn_o_norm, v_hgrn_w_o, v_mlp_w1, v_mlp_w2):
    weights = dict(norm_gains=norm_gains, mla_w_in=mla_w_in, mla_q_norm=mla_q_norm, mla_kv_norm=mla_kv_norm,
                   mla_w_uq=mla_w_uq, mla_w_ukv=mla_w_ukv, mla_w_o=mla_w_o, hgrn_w_in=hgrn_w_in,
                   hgrn_lb_logits=hgrn_lb_logits, hgrn_o_norm=hgrn_o_norm, hgrn_w_o=hgrn_w_o, mlp_w1=mlp_w1, mlp_w2=mlp_w2)
    mom_m = dict(norm_gains=m_norm_gains, mla_w_in=m_mla_w_in, mla_q_norm=m_mla_q_norm, mla_kv_norm=m_mla_kv_norm,
                 mla_w_uq=m_mla_w_uq, mla_w_ukv=m_mla_w_ukv, mla_w_o=m_mla_w_o, hgrn_w_in=m_hgrn_w_in,
                 hgrn_lb_logits=m_hgrn_lb_logits, hgrn_o_norm=m_hgrn_o_norm, hgrn_w_o=m_hgrn_w_o, mlp_w1=m_mlp_w1, mlp_w2=m_mlp_w2)
    mom_v = dict(norm_gains=v_norm_gains, mla_w_in=v_mla_w_in, mla_q_norm=v_mla_q_norm, mla_kv_norm=v_mla_kv_norm,
                 mla_w_uq=v_mla_w_uq, mla_w_ukv=v_mla_w_ukv, mla_w_o=v_mla_w_o, hgrn_w_in=v_hgrn_w_in,
                 hgrn_lb_logits=v_hgrn_lb_logits, hgrn_o_norm=v_hgrn_o_norm, hgrn_w_o=v_hgrn_w_o, mlp_w1=v_mlp_w1, mlp_w2=v_mlp_w2)
    order = list(weights)
    seq = x.shape[1]
    h = x.reshape(seq, D_MODEL)
    target = loss_target.reshape(seq, D_MODEL)

    full = _gathered(_run_alone(_gather_rider(weights, _GATHER_FIRST), "gather_first"), _GATHER_FIRST)
    gains = _all_gather(norm_gains.reshape(DEPTH * 4, D_MODEL // N_DEV), "gather_gains")
    gains = gains.transpose(1, 0, 2).reshape(DEPTH, 4, 1, D_MODEL)

    def gather_riders(layer, keys):
        return {k: _gather_rider(weights, _GATHER_PLAN[(layer, k)]) for k in keys if (layer, k) in _GATHER_PLAN}

    seen = set()

    def arrived(layer, rode):
        for k, outs in rode.items():
            if (layer, k) not in seen:
                seen.add((layer, k))
                full.update(_gathered(outs, _GATHER_PLAN[(layer, k)]))

    def late_weights(layer):
        def late(rode):
            arrived(layer, rode)
            return tuple(full.get((name, layer // 2)) for name in ("mla_w_uq", "mla_w_ukv", "mla_w_o"))
        return late

    cos, sin = _rope_tables(positions.reshape(seq, 1), "rope_tables")
    lower = _lb_fwd(hgrn_lb_logits, "lower_bounds")

    def mixer_args(layer):
        slot = layer // 2
        if layer % 2 == 0:
            return (cos, sin, full[("mla_w_in", slot)], mla_q_norm[slot:slot + 1], mla_kv_norm[slot:slot + 1],
                    full[("mla_w_uq", slot)], full[("mla_w_ukv", slot)], full[("mla_w_o", slot)])
        return (lower[layer:layer + 1], hgrn_o_norm[slot:slot + 1], full[("hgrn_w_in", slot)], full[("hgrn_w_o", slot)])

    saved = []
    for layer in range(DEPTH):
        g = gains[layer]
        if layer % 2 == 0:
            slot = layer // 2
            h, sv_mix, rode = _mla_layer_fwd(
                h, g[0], g[1], cos, sin, full[("mla_w_in", slot)], late_weights(layer), mla_q_norm[slot:slot + 1],
                mla_kv_norm[slot:slot + 1], f"l{layer}_mla", gather_riders(layer, ["in", "attn"]))
            arrived(layer, rode)
        else:
            h, sv_mix, rode = _hgrn_layer_fwd(h, g[0], g[1], *mixer_args(layer), f"l{layer}_hgrn",
                                              gather_riders(layer, ["in", "scan"]))
            arrived(layer, rode)
        h, sv_mlp, rode = _mlp_fwd(h, g[2], g[3], full[("mlp_w1", layer)], full[("mlp_w2", layer)], f"l{layer}_mlp",
                                   gather_riders(layer, ["up", "down"]))
        arrived(layer, rode)
        saved.append((sv_mix, sv_mlp))

    loss_part, dh = _loss(h, target, "loss")
    loss = lax.psum(loss_part[0, 0], AXES)

    zero_row = jnp.zeros((1, D_MODEL), F32)
    dgains = [[None] * 4 for _ in range(DEPTH)]
    dlower = [zero_row] * DEPTH
    partials, lands = {}, {}
    dqn, dkvn, donorm = [None] * 2, [None] * 2, [None] * 2

    def exchange_riders(layer, keys):
        return {k: _ExchangeRide([partials[e] for e in _EXCHANGE_PLAN[(layer, k)]]) for k in keys
                if (layer, k) in _EXCHANGE_PLAN}

    def landed(layer, rode):
        for k, outs in rode.items():
            lands.update(zip(_EXCHANGE_PLAN[(layer, k)], outs))

    for layer in range(DEPTH - 1, -1, -1):
        slot = layer // 2
        g = gains[layer]
        sv_mix, sv_mlp = saved[layer]
        dh, dgains[layer][2], dgains[layer][3], partials[("mlp_w1", layer)], partials[("mlp_w2", layer)], rode = _mlp_bwd(
            dh, sv_mlp, g[2], g[3], full[("mlp_w1", layer)], full[("mlp_w2", layer)], f"l{layer}_mlp",
            exchange_riders(layer, ["ddown", "dw2", "dup"]))
        landed(layer, rode)
        key = "dattn" if layer % 2 == 0 else "dscan"
        rider = exchange_riders(layer, [key]).get(key)
        if layer % 2 == 0:
            (dh, dgains[layer][0], dgains[layer][1], dqn[slot], dkvn[slot], partials[("mla_w_in", slot)],
             partials[("mla_w_uq", slot)], partials[("mla_w_ukv", slot)], partials[("mla_w_o", slot)], brought,
             own) = _mla_layer_bwd(dh, sv_mix, g[0], g[1], *mixer_args(layer), f"l{layer}_mla", rider, own_ride=(layer == 0))
            lands.update({(name, slot): land for name, land in own.items()})
        else:
            (dh, dgains[layer][0], dgains[layer][1], dlower[layer], donorm[slot], partials[("hgrn_w_in", slot)],
             partials[("hgrn_w_o", slot)], brought) = _hgrn_layer_bwd(dh, sv_mix, g[0], g[1], *mixer_args(layer), f"l{layer}_hgrn", rider)
        landed(layer, {key: brought} if rider is not None else {})
    grad_x = dh.reshape(x.shape)
    dlogits = _lb_bwd(hgrn_lb_logits, jnp.concatenate(dlower, axis=0), "lower_bounds_bwd")

    pad = jnp.zeros((1, D_MODEL - 2 * MLA_KV_LORA), F32)
    pad2 = jnp.zeros((1, D_MODEL - 2 * HGRN_D), F32)
    small = jnp.concatenate(
        [jnp.concatenate([gg for row in dgains for gg in row], axis=0), jnp.concatenate(dqn, axis=1),
         jnp.concatenate(dkvn + [pad], axis=1), dlogits, jnp.concatenate(donorm + [pad2], axis=1), zero_row], axis=0)
    small = _sum_slots(_all_gather(small, "gather_small_grads"), "sum_small_grads")
    me = 4 * lax.axis_index("x") + 2 * lax.axis_index("y") + lax.axis_index("c")
    n_g = DEPTH * 4
    width = D_MODEL // N_DEV
    grads = {}
    grads["norm_gains"] = lax.dynamic_slice(small[:n_g], (0, me * width), (n_g, width)).reshape(DEPTH, 4, width)
    grads["mla_q_norm"] = small[n_g].reshape(2, MLA_Q_LORA)
    grads["mla_kv_norm"] = small[n_g + 1, :2 * MLA_KV_LORA].reshape(2, MLA_KV_LORA)
    grads["hgrn_lb_logits"] = small[n_g + 2:n_g + 2 + DEPTH]
    grads["hgrn_o_norm"] = small[n_g + 2 + DEPTH, :2 * HGRN_D].reshape(2, HGRN_D)

    deltas, new_m, new_v = {}, {}, {}
    for name in order:
        if name in _CUT:
            per_layer = [lands[(name, idx)] for idx in range(weights[name].shape[0])]
            grads[name], deltas[name], new_m[name], new_v[name], _ = _adamw_layers(
                weights[name], per_layer, mom_m[name], mom_v[name], f"adamw_{name}")
        else:
            deltas[name], new_m[name], new_v[name] = _adamw_nd(weights[name], grads[name], mom_m[name], mom_v[name], f"adamw_{name}")
    return (loss, grad_x, *[grads[n] for n in order], *[deltas[n] for n in order], *[new_m[n] for n in order],
            *[new_v[n] for n in order])
```

```python
import numpy as np
import jax
import jax.numpy as jnp
from jax import lax
from jax.experimental import pallas as pl
from jax.experimental.pallas import tpu as pltpu

F32, BF16 = jnp.float32, jnp.bfloat16

N_DEV = 8
AXES = ("x", "y", "c")
D_MODEL = 1024
DEPTH = 4
MLA_HEADS = 8
MLA_Q_LORA = 512
MLA_KV_LORA = 256
MLA_NOPE = 128
MLA_ROPE = 64
MLA_V = 128
MLA_QK = MLA_NOPE + MLA_ROPE
MLA_IN = MLA_Q_LORA + MLA_KV_LORA + MLA_ROPE
ROPE_BASE = 10000.0
HGRN_HEADS = 8
HGRN_D = 128
HGRN_CHUNK = 32
D_FF = 4 * D_MODEL
EPS = 1e-6
LOG2_E = 1.4426950408889634
ADAM_LR, ADAM_B1, ADAM_B2, ADAM_EPS, ADAM_WD, ADAM_STEP = 0.001, 0.9, 0.999, 1e-08, 0.01, 10

V7X_VMEM_LIMIT_BYTES = 56 * 1024 * 1024

NN = (((1,), (0,)), ((), ()))
NT = (((1,), (1,)), ((), ()))
TN = (((0,), (0,)), ((), ()))
_DIMS = {"nn": NN, "nt": NT, "tn": TN}


def _params(*sem):
    return pltpu.CompilerParams(dimension_semantics=sem, vmem_limit_bytes=V7X_VMEM_LIMIT_BYTES)


def _dot(a, b, dims=NN):
    return lax.dot_general(a, b, dims, preferred_element_type=F32)


def _dot_select(sel, x, pieces, dims=NN):
    sel = sel.astype(BF16)
    acc, rest = None, x
    for _ in range(pieces):
        term = rest.astype(BF16)
        part = _dot(sel, term, dims)
        acc = part if acc is None else acc + part
        rest = rest - term.astype(F32)
    return acc


def _rstd(x):
    return lax.rsqrt(jnp.mean(x * x, axis=-1, keepdims=True) + EPS)


def _rms_bwd_rows(x, g, dy):
    r = _rstd(x)
    xh = x * r
    dyg = dy * g
    dx = r * (dyg - xh * jnp.mean(dyg * xh, axis=-1, keepdims=True))
    return dx, dy * xh


def _row_tile(n, want):
    t = min(n, want)
    assert n % t == 0, (n, t)
    return t


def _divisor_tile(n, cap, mult):
    for t in range(min(cap, n) - min(cap, n) % mult, 0, -mult):
        if n % t == 0:
            return t
    return n


def _rms_fwd(x, g, res, out_dtype, name):
    s, d = x.shape
    ts = _row_tile(s, 512)

    def body(x_ref, g_ref, *rest):
        xf = x_ref[...]
        y = xf * _rstd(xf) * g_ref[...]
        if res is not None:
            y = rest[0][...] + y
        rest[-1][...] = y.astype(out_dtype)

    row = pl.BlockSpec((ts, d), lambda i: (i, 0))
    vec = pl.BlockSpec((1, d), lambda i: (0, 0))
    ins = [x, g] + ([res] if res is not None else [])
    return pl.pallas_call(
        body, grid=(s // ts,), in_specs=[row, vec] + ([row] if res is not None else []), out_specs=row,
        out_shape=jax.ShapeDtypeStruct((s, d), out_dtype), compiler_params=_params("parallel"), name=name)(*ins)


def _rms_bwd(x, g, dy, res, out_dtype, name):
    s, d = x.shape
    ts = _row_tile(s, 512)

    def body(x_ref, g_ref, dy_ref, *rest):
        dx_ref, dg_ref = rest[-2:]
        dx, dg = _rms_bwd_rows(x_ref[...], g_ref[...], dy_ref[...].astype(F32))
        if res is not None:
            dx = rest[0][...] + dx
        dx_ref[...] = dx.astype(out_dtype)

        @pl.when(pl.program_id(0) == 0)
        def _():
            dg_ref[...] = jnp.zeros_like(dg_ref)

        dg_ref[...] += jnp.sum(dg, axis=0, keepdims=True)

    row = pl.BlockSpec((ts, d), lambda i: (i, 0))
    vec = pl.BlockSpec((1, d), lambda i: (0, 0))
    ins = [x, g, dy] + ([res] if res is not None else [])
    return pl.pallas_call(
        body, grid=(s // ts,), in_specs=[row, vec, row] + ([row] if res is not None else []), out_specs=(row, vec),
        out_shape=(jax.ShapeDtypeStruct((s, d), out_dtype), jax.ShapeDtypeStruct((1, d), F32)),
        compiler_params=_params("arbitrary"), name=name)(*ins)


def _mm(a, b, mode, tm, tn, name, out_dtypes=(F32,), shard=None, epi=None, extras=(), rider=None):
    if mode == "tn":
        k, m = a.shape
        a_spec = pl.BlockSpec((k, tm), lambda i, j: (0, i))
    else:
        m, k = a.shape
        a_spec = pl.BlockSpec((tm, k), lambda i, j: (i, 0))
    if mode == "nt":
        n = b.shape[0]
        b_spec = pl.BlockSpec((tn, k), lambda i, j: (j, 0))
    else:
        n = b.shape[1]
        b_spec = pl.BlockSpec((k, tn), lambda i, j: (0, j))
    assert m % tm == 0 and n % tn == 0, (name, m, tm, n, tn)
    tile = pl.BlockSpec((tm, tn), lambda i, j: (i, j))
    if shard == "rows":
        per = m // N_DEV // tm
        out_specs = [pl.BlockSpec((None, tm, tn), lambda i, j: (i // per, i % per, j))]
        out_shape = [jax.ShapeDtypeStruct((N_DEV, m // N_DEV, n), out_dtypes[0])]
    elif shard == "cols":
        per = n // N_DEV // tn
        out_specs = [pl.BlockSpec((None, tm, tn), lambda i, j: (j // per, i, j % per))]
        out_shape = [jax.ShapeDtypeStruct((N_DEV, m, n // N_DEV), out_dtypes[0])]
    else:
        out_specs = [tile for _ in out_dtypes]
        out_shape = [jax.ShapeDtypeStruct((m, n), dt) for dt in out_dtypes]
    n_ex, n_out = len(extras), len(out_shape)
    r_in, r_in_specs, r_out, r_out_specs, r_scr = _rider_specs(rider)
    n_rin, n_rout = len(r_in), len(r_out)
    grid = (m // tm, n // tn)

    def body(a_ref, b_ref, *refs):
        ex_refs = refs[:n_ex]
        o_refs = refs[n_ex + n_rin:n_ex + n_rin + n_out]
        r_refs = refs[n_ex:n_ex + n_rin] + refs[n_ex + n_rin + n_out:]
        _ride(rider, pl.program_id(0) * grid[1] + pl.program_id(1), grid[0] * grid[1], r_refs)
        acc = _dot(a_ref[...].astype(BF16), b_ref[...].astype(BF16), _DIMS[mode])
        vals = (acc,) if epi is None else epi(acc, *[r[...] for r in ex_refs])
        for o_ref, val in zip(o_refs, vals):
            o_ref[...] = val.astype(o_ref.dtype)

    sem = ("parallel", "parallel") if rider is None else ("arbitrary", "arbitrary")
    out = pl.pallas_call(
        body, grid=grid, in_specs=[a_spec, b_spec] + [tile] * n_ex + r_in_specs, out_specs=out_specs + r_out_specs,
        out_shape=out_shape + r_out, scratch_shapes=r_scr, compiler_params=_params(*sem), name=name)(a, b, *extras, *r_in)
    res = out[0] if n_out == 1 else out[:n_out]
    return res if rider is None else (res, out[n_out:])


def _norm_mm(x, g, b, tm, tn, name, out_dtypes=(F32,), epi=None, rider=None):
    m, k = x.shape
    n = b.shape[1]
    assert m % tm == 0 and n % tn == 0, (name, m, tm, n, tn)
    grid = (m // tm, n // tn)
    n_out = len(out_dtypes)
    r_in, r_in_specs, r_out, r_out_specs, r_scr = _rider_specs(rider)
    n_rin = len(r_in)

    def body(x_ref, g_ref, b_ref, *refs):
        a_ref = refs[n_rin]
        o_refs = refs[n_rin + 1:n_rin + 1 + n_out]
        _ride(rider, pl.program_id(0) * grid[1] + pl.program_id(1), grid[0] * grid[1], refs[:n_rin] + refs[n_rin + 1 + n_out:])

        @pl.when(pl.program_id(1) == 0)
        def _():
            xf = x_ref[...]
            a_ref[...] = (xf * _rstd(xf) * g_ref[...]).astype(BF16)

        acc = _dot(a_ref[...], b_ref[...].astype(BF16))
        vals = (acc,) if epi is None else epi(acc)
        for o_ref, val in zip(o_refs, vals):
            o_ref[...] = val.astype(o_ref.dtype)

    row = pl.BlockSpec((tm, k), lambda i, j: (i, 0))
    tile = pl.BlockSpec((tm, tn), lambda i, j: (i, j))
    out = pl.pallas_call(
        body, grid=grid,
        in_specs=[row, pl.BlockSpec((1, k), lambda i, j: (0, 0)), pl.BlockSpec((k, tn), lambda i, j: (0, j))] + r_in_specs,
        out_specs=[row] + [tile] * n_out + r_out_specs,
        out_shape=[jax.ShapeDtypeStruct((m, k), BF16)] + [jax.ShapeDtypeStruct((m, n), dt) for dt in out_dtypes] + r_out,
        scratch_shapes=r_scr, compiler_params=_params("arbitrary", "arbitrary"), name=name)(x, g, b, *r_in)
    return out[:1 + n_out], out[1 + n_out:]


def _mm_norm_res(a, b, g, res, tm, name, rider=None):
    m, k = a.shape
    n = b.shape[1]
    assert m % tm == 0, (name, m, tm)
    r_in, r_in_specs, r_out, r_out_specs, r_scr = _rider_specs(rider)
    n_rin = len(r_in)

    def body(a_ref, b_ref, g_ref, res_ref, *refs):
        z_ref, o_ref = refs[n_rin:n_rin + 2]
        _ride(rider, pl.program_id(0), m // tm, refs[:n_rin] + refs[n_rin + 2:])
        z = _dot(a_ref[...].astype(BF16), b_ref[...].astype(BF16))
        z_ref[...] = z
        o_ref[...] = res_ref[...] + z * _rstd(z) * g_ref[...]

    row = pl.BlockSpec((tm, n), lambda i: (i, 0))
    out = pl.pallas_call(
        body, grid=(m // tm,),
        in_specs=[pl.BlockSpec((tm, k), lambda i: (i, 0)), pl.BlockSpec((k, n), lambda i: (0, 0)),
                  pl.BlockSpec((1, n), lambda i: (0, 0)), row] + r_in_specs,
        out_specs=[row, row] + r_out_specs,
        out_shape=[jax.ShapeDtypeStruct((m, n), F32), jax.ShapeDtypeStruct((m, n), F32)] + r_out,
        scratch_shapes=r_scr, compiler_params=_params("arbitrary"), name=name)(a, b, g, res, *r_in)
    return out[0], out[1], out[2:]


def _rmsbwd_mm(x, g, dy, b, tm, tn, name, out_dtypes=(F32,), epi=None, extras=(), rider=None, group_sums=None):
    m, k = x.shape
    n = b.shape[0]
    assert m % tm == 0 and n % tn == 0, (name, m, tm, n, tn)
    grid = (m // tm, n // tn)
    n_ex, n_out = len(extras), len(out_dtypes)
    r_in, r_in_specs, r_out, r_out_specs, r_scr = _rider_specs(rider)
    n_rin = len(r_in)
    n_gs = 0 if group_sums is None else 1
    if n_gs:
        assert tn == n and epi is None
        extras = tuple(extras) + (group_sums[0],)
        n_ex += 1

    def body(x_ref, g_ref, dy_ref, b_ref, *refs):
        ex_refs = refs[:n_ex]
        dx_ref, dg_ref = refs[n_ex + n_rin:n_ex + n_rin + 2]
        o_refs = refs[n_ex + n_rin + 2:n_ex + n_rin + 2 + n_out]
        i, j = pl.program_id(0), pl.program_id(1)
        _ride(rider, i * grid[1] + j, grid[0] * grid[1], refs[n_ex:n_ex + n_rin] + refs[n_ex + n_rin + 2 + n_out + n_gs:])

        @pl.when((i == 0) & (j == 0))
        def _():
            dg_ref[...] = jnp.zeros_like(dg_ref)

        @pl.when(j == 0)
        def _():
            dx, dg = _rms_bwd_rows(x_ref[...], g_ref[...], dy_ref[...])
            dx_ref[...] = dx.astype(BF16)
            dg_ref[...] += jnp.sum(dg, axis=0, keepdims=True)

        acc = _dot(dx_ref[...], b_ref[...].astype(BF16), NT)
        if n_gs:
            groups = group_sums[1]
            col = lax.broadcasted_iota(jnp.int32, (groups, n), 1) // (n // groups)
            sel = jnp.where(col == lax.broadcasted_iota(jnp.int32, (groups, n), 0), 1.0, 0.0)
            refs[n_ex + n_rin + 2 + n_out][...] = _dot_select(sel, acc * ex_refs[-1][...], 3, NT)
            vals = (acc,)
        else:
            vals = (acc,) if epi is None else epi(acc, *[r[...] for r in ex_refs])
        for o_ref, val in zip(o_refs, vals):
            o_ref[...] = val.astype(o_ref.dtype)

    row = pl.BlockSpec((tm, k), lambda i, j: (i, 0))
    vec = pl.BlockSpec((1, k), lambda i, j: (0, 0))
    tile = pl.BlockSpec((tm, tn), lambda i, j: (i, j))
    gs_specs = [pl.BlockSpec((group_sums[1], tm), lambda i, j: (0, i))] if n_gs else []
    gs_shape = [jax.ShapeDtypeStruct((group_sums[1], m), F32)] if n_gs else []
    out = pl.pallas_call(
        body, grid=grid,
        in_specs=[row, vec, row, pl.BlockSpec((tn, k), lambda i, j: (j, 0))] + [tile] * n_ex + r_in_specs,
        out_specs=[row, vec] + [tile] * n_out + gs_specs + r_out_specs,
        out_shape=[jax.ShapeDtypeStruct((m, k), BF16), jax.ShapeDtypeStruct((1, k), F32)]
        + [jax.ShapeDtypeStruct((m, n), dt) for dt in out_dtypes] + gs_shape + r_out,
        scratch_shapes=r_scr, compiler_params=_params("arbitrary", "arbitrary"), name=name)(x, g, dy, b, *extras, *r_in)
    return out[:2 + n_out + n_gs], out[2 + n_out + n_gs:]


def _mm_rmsbwd_res(a, b, x, g, res, tm, name, rider=None):
    m, k = a.shape
    n = b.shape[0]
    assert m % tm == 0, (name, m, tm)
    r_in, r_in_specs, r_out, r_out_specs, r_scr = _rider_specs(rider)
    n_rin = len(r_in)

    def body(a_ref, b_ref, x_ref, g_ref, res_ref, *refs):
        o_ref, dg_ref = refs[n_rin:n_rin + 2]
        _ride(rider, pl.program_id(0), m // tm, refs[:n_rin] + refs[n_rin + 2:])

        @pl.when(pl.program_id(0) == 0)
        def _():
            dg_ref[...] = jnp.zeros_like(dg_ref)

        da = _dot(a_ref[...].astype(BF16), b_ref[...].astype(BF16), NT)
        dx, dg = _rms_bwd_rows(x_ref[...], g_ref[...], da)
        o_ref[...] = res_ref[...] + dx
        dg_ref[...] += jnp.sum(dg, axis=0, keepdims=True)

    row = pl.BlockSpec((tm, n), lambda i: (i, 0))
    vec = pl.BlockSpec((1, n), lambda i: (0, 0))
    out = pl.pallas_call(
        body, grid=(m // tm,),
        in_specs=[pl.BlockSpec((tm, k), lambda i: (i, 0)), pl.BlockSpec((n, k), lambda i: (0, 0)), row, vec, row] + r_in_specs,
        out_specs=[row, vec] + r_out_specs,
        out_shape=[jax.ShapeDtypeStruct((m, n), F32), jax.ShapeDtypeStruct((1, n), F32)] + r_out,
        scratch_shapes=r_scr, compiler_params=_params("arbitrary"), name=name)(a, b, x, g, res, *r_in)
    return out[0], out[1], out[2:]


def _rope_tables(pos, name):
    s = pos.shape[0]
    half = MLA_ROPE // 2
    inv_freq = jnp.asarray(np.power(np.float32(ROPE_BASE), -np.arange(0, MLA_ROPE, 2, dtype=np.float32) / MLA_ROPE)
                           .astype(np.float32).reshape(1, half))

    def body(p_ref, f_ref, c_ref, s_ref):
        ang = p_ref[...].astype(F32) * f_ref[...]
        c_ref[...] = jnp.cos(ang)
        s_ref[...] = jnp.sin(ang)

    return pl.pallas_call(
        body, out_shape=(jax.ShapeDtypeStruct((s, half), F32), jax.ShapeDtypeStruct((s, half), F32)), name=name)(pos, inv_freq)


def _lb_softmax(logits):
    m = jnp.max(logits, axis=0, keepdims=True)
    e = jnp.exp(logits - m)
    return e / jnp.sum(e, axis=0, keepdims=True)


def _lb_fwd(logits, name):
    def body(l_ref, o_ref):
        p = _lb_softmax(l_ref[...])
        acc = jnp.zeros_like(p[0:1])
        o_ref[0:1, :] = acc
        for layer in range(1, DEPTH):
            acc = acc + p[layer:layer + 1]
            o_ref[layer:layer + 1, :] = acc

    return pl.pallas_call(body, out_shape=jax.ShapeDtypeStruct(logits.shape, F32), name=name)(logits)


def _lb_bwd(logits, dlb, name):
    def body(l_ref, d_ref, o_ref):
        p = _lb_softmax(l_ref[...])
        d = d_ref[...]
        dp = [jnp.zeros_like(d[0:1])] * DEPTH
        run = jnp.zeros_like(d[0:1])
        for layer in range(DEPTH - 1, 0, -1):
            run = run + d[layer:layer + 1]
            dp[layer] = run
        inner = sum(p[layer:layer + 1] * dp[layer] for layer in range(DEPTH))
        for layer in range(DEPTH):
            o_ref[layer:layer + 1, :] = p[layer:layer + 1] * (dp[layer] - inner)

    return pl.pallas_call(body, out_shape=jax.ShapeDtypeStruct(logits.shape, F32), name=name)(logits, dlb)


def _loss(y, target, name):
    s, d = y.shape
    ts = _row_tile(s, 512)

    def body(y_ref, t_ref, l_ref, dy_ref):
        e = y_ref[...] - t_ref[...]
        dy_ref[...] = e / d

        @pl.when(pl.program_id(0) == 0)
        def _():
            l_ref[...] = jnp.zeros_like(l_ref)

        l_ref[...] += 0.5 * jnp.sum(jnp.mean(e * e, axis=-1, keepdims=True), axis=0, keepdims=True)

    row = pl.BlockSpec((ts, d), lambda i: (i, 0))
    return pl.pallas_call(
        body, grid=(s // ts,), in_specs=[row, row], out_specs=(pl.BlockSpec((1, 1), lambda i: (0, 0)), row),
        out_shape=(jax.ShapeDtypeStruct((1, 1), F32), jax.ShapeDtypeStruct((s, d), F32)),
        compiler_params=_params("arbitrary"), name=name)(y, target)


def _rope(t1, t2, cos, sin):
    return t1 * cos - t2 * sin, t1 * sin + t2 * cos


def _rope_bwd(d1, d2, cos, sin):
    return d1 * cos + d2 * sin, d2 * cos - d1 * sin


def _mla_qkv(proj, qn, kvn, w_uq, w_ukv, cos, sin, name):
    s = proj.shape[0]
    ts = _row_tile(s, 256)
    hh, half = MLA_HEADS, MLA_ROPE // 2

    def body(p_ref, qn_ref, kvn_ref, wq_ref, wkv_ref, c_ref, s_ref, cq_ref, ckv_ref, q_ref, k_ref, v_ref):
        p = p_ref[...]
        cq, ckv, kr = p[:, :MLA_Q_LORA], p[:, MLA_Q_LORA:MLA_Q_LORA + MLA_KV_LORA], p[:, MLA_Q_LORA + MLA_KV_LORA:]
        cqn = (cq * _rstd(cq) * qn_ref[...]).astype(BF16)
        ckvn = (ckv * _rstd(ckv) * kvn_ref[...]).astype(BF16)
        cq_ref[...] = cqn
        ckv_ref[...] = ckvn
        qe = _dot(cqn, wq_ref[...])
        kve = _dot(ckvn, wkv_ref[...])
        cos_, sin_ = c_ref[...], s_ref[...]
        k1, k2 = _rope(kr[:, :half], kr[:, half:], cos_, sin_)
        k1, k2 = k1.astype(BF16), k2.astype(BF16)
        for h in range(hh):
            b = h * MLA_QK
            q_ref[h, :, 0:MLA_NOPE] = qe[:, b:b + MLA_NOPE].astype(BF16)
            q1, q2 = _rope(qe[:, b + MLA_NOPE:b + MLA_NOPE + half], qe[:, b + MLA_NOPE + half:b + MLA_QK], cos_, sin_)
            q_ref[h, :, MLA_NOPE:MLA_NOPE + half] = q1.astype(BF16)
            q_ref[h, :, MLA_NOPE + half:MLA_QK] = q2.astype(BF16)
            b = h * (MLA_NOPE + MLA_V)
            k_ref[h, :, 0:MLA_NOPE] = kve[:, b:b + MLA_NOPE].astype(BF16)
            k_ref[h, :, MLA_NOPE:MLA_NOPE + half] = k1
            k_ref[h, :, MLA_NOPE + half:MLA_QK] = k2
            v_ref[h] = kve[:, b + MLA_NOPE:b + MLA_NOPE + MLA_V].astype(BF16)

    def row(w):
        return pl.BlockSpec((ts, w), lambda i: (i, 0))

    def full(shape):
        return pl.BlockSpec(shape, lambda i: (0,) * len(shape))

    def heads(w):
        return pl.BlockSpec((hh, ts, w), lambda i: (0, i, 0))

    return pl.pallas_call(
        body, grid=(s // ts,),
        in_specs=[row(MLA_IN), full(qn.shape), full(kvn.shape), full(w_uq.shape), full(w_ukv.shape), row(half), row(half)],
        out_specs=(row(MLA_Q_LORA), row(MLA_KV_LORA), heads(MLA_QK), heads(MLA_QK), heads(MLA_V)),
        out_shape=(jax.ShapeDtypeStruct((s, MLA_Q_LORA), BF16), jax.ShapeDtypeStruct((s, MLA_KV_LORA), BF16),
                   jax.ShapeDtypeStruct((hh, s, MLA_QK), BF16), jax.ShapeDtypeStruct((hh, s, MLA_QK), BF16),
                   jax.ShapeDtypeStruct((hh, s, MLA_V), BF16)),
        compiler_params=_params("parallel"), name=name)(proj, qn, kvn, w_uq, w_ukv, cos, sin)


ATTN_BLOCK = 2048
ATTN_FWD_TILE = (256, 1024)
ATTN_BWD_TILE = (512, 512)


def _attn_block(s):
    return _row_tile(s, ATTN_BLOCK)


def _tile_sees(diag, q0, tq, k0, tk):
    if not diag:
        return True, False
    return k0 <= q0 + tq - 1, k0 + tk - 1 > q0


def _causal_pairs(nb, kv_major):
    if kv_major:
        pairs = [(i, j) for j in range(nb) for i in range(j, nb)]
    else:
        pairs = [(i, j) for i in range(nb) for j in range(i + 1)]
    return (jnp.asarray(np.array([p[0] for p in pairs], np.int32)), jnp.asarray(np.array([p[1] for p in pairs], np.int32)))


def _ride(rider, step, total, refs):
    if rider is None:
        return

    @pl.when(step == 0)
    def _():
        rider.start(*refs)

    @pl.when(step == (total * 7) // 8)
    def _():
        rider.middle(*refs)

    @pl.when(step == total - 1)
    def _():
        rider.finish(*refs)


def _rider_specs(rider):
    if rider is None:
        return [], [], [], [], []
    return (list(rider.operands), [_HBM] * len(rider.operands), list(rider.out_shapes), [_HBM] * len(rider.out_shapes),
            list(rider.scratch))


def _attn_fwd(q, k, v, name, rider=None):
    hh, s, _ = q.shape
    blk = _attn_block(s)
    tq, tk = min(blk, ATTN_FWD_TILE[0]), min(blk, ATTN_FWD_TILE[1])
    nb = s // blk
    it, jt = _causal_pairs(nb, kv_major=False)
    npair = int(it.shape[0])
    scale = MLA_QK ** -0.5
    c2 = scale * LOG2_E
    r_in, r_in_specs, r_out, r_out_specs, r_scr = _rider_specs(rider)
    n_rin, n_rout, n_rscr = len(r_in), len(r_out), len(r_scr)

    def body(it_ref, jt_ref, q_ref, k_ref, v_ref, *refs):
        r_refs = refs[:n_rin] + refs[n_rin + 2:n_rin + 2 + n_rout] + refs[len(refs) - n_rscr:]
        o_ref, lse_ref = refs[n_rin:n_rin + 2]
        m_scr, acc_scr, v_scr = refs[n_rin + 2 + n_rout:n_rin + 2 + n_rout + 3]
        h, t = pl.program_id(0), pl.program_id(1)
        step = h * npair + t
        _ride(rider, step, hh * npair, r_refs)
        i, j = it_ref[t], jt_ref[t]

        @pl.when(j == 0)
        def _():
            m_scr[...] = jnp.full_like(m_scr, -jnp.inf)
            acc_scr[...] = jnp.zeros_like(acc_scr)
            v_scr[:, MLA_V:] = jnp.ones((blk, MLA_V), BF16)

        def block(diag):
            v_scr[:, :MLA_V] = v_ref[...]
            for k0 in range(0, blk, tk):
                kb, vb = k_ref[k0:k0 + tk, :], v_scr[k0:k0 + tk, :]
                for q0 in range(0, blk, tq):
                    visible, needs_mask = _tile_sees(diag, q0, tq, k0, tk)
                    if not visible:
                        continue
                    rows = slice(q0, q0 + tq)
                    sc = _dot(q_ref[rows, :], kb, NT)
                    if needs_mask:
                        qpos = q0 + lax.broadcasted_iota(jnp.int32, (tq, tk), 0)
                        kpos = k0 + lax.broadcasted_iota(jnp.int32, (tq, tk), 1)
                        sc = jnp.where(qpos >= kpos, sc, -jnp.inf)
                    m_prev = m_scr[rows, :]
                    m_new = jnp.maximum(m_prev, jnp.max(sc, axis=-1, keepdims=True))
                    alpha = jnp.exp2((m_prev - m_new) * c2)
                    p = jnp.exp2((sc - m_new) * c2)
                    acc_scr[rows, :] = alpha * acc_scr[rows, :] + _dot(p.astype(BF16), vb)
                    m_scr[rows, :] = m_new

        @pl.when(j < i)
        def _():
            block(False)

        @pl.when(j == i)
        def _():
            block(True)
            acc = acc_scr[...]
            l = acc[:, MLA_V:MLA_V + 1]
            o_ref[...] = acc[:, :MLA_V] / l
            lse_ref[...] = m_scr[...] * scale + jnp.log(l)

    grid_spec = pltpu.PrefetchScalarGridSpec(
        num_scalar_prefetch=2, grid=(hh, npair),
        in_specs=[pl.BlockSpec((None, blk, MLA_QK), lambda h, t, it_, jt_: (h, it_[t], 0)),
                  pl.BlockSpec((None, blk, MLA_QK), lambda h, t, it_, jt_: (h, jt_[t], 0)),
                  pl.BlockSpec((None, blk, MLA_V), lambda h, t, it_, jt_: (h, jt_[t], 0))] + r_in_specs,
        out_specs=[pl.BlockSpec((blk, MLA_V), lambda h, t, it_, jt_: (it_[t], h)),
                   pl.BlockSpec((None, blk, 1), lambda h, t, it_, jt_: (h, it_[t], 0))] + r_out_specs,
        scratch_shapes=[pltpu.VMEM((blk, 1), F32), pltpu.VMEM((blk, 2 * MLA_V), F32),
                        pltpu.VMEM((blk, 2 * MLA_V), BF16)] + r_scr)
    out = pl.pallas_call(
        body, grid_spec=grid_spec,
        out_shape=[jax.ShapeDtypeStruct((s, hh * MLA_V), F32), jax.ShapeDtypeStruct((hh, s, 1), F32)] + r_out,
        compiler_params=_params("arbitrary", "arbitrary"), name=name)(it, jt, q, k, v, *r_in)
    return out[0], out[1], out[2:]


def _attn_bwd(q, k, v, do, lse_row, delta_row, name, rider=None):
    hh, s, _ = q.shape
    blk = _attn_block(s)
    tq, tk = min(blk, ATTN_BWD_TILE[0]), min(blk, ATTN_BWD_TILE[1])
    nb = s // blk
    it, jt = _causal_pairs(nb, kv_major=True)
    npair = int(it.shape[0])
    scale = MLA_QK ** -0.5
    c2 = scale * LOG2_E
    r_in, r_in_specs, r_out, r_out_specs, r_scr = _rider_specs(rider)
    n_rin, n_rout, n_rscr = len(r_in), len(r_out), len(r_scr)

    def body(it_ref, jt_ref, q_ref, k_ref, v_ref, do_ref, lse_ref, dl_ref, *refs):
        r_refs = refs[:n_rin] + refs[n_rin + 3:n_rin + 3 + n_rout] + refs[len(refs) - n_rscr:]
        dq_out, dk_out, dv_out = refs[n_rin:n_rin + 3]
        dq_ref, dk_ref, dv_ref = refs[n_rin + 3 + n_rout:n_rin + 3 + n_rout + 3]
        h, t = pl.program_id(0), pl.program_id(1)
        step = h * npair + t
        _ride(rider, step, hh * npair, r_refs)
        i, j = it_ref[t], jt_ref[t]

        @pl.when(t == 0)
        def _():
            dq_ref[...] = jnp.zeros_like(dq_ref)

        def block(diag):
            if diag:
                dk_ref[...] = jnp.zeros_like(dk_ref)
                dv_ref[...] = jnp.zeros_like(dv_ref)
            for q0 in range(0, blk, tq):
                qb = q_ref[q0:q0 + tq, :]
                dob = do_ref[q0:q0 + tq, :].astype(BF16)
                lse2 = lse_ref[:, q0:q0 + tq] * LOG2_E
                dl = dl_ref[pl.ds(h, 1), q0:q0 + tq]
                dq = None
                for k0 in range(0, blk, tk):
                    visible, needs_mask = _tile_sees(diag, q0, tq, k0, tk)
                    if not visible:
                        continue
                    kb, vb = k_ref[k0:k0 + tk, :], v_ref[k0:k0 + tk, :]
                    pt = jnp.exp2(_dot(kb, qb, NT) * c2 - lse2)
                    if needs_mask:
                        kpos = k0 + lax.broadcasted_iota(jnp.int32, (tk, tq), 0)
                        qpos = q0 + lax.broadcasted_iota(jnp.int32, (tk, tq), 1)
                        pt = jnp.where(qpos >= kpos, pt, 0.0)
                    dv_ref[k0:k0 + tk, :] += _dot(pt.astype(BF16), dob)
                    dpt = _dot(vb, dob, NT)
                    dst = (pt * (dpt - dl) * scale).astype(BF16)
                    dk_ref[k0:k0 + tk, :] += _dot(dst, qb)
                    part = _dot(dst, kb, TN)
                    dq = part if dq is None else dq + part
                rows = pl.ds(pl.multiple_of(i * blk + q0, tq), tq)
                dq_ref[rows, :] += dq

        @pl.when(i == j)
        def _():
            block(True)

        @pl.when(i > j)
        def _():
            block(False)

        @pl.when(i == nb - 1)
        def _():
            dk_out[...] = dk_ref[...].astype(BF16)
            dv_out[...] = dv_ref[...].astype(BF16)

        @pl.when(t == npair - 1)
        def _():
            dq_out[...] = dq_ref[...].astype(BF16)

    grid_spec = pltpu.PrefetchScalarGridSpec(
        num_scalar_prefetch=2, grid=(hh, npair),
        in_specs=[pl.BlockSpec((None, blk, MLA_QK), lambda h, t, it_, jt_: (h, it_[t], 0)),
                  pl.BlockSpec((None, blk, MLA_QK), lambda h, t, it_, jt_: (h, jt_[t], 0)),
                  pl.BlockSpec((None, blk, MLA_V), lambda h, t, it_, jt_: (h, jt_[t], 0)),
                  pl.BlockSpec((blk, MLA_V), lambda h, t, it_, jt_: (it_[t], h)),
                  pl.BlockSpec((None, 1, blk), lambda h, t, it_, jt_: (h, 0, it_[t])),
                  pl.BlockSpec((hh, blk), lambda h, t, it_, jt_: (0, it_[t]))] + r_in_specs,
        out_specs=[pl.BlockSpec((None, s, MLA_QK), lambda h, t, it_, jt_: (h, 0, 0)),
                   pl.BlockSpec((None, blk, MLA_QK), lambda h, t, it_, jt_: (h, jt_[t], 0)),
                   pl.BlockSpec((None, blk, MLA_V), lambda h, t, it_, jt_: (h, jt_[t], 0))] + r_out_specs,
        scratch_shapes=[pltpu.VMEM((s, MLA_QK), F32), pltpu.VMEM((blk, MLA_QK), F32), pltpu.VMEM((blk, MLA_V), F32)] + r_scr)
    out = pl.pallas_call(
        body, grid_spec=grid_spec,
        out_shape=[jax.ShapeDtypeStruct((hh, s, MLA_QK), BF16), jax.ShapeDtypeStruct((hh, s, MLA_QK), BF16),
                   jax.ShapeDtypeStruct((hh, s, MLA_V), BF16)] + r_out,
        compiler_params=_params("arbitrary", "arbitrary"), name=name)(it, jt, q, k, v, do, lse_row, delta_row, *r_in)
    return out[0], out[1], out[2], out[3:]


def _mla_bwd_mid(dq, dk, dv, cos, sin, proj, qn, kvn, w_uq, w_ukv, name):
    s = proj.shape[0]
    ts = _row_tile(s, 256)
    hh, half = MLA_HEADS, MLA_ROPE // 2
    nq, nkv = hh * MLA_QK, hh * (MLA_NOPE + MLA_V)

    def body(dq_ref, dk_ref, dv_ref, c_ref, s_ref, p_ref, qn_ref, kvn_ref, wq_ref, wkv_ref,
             dqe_ref, dkve_ref, dp_ref, dqn_ref, dkvn_ref):
        cos_, sin_ = c_ref[...], s_ref[...]
        dkr1 = jnp.zeros((ts, half), F32)
        dkr2 = jnp.zeros((ts, half), F32)
        for h in range(hh):
            dqh, dkh = dq_ref[h], dk_ref[h]
            b = h * MLA_QK
            dqe_ref[:, b:b + MLA_NOPE] = dqh[:, :MLA_NOPE].astype(BF16)
            d1, d2 = _rope_bwd(dqh[:, MLA_NOPE:MLA_NOPE + half], dqh[:, MLA_NOPE + half:], cos_, sin_)
            dqe_ref[:, b + MLA_NOPE:b + MLA_NOPE + half] = d1.astype(BF16)
            dqe_ref[:, b + MLA_NOPE + half:b + MLA_QK] = d2.astype(BF16)
            b = h * (MLA_NOPE + MLA_V)
            dkve_ref[:, b:b + MLA_NOPE] = dkh[:, :MLA_NOPE].astype(BF16)
            dkve_ref[:, b + MLA_NOPE:b + MLA_NOPE + MLA_V] = dv_ref[h].astype(BF16)
            dkr1 = dkr1 + dkh[:, MLA_NOPE:MLA_NOPE + half]
            dkr2 = dkr2 + dkh[:, MLA_NOPE + half:]
        dkr1, dkr2 = _rope_bwd(dkr1, dkr2, cos_, sin_)
        dcqn = _dot(dqe_ref[...], wq_ref[...], NT)
        dckvn = _dot(dkve_ref[...], wkv_ref[...], NT)
        p = p_ref[...]
        dcq, dqn = _rms_bwd_rows(p[:, :MLA_Q_LORA], qn_ref[...], dcqn)
        dckv, dkvn = _rms_bwd_rows(p[:, MLA_Q_LORA:MLA_Q_LORA + MLA_KV_LORA], kvn_ref[...], dckvn)
        dp_ref[:, :MLA_Q_LORA] = dcq.astype(BF16)
        dp_ref[:, MLA_Q_LORA:MLA_Q_LORA + MLA_KV_LORA] = dckv.astype(BF16)
        dp_ref[:, MLA_Q_LORA + MLA_KV_LORA:MLA_Q_LORA + MLA_KV_LORA + half] = dkr1.astype(BF16)
        dp_ref[:, MLA_Q_LORA + MLA_KV_LORA + half:] = dkr2.astype(BF16)

        @pl.when(pl.program_id(0) == 0)
        def _():
            dqn_ref[...] = jnp.zeros_like(dqn_ref)
            dkvn_ref[...] = jnp.zeros_like(dkvn_ref)

        dqn_ref[...] += jnp.sum(dqn, axis=0, keepdims=True)
        dkvn_ref[...] += jnp.sum(dkvn, axis=0, keepdims=True)

    def row(w):
        return pl.BlockSpec((ts, w), lambda i: (i, 0))

    def full(shape):
        return pl.BlockSpec(shape, lambda i: (0,) * len(shape))

    def heads(w):
        return pl.BlockSpec((hh, ts, w), lambda i: (0, i, 0))

    return pl.pallas_call(
        body, grid=(s // ts,),
        in_specs=[heads(MLA_QK), heads(MLA_QK), heads(MLA_V), row(half), row(half), row(MLA_IN),
                  full(qn.shape), full(kvn.shape), full(w_uq.shape), full(w_ukv.shape)],
        out_specs=(row(nq), row(nkv), row(MLA_IN), full(qn.shape), full(kvn.shape)),
        out_shape=(jax.ShapeDtypeStruct((s, nq), BF16), jax.ShapeDtypeStruct((s, nkv), BF16),
                   jax.ShapeDtypeStruct((s, MLA_IN), BF16), jax.ShapeDtypeStruct(qn.shape, F32),
                   jax.ShapeDtypeStruct(kvn.shape, F32)),
        compiler_params=_params("arbitrary"), name=name)(dq, dk, dv, cos, sin, proj, qn, kvn, w_uq, w_ukv)


HGRN_TILE = 256


def _chunk_masks(t):
    r = lax.broadcasted_iota(jnp.int32, (t, t), 0)
    c = lax.broadcasted_iota(jnp.int32, (t, t), 1)
    same = (r // HGRN_CHUNK) == (c // HGRN_CHUNK)
    return r, c, same


def _hgrn_gates(p, lb):
    hk = HGRN_HEADS * HGRN_D
    qx, fx, ix, gx = p[:, :hk], p[:, hk:2 * hk], p[:, 2 * hk:3 * hk], p[:, 3 * hk:]
    sig_f = jax.nn.sigmoid(fx)
    f = lb + (1.0 - lb) * sig_f
    sig_q = jax.nn.sigmoid(qx)
    t = p.shape[0]
    r, c, same = _chunk_masks(t)
    lower = jnp.where(same & (c <= r), 1.0, 0.0).astype(F32)
    b = _dot_select(lower, jnp.log(f), 3)
    b3 = b.reshape(t // HGRN_CHUNK, HGRN_CHUNK, hk)
    bref = jnp.broadcast_to(b3[:, HGRN_CHUNK // 2:HGRN_CHUNK // 2 + 1, :], b3.shape).reshape(t, hk)
    blast = jnp.broadcast_to(b3[:, HGRN_CHUNK - 1:, :], b3.shape).reshape(t, hk)
    return qx, ix, gx, sig_f, f, sig_q, b, bref, blast


def _hgrn_fwd(proj, lb, onorm, name, rider=None):
    s = proj.shape[0]
    t = _row_tile(s, HGRN_TILE)
    nc = t // HGRN_CHUNK
    hh, dd, hk = HGRN_HEADS, HGRN_D, HGRN_HEADS * HGRN_D
    r_in, r_in_specs, r_out, r_out_specs, r_scr = _rider_specs(rider)
    n_rin, n_rout = len(r_in), len(r_out)

    def body(p_ref, lb_ref, on_ref, *refs):
        y_ref, o_ref, st_ref = refs[n_rin:n_rin + 3]
        st_scr = refs[n_rin + 3 + n_rout]
        _ride(rider, pl.program_id(0), s // t, refs[:n_rin] + refs[n_rin + 3:n_rin + 3 + n_rout] + refs[n_rin + 4 + n_rout:])

        @pl.when(pl.program_id(0) == 0)
        def _():
            st_scr[...] = jnp.zeros_like(st_scr)

        qx, ix, gx, _, f, sig_q, b, bref, blast = _hgrn_gates(p_ref[...], lb_ref[...])
        q = qx * sig_q
        k = 1.0 - f
        r, c, same = _chunk_masks(t)
        causal = same & (c <= r)
        for h in range(hh):
            sl = slice(h * dd, (h + 1) * dd)
            bh, brefh, blasth, qh, kh = b[:, sl], bref[:, sl], blast[:, sl], q[:, sl], k[:, sl]
            vh = ix[:, sl].astype(BF16)
            q_rel = (qh * jnp.exp(bh - brefh)).astype(BF16)
            k_rel = (kh * jnp.exp(brefh - bh)).astype(BF16)
            a = jnp.where(causal, _dot(q_rel, k_rel, NT), 0.0)
            o_intra = _dot(a.astype(BF16), vh)
            q_dec = (qh * jnp.exp(bh)).astype(BF16)
            k_dec = (kh * jnp.exp(blasth - bh)).astype(BF16)
            dec = jnp.exp(blasth)
            pieces = []
            for ci in range(nc):
                rows = slice(ci * HGRN_CHUNK, (ci + 1) * HGRN_CHUNK)
                st = st_scr[h]
                if ci == 0:
                    st_ref[h] = st
                pieces.append(_dot(q_dec[rows], st.astype(BF16), NT))
                st_scr[h] = st * dec[ci * HGRN_CHUNK:ci * HGRN_CHUNK + 1, :] + _dot(vh[rows], k_dec[rows], TN)
            oh = o_intra + jnp.concatenate(pieces, axis=0)
            o_ref[:, sl] = oh
            gate = gx[:, sl] * jax.nn.sigmoid(gx[:, sl])
            y_ref[:, sl] = (oh * _rstd(oh) * on_ref[...] * gate).astype(BF16)

    out = pl.pallas_call(
        body, grid=(s // t,),
        in_specs=[pl.BlockSpec((t, 4 * hk), lambda i: (i, 0)), pl.BlockSpec((1, hk), lambda i: (0, 0)),
                  pl.BlockSpec((1, dd), lambda i: (0, 0))] + r_in_specs,
        out_specs=[pl.BlockSpec((t, hk), lambda i: (i, 0)), pl.BlockSpec((t, hk), lambda i: (i, 0)),
                   pl.BlockSpec((None, hh, dd, dd), lambda i: (i, 0, 0, 0))] + r_out_specs,
        out_shape=[jax.ShapeDtypeStruct((s, hk), BF16), jax.ShapeDtypeStruct((s, hk), F32),
                   jax.ShapeDtypeStruct((s // t, hh, dd, dd), F32)] + r_out,
        scratch_shapes=[pltpu.VMEM((hh, dd, dd), F32)] + r_scr,
        compiler_params=_params("arbitrary"), name=name)(proj, lb, onorm, *r_in)
    return out[0], out[1], out[2], out[3:]


def _hgrn_bwd(proj, lb, onorm, o, states, dy, name, rider=None):
    s = proj.shape[0]
    t = _row_tile(s, HGRN_TILE)
    nt = s // t
    nc = t // HGRN_CHUNK
    hh, dd, hk = HGRN_HEADS, HGRN_D, HGRN_HEADS * HGRN_D
    r_in, r_in_specs, r_out, r_out_specs, r_scr = _rider_specs(rider)
    n_rin, n_rout, n_rscr = len(r_in), len(r_out), len(r_scr)

    def body(p_ref, lb_ref, on_ref, o_ref, st_ref, dy_ref, *refs):
        r_refs = refs[:n_rin] + refs[n_rin + 3:n_rin + 3 + n_rout] + refs[len(refs) - n_rscr:]
        dp_ref, dlb_ref, don_ref = refs[n_rin:n_rin + 3]
        dst_scr, cat_scr, ext_scr, dk_scr, dq_scr = refs[n_rin + 3 + n_rout:n_rin + 3 + n_rout + 5]
        _ride(rider, pl.program_id(0), nt, r_refs)

        @pl.when(pl.program_id(0) == 0)
        def _():
            dst_scr[...] = jnp.zeros_like(dst_scr)
            dlb_ref[...] = jnp.zeros_like(dlb_ref)
            don_ref[...] = jnp.zeros_like(don_ref)

        lbv = lb_ref[...]
        qx, ix, gx, sig_f, f, sig_q, b, bref, blast = _hgrn_gates(p_ref[...], lbv)
        q = qx * sig_q
        k = 1.0 - f
        r, c, same = _chunk_masks(t)
        causal = same & (c <= r)
        on = on_ref[...]
        don = jnp.zeros((1, dd), F32)
        for h in range(hh):
            sl = slice(h * dd, (h + 1) * dd)
            oh = o_ref[:, sl]
            dyh = dy_ref[:, sl]
            gxh = gx[:, sl]
            sig_g = jax.nn.sigmoid(gxh)
            rs = _rstd(oh)
            dgate = dyh * (oh * rs * on)
            dp_ref[:, 3 * hk + h * dd:3 * hk + (h + 1) * dd] = (dgate * (sig_g * (1.0 + gxh * (1.0 - sig_g)))).astype(BF16)
            do, donh = _rms_bwd_rows(oh, on, dyh * (gxh * sig_g))
            don = don + jnp.sum(donh, axis=0, keepdims=True)
            dob = do.astype(BF16)
            bh, brefh, blasth, qh, kh = b[:, sl], bref[:, sl], blast[:, sl], q[:, sl], k[:, sl]
            vh = ix[:, sl].astype(BF16)
            e_qr, e_kr, e_qd, e_kd = jnp.exp(bh - brefh), jnp.exp(brefh - bh), jnp.exp(bh), jnp.exp(blasth - bh)
            dec = jnp.exp(blasth)
            q_rel, k_rel, q_dec, k_dec = qh * e_qr, kh * e_kr, qh * e_qd, kh * e_kd
            q_relb, k_relb, q_decb, k_decb = q_rel.astype(BF16), k_rel.astype(BF16), q_dec.astype(BF16), k_dec.astype(BF16)
            a = jnp.where(causal, _dot(q_relb, k_relb, NT), 0.0).astype(BF16)
            dv = _dot(a, dob, TN)
            da = jnp.where(causal, _dot(dob, vh, NT), 0.0).astype(BF16)
            dq_rel = _dot(da, k_relb)
            dk_rel = _dot(da, q_relb, TN)
            sts = [st_ref[h]]
            for ci in range(nc - 1):
                rows = slice(ci * HGRN_CHUNK, (ci + 1) * HGRN_CHUNK)
                sts.append(sts[-1] * dec[ci * HGRN_CHUNK:ci * HGRN_CHUNK + 1, :] + _dot(vh[rows], k_decb[rows], TN))
            dq_dec, dk_dec, dv_inter, ddec = [None] * nc, [None] * nc, [None] * nc, [None] * nc
            for ci in range(nc - 1, -1, -1):
                rows = slice(ci * HGRN_CHUNK, (ci + 1) * HGRN_CHUNK)
                st = sts[ci]
                dst = dst_scr[h]
                dstb = dst.astype(BF16)
                dq_dec[ci] = _dot(dob[rows], st.astype(BF16))
                dk_dec[ci] = _dot(vh[rows], dstb)
                dv_inter[ci] = _dot(k_decb[rows], dstb, NT)
                ddec[ci] = jnp.broadcast_to(jnp.sum(dst * st, axis=0, keepdims=True), (HGRN_CHUNK, dd))
                dst_scr[h] = dst * dec[ci * HGRN_CHUNK:ci * HGRN_CHUNK + 1, :] + _dot(dob[rows], q_decb[rows], TN)
            dq_dec = jnp.concatenate(dq_dec, axis=0)
            dk_dec = jnp.concatenate(dk_dec, axis=0)
            dv = dv + jnp.concatenate(dv_inter, axis=0)
            ddec = jnp.concatenate(ddec, axis=0)
            dp_ref[:, 2 * hk + h * dd:2 * hk + (h + 1) * dd] = dv.astype(BF16)
            dq_scr[:, sl] = dq_rel * e_qr + dq_dec * e_qd
            dk_scr[:, sl] = dk_rel * e_kr + dk_dec * e_kd
            g_qr, g_kr, g_qd, g_kd = dq_rel * q_rel, dk_rel * k_rel, dq_dec * q_dec, dk_dec * k_dec
            cat_scr[0:t, sl] = g_qr - g_kr + g_qd - g_kd
            cat_scr[t:2 * t, sl] = g_kr - g_qr
            cat_scr[2 * t:3 * t, sl] = g_kd
            ext_scr[:, sl] = ddec * dec
        upper = jnp.where(same & (c >= r), 1.0, 0.0).astype(F32)
        to_ref = jnp.where(same & (r % HGRN_CHUNK <= HGRN_CHUNK // 2), 1.0, 0.0).astype(F32)
        to_all = jnp.where(same, 1.0, 0.0).astype(F32)
        dlogf = _dot_select(jnp.concatenate([upper, to_ref, to_all], axis=1), cat_scr[...], 2) + ext_scr[...]
        df = dlogf / f - dk_scr[...]
        dp_ref[:, hk:2 * hk] = (df * (1.0 - lbv) * sig_f * (1.0 - sig_f)).astype(BF16)
        dp_ref[:, 0:hk] = (dq_scr[...] * (sig_q * (1.0 + qx * (1.0 - sig_q)))).astype(BF16)
        dlb_ref[...] += jnp.sum(df * (1.0 - sig_f), axis=0, keepdims=True)
        don_ref[...] += don

    def rev(i):
        return nt - 1 - i

    out = pl.pallas_call(
        body, grid=(nt,),
        in_specs=[pl.BlockSpec((t, 4 * hk), lambda i: (rev(i), 0)), pl.BlockSpec((1, hk), lambda i: (0, 0)),
                  pl.BlockSpec((1, dd), lambda i: (0, 0)), pl.BlockSpec((t, hk), lambda i: (rev(i), 0)),
                  pl.BlockSpec((None, hh, dd, dd), lambda i: (rev(i), 0, 0, 0)),
                  pl.BlockSpec((t, hk), lambda i: (rev(i), 0))] + r_in_specs,
        out_specs=[pl.BlockSpec((t, 4 * hk), lambda i: (rev(i), 0)), pl.BlockSpec((1, hk), lambda i: (0, 0)),
                   pl.BlockSpec((1, dd), lambda i: (0, 0))] + r_out_specs,
        out_shape=[jax.ShapeDtypeStruct((s, 4 * hk), BF16), jax.ShapeDtypeStruct((1, hk), F32),
                   jax.ShapeDtypeStruct((1, dd), F32)] + r_out,
        scratch_shapes=[pltpu.VMEM((hh, dd, dd), F32), pltpu.VMEM((3 * t, hk), F32), pltpu.VMEM((t, hk), F32),
                        pltpu.VMEM((t, hk), F32), pltpu.VMEM((t, hk), F32)] + r_scr,
        compiler_params=_params("arbitrary"), name=name)(proj, lb, onorm, o, states, dy, *r_in)
    return out[0], out[1], out[2], out[3:]


def _adamw_update(w, g, m, v):
    nm = ADAM_B1 * m + (1.0 - ADAM_B1) * g
    nv = ADAM_B2 * v + (1.0 - ADAM_B2) * (g * g)
    m_hat = nm / (1.0 - ADAM_B1 ** ADAM_STEP)
    v_hat = nv / (1.0 - ADAM_B2 ** ADAM_STEP)
    return -ADAM_LR * (m_hat / (jnp.sqrt(v_hat) + ADAM_EPS) + ADAM_WD * w), nm, nv


def _adamw(w, g, m, v, name):
    rows, cols = w.shape
    tr = _divisor_tile(rows, 256, 8)

    def body(w_ref, g_ref, m_ref, v_ref, d_ref, nm_ref, nv_ref):
        d_ref[...], nm_ref[...], nv_ref[...] = _adamw_update(w_ref[...], g_ref[...], m_ref[...], v_ref[...])

    blk = pl.BlockSpec((tr, cols), lambda i: (i, 0))
    shp = jax.ShapeDtypeStruct((rows, cols), F32)
    return pl.pallas_call(
        body, grid=(rows // tr,), in_specs=[blk] * 4, out_specs=(blk,) * 3, out_shape=(shp,) * 3,
        compiler_params=_params("parallel"), name=name)(w, g, m, v)


ADAMW_BLOCK_ELEMS = 128 * 1024


def _adamw_layers(w, lands, m, v, name, rider=None):
    ll, rows, cols = w.shape
    tr = _divisor_tile(rows, max(16, ADAMW_BLOCK_ELEMS // cols), 16)
    r_in, r_in_specs, r_out, r_out_specs, r_scr = _rider_specs(rider)
    n_rin = len(r_in)

    def body(w_ref, m_ref, v_ref, *refs):
        land_refs = refs[:ll]
        g_out, d_ref, nm_ref, nv_ref = refs[ll + n_rin:ll + n_rin + 4]
        layer = pl.program_id(0)
        _ride(rider, layer * (rows // tr) + pl.program_id(1), ll * (rows // tr), refs[ll:ll + n_rin] + refs[ll + n_rin + 4:])
        for k in range(ll):
            @pl.when(layer == k)
            def _(k=k):
                g = land_refs[k][0].astype(F32)
                for slot in range(1, N_DEV):
                    g = g + land_refs[k][slot].astype(F32)
                g_out[...] = g

        d_ref[...], nm_ref[...], nv_ref[...] = _adamw_update(w_ref[...], g_out[...], m_ref[...], v_ref[...])

    stacked = pl.BlockSpec((None, tr, cols), lambda l, i: (l, i, 0))

    def one(k):
        return pl.BlockSpec((N_DEV, tr, cols), lambda l, i: (0, jnp.where(l == k, i, 0), 0))

    shp = jax.ShapeDtypeStruct(w.shape, F32)
    out = pl.pallas_call(
        body, grid=(ll, rows // tr), in_specs=[stacked] * 3 + [one(k) for k in range(ll)] + r_in_specs,
        out_specs=[stacked] * 4 + r_out_specs, out_shape=[shp] * 4 + r_out, scratch_shapes=r_scr,
        compiler_params=_params("arbitrary", "arbitrary"), name=name)(w, m, v, *lands, *r_in)
    return out[0], out[1], out[2], out[3], out[4:]


_HBM = pl.BlockSpec(memory_space=pltpu.HBM)
_MESH = pl.DeviceIdType.MESH


class _GatherRide:
    def __init__(self, blocks, cuts):
        self.operands = list(blocks)
        self.cuts = list(cuts)
        self.out_shapes = []
        for b, cut in zip(blocks, cuts):
            r, c = b.shape
            shape = {"rows": (N_DEV * r, c), "cols": (r, N_DEV * c), "slots": (N_DEV, r, c)}[cut]
            self.out_shapes.append(jax.ShapeDtypeStruct(shape, b.dtype))
        n = len(blocks)
        self.scratch = [pltpu.SemaphoreType.DMA((7 * n,)), pltpu.SemaphoreType.DMA((7 * n,)), pltpu.SemaphoreType.DMA((n,))]

    def _parts(self, *refs):
        n = len(self.operands)
        x_refs, out_refs = refs[:n], refs[n:2 * n]
        send_sems, recv_sems, local_sems = refs[2 * n:]
        x, y, c = lax.axis_index("x"), lax.axis_index("y"), lax.axis_index("c")
        me, sibling = (x, y, c), (x, y, 1 - c)
        chips = [(1 - x, y), (x, 1 - y), (1 - x, 1 - y)]
        mine, first, passed, landed, from_sibling = [], [], [], [], []
        for e in range(n):
            x_ref, out_ref, cut = x_refs[e], out_refs[e], self.cuts[e]
            r, cc = x_ref.shape

            def place(px, py, pc, out_ref=out_ref, cut=cut, r=r, cc=cc):
                p = 4 * px + 2 * py + pc
                if cut == "rows":
                    return out_ref.at[pl.ds(pl.multiple_of(p * r, r), r), :]
                if cut == "cols":
                    return out_ref.at[:, pl.ds(pl.multiple_of(p * cc, cc), cc)]
                return out_ref.at[p]

            def copy(k, block, to, src=None, place=place, e=e):
                return pltpu.make_async_remote_copy(
                    src_ref=place(*block) if src is None else src, dst_ref=place(*block), send_sem=send_sems.at[7 * e + k],
                    recv_sem=recv_sems.at[7 * e + k], device_id=to, device_id_type=_MESH)

            mine.append(pltpu.make_async_copy(x_ref, place(*me), local_sems.at[e]))
            first += [copy(0, me, sibling, src=x_ref)] + [copy(1 + j, me, (*chip, c), src=x_ref) for j, chip in enumerate(chips)]
            passed += [copy(4 + j, (*chip, c), sibling) for j, chip in enumerate(chips)]
            landed += [copy(1 + j, (*chip, c), me) for j, chip in enumerate(chips)]
            from_sibling += [copy(0, sibling, me)] + [copy(4 + j, (*chip, 1 - c), me) for j, chip in enumerate(chips)]
        return mine, first, passed, landed, from_sibling

    def start(self, *refs):
        mine, first, _, _, _ = self._parts(*refs)
        for cp in mine + first:
            cp.start()

    def middle(self, *refs):
        _, _, passed, landed, _ = self._parts(*refs)
        for got, fwd in zip(landed, passed):
            got.wait_recv()
            fwd.start()

    def finish(self, *refs):
        mine, first, passed, _, from_sibling = self._parts(*refs)
        for cp in from_sibling:
            cp.wait_recv()
        for cp in first + passed:
            cp.wait_send()
        for cp in mine:
            cp.wait()


class _ExchangeRide:
    def __init__(self, sends):
        self.operands = list(sends)
        self.out_shapes = [jax.ShapeDtypeStruct(s.shape, s.dtype) for s in sends]
        n = len(sends)
        self.scratch = [pltpu.SemaphoreType.DMA((7 * n,)), pltpu.SemaphoreType.DMA((7 * n,)), pltpu.SemaphoreType.DMA((n,))]

    def _parts(self, *refs):
        n = len(self.operands)
        s_refs, land_refs = refs[:n], refs[n:2 * n]
        send_sems, recv_sems, local_sems = refs[2 * n:]
        x, y, c = lax.axis_index("x"), lax.axis_index("y"), lax.axis_index("c")
        me = 4 * x + 2 * y + c
        own, sends, recvs = [], [], []
        for e in range(n):
            s_ref, land_ref = s_refs[e], land_refs[e]
            own.append(pltpu.make_async_copy(s_ref.at[me], land_ref.at[me], local_sems.at[e]))
            for rel in range(1, N_DEV):
                px = 1 - x if rel & 4 else x
                py = 1 - y if rel & 2 else y
                pc = 1 - c if rel & 1 else c
                peer = 4 * px + 2 * py + pc
                k = 7 * e + rel - 1
                sends.append(pltpu.make_async_remote_copy(
                    src_ref=s_ref.at[peer], dst_ref=land_ref.at[me], send_sem=send_sems.at[k], recv_sem=recv_sems.at[k],
                    device_id=(px, py, pc), device_id_type=_MESH))
                recvs.append(pltpu.make_async_remote_copy(
                    src_ref=s_ref.at[me], dst_ref=land_ref.at[peer], send_sem=send_sems.at[k], recv_sem=recv_sems.at[k],
                    device_id=(px, py, pc), device_id_type=_MESH))
        return own, sends, recvs

    def start(self, *refs):
        own, sends, _ = self._parts(*refs)
        for cp in own + sends:
            cp.start()

    def middle(self, *refs):
        pass

    def finish(self, *refs):
        own, sends, recvs = self._parts(*refs)
        for cp in recvs:
            cp.wait_recv()
        for cp in sends:
            cp.wait_send()
        for cp in own:
            cp.wait()


def _run_alone(rider, name):
    def body(*refs):
        rider.start(*refs)
        rider.middle(*refs)
        rider.finish(*refs)

    return pl.pallas_call(
        body, out_shape=rider.out_shapes, in_specs=[_HBM] * len(rider.operands), out_specs=[_HBM] * len(rider.out_shapes),
        scratch_shapes=rider.scratch, name=name)(*rider.operands)


def _all_gather(xs, name):
    return _run_alone(_GatherRide([xs], ["slots"]), name)[0]


def _sum_slots(parts, name):
    _, rows, cols = parts.shape
    tr = _divisor_tile(rows, 256, 16)

    def body(p_ref, o_ref):
        acc = p_ref[0].astype(F32)
        for slot in range(1, N_DEV):
            acc = acc + p_ref[slot].astype(F32)
        o_ref[...] = acc

    return pl.pallas_call(
        body, grid=(rows // tr,), in_specs=[pl.BlockSpec((N_DEV, tr, cols), lambda i: (0, i, 0))],
        out_specs=pl.BlockSpec((tr, cols), lambda i: (i, 0)), out_shape=jax.ShapeDtypeStruct((rows, cols), F32),
        compiler_params=_params("parallel"), name=name)(parts)


def _carry(rode, key, riders, call):
    rider = riders.get(key)
    res = call(rider)
    if rider is None:
        return res
    res, rode[key] = res
    return res


def _kept(rode, key, riders, brought):
    if key in riders:
        rode[key] = brought


def _mlp_fwd(h, g_pre, g_post, w1, w2, tag, riders):
    rode = {}
    tm = _row_tile(h.shape[0], 2048)
    (a, r2), brought = _norm_mm(h, g_pre, w1, tm, 1024, f"{tag}_up", out_dtypes=(BF16,),
                                epi=lambda acc: (jnp.square(jnp.maximum(acc, 0.0)),), rider=riders.get("up"))
    _kept(rode, "up", riders, brought)
    z, out, brought = _mm_norm_res(r2, w2, g_post, h, 512, f"{tag}_down", riders.get("down"))
    _kept(rode, "down", riders, brought)
    return out, (h, a, r2, z), rode


def _mlp_bwd(dh, saved, g_pre, g_post, w1, w2, tag, riders):
    h, a, r2, z = saved
    rode = {}
    tm = _row_tile(h.shape[0], 1024)
    (dz, dg_post, du), brought = _rmsbwd_mm(
        z, g_post, dh, w2, tm, 2048, f"{tag}_ddown", out_dtypes=(BF16,), extras=(r2,),
        epi=lambda acc, rr: (acc * (2.0 * jnp.sqrt(rr.astype(F32))),), rider=riders.get("ddown"))
    _kept(rode, "ddown", riders, brought)
    dw2 = _carry(rode, "dw2", riders, lambda r: _mm(
        r2, dz, "tn", 512, 1024, f"{tag}_dw2", out_dtypes=(BF16,), shard="rows", rider=r))
    dw1 = _mm(a, du, "tn", 1024, 512, f"{tag}_dw1", out_dtypes=(BF16,), shard="cols")
    dh_in, dg_pre, _ = _mm_rmsbwd_res(du, w1, h, g_pre, dh, 512, f"{tag}_dup")
    return dh_in, dg_pre, dg_post, dw1, dw2, rode


def _hgrn_layer_fwd(h, g_pre, g_post, lb, onorm, w_in, w_o, tag, riders):
    rode = {}
    (a, proj), brought = _norm_mm(h, g_pre, w_in, _row_tile(h.shape[0], 2048), 1024, f"{tag}_in", rider=riders.get("in"))
    _kept(rode, "in", riders, brought)
    y, o, states, brought = _hgrn_fwd(proj, lb, onorm, f"{tag}_scan", riders.get("scan"))
    _kept(rode, "scan", riders, brought)
    m, out, _ = _mm_norm_res(y, w_o, g_post, h, 512, f"{tag}_o")
    return out, (h, a, proj, y, o, states, m), rode


def _hgrn_layer_bwd(dh, saved, g_pre, g_post, lb, onorm, w_in, w_o, tag, rider=None):
    h, a, proj, y, o, states, m = saved
    (dm, dg_post, dy), _ = _rmsbwd_mm(m, g_post, dh, w_o, 512, 1024, f"{tag}_do")
    dw_o = _mm(y, dm, "tn", 128, 1024, f"{tag}_dwo", out_dtypes=(BF16,), shard="rows")
    dproj, dlb, donorm, rode = _hgrn_bwd(proj, lb, onorm, o, states, dy, f"{tag}_dscan", rider)
    dw_in = _mm(a, dproj, "tn", 1024, 512, f"{tag}_dwin", out_dtypes=(BF16,), shard="cols")
    dh_in, dg_pre, _ = _mm_rmsbwd_res(dproj, w_in, h, g_pre, dh, 512, f"{tag}_din")
    return dh_in, dg_pre, dg_post, dlb, donorm, dw_in, dw_o, rode


def _mla_layer_fwd(h, g_pre, g_post, cos, sin, w_in, late, qn, kvn, tag, riders):
    rode = {}
    (a, proj), brought = _norm_mm(h, g_pre, w_in, 512, MLA_IN, f"{tag}_in", rider=riders.get("in"))
    _kept(rode, "in", riders, brought)
    w_uq, w_ukv, _ = late(rode)
    cqn, ckvn, q, k, v = _mla_qkv(proj, qn, kvn, w_uq, w_ukv, cos, sin, f"{tag}_qkv")
    o, lse, brought = _attn_fwd(q, k, v, f"{tag}_attn", riders.get("attn"))
    _kept(rode, "attn", riders, brought)
    _, _, w_o = late(rode)
    m, out, _ = _mm_norm_res(o, w_o, g_post, h, 512, f"{tag}_o")
    return out, (h, a, proj, cqn, ckvn, q, k, v, o, lse, m), rode


def _mla_layer_bwd(dh, saved, g_pre, g_post, cos, sin, w_in, qn, kvn, w_uq, w_ukv, w_o, tag, rider=None, own_ride=False):
    h, a, proj, cqn, ckvn, q, k, v, o, lse, m = saved
    hh, s = q.shape[0], q.shape[1]
    own = {}
    (dm, dg_post, do, delta), _ = _rmsbwd_mm(m, g_post, dh, w_o, 512, 1024, f"{tag}_do", group_sums=(o, hh))
    dw_o = _mm(o, dm, "tn", 128, 1024, f"{tag}_dwo", out_dtypes=(BF16,), shard="rows")
    if own_ride:
        rider = _ExchangeRide(list(rider.operands) + [dw_o])
    dq, dk, dv, rode = _attn_bwd(q, k, v, do, lse.reshape(hh, 1, s), delta, f"{tag}_dattn", rider)
    if own_ride:
        own["mla_w_o"] = rode[-1]
    dqe, dkve, dproj, dqn, dkvn = _mla_bwd_mid(dq, dk, dv, cos, sin, proj, qn, kvn, w_uq, w_ukv, f"{tag}_dqkv")
    dw_uq = _mm(cqn, dqe, "tn", MLA_Q_LORA, 768, f"{tag}_dwuq", out_dtypes=(BF16,))
    dw_uq = dw_uq.reshape(MLA_Q_LORA, N_DEV, -1).transpose(1, 0, 2)
    dw_ukv = _mm(ckvn, dkve, "tn", MLA_KV_LORA, 256, f"{tag}_dwukv", out_dtypes=(BF16,), shard="cols")
    dw_in = _mm(a, dproj, "tn", 128, MLA_IN, f"{tag}_dwin", out_dtypes=(BF16,), shard="rows",
                rider=_ExchangeRide([dw_uq, dw_ukv]) if own_ride else None)
    if own_ride:
        dw_in, (own["mla_w_uq"], own["mla_w_ukv"]) = dw_in
    dh_in, dg_pre, brought = _mm_rmsbwd_res(dproj, w_in, h, g_pre, dh, 512, f"{tag}_din",
                                             _ExchangeRide([dw_in]) if own_ride else None)
    if own_ride:
        own["mla_w_in"] = brought[0]
    return dh_in, dg_pre, dg_post, dqn, dkvn, dw_in, dw_uq, dw_ukv, dw_o, rode, own


_CUT = dict(mla_w_in="rows", mla_w_uq="cols", mla_w_ukv="cols", mla_w_o="rows", hgrn_w_in="cols", hgrn_w_o="rows",
            mlp_w1="cols", mlp_w2="rows")


def _unit(layer, kind):
    slot = layer // 2
    if kind == "mla":
        return [("mla_w_in", slot), ("mla_w_uq", slot), ("mla_w_ukv", slot), ("mla_w_o", slot)]
    if kind == "hgrn":
        return [("hgrn_w_in", slot), ("hgrn_w_o", slot)]
    return [("mlp_w1", layer), ("mlp_w2", layer)]


_GATHER_FIRST = [("mla_w_in", 0)]
_GATHER_PLAN = {
    (0, "in"): [("mla_w_uq", 0), ("mla_w_ukv", 0)],
    (0, "attn"): [("mla_w_o", 0)] + _unit(0, "mlp") + _unit(1, "hgrn"),
    (0, "up"): [("mlp_w1", 1)],
    (0, "down"): [("mlp_w2", 1)],
    (1, "in"): _unit(2, "mla"),
    (1, "scan"): _unit(2, "mlp"),
    (2, "attn"): _unit(3, "hgrn") + _unit(3, "mlp"),
}
_EXCHANGE_PLAN = {
    (3, "dscan"): _unit(3, "mlp"),
    (2, "ddown"): [("hgrn_w_in", 1)],
    (2, "dw2"): [("hgrn_w_o", 1)],
    (2, "dattn"): _unit(2, "mlp"),
    (1, "ddown"): _unit(2, "mla"),
    (1, "dscan"): _unit(1, "mlp"),
    (0, "ddown"): [("hgrn_w_in", 0)],
    (0, "dw2"): [("hgrn_w_o", 0)],
    (0, "dattn"): _unit(0, "mlp"),
}


def _gather_cut(name):
    return "slots" if name == "mla_w_uq" else _CUT[name]


def _gather_rider(weights, ents):
    return _GatherRide([weights[name][idx].astype(BF16) for name, idx in ents], [_gather_cut(name) for name, _ in ents])


def _gathered(outs, ents):
    res = {}
    for (name, idx), out in zip(ents, outs):
        if _gather_cut(name) == "slots":
            out = out.transpose(1, 0, 2).reshape(out.shape[1], -1)
        res[(name, idx)] = out
    return res


def _adamw_nd(w, g, m, v, name):
    shape = w.shape
    c = shape[-1]
    d, nm, nv = _adamw(w.reshape(-1, c), g.reshape(-1, c), m.reshape(-1, c), v.reshape(-1, c), name)
    return d.reshape(shape), nm.reshape(shape), nv.reshape(shape)


def kernel(x, positions, norm_gains, mla_w_in, mla_q_norm, mla_kv_norm, mla_w_uq, mla_w_ukv, mla_w_o, hgrn_w_in, hgrn_lb_logits, hgrn_o_norm, hgrn_w_o, mlp_w1, mlp_w2, loss_target, m_norm_gains, m_mla_w_in, m_mla_q_norm, m_mla_kv_norm, m_mla_w_uq, m_mla_w_ukv, m_mla_w_o, m_hgrn_w_in, m_hgrn_lb_logits, m_hgrn_o_norm, m_hgrn_w_o, m_mlp_w1, m_mlp_w2, v_norm_gains, v_mla_w_in, v_mla_q_norm, v_mla_kv_norm, v_mla_w_uq, v_mla_w_ukv, v_mla_w_o, v_hgrn_w_in, v_hgrn_lb_logits, v_hgrn_o_norm, v_hgrn_w_o, v_mlp_w1, v_mlp_w2):
    weights = dict(norm_gains=norm_gains, mla_w_in=mla_w_in, mla_q_norm=mla_q_norm, mla_kv_norm=mla_kv_norm,
                   mla_w_uq=mla_w_uq, mla_w_ukv=mla_w_ukv, mla_w_o=mla_w_o, hgrn_w_in=hgrn_w_in,
                   hgrn_lb_logits=hgrn_lb_logits, hgrn_o_norm=hgrn_o_norm, hgrn_w_o=hgrn_w_o, mlp_w1=mlp_w1, mlp_w2=mlp_w2)
    mom_m = dict(norm_gains=m_norm_gains, mla_w_in=m_mla_w_in, mla_q_norm=m_mla_q_norm, mla_kv_norm=m_mla_kv_norm,
                 mla_w_uq=m_mla_w_uq, mla_w_ukv=m_mla_w_ukv, mla_w_o=m_mla_w_o, hgrn_w_in=m_hgrn_w_in,
                 hgrn_lb_logits=m_hgrn_lb_logits, hgrn_o_norm=m_hgrn_o_norm, hgrn_w_o=m_hgrn_w_o, mlp_w1=m_mlp_w1, mlp_w2=m_mlp_w2)
    mom_v = dict(norm_gains=v_norm_gains, mla_w_in=v_mla_w_in, mla_q_norm=v_mla_q_norm, mla_kv_norm=v_mla_kv_norm,
                 mla_w_uq=v_mla_w_uq, mla_w_ukv=v_mla_w_ukv, mla_w_o=v_mla_w_o, hgrn_w_in=v_hgrn_w_in,
                 hgrn_lb_logits=v_hgrn_lb_logits, hgrn_o_norm=v_hgrn_o_norm, hgrn_w_o=v_hgrn_w_o, mlp_w1=v_mlp_w1, mlp_w2=v_mlp_w2)
    order = list(weights)
    seq = x.shape[1]
    h = x.reshape(seq, D_MODEL)
    target = loss_target.reshape(seq, D_MODEL)

    first = _gather_rider(weights, _GATHER_FIRST)
    first = _GatherRide(first.operands + [norm_gains.reshape(DEPTH * 4, D_MODEL // N_DEV)], first.cuts + ["slots"])
    *first_w, gains = _run_alone(first, "gather_first")
    full = _gathered(first_w, _GATHER_FIRST)
    gains = gains.transpose(1, 0, 2).reshape(DEPTH, 4, 1, D_MODEL)

    def gather_riders(layer, keys):
        return {k: _gather_rider(weights, _GATHER_PLAN[(layer, k)]) for k in keys if (layer, k) in _GATHER_PLAN}

    seen = set()

    def arrived(layer, rode):
        for k, outs in rode.items():
            if (layer, k) not in seen:
                seen.add((layer, k))
                full.update(_gathered(outs, _GATHER_PLAN[(layer, k)]))

    def late_weights(layer):
        def late(rode):
            arrived(layer, rode)
            return tuple(full.get((name, layer // 2)) for name in ("mla_w_uq", "mla_w_ukv", "mla_w_o"))
        return late

    cos, sin = _rope_tables(positions.reshape(seq, 1), "rope_tables")
    lower = _lb_fwd(hgrn_lb_logits, "lower_bounds")

    def mixer_args(layer):
        slot = layer // 2
        if layer % 2 == 0:
            return (cos, sin, full[("mla_w_in", slot)], mla_q_norm[slot:slot + 1], mla_kv_norm[slot:slot + 1],
                    full[("mla_w_uq", slot)], full[("mla_w_ukv", slot)], full[("mla_w_o", slot)])
        return (lower[layer:layer + 1], hgrn_o_norm[slot:slot + 1], full[("hgrn_w_in", slot)], full[("hgrn_w_o", slot)])

    saved = []
    for layer in range(DEPTH):
        g = gains[layer]
        if layer % 2 == 0:
            slot = layer // 2
            h, sv_mix, rode = _mla_layer_fwd(
                h, g[0], g[1], cos, sin, full[("mla_w_in", slot)], late_weights(layer), mla_q_norm[slot:slot + 1],
                mla_kv_norm[slot:slot + 1], f"l{layer}_mla", gather_riders(layer, ["in", "attn"]))
            arrived(layer, rode)
        else:
            h, sv_mix, rode = _hgrn_layer_fwd(h, g[0], g[1], *mixer_args(layer), f"l{layer}_hgrn",
                                              gather_riders(layer, ["in", "scan"]))
            arrived(layer, rode)
        h, sv_mlp, rode = _mlp_fwd(h, g[2], g[3], full[("mlp_w1", layer)], full[("mlp_w2", layer)], f"l{layer}_mlp",
                                   gather_riders(layer, ["up", "down"]))
        arrived(layer, rode)
        saved.append((sv_mix, sv_mlp))

    loss_part, dh = _loss(h, target, "loss")
    loss = lax.psum(loss_part[0, 0], AXES)

    zero_row = jnp.zeros((1, D_MODEL), F32)
    dgains = [[None] * 4 for _ in range(DEPTH)]
    dlower = [zero_row] * DEPTH
    partials, lands = {}, {}
    dqn, dkvn, donorm = [None] * 2, [None] * 2, [None] * 2

    def exchange_riders(layer, keys):
        return {k: _ExchangeRide([partials[e] for e in _EXCHANGE_PLAN[(layer, k)]]) for k in keys
                if (layer, k) in _EXCHANGE_PLAN}

    def landed(layer, rode):
        for k, outs in rode.items():
            lands.update(zip(_EXCHANGE_PLAN[(layer, k)], outs))

    for layer in range(DEPTH - 1, -1, -1):
        slot = layer // 2
        g = gains[layer]
        sv_mix, sv_mlp = saved[layer]
        dh, dgains[layer][2], dgains[layer][3], partials[("mlp_w1", layer)], partials[("mlp_w2", layer)], rode = _mlp_bwd(
            dh, sv_mlp, g[2], g[3], full[("mlp_w1", layer)], full[("mlp_w2", layer)], f"l{layer}_mlp",
            exchange_riders(layer, ["ddown", "dw2"]))
        landed(layer, rode)
        key = "dattn" if layer % 2 == 0 else "dscan"
        rider = exchange_riders(layer, [key]).get(key)
        if layer % 2 == 0:
            (dh, dgains[layer][0], dgains[layer][1], dqn[slot], dkvn[slot], partials[("mla_w_in", slot)],
             partials[("mla_w_uq", slot)], partials[("mla_w_ukv", slot)], partials[("mla_w_o", slot)], brought,
             own) = _mla_layer_bwd(dh, sv_mix, g[0], g[1], *mixer_args(layer), f"l{layer}_mla", rider, own_ride=(layer == 0))
            lands.update({(name, slot): land for name, land in own.items()})
        else:
            (dh, dgains[layer][0], dgains[layer][1], dlower[layer], donorm[slot], partials[("hgrn_w_in", slot)],
             partials[("hgrn_w_o", slot)], brought) = _hgrn_layer_bwd(dh, sv_mix, g[0], g[1], *mixer_args(layer), f"l{layer}_hgrn", rider)
        landed(layer, {key: brought} if rider is not None else {})
    grad_x = dh.reshape(x.shape)
    dlogits = _lb_bwd(hgrn_lb_logits, jnp.concatenate(dlower, axis=0), "lower_bounds_bwd")

    pad = jnp.zeros((1, D_MODEL - 2 * MLA_KV_LORA), F32)
    pad2 = jnp.zeros((1, D_MODEL - 2 * HGRN_D), F32)
    small = jnp.concatenate(
        [jnp.concatenate([gg for row in dgains for gg in row], axis=0), jnp.concatenate(dqn, axis=1),
         jnp.concatenate(dkvn + [pad], axis=1), dlogits, jnp.concatenate(donorm + [pad2], axis=1), zero_row], axis=0)
    small = _sum_slots(_all_gather(small, "gather_small_grads"), "sum_small_grads")
    me = 4 * lax.axis_index("x") + 2 * lax.axis_index("y") + lax.axis_index("c")
    n_g = DEPTH * 4
    width = D_MODEL // N_DEV
    grads = {}
    grads["norm_gains"] = lax.dynamic_slice(small[:n_g], (0, me * width), (n_g, width)).reshape(DEPTH, 4, width)
    grads["mla_q_norm"] = small[n_g].reshape(2, MLA_Q_LORA)
    grads["mla_kv_norm"] = small[n_g + 1, :2 * MLA_KV_LORA].reshape(2, MLA_KV_LORA)
    grads["hgrn_lb_logits"] = small[n_g + 2:n_g + 2 + DEPTH]
    grads["hgrn_o_norm"] = small[n_g + 2 + DEPTH, :2 * HGRN_D].reshape(2, HGRN_D)

    deltas, new_m, new_v = {}, {}, {}
    for name in order:
        if name in _CUT:
            per_layer = [lands[(name, idx)] for idx in range(weights[name].shape[0])]
            grads[name], deltas[name], new_m[name], new_v[name], _ = _adamw_layers(
                weights[name], per_layer, mom_m[name], mom_v[name], f"adamw_{name}")
        else:
            deltas[name], new_m[name], new_v[name] = _adamw_nd(weights[name], grads[name], mom_m[name], mom_v[name], f"adamw_{name}")
    return (loss, grad_x, *[grads[n] for n in order], *[deltas[n] for n in order], *[new_m[n] for n in order],
            *[new_v[n] for n in order])
```

```python
import numpy as np
import jax
import jax.numpy as jnp
from jax import lax
from jax.experimental import pallas as pl
from jax.experimental.pallas import tpu as pltpu

F32, BF16 = jnp.float32, jnp.bfloat16

N_DEV = 8
AXES = ("x", "y", "c")
D_MODEL = 1024
DEPTH = 4
MLA_HEADS = 8
MLA_Q_LORA = 512
MLA_KV_LORA = 256
MLA_NOPE = 128
MLA_ROPE = 64
MLA_V = 128
MLA_QK = MLA_NOPE + MLA_ROPE
MLA_IN = MLA_Q_LORA + MLA_KV_LORA + MLA_ROPE
ROPE_BASE = 10000.0
HGRN_HEADS = 8
HGRN_D = 128
HGRN_CHUNK = 32
D_FF = 4 * D_MODEL
EPS = 1e-6
LOG2_E = 1.4426950408889634
ADAM_LR, ADAM_B1, ADAM_B2, ADAM_EPS, ADAM_WD, ADAM_STEP = 0.001, 0.9, 0.999, 1e-08, 0.01, 10

V7X_VMEM_LIMIT_BYTES = 56 * 1024 * 1024

NN = (((1,), (0,)), ((), ()))
NT = (((1,), (1,)), ((), ()))
TN = (((0,), (0,)), ((), ()))
_DIMS = {"nn": NN, "nt": NT, "tn": TN}


def _params(*sem):
    return pltpu.CompilerParams(dimension_semantics=sem, vmem_limit_bytes=V7X_VMEM_LIMIT_BYTES)


def _dot(a, b, dims=NN):
    return lax.dot_general(a, b, dims, preferred_element_type=F32)


def _dot_select(sel, x, pieces, dims=NN):
    sel = sel.astype(BF16)
    acc, rest = None, x
    for _ in range(pieces):
        term = rest.astype(BF16)
        part = _dot(sel, term, dims)
        acc = part if acc is None else acc + part
        rest = rest - term.astype(F32)
    return acc


def _rstd(x):
    return lax.rsqrt(jnp.mean(x * x, axis=-1, keepdims=True) + EPS)


def _rms_bwd_rows(x, g, dy):
    r = _rstd(x)
    xh = x * r
    dyg = dy * g
    dx = r * (dyg - xh * jnp.mean(dyg * xh, axis=-1, keepdims=True))
    return dx, dy * xh


def _row_tile(n, want):
    t = min(n, want)
    assert n % t == 0, (n, t)
    return t


def _divisor_tile(n, cap, mult):
    for t in range(min(cap, n) - min(cap, n) % mult, 0, -mult):
        if n % t == 0:
            return t
    return n


def _rms_fwd(x, g, res, out_dtype, name):
    s, d = x.shape
    ts = _row_tile(s, 512)

    def body(x_ref, g_ref, *rest):
        xf = x_ref[...]
        y = xf * _rstd(xf) * g_ref[...]
        if res is not None:
            y = rest[0][...] + y
        rest[-1][...] = y.astype(out_dtype)

    row = pl.BlockSpec((ts, d), lambda i: (i, 0))
    vec = pl.BlockSpec((1, d), lambda i: (0, 0))
    ins = [x, g] + ([res] if res is not None else [])
    return pl.pallas_call(
        body, grid=(s // ts,), in_specs=[row, vec] + ([row] if res is not None else []), out_specs=row,
        out_shape=jax.ShapeDtypeStruct((s, d), out_dtype), compiler_params=_params("parallel"), name=name)(*ins)


def _rms_bwd(x, g, dy, res, out_dtype, name):
    s, d = x.shape
    ts = _row_tile(s, 512)

    def body(x_ref, g_ref, dy_ref, *rest):
        dx_ref, dg_ref = rest[-2:]
        dx, dg = _rms_bwd_rows(x_ref[...], g_ref[...], dy_ref[...].astype(F32))
        if res is not None:
            dx = rest[0][...] + dx
        dx_ref[...] = dx.astype(out_dtype)

        @pl.when(pl.program_id(0) == 0)
        def _():
            dg_ref[...] = jnp.zeros_like(dg_ref)

        dg_ref[...] += jnp.sum(dg, axis=0, keepdims=True)

    row = pl.BlockSpec((ts, d), lambda i: (i, 0))
    vec = pl.BlockSpec((1, d), lambda i: (0, 0))
    ins = [x, g, dy] + ([res] if res is not None else [])
    return pl.pallas_call(
        body, grid=(s // ts,), in_specs=[row, vec, row] + ([row] if res is not None else []), out_specs=(row, vec),
        out_shape=(jax.ShapeDtypeStruct((s, d), out_dtype), jax.ShapeDtypeStruct((1, d), F32)),
        compiler_params=_params("arbitrary"), name=name)(*ins)


def _mm(a, b, mode, tm, tn, name, out_dtypes=(F32,), shard=None, epi=None, extras=(), rider=None):
    if mode == "tn":
        k, m = a.shape
        a_spec = pl.BlockSpec((k, tm), lambda i, j: (0, i))
    else:
        m, k = a.shape
        a_spec = pl.BlockSpec((tm, k), lambda i, j: (i, 0))
    if mode == "nt":
        n = b.shape[0]
        b_spec = pl.BlockSpec((tn, k), lambda i, j: (j, 0))
    else:
        n = b.shape[1]
        b_spec = pl.BlockSpec((k, tn), lambda i, j: (0, j))
    assert m % tm == 0 and n % tn == 0, (name, m, tm, n, tn)
    tile = pl.BlockSpec((tm, tn), lambda i, j: (i, j))
    if shard == "rows":
        per = m // N_DEV // tm
        out_specs = [pl.BlockSpec((None, tm, tn), lambda i, j: (i // per, i % per, j))]
        out_shape = [jax.ShapeDtypeStruct((N_DEV, m // N_DEV, n), out_dtypes[0])]
    elif shard == "cols":
        per = n // N_DEV // tn
        out_specs = [pl.BlockSpec((None, tm, tn), lambda i, j: (j // per, i, j % per))]
        out_shape = [jax.ShapeDtypeStruct((N_DEV, m, n // N_DEV), out_dtypes[0])]
    else:
        out_specs = [tile for _ in out_dtypes]
        out_shape = [jax.ShapeDtypeStruct((m, n), dt) for dt in out_dtypes]
    n_ex, n_out = len(extras), len(out_shape)
    r_in, r_in_specs, r_out, r_out_specs, r_scr = _rider_specs(rider)
    n_rin, n_rout = len(r_in), len(r_out)
    grid = (m // tm, n // tn)

    def body(a_ref, b_ref, *refs):
        ex_refs = refs[:n_ex]
        o_refs = refs[n_ex + n_rin:n_ex + n_rin + n_out]
        r_refs = refs[n_ex:n_ex + n_rin] + refs[n_ex + n_rin + n_out:]
        _ride(rider, pl.program_id(0) * grid[1] + pl.program_id(1), grid[0] * grid[1], r_refs)
        acc = _dot(a_ref[...].astype(BF16), b_ref[...].astype(BF16), _DIMS[mode])
        vals = (acc,) if epi is None else epi(acc, *[r[...] for r in ex_refs])
        for o_ref, val in zip(o_refs, vals):
            o_ref[...] = val.astype(o_ref.dtype)

    sem = ("parallel", "parallel") if rider is None else ("arbitrary", "arbitrary")
    out = pl.pallas_call(
        body, grid=grid, in_specs=[a_spec, b_spec] + [tile] * n_ex + r_in_specs, out_specs=out_specs + r_out_specs,
        out_shape=out_shape + r_out, scratch_shapes=r_scr, compiler_params=_params(*sem), name=name)(a, b, *extras, *r_in)
    res = out[0] if n_out == 1 else out[:n_out]
    return res if rider is None else (res, out[n_out:])


def _norm_mm(x, g, b, tm, tn, name, out_dtypes=(F32,), epi=None, rider=None):
    m, k = x.shape
    n = b.shape[1]
    assert m % tm == 0 and n % tn == 0, (name, m, tm, n, tn)
    grid = (m // tm, n // tn)
    n_out = len(out_dtypes)
    r_in, r_in_specs, r_out, r_out_specs, r_scr = _rider_specs(rider)
    n_rin = len(r_in)

    def body(x_ref, g_ref, b_ref, *refs):
        a_ref = refs[n_rin]
        o_refs = refs[n_rin + 1:n_rin + 1 + n_out]
        _ride(rider, pl.program_id(0) * grid[1] + pl.program_id(1), grid[0] * grid[1], refs[:n_rin] + refs[n_rin + 1 + n_out:])

        @pl.when(pl.program_id(1) == 0)
        def _():
            xf = x_ref[...]
            a_ref[...] = (xf * _rstd(xf) * g_ref[...]).astype(BF16)

        acc = _dot(a_ref[...], b_ref[...].astype(BF16))
        vals = (acc,) if epi is None else epi(acc)
        for o_ref, val in zip(o_refs, vals):
            o_ref[...] = val.astype(o_ref.dtype)

    row = pl.BlockSpec((tm, k), lambda i, j: (i, 0))
    tile = pl.BlockSpec((tm, tn), lambda i, j: (i, j))
    out = pl.pallas_call(
        body, grid=grid,
        in_specs=[row, pl.BlockSpec((1, k), lambda i, j: (0, 0)), pl.BlockSpec((k, tn), lambda i, j: (0, j))] + r_in_specs,
        out_specs=[row] + [tile] * n_out + r_out_specs,
        out_shape=[jax.ShapeDtypeStruct((m, k), BF16)] + [jax.ShapeDtypeStruct((m, n), dt) for dt in out_dtypes] + r_out,
        scratch_shapes=r_scr, compiler_params=_params("arbitrary", "arbitrary"), name=name)(x, g, b, *r_in)
    return out[:1 + n_out], out[1 + n_out:]


def _mm_norm_res(a, b, g, res, tm, name, rider=None):
    m, k = a.shape
    n = b.shape[1]
    assert m % tm == 0, (name, m, tm)
    r_in, r_in_specs, r_out, r_out_specs, r_scr = _rider_specs(rider)
    n_rin = len(r_in)

    def body(a_ref, b_ref, g_ref, res_ref, *refs):
        z_ref, o_ref = refs[n_rin:n_rin + 2]
        _ride(rider, pl.program_id(0), m // tm, refs[:n_rin] + refs[n_rin + 2:])
        z = _dot(a_ref[...].astype(BF16), b_ref[...].astype(BF16))
        z_ref[...] = z
        o_ref[...] = res_ref[...] + z * _rstd(z) * g_ref[...]

    row = pl.BlockSpec((tm, n), lambda i: (i, 0))
    out = pl.pallas_call(
        body, grid=(m // tm,),
        in_specs=[pl.BlockSpec((tm, k), lambda i: (i, 0)), pl.BlockSpec((k, n), lambda i: (0, 0)),
                  pl.BlockSpec((1, n), lambda i: (0, 0)), row] + r_in_specs,
        out_specs=[row, row] + r_out_specs,
        out_shape=[jax.ShapeDtypeStruct((m, n), F32), jax.ShapeDtypeStruct((m, n), F32)] + r_out,
        scratch_shapes=r_scr, compiler_params=_params("arbitrary"), name=name)(a, b, g, res, *r_in)
    return out[0], out[1], out[2:]


def _rmsbwd_mm(x, g, dy, b, tm, tn, name, out_dtypes=(F32,), epi=None, extras=(), rider=None, group_sums=None):
    m, k = x.shape
    n = b.shape[0]
    assert m % tm == 0 and n % tn == 0, (name, m, tm, n, tn)
    grid = (m // tm, n // tn)
    n_ex, n_out = len(extras), len(out_dtypes)
    r_in, r_in_specs, r_out, r_out_specs, r_scr = _rider_specs(rider)
    n_rin = len(r_in)
    n_gs = 0 if group_sums is None else 1
    if n_gs:
        assert tn == n and epi is None
        extras = tuple(extras) + (group_sums[0],)
        n_ex += 1

    def body(x_ref, g_ref, dy_ref, b_ref, *refs):
        ex_refs = refs[:n_ex]
        dx_ref, dg_ref = refs[n_ex + n_rin:n_ex + n_rin + 2]
        o_refs = refs[n_ex + n_rin + 2:n_ex + n_rin + 2 + n_out]
        i, j = pl.program_id(0), pl.program_id(1)
        _ride(rider, i * grid[1] + j, grid[0] * grid[1], refs[n_ex:n_ex + n_rin] + refs[n_ex + n_rin + 2 + n_out + n_gs:])

        @pl.when((i == 0) & (j == 0))
        def _():
            dg_ref[...] = jnp.zeros_like(dg_ref)

        @pl.when(j == 0)
        def _():
            dx, dg = _rms_bwd_rows(x_ref[...], g_ref[...], dy_ref[...])
            dx_ref[...] = dx.astype(BF16)
            dg_ref[...] += jnp.sum(dg, axis=0, keepdims=True)

        acc = _dot(dx_ref[...], b_ref[...].astype(BF16), NT)
        if n_gs:
            groups = group_sums[1]
            col = lax.broadcasted_iota(jnp.int32, (groups, n), 1) // (n // groups)
            sel = jnp.where(col == lax.broadcasted_iota(jnp.int32, (groups, n), 0), 1.0, 0.0)
            refs[n_ex + n_rin + 2 + n_out][...] = _dot_select(sel, acc * ex_refs[-1][...], 3, NT)
            vals = (acc,)
        else:
            vals = (acc,) if epi is None else epi(acc, *[r[...] for r in ex_refs])
        for o_ref, val in zip(o_refs, vals):
            o_ref[...] = val.astype(o_ref.dtype)

    row = pl.BlockSpec((tm, k), lambda i, j: (i, 0))
    vec = pl.BlockSpec((1, k), lambda i, j: (0, 0))
    tile = pl.BlockSpec((tm, tn), lambda i, j: (i, j))
    gs_specs = [pl.BlockSpec((group_sums[1], tm), lambda i, j: (0, i))] if n_gs else []
    gs_shape = [jax.ShapeDtypeStruct((group_sums[1], m), F32)] if n_gs else []
    out = pl.pallas_call(
        body, grid=grid,
        in_specs=[row, vec, row, pl.BlockSpec((tn, k), lambda i, j: (j, 0))] + [tile] * n_ex + r_in_specs,
        out_specs=[row, vec] + [tile] * n_out + gs_specs + r_out_specs,
        out_shape=[jax.ShapeDtypeStruct((m, k), BF16), jax.ShapeDtypeStruct((1, k), F32)]
        + [jax.ShapeDtypeStruct((m, n), dt) for dt in out_dtypes] + gs_shape + r_out,
        scratch_shapes=r_scr, compiler_params=_params("arbitrary", "arbitrary"), name=name)(x, g, dy, b, *extras, *r_in)
    return out[:2 + n_out + n_gs], out[2 + n_out + n_gs:]


def _mm_rmsbwd_res(a, b, x, g, res, tm, name, rider=None):
    m, k = a.shape
    n = b.shape[0]
    assert m % tm == 0, (name, m, tm)
    r_in, r_in_specs, r_out, r_out_specs, r_scr = _rider_specs(rider)
    n_rin = len(r_in)

    def body(a_ref, b_ref, x_ref, g_ref, res_ref, *refs):
        o_ref, dg_ref = refs[n_rin:n_rin + 2]
        _ride(rider, pl.program_id(0), m // tm, refs[:n_rin] + refs[n_rin + 2:])

        @pl.when(pl.program_id(0) == 0)
        def _():
            dg_ref[...] = jnp.zeros_like(dg_ref)

        da = _dot(a_ref[...].astype(BF16), b_ref[...].astype(BF16), NT)
        dx, dg = _rms_bwd_rows(x_ref[...], g_ref[...], da)
        o_ref[...] = res_ref[...] + dx
        dg_ref[...] += jnp.sum(dg, axis=0, keepdims=True)

    row = pl.BlockSpec((tm, n), lambda i: (i, 0))
    vec = pl.BlockSpec((1, n), lambda i: (0, 0))
    out = pl.pallas_call(
        body, grid=(m // tm,),
        in_specs=[pl.BlockSpec((tm, k), lambda i: (i, 0)), pl.BlockSpec((n, k), lambda i: (0, 0)), row, vec, row] + r_in_specs,
        out_specs=[row, vec] + r_out_specs,
        out_shape=[jax.ShapeDtypeStruct((m, n), F32), jax.ShapeDtypeStruct((1, n), F32)] + r_out,
        scratch_shapes=r_scr, compiler_params=_params("arbitrary"), name=name)(a, b, x, g, res, *r_in)
    return out[0], out[1], out[2:]


def _rope_tables(pos, name):
    s = pos.shape[0]
    half = MLA_ROPE // 2
    inv_freq = jnp.asarray(np.power(np.float32(ROPE_BASE), -np.arange(0, MLA_ROPE, 2, dtype=np.float32) / MLA_ROPE)
                           .astype(np.float32).reshape(1, half))

    def body(p_ref, f_ref, c_ref, s_ref):
        ang = p_ref[...].astype(F32) * f_ref[...]
        c_ref[...] = jnp.cos(ang)
        s_ref[...] = jnp.sin(ang)

    return pl.pallas_call(
        body, out_shape=(jax.ShapeDtypeStruct((s, half), F32), jax.ShapeDtypeStruct((s, half), F32)), name=name)(pos, inv_freq)


def _lb_softmax(logits):
    m = jnp.max(logits, axis=0, keepdims=True)
    e = jnp.exp(logits - m)
    return e / jnp.sum(e, axis=0, keepdims=True)


def _lb_fwd(logits, name):
    def body(l_ref, o_ref):
        p = _lb_softmax(l_ref[...])
        acc = jnp.zeros_like(p[0:1])
        o_ref[0:1, :] = acc
        for layer in range(1, DEPTH):
            acc = acc + p[layer:layer + 1]
            o_ref[layer:layer + 1, :] = acc

    return pl.pallas_call(body, out_shape=jax.ShapeDtypeStruct(logits.shape, F32), name=name)(logits)


def _lb_bwd(logits, dlb, name):
    def body(l_ref, d_ref, o_ref):
        p = _lb_softmax(l_ref[...])
        d = d_ref[...]
        dp = [jnp.zeros_like(d[0:1])] * DEPTH
        run = jnp.zeros_like(d[0:1])
        for layer in range(DEPTH - 1, 0, -1):
            run = run + d[layer:layer + 1]
            dp[layer] = run
        inner = sum(p[layer:layer + 1] * dp[layer] for layer in range(DEPTH))
        for layer in range(DEPTH):
            o_ref[layer:layer + 1, :] = p[layer:layer + 1] * (dp[layer] - inner)

    return pl.pallas_call(body, out_shape=jax.ShapeDtypeStruct(logits.shape, F32), name=name)(logits, dlb)


def _loss(y, target, name):
    s, d = y.shape
    ts = _row_tile(s, 512)

    def body(y_ref, t_ref, l_ref, dy_ref):
        e = y_ref[...] - t_ref[...]
        dy_ref[...] = e / d

        @pl.when(pl.program_id(0) == 0)
        def _():
            l_ref[...] = jnp.zeros_like(l_ref)

        l_ref[...] += 0.5 * jnp.sum(jnp.mean(e * e, axis=-1, keepdims=True), axis=0, keepdims=True)

    row = pl.BlockSpec((ts, d), lambda i: (i, 0))
    return pl.pallas_call(
        body, grid=(s // ts,), in_specs=[row, row], out_specs=(pl.BlockSpec((1, 1), lambda i: (0, 0)), row),
        out_shape=(jax.ShapeDtypeStruct((1, 1), F32), jax.ShapeDtypeStruct((s, d), F32)),
        compiler_params=_params("arbitrary"), name=name)(y, target)


def _rope(t1, t2, cos, sin):
    return t1 * cos - t2 * sin, t1 * sin + t2 * cos


def _rope_bwd(d1, d2, cos, sin):
    return d1 * cos + d2 * sin, d2 * cos - d1 * sin


def _mla_qkv(proj, qn, kvn, w_uq, w_ukv, cos, sin, name):
    s = proj.shape[0]
    ts = _row_tile(s, 256)
    hh, half = MLA_HEADS, MLA_ROPE // 2

    def body(p_ref, qn_ref, kvn_ref, wq_ref, wkv_ref, c_ref, s_ref, cq_ref, ckv_ref, q_ref, k_ref, v_ref):
        p = p_ref[...]
        cq, ckv, kr = p[:, :MLA_Q_LORA], p[:, MLA_Q_LORA:MLA_Q_LORA + MLA_KV_LORA], p[:, MLA_Q_LORA + MLA_KV_LORA:]
        cqn = (cq * _rstd(cq) * qn_ref[...]).astype(BF16)
        ckvn = (ckv * _rstd(ckv) * kvn_ref[...]).astype(BF16)
        cq_ref[...] = cqn
        ckv_ref[...] = ckvn
        qe = _dot(cqn, wq_ref[...])
        kve = _dot(ckvn, wkv_ref[...])
        cos_, sin_ = c_ref[...], s_ref[...]
        k1, k2 = _rope(kr[:, :half], kr[:, half:], cos_, sin_)
        k1, k2 = k1.astype(BF16), k2.astype(BF16)
        for h in range(hh):
            b = h * MLA_QK
            q_ref[h, :, 0:MLA_NOPE] = qe[:, b:b + MLA_NOPE].astype(BF16)
            q1, q2 = _rope(qe[:, b + MLA_NOPE:b + MLA_NOPE + half], qe[:, b + MLA_NOPE + half:b + MLA_QK], cos_, sin_)
            q_ref[h, :, MLA_NOPE:MLA_NOPE + half] = q1.astype(BF16)
            q_ref[h, :, MLA_NOPE + half:MLA_QK] = q2.astype(BF16)
            b = h * (MLA_NOPE + MLA_V)
            k_ref[h, :, 0:MLA_NOPE] = kve[:, b:b + MLA_NOPE].astype(BF16)
            k_ref[h, :, MLA_NOPE:MLA_NOPE + half] = k1
            k_ref[h, :, MLA_NOPE + half:MLA_QK] = k2
            v_ref[h] = kve[:, b + MLA_NOPE:b + MLA_NOPE + MLA_V].astype(BF16)

    def row(w):
        return pl.BlockSpec((ts, w), lambda i: (i, 0))

    def full(shape):
        return pl.BlockSpec(shape, lambda i: (0,) * len(shape))

    def heads(w):
        return pl.BlockSpec((hh, ts, w), lambda i: (0, i, 0))

    return pl.pallas_call(
        body, grid=(s // ts,),
        in_specs=[row(MLA_IN), full(qn.shape), full(kvn.shape), full(w_uq.shape), full(w_ukv.shape), row(half), row(half)],
        out_specs=(row(MLA_Q_LORA), row(MLA_KV_LORA), heads(MLA_QK), heads(MLA_QK), heads(MLA_V)),
        out_shape=(jax.ShapeDtypeStruct((s, MLA_Q_LORA), BF16), jax.ShapeDtypeStruct((s, MLA_KV_LORA), BF16),
                   jax.ShapeDtypeStruct((hh, s, MLA_QK), BF16), jax.ShapeDtypeStruct((hh, s, MLA_QK), BF16),
                   jax.ShapeDtypeStruct((hh, s, MLA_V), BF16)),
        compiler_params=_params("parallel"), name=name)(proj, qn, kvn, w_uq, w_ukv, cos, sin)


ATTN_BLOCK = 2048
ATTN_FWD_TILE = (256, 1024)
ATTN_BWD_TILE = (512, 512)


def _attn_block(s):
    return _row_tile(s, ATTN_BLOCK)


def _tile_sees(diag, q0, tq, k0, tk):
    if not diag:
        return True, False
    return k0 <= q0 + tq - 1, k0 + tk - 1 > q0


def _causal_pairs(nb, kv_major):
    if kv_major:
        pairs = [(i, j) for j in range(nb) for i in range(j, nb)]
    else:
        pairs = [(i, j) for i in range(nb) for j in range(i + 1)]
    return (jnp.asarray(np.array([p[0] for p in pairs], np.int32)), jnp.asarray(np.array([p[1] for p in pairs], np.int32)))


def _ride(rider, step, total, refs):
    if rider is None:
        return

    @pl.when(step == 0)
    def _():
        rider.start(*refs)

    @pl.when(step == (total * 7) // 8)
    def _():
        rider.middle(*refs)

    @pl.when(step == total - 1)
    def _():
        rider.finish(*refs)


def _rider_specs(rider):
    if rider is None:
        return [], [], [], [], []
    return (list(rider.operands), [_HBM] * len(rider.operands), list(rider.out_shapes), [_HBM] * len(rider.out_shapes),
            list(rider.scratch))


def _attn_fwd(q, k, v, name, rider=None):
    hh, s, _ = q.shape
    blk = _attn_block(s)
    tq, tk = min(blk, ATTN_FWD_TILE[0]), min(blk, ATTN_FWD_TILE[1])
    nb = s // blk
    it, jt = _causal_pairs(nb, kv_major=False)
    npair = int(it.shape[0])
    scale = MLA_QK ** -0.5
    c2 = scale * LOG2_E
    r_in, r_in_specs, r_out, r_out_specs, r_scr = _rider_specs(rider)
    n_rin, n_rout, n_rscr = len(r_in), len(r_out), len(r_scr)

    def body(it_ref, jt_ref, q_ref, k_ref, v_ref, *refs):
        r_refs = refs[:n_rin] + refs[n_rin + 2:n_rin + 2 + n_rout] + refs[len(refs) - n_rscr:]
        o_ref, lse_ref = refs[n_rin:n_rin + 2]
        m_scr, acc_scr, v_scr = refs[n_rin + 2 + n_rout:n_rin + 2 + n_rout + 3]
        h, t = pl.program_id(0), pl.program_id(1)
        step = h * npair + t
        _ride(rider, step, hh * npair, r_refs)
        i, j = it_ref[t], jt_ref[t]

        @pl.when(j == 0)
        def _():
            m_scr[...] = jnp.full_like(m_scr, -jnp.inf)
            acc_scr[...] = jnp.zeros_like(acc_scr)
            v_scr[:, MLA_V:] = jnp.ones((blk, MLA_V), BF16)

        def block(diag):
            v_scr[:, :MLA_V] = v_ref[...]
            for k0 in range(0, blk, tk):
                kb, vb = k_ref[k0:k0 + tk, :], v_scr[k0:k0 + tk, :]
                for q0 in range(0, blk, tq):
                    visible, needs_mask = _tile_sees(diag, q0, tq, k0, tk)
                    if not visible:
                        continue
                    rows = slice(q0, q0 + tq)
                    sc = _dot(q_ref[rows, :], kb, NT)
                    if needs_mask:
                        qpos = q0 + lax.broadcasted_iota(jnp.int32, (tq, tk), 0)
                        kpos = k0 + lax.broadcasted_iota(jnp.int32, (tq, tk), 1)
                        sc = jnp.where(qpos >= kpos, sc, -jnp.inf)
                    m_prev = m_scr[rows, :]
                    m_new = jnp.maximum(m_prev, jnp.max(sc, axis=-1, keepdims=True))
                    alpha = jnp.exp2((m_prev - m_new) * c2)
                    p = jnp.exp2((sc - m_new) * c2)
                    acc_scr[rows, :] = alpha * acc_scr[rows, :] + _dot(p.astype(BF16), vb)
                    m_scr[rows, :] = m_new

        @pl.when(j < i)
        def _():
            block(False)

        @pl.when(j == i)
        def _():
            block(True)
            acc = acc_scr[...]
            l = acc[:, MLA_V:MLA_V + 1]
            o_ref[...] = acc[:, :MLA_V] / l
            lse_ref[...] = m_scr[...] * scale + jnp.log(l)

    grid_spec = pltpu.PrefetchScalarGridSpec(
        num_scalar_prefetch=2, grid=(hh, npair),
        in_specs=[pl.BlockSpec((None, blk, MLA_QK), lambda h, t, it_, jt_: (h, it_[t], 0)),
                  pl.BlockSpec((None, blk, MLA_QK), lambda h, t, it_, jt_: (h, jt_[t], 0)),
                  pl.BlockSpec((None, blk, MLA_V), lambda h, t, it_, jt_: (h, jt_[t], 0))] + r_in_specs,
        out_specs=[pl.BlockSpec((blk, MLA_V), lambda h, t, it_, jt_: (it_[t], h)),
                   pl.BlockSpec((None, blk, 1), lambda h, t, it_, jt_: (h, it_[t], 0))] + r_out_specs,
        scratch_shapes=[pltpu.VMEM((blk, 1), F32), pltpu.VMEM((blk, 2 * MLA_V), F32),
                        pltpu.VMEM((blk, 2 * MLA_V), BF16)] + r_scr)
    out = pl.pallas_call(
        body, grid_spec=grid_spec,
        out_shape=[jax.ShapeDtypeStruct((s, hh * MLA_V), F32), jax.ShapeDtypeStruct((hh, s, 1), F32)] + r_out,
        compiler_params=_params("arbitrary", "arbitrary"), name=name)(it, jt, q, k, v, *r_in)
    return out[0], out[1], out[2:]


def _attn_bwd(q, k, v, do, lse_row, delta_row, name, rider=None):
    hh, s, _ = q.shape
    blk = _attn_block(s)
    tq, tk = min(blk, ATTN_BWD_TILE[0]), min(blk, ATTN_BWD_TILE[1])
    nb = s // blk
    it, jt = _causal_pairs(nb, kv_major=True)
    npair = int(it.shape[0])
    scale = MLA_QK ** -0.5
    c2 = scale * LOG2_E
    r_in, r_in_specs, r_out, r_out_specs, r_scr = _rider_specs(rider)
    n_rin, n_rout, n_rscr = len(r_in), len(r_out), len(r_scr)

    def body(it_ref, jt_ref, q_ref, k_ref, v_ref, do_ref, lse_ref, dl_ref, *refs):
        r_refs = refs[:n_rin] + refs[n_rin + 3:n_rin + 3 + n_rout] + refs[len(refs) - n_rscr:]
        dq_out, dk_out, dv_out = refs[n_rin:n_rin + 3]
        dq_ref, dk_ref, dv_ref = refs[n_rin + 3 + n_rout:n_rin + 3 + n_rout + 3]
        h, t = pl.program_id(0), pl.program_id(1)
        step = h * npair + t
        _ride(rider, step, hh * npair, r_refs)
        i, j = it_ref[t], jt_ref[t]

        @pl.when(t == 0)
        def _():
            dq_ref[...] = jnp.zeros_like(dq_ref)

        def block(diag):
            if diag:
                dk_ref[...] = jnp.zeros_like(dk_ref)
                dv_ref[...] = jnp.zeros_like(dv_ref)
            for q0 in range(0, blk, tq):
                qb = q_ref[q0:q0 + tq, :]
                dob = do_ref[q0:q0 + tq, :].astype(BF16)
                lse2 = lse_ref[:, q0:q0 + tq] * LOG2_E
                dl = dl_ref[pl.ds(h, 1), q0:q0 + tq]
                dq = None
                for k0 in range(0, blk, tk):
                    visible, needs_mask = _tile_sees(diag, q0, tq, k0, tk)
                    if not visible:
                        continue
                    kb, vb = k_ref[k0:k0 + tk, :], v_ref[k0:k0 + tk, :]
                    pt = jnp.exp2(_dot(kb, qb, NT) * c2 - lse2)
                    if needs_mask:
                        kpos = k0 + lax.broadcasted_iota(jnp.int32, (tk, tq), 0)
                        qpos = q0 + lax.broadcasted_iota(jnp.int32, (tk, tq), 1)
                        pt = jnp.where(qpos >= kpos, pt, 0.0)
                    dv_ref[k0:k0 + tk, :] += _dot(pt.astype(BF16), dob)
                    dpt = _dot(vb, dob, NT)
                    dst = (pt * (dpt - dl) * scale).astype(BF16)
                    dk_ref[k0:k0 + tk, :] += _dot(dst, qb)
                    part = _dot(dst, kb, TN)
                    dq = part if dq is None else dq + part
                rows = pl.ds(pl.multiple_of(i * blk + q0, tq), tq)
                dq_ref[rows, :] += dq

        @pl.when(i == j)
        def _():
            block(True)

        @pl.when(i > j)
        def _():
            block(False)

        @pl.when(i == nb - 1)
        def _():
            dk_out[...] = dk_ref[...].astype(BF16)
            dv_out[...] = dv_ref[...].astype(BF16)

        @pl.when(t == npair - 1)
        def _():
            dq_out[...] = dq_ref[...].astype(BF16)

    grid_spec = pltpu.PrefetchScalarGridSpec(
        num_scalar_prefetch=2, grid=(hh, npair),
        in_specs=[pl.BlockSpec((None, blk, MLA_QK), lambda h, t, it_, jt_: (h, it_[t], 0)),
                  pl.BlockSpec((None, blk, MLA_QK), lambda h, t, it_, jt_: (h, jt_[t], 0)),
                  pl.BlockSpec((None, blk, MLA_V), lambda h, t, it_, jt_: (h, jt_[t], 0)),
                  pl.BlockSpec((blk, MLA_V), lambda h, t, it_, jt_: (it_[t], h)),
                  pl.BlockSpec((None, 1, blk), lambda h, t, it_, jt_: (h, 0, it_[t])),
                  pl.BlockSpec((hh, blk), lambda h, t, it_, jt_: (0, it_[t]))] + r_in_specs,
        out_specs=[pl.BlockSpec((None, s, MLA_QK), lambda h, t, it_, jt_: (h, 0, 0)),
                   pl.BlockSpec((None, blk, MLA_QK), lambda h, t, it_, jt_: (h, jt_[t], 0)),
                   pl.BlockSpec((None, blk, MLA_V), lambda h, t, it_, jt_: (h, jt_[t], 0))] + r_out_specs,
        scratch_shapes=[pltpu.VMEM((s, MLA_QK), F32), pltpu.VMEM((blk, MLA_QK), F32), pltpu.VMEM((blk, MLA_V), F32)] + r_scr)
    out = pl.pallas_call(
        body, grid_spec=grid_spec,
        out_shape=[jax.ShapeDtypeStruct((hh, s, MLA_QK), BF16), jax.ShapeDtypeStruct((hh, s, MLA_QK), BF16),
                   jax.ShapeDtypeStruct((hh, s, MLA_V), BF16)] + r_out,
        compiler_params=_params("arbitrary", "arbitrary"), name=name)(it, jt, q, k, v, do, lse_row, delta_row, *r_in)
    return out[0], out[1], out[2], out[3:]


def _mla_bwd_mid(dq, dk, dv, cos, sin, proj, qn, kvn, w_uq, w_ukv, name):
    s = proj.shape[0]
    ts = _row_tile(s, 256)
    hh, half = MLA_HEADS, MLA_ROPE // 2
    nq, nkv = hh * MLA_QK, hh * (MLA_NOPE + MLA_V)

    def body(dq_ref, dk_ref, dv_ref, c_ref, s_ref, p_ref, qn_ref, kvn_ref, wq_ref, wkv_ref,
             dqe_ref, dkve_ref, dp_ref, dqn_ref, dkvn_ref):
        cos_, sin_ = c_ref[...], s_ref[...]
        dkr1 = jnp.zeros((ts, half), F32)
        dkr2 = jnp.zeros((ts, half), F32)
        for h in range(hh):
            dqh, dkh = dq_ref[h], dk_ref[h]
            b = h * MLA_QK
            dqe_ref[:, b:b + MLA_NOPE] = dqh[:, :MLA_NOPE].astype(BF16)
            d1, d2 = _rope_bwd(dqh[:, MLA_NOPE:MLA_NOPE + half], dqh[:, MLA_NOPE + half:], cos_, sin_)
            dqe_ref[:, b + MLA_NOPE:b + MLA_NOPE + half] = d1.astype(BF16)
            dqe_ref[:, b + MLA_NOPE + half:b + MLA_QK] = d2.astype(BF16)
            b = h * (MLA_NOPE + MLA_V)
            dkve_ref[:, b:b + MLA_NOPE] = dkh[:, :MLA_NOPE].astype(BF16)
            dkve_ref[:, b + MLA_NOPE:b + MLA_NOPE + MLA_V] = dv_ref[h].astype(BF16)
            dkr1 = dkr1 + dkh[:, MLA_NOPE:MLA_NOPE + half]
            dkr2 = dkr2 + dkh[:, MLA_NOPE + half:]
        dkr1, dkr2 = _rope_bwd(dkr1, dkr2, cos_, sin_)
        dcqn = _dot(dqe_ref[...], wq_ref[...], NT)
        dckvn = _dot(dkve_ref[...], wkv_ref[...], NT)
        p = p_ref[...]
        dcq, dqn = _rms_bwd_rows(p[:, :MLA_Q_LORA], qn_ref[...], dcqn)
        dckv, dkvn = _rms_bwd_rows(p[:, MLA_Q_LORA:MLA_Q_LORA + MLA_KV_LORA], kvn_ref[...], dckvn)
        dp_ref[:, :MLA_Q_LORA] = dcq.astype(BF16)
        dp_ref[:, MLA_Q_LORA:MLA_Q_LORA + MLA_KV_LORA] = dckv.astype(BF16)
        dp_ref[:, MLA_Q_LORA + MLA_KV_LORA:MLA_Q_LORA + MLA_KV_LORA + half] = dkr1.astype(BF16)
        dp_ref[:, MLA_Q_LORA + MLA_KV_LORA + half:] = dkr2.astype(BF16)

        @pl.when(pl.program_id(0) == 0)
        def _():
            dqn_ref[...] = jnp.zeros_like(dqn_ref)
            dkvn_ref[...] = jnp.zeros_like(dkvn_ref)

        dqn_ref[...] += jnp.sum(dqn, axis=0, keepdims=True)
        dkvn_ref[...] += jnp.sum(dkvn, axis=0, keepdims=True)

    def row(w):
        return pl.BlockSpec((ts, w), lambda i: (i, 0))

    def full(shape):
        return pl.BlockSpec(shape, lambda i: (0,) * len(shape))

    def heads(w):
        return pl.BlockSpec((hh, ts, w), lambda i: (0, i, 0))

    return pl.pallas_call(
        body, grid=(s // ts,),
        in_specs=[heads(MLA_QK), heads(MLA_QK), heads(MLA_V), row(half), row(half), row(MLA_IN),
                  full(qn.shape), full(kvn.shape), full(w_uq.shape), full(w_ukv.shape)],
        out_specs=(row(nq), row(nkv), row(MLA_IN), full(qn.shape), full(kvn.shape)),
        out_shape=(jax.ShapeDtypeStruct((s, nq), BF16), jax.ShapeDtypeStruct((s, nkv), BF16),
                   jax.ShapeDtypeStruct((s, MLA_IN), BF16), jax.ShapeDtypeStruct(qn.shape, F32),
                   jax.ShapeDtypeStruct(kvn.shape, F32)),
        compiler_params=_params("arbitrary"), name=name)(dq, dk, dv, cos, sin, proj, qn, kvn, w_uq, w_ukv)


HGRN_TILE = 256


def _chunk_masks(t):
    r = lax.broadcasted_iota(jnp.int32, (t, t), 0)
    c = lax.broadcasted_iota(jnp.int32, (t, t), 1)
    same = (r // HGRN_CHUNK) == (c // HGRN_CHUNK)
    return r, c, same


def _hgrn_gates(p, lb):
    hk = HGRN_HEADS * HGRN_D
    qx, fx, ix, gx = p[:, :hk], p[:, hk:2 * hk], p[:, 2 * hk:3 * hk], p[:, 3 * hk:]
    sig_f = jax.nn.sigmoid(fx)
    f = lb + (1.0 - lb) * sig_f
    sig_q = jax.nn.sigmoid(qx)
    t = p.shape[0]
    r, c, same = _chunk_masks(t)
    lower = jnp.where(same & (c <= r), 1.0, 0.0).astype(F32)
    b = _dot_select(lower, jnp.log(f), 3)
    b3 = b.reshape(t // HGRN_CHUNK, HGRN_CHUNK, hk)
    bref = jnp.broadcast_to(b3[:, HGRN_CHUNK // 2:HGRN_CHUNK // 2 + 1, :], b3.shape).reshape(t, hk)
    blast = jnp.broadcast_to(b3[:, HGRN_CHUNK - 1:, :], b3.shape).reshape(t, hk)
    return qx, ix, gx, sig_f, f, sig_q, b, bref, blast


def _hgrn_fwd(proj, lb, onorm, name, rider=None):
    s = proj.shape[0]
    t = _row_tile(s, HGRN_TILE)
    nc = t // HGRN_CHUNK
    hh, dd, hk = HGRN_HEADS, HGRN_D, HGRN_HEADS * HGRN_D
    r_in, r_in_specs, r_out, r_out_specs, r_scr = _rider_specs(rider)
    n_rin, n_rout = len(r_in), len(r_out)

    def body(p_ref, lb_ref, on_ref, *refs):
        y_ref, o_ref, st_ref = refs[n_rin:n_rin + 3]
        st_scr = refs[n_rin + 3 + n_rout]
        _ride(rider, pl.program_id(0), s // t, refs[:n_rin] + refs[n_rin + 3:n_rin + 3 + n_rout] + refs[n_rin + 4 + n_rout:])

        @pl.when(pl.program_id(0) == 0)
        def _():
            st_scr[...] = jnp.zeros_like(st_scr)

        qx, ix, gx, _, f, sig_q, b, bref, blast = _hgrn_gates(p_ref[...], lb_ref[...])
        q = qx * sig_q
        k = 1.0 - f
        r, c, same = _chunk_masks(t)
        causal = same & (c <= r)
        for h in range(hh):
            sl = slice(h * dd, (h + 1) * dd)
            bh, brefh, blasth, qh, kh = b[:, sl], bref[:, sl], blast[:, sl], q[:, sl], k[:, sl]
            vh = ix[:, sl].astype(BF16)
            q_rel = (qh * jnp.exp(bh - brefh)).astype(BF16)
            k_rel = (kh * jnp.exp(brefh - bh)).astype(BF16)
            a = jnp.where(causal, _dot(q_rel, k_rel, NT), 0.0)
            o_intra = _dot(a.astype(BF16), vh)
            q_dec = (qh * jnp.exp(bh)).astype(BF16)
            k_dec = (kh * jnp.exp(blasth - bh)).astype(BF16)
            dec = jnp.exp(blasth)
            pieces = []
            for ci in range(nc):
                rows = slice(ci * HGRN_CHUNK, (ci + 1) * HGRN_CHUNK)
                st = st_scr[h]
                if ci == 0:
                    st_ref[h] = st
                pieces.append(_dot(q_dec[rows], st.astype(BF16), NT))
                st_scr[h] = st * dec[ci * HGRN_CHUNK:ci * HGRN_CHUNK + 1, :] + _dot(vh[rows], k_dec[rows], TN)
            oh = o_intra + jnp.concatenate(pieces, axis=0)
            o_ref[:, sl] = oh
            gate = gx[:, sl] * jax.nn.sigmoid(gx[:, sl])
            y_ref[:, sl] = (oh * _rstd(oh) * on_ref[...] * gate).astype(BF16)

    out = pl.pallas_call(
        body, grid=(s // t,),
        in_specs=[pl.BlockSpec((t, 4 * hk), lambda i: (i, 0)), pl.BlockSpec((1, hk), lambda i: (0, 0)),
                  pl.BlockSpec((1, dd), lambda i: (0, 0))] + r_in_specs,
        out_specs=[pl.BlockSpec((t, hk), lambda i: (i, 0)), pl.BlockSpec((t, hk), lambda i: (i, 0)),
                   pl.BlockSpec((None, hh, dd, dd), lambda i: (i, 0, 0, 0))] + r_out_specs,
        out_shape=[jax.ShapeDtypeStruct((s, hk), BF16), jax.ShapeDtypeStruct((s, hk), F32),
                   jax.ShapeDtypeStruct((s // t, hh, dd, dd), F32)] + r_out,
        scratch_shapes=[pltpu.VMEM((hh, dd, dd), F32)] + r_scr,
        compiler_params=_params("arbitrary"), name=name)(proj, lb, onorm, *r_in)
    return out[0], out[1], out[2], out[3:]


def _hgrn_bwd(proj, lb, onorm, o, states, dy, name, rider=None):
    s = proj.shape[0]
    t = _row_tile(s, HGRN_TILE)
    nt = s // t
    nc = t // HGRN_CHUNK
    hh, dd, hk = HGRN_HEADS, HGRN_D, HGRN_HEADS * HGRN_D
    r_in, r_in_specs, r_out, r_out_specs, r_scr = _rider_specs(rider)
    n_rin, n_rout, n_rscr = len(r_in), len(r_out), len(r_scr)

    def body(p_ref, lb_ref, on_ref, o_ref, st_ref, dy_ref, *refs):
        r_refs = refs[:n_rin] + refs[n_rin + 3:n_rin + 3 + n_rout] + refs[len(refs) - n_rscr:]
        dp_ref, dlb_ref, don_ref = refs[n_rin:n_rin + 3]
        dst_scr, cat_scr, ext_scr, dk_scr, dq_scr = refs[n_rin + 3 + n_rout:n_rin + 3 + n_rout + 5]
        _ride(rider, pl.program_id(0), nt, r_refs)

        @pl.when(pl.program_id(0) == 0)
        def _():
            dst_scr[...] = jnp.zeros_like(dst_scr)
            dlb_ref[...] = jnp.zeros_like(dlb_ref)
            don_ref[...] = jnp.zeros_like(don_ref)

        lbv = lb_ref[...]
        qx, ix, gx, sig_f, f, sig_q, b, bref, blast = _hgrn_gates(p_ref[...], lbv)
        q = qx * sig_q
        k = 1.0 - f
        r, c, same = _chunk_masks(t)
        causal = same & (c <= r)
        on = on_ref[...]
        don = jnp.zeros((1, dd), F32)
        for h in range(hh):
            sl = slice(h * dd, (h + 1) * dd)
            oh = o_ref[:, sl]
            dyh = dy_ref[:, sl]
            gxh = gx[:, sl]
            sig_g = jax.nn.sigmoid(gxh)
            rs = _rstd(oh)
            dgate = dyh * (oh * rs * on)
            dp_ref[:, 3 * hk + h * dd:3 * hk + (h + 1) * dd] = (dgate * (sig_g * (1.0 + gxh * (1.0 - sig_g)))).astype(BF16)
            do, donh = _rms_bwd_rows(oh, on, dyh * (gxh * sig_g))
            don = don + jnp.sum(donh, axis=0, keepdims=True)
            dob = do.astype(BF16)
            bh, brefh, blasth, qh, kh = b[:, sl], bref[:, sl], blast[:, sl], q[:, sl], k[:, sl]
            vh = ix[:, sl].astype(BF16)
            e_qr, e_kr, e_qd, e_kd = jnp.exp(bh - brefh), jnp.exp(brefh - bh), jnp.exp(bh), jnp.exp(blasth - bh)
            dec = jnp.exp(blasth)
            q_rel, k_rel, q_dec, k_dec = qh * e_qr, kh * e_kr, qh * e_qd, kh * e_kd
            q_relb, k_relb, q_decb, k_decb = q_rel.astype(BF16), k_rel.astype(BF16), q_dec.astype(BF16), k_dec.astype(BF16)
            a = jnp.where(causal, _dot(q_relb, k_relb, NT), 0.0).astype(BF16)
            dv = _dot(a, dob, TN)
            da = jnp.where(causal, _dot(dob, vh, NT), 0.0).astype(BF16)
            dq_rel = _dot(da, k_relb)
            dk_rel = _dot(da, q_relb, TN)
            sts = [st_ref[h]]
            for ci in range(nc - 1):
                rows = slice(ci * HGRN_CHUNK, (ci + 1) * HGRN_CHUNK)
                sts.append(sts[-1] * dec[ci * HGRN_CHUNK:ci * HGRN_CHUNK + 1, :] + _dot(vh[rows], k_decb[rows], TN))
            dq_dec, dk_dec, dv_inter, ddec = [None] * nc, [None] * nc, [None] * nc, [None] * nc
            for ci in range(nc - 1, -1, -1):
                rows = slice(ci * HGRN_CHUNK, (ci + 1) * HGRN_CHUNK)
                st = sts[ci]
                dst = dst_scr[h]
                dstb = dst.astype(BF16)
                dq_dec[ci] = _dot(dob[rows], st.astype(BF16))
                dk_dec[ci] = _dot(vh[rows], dstb)
                dv_inter[ci] = _dot(k_decb[rows], dstb, NT)
                ddec[ci] = jnp.broadcast_to(jnp.sum(dst * st, axis=0, keepdims=True), (HGRN_CHUNK, dd))
                dst_scr[h] = dst * dec[ci * HGRN_CHUNK:ci * HGRN_CHUNK + 1, :] + _dot(dob[rows], q_decb[rows], TN)
            dq_dec = jnp.concatenate(dq_dec, axis=0)
            dk_dec = jnp.concatenate(dk_dec, axis=0)
            dv = dv + jnp.concatenate(dv_inter, axis=0)
            ddec = jnp.concatenate(ddec, axis=0)
            dp_ref[:, 2 * hk + h * dd:2 * hk + (h + 1) * dd] = dv.astype(BF16)
            dq_scr[:, sl] = dq_rel * e_qr + dq_dec * e_qd
            dk_scr[:, sl] = dk_rel * e_kr + dk_dec * e_kd
            g_qr, g_kr, g_qd, g_kd = dq_rel * q_rel, dk_rel * k_rel, dq_dec * q_dec, dk_dec * k_dec
            cat_scr[0:t, sl] = g_qr - g_kr + g_qd - g_kd
            cat_scr[t:2 * t, sl] = g_kr - g_qr
            cat_scr[2 * t:3 * t, sl] = g_kd
            ext_scr[:, sl] = ddec * dec
        upper = jnp.where(same & (c >= r), 1.0, 0.0).astype(F32)
        to_ref = jnp.where(same & (r % HGRN_CHUNK <= HGRN_CHUNK // 2), 1.0, 0.0).astype(F32)
        to_all = jnp.where(same, 1.0, 0.0).astype(F32)
        dlogf = _dot_select(jnp.concatenate([upper, to_ref, to_all], axis=1), cat_scr[...], 2) + ext_scr[...]
        df = dlogf / f - dk_scr[...]
        dp_ref[:, hk:2 * hk] = (df * (1.0 - lbv) * sig_f * (1.0 - sig_f)).astype(BF16)
        dp_ref[:, 0:hk] = (dq_scr[...] * (sig_q * (1.0 + qx * (1.0 - sig_q)))).astype(BF16)
        dlb_ref[...] += jnp.sum(df * (1.0 - sig_f), axis=0, keepdims=True)
        don_ref[...] += don

    def rev(i):
        return nt - 1 - i

    out = pl.pallas_call(
        body, grid=(nt,),
        in_specs=[pl.BlockSpec((t, 4 * hk), lambda i: (rev(i), 0)), pl.BlockSpec((1, hk), lambda i: (0, 0)),
                  pl.BlockSpec((1, dd), lambda i: (0, 0)), pl.BlockSpec((t, hk), lambda i: (rev(i), 0)),
                  pl.BlockSpec((None, hh, dd, dd), lambda i: (rev(i), 0, 0, 0)),
                  pl.BlockSpec((t, hk), lambda i: (rev(i), 0))] + r_in_specs,
        out_specs=[pl.BlockSpec((t, 4 * hk), lambda i: (rev(i), 0)), pl.BlockSpec((1, hk), lambda i: (0, 0)),
                   pl.BlockSpec((1, dd), lambda i: (0, 0))] + r_out_specs,
        out_shape=[jax.ShapeDtypeStruct((s, 4 * hk), BF16), jax.ShapeDtypeStruct((1, hk), F32),
                   jax.ShapeDtypeStruct((1, dd), F32)] + r_out,
        scratch_shapes=[pltpu.VMEM((hh, dd, dd), F32), pltpu.VMEM((3 * t, hk), F32), pltpu.VMEM((t, hk), F32),
                        pltpu.VMEM((t, hk), F32), pltpu.VMEM((t, hk), F32)] + r_scr,
        compiler_params=_params("arbitrary"), name=name)(proj, lb, onorm, o, states, dy, *r_in)
    return out[0], out[1], out[2], out[3:]


def _adamw_update(w, g, m, v):
    nm = ADAM_B1 * m + (1.0 - ADAM_B1) * g
    nv = ADAM_B2 * v + (1.0 - ADAM_B2) * (g * g)
    m_hat = nm / (1.0 - ADAM_B1 ** ADAM_STEP)
    v_hat = nv / (1.0 - ADAM_B2 ** ADAM_STEP)
    return -ADAM_LR * (m_hat / (jnp.sqrt(v_hat) + ADAM_EPS) + ADAM_WD * w), nm, nv


def _adamw(w, g, m, v, name):
    rows, cols = w.shape
    tr = _divisor_tile(rows, 256, 8)

    def body(w_ref, g_ref, m_ref, v_ref, d_ref, nm_ref, nv_ref):
        d_ref[...], nm_ref[...], nv_ref[...] = _adamw_update(w_ref[...], g_ref[...], m_ref[...], v_ref[...])

    blk = pl.BlockSpec((tr, cols), lambda i: (i, 0))
    shp = jax.ShapeDtypeStruct((rows, cols), F32)
    return pl.pallas_call(
        body, grid=(rows // tr,), in_specs=[blk] * 4, out_specs=(blk,) * 3, out_shape=(shp,) * 3,
        compiler_params=_params("parallel"), name=name)(w, g, m, v)


ADAMW_BLOCK_ELEMS = 128 * 1024


def _adamw_layers(w, lands, m, v, name, rider=None):
    ll, rows, cols = w.shape
    tr = _divisor_tile(rows, max(16, ADAMW_BLOCK_ELEMS // cols), 16)
    r_in, r_in_specs, r_out, r_out_specs, r_scr = _rider_specs(rider)
    n_rin = len(r_in)

    def body(w_ref, m_ref, v_ref, *refs):
        land_refs = refs[:ll]
        g_out, d_ref, nm_ref, nv_ref = refs[ll + n_rin:ll + n_rin + 4]
        layer = pl.program_id(0)
        _ride(rider, layer * (rows // tr) + pl.program_id(1), ll * (rows // tr), refs[ll:ll + n_rin] + refs[ll + n_rin + 4:])
        for k in range(ll):
            @pl.when(layer == k)
            def _(k=k):
                g = land_refs[k][0].astype(F32)
                for slot in range(1, N_DEV):
                    g = g + land_refs[k][slot].astype(F32)
                g_out[...] = g

        d_ref[...], nm_ref[...], nv_ref[...] = _adamw_update(w_ref[...], g_out[...], m_ref[...], v_ref[...])

    stacked = pl.BlockSpec((None, tr, cols), lambda l, i: (l, i, 0))

    def one(k):
        return pl.BlockSpec((N_DEV, tr, cols), lambda l, i: (0, jnp.where(l == k, i, 0), 0))

    shp = jax.ShapeDtypeStruct(w.shape, F32)
    out = pl.pallas_call(
        body, grid=(ll, rows // tr), in_specs=[stacked] * 3 + [one(k) for k in range(ll)] + r_in_specs,
        out_specs=[stacked] * 4 + r_out_specs, out_shape=[shp] * 4 + r_out, scratch_shapes=r_scr,
        compiler_params=_params("arbitrary", "arbitrary"), name=name)(w, m, v, *lands, *r_in)
    return out[0], out[1], out[2], out[3], out[4:]


_HBM = pl.BlockSpec(memory_space=pltpu.HBM)
_MESH = pl.DeviceIdType.MESH


class _GatherRide:
    def __init__(self, blocks, cuts):
        self.operands = list(blocks)
        self.cuts = list(cuts)
        self.out_shapes = []
        for b, cut in zip(blocks, cuts):
            r, c = b.shape
            shape = {"rows": (N_DEV * r, c), "cols": (r, N_DEV * c), "slots": (N_DEV, r, c)}[cut]
            self.out_shapes.append(jax.ShapeDtypeStruct(shape, b.dtype))
        n = len(blocks)
        self.scratch = [pltpu.SemaphoreType.DMA((7 * n,)), pltpu.SemaphoreType.DMA((7 * n,)), pltpu.SemaphoreType.DMA((n,))]

    def _parts(self, *refs):
        n = len(self.operands)
        x_refs, out_refs = refs[:n], refs[n:2 * n]
        send_sems, recv_sems, local_sems = refs[2 * n:]
        x, y, c = lax.axis_index("x"), lax.axis_index("y"), lax.axis_index("c")
        me, sibling = (x, y, c), (x, y, 1 - c)
        chips = [(1 - x, y), (x, 1 - y), (1 - x, 1 - y)]
        mine, first, passed, landed, from_sibling = [], [], [], [], []
        for e in range(n):
            x_ref, out_ref, cut = x_refs[e], out_refs[e], self.cuts[e]
            r, cc = x_ref.shape

            def place(px, py, pc, out_ref=out_ref, cut=cut, r=r, cc=cc):
                p = 4 * px + 2 * py + pc
                if cut == "rows":
                    return out_ref.at[pl.ds(pl.multiple_of(p * r, r), r), :]
                if cut == "cols":
                    return out_ref.at[:, pl.ds(pl.multiple_of(p * cc, cc), cc)]
                return out_ref.at[p]

            def copy(k, block, to, src=None, place=place, e=e):
                return pltpu.make_async_remote_copy(
                    src_ref=place(*block) if src is None else src, dst_ref=place(*block), send_sem=send_sems.at[7 * e + k],
                    recv_sem=recv_sems.at[7 * e + k], device_id=to, device_id_type=_MESH)

            mine.append(pltpu.make_async_copy(x_ref, place(*me), local_sems.at[e]))
            first += [copy(0, me, sibling, src=x_ref)] + [copy(1 + j, me, (*chip, c), src=x_ref) for j, chip in enumerate(chips)]
            passed += [copy(4 + j, (*chip, c), sibling) for j, chip in enumerate(chips)]
            landed += [copy(1 + j, (*chip, c), me) for j, chip in enumerate(chips)]
            from_sibling += [copy(0, sibling, me)] + [copy(4 + j, (*chip, 1 - c), me) for j, chip in enumerate(chips)]
        return mine, first, passed, landed, from_sibling

    def start(self, *refs):
        mine, first, _, _, _ = self._parts(*refs)
        for cp in mine + first:
            cp.start()

    def middle(self, *refs):
        _, _, passed, landed, _ = self._parts(*refs)
        for got, fwd in zip(landed, passed):
            got.wait_recv()
            fwd.start()

    def finish(self, *refs):
        mine, first, passed, _, from_sibling = self._parts(*refs)
        for cp in from_sibling:
            cp.wait_recv()
        for cp in first + passed:
            cp.wait_send()
        for cp in mine:
            cp.wait()


class _ExchangeRide:
    def __init__(self, sends):
        self.operands = list(sends)
        self.out_shapes = [jax.ShapeDtypeStruct(s.shape, s.dtype) for s in sends]
        n = len(sends)
        self.scratch = [pltpu.SemaphoreType.DMA((7 * n,)), pltpu.SemaphoreType.DMA((7 * n,)), pltpu.SemaphoreType.DMA((n,))]

    def _parts(self, *refs):
        n = len(self.operands)
        s_refs, land_refs = refs[:n], refs[n:2 * n]
        send_sems, recv_sems, local_sems = refs[2 * n:]
        x, y, c = lax.axis_index("x"), lax.axis_index("y"), lax.axis_index("c")
        me = 4 * x + 2 * y + c
        own, sends, recvs = [], [], []
        for e in range(n):
            s_ref, land_ref = s_refs[e], land_refs[e]
            own.append(pltpu.make_async_copy(s_ref.at[me], land_ref.at[me], local_sems.at[e]))
            for rel in range(1, N_DEV):
                px = 1 - x if rel & 4 else x
                py = 1 - y if rel & 2 else y
                pc = 1 - c if rel & 1 else c
                peer = 4 * px + 2 * py + pc
                k = 7 * e + rel - 1
                sends.append(pltpu.make_async_remote_copy(
                    src_ref=s_ref.at[peer], dst_ref=land_ref.at[me], send_sem=send_sems.at[k], recv_sem=recv_sems.at[k],
                    device_id=(px, py, pc), device_id_type=_MESH))
                recvs.append(pltpu.make_async_remote_copy(
                    src_ref=s_ref.at[me], dst_ref=land_ref.at[peer], send_sem=send_sems.at[k], recv_sem=recv_sems.at[k],
                    device_id=(px, py, pc), device_id_type=_MESH))
        return own, sends, recvs

    def start(self, *refs):
        own, sends, _ = self._parts(*refs)
        for cp in own + sends:
            cp.start()

    def middle(self, *refs):
        pass

    def finish(self, *refs):
        own, sends, recvs = self._parts(*refs)
        for cp in recvs:
            cp.wait_recv()
        for cp in sends:
            cp.wait_send()
        for cp in own:
            cp.wait()


def _run_alone(rider, name):
    def body(*refs):
        rider.start(*refs)
        rider.middle(*refs)
        rider.finish(*refs)

    return pl.pallas_call(
        body, out_shape=rider.out_shapes, in_specs=[_HBM] * len(rider.operands), out_specs=[_HBM] * len(rider.out_shapes),
        scratch_shapes=rider.scratch, name=name)(*rider.operands)


def _all_gather(xs, name):
    return _run_alone(_GatherRide([xs], ["slots"]), name)[0]


def _sum_slots(parts, name):
    _, rows, cols = parts.shape
    tr = _divisor_tile(rows, 256, 16)

    def body(p_ref, o_ref):
        acc = p_ref[0].astype(F32)
        for slot in range(1, N_DEV):
            acc = acc + p_ref[slot].astype(F32)
        o_ref[...] = acc

    return pl.pallas_call(
        body, grid=(rows // tr,), in_specs=[pl.BlockSpec((N_DEV, tr, cols), lambda i: (0, i, 0))],
        out_specs=pl.BlockSpec((tr, cols), lambda i: (i, 0)), out_shape=jax.ShapeDtypeStruct((rows, cols), F32),
        compiler_params=_params("parallel"), name=name)(parts)


def _carry(rode, key, riders, call):
    rider = riders.get(key)
    res = call(rider)
    if rider is None:
        return res
    res, rode[key] = res
    return res


def _kept(rode, key, riders, brought):
    if key in riders:
        rode[key] = brought


def _mlp_fwd(h, g_pre, g_post, w1, w2, tag, riders):
    rode = {}
    tm = _row_tile(h.shape[0], 2048)
    (a, r2), brought = _norm_mm(h, g_pre, w1, tm, 1024, f"{tag}_up", out_dtypes=(BF16,),
                                epi=lambda acc: (jnp.square(jnp.maximum(acc, 0.0)),), rider=riders.get("up"))
    _kept(rode, "up", riders, brought)
    z, out, brought = _mm_norm_res(r2, w2, g_post, h, 512, f"{tag}_down", riders.get("down"))
    _kept(rode, "down", riders, brought)
    return out, (h, a, r2, z), rode


def _mlp_bwd(dh, saved, g_pre, g_post, w1, w2, tag, riders):
    h, a, r2, z = saved
    rode = {}
    tm = _row_tile(h.shape[0], 1024)
    (dz, dg_post, du), brought = _rmsbwd_mm(
        z, g_post, dh, w2, tm, 1024, f"{tag}_ddown", out_dtypes=(BF16,), extras=(r2,),
        epi=lambda acc, rr: (acc * (2.0 * jnp.sqrt(rr.astype(F32))),), rider=riders.get("ddown"))
    _kept(rode, "ddown", riders, brought)
    dw2 = _carry(rode, "dw2", riders, lambda r: _mm(
        r2, dz, "tn", 512, 1024, f"{tag}_dw2", out_dtypes=(BF16,), shard="rows", rider=r))
    dw1 = _mm(a, du, "tn", 1024, 512, f"{tag}_dw1", out_dtypes=(BF16,), shard="cols")
    dh_in, dg_pre, _ = _mm_rmsbwd_res(du, w1, h, g_pre, dh, 512, f"{tag}_dup")
    return dh_in, dg_pre, dg_post, dw1, dw2, rode


def _hgrn_layer_fwd(h, g_pre, g_post, lb, onorm, w_in, w_o, tag, riders):
    rode = {}
    (a, proj), brought = _norm_mm(h, g_pre, w_in, _row_tile(h.shape[0], 2048), 1024, f"{tag}_in", rider=riders.get("in"))
    _kept(rode, "in", riders, brought)
    y, o, states, brought = _hgrn_fwd(proj, lb, onorm, f"{tag}_scan", riders.get("scan"))
    _kept(rode, "scan", riders, brought)
    m, out, _ = _mm_norm_res(y, w_o, g_post, h, 512, f"{tag}_o")
    return out, (h, a, proj, y, o, states, m), rode


def _hgrn_layer_bwd(dh, saved, g_pre, g_post, lb, onorm, w_in, w_o, tag, rider=None):
    h, a, proj, y, o, states, m = saved
    (dm, dg_post, dy), _ = _rmsbwd_mm(m, g_post, dh, w_o, 512, 1024, f"{tag}_do")
    dw_o = _mm(y, dm, "tn", 128, 1024, f"{tag}_dwo", out_dtypes=(BF16,), shard="rows")
    dproj, dlb, donorm, rode = _hgrn_bwd(proj, lb, onorm, o, states, dy, f"{tag}_dscan", rider)
    dw_in = _mm(a, dproj, "tn", 1024, 512, f"{tag}_dwin", out_dtypes=(BF16,), shard="cols")
    dh_in, dg_pre, _ = _mm_rmsbwd_res(dproj, w_in, h, g_pre, dh, 512, f"{tag}_din")
    return dh_in, dg_pre, dg_post, dlb, donorm, dw_in, dw_o, rode


def _mla_layer_fwd(h, g_pre, g_post, cos, sin, w_in, late, qn, kvn, tag, riders):
    rode = {}
    (a, proj), brought = _norm_mm(h, g_pre, w_in, 512, MLA_IN, f"{tag}_in", rider=riders.get("in"))
    _kept(rode, "in", riders, brought)
    w_uq, w_ukv, _ = late(rode)
    cqn, ckvn, q, k, v = _mla_qkv(proj, qn, kvn, w_uq, w_ukv, cos, sin, f"{tag}_qkv")
    o, lse, brought = _attn_fwd(q, k, v, f"{tag}_attn", riders.get("attn"))
    _kept(rode, "attn", riders, brought)
    _, _, w_o = late(rode)
    m, out, _ = _mm_norm_res(o, w_o, g_post, h, 512, f"{tag}_o")
    return out, (h, a, proj, cqn, ckvn, q, k, v, o, lse, m), rode


def _mla_layer_bwd(dh, saved, g_pre, g_post, cos, sin, w_in, qn, kvn, w_uq, w_ukv, w_o, tag, rider=None, own_ride=False):
    h, a, proj, cqn, ckvn, q, k, v, o, lse, m = saved
    hh, s = q.shape[0], q.shape[1]
    own = {}
    (dm, dg_post, do, delta), _ = _rmsbwd_mm(m, g_post, dh, w_o, 512, 1024, f"{tag}_do", group_sums=(o, hh))
    dw_o = _mm(o, dm, "tn", 128, 1024, f"{tag}_dwo", out_dtypes=(BF16,), shard="rows")
    if own_ride:
        rider = _ExchangeRide(list(rider.operands) + [dw_o])
    dq, dk, dv, rode = _attn_bwd(q, k, v, do, lse.reshape(hh, 1, s), delta, f"{tag}_dattn", rider)
    if own_ride:
        own["mla_w_o"] = rode[-1]
    dqe, dkve, dproj, dqn, dkvn = _mla_bwd_mid(dq, dk, dv, cos, sin, proj, qn, kvn, w_uq, w_ukv, f"{tag}_dqkv")
    dw_uq = _mm(cqn, dqe, "tn", MLA_Q_LORA, 768, f"{tag}_dwuq", out_dtypes=(BF16,))
    dw_uq = dw_uq.reshape(MLA_Q_LORA, N_DEV, -1).transpose(1, 0, 2)
    dw_ukv = _mm(ckvn, dkve, "tn", MLA_KV_LORA, 256, f"{tag}_dwukv", out_dtypes=(BF16,), shard="cols")
    dw_in = _mm(a, dproj, "tn", 128, MLA_IN, f"{tag}_dwin", out_dtypes=(BF16,), shard="rows",
                rider=_ExchangeRide([dw_uq, dw_ukv]) if own_ride else None)
    if own_ride:
        dw_in, (own["mla_w_uq"], own["mla_w_ukv"]) = dw_in
    dh_in, dg_pre, brought = _mm_rmsbwd_res(dproj, w_in, h, g_pre, dh, 512, f"{tag}_din",
                                             _ExchangeRide([dw_in]) if own_ride else None)
    if own_ride:
        own["mla_w_in"] = brought[0]
    return dh_in, dg_pre, dg_post, dqn, dkvn, dw_in, dw_uq, dw_ukv, dw_o, rode, own


_CUT = dict(mla_w_in="rows", mla_w_uq="cols", mla_w_ukv="cols", mla_w_o="rows", hgrn_w_in="cols", hgrn_w_o="rows",
            mlp_w1="cols", mlp_w2="rows")


def _unit(layer, kind):
    slot = layer // 2
    if kind == "mla":
        return [("mla_w_in", slot), ("mla_w_uq", slot), ("mla_w_ukv", slot), ("mla_w_o", slot)]
    if kind == "hgrn":
        return [("hgrn_w_in", slot), ("hgrn_w_o", slot)]
    return [("mlp_w1", layer), ("mlp_w2", layer)]


_GATHER_FIRST = [("mla_w_in", 0)]
_GATHER_PLAN = {
    (0, "in"): [("mla_w_uq", 0), ("mla_w_ukv", 0)],
    (0, "attn"): [("mla_w_o", 0)] + _unit(0, "mlp") + _unit(1, "hgrn"),
    (0, "up"): [("mlp_w1", 1)],
    (0, "down"): [("mlp_w2", 1)],
    (1, "in"): _unit(2, "mla"),
    (1, "scan"): _unit(2, "mlp"),
    (2, "attn"): _unit(3, "hgrn") + _unit(3, "mlp"),
}
_EXCHANGE_PLAN = {
    (3, "dscan"): _unit(3, "mlp"),
    (2, "ddown"): [("hgrn_w_in", 1)],
    (2, "dw2"): [("hgrn_w_o", 1)],
    (2, "dattn"): _unit(2, "mlp"),
    (1, "ddown"): _unit(2, "mla"),
    (1, "dscan"): _unit(1, "mlp"),
    (0, "ddown"): [("hgrn_w_in", 0)],
    (0, "dw2"): [("hgrn_w_o", 0)],
    (0, "dattn"): _unit(0, "mlp"),
}


def _gather_cut(name):
    return "slots" if name == "mla_w_uq" else _CUT[name]


def _gather_rider(weights, ents):
    return _GatherRide([weights[name][idx].astype(BF16) for name, idx in ents], [_gather_cut(name) for name, _ in ents])


def _gathered(outs, ents):
    res = {}
    for (name, idx), out in zip(ents, outs):
        if _gather_cut(name) == "slots":
            out = out.transpose(1, 0, 2).reshape(out.shape[1], -1)
        res[(name, idx)] = out
    return res


def _adamw_nd(w, g, m, v, name):
    shape = w.shape
    c = shape[-1]
    d, nm, nv = _adamw(w.reshape(-1, c), g.reshape(-1, c), m.reshape(-1, c), v.reshape(-1, c), name)
    return d.reshape(shape), nm.reshape(shape), nv.reshape(shape)


def kernel(x, positions, norm_gains, mla_w_in, mla_q_norm, mla_kv_norm, mla_w_uq, mla_w_ukv, mla_w_o, hgrn_w_in, hgrn_lb_logits, hgrn_o_norm, hgrn_w_o, mlp_w1, mlp_w2, loss_target, m_norm_gains, m_mla_w_in, m_mla_q_norm, m_mla_kv_norm, m_mla_w_uq, m_mla_w_ukv, m_mla_w_o, m_hgrn_w_in, m_hgrn_lb_logits, m_hgrn_o_norm, m_hgrn_w_o, m_mlp_w1, m_mlp_w2, v_norm_gains, v_mla_w_in, v_mla_q_norm, v_mla_kv_norm, v_mla_w_uq, v_mla_w_ukv, v_mla_w_o, v_hgrn_w_in, v_hgrn_lb_logits, v_hgrn_o_norm, v_hgrn_w_o, v_mlp_w1, v_mlp_w2):
    weights = dict(norm_gains=norm_gains, mla_w_in=mla_w_in, mla_q_norm=mla_q_norm, mla_kv_norm=mla_kv_norm,
                   mla_w_uq=mla_w_uq, mla_w_ukv=mla_w_ukv, mla_w_o=mla_w_o, hgrn_w_in=hgrn_w_in,
                   hgrn_lb_logits=hgrn_lb_logits, hgrn_o_norm=hgrn_o_norm, hgrn_w_o=hgrn_w_o, mlp_w1=mlp_w1, mlp_w2=mlp_w2)
    mom_m = dict(norm_gains=m_norm_gains, mla_w_in=m_mla_w_in, mla_q_norm=m_mla_q_norm, mla_kv_norm=m_mla_kv_norm,
                 mla_w_uq=m_mla_w_uq, mla_w_ukv=m_mla_w_ukv, mla_w_o=m_mla_w_o, hgrn_w_in=m_hgrn_w_in,
                 hgrn_lb_logits=m_hgrn_lb_logits, hgrn_o_norm=m_hgrn_o_norm, hgrn_w_o=m_hgrn_w_o, mlp_w1=m_mlp_w1, mlp_w2=m_mlp_w2)
    mom_v = dict(norm_gains=v_norm_gains, mla_w_in=v_mla_w_in, mla_q_norm=v_mla_q_norm, mla_kv_norm=v_mla_kv_norm,
                 mla_w_uq=v_mla_w_uq, mla_w_ukv=v_mla_w_ukv, mla_w_o=v_mla_w_o, hgrn_w_in=v_hgrn_w_in,
                 hgrn_lb_logits=v_hgrn_lb_logits, hgrn_o_norm=v_hgrn_o_norm, hgrn_w_o=v_hgrn_w_o, mlp_w1=v_mlp_w1, mlp_w2=v_mlp_w2)
    order = list(weights)
    seq = x.shape[1]
    h = x.reshape(seq, D_MODEL)
    target = loss_target.reshape(seq, D_MODEL)

    first = _gather_rider(weights, _GATHER_FIRST)
    first = _GatherRide(first.operands + [norm_gains.reshape(DEPTH * 4, D_MODEL // N_DEV)], first.cuts + ["slots"])
    *first_w, gains = _run_alone(first, "gather_first")
    full = _gathered(first_w, _GATHER_FIRST)
    gains = gains.transpose(1, 0, 2).reshape(DEPTH, 4, 1, D_MODEL)

    def gather_riders(layer, keys):
        return {k: _gather_rider(weights, _GATHER_PLAN[(layer, k)]) for k in keys if (layer, k) in _GATHER_PLAN}

    seen = set()

    def arrived(layer, rode):
        for k, outs in rode.items():
            if (layer, k) not in seen:
                seen.add((layer, k))
                full.update(_gathered(outs, _GATHER_PLAN[(layer, k)]))

    def late_weights(layer):
        def late(rode):
            arrived(layer, rode)
            return tuple(full.get((name, layer // 2)) for name in ("mla_w_uq", "mla_w_ukv", "mla_w_o"))
        return late

    cos, sin = _rope_tables(positions.reshape(seq, 1), "rope_tables")
    lower = _lb_fwd(hgrn_lb_logits, "lower_bounds")

    def mixer_args(layer):
        slot = layer // 2
        if layer % 2 == 0:
            return (cos, sin, full[("mla_w_in", slot)], mla_q_norm[slot:slot + 1], mla_kv_norm[slot:slot + 1],
                    full[("mla_w_uq", slot)], full[("mla_w_ukv", slot)], full[("mla_w_o", slot)])
        return (lower[layer:layer + 1], hgrn_o_norm[slot:slot + 1], full[("hgrn_w_in", slot)], full[("hgrn_w_o", slot)])

    saved = []
    for layer in range(DEPTH):
        g = gains[layer]
        if layer % 2 == 0:
            slot = layer // 2
            h, sv_mix, rode = _mla_layer_fwd(
                h, g[0], g[1], cos, sin, full[("mla_w_in", slot)], late_weights(layer), mla_q_norm[slot:slot + 1],
                mla_kv_norm[slot:slot + 1], f"l{layer}_mla", gather_riders(layer, ["in", "attn"]))
            arrived(layer, rode)
        else:
            h, sv_mix, rode = _hgrn_layer_fwd(h, g[0], g[1], *mixer_args(layer), f"l{layer}_hgrn",
                                              gather_riders(layer, ["in", "scan"]))
            arrived(layer, rode)
        h, sv_mlp, rode = _mlp_fwd(h, g[2], g[3], full[("mlp_w1", layer)], full[("mlp_w2", layer)], f"l{layer}_mlp",
                                   gather_riders(layer, ["up", "down"]))
        arrived(layer, rode)
        saved.append((sv_mix, sv_mlp))

    loss_part, dh = _loss(h, target, "loss")
    loss = lax.psum(loss_part[0, 0], AXES)

    zero_row = jnp.zeros((1, D_MODEL), F32)
    dgains = [[None] * 4 for _ in range(DEPTH)]
    dlower = [zero_row] * DEPTH
    partials, lands = {}, {}
    dqn, dkvn, donorm = [None] * 2, [None] * 2, [None] * 2

    def exchange_riders(layer, keys):
        return {k: _ExchangeRide([partials[e] for e in _EXCHANGE_PLAN[(layer, k)]]) for k in keys
                if (layer, k) in _EXCHANGE_PLAN}

    def landed(layer, rode):
        for k, outs in rode.items():
            lands.update(zip(_EXCHANGE_PLAN[(layer, k)], outs))

    for layer in range(DEPTH - 1, -1, -1):
        slot = layer // 2
        g = gains[layer]
        sv_mix, sv_mlp = saved[layer]
        dh, dgains[layer][2], dgains[layer][3], partials[("mlp_w1", layer)], partials[("mlp_w2", layer)], rode = _mlp_bwd(
            dh, sv_mlp, g[2], g[3], full[("mlp_w1", layer)], full[("mlp_w2", layer)], f"l{layer}_mlp",
            exchange_riders(layer, ["ddown", "dw2"]))
        landed(layer, rode)
        key = "dattn" if layer % 2 == 0 else "dscan"
        rider = exchange_riders(layer, [key]).get(key)
        if layer % 2 == 0:
            (dh, dgains[layer][0], dgains[layer][1], dqn[slot], dkvn[slot], partials[("mla_w_in", slot)],
             partials[("mla_w_uq", slot)], partials[("mla_w_ukv", slot)], partials[("mla_w_o", slot)], brought,
             own) = _mla_layer_bwd(dh, sv_mix, g[0], g[1], *mixer_args(layer), f"l{layer}_mla", rider, own_ride=(layer == 0))
            lands.update({(name, slot): land for name, land in own.items()})
        else:
            (dh, dgains[layer][0], dgains[layer][1], dlower[layer], donorm[slot], partials[("hgrn_w_in", slot)],
             partials[("hgrn_w_o", slot)], brought) = _hgrn_layer_bwd(dh, sv_mix, g[0], g[1], *mixer_args(layer), f"l{layer}_hgrn", rider)
        landed(layer, {key: brought} if rider is not None else {})
    grad_x = dh.reshape(x.shape)
    dlogits = _lb_bwd(hgrn_lb_logits, jnp.concatenate(dlower, axis=0), "lower_bounds_bwd")

    pad = jnp.zeros((1, D_MODEL - 2 * MLA_KV_LORA), F32)
    pad2 = jnp.zeros((1, D_MODEL - 2 * HGRN_D), F32)
    small = jnp.concatenate(
        [jnp.concatenate([gg for row in dgains for gg in row], axis=0), jnp.concatenate(dqn, axis=1),
         jnp.concatenate(dkvn + [pad], axis=1), dlogits, jnp.concatenate(donorm + [pad2], axis=1), zero_row], axis=0)
    small = _sum_slots(_all_gather(small, "gather_small_grads"), "sum_small_grads")
    me = 4 * lax.axis_index("x") + 2 * lax.axis_index("y") + lax.axis_index("c")
    n_g = DEPTH * 4
    width = D_MODEL // N_DEV
    grads = {}
    grads["norm_gains"] = lax.dynamic_slice(small[:n_g], (0, me * width), (n_g, width)).reshape(DEPTH, 4, width)
    grads["mla_q_norm"] = small[n_g].reshape(2, MLA_Q_LORA)
    grads["mla_kv_norm"] = small[n_g + 1, :2 * MLA_KV_LORA].reshape(2, MLA_KV_LORA)
    grads["hgrn_lb_logits"] = small[n_g + 2:n_g + 2 + DEPTH]
    grads["hgrn_o_norm"] = small[n_g + 2 + DEPTH, :2 * HGRN_D].reshape(2, HGRN_D)

    deltas, new_m, new_v = {}, {}, {}
    for name in order:
        if name in _CUT:
            per_layer = [lands[(name, idx)] for idx in range(weights[name].shape[0])]
            grads[name], deltas[name], new_m[name], new_v[name], _ = _adamw_layers(
                weights[name], per_layer, mom_m[name], mom_v[name], f"adamw_{name}")
        else:
            deltas[name], new_m[name], new_v[name] = _adamw_nd(weights[name], grads[name], mom_m[name], mom_v[name], f"adamw_{name}")
    return (loss, grad_x, *[grads[n] for n in order], *[deltas[n] for n in order], *[new_m[n] for n in order],
            *[new_v[n] for n in order])
```
